```python
import math
import jax, jax.numpy as jnp
from jax import lax
import numpy as np

D_MODEL = 1024
BATCH = 16
SEQ = 256
DEPTH = 2
DEC_BATCH = 8
DEC_SEQ = 1024
PAST_LEN = 256

GRID_W = 64
N_EVEN = (DEPTH + 1) // 2
N_ODD = DEPTH // 2
NA_HEADS = 8
NA_HEAD_DIM = 64
NA_WIDTH = NA_HEADS * NA_HEAD_DIM
NA_ROWS_MAX = 8
NA_COLS = 16
RPB_ROWS = 2 * NA_ROWS_MAX - 1
RPB_COLS = 2 * NA_COLS - 1
LRU_WIDTH = 512
LRU_BLOCKS = 8
LRU_BLOCK = LRU_WIDTH // LRU_BLOCKS
LRU_C = 8.0
CONV_WIDTH = 4
CONV_PAD_L = 2
CONV_PAD_R = 1
DIFF_HEADS = 8
DIFF_HEAD_DIM = 64
DIFF_WIDTH = DIFF_HEADS * 2 * DIFF_HEAD_DIM
EVEN_IN = 4 * NA_WIDTH + 2 * LRU_WIDTH
EVEN_OUT = NA_WIDTH + LRU_WIDTH
ODD_IN = 4 * DIFF_WIDTH
ROPE_BASE = 10000.0
Q_BLOCK = 128
EPS = 1e-6
NEG_INF = -1e30

kernel_name = 'hybrid_natten_rglru_diffattn_prefix_step'


def rmsnorm(x, g):
    xf = x.astype(jnp.float32)
    y = xf * lax.rsqrt(jnp.mean(xf * xf, axis=-1, keepdims=True) + EPS)
    return y.astype(x.dtype) * g


def adaln_input(x, cvec, norm_g, ada_w, ada_b):
    m = (jax.nn.silu(cvec) @ ada_w + ada_b)[..., None, :]
    shift, scale, gate = jnp.split(m, 3, axis=-1)
    return rmsnorm(x, norm_g) * (1 + scale) + shift, gate


def _query_blocks(q):
    b, t = q.shape[:2]
    return jnp.moveaxis(q.reshape((b, t // Q_BLOCK, Q_BLOCK) + q.shape[2:]), 1, 0)


def _merge_blocks(o):
    nb, b, qb = o.shape[:3]
    return jnp.moveaxis(o, 0, 1).reshape((b, nb * qb) + o.shape[3:])


def dense_attention(q, k, v):
    scale = q.shape[-1] ** -0.5
    def one(qb):
        s = jnp.einsum('bqhd,bkhd->bhqk', qb, k).astype(jnp.float32) * scale
        p = jax.nn.softmax(s, axis=-1).astype(v.dtype)
        return jnp.einsum('bhqk,bkhd->bqhd', p, v)
    return _merge_blocks(lax.map(one, _query_blocks(q)))


def diff_attention(q, k, v, lam):
    scale = DIFF_HEAD_DIM ** -0.5
    k1, k2 = jnp.split(k, 2, axis=-1)
    def one(qb):
        q1, q2 = jnp.split(qb, 2, axis=-1)
        s1 = jnp.einsum('bqhd,bkhd->bhqk', q1, k1).astype(jnp.float32) * scale
        s2 = jnp.einsum('bqhd,bkhd->bhqk', q2, k2).astype(jnp.float32) * scale
        p = jax.nn.softmax(s1, axis=-1) - lam * jax.nn.softmax(s2, axis=-1)
        return jnp.einsum('bhqk,bkhd->bqhd', p.astype(v.dtype), v)
    return _merge_blocks(lax.map(one, _query_blocks(q)))


def rope_2d(x):
    t_len = x.shape[1]
    half = DIFF_HEAD_DIM // 2
    m = half // 2
    t = jnp.arange(t_len)
    rows = (t // GRID_W).astype(jnp.float32)
    cols = (t % GRID_W).astype(jnp.float32)
    inv = ROPE_BASE ** (-jnp.arange(m, dtype=jnp.float32) / m)
    def rot(xh, pos):
        ang = pos[:, None] * inv[None, :]
        cos = jnp.cos(ang)[None, :, None, None, :].astype(x.dtype)
        sin = jnp.sin(ang)[None, :, None, None, :].astype(x.dtype)
        x1, x2 = jnp.split(xh, 2, axis=-1)
        return jnp.concatenate([x1 * cos - x2 * sin, x2 * cos + x1 * sin], axis=-1)
    return jnp.concatenate([rot(x[..., :half], rows), rot(x[..., half:], cols)], axis=-1)


def neighbourhood_attention(q, k, v, ck, cv, rpb):
    b, t, h, d = q.shape
    n_rows = t // GRID_W
    kr = min(NA_ROWS_MAX, n_rows)
    kc = NA_COLS
    scale = d ** -0.5
    rows = jnp.arange(n_rows)
    cols = jnp.arange(GRID_W)
    row_start = jnp.clip(rows - kr // 2, 0, n_rows - kr)
    row_idx = row_start[:, None] + jnp.arange(kr)[None, :]
    col_start = jnp.clip(cols - kc // 2, 0, GRID_W - kc)
    col_in = (cols[None, :] >= col_start[:, None]) & (cols[None, :] < col_start[:, None] + kc)
    dr = row_idx - rows[:, None] + (NA_ROWS_MAX - 1)
    dc = jnp.clip(cols[None, :] - cols[:, None] + (kc - 1), 0, 2 * kc - 2)
    bias = rpb[:, dr[:, None, :, None], dc[None, :, None, :]]
    qg = q.reshape(b, n_rows, GRID_W, h, d)
    kg = jnp.take(k.reshape(b, n_rows, GRID_W, h, d), row_idx, axis=1)
    vg = jnp.take(v.reshape(b, n_rows, GRID_W, h, d), row_idx, axis=1)
    s_win = jnp.einsum('brqhd,brkwhd->bhrqkw', qg, kg).astype(jnp.float32) * scale + bias[None].astype(jnp.float32)
    s_win = jnp.where(col_in[:, None, :], s_win, NEG_INF)
    s_ctx = jnp.einsum('brqhd,bkhd->bhrqk', qg, ck).astype(jnp.float32) * scale
    n_win = kr * GRID_W
    s = jnp.concatenate([s_win.reshape(b, h, n_rows, GRID_W, n_win), s_ctx], axis=-1)
    p = jax.nn.softmax(s, axis=-1).astype(v.dtype)
    p_win = p[..., :n_win].reshape(b, h, n_rows, GRID_W, kr, GRID_W)
    p_ctx = p[..., n_win:]
    o = jnp.einsum('bhrqkw,brkwhd->brqhd', p_win, vg) + jnp.einsum('bhrqk,bkhd->brqhd', p_ctx, cv)
    return o.reshape(b, t, h, d)


def dwconv(x, w, bias):
    y = lax.conv_general_dilated(x, w[:, None, :], window_strides=(1,), padding=[(CONV_PAD_L, CONV_PAD_R)],
                                 dimension_numbers=('NWC', 'WIO', 'NWC'), feature_group_count=x.shape[-1])
    return y + bias


def rglru(x, w_a, b_a, w_x, b_x, lam, h0, reverse):
    b, t, wd = x.shape
    xb = x.reshape(b, t, LRU_BLOCKS, LRU_BLOCK)
    r = jax.nn.sigmoid(jnp.einsum('btnc,ncd->btnd', xb, w_a).reshape(b, t, wd) + b_a)
    i = jax.nn.sigmoid(jnp.einsum('btnc,ncd->btnd', xb, w_x).reshape(b, t, wd) + b_x)
    log_a = -LRU_C * r * jax.nn.softplus(-lam)
    a = jnp.exp(log_a)
    u = jnp.sqrt(-jnp.expm1(2 * log_a)) * (i * x)
    def combine(e1, e2):
        a1, b1 = e1
        a2, b2 = e2
        return a1 * a2, a2 * b1 + b2
    a_acc, u_acc = lax.associative_scan(combine, (a, u), reverse=reverse, axis=1)
    hs = u_acc + a_acc * h0[:, None, :]
    final = hs[:, 0] if reverse else hs[:, -1]
    return hs, final


def lru_branch(xb, conv_w, conv_b, wa, ba, wx, bx, lam, h0f, h0b):
    xc = dwconv(xb, conv_w, conv_b)
    hf, ff = rglru(xc, wa[0], ba[0], wx[0], bx[0], lam[0], h0f, False)
    hb, fb = rglru(xc, wa[1], ba[1], wx[1], bx[1], lam[1], h0b, True)
    return hf + hb, ff, fb


def even_split(h, w_in):
    return jnp.split(h @ w_in, [NA_WIDTH, 2 * NA_WIDTH, 3 * NA_WIDTH, 4 * NA_WIDTH, 4 * NA_WIDTH + LRU_WIDTH], axis=-1)


def even_layer_ctx(x, cvec, norm_g, ada_w, ada_b, w_in, conv_w, conv_b, wa, ba, wx, bx, lam, w_out):
    b, t, _ = x.shape
    h, gate = adaln_input(x, cvec, norm_g, ada_w, ada_b)
    q, k, v, ga, xb, gb = even_split(h, w_in)
    q = q.reshape(b, t, NA_HEADS, NA_HEAD_DIM)
    k = k.reshape(b, t, NA_HEADS, NA_HEAD_DIM)
    v = v.reshape(b, t, NA_HEADS, NA_HEAD_DIM)
    o_a = dense_attention(q, k, v).reshape(b, t, NA_WIDTH)
    zeros = jnp.zeros((b, LRU_WIDTH), x.dtype)
    o_b, ff, fb = lru_branch(xb, conv_w, conv_b, wa, ba, wx, bx, lam, zeros, zeros)
    o = jnp.concatenate([o_a * jax.nn.silu(ga), o_b * jax.nn.silu(gb)], axis=-1) @ w_out
    return x + gate * o, k, v, jnp.stack([ff, fb], axis=1)


def even_layer_lat(x, cvec, ck, cv, lru_state, rpb, norm_g, ada_w, ada_b, w_in, conv_w, conv_b, wa, ba, wx, bx, lam, w_out):
    b, t, _ = x.shape
    h, gate = adaln_input(x, cvec, norm_g, ada_w, ada_b)
    q, k, v, ga, xb, gb = even_split(h, w_in)
    q = q.reshape(b, t, NA_HEADS, NA_HEAD_DIM)
    k = k.reshape(b, t, NA_HEADS, NA_HEAD_DIM)
    v = v.reshape(b, t, NA_HEADS, NA_HEAD_DIM)
    o_a = neighbourhood_attention(q, k, v, ck, cv, rpb).reshape(b, t, NA_WIDTH)
    o_b, _, _ = lru_branch(xb, conv_w, conv_b, wa, ba, wx, bx, lam, lru_state[:, 0], lru_state[:, 1])
    o = jnp.concatenate([o_a * jax.nn.silu(ga), o_b * jax.nn.silu(gb)], axis=-1) @ w_out
    return x + gate * o


def diff_lambda(lq1, lk1, lq2, lk2, lam_init):
    return jnp.exp(jnp.sum(lq1 * lk1)) - jnp.exp(jnp.sum(lq2 * lk2)) + lam_init


def diff_output(o, g, sub_g, w_out, lam_init):
    b, t = o.shape[:2]
    o = rmsnorm(o, sub_g) * (1 - lam_init)
    return (o.reshape(b, t, DIFF_WIDTH) * jax.nn.silu(g)) @ w_out


def odd_layer_ctx(x, cvec, lam_init, norm_g, ada_w, ada_b, w_in, lq1, lk1, lq2, lk2, sub_g, w_out):
    b, t, _ = x.shape
    h, gate = adaln_input(x, cvec, norm_g, ada_w, ada_b)
    q, k, v, g = jnp.split(h @ w_in, 4, axis=-1)
    q = q.reshape(b, t, DIFF_HEADS, 2 * DIFF_HEAD_DIM)
    k = k.reshape(b, t, DIFF_HEADS, 2 * DIFF_HEAD_DIM)
    v = v.reshape(b, t, DIFF_HEADS, 2 * DIFF_HEAD_DIM)
    lam = diff_lambda(lq1, lk1, lq2, lk2, lam_init)
    o = diff_attention(q, k, v, lam)
    return x + gate * diff_output(o, g, sub_g, w_out, lam_init), k, v


def odd_layer_lat(x, cvec, ck, cv, lam_init, norm_g, ada_w, ada_b, w_in, lq1, lk1, lq2, lk2, sub_g, w_out):
    b, t, _ = x.shape
    h, gate = adaln_input(x, cvec, norm_g, ada_w, ada_b)
    q, k, v, g = jnp.split(h @ w_in, 4, axis=-1)
    q = rope_2d(q.reshape(b, t, DIFF_HEADS, 2, DIFF_HEAD_DIM)).reshape(b, t, DIFF_HEADS, 2 * DIFF_HEAD_DIM)
    k = rope_2d(k.reshape(b, t, DIFF_HEADS, 2, DIFF_HEAD_DIM)).reshape(b, t, DIFF_HEADS, 2 * DIFF_HEAD_DIM)
    v = v.reshape(b, t, DIFF_HEADS, 2 * DIFF_HEAD_DIM)
    k_all = jnp.concatenate([ck, k], axis=1)
    v_all = jnp.concatenate([cv, v], axis=1)
    lam = diff_lambda(lq1, lk1, lq2, lk2, lam_init)
    o = diff_attention(q, k_all, v_all, lam)
    return x + gate * diff_output(o, g, sub_g, w_out, lam_init)


def setup_inputs(seed: int = 0) -> dict:
    key = jax.random.key(seed)
    kit = iter(jax.random.split(key, 40))
    def nrm(shape, s):
        return jax.random.normal(next(kit), shape, jnp.float32) * s
    def gain(shape):
        return 1.0 + nrm(shape, 0.02)
    a0 = jax.random.uniform(next(kit), (N_EVEN, 2, LRU_WIDTH), jnp.float32, minval=0.9, maxval=0.999)
    a_base = a0 ** (1.0 / LRU_C)
    lru_lam = jnp.log(a_base) - jnp.log1p(-a_base)
    return {
        'x_prompt': nrm((BATCH, SEQ, D_MODEL), 1.0),
        'x_sample': nrm((DEC_BATCH, DEC_SEQ, D_MODEL), 1.0),
        'c': nrm((DEC_BATCH, D_MODEL), 1.0),
        'cache_na_k': nrm((DEC_BATCH, N_EVEN, PAST_LEN, NA_HEADS, NA_HEAD_DIM), 1.0),
        'cache_na_v': nrm((DEC_BATCH, N_EVEN, PAST_LEN, NA_HEADS, NA_HEAD_DIM), 1.0),
        'state_lru': nrm((DEC_BATCH, N_EVEN, 2, LRU_WIDTH), 0.3),
        'cache_diff_k': nrm((DEC_BATCH, N_ODD, PAST_LEN, DIFF_HEADS, 2 * DIFF_HEAD_DIM), 1.0),
        'cache_diff_v': nrm((DEC_BATCH, N_ODD, PAST_LEN, DIFF_HEADS, 2 * DIFF_HEAD_DIM), 1.0),
        'c_ctx': nrm((D_MODEL,), 1.0),
        'e_norm': gain((N_EVEN, D_MODEL)),
        'e_ada_w': nrm((N_EVEN, D_MODEL, 3 * D_MODEL), D_MODEL ** -0.5),
        'e_ada_b': nrm((N_EVEN, 3 * D_MODEL), 0.02),
        'e_w_in': nrm((N_EVEN, D_MODEL, EVEN_IN), D_MODEL ** -0.5),
        'e_rpb': nrm((N_EVEN, NA_HEADS, RPB_ROWS, RPB_COLS), 0.2),
        'e_conv_w': nrm((N_EVEN, CONV_WIDTH, LRU_WIDTH), 0.5),
        'e_conv_b': nrm((N_EVEN, LRU_WIDTH), 0.02),
        'e_lru_wa': nrm((N_EVEN, 2, LRU_BLOCKS, LRU_BLOCK, LRU_BLOCK), LRU_BLOCK ** -0.5),
        'e_lru_ba': nrm((N_EVEN, 2, LRU_WIDTH), 0.02),
        'e_lru_wx': nrm((N_EVEN, 2, LRU_BLOCKS, LRU_BLOCK, LRU_BLOCK), LRU_BLOCK ** -0.5),
        'e_lru_bx': nrm((N_EVEN, 2, LRU_WIDTH), 0.02),
        'e_lru_lam': lru_lam,
        'e_w_out': nrm((N_EVEN, EVEN_OUT, D_MODEL), EVEN_OUT ** -0.5),
        'o_norm': gain((N_ODD, D_MODEL)),
        'o_ada_w': nrm((N_ODD, D_MODEL, 3 * D_MODEL), D_MODEL ** -0.5),
        'o_ada_b': nrm((N_ODD, 3 * D_MODEL), 0.02),
        'o_w_in': nrm((N_ODD, D_MODEL, ODD_IN), D_MODEL ** -0.5),
        'o_lq1': nrm((N_ODD, DIFF_HEAD_DIM), 0.1),
        'o_lk1': nrm((N_ODD, DIFF_HEAD_DIM), 0.1),
        'o_lq2': nrm((N_ODD, DIFF_HEAD_DIM), 0.1),
        'o_lk2': nrm((N_ODD, DIFF_HEAD_DIM), 0.1),
        'o_sub_g': gain((N_ODD, 2 * DIFF_HEAD_DIM)),
        'o_w_out': nrm((N_ODD, DIFF_WIDTH, D_MODEL), DIFF_WIDTH ** -0.5),
        'final_norm': gain((D_MODEL,)),
    }


def reference(x_prompt, x_sample, c, cache_na_k, cache_na_v, state_lru, cache_diff_k, cache_diff_v, c_ctx,
              e_norm, e_ada_w, e_ada_b, e_w_in, e_rpb, e_conv_w, e_conv_b, e_lru_wa, e_lru_ba, e_lru_wx, e_lru_bx,
              e_lru_lam, e_w_out, o_norm, o_ada_w, o_ada_b, o_w_in, o_lq1, o_lk1, o_lq2, o_lk2, o_sub_g, o_w_out,
              final_norm):
    xp = x_prompt
    xs = x_sample
    na_k, na_v, lru_s, df_k, df_v = [], [], [], [], []
    for i in range(DEPTH):
        j = i // 2
        if i % 2 == 0:
            ep = (e_norm[j], e_ada_w[j], e_ada_b[j], e_w_in[j], e_conv_w[j], e_conv_b[j], e_lru_wa[j], e_lru_ba[j],
                  e_lru_wx[j], e_lru_bx[j], e_lru_lam[j], e_w_out[j])
            xp, k, v, s = even_layer_ctx(xp, c_ctx, *ep)
            xs = even_layer_lat(xs, c, cache_na_k[:, j], cache_na_v[:, j], state_lru[:, j], e_rpb[j], *ep)
            na_k.append(k)
            na_v.append(v)
            lru_s.append(s)
        else:
            lam_init = 0.8 - 0.6 * math.exp(-0.3 * i)
            op = (o_norm[j], o_ada_w[j], o_ada_b[j], o_w_in[j], o_lq1[j], o_lk1[j], o_lq2[j], o_lk2[j], o_sub_g[j], o_w_out[j])
            xp, k, v = odd_layer_ctx(xp, c_ctx, lam_init, *op)
            xs = odd_layer_lat(xs, c, cache_diff_k[:, j], cache_diff_v[:, j], lam_init, *op)
            df_k.append(k)
            df_v.append(v)
    y_prompt = rmsnorm(xp, final_norm)
    y_sample = rmsnorm(xs, final_norm)
    new_na_k = jnp.stack(na_k, axis=1)
    new_na_v = jnp.stack(na_v, axis=1)
    new_lru = jnp.stack(lru_s, axis=1)
    new_diff_k = jnp.stack(df_k, axis=1)
    new_diff_v = jnp.stack(df_v, axis=1)
    return (y_prompt, y_sample, new_na_k, new_na_v, new_lru, new_diff_k, new_diff_v)
```

```python
import functools
import math

import jax
import jax.numpy as jnp
from jax import lax
from jax.experimental import pallas as pl
from jax.experimental.pallas import tpu as pltpu

D_MODEL = 1024
DEPTH = 2
GRID_W = 64
NA_HEADS = 8
NA_HEAD_DIM = 64
NA_WIDTH = NA_HEADS * NA_HEAD_DIM
NA_ROWS_MAX = 8
NA_COLS = 16
RPB_ROWS = 2 * NA_ROWS_MAX - 1
RPB_COLS = 2 * NA_COLS - 1
LRU_WIDTH = 512
LRU_BLOCKS = 8
LRU_BLOCK = LRU_WIDTH // LRU_BLOCKS
LRU_C = 8.0
DIFF_HEADS = 8
DIFF_HEAD_DIM = 64
DIFF_WIDTH = DIFF_HEADS * 2 * DIFF_HEAD_DIM
ROPE_BASE = 10000.0
EPS = 1e-6
NEG_INF = -1e30

LANES = 128
SUBLANES = 8
VMEM_LIMIT = 56 * 1024 * 1024

NA_Q_ROWS = 4
NA_K_ROWS = 12
NA_PAIR_TILES = 16

F32 = jnp.float32
BF16 = jnp.bfloat16


def _params(*semantics):
    return pltpu.CompilerParams(dimension_semantics=semantics, vmem_limit_bytes=VMEM_LIMIT)


def _sigmoid(x):
    return 1.0 / (1.0 + jnp.exp(-x))


def _silu(x):
    return x * _sigmoid(x)


def _dot(a, b):
    return jnp.dot(a, b, preferred_element_type=F32)


def _dot_nt(a, b):
    return lax.dot_general(a, b, (((1,), (1,)), ((), ())), preferred_element_type=F32)


def _mod_kernel(c_ref, w_ref, b_ref, o_ref):
    s = _silu(c_ref[...])
    o_ref[...] = _dot(s.astype(BF16), w_ref[...].astype(BF16)) + b_ref[...]


def _modulation(cc, w, b):
    rows, d = cc.shape
    n = w.shape[1]
    tn = 1024
    return pl.pallas_call(
        _mod_kernel,
        grid=(n // tn,),
        in_specs=[pl.BlockSpec((rows, d), lambda j: (0, 0)),
                  pl.BlockSpec((d, tn), lambda j: (0, j)),
                  pl.BlockSpec((1, tn), lambda j: (0, j))],
        out_specs=pl.BlockSpec((rows, tn), lambda j: (0, j)),
        out_shape=jax.ShapeDtypeStruct((rows, n), F32),
        compiler_params=_params("arbitrary"),
        name="modulation",
    )(cc, w, b.reshape(1, n))


def _rope_block(x, cos, sin_signed, low16):
    partner = jnp.where(low16, pltpu.roll(x, LANES - 16, axis=1), pltpu.roll(x, 16, axis=1))
    return x * cos + partner * sin_signed


def _inproj_kernel(*refs, outs, rope):
    if rope:
        x_ref, mod_ref, g_ref, w_ref, cos_ref, sin_ref = refs[:6]
        out_refs = refs[6:]
    else:
        x_ref, mod_ref, g_ref, w_ref = refs[:4]
        out_refs = refs[4:]
    d = x_ref.shape[-1]
    x = x_ref[0]
    y = x * lax.rsqrt(jnp.mean(x * x, axis=-1, keepdims=True) + EPS) * g_ref[...]
    mod = mod_ref[0]
    h = (y * (1.0 + mod[:, d:2 * d]) + mod[:, :d]).astype(BF16)
    if rope:
        cos = cos_ref[...]
        sin = sin_ref[...]
        lane = lax.broadcasted_iota(jnp.int32, (1, LANES), 1)
        low16 = (lane & 16) == 0
    off = 0
    for o_ref, (width, roped) in zip(out_refs, outs):
        acc = _dot(h, w_ref[:, off:off + width])
        if roped:
            acc = jnp.concatenate(
                [_rope_block(acc[:, j:j + LANES], cos, sin, low16) for j in range(0, width, LANES)], axis=1)
        o_ref[0] = acc.astype(o_ref.dtype)
        off += width


def _inproj(x, mod, norm_g, w_bf16, outs, rope_tables=None, tm=256):
    b, t, d = x.shape
    n = w_bf16.shape[1]
    per_batch = mod.shape[0] > 1
    mod_map = (lambda i, j: (i, 0, 0)) if per_batch else (lambda i, j: (0, 0, 0))
    in_specs = [pl.BlockSpec((1, tm, d), lambda i, j: (i, j, 0)),
                pl.BlockSpec((1, 1, 3 * d), mod_map),
                pl.BlockSpec((1, d), lambda i, j: (0, 0)),
                pl.BlockSpec((d, n), lambda i, j: (0, 0))]
    args = [x, mod, norm_g.reshape(1, d), w_bf16]
    rope = rope_tables is not None
    if rope:
        in_specs += [pl.BlockSpec((tm, LANES), lambda i, j: (j, 0))] * 2
        args += list(rope_tables)
    kern = functools.partial(_inproj_kernel, outs=tuple((w, r) for w, _, r in outs), rope=rope)
    return pl.pallas_call(
        kern,
        grid=(b, t // tm),
        in_specs=in_specs,
        out_specs=[pl.BlockSpec((1, tm, w), lambda i, j: (i, j, 0)) for w, _, _ in outs],
        out_shape=[jax.ShapeDtypeStruct((b, t, w), dt) for w, dt, _ in outs],
        compiler_params=_params("parallel", "parallel"),
        name="inproj",
    )(*args)


def _outproj_kernel(*refs, n_in, final):
    o_refs = refs[:n_in]
    w_refs = refs[n_in:2 * n_in]
    x_ref, mod_ref = refs[2 * n_in:2 * n_in + 2]
    fin_ref = refs[2 * n_in + 2] if final else None
    out_ref = refs[-1]
    d = x_ref.shape[-1]
    acc = _dot(o_refs[0][0], w_refs[0][...])
    for o_ref, w_ref in zip(o_refs[1:], w_refs[1:]):
        acc = acc + _dot(o_ref[0], w_ref[...])
    y = x_ref[0] + mod_ref[0][:, 2 * d:] * acc
    if final:
        y = y * lax.rsqrt(jnp.mean(y * y, axis=-1, keepdims=True) + EPS) * fin_ref[...]
    out_ref[0] = y


def _outproj(o_list, w_list, x, mod, final_g=None, tm=256):
    b, t, d = x.shape
    per_batch = mod.shape[0] > 1
    mod_map = (lambda i, j: (i, 0, 0)) if per_batch else (lambda i, j: (0, 0, 0))
    in_specs = [pl.BlockSpec((1, tm, o.shape[-1]), lambda i, j: (i, j, 0)) for o in o_list]
    in_specs += [pl.BlockSpec(w.shape, lambda i, j: (0, 0)) for w in w_list]
    in_specs += [pl.BlockSpec((1, tm, d), lambda i, j: (i, j, 0)),
                 pl.BlockSpec((1, 1, 3 * d), mod_map)]
    args = list(o_list) + list(w_list) + [x, mod]
    final = final_g is not None
    if final:
        in_specs.append(pl.BlockSpec((1, d), lambda i, j: (0, 0)))
        args.append(final_g.reshape(1, d))
    kern = functools.partial(_outproj_kernel, n_in=len(o_list), final=final)
    return pl.pallas_call(
        kern,
        grid=(b, t // tm),
        in_specs=in_specs,
        out_specs=pl.BlockSpec((1, tm, d), lambda i, j: (i, j, 0)),
        out_shape=jax.ShapeDtypeStruct((b, t, d), F32),
        compiler_params=_params("parallel", "parallel"),
        name="outproj",
    )(*args)


def _softmax_parts(scores):
    m = scores[0].max(axis=-1, keepdims=True)
    for s in scores[1:]:
        m = jnp.maximum(m, s.max(axis=-1, keepdims=True))
    es = [jnp.exp(s - m) for s in scores]
    l = es[0].sum(axis=-1, keepdims=True)
    for e in es[1:]:
        l = l + e.sum(axis=-1, keepdims=True)
    return es, l


def _lane_low_half():
    return lax.broadcasted_iota(jnp.int32, (1, LANES), 1) < (LANES // 2)


def _ctx_attn_kernel(q_ref, k_ref, v_ref, g_ref, o_ref):
    scale = NA_HEAD_DIM ** -0.5
    low = _lane_low_half()
    for j in range(0, NA_WIDTH, LANES):
        q = q_ref[0, :, j:j + LANES]
        k = k_ref[0, :, j:j + LANES].astype(BF16)
        v = v_ref[0, :, j:j + LANES].astype(BF16)
        halves = []
        for sel in (low, jnp.logical_not(low)):
            qh = jnp.where(sel, q, jnp.zeros_like(q))
            (e,), l = _softmax_parts([_dot_nt(qh, k) * scale])
            halves.append(_dot(e.astype(BF16), v) / l)
        o = jnp.where(low, halves[0], halves[1])
        o_ref[0, :, j:j + LANES] = (o * _silu(g_ref[0, :, j:j + LANES].astype(F32))).astype(o_ref.dtype)


def _ctx_attn(q, k, v, g):
    b, t, w = q.shape
    spec = pl.BlockSpec((1, t, w), lambda i: (i, 0, 0))
    return pl.pallas_call(
        _ctx_attn_kernel,
        grid=(b,),
        in_specs=[spec] * 4,
        out_specs=spec,
        out_shape=jax.ShapeDtypeStruct((b, t, w), BF16),
        compiler_params=_params("parallel"),
        name="ctx_attn",
    )(q, k, v, g)


def _na_bias_kernel(rpb_ref, o_ref):
    h = pl.program_id(0)
    qcol = lax.broadcasted_iota(jnp.int32, (GRID_W, LANES), 0)
    lane = lax.broadcasted_iota(jnp.int32, (GRID_W, LANES), 1)
    right = lane >= GRID_W
    dc = (lane & (GRID_W - 1)) - qcol + (NA_COLS - 1)
    base = h * (RPB_ROWS * RPB_COLS)
    for p in range(NA_PAIR_TILES):
        acc = jnp.zeros((GRID_W, LANES), F32)
        for c in range(RPB_COLS):
            left_v = rpb_ref[base + (p - 1) * RPB_COLS + c] if p >= 1 else 0.0
            right_v = rpb_ref[base + p * RPB_COLS + c] if p < RPB_ROWS else 0.0
            acc = jnp.where(dc == c, jnp.where(right, right_v, left_v), acc)
        o_ref[0, p] = acc


def _na_bias_tiles(rpb):
    h = rpb.shape[0]
    return pl.pallas_call(
        _na_bias_kernel,
        grid=(h,),
        in_specs=[pl.BlockSpec(memory_space=pltpu.SMEM)],
        out_specs=pl.BlockSpec((1, NA_PAIR_TILES, GRID_W, LANES), lambda i: (i, 0, 0, 0)),
        out_shape=jax.ShapeDtypeStruct((h, NA_PAIR_TILES, GRID_W, LANES), F32),
        compiler_params=_params("arbitrary"),
        name="na_bias_tiles",
    )(rpb.reshape(-1))


def _na_kernel(q_ref, k_ref, v_ref, ck_ref, cv_ref, pt_ref, g_ref, o_ref, *, n_rows):
    scale = NA_HEAD_DIM ** -0.5
    grp = pl.program_id(1)
    kr = min(NA_ROWS_MAX, n_rows)
    krow0 = jnp.clip(grp * NA_Q_ROWS - NA_Q_ROWS, 0, n_rows - NA_K_ROWS)
    kstart = pl.multiple_of(krow0 * GRID_W, GRID_W * NA_Q_ROWS)
    n_win = NA_K_ROWS * GRID_W
    kw = k_ref[0, pl.ds(kstart, n_win), :]
    vw = v_ref[0, pl.ds(kstart, n_win), :]
    ck = ck_ref[0].astype(BF16)
    cv = cv_ref[0].astype(BF16)
    q = q_ref[0]

    low = _lane_low_half()
    qcol = lax.broadcasted_iota(jnp.int32, (GRID_W, LANES), 0)
    lane = lax.broadcasted_iota(jnp.int32, (GRID_W, LANES), 1)
    right = lane >= GRID_W
    kcol = lane & (GRID_W - 1)
    col_start = jnp.clip(qcol - NA_COLS // 2, 0, GRID_W - NA_COLS)
    col_ok = (kcol >= col_start) & (kcol < col_start + NA_COLS)

    halves = []
    for half, sel in enumerate((low, jnp.logical_not(low))):
        qh = jnp.where(sel, q, jnp.zeros_like(q))
        s_win = _dot_nt(qh, kw) * scale
        s_ctx = _dot_nt(qh, ck) * scale
        es_rows, l_rows = [], []
        for i in range(NA_Q_ROWS):
            r = grp * NA_Q_ROWS + i
            row_start = jnp.clip(r - kr // 2, 0, n_rows - kr)
            pieces = []
            for m in range(NA_K_ROWS // 2):
                key_row = krow0 + 2 * m
                dr = key_row - r + (NA_ROWS_MAX - 1)
                tile = pt_ref[half, jnp.clip(dr + 1, 0, NA_PAIR_TILES - 1)]
                ok_l = ((key_row >= row_start) & (key_row < row_start + kr)).astype(jnp.int32)
                ok_r = ((key_row + 1 >= row_start) & (key_row + 1 < row_start + kr)).astype(jnp.int32)
                ok = col_ok & (jnp.where(right, ok_r, ok_l) > 0)
                blk = s_win[i * GRID_W:(i + 1) * GRID_W, m * LANES:(m + 1) * LANES]
                pieces.append(jnp.where(ok, blk + tile, NEG_INF))
            sw = jnp.concatenate(pieces, axis=1)
            sc = s_ctx[i * GRID_W:(i + 1) * GRID_W, :]
            (ew, ec), l = _softmax_parts([sw, sc])
            es_rows.append((ew, ec))
            l_rows.append(l)
        ew = jnp.concatenate([e[0] for e in es_rows], axis=0).astype(BF16)
        ec = jnp.concatenate([e[1] for e in es_rows], axis=0).astype(BF16)
        l = jnp.concatenate(l_rows, axis=0)
        halves.append((_dot(ew, vw) + _dot(ec, cv)) / l)
    o = jnp.where(low, halves[0], halves[1])
    o_ref[0] = (o * _silu(g_ref[0].astype(F32))).astype(o_ref.dtype)


def _na_attn(q, k, v, ck, cv, pair_tiles, g):
    b, t, w = q.shape
    c = ck.shape[1]
    n_rows = t // GRID_W
    tq = NA_Q_ROWS * GRID_W
    qspec = pl.BlockSpec((1, tq, LANES), lambda hp, gi, bi: (bi, gi, hp))
    kspec = pl.BlockSpec((1, t, LANES), lambda hp, gi, bi: (bi, 0, hp))
    cspec = pl.BlockSpec((1, c, LANES), lambda hp, gi, bi: (bi, 0, hp))
    ptspec = pl.BlockSpec((2, NA_PAIR_TILES, GRID_W, LANES), lambda hp, gi, bi: (hp, 0, 0, 0))
    return pl.pallas_call(
        functools.partial(_na_kernel, n_rows=n_rows),
        grid=(w // LANES, n_rows // NA_Q_ROWS, b),
        in_specs=[qspec, kspec, kspec, cspec, cspec, ptspec, qspec],
        out_specs=qspec,
        out_shape=jax.ShapeDtypeStruct((b, t, w), BF16),
        compiler_params=_params("parallel", "parallel", "parallel"),
        name="na_attn",
    )(q, k, v, ck, cv, pair_tiles, g)


def _softplus(x):
    return jnp.maximum(x, 0.0) + jnp.log(1.0 + jnp.exp(-jnp.abs(x)))


def _lru_kernel(x_ref, gb_ref, h0f_ref, h0b_ref, cw_ref, cb_ref, wg_ref, bg_ref, lam_ref,
                o_ref, ff_ref, fb_ref, af_s, uf_s, ab_s, ub_s, *, pitch):
    nb, t, _ = x_ref.shape
    cw = cw_ref[...]
    cb = cb_ref[...]
    wg = wg_ref[0]
    bg = bg_ref[0]
    sp = _softplus(-lam_ref[...])
    trow = lax.broadcasted_iota(jnp.int32, (t, LANES), 0)

    def gates(b, carry):
        x = x_ref[b]
        xm2 = jnp.where(trow >= 2, pltpu.roll(x, 2, axis=0), 0.0)
        xm1 = jnp.where(trow >= 1, pltpu.roll(x, 1, axis=0), 0.0)
        xp1 = jnp.where(trow < t - 1, pltpu.roll(x, t - 1, axis=0), 0.0)
        xc = cw[0:1] * xm2 + cw[1:2] * xm1 + cw[2:3] * x + cw[3:4] * xp1 + cb
        gt = _dot(xc.astype(BF16), wg) + bg
        row0 = pl.multiple_of(b * pitch, SUBLANES)
        for d, (a_s, u_s) in enumerate(((af_s, uf_s), (ab_s, ub_s))):
            r = _sigmoid(gt[:, (2 * d) * LANES:(2 * d + 1) * LANES])
            i = _sigmoid(gt[:, (2 * d + 1) * LANES:(2 * d + 2) * LANES])
            log_a = -LRU_C * r * sp[d:d + 1]
            a = jnp.exp(log_a)
            a_s[pl.ds(row0, t), :] = a
            u_s[pl.ds(row0, t), :] = jnp.sqrt(1.0 - a * a) * (i * xc)
        return carry

    lax.fori_loop(0, nb, gates, 0)

    def step(s, carry):
        hf, hb = carry
        fwd = pl.ds(s, nb, stride=pitch)
        hf = af_s[fwd, :] * hf + uf_s[fwd, :]
        uf_s[fwd, :] = hf
        bwd = pl.ds(t - 1 - s, nb, stride=pitch)
        hb = ab_s[bwd, :] * hb + ub_s[bwd, :]
        ub_s[bwd, :] = hb
        return hf, hb

    hf, hb = lax.fori_loop(0, t, step, (h0f_ref[...], h0b_ref[...]), unroll=8)
    ff_ref[...] = hf
    fb_ref[...] = hb

    def emit(b, carry):
        row0 = pl.multiple_of(b * pitch, SUBLANES)
        hs = uf_s[pl.ds(row0, t), :] + ub_s[pl.ds(row0, t), :]
        o_ref[b] = (hs * _silu(gb_ref[b].astype(F32))).astype(o_ref.dtype)
        return carry

    lax.fori_loop(0, nb, emit, 0)


def _lru_branch(xb, gb, h0f, h0b, conv_w, conv_b, wg, bg, lam):
    b, t, w = xb.shape
    nb = SUBLANES
    pitch = t + SUBLANES
    xspec = pl.BlockSpec((nb, t, LANES), lambda i, c: (i, 0, c))
    hspec = pl.BlockSpec((nb, LANES), lambda i, c: (i, c))
    scratch = [pltpu.VMEM((nb * pitch, LANES), F32) for _ in range(4)]
    return pl.pallas_call(
        functools.partial(_lru_kernel, pitch=pitch),
        grid=(b // nb, w // LANES),
        in_specs=[xspec, xspec, hspec, hspec,
                  pl.BlockSpec((conv_w.shape[0], LANES), lambda i, c: (0, c)),
                  pl.BlockSpec((1, LANES), lambda i, c: (0, c)),
                  pl.BlockSpec((1, LANES, 4 * LANES), lambda i, c: (c, 0, 0)),
                  pl.BlockSpec((1, 1, 4 * LANES), lambda i, c: (c, 0, 0)),
                  pl.BlockSpec((2, LANES), lambda i, c: (0, c))],
        out_specs=[xspec, hspec, hspec],
        out_shape=[jax.ShapeDtypeStruct((b, t, w), BF16),
                   jax.ShapeDtypeStruct((b, w), F32),
                   jax.ShapeDtypeStruct((b, w), F32)],
        scratch_shapes=scratch,
        compiler_params=_params("parallel", "parallel"),
        name="lru_branch",
    )(xb, gb, h0f, h0b, conv_w, conv_b.reshape(1, w), wg, bg, lam)


def _lru_gate_weights(wa, ba, wx, bx):
    def tiles(w):
        w2 = w.reshape(LRU_BLOCKS // 2, 2, LRU_BLOCK, LRU_BLOCK)
        z = jnp.zeros_like(w2[:, 0])
        return jnp.concatenate([jnp.concatenate([w2[:, 0], z], axis=2),
                                jnp.concatenate([z, w2[:, 1]], axis=2)], axis=1)
    wg = jnp.concatenate([tiles(wa[0]), tiles(wx[0]), tiles(wa[1]), tiles(wx[1])], axis=2).astype(BF16)
    n_tiles = LRU_WIDTH // LANES
    bg = jnp.stack([ba[0], bx[0], ba[1], bx[1]], axis=0).reshape(4, n_tiles, LANES)
    bg = jnp.transpose(bg, (1, 0, 2)).reshape(n_tiles, 1, 4 * LANES)
    return wg, bg


def _diff_kernel(*refs, n_kv, lam_init):
    q_ref = refs[0]
    kv_refs = refs[1:1 + 2 * n_kv]
    g_ref, lamp_ref, subg_ref, o_ref = refs[1 + 2 * n_kv:]
    scale = DIFF_HEAD_DIM ** -0.5
    low = _lane_low_half()
    q = q_ref[0]
    lp = lamp_ref[...]
    lam = (jnp.exp(jnp.sum(lp[0:1] * lp[1:2], axis=-1, keepdims=True))
           - jnp.exp(jnp.sum(lp[2:3] * lp[3:4], axis=-1, keepdims=True)) + lam_init)
    ks = [kv_refs[2 * i][0].astype(BF16) for i in range(n_kv)]
    vs = [kv_refs[2 * i + 1][0].astype(BF16) for i in range(n_kv)]
    q1 = jnp.where(low, q, jnp.zeros_like(q))
    q2 = jnp.where(low, jnp.zeros_like(q), q)
    e1, l1 = _softmax_parts([_dot_nt(q1, k) * scale for k in ks])
    e2, l2 = _softmax_parts([_dot_nt(q2, k) * scale for k in ks])
    c1 = 1.0 / l1
    c2 = lam / l2
    o = None
    for a, b, v in zip(e1, e2, vs):
        part = _dot((a * c1 - b * c2).astype(BF16), v)
        o = part if o is None else o + part
    o = o * lax.rsqrt(jnp.mean(o * o, axis=-1, keepdims=True) + EPS) * subg_ref[...] * (1.0 - lam_init)
    o_ref[0] = (o * _silu(g_ref[0].astype(F32))).astype(o_ref.dtype)


def _diff_attn(q, kv_list, g, lam_params, sub_g, lam_init, tq):
    b, t, w = q.shape
    qspec = pl.BlockSpec((1, tq, LANES), lambda bi, h, qi: (bi, qi, h))
    in_specs = [qspec]
    args = [q]
    for k, v in kv_list:
        spec = pl.BlockSpec((1, k.shape[1], LANES), lambda bi, h, qi: (bi, 0, h))
        in_specs += [spec, spec]
        args += [k, v]
    in_specs += [qspec,
                 pl.BlockSpec(lam_params.shape, lambda bi, h, qi: (0, 0)),
                 pl.BlockSpec((1, LANES), lambda bi, h, qi: (0, 0))]
    args += [g, lam_params, sub_g.reshape(1, LANES)]
    return pl.pallas_call(
        functools.partial(_diff_kernel, n_kv=len(kv_list), lam_init=lam_init),
        grid=(b, w // LANES, t // tq),
        in_specs=in_specs,
        out_specs=qspec,
        out_shape=jax.ShapeDtypeStruct((b, t, w), BF16),
        compiler_params=_params("parallel", "parallel", "parallel"),
        name="diff_attn",
    )(*args)


def _rope_tables(t_len):
    m = DIFF_HEAD_DIM // 4
    t = jnp.arange(t_len)
    rows = (t // GRID_W).astype(F32)
    cols = (t % GRID_W).astype(F32)
    inv = ROPE_BASE ** (-jnp.arange(m, dtype=F32) / m)
    ang_r = rows[:, None] * inv[None, :]
    ang_c = cols[:, None] * inv[None, :]
    cos = jnp.concatenate([jnp.cos(ang_r)] * 2 + [jnp.cos(ang_c)] * 2, axis=1)
    sin = jnp.concatenate([-jnp.sin(ang_r), jnp.sin(ang_r), -jnp.sin(ang_c), jnp.sin(ang_c)], axis=1)
    reps = LANES // DIFF_HEAD_DIM
    return jnp.tile(cos, (1, reps)), jnp.tile(sin, (1, reps))


def kernel(x_prompt, x_sample, c, cache_na_k, cache_na_v, state_lru, cache_diff_k, cache_diff_v, c_ctx,
           e_norm, e_ada_w, e_ada_b, e_w_in, e_rpb, e_conv_w, e_conv_b, e_lru_wa, e_lru_ba, e_lru_wx, e_lru_bx,
           e_lru_lam, e_w_out, o_norm, o_ada_w, o_ada_b, o_w_in, o_lq1, o_lk1, o_lq2, o_lk2, o_sub_g, o_w_out,
           final_norm):
    bp, tp, d = x_prompt.shape
    bs, ts, _ = x_sample.shape
    past = cache_na_k.shape[2]
    pad_rows = 2 * SUBLANES - bs - 1
    cc = jnp.concatenate([c, c_ctx[None, :], jnp.zeros((pad_rows, d), F32)], axis=0)

    xp, xs = x_prompt, x_sample
    na_k, na_v, lru_s, df_k, df_v = [], [], [], [], []
    for i in range(DEPTH):
        j = i // 2
        last = i == DEPTH - 1
        fin = final_norm if last else None
        if i % 2 == 0:
            m = _modulation(cc, e_ada_w[j], e_ada_b[j])
            mod_s = m[:bs].reshape(bs, 1, 3 * d)
            mod_p = m[bs:bs + 1].reshape(1, 1, 3 * d)
            w_in = e_w_in[j].astype(BF16)
            w_out = e_w_out[j].astype(BF16)
            w_out_a, w_out_b = w_out[:NA_WIDTH], w_out[NA_WIDTH:]
            wg, bg = _lru_gate_weights(e_lru_wa[j], e_lru_ba[j], e_lru_wx[j], e_lru_bx[j])
            pair_tiles = _na_bias_tiles(e_rpb[j])
            nw, lw = NA_WIDTH, LRU_WIDTH

            q, k, v, ga, xb, gb = _inproj(
                xp, mod_p, e_norm[j], w_in,
                [(nw, BF16, False), (nw, F32, False), (nw, F32, False), (nw, BF16, False),
                 (lw, F32, False), (lw, BF16, False)])
            oa = _ctx_attn(q, k, v, ga)
            zeros = jnp.zeros((bp, lw), F32)
            ob, ff, fb = _lru_branch(xb, gb, zeros, zeros, e_conv_w[j], e_conv_b[j], wg, bg, e_lru_lam[j])
            xp = _outproj([oa, ob], [w_out_a, w_out_b], xp, mod_p, fin)
            na_k.append(k.reshape(bp, tp, NA_HEADS, NA_HEAD_DIM))
            na_v.append(v.reshape(bp, tp, NA_HEADS, NA_HEAD_DIM))
            lru_s.append(jnp.stack([ff, fb], axis=1))

            q, k, v, ga, xb, gb = _inproj(
                xs, mod_s, e_norm[j], w_in,
                [(nw, BF16, False), (nw, BF16, False), (nw, BF16, False), (nw, BF16, False),
                 (lw, F32, False), (lw, BF16, False)])
            ck = cache_na_k[:, j].reshape(bs, past, nw)
            cv = cache_na_v[:, j].reshape(bs, past, nw)
            oa = _na_attn(q, k, v, ck, cv, pair_tiles, ga)
            ob, _, _ = _lru_branch(xb, gb, state_lru[:, j, 0], state_lru[:, j, 1],
                                   e_conv_w[j], e_conv_b[j], wg, bg, e_lru_lam[j])
            xs = _outproj([oa, ob], [w_out_a, w_out_b], xs, mod_s, fin)
        else:
            lam_init = 0.8 - 0.6 * math.exp(-0.3 * i)
            m = _modulation(cc, o_ada_w[j], o_ada_b[j])
            mod_s = m[:bs].reshape(bs, 1, 3 * d)
            mod_p = m[bs:bs + 1].reshape(1, 1, 3 * d)
            w_in = o_w_in[j].astype(BF16)
            w_out = o_w_out[j].astype(BF16)
            lam_params = jnp.stack([o_lq1[j], o_lk1[j], o_lq2[j], o_lk2[j]], axis=0)
            dw = DIFF_WIDTH

            q, k, v, g = _inproj(
                xp, mod_p, o_norm[j], w_in,
                [(dw, BF16, False), (dw, F32, False), (dw, F32, False), (dw, BF16, False)])
            o = _diff_attn(q, [(k, v)], g, lam_params, o_sub_g[j], lam_init, tq=tp)
            xp = _outproj([o], [w_out], xp, mod_p, fin)
            df_k.append(k.reshape(bp, tp, DIFF_HEADS, 2 * DIFF_HEAD_DIM))
            df_v.append(v.reshape(bp, tp, DIFF_HEADS, 2 * DIFF_HEAD_DIM))

            q, k, v, g = _inproj(
                xs, mod_s, o_norm[j], w_in,
                [(dw, BF16, True), (dw, BF16, True), (dw, BF16, False), (dw, BF16, False)],
                rope_tables=_rope_tables(ts))
            ck = cache_diff_k[:, j].reshape(bs, past, dw)
            cv = cache_diff_v[:, j].reshape(bs, past, dw)
            o = _diff_attn(q, [(ck, cv), (k, v)], g, lam_params, o_sub_g[j], lam_init, tq=512)
            xs = _outproj([o], [w_out], xs, mod_s, fin)

    return (xp, xs, jnp.stack(na_k, axis=1), jnp.stack(na_v, axis=1), jnp.stack(lru_s, axis=1),
            jnp.stack(df_k, axis=1), jnp.stack(df_v, axis=1))
```

```python
import functools
import math

import jax
import jax.numpy as jnp
from jax import lax
from jax.experimental import pallas as pl
from jax.experimental.pallas import tpu as pltpu

D_MODEL = 1024
DEPTH = 2
GRID_W = 64
NA_HEADS = 8
NA_HEAD_DIM = 64
NA_WIDTH = NA_HEADS * NA_HEAD_DIM
NA_ROWS_MAX = 8
NA_COLS = 16
RPB_ROWS = 2 * NA_ROWS_MAX - 1
RPB_COLS = 2 * NA_COLS - 1
LRU_WIDTH = 512
LRU_BLOCKS = 8
LRU_BLOCK = LRU_WIDTH // LRU_BLOCKS
LRU_C = 8.0
DIFF_HEADS = 8
DIFF_HEAD_DIM = 64
DIFF_WIDTH = DIFF_HEADS * 2 * DIFF_HEAD_DIM
ROPE_BASE = 10000.0
EPS = 1e-6
NEG_INF = -1e30
LOG2E = math.log2(math.e)

LANES = 128
SUBLANES = 8
MXU_DIM = 256
VMEM_LIMIT = 56 * 1024 * 1024

ROW_BLK = MXU_DIM
KEY_TILE = MXU_DIM

NA_Q_ROWS = 4
NA_K_ROWS = 12
NA_PAIR_TILES = 16

F32 = jnp.float32
BF16 = jnp.bfloat16


def _params(*semantics):
    return pltpu.CompilerParams(dimension_semantics=semantics, vmem_limit_bytes=VMEM_LIMIT)


def _sigmoid(x):
    return 0.5 * jnp.tanh(0.5 * x) + 0.5


def _silu(x):
    return x * _sigmoid(x)


def _dot(a, b):
    return jnp.dot(a, b, preferred_element_type=F32)


def _dot_nt(a, b):
    return lax.dot_general(a, b, (((1,), (1,)), ((), ())), preferred_element_type=F32)


def _mod_kernel(c_ref, w_ref, b_ref, o_ref):
    s = _silu(c_ref[...])
    o_ref[...] = _dot(s.astype(BF16), w_ref[...].astype(BF16)) + b_ref[...]


def _modulation(cc, w, b):
    rows, d = cc.shape
    n = w.shape[1]
    tn = 1024
    return pl.pallas_call(
        _mod_kernel,
        grid=(n // tn,),
        in_specs=[pl.BlockSpec((rows, d), lambda j: (0, 0)),
                  pl.BlockSpec((d, tn), lambda j: (0, j)),
                  pl.BlockSpec((1, tn), lambda j: (0, j))],
        out_specs=pl.BlockSpec((rows, tn), lambda j: (0, j)),
        out_shape=jax.ShapeDtypeStruct((rows, n), F32),
        compiler_params=_params("arbitrary"),
        name="modulation",
    )(cc, w, b.reshape(1, n))


def _rope_block(x, cos, sin_signed, low16):
    partner = jnp.where(low16, pltpu.roll(x, LANES - 16, axis=1), pltpu.roll(x, 16, axis=1))
    return x * cos + partner * sin_signed


def _inproj_kernel(*refs, outs, rope):
    if rope:
        x_ref, mod_ref, g_ref, w_ref, cos_ref, sin_ref = refs[:6]
        out_refs = refs[6:]
    else:
        x_ref, mod_ref, g_ref, w_ref = refs[:4]
        out_refs = refs[4:]
    d = x_ref.shape[-1]
    x = x_ref[0]
    y = x * lax.rsqrt(jnp.mean(x * x, axis=-1, keepdims=True) + EPS) * g_ref[...]
    mod = mod_ref[0]
    h = (y * (1.0 + mod[:, d:2 * d]) + mod[:, :d]).astype(BF16)
    if rope:
        cos = cos_ref[...]
        sin = sin_ref[...]
        lane = lax.broadcasted_iota(jnp.int32, (1, LANES), 1)
        low16 = (lane & 16) == 0
    off = 0
    for o_ref, (width, roped, mul) in zip(out_refs, outs):
        acc = _dot(h, w_ref[:, off:off + width])
        if roped:
            acc = jnp.concatenate(
                [_rope_block(acc[:, j:j + LANES], cos, sin, low16) for j in range(0, width, LANES)], axis=1)
        if mul is not None:
            acc = acc * mul
        o_ref[0] = acc.astype(o_ref.dtype)
        off += width


def _inproj(x, mod, norm_g, w_bf16, outs, rope_tables=None, tm=512):
    b, t, d = x.shape
    tm = min(tm, t)
    n = w_bf16.shape[1]
    per_batch = mod.shape[0] > 1
    mod_map = (lambda i, j: (i, 0, 0)) if per_batch else (lambda i, j: (0, 0, 0))
    in_specs = [pl.BlockSpec((1, tm, d), lambda i, j: (i, j, 0)),
                pl.BlockSpec((1, 1, 3 * d), mod_map),
                pl.BlockSpec((1, d), lambda i, j: (0, 0)),
                pl.BlockSpec((d, n), lambda i, j: (0, 0))]
    args = [x, mod, norm_g.reshape(1, d), w_bf16]
    rope = rope_tables is not None
    if rope:
        in_specs += [pl.BlockSpec((tm, LANES), lambda i, j: (j, 0))] * 2
        args += list(rope_tables)
    kern = functools.partial(_inproj_kernel, outs=tuple((w, r, m) for w, _, r, m in outs), rope=rope)
    return pl.pallas_call(
        kern,
        grid=(b, t // tm),
        in_specs=in_specs,
        out_specs=[pl.BlockSpec((1, tm, w), lambda i, j: (i, j, 0)) for w, _, _, _ in outs],
        out_shape=[jax.ShapeDtypeStruct((b, t, w), dt) for w, dt, _, _ in outs],
        compiler_params=_params("parallel", "parallel"),
        name="inproj",
    )(*args)


def _outproj_kernel(*refs, n_in, final):
    o_refs = refs[:n_in]
    w_refs = refs[n_in:2 * n_in]
    x_ref, mod_ref = refs[2 * n_in:2 * n_in + 2]
    fin_ref = refs[2 * n_in + 2] if final else None
    out_ref = refs[-1]
    d = x_ref.shape[-1]
    acc = _dot(o_refs[0][0], w_refs[0][...])
    for o_ref, w_ref in zip(o_refs[1:], w_refs[1:]):
        acc = acc + _dot(o_ref[0], w_ref[...])
    y = x_ref[0] + mod_ref[0][:, 2 * d:] * acc
    if final:
        y = y * lax.rsqrt(jnp.mean(y * y, axis=-1, keepdims=True) + EPS) * fin_ref[...]
    out_ref[0] = y


def _outproj(o_list, w_list, x, mod, final_g=None, tm=512):
    b, t, d = x.shape
    tm = min(tm, t)
    per_batch = mod.shape[0] > 1
    mod_map = (lambda i, j: (i, 0, 0)) if per_batch else (lambda i, j: (0, 0, 0))
    in_specs = [pl.BlockSpec((1, tm, o.shape[-1]), lambda i, j: (i, j, 0)) for o in o_list]
    in_specs += [pl.BlockSpec(w.shape, lambda i, j: (0, 0)) for w in w_list]
    in_specs += [pl.BlockSpec((1, tm, d), lambda i, j: (i, j, 0)),
                 pl.BlockSpec((1, 1, 3 * d), mod_map)]
    args = list(o_list) + list(w_list) + [x, mod]
    final = final_g is not None
    if final:
        in_specs.append(pl.BlockSpec((1, d), lambda i, j: (0, 0)))
        args.append(final_g.reshape(1, d))
    kern = functools.partial(_outproj_kernel, n_in=len(o_list), final=final)
    return pl.pallas_call(
        kern,
        grid=(b, t // tm),
        in_specs=in_specs,
        out_specs=pl.BlockSpec((1, tm, d), lambda i, j: (i, j, 0)),
        out_shape=jax.ShapeDtypeStruct((b, t, d), F32),
        compiler_params=_params("parallel", "parallel"),
        name="outproj",
    )(*args)


def _attend(qs, k_tiles, v_tiles, s_ref, score_fn=None):
    n_rb = qs.shape[0] // ROW_BLK
    n_tiles = len(k_tiles)
    m_acc = [None] * n_rb

    def rows_of(rb):
        return slice(rb * ROW_BLK, (rb + 1) * ROW_BLK)

    def cols_of(c):
        return slice(c * KEY_TILE, (c + 1) * KEY_TILE)

    def score_tile(rb, c):
        s = _dot_nt(qs[rows_of(rb)], k_tiles[c])
        if score_fn is not None:
            s = score_fn(rb, c, s)
        s_ref[rows_of(rb), cols_of(c)] = s
        mt = jnp.maximum(s[:, :LANES], s[:, LANES:])
        m_acc[rb] = mt if m_acc[rb] is None else jnp.maximum(m_acc[rb], mt)

    for c in range(n_tiles):
        score_tile(0, c)
    outs, sums = [], []
    for rb in range(n_rb):
        m = m_acc[rb].max(axis=-1, keepdims=True)
        l_acc, o = None, None
        for c in range(n_tiles):
            if rb + 1 < n_rb:
                score_tile(rb + 1, c)
            e = jnp.exp2(s_ref[rows_of(rb), cols_of(c)] - m)
            lt = e[:, :LANES] + e[:, LANES:]
            l_acc = lt if l_acc is None else l_acc + lt
            pv = _dot(e.astype(BF16), v_tiles[c])
            o = pv if o is None else o + pv
        outs.append(o)
        sums.append(l_acc.sum(axis=-1, keepdims=True))
    return jnp.concatenate(outs, axis=0), jnp.concatenate(sums, axis=0)


def _lane_low_half():
    return lax.broadcasted_iota(jnp.int32, (1, LANES), 1) < (LANES // 2)


def _stack_lane_halves(q, low):
    zero = jnp.zeros_like(q)
    return jnp.concatenate([jnp.where(low, q, zero), jnp.where(low, zero, q)], axis=0)


def _tiles(ref, n_tiles, j):
    return [ref[0, c * KEY_TILE:(c + 1) * KEY_TILE, j:j + LANES].astype(BF16) for c in range(n_tiles)]


def _ctx_attn_kernel(q_ref, k_ref, v_ref, g_ref, o_ref, s_ref):
    low = _lane_low_half()
    tq = q_ref.shape[1]
    n_tiles = k_ref.shape[1] // KEY_TILE
    for j in range(0, q_ref.shape[2], LANES):
        qs = _stack_lane_halves(q_ref[0, :, j:j + LANES], low)
        o, l = _attend(qs, _tiles(k_ref, n_tiles, j), _tiles(v_ref, n_tiles, j), s_ref.at[j // LANES])
        o = o / l
        o = jnp.where(low, o[:tq], o[tq:])
        o_ref[0, :, j:j + LANES] = (o * _silu(g_ref[0, :, j:j + LANES].astype(F32))).astype(o_ref.dtype)


def _ctx_attn(q, k, v, g):
    b, t, w = q.shape
    spec = pl.BlockSpec((1, t, w), lambda i: (i, 0, 0))
    return pl.pallas_call(
        _ctx_attn_kernel,
        grid=(b,),
        in_specs=[spec] * 4,
        out_specs=spec,
        out_shape=jax.ShapeDtypeStruct((b, t, w), BF16),
        scratch_shapes=[pltpu.VMEM((w // LANES, 2 * t, k.shape[1]), F32)],
        compiler_params=_params("parallel"),
        name="ctx_attn",
    )(q, k, v, g)


def _na_bias_kernel(rpb_ref, o_ref):
    h = pl.program_id(0)
    qcol = lax.broadcasted_iota(jnp.int32, (GRID_W, LANES), 0)
    lane = lax.broadcasted_iota(jnp.int32, (GRID_W, LANES), 1)
    right = lane >= GRID_W
    dc = (lane & (GRID_W - 1)) - qcol + (NA_COLS - 1)
    base = h * (RPB_ROWS * RPB_COLS)
    for p in range(NA_PAIR_TILES):
        acc = jnp.zeros((GRID_W, LANES), F32)
        for c in range(RPB_COLS):
            left_v = rpb_ref[base + (p - 1) * RPB_COLS + c] if p >= 1 else 0.0
            right_v = rpb_ref[base + p * RPB_COLS + c] if p < RPB_ROWS else 0.0
            acc = jnp.where(dc == c, jnp.where(right, right_v, left_v), acc)
        o_ref[0, p] = acc * LOG2E


def _na_bias_tiles(rpb):
    h = rpb.shape[0]
    return pl.pallas_call(
        _na_bias_kernel,
        grid=(h,),
        in_specs=[pl.BlockSpec(memory_space=pltpu.SMEM)],
        out_specs=pl.BlockSpec((1, NA_PAIR_TILES, GRID_W, LANES), lambda i: (i, 0, 0, 0)),
        out_shape=jax.ShapeDtypeStruct((h, NA_PAIR_TILES, GRID_W, LANES), F32),
        compiler_params=_params("arbitrary"),
        name="na_bias_tiles",
    )(rpb.reshape(-1))


def _na_kernel(q_ref, k_ref, v_ref, ck_ref, cv_ref, pt_ref, g_ref, o_ref, bias_ref, s_ref, *, n_rows):
    grp = pl.program_id(1)
    kr = min(NA_ROWS_MAX, n_rows)
    krow0 = jnp.clip(grp * NA_Q_ROWS - NA_Q_ROWS, 0, n_rows - NA_K_ROWS)
    tq = NA_Q_ROWS * GRID_W
    n_win = (NA_K_ROWS * GRID_W) // KEY_TILE
    low = _lane_low_half()

    @pl.when(pl.program_id(2) == 0)
    def _build_bias():
        qcol = lax.broadcasted_iota(jnp.int32, (GRID_W, LANES), 0)
        lane = lax.broadcasted_iota(jnp.int32, (GRID_W, LANES), 1)
        right = lane >= GRID_W
        kcol = lane & (GRID_W - 1)
        col_start = jnp.clip(qcol - NA_COLS // 2, 0, GRID_W - NA_COLS)
        col_ok = (kcol >= col_start) & (kcol < col_start + NA_COLS)
        for half in range(2):
            for i in range(NA_Q_ROWS):
                r = grp * NA_Q_ROWS + i
                row_start = jnp.clip(r - kr // 2, 0, n_rows - kr)
                for m in range(NA_K_ROWS // 2):
                    key_row = krow0 + 2 * m
                    dr = key_row - r + (NA_ROWS_MAX - 1)
                    tile = pt_ref[half, jnp.clip(dr + 1, 0, NA_PAIR_TILES - 1)]
                    ok_l = ((key_row >= row_start) & (key_row < row_start + kr)).astype(jnp.int32)
                    ok_r = ((key_row + 1 >= row_start) & (key_row + 1 < row_start + kr)).astype(jnp.int32)
                    ok = col_ok & (jnp.where(right, ok_r, ok_l) > 0)
                    bias_ref[half, i * GRID_W:(i + 1) * GRID_W, m * LANES:(m + 1) * LANES] = (
                        jnp.where(ok, tile, NEG_INF))

    kstart = krow0 * GRID_W
    k_tiles, v_tiles = [], []
    for c in range(n_win):
        rows = pl.ds(pl.multiple_of(kstart + c * KEY_TILE, KEY_TILE), KEY_TILE)
        k_tiles.append(k_ref[0, rows, :])
        v_tiles.append(v_ref[0, rows, :])
    n_ctx = ck_ref.shape[1] // KEY_TILE
    k_tiles += _tiles(ck_ref, n_ctx, 0)
    v_tiles += _tiles(cv_ref, n_ctx, 0)

    def masked_bias(half, c, s):
        if c >= n_win:
            return s
        b = bias_ref[half, :, c * KEY_TILE:(c + 1) * KEY_TILE]
        return jnp.where(b > 0.5 * NEG_INF, s + b, NEG_INF)

    qs = _stack_lane_halves(q_ref[0], low)
    o, l = _attend(qs, k_tiles, v_tiles, s_ref, masked_bias)
    o = o / l
    o = jnp.where(low, o[:tq], o[tq:])
    o_ref[0] = (o * _silu(g_ref[0].astype(F32))).astype(o_ref.dtype)


def _na_attn(q, k, v, ck, cv, pair_tiles, g):
    b, t, w = q.shape
    c = ck.shape[1]
    n_rows = t // GRID_W
    tq = NA_Q_ROWS * GRID_W
    assert tq == ROW_BLK
    n_win = NA_K_ROWS * GRID_W
    qspec = pl.BlockSpec((1, tq, LANES), lambda hp, gi, bi: (bi, gi, hp))
    kspec = pl.BlockSpec((1, t, LANES), lambda hp, gi, bi: (bi, 0, hp))
    cspec = pl.BlockSpec((1, c, LANES), lambda hp, gi, bi: (bi, 0, hp))
    ptspec = pl.BlockSpec((2, NA_PAIR_TILES, GRID_W, LANES), lambda hp, gi, bi: (hp, 0, 0, 0))
    return pl.pallas_call(
        functools.partial(_na_kernel, n_rows=n_rows),
        grid=(w // LANES, n_rows // NA_Q_ROWS, b),
        in_specs=[qspec, kspec, kspec, cspec, cspec, ptspec, qspec],
        out_specs=qspec,
        out_shape=jax.ShapeDtypeStruct((b, t, w), BF16),
        scratch_shapes=[pltpu.VMEM((2, tq, n_win), F32), pltpu.VMEM((2 * tq, n_win + c), F32)],
        compiler_params=_params("parallel", "parallel", "arbitrary"),
        name="na_attn",
    )(q, k, v, ck, cv, pair_tiles, g)


def _softplus(x):
    return jnp.maximum(x, 0.0) + jnp.log(1.0 + jnp.exp(-jnp.abs(x)))


def _lru_kernel(x_ref, gb_ref, h0f_ref, h0b_ref, cw_ref, cb_ref, wg_ref, bg_ref, lam_ref,
                o_ref, ff_ref, fb_ref, af_s, uf_s, ab_s, ub_s, *, pitch):
    nb, t, _ = x_ref.shape
    cw = cw_ref[...]
    cb = cb_ref[...]
    wg = wg_ref[0]
    bg = bg_ref[0]
    kk = (-0.5 * LRU_C * LOG2E) * _softplus(-lam_ref[...])
    trow = lax.broadcasted_iota(jnp.int32, (t, LANES), 0)

    def gates(b, carry):
        x = x_ref[b]
        xm2 = jnp.where(trow >= 2, pltpu.roll(x, 2, axis=0), 0.0)
        xm1 = jnp.where(trow >= 1, pltpu.roll(x, 1, axis=0), 0.0)
        xp1 = jnp.where(trow < t - 1, pltpu.roll(x, t - 1, axis=0), 0.0)
        xc = cw[0:1] * xm2 + cw[1:2] * xm1 + cw[2:3] * x + cw[3:4] * xp1 + cb
        th = jnp.tanh(_dot(xc.astype(BF16), wg) + bg)
        xh = 0.5 * xc
        row0 = pl.multiple_of(b * pitch, SUBLANES)
        for d, (a_s, u_s) in enumerate(((af_s, uf_s), (ab_s, ub_s))):
            t_a = th[:, (2 * d) * LANES:(2 * d + 1) * LANES]
            t_x = th[:, (2 * d + 1) * LANES:(2 * d + 2) * LANES]
            a = jnp.exp2(kk[d:d + 1] * t_a + kk[d:d + 1])
            y = 1.0 - a * a
            sq = y * lax.rsqrt(jnp.maximum(y, 1e-30))
            a_s[pl.ds(row0, t), :] = a
            u_s[pl.ds(row0, t), :] = sq * (xh * t_x + xh)
        return carry

    lax.fori_loop(0, nb, gates, 0)

    def step(s, carry):
        hf, hb = carry
        fwd = pl.ds(s, nb, stride=pitch)
        hf = af_s[fwd, :] * hf + uf_s[fwd, :]
        uf_s[fwd, :] = hf
        bwd = pl.ds(t - 1 - s, nb, stride=pitch)
        hb = ab_s[bwd, :] * hb + ub_s[bwd, :]
        ub_s[bwd, :] = hb
        return hf, hb

    hf, hb = lax.fori_loop(0, t, step, (h0f_ref[...], h0b_ref[...]), unroll=8)
    ff_ref[...] = hf
    fb_ref[...] = hb

    def emit(b, carry):
        row0 = pl.multiple_of(b * pitch, SUBLANES)
        hs = uf_s[pl.ds(row0, t), :] + ub_s[pl.ds(row0, t), :]
        o_ref[b] = (hs * _silu(gb_ref[b].astype(F32))).astype(o_ref.dtype)
        return carry

    lax.fori_loop(0, nb, emit, 0)


def _lru_branch(xb, gb, h0f, h0b, conv_w, conv_b, wg, bg, lam):
    b, t, w = xb.shape
    nb = SUBLANES
    pitch = t + SUBLANES
    xspec = pl.BlockSpec((nb, t, LANES), lambda i, c: (i, 0, c))
    hspec = pl.BlockSpec((nb, LANES), lambda i, c: (i, c))
    scratch = [pltpu.VMEM((nb * pitch, LANES), F32) for _ in range(4)]
    return pl.pallas_call(
        functools.partial(_lru_kernel, pitch=pitch),
        grid=(b // nb, w // LANES),
        in_specs=[xspec, xspec, hspec, hspec,
                  pl.BlockSpec((conv_w.shape[0], LANES), lambda i, c: (0, c)),
                  pl.BlockSpec((1, LANES), lambda i, c: (0, c)),
                  pl.BlockSpec((1, LANES, 4 * LANES), lambda i, c: (c, 0, 0)),
                  pl.BlockSpec((1, 1, 4 * LANES), lambda i, c: (c, 0, 0)),
                  pl.BlockSpec((2, LANES), lambda i, c: (0, c))],
        out_specs=[xspec, hspec, hspec],
        out_shape=[jax.ShapeDtypeStruct((b, t, w), BF16),
                   jax.ShapeDtypeStruct((b, w), F32),
                   jax.ShapeDtypeStruct((b, w), F32)],
        scratch_shapes=scratch,
        compiler_params=_params("parallel", "parallel"),
        name="lru_branch",
    )(xb, gb, h0f, h0b, conv_w, conv_b.reshape(1, w), wg, bg, lam)


def _lru_gate_weights(wa, ba, wx, bx):
    def tiles(w):
        w2 = w.reshape(LRU_BLOCKS // 2, 2, LRU_BLOCK, LRU_BLOCK)
        z = jnp.zeros_like(w2[:, 0])
        return jnp.concatenate([jnp.concatenate([w2[:, 0], z], axis=2),
                                jnp.concatenate([z, w2[:, 1]], axis=2)], axis=1)
    wg = jnp.concatenate([tiles(wa[0]), tiles(wx[0]), tiles(wa[1]), tiles(wx[1])], axis=2)
    n_tiles = LRU_WIDTH // LANES
    bg = jnp.stack([ba[0], bx[0], ba[1], bx[1]], axis=0).reshape(4, n_tiles, LANES)
    bg = jnp.transpose(bg, (1, 0, 2)).reshape(n_tiles, 1, 4 * LANES)
    return (0.5 * wg).astype(BF16), 0.5 * bg


def _diff_kernel(*refs, n_kv, lam_init):
    q_ref = refs[0]
    kv_refs = refs[1:1 + 2 * n_kv]
    g_ref, lamp_ref, subg_ref, o_ref, s_ref = refs[1 + 2 * n_kv:]
    low = _lane_low_half()
    tq = q_ref.shape[1]
    lp = lamp_ref[...]
    lam = (jnp.exp(jnp.sum(lp[0:1] * lp[1:2], axis=-1, keepdims=True))
           - jnp.exp(jnp.sum(lp[2:3] * lp[3:4], axis=-1, keepdims=True)) + lam_init)
    for j in range(0, q_ref.shape[2], LANES):
        k_tiles, v_tiles = [], []
        for i in range(n_kv):
            n_tiles = kv_refs[2 * i].shape[1] // KEY_TILE
            k_tiles += _tiles(kv_refs[2 * i], n_tiles, j)
            v_tiles += _tiles(kv_refs[2 * i + 1], n_tiles, j)
        qs = _stack_lane_halves(q_ref[0, :, j:j + LANES], low)
        o, l = _attend(qs, k_tiles, v_tiles, s_ref.at[j // LANES])
        o = o[:tq] / l[:tq] - o[tq:] * (lam / l[tq:])
        o = o * lax.rsqrt(jnp.mean(o * o, axis=-1, keepdims=True) + EPS) * subg_ref[...] * (1.0 - lam_init)
        o_ref[0, :, j:j + LANES] = (o * _silu(g_ref[0, :, j:j + LANES].astype(F32))).astype(o_ref.dtype)


def _diff_attn(q, kv_list, g, lam_params, sub_g, lam_init, tq, heads_per_step):
    b, t, w = q.shape
    wb = heads_per_step * LANES
    qspec = pl.BlockSpec((1, tq, wb), lambda bi, h, qi: (bi, qi, h))
    in_specs = [qspec]
    args = [q]
    n_keys = 0
    for k, v in kv_list:
        spec = pl.BlockSpec((1, k.shape[1], wb), lambda bi, h, qi: (bi, 0, h))
        in_specs += [spec, spec]
        args += [k, v]
        n_keys += k.shape[1]
    in_specs += [qspec,
                 pl.BlockSpec(lam_params.shape, lambda bi, h, qi: (0, 0)),
                 pl.BlockSpec((1, LANES), lambda bi, h, qi: (0, 0))]
    args += [g, lam_params, sub_g.reshape(1, LANES)]
    return pl.pallas_call(
        functools.partial(_diff_kernel, n_kv=len(kv_list), lam_init=lam_init),
        grid=(b, w // wb, t // tq),
        in_specs=in_specs,
        out_specs=qspec,
        out_shape=jax.ShapeDtypeStruct((b, t, w), BF16),
        scratch_shapes=[pltpu.VMEM((heads_per_step, 2 * tq, n_keys), F32)],
        compiler_params=_params("parallel", "parallel", "parallel"),
        name="diff_attn",
    )(*args)


def _rope_tables(t_len):
    m = DIFF_HEAD_DIM // 4
    t = jnp.arange(t_len)
    rows = (t // GRID_W).astype(F32)
    cols = (t % GRID_W).astype(F32)
    inv = ROPE_BASE ** (-jnp.arange(m, dtype=F32) / m)
    ang_r = rows[:, None] * inv[None, :]
    ang_c = cols[:, None] * inv[None, :]
    cos = jnp.concatenate([jnp.cos(ang_r)] * 2 + [jnp.cos(ang_c)] * 2, axis=1)
    sin = jnp.concatenate([-jnp.sin(ang_r), jnp.sin(ang_r), -jnp.sin(ang_c), jnp.sin(ang_c)], axis=1)
    reps = LANES // DIFF_HEAD_DIM
    return jnp.tile(cos, (1, reps)), jnp.tile(sin, (1, reps))


def kernel(x_prompt, x_sample, c, cache_na_k, cache_na_v, state_lru, cache_diff_k, cache_diff_v, c_ctx,
           e_norm, e_ada_w, e_ada_b, e_w_in, e_rpb, e_conv_w, e_conv_b, e_lru_wa, e_lru_ba, e_lru_wx, e_lru_bx,
           e_lru_lam, e_w_out, o_norm, o_ada_w, o_ada_b, o_w_in, o_lq1, o_lk1, o_lq2, o_lk2, o_sub_g, o_w_out,
           final_norm):
    bp, tp, d = x_prompt.shape
    bs, ts, _ = x_sample.shape
    past = cache_na_k.shape[2]
    pad_rows = 2 * SUBLANES - bs - 1
    cc = jnp.concatenate([c, c_ctx[None, :], jnp.zeros((pad_rows, d), F32)], axis=0)

    xp, xs = x_prompt, x_sample
    na_k, na_v, lru_s, df_k, df_v = [], [], [], [], []
    for i in range(DEPTH):
        j = i // 2
        last = i == DEPTH - 1
        fin = final_norm if last else None
        if i % 2 == 0:
            m = _modulation(cc, e_ada_w[j], e_ada_b[j])
            mod_s = m[:bs].reshape(bs, 1, 3 * d)
            mod_p = m[bs:bs + 1].reshape(1, 1, 3 * d)
            w_in = e_w_in[j].astype(BF16)
            w_out = e_w_out[j].astype(BF16)
            w_out_a, w_out_b = w_out[:NA_WIDTH], w_out[NA_WIDTH:]
            wg, bg = _lru_gate_weights(e_lru_wa[j], e_lru_ba[j], e_lru_wx[j], e_lru_bx[j])
            pair_tiles = _na_bias_tiles(e_rpb[j])
            nw, lw = NA_WIDTH, LRU_WIDTH
            qmul = NA_HEAD_DIM ** -0.5 * LOG2E

            q, k, v, ga, xb, gb = _inproj(
                xp, mod_p, e_norm[j], w_in,
                [(nw, BF16, False, qmul), (nw, F32, False, None), (nw, F32, False, None),
                 (nw, BF16, False, None), (lw, F32, False, None), (lw, BF16, False, None)])
            oa = _ctx_attn(q, k, v, ga)
            zeros = jnp.zeros((bp, lw), F32)
            ob, ff, fb = _lru_branch(xb, gb, zeros, zeros, e_conv_w[j], e_conv_b[j], wg, bg, e_lru_lam[j])
            xp = _outproj([oa, ob], [w_out_a, w_out_b], xp, mod_p, fin)
            na_k.append(k.reshape(bp, tp, NA_HEADS, NA_HEAD_DIM))
            na_v.append(v.reshape(bp, tp, NA_HEADS, NA_HEAD_DIM))
            lru_s.append(jnp.stack([ff, fb], axis=1))

            q, k, v, ga, xb, gb = _inproj(
                xs, mod_s, e_norm[j], w_in,
                [(nw, BF16, False, qmul), (nw, BF16, False, None), (nw, BF16, False, None),
                 (nw, BF16, False, None), (lw, F32, False, None), (lw, BF16, False, None)])
            ck = cache_na_k[:, j].reshape(bs, past, nw)
            cv = cache_na_v[:, j].reshape(bs, past, nw)
            oa = _na_attn(q, k, v, ck, cv, pair_tiles, ga)
            ob, _, _ = _lru_branch(xb, gb, state_lru[:, j, 0], state_lru[:, j, 1],
                                   e_conv_w[j], e_conv_b[j], wg, bg, e_lru_lam[j])
            xs = _outproj([oa, ob], [w_out_a, w_out_b], xs, mod_s, fin)
        else:
            lam_init = 0.8 - 0.6 * math.exp(-0.3 * i)
            m = _modulation(cc, o_ada_w[j], o_ada_b[j])
            mod_s = m[:bs].reshape(bs, 1, 3 * d)
            mod_p = m[bs:bs + 1].reshape(1, 1, 3 * d)
            w_in = o_w_in[j].astype(BF16)
            w_out = o_w_out[j].astype(BF16)
            lam_params = jnp.stack([o_lq1[j], o_lk1[j], o_lq2[j], o_lk2[j]], axis=0)
            dw = DIFF_WIDTH
            qmul = DIFF_HEAD_DIM ** -0.5 * LOG2E

            q, k, v, g = _inproj(
                xp, mod_p, o_norm[j], w_in,
                [(dw, BF16, False, qmul), (dw, F32, False, None), (dw, F32, False, None), (dw, BF16, False, None)])
            o = _diff_attn(q, [(k, v)], g, lam_params, o_sub_g[j], lam_init, tq=tp, heads_per_step=DIFF_HEADS)
            xp = _outproj([o], [w_out], xp, mod_p, fin)
            df_k.append(k.reshape(bp, tp, DIFF_HEADS, 2 * DIFF_HEAD_DIM))
            df_v.append(v.reshape(bp, tp, DIFF_HEADS, 2 * DIFF_HEAD_DIM))

            q, k, v, g = _inproj(
                xs, mod_s, o_norm[j], w_in,
                [(dw, BF16, True, qmul), (dw, BF16, True, None), (dw, BF16, False, None), (dw, BF16, False, None)],
                rope_tables=_rope_tables(ts))
            ck = cache_diff_k[:, j].reshape(bs, past, dw)
            cv = cache_diff_v[:, j].reshape(bs, past, dw)
            o = _diff_attn(q, [(ck, cv), (k, v)], g, lam_params, o_sub_g[j], lam_init, tq=512, heads_per_step=1)
            xs = _outproj([o], [w_out], xs, mod_s, fin)

    return (xp, xs, jnp.stack(na_k, axis=1), jnp.stack(na_v, axis=1), jnp.stack(lru_s, axis=1),
            jnp.stack(df_k, axis=1), jnp.stack(df_v, axis=1))
```

```python
import functools
import math

import jax
import jax.numpy as jnp
from jax import lax
from jax.experimental import pallas as pl
from jax.experimental.pallas import tpu as pltpu

D_MODEL = 1024
DEPTH = 2
GRID_W = 64
NA_HEADS = 8
NA_HEAD_DIM = 64
NA_WIDTH = NA_HEADS * NA_HEAD_DIM
NA_ROWS_MAX = 8
NA_COLS = 16
RPB_ROWS = 2 * NA_ROWS_MAX - 1
RPB_COLS = 2 * NA_COLS - 1
LRU_WIDTH = 512
LRU_BLOCKS = 8
LRU_BLOCK = LRU_WIDTH // LRU_BLOCKS
LRU_C = 8.0
DIFF_HEADS = 8
DIFF_HEAD_DIM = 64
DIFF_WIDTH = DIFF_HEADS * 2 * DIFF_HEAD_DIM
ROPE_BASE = 10000.0
EPS = 1e-6
NEG_INF = -1e30
LOG2E = math.log2(math.e)

LANES = 128
SUBLANES = 8
MXU_DIM = 256
VMEM_LIMIT = 56 * 1024 * 1024

ROW_BLK = MXU_DIM
KEY_TILE = MXU_DIM

NA_Q_ROWS = 4
NA_K_ROWS = 12
NA_PAIR_TILES = 16

F32 = jnp.float32
BF16 = jnp.bfloat16


def _params(*semantics):
    return pltpu.CompilerParams(dimension_semantics=semantics, vmem_limit_bytes=VMEM_LIMIT)


def _sigmoid(x):
    return 0.5 * jnp.tanh(0.5 * x) + 0.5


def _silu(x):
    return x * _sigmoid(x)


def _dot(a, b):
    return jnp.dot(a, b, preferred_element_type=F32)


def _dot_nt(a, b):
    return lax.dot_general(a, b, (((1,), (1,)), ((), ())), preferred_element_type=F32)


def _mod_kernel(c_ref, w_ref, b_ref, o_ref):
    s = _silu(c_ref[...])
    o_ref[...] = _dot(s.astype(BF16), w_ref[...].astype(BF16)) + b_ref[...]


def _modulation(cc, w, b):
    rows, d = cc.shape
    n = w.shape[1]
    tn = 1024
    return pl.pallas_call(
        _mod_kernel,
        grid=(n // tn,),
        in_specs=[pl.BlockSpec((rows, d), lambda j: (0, 0)),
                  pl.BlockSpec((d, tn), lambda j: (0, j)),
                  pl.BlockSpec((1, tn), lambda j: (0, j))],
        out_specs=pl.BlockSpec((rows, tn), lambda j: (0, j)),
        out_shape=jax.ShapeDtypeStruct((rows, n), F32),
        compiler_params=_params("arbitrary"),
        name="modulation",
    )(cc, w, b.reshape(1, n))


def _rope_block(x, cos, sin_signed, low16):
    partner = jnp.where(low16, pltpu.roll(x, LANES - 16, axis=1), pltpu.roll(x, 16, axis=1))
    return x * cos + partner * sin_signed


def _inproj_kernel(*refs, outs, rope):
    if rope:
        x_ref, mod_ref, g_ref, w_ref, cos_ref, sin_ref = refs[:6]
        out_refs = refs[6:]
    else:
        x_ref, mod_ref, g_ref, w_ref = refs[:4]
        out_refs = refs[4:]
    d = x_ref.shape[-1]
    x = x_ref[0]
    y = x * lax.rsqrt(jnp.mean(x * x, axis=-1, keepdims=True) + EPS) * g_ref[...]
    mod = mod_ref[0]
    h = (y * (1.0 + mod[:, d:2 * d]) + mod[:, :d]).astype(BF16)
    if rope:
        cos = cos_ref[...]
        sin = sin_ref[...]
        lane = lax.broadcasted_iota(jnp.int32, (1, LANES), 1)
        low16 = (lane & 16) == 0
    off = 0
    for o_ref, (width, roped, mul) in zip(out_refs, outs):
        acc = _dot(h, w_ref[:, off:off + width])
        if roped:
            acc = jnp.concatenate(
                [_rope_block(acc[:, j:j + LANES], cos, sin, low16) for j in range(0, width, LANES)], axis=1)
        if mul is not None:
            acc = acc * mul
        o_ref[0] = acc.astype(o_ref.dtype)
        off += width


def _inproj(x, mod, norm_g, w_bf16, outs, rope_tables=None, tm=512):
    b, t, d = x.shape
    tm = min(tm, t)
    n = w_bf16.shape[1]
    per_batch = mod.shape[0] > 1
    mod_map = (lambda i, j: (i, 0, 0)) if per_batch else (lambda i, j: (0, 0, 0))
    in_specs = [pl.BlockSpec((1, tm, d), lambda i, j: (i, j, 0)),
                pl.BlockSpec((1, 1, 3 * d), mod_map),
                pl.BlockSpec((1, d), lambda i, j: (0, 0)),
                pl.BlockSpec((d, n), lambda i, j: (0, 0))]
    args = [x, mod, norm_g.reshape(1, d), w_bf16]
    rope = rope_tables is not None
    if rope:
        in_specs += [pl.BlockSpec((tm, LANES), lambda i, j: (j, 0))] * 2
        args += list(rope_tables)
    kern = functools.partial(_inproj_kernel, outs=tuple((w, r, m) for w, _, r, m in outs), rope=rope)
    return pl.pallas_call(
        kern,
        grid=(b, t // tm),
        in_specs=in_specs,
        out_specs=[pl.BlockSpec((1, tm, w), lambda i, j: (i, j, 0)) for w, _, _, _ in outs],
        out_shape=[jax.ShapeDtypeStruct((b, t, w), dt) for w, dt, _, _ in outs],
        compiler_params=_params("parallel", "parallel"),
        name="inproj",
    )(*args)


def _outproj_kernel(*refs, n_in, final):
    o_refs = refs[:n_in]
    w_refs = refs[n_in:2 * n_in]
    x_ref, mod_ref = refs[2 * n_in:2 * n_in + 2]
    fin_ref = refs[2 * n_in + 2] if final else None
    out_ref = refs[-1]
    d = x_ref.shape[-1]
    acc = _dot(o_refs[0][0], w_refs[0][...])
    for o_ref, w_ref in zip(o_refs[1:], w_refs[1:]):
        acc = acc + _dot(o_ref[0], w_ref[...])
    y = x_ref[0] + mod_ref[0][:, 2 * d:] * acc
    if final:
        y = y * lax.rsqrt(jnp.mean(y * y, axis=-1, keepdims=True) + EPS) * fin_ref[...]
    out_ref[0] = y


def _outproj(o_list, w_list, x, mod, final_g=None, tm=512):
    b, t, d = x.shape
    tm = min(tm, t)
    per_batch = mod.shape[0] > 1
    mod_map = (lambda i, j: (i, 0, 0)) if per_batch else (lambda i, j: (0, 0, 0))
    in_specs = [pl.BlockSpec((1, tm, o.shape[-1]), lambda i, j: (i, j, 0)) for o in o_list]
    in_specs += [pl.BlockSpec(w.shape, lambda i, j: (0, 0)) for w in w_list]
    in_specs += [pl.BlockSpec((1, tm, d), lambda i, j: (i, j, 0)),
                 pl.BlockSpec((1, 1, 3 * d), mod_map)]
    args = list(o_list) + list(w_list) + [x, mod]
    final = final_g is not None
    if final:
        in_specs.append(pl.BlockSpec((1, d), lambda i, j: (0, 0)))
        args.append(final_g.reshape(1, d))
    kern = functools.partial(_outproj_kernel, n_in=len(o_list), final=final)
    return pl.pallas_call(
        kern,
        grid=(b, t // tm),
        in_specs=in_specs,
        out_specs=pl.BlockSpec((1, tm, d), lambda i, j: (i, j, 0)),
        out_shape=jax.ShapeDtypeStruct((b, t, d), F32),
        compiler_params=_params("parallel", "parallel"),
        name="outproj",
    )(*args)


def _attend(q_blocks, kv_blocks, s_ref, score_fn=None):
    n_rb = len(q_blocks)
    ones = jnp.ones((KEY_TILE, LANES), BF16)

    def tile_of(rb, c):
        return (slice(rb * ROW_BLK, (rb + 1) * ROW_BLK), slice(c * KEY_TILE, (c + 1) * KEY_TILE))

    def score_tile(rb, c):
        s_ref[tile_of(rb, c)] = _dot_nt(q_blocks[rb], kv_blocks[rb][0][c])

    def row_max(rb):
        m_acc = None
        for c in range(len(kv_blocks[rb][0])):
            s = s_ref[tile_of(rb, c)]
            adjusted = None if score_fn is None else score_fn(rb, c, s)
            if adjusted is not None:
                s = adjusted
                s_ref[tile_of(rb, c)] = s
            mt = jnp.maximum(s[:, :LANES], s[:, LANES:])
            m_acc = mt if m_acc is None else jnp.maximum(m_acc, mt)
        return m_acc.max(axis=-1, keepdims=True)

    for c in range(len(kv_blocks[0][0])):
        score_tile(0, c)
    outs = []
    for rb in range(n_rb):
        m = row_max(rb)
        n_tiles = len(kv_blocks[rb][0])
        n_next = len(kv_blocks[rb + 1][0]) if rb + 1 < n_rb else 0
        acc = None
        for c in range(max(n_tiles, n_next)):
            if c < n_next:
                score_tile(rb + 1, c)
            if c < n_tiles:
                e = jnp.exp2(s_ref[tile_of(rb, c)] - m)
                pv = _dot(e.astype(BF16), jnp.concatenate([kv_blocks[rb][1][c], ones], axis=1))
                acc = pv if acc is None else acc + pv
        outs.append((acc[:, :LANES], acc[:, LANES:]))
    return outs


def _lane_low_half():
    return lax.broadcasted_iota(jnp.int32, (1, LANES), 1) < (LANES // 2)


def _split_lane_halves(q, low):
    zero = jnp.zeros_like(q)
    return jnp.where(low, q, zero), jnp.where(low, zero, q)


def _tiles(ref, n_tiles, j):
    return [ref[0, c * KEY_TILE:(c + 1) * KEY_TILE, j:j + LANES].astype(BF16) for c in range(n_tiles)]


def _ctx_attn_kernel(q_ref, k_ref, v_ref, g_ref, o_ref, s_ref):
    low = _lane_low_half()
    tq = q_ref.shape[1]
    n_tiles = k_ref.shape[1] // KEY_TILE
    lane_groups = range(0, q_ref.shape[2], LANES)
    q_blocks, kv_blocks = [], []
    for j in lane_groups:
        kv = (_tiles(k_ref, n_tiles, j), _tiles(v_ref, n_tiles, j))
        for r in range(0, tq, ROW_BLK):
            q_blocks += list(_split_lane_halves(q_ref[0, r:r + ROW_BLK, j:j + LANES], low))
            kv_blocks += [kv, kv]
    res = _attend(q_blocks, kv_blocks, s_ref)
    i = 0
    for j in lane_groups:
        for r in range(0, tq, ROW_BLK):
            (o_lo, l_lo), (o_hi, l_hi) = res[i], res[i + 1]
            i += 2
            o = jnp.where(low, o_lo / l_lo, o_hi / l_hi)
            gate = _silu(g_ref[0, r:r + ROW_BLK, j:j + LANES].astype(F32))
            o_ref[0, r:r + ROW_BLK, j:j + LANES] = (o * gate).astype(o_ref.dtype)


def _ctx_attn(q, k, v, g):
    b, t, w = q.shape
    spec = pl.BlockSpec((1, t, w), lambda i: (i, 0, 0))
    return pl.pallas_call(
        _ctx_attn_kernel,
        grid=(b,),
        in_specs=[spec] * 4,
        out_specs=spec,
        out_shape=jax.ShapeDtypeStruct((b, t, w), BF16),
        scratch_shapes=[pltpu.VMEM((2 * t * (w // LANES), k.shape[1]), F32)],
        compiler_params=_params("parallel"),
        name="ctx_attn",
    )(q, k, v, g)


def _na_bias_kernel(rpb_ref, o_ref):
    h = pl.program_id(0)
    qcol = lax.broadcasted_iota(jnp.int32, (GRID_W, LANES), 0)
    lane = lax.broadcasted_iota(jnp.int32, (GRID_W, LANES), 1)
    right = lane >= GRID_W
    dc = (lane & (GRID_W - 1)) - qcol + (NA_COLS - 1)
    base = h * (RPB_ROWS * RPB_COLS)
    for p in range(NA_PAIR_TILES):
        acc = jnp.zeros((GRID_W, LANES), F32)
        for c in range(RPB_COLS):
            left_v = rpb_ref[base + (p - 1) * RPB_COLS + c] if p >= 1 else 0.0
            right_v = rpb_ref[base + p * RPB_COLS + c] if p < RPB_ROWS else 0.0
            acc = jnp.where(dc == c, jnp.where(right, right_v, left_v), acc)
        o_ref[0, p] = acc * LOG2E


def _na_bias_tiles(rpb):
    h = rpb.shape[0]
    return pl.pallas_call(
        _na_bias_kernel,
        grid=(h,),
        in_specs=[pl.BlockSpec(memory_space=pltpu.SMEM)],
        out_specs=pl.BlockSpec((1, NA_PAIR_TILES, GRID_W, LANES), lambda i: (i, 0, 0, 0)),
        out_shape=jax.ShapeDtypeStruct((h, NA_PAIR_TILES, GRID_W, LANES), F32),
        compiler_params=_params("arbitrary"),
        name="na_bias_tiles",
    )(rpb.reshape(-1))


def _na_window(grp, n_rows):
    return min(max(grp * NA_Q_ROWS - NA_Q_ROWS, 0), n_rows - NA_K_ROWS)


def _na_kernel(q_ref, k_ref, v_ref, ck_ref, cv_ref, pt_ref, g_ref, o_ref, badd_ref, bcap_ref, s_ref, *, n_rows):
    kr = min(NA_ROWS_MAX, n_rows)
    n_grp = n_rows // NA_Q_ROWS
    n_win = (NA_K_ROWS * GRID_W) // KEY_TILE
    low = _lane_low_half()

    @pl.when(pl.program_id(1) == 0)
    def _build_bias():
        qcol = lax.broadcasted_iota(jnp.int32, (GRID_W, LANES), 0)
        lane = lax.broadcasted_iota(jnp.int32, (GRID_W, LANES), 1)
        right = lane >= GRID_W
        kcol = lane & (GRID_W - 1)
        col_start = jnp.clip(qcol - NA_COLS // 2, 0, GRID_W - NA_COLS)
        col_ok = (kcol >= col_start) & (kcol < col_start + NA_COLS)
        for grp in range(n_grp):
            krow0 = _na_window(grp, n_rows)
            for i in range(NA_Q_ROWS):
                r = grp * NA_Q_ROWS + i
                row_start = min(max(r - kr // 2, 0), n_rows - kr)
                for m in range(NA_K_ROWS // 2):
                    key_row = krow0 + 2 * m
                    ok_l = row_start <= key_row < row_start + kr
                    ok_r = row_start <= key_row + 1 < row_start + kr
                    rows = slice(i * GRID_W, (i + 1) * GRID_W)
                    cols = slice(m * LANES, (m + 1) * LANES)
                    if not (ok_l or ok_r):
                        bcap_ref[grp, rows, cols] = jnp.full((GRID_W, LANES), NEG_INF, F32)
                        for half in range(2):
                            badd_ref[2 * grp + half, rows, cols] = jnp.zeros((GRID_W, LANES), F32)
                        continue
                    ok = col_ok
                    if not ok_r:
                        ok = ok & jnp.logical_not(right)
                    if not ok_l:
                        ok = ok & right
                    bcap_ref[grp, rows, cols] = jnp.where(ok, jnp.inf, NEG_INF)
                    p = key_row - r + NA_ROWS_MAX
                    for half in range(2):
                        badd_ref[2 * grp + half, rows, cols] = jnp.where(ok, pt_ref[half, p], 0.0)

    n_ctx = ck_ref.shape[1] // KEY_TILE
    ctx_k = _tiles(ck_ref, n_ctx, 0)
    ctx_v = _tiles(cv_ref, n_ctx, 0)
    q_blocks, kv_blocks = [], []
    for grp in range(n_grp):
        k0 = _na_window(grp, n_rows) * GRID_W
        win = [slice(k0 + c * KEY_TILE, k0 + (c + 1) * KEY_TILE) for c in range(n_win)]
        kv = ([k_ref[0, w, :] for w in win] + ctx_k, [v_ref[0, w, :] for w in win] + ctx_v)
        q_blocks += list(_split_lane_halves(q_ref[0, grp * ROW_BLK:(grp + 1) * ROW_BLK, :], low))
        kv_blocks += [kv, kv]

    def masked_bias(rb, c, s):
        if c >= n_win:
            return None
        cols = slice(c * KEY_TILE, (c + 1) * KEY_TILE)
        return jnp.minimum(s + badd_ref[rb, :, cols], bcap_ref[rb // 2, :, cols])

    res = _attend(q_blocks, kv_blocks, s_ref, masked_bias)
    for grp in range(n_grp):
        (o_lo, l_lo), (o_hi, l_hi) = res[2 * grp], res[2 * grp + 1]
        o = jnp.where(low, o_lo / l_lo, o_hi / l_hi)
        rows = slice(grp * ROW_BLK, (grp + 1) * ROW_BLK)
        o_ref[0, rows, :] = (o * _silu(g_ref[0, rows, :].astype(F32))).astype(o_ref.dtype)


def _na_attn(q, k, v, ck, cv, pair_tiles, g):
    b, t, w = q.shape
    c = ck.shape[1]
    n_rows = t // GRID_W
    assert NA_Q_ROWS * GRID_W == ROW_BLK and n_rows % NA_Q_ROWS == 0 and n_rows >= NA_K_ROWS
    n_grp = n_rows // NA_Q_ROWS
    n_win = NA_K_ROWS * GRID_W
    tspec = pl.BlockSpec((1, t, LANES), lambda hp, bi: (bi, 0, hp))
    cspec = pl.BlockSpec((1, c, LANES), lambda hp, bi: (bi, 0, hp))
    ptspec = pl.BlockSpec((2, NA_PAIR_TILES, GRID_W, LANES), lambda hp, bi: (hp, 0, 0, 0))
    return pl.pallas_call(
        functools.partial(_na_kernel, n_rows=n_rows),
        grid=(w // LANES, b),
        in_specs=[tspec, tspec, tspec, cspec, cspec, ptspec, tspec],
        out_specs=tspec,
        out_shape=jax.ShapeDtypeStruct((b, t, w), BF16),
        scratch_shapes=[pltpu.VMEM((2 * n_grp, ROW_BLK, n_win), F32),
                        pltpu.VMEM((n_grp, ROW_BLK, n_win), F32),
                        pltpu.VMEM((2 * t, n_win + c), F32)],
        compiler_params=_params("parallel", "arbitrary"),
        name="na_attn",
    )(q, k, v, ck, cv, pair_tiles, g)


def _softplus(x):
    return jnp.maximum(x, 0.0) + jnp.log(1.0 + jnp.exp(-jnp.abs(x)))


def _lru_kernel(x_ref, gb_ref, h0f_ref, h0b_ref, cw_ref, cb_ref, wg_ref, bg_ref, lam_ref,
                o_ref, ff_ref, fb_ref, af_s, uf_s, ab_s, ub_s, *, pitch):
    nb, t, _ = x_ref.shape
    cw = cw_ref[...]
    cb = cb_ref[...]
    wg = wg_ref[0]
    bg = bg_ref[0]
    kk = (-0.5 * LRU_C * LOG2E) * _softplus(-lam_ref[...])
    trow = lax.broadcasted_iota(jnp.int32, (t, LANES), 0)

    def gates(b, carry):
        x = x_ref[b]
        xm2 = jnp.where(trow >= 2, pltpu.roll(x, 2, axis=0), 0.0)
        xm1 = jnp.where(trow >= 1, pltpu.roll(x, 1, axis=0), 0.0)
        xp1 = jnp.where(trow < t - 1, pltpu.roll(x, t - 1, axis=0), 0.0)
        xc = cw[0:1] * xm2 + cw[1:2] * xm1 + cw[2:3] * x + cw[3:4] * xp1 + cb
        th = jnp.tanh(_dot(xc.astype(BF16), wg) + bg)
        xh = 0.5 * xc
        row0 = pl.multiple_of(b * pitch, SUBLANES)
        for d, (a_s, u_s) in enumerate(((af_s, uf_s), (ab_s, ub_s))):
            t_a = th[:, (2 * d) * LANES:(2 * d + 1) * LANES]
            t_x = th[:, (2 * d + 1) * LANES:(2 * d + 2) * LANES]
            a = jnp.exp2(kk[d:d + 1] * t_a + kk[d:d + 1])
            y = 1.0 - a * a
            sq = y * lax.rsqrt(jnp.maximum(y, 1e-30))
            a_s[pl.ds(row0, t), :] = a
            u_s[pl.ds(row0, t), :] = sq * (xh * t_x + xh)
        return carry

    lax.fori_loop(0, nb, gates, 0)

    def step(s, carry):
        hf, hb = carry
        fwd = pl.ds(s, nb, stride=pitch)
        hf = af_s[fwd, :] * hf + uf_s[fwd, :]
        uf_s[fwd, :] = hf
        bwd = pl.ds(t - 1 - s, nb, stride=pitch)
        hb = ab_s[bwd, :] * hb + ub_s[bwd, :]
        ub_s[bwd, :] = hb
        return hf, hb

    hf, hb = lax.fori_loop(0, t, step, (h0f_ref[...], h0b_ref[...]), unroll=8)
    ff_ref[...] = hf
    fb_ref[...] = hb

    def emit(b, carry):
        row0 = pl.multiple_of(b * pitch, SUBLANES)
        hs = uf_s[pl.ds(row0, t), :] + ub_s[pl.ds(row0, t), :]
        o_ref[b] = (hs * _silu(gb_ref[b].astype(F32))).astype(o_ref.dtype)
        return carry

    lax.fori_loop(0, nb, emit, 0)


def _lru_branch(xb, gb, h0f, h0b, conv_w, conv_b, wg, bg, lam):
    b, t, w = xb.shape
    nb = SUBLANES
    pitch = t + SUBLANES
    xspec = pl.BlockSpec((nb, t, LANES), lambda i, c: (i, 0, c))
    hspec = pl.BlockSpec((nb, LANES), lambda i, c: (i, c))
    scratch = [pltpu.VMEM((nb * pitch, LANES), F32) for _ in range(4)]
    return pl.pallas_call(
        functools.partial(_lru_kernel, pitch=pitch),
        grid=(b // nb, w // LANES),
        in_specs=[xspec, xspec, hspec, hspec,
                  pl.BlockSpec((conv_w.shape[0], LANES), lambda i, c: (0, c)),
                  pl.BlockSpec((1, LANES), lambda i, c: (0, c)),
                  pl.BlockSpec((1, LANES, 4 * LANES), lambda i, c: (c, 0, 0)),
                  pl.BlockSpec((1, 1, 4 * LANES), lambda i, c: (c, 0, 0)),
                  pl.BlockSpec((2, LANES), lambda i, c: (0, c))],
        out_specs=[xspec, hspec, hspec],
        out_shape=[jax.ShapeDtypeStruct((b, t, w), BF16),
                   jax.ShapeDtypeStruct((b, w), F32),
                   jax.ShapeDtypeStruct((b, w), F32)],
        scratch_shapes=scratch,
        compiler_params=_params("parallel", "parallel"),
        name="lru_branch",
    )(xb, gb, h0f, h0b, conv_w, conv_b.reshape(1, w), wg, bg, lam)


def _lru_gate_weights(wa, ba, wx, bx):
    def tiles(w):
        w2 = w.reshape(LRU_BLOCKS // 2, 2, LRU_BLOCK, LRU_BLOCK)
        z = jnp.zeros_like(w2[:, 0])
        return jnp.concatenate([jnp.concatenate([w2[:, 0], z], axis=2),
                                jnp.concatenate([z, w2[:, 1]], axis=2)], axis=1)
    wg = jnp.concatenate([tiles(wa[0]), tiles(wx[0]), tiles(wa[1]), tiles(wx[1])], axis=2)
    n_tiles = LRU_WIDTH // LANES
    bg = jnp.stack([ba[0], bx[0], ba[1], bx[1]], axis=0).reshape(4, n_tiles, LANES)
    bg = jnp.transpose(bg, (1, 0, 2)).reshape(n_tiles, 1, 4 * LANES)
    return (0.5 * wg).astype(BF16), 0.5 * bg


def _diff_kernel(*refs, n_kv, lam_init):
    q_ref = refs[0]
    kv_refs = refs[1:1 + 2 * n_kv]
    g_ref, lamp_ref, subg_ref, o_ref, s_ref = refs[1 + 2 * n_kv:]
    low = _lane_low_half()
    tq = q_ref.shape[1]
    lp = lamp_ref[...]
    lam = (jnp.exp(jnp.sum(lp[0:1] * lp[1:2], axis=-1, keepdims=True))
           - jnp.exp(jnp.sum(lp[2:3] * lp[3:4], axis=-1, keepdims=True)) + lam_init)
    lane_groups = range(0, q_ref.shape[2], LANES)
    q_blocks, kv_blocks = [], []
    for j in lane_groups:
        k_tiles, v_tiles = [], []
        for i in range(n_kv):
            n_tiles = kv_refs[2 * i].shape[1] // KEY_TILE
            k_tiles += _tiles(kv_refs[2 * i], n_tiles, j)
            v_tiles += _tiles(kv_refs[2 * i + 1], n_tiles, j)
        for r in range(0, tq, ROW_BLK):
            q_blocks += list(_split_lane_halves(q_ref[0, r:r + ROW_BLK, j:j + LANES], low))
            kv_blocks += [(k_tiles, v_tiles)] * 2
    res = _attend(q_blocks, kv_blocks, s_ref)
    i = 0
    for j in lane_groups:
        for r in range(0, tq, ROW_BLK):
            (o1, l1), (o2, l2) = res[i], res[i + 1]
            i += 2
            o = o1 / l1 - o2 * (lam / l2)
            o = o * lax.rsqrt(jnp.mean(o * o, axis=-1, keepdims=True) + EPS) * subg_ref[...] * (1.0 - lam_init)
            gate = _silu(g_ref[0, r:r + ROW_BLK, j:j + LANES].astype(F32))
            o_ref[0, r:r + ROW_BLK, j:j + LANES] = (o * gate).astype(o_ref.dtype)


def _diff_attn(q, kv_list, g, lam_params, sub_g, lam_init, tq, heads_per_step):
    b, t, w = q.shape
    wb = heads_per_step * LANES
    qspec = pl.BlockSpec((1, tq, wb), lambda bi, h, qi: (bi, qi, h))
    in_specs = [qspec]
    args = [q]
    n_keys = 0
    for k, v in kv_list:
        spec = pl.BlockSpec((1, k.shape[1], wb), lambda bi, h, qi: (bi, 0, h))
        in_specs += [spec, spec]
        args += [k, v]
        n_keys += k.shape[1]
    in_specs += [qspec,
                 pl.BlockSpec(lam_params.shape, lambda bi, h, qi: (0, 0)),
                 pl.BlockSpec((1, LANES), lambda bi, h, qi: (0, 0))]
    args += [g, lam_params, sub_g.reshape(1, LANES)]
    return pl.pallas_call(
        functools.partial(_diff_kernel, n_kv=len(kv_list), lam_init=lam_init),
        grid=(b, w // wb, t // tq),
        in_specs=in_specs,
        out_specs=qspec,
        out_shape=jax.ShapeDtypeStruct((b, t, w), BF16),
        scratch_shapes=[pltpu.VMEM((heads_per_step * 2 * tq, n_keys), F32)],
        compiler_params=_params("parallel", "parallel", "parallel"),
        name="diff_attn",
    )(*args)


def _rope_tables(t_len):
    m = DIFF_HEAD_DIM // 4
    t = jnp.arange(t_len)
    rows = (t // GRID_W).astype(F32)
    cols = (t % GRID_W).astype(F32)
    inv = ROPE_BASE ** (-jnp.arange(m, dtype=F32) / m)
    ang_r = rows[:, None] * inv[None, :]
    ang_c = cols[:, None] * inv[None, :]
    cos = jnp.concatenate([jnp.cos(ang_r)] * 2 + [jnp.cos(ang_c)] * 2, axis=1)
    sin = jnp.concatenate([-jnp.sin(ang_r), jnp.sin(ang_r), -jnp.sin(ang_c), jnp.sin(ang_c)], axis=1)
    reps = LANES // DIFF_HEAD_DIM
    return jnp.tile(cos, (1, reps)), jnp.tile(sin, (1, reps))


def kernel(x_prompt, x_sample, c, cache_na_k, cache_na_v, state_lru, cache_diff_k, cache_diff_v, c_ctx,
           e_norm, e_ada_w, e_ada_b, e_w_in, e_rpb, e_conv_w, e_conv_b, e_lru_wa, e_lru_ba, e_lru_wx, e_lru_bx,
           e_lru_lam, e_w_out, o_norm, o_ada_w, o_ada_b, o_w_in, o_lq1, o_lk1, o_lq2, o_lk2, o_sub_g, o_w_out,
           final_norm):
    bp, tp, d = x_prompt.shape
    bs, ts, _ = x_sample.shape
    past = cache_na_k.shape[2]
    pad_rows = 2 * SUBLANES - bs - 1
    cc = jnp.concatenate([c, c_ctx[None, :], jnp.zeros((pad_rows, d), F32)], axis=0)

    xp, xs = x_prompt, x_sample
    na_k, na_v, lru_s, df_k, df_v = [], [], [], [], []
    for i in range(DEPTH):
        j = i // 2
        last = i == DEPTH - 1
        fin = final_norm if last else None
        if i % 2 == 0:
            m = _modulation(cc, e_ada_w[j], e_ada_b[j])
            mod_s = m[:bs].reshape(bs, 1, 3 * d)
            mod_p = m[bs:bs + 1].reshape(1, 1, 3 * d)
            w_in = e_w_in[j].astype(BF16)
            w_out = e_w_out[j].astype(BF16)
            w_out_a, w_out_b = w_out[:NA_WIDTH], w_out[NA_WIDTH:]
            wg, bg = _lru_gate_weights(e_lru_wa[j], e_lru_ba[j], e_lru_wx[j], e_lru_bx[j])
            pair_tiles = _na_bias_tiles(e_rpb[j])
            nw, lw = NA_WIDTH, LRU_WIDTH
            qmul = NA_HEAD_DIM ** -0.5 * LOG2E

            q, k, v, ga, xb, gb = _inproj(
                xp, mod_p, e_norm[j], w_in,
                [(nw, BF16, False, qmul), (nw, F32, False, None), (nw, F32, False, None),
                 (nw, BF16, False, None), (lw, F32, False, None), (lw, BF16, False, None)])
            oa = _ctx_attn(q, k, v, ga)
            zeros = jnp.zeros((bp, lw), F32)
            ob, ff, fb = _lru_branch(xb, gb, zeros, zeros, e_conv_w[j], e_conv_b[j], wg, bg, e_lru_lam[j])
            xp = _outproj([oa, ob], [w_out_a, w_out_b], xp, mod_p, fin)
            na_k.append(k.reshape(bp, tp, NA_HEADS, NA_HEAD_DIM))
            na_v.append(v.reshape(bp, tp, NA_HEADS, NA_HEAD_DIM))
            lru_s.append(jnp.stack([ff, fb], axis=1))

            q, k, v, ga, xb, gb = _inproj(
                xs, mod_s, e_norm[j], w_in,
                [(nw, BF16, False, qmul), (nw, BF16, False, None), (nw, BF16, False, None),
                 (nw, BF16, False, None), (lw, F32, False, None), (lw, BF16, False, None)])
            ck = cache_na_k[:, j].reshape(bs, past, nw)
            cv = cache_na_v[:, j].reshape(bs, past, nw)
            oa = _na_attn(q, k, v, ck, cv, pair_tiles, ga)
            ob, _, _ = _lru_branch(xb, gb, state_lru[:, j, 0], state_lru[:, j, 1],
                                   e_conv_w[j], e_conv_b[j], wg, bg, e_lru_lam[j])
            xs = _outproj([oa, ob], [w_out_a, w_out_b], xs, mod_s, fin)
        else:
            lam_init = 0.8 - 0.6 * math.exp(-0.3 * i)
            m = _modulation(cc, o_ada_w[j], o_ada_b[j])
            mod_s = m[:bs].reshape(bs, 1, 3 * d)
            mod_p = m[bs:bs + 1].reshape(1, 1, 3 * d)
            w_in = o_w_in[j].astype(BF16)
            w_out = o_w_out[j].astype(BF16)
            lam_params = jnp.stack([o_lq1[j], o_lk1[j], o_lq2[j], o_lk2[j]], axis=0)
            dw = DIFF_WIDTH
            qmul = DIFF_HEAD_DIM ** -0.5 * LOG2E

            q, k, v, g = _inproj(
                xp, mod_p, o_norm[j], w_in,
                [(dw, BF16, False, qmul), (dw, F32, False, None), (dw, F32, False, None), (dw, BF16, False, None)])
            o = _diff_attn(q, [(k, v)], g, lam_params, o_sub_g[j], lam_init, tq=tp, heads_per_step=DIFF_HEADS)
            xp = _outproj([o], [w_out], xp, mod_p, fin)
            df_k.append(k.reshape(bp, tp, DIFF_HEADS, 2 * DIFF_HEAD_DIM))
            df_v.append(v.reshape(bp, tp, DIFF_HEADS, 2 * DIFF_HEAD_DIM))

            q, k, v, g = _inproj(
                xs, mod_s, o_norm[j], w_in,
                [(dw, BF16, True, qmul), (dw, BF16, True, None), (dw, BF16, False, None), (dw, BF16, False, None)],
                rope_tables=_rope_tables(ts))
            ck = cache_diff_k[:, j].reshape(bs, past, dw)
            cv = cache_diff_v[:, j].reshape(bs, past, dw)
            o = _diff_attn(q, [(ck, cv), (k, v)], g, lam_params, o_sub_g[j], lam_init, tq=ts, heads_per_step=1)
            xs = _outproj([o], [w_out], xs, mod_s, fin)

    return (xp, xs, jnp.stack(na_k, axis=1), jnp.stack(na_v, axis=1), jnp.stack(lru_s, axis=1),
            jnp.stack(df_k, axis=1), jnp.stack(df_v, axis=1))
```

```python
import functools
import math

import jax
import jax.numpy as jnp
from jax import lax
from jax.experimental import pallas as pl
from jax.experimental.pallas import tpu as pltpu

D_MODEL = 1024
DEPTH = 2
GRID_W = 64
NA_HEADS = 8
NA_HEAD_DIM = 64
NA_WIDTH = NA_HEADS * NA_HEAD_DIM
NA_ROWS_MAX = 8
NA_COLS = 16
RPB_ROWS = 2 * NA_ROWS_MAX - 1
RPB_COLS = 2 * NA_COLS - 1
LRU_WIDTH = 512
LRU_BLOCKS = 8
LRU_BLOCK = LRU_WIDTH // LRU_BLOCKS
LRU_C = 8.0
DIFF_HEADS = 8
DIFF_HEAD_DIM = 64
DIFF_WIDTH = DIFF_HEADS * 2 * DIFF_HEAD_DIM
ROPE_BASE = 10000.0
EPS = 1e-6
NEG_INF = -1e30
LOG2E = math.log2(math.e)

LANES = 128
SUBLANES = 8
MXU_DIM = 256
VMEM_LIMIT = 56 * 1024 * 1024

CAST_COLS = 512
ROW_BLK = MXU_DIM
KEY_TILE = MXU_DIM

NA_Q_ROWS = 4
NA_K_ROWS = 12
NA_PAIR_TILES = 16

F32 = jnp.float32
BF16 = jnp.bfloat16


def _params(*semantics):
    return pltpu.CompilerParams(dimension_semantics=semantics, vmem_limit_bytes=VMEM_LIMIT)


def _sigmoid(x):
    return 0.5 * jnp.tanh(0.5 * x) + 0.5


def _silu(x):
    return x * _sigmoid(x)


def _dot(a, b):
    return jnp.dot(a, b, preferred_element_type=F32)


def _dot_nt(a, b):
    return lax.dot_general(a, b, (((1,), (1,)), ((), ())), preferred_element_type=F32)


def _mod_kernel(c_ref, w_ref, b_ref, o_ref):
    s = _silu(c_ref[...])
    o_ref[...] = _dot(s.astype(BF16), w_ref[...].astype(BF16)) + b_ref[...]


def _modulation(cc, w, b):
    rows, d = cc.shape
    n = w.shape[1]
    tn = 1024
    return pl.pallas_call(
        _mod_kernel,
        grid=(n // tn,),
        in_specs=[pl.BlockSpec((rows, d), lambda j: (0, 0)),
                  pl.BlockSpec((d, tn), lambda j: (0, j)),
                  pl.BlockSpec((1, tn), lambda j: (0, j))],
        out_specs=pl.BlockSpec((rows, tn), lambda j: (0, j)),
        out_shape=jax.ShapeDtypeStruct((rows, n), F32),
        compiler_params=_params("arbitrary"),
        name="modulation",
    )(cc, w, b.reshape(1, n))


def _rope_block(x, cos, sin_signed, low16):
    partner = jnp.where(low16, pltpu.roll(x, LANES - 16, axis=1), pltpu.roll(x, 16, axis=1))
    return x * cos + partner * sin_signed


def _cast_weights_once(w_ref, wbf_ref):
    @pl.when((pl.program_id(0) == 0) & (pl.program_id(1) == 0))
    def _():
        for off in range(0, w_ref.shape[1], CAST_COLS):
            wbf_ref[:, off:off + CAST_COLS] = w_ref[:, off:off + CAST_COLS].astype(BF16)


def _inproj_kernel(*refs, outs, rope):
    if rope:
        x_ref, mod_ref, g_ref, w_ref, cos_ref, sin_ref = refs[:6]
        out_refs = refs[6:-1]
    else:
        x_ref, mod_ref, g_ref, w_ref = refs[:4]
        out_refs = refs[4:-1]
    wbf_ref = refs[-1]
    _cast_weights_once(w_ref, wbf_ref)
    d = x_ref.shape[-1]
    tm = x_ref.shape[1]
    x = x_ref[0]
    y = x * lax.rsqrt(jnp.mean(x * x, axis=-1, keepdims=True) + EPS) * g_ref[...]
    mod = mod_ref[0]
    h = (y * (1.0 + mod[:, d:2 * d]) + mod[:, :d]).astype(BF16)
    if rope:
        cos = cos_ref[...]
        sin = sin_ref[...]
        lane = lax.broadcasted_iota(jnp.int32, (1, LANES), 1)
        low16 = (lane & 16) == 0
    off = 0
    for o_ref, (width, roped, mul, packed) in zip(out_refs, outs):
        acc = _dot(h, wbf_ref[:, off:off + width])
        if roped:
            acc = jnp.concatenate(
                [_rope_block(acc[:, j:j + LANES], cos, sin, low16) for j in range(0, width, LANES)], axis=1)
        if mul is not None:
            acc = acc * mul
        if packed:
            n_heads = width // LANES
            for hh in range(n_heads):
                o_ref[0, pl.ds(hh, tm, stride=n_heads), :] = acc[:, hh * LANES:(hh + 1) * LANES].astype(o_ref.dtype)
        else:
            o_ref[0] = acc.astype(o_ref.dtype)
        off += width


def _inproj(x, mod, norm_g, w, outs, rope_tables=None, tm=512):
    b, t, d = x.shape
    tm = min(tm, t)
    n = w.shape[1]
    per_batch = mod.shape[0] > 1
    mod_map = (lambda i, j: (i, 0, 0)) if per_batch else (lambda i, j: (0, 0, 0))
    in_specs = [pl.BlockSpec((1, tm, d), lambda i, j: (i, j, 0)),
                pl.BlockSpec((1, 1, 3 * d), mod_map),
                pl.BlockSpec((1, d), lambda i, j: (0, 0)),
                pl.BlockSpec((d, n), lambda i, j: (0, 0), pipeline_mode=pl.Buffered(1))]
    args = [x, mod, norm_g.reshape(1, d), w]
    rope = rope_tables is not None
    if rope:
        in_specs += [pl.BlockSpec((tm, LANES), lambda i, j: (j, 0))] * 2
        args += list(rope_tables)
    out_specs, out_shape = [], []
    for width, dt, _, _, packed in outs:
        if packed:
            heads = width // LANES
            out_specs.append(pl.BlockSpec((1, tm * heads, LANES), lambda i, j: (i, j, 0)))
            out_shape.append(jax.ShapeDtypeStruct((b, t * heads, LANES), dt))
        else:
            out_specs.append(pl.BlockSpec((1, tm, width), lambda i, j: (i, j, 0)))
            out_shape.append(jax.ShapeDtypeStruct((b, t, width), dt))
    kern = functools.partial(_inproj_kernel, outs=tuple((w_, r, m, p) for w_, _, r, m, p in outs), rope=rope)
    return pl.pallas_call(
        kern,
        grid=(b, t // tm),
        in_specs=in_specs,
        out_specs=out_specs,
        out_shape=out_shape,
        scratch_shapes=[pltpu.VMEM((d, n), BF16)],
        compiler_params=_params("arbitrary", "arbitrary"),
        name="inproj",
    )(*args)


def _outproj_kernel(*refs, n_in, final):
    o_refs = refs[:n_in]
    w_ref, x_ref, mod_ref = refs[n_in:n_in + 3]
    fin_ref = refs[n_in + 3] if final else None
    out_ref, wbf_ref = refs[-2:]
    _cast_weights_once(w_ref, wbf_ref)
    d = x_ref.shape[-1]
    acc, row = None, 0
    for o_ref in o_refs:
        k = o_ref.shape[-1]
        part = _dot(o_ref[0], wbf_ref[row:row + k, :])
        acc = part if acc is None else acc + part
        row += k
    y = x_ref[0] + mod_ref[0][:, 2 * d:] * acc
    if final:
        y = y * lax.rsqrt(jnp.mean(y * y, axis=-1, keepdims=True) + EPS) * fin_ref[...]
    out_ref[0] = y


def _outproj(o_list, w, x, mod, final_g=None, tm=512):
    b, t, d = x.shape
    tm = min(tm, t)
    per_batch = mod.shape[0] > 1
    mod_map = (lambda i, j: (i, 0, 0)) if per_batch else (lambda i, j: (0, 0, 0))
    in_specs = [pl.BlockSpec((1, tm, o.shape[-1]), lambda i, j: (i, j, 0)) for o in o_list]
    in_specs += [pl.BlockSpec(w.shape, lambda i, j: (0, 0), pipeline_mode=pl.Buffered(1)),
                 pl.BlockSpec((1, tm, d), lambda i, j: (i, j, 0)),
                 pl.BlockSpec((1, 1, 3 * d), mod_map)]
    args = list(o_list) + [w, x, mod]
    final = final_g is not None
    if final:
        in_specs.append(pl.BlockSpec((1, d), lambda i, j: (0, 0)))
        args.append(final_g.reshape(1, d))
    kern = functools.partial(_outproj_kernel, n_in=len(o_list), final=final)
    return pl.pallas_call(
        kern,
        grid=(b, t // tm),
        in_specs=in_specs,
        out_specs=pl.BlockSpec((1, tm, d), lambda i, j: (i, j, 0)),
        out_shape=jax.ShapeDtypeStruct((b, t, d), F32),
        scratch_shapes=[pltpu.VMEM(w.shape, BF16)],
        compiler_params=_params("arbitrary", "arbitrary"),
        name="outproj",
    )(*args)


def _attend(q_blocks, kv_blocks, s_ref, score_fn=None):
    n_rb = len(q_blocks)
    ones = jnp.ones((KEY_TILE, LANES), BF16)

    def tile_of(rb, c):
        return (slice(rb * ROW_BLK, (rb + 1) * ROW_BLK), slice(c * KEY_TILE, (c + 1) * KEY_TILE))

    def score_tile(rb, c):
        s_ref[tile_of(rb, c)] = _dot_nt(q_blocks[rb], kv_blocks[rb][0][c])

    def row_max(rb):
        m_acc = None
        for c in range(len(kv_blocks[rb][0])):
            s = s_ref[tile_of(rb, c)]
            adjusted = None if score_fn is None else score_fn(rb, c, s)
            if adjusted is not None:
                s = adjusted
                s_ref[tile_of(rb, c)] = s
            mt = jnp.maximum(s[:, :LANES], s[:, LANES:])
            m_acc = mt if m_acc is None else jnp.maximum(m_acc, mt)
        return m_acc.max(axis=-1, keepdims=True)

    for c in range(len(kv_blocks[0][0])):
        score_tile(0, c)
    outs = []
    for rb in range(n_rb):
        m = row_max(rb)
        n_tiles = len(kv_blocks[rb][0])
        n_next = len(kv_blocks[rb + 1][0]) if rb + 1 < n_rb else 0
        acc = None
        for c in range(max(n_tiles, n_next)):
            if c < n_next:
                score_tile(rb + 1, c)
            if c < n_tiles:
                e = jnp.exp2(s_ref[tile_of(rb, c)] - m)
                pv = _dot(e.astype(BF16), jnp.concatenate([kv_blocks[rb][1][c], ones], axis=1))
                acc = pv if acc is None else acc + pv
        outs.append((acc[:, :LANES], acc[:, LANES:]))
    return outs


def _lane_low_half():
    return lax.broadcasted_iota(jnp.int32, (1, LANES), 1) < (LANES // 2)


def _split_lane_halves(q, low):
    zero = jnp.zeros_like(q)
    return jnp.where(low, q, zero), jnp.where(low, zero, q)


def _tiles(ref, n_tiles, j):
    return [ref[0, c * KEY_TILE:(c + 1) * KEY_TILE, j:j + LANES].astype(BF16) for c in range(n_tiles)]


def _ctx_attn_kernel(q_ref, k_ref, v_ref, g_ref, o_ref, s_ref):
    low = _lane_low_half()
    tq = q_ref.shape[1]
    n_tiles = k_ref.shape[1] // KEY_TILE
    lane_groups = range(0, q_ref.shape[2], LANES)
    q_blocks, kv_blocks = [], []
    for j in lane_groups:
        kv = (_tiles(k_ref, n_tiles, j), _tiles(v_ref, n_tiles, j))
        for r in range(0, tq, ROW_BLK):
            q_blocks += list(_split_lane_halves(q_ref[0, r:r + ROW_BLK, j:j + LANES], low))
            kv_blocks += [kv, kv]
    res = _attend(q_blocks, kv_blocks, s_ref)
    i = 0
    for j in lane_groups:
        for r in range(0, tq, ROW_BLK):
            (o_lo, l_lo), (o_hi, l_hi) = res[i], res[i + 1]
            i += 2
            o = jnp.where(low, o_lo / l_lo, o_hi / l_hi)
            gate = _silu(g_ref[0, r:r + ROW_BLK, j:j + LANES].astype(F32))
            o_ref[0, r:r + ROW_BLK, j:j + LANES] = (o * gate).astype(o_ref.dtype)


def _ctx_attn(q, k, v, g):
    b, t, w = q.shape
    spec = pl.BlockSpec((1, t, w), lambda i: (i, 0, 0))
    return pl.pallas_call(
        _ctx_attn_kernel,
        grid=(b,),
        in_specs=[spec] * 4,
        out_specs=spec,
        out_shape=jax.ShapeDtypeStruct((b, t, w), BF16),
        scratch_shapes=[pltpu.VMEM((2 * t * (w // LANES), k.shape[1]), F32)],
        compiler_params=_params("parallel"),
        name="ctx_attn",
    )(q, k, v, g)


def _na_bias_kernel(rpb_ref, o_ref):
    h = pl.program_id(0)
    qcol = lax.broadcasted_iota(jnp.int32, (GRID_W, LANES), 0)
    lane = lax.broadcasted_iota(jnp.int32, (GRID_W, LANES), 1)
    right = lane >= GRID_W
    dc = (lane & (GRID_W - 1)) - qcol + (NA_COLS - 1)
    base = h * (RPB_ROWS * RPB_COLS)
    for p in range(NA_PAIR_TILES):
        acc = jnp.zeros((GRID_W, LANES), F32)
        for c in range(RPB_COLS):
            left_v = rpb_ref[base + (p - 1) * RPB_COLS + c] if p >= 1 else 0.0
            right_v = rpb_ref[base + p * RPB_COLS + c] if p < RPB_ROWS else 0.0
            acc = jnp.where(dc == c, jnp.where(right, right_v, left_v), acc)
        o_ref[0, p] = acc * LOG2E


def _na_bias_tiles(rpb):
    h = rpb.shape[0]
    return pl.pallas_call(
        _na_bias_kernel,
        grid=(h,),
        in_specs=[pl.BlockSpec(memory_space=pltpu.SMEM)],
        out_specs=pl.BlockSpec((1, NA_PAIR_TILES, GRID_W, LANES), lambda i: (i, 0, 0, 0)),
        out_shape=jax.ShapeDtypeStruct((h, NA_PAIR_TILES, GRID_W, LANES), F32),
        compiler_params=_params("arbitrary"),
        name="na_bias_tiles",
    )(rpb.reshape(-1))


def _na_window(grp, n_rows):
    return min(max(grp * NA_Q_ROWS - NA_Q_ROWS, 0), n_rows - NA_K_ROWS)


def _na_kernel(q_ref, k_ref, v_ref, ck_ref, cv_ref, pt_ref, g_ref, o_ref, badd_ref, bcap_ref, s_ref, *, n_rows):
    kr = min(NA_ROWS_MAX, n_rows)
    n_grp = n_rows // NA_Q_ROWS
    n_win = (NA_K_ROWS * GRID_W) // KEY_TILE
    low = _lane_low_half()

    @pl.when(pl.program_id(1) == 0)
    def _build_bias():
        qcol = lax.broadcasted_iota(jnp.int32, (GRID_W, LANES), 0)
        lane = lax.broadcasted_iota(jnp.int32, (GRID_W, LANES), 1)
        right = lane >= GRID_W
        kcol = lane & (GRID_W - 1)
        col_start = jnp.clip(qcol - NA_COLS // 2, 0, GRID_W - NA_COLS)
        col_ok = (kcol >= col_start) & (kcol < col_start + NA_COLS)
        for grp in range(n_grp):
            krow0 = _na_window(grp, n_rows)
            for i in range(NA_Q_ROWS):
                r = grp * NA_Q_ROWS + i
                row_start = min(max(r - kr // 2, 0), n_rows - kr)
                for m in range(NA_K_ROWS // 2):
                    key_row = krow0 + 2 * m
                    ok_l = row_start <= key_row < row_start + kr
                    ok_r = row_start <= key_row + 1 < row_start + kr
                    rows = slice(i * GRID_W, (i + 1) * GRID_W)
                    cols = slice(m * LANES, (m + 1) * LANES)
                    if not (ok_l or ok_r):
                        bcap_ref[grp, rows, cols] = jnp.full((GRID_W, LANES), NEG_INF, F32)
                        for half in range(2):
                            badd_ref[2 * grp + half, rows, cols] = jnp.zeros((GRID_W, LANES), F32)
                        continue
                    ok = col_ok
                    if not ok_r:
                        ok = ok & jnp.logical_not(right)
                    if not ok_l:
                        ok = ok & right
                    bcap_ref[grp, rows, cols] = jnp.where(ok, jnp.inf, NEG_INF)
                    p = key_row - r + NA_ROWS_MAX
                    for half in range(2):
                        badd_ref[2 * grp + half, rows, cols] = jnp.where(ok, pt_ref[half, p], 0.0)

    n_ctx = ck_ref.shape[1] // KEY_TILE
    ctx_k = _tiles(ck_ref, n_ctx, 0)
    ctx_v = _tiles(cv_ref, n_ctx, 0)
    q_blocks, kv_blocks = [], []
    for grp in range(n_grp):
        k0 = _na_window(grp, n_rows) * GRID_W
        win = [slice(k0 + c * KEY_TILE, k0 + (c + 1) * KEY_TILE) for c in range(n_win)]
        kv = ([k_ref[0, w, :] for w in win] + ctx_k, [v_ref[0, w, :] for w in win] + ctx_v)
        q_blocks += list(_split_lane_halves(q_ref[0, grp * ROW_BLK:(grp + 1) * ROW_BLK, :], low))
        kv_blocks += [kv, kv]

    def masked_bias(rb, c, s):
        if c >= n_win:
            return None
        cols = slice(c * KEY_TILE, (c + 1) * KEY_TILE)
        return jnp.minimum(s + badd_ref[rb, :, cols], bcap_ref[rb // 2, :, cols])

    res = _attend(q_blocks, kv_blocks, s_ref, masked_bias)
    for grp in range(n_grp):
        (o_lo, l_lo), (o_hi, l_hi) = res[2 * grp], res[2 * grp + 1]
        o = jnp.where(low, o_lo / l_lo, o_hi / l_hi)
        rows = slice(grp * ROW_BLK, (grp + 1) * ROW_BLK)
        o_ref[0, rows, :] = (o * _silu(g_ref[0, rows, :].astype(F32))).astype(o_ref.dtype)


def _na_attn(q, k, v, ck, cv, pair_tiles, g):
    b, t, w = q.shape
    c = ck.shape[1]
    n_rows = t // GRID_W
    assert NA_Q_ROWS * GRID_W == ROW_BLK and n_rows % NA_Q_ROWS == 0 and n_rows >= NA_K_ROWS
    n_grp = n_rows // NA_Q_ROWS
    n_win = NA_K_ROWS * GRID_W
    tspec = pl.BlockSpec((1, t, LANES), lambda hp, bi: (bi, 0, hp))
    cspec = pl.BlockSpec((1, c, LANES), lambda hp, bi: (bi, 0, hp))
    ptspec = pl.BlockSpec((2, NA_PAIR_TILES, GRID_W, LANES), lambda hp, bi: (hp, 0, 0, 0))
    return pl.pallas_call(
        functools.partial(_na_kernel, n_rows=n_rows),
        grid=(w // LANES, b),
        in_specs=[tspec, tspec, tspec, cspec, cspec, ptspec, tspec],
        out_specs=tspec,
        out_shape=jax.ShapeDtypeStruct((b, t, w), BF16),
        scratch_shapes=[pltpu.VMEM((2 * n_grp, ROW_BLK, n_win), F32),
                        pltpu.VMEM((n_grp, ROW_BLK, n_win), F32),
                        pltpu.VMEM((2 * t, n_win + c), F32)],
        compiler_params=_params("parallel", "arbitrary"),
        name="na_attn",
    )(q, k, v, ck, cv, pair_tiles, g)


def _softplus(x):
    return jnp.maximum(x, 0.0) + jnp.log(1.0 + jnp.exp(-jnp.abs(x)))


def _lru_kernel(x_ref, gb_ref, h0f_ref, h0b_ref, cw_ref, cb_ref, wg_ref, bg_ref, lam_ref,
                o_ref, ff_ref, fb_ref, af_s, uf_s, ab_s, ub_s, *, pitch):
    nb, t, _ = x_ref.shape
    cw = cw_ref[...]
    cb = cb_ref[...]
    wg = wg_ref[0]
    bg = bg_ref[0]
    kk = (-0.5 * LRU_C * LOG2E) * _softplus(-lam_ref[...])
    trow = lax.broadcasted_iota(jnp.int32, (t, LANES), 0)

    def gates(b, carry):
        x = x_ref[b]
        xm2 = jnp.where(trow >= 2, pltpu.roll(x, 2, axis=0), 0.0)
        xm1 = jnp.where(trow >= 1, pltpu.roll(x, 1, axis=0), 0.0)
        xp1 = jnp.where(trow < t - 1, pltpu.roll(x, t - 1, axis=0), 0.0)
        xc = cw[0:1] * xm2 + cw[1:2] * xm1 + cw[2:3] * x + cw[3:4] * xp1 + cb
        th = jnp.tanh(_dot(xc.astype(BF16), wg) + bg)
        xh = 0.5 * xc
        row0 = pl.multiple_of(b * pitch, SUBLANES)
        for d, (a_s, u_s) in enumerate(((af_s, uf_s), (ab_s, ub_s))):
            t_a = th[:, (2 * d) * LANES:(2 * d + 1) * LANES]
            t_x = th[:, (2 * d + 1) * LANES:(2 * d + 2) * LANES]
            a = jnp.exp2(kk[d:d + 1] * t_a + kk[d:d + 1])
            y = 1.0 - a * a
            sq = y * lax.rsqrt(jnp.maximum(y, 1e-30))
            a_s[pl.ds(row0, t), :] = a
            u_s[pl.ds(row0, t), :] = sq * (xh * t_x + xh)
        return carry

    lax.fori_loop(0, nb, gates, 0)

    def step(s, carry):
        hf, hb = carry
        fwd = pl.ds(s, nb, stride=pitch)
        hf = af_s[fwd, :] * hf + uf_s[fwd, :]
        uf_s[fwd, :] = hf
        bwd = pl.ds(t - 1 - s, nb, stride=pitch)
        hb = ab_s[bwd, :] * hb + ub_s[bwd, :]
        ub_s[bwd, :] = hb
        return hf, hb

    hf, hb = lax.fori_loop(0, t, step, (h0f_ref[...], h0b_ref[...]), unroll=8)
    ff_ref[...] = hf
    fb_ref[...] = hb

    def emit(b, carry):
        row0 = pl.multiple_of(b * pitch, SUBLANES)
        hs = uf_s[pl.ds(row0, t), :] + ub_s[pl.ds(row0, t), :]
        o_ref[b] = (hs * _silu(gb_ref[b].astype(F32))).astype(o_ref.dtype)
        return carry

    lax.fori_loop(0, nb, emit, 0)


def _lru_branch(xb, gb, h0f, h0b, conv_w, conv_b, wg, bg, lam):
    b, t, w = xb.shape
    nb = SUBLANES
    pitch = t + SUBLANES
    xspec = pl.BlockSpec((nb, t, LANES), lambda i, c: (i, 0, c))
    hspec = pl.BlockSpec((nb, LANES), lambda i, c: (i, c))
    scratch = [pltpu.VMEM((nb * pitch, LANES), F32) for _ in range(4)]
    return pl.pallas_call(
        functools.partial(_lru_kernel, pitch=pitch),
        grid=(b // nb, w // LANES),
        in_specs=[xspec, xspec, hspec, hspec,
                  pl.BlockSpec((conv_w.shape[0], LANES), lambda i, c: (0, c)),
                  pl.BlockSpec((1, LANES), lambda i, c: (0, c)),
                  pl.BlockSpec((1, LANES, 4 * LANES), lambda i, c: (c, 0, 0)),
                  pl.BlockSpec((1, 1, 4 * LANES), lambda i, c: (c, 0, 0)),
                  pl.BlockSpec((2, LANES), lambda i, c: (0, c))],
        out_specs=[xspec, hspec, hspec],
        out_shape=[jax.ShapeDtypeStruct((b, t, w), BF16),
                   jax.ShapeDtypeStruct((b, w), F32),
                   jax.ShapeDtypeStruct((b, w), F32)],
        scratch_shapes=scratch,
        compiler_params=_params("parallel", "parallel"),
        name="lru_branch",
    )(xb, gb, h0f, h0b, conv_w, conv_b.reshape(1, w), wg, bg, lam)


def _lru_gate_weights(wa, ba, wx, bx):
    def tiles(w):
        w2 = w.reshape(LRU_BLOCKS // 2, 2, LRU_BLOCK, LRU_BLOCK)
        z = jnp.zeros_like(w2[:, 0])
        return jnp.concatenate([jnp.concatenate([w2[:, 0], z], axis=2),
                                jnp.concatenate([z, w2[:, 1]], axis=2)], axis=1)
    wg = jnp.concatenate([tiles(wa[0]), tiles(wx[0]), tiles(wa[1]), tiles(wx[1])], axis=2)
    n_tiles = LRU_WIDTH // LANES
    bg = jnp.stack([ba[0], bx[0], ba[1], bx[1]], axis=0).reshape(4, n_tiles, LANES)
    bg = jnp.transpose(bg, (1, 0, 2)).reshape(n_tiles, 1, 4 * LANES)
    return (0.5 * wg).astype(BF16), 0.5 * bg


def _packed_tiles(ref, n_heads, head):
    n_tiles = ref.shape[1] // (n_heads * KEY_TILE)
    return [ref[0, pl.ds(c * KEY_TILE * n_heads + head, KEY_TILE, stride=n_heads), :].astype(BF16)
            for c in range(n_tiles)]


def _diff_kernel(*refs, packed, n_heads, lam_init):
    n_kv = len(packed)
    q_ref = refs[0]
    kv_refs = refs[1:1 + 2 * n_kv]
    g_ref, lamp_ref, subg_ref, o_ref, s_ref = refs[1 + 2 * n_kv:]
    low = _lane_low_half()
    tq = q_ref.shape[1]
    first_head = pl.program_id(1) * (q_ref.shape[2] // LANES)
    lp = lamp_ref[...]
    lam = (jnp.exp(jnp.sum(lp[0:1] * lp[1:2], axis=-1, keepdims=True))
           - jnp.exp(jnp.sum(lp[2:3] * lp[3:4], axis=-1, keepdims=True)) + lam_init)
    lane_groups = range(0, q_ref.shape[2], LANES)
    q_blocks, kv_blocks = [], []
    for j in lane_groups:
        k_tiles, v_tiles = [], []
        for i in range(n_kv):
            if packed[i]:
                head = first_head + j // LANES
                k_tiles += _packed_tiles(kv_refs[2 * i], n_heads, head)
                v_tiles += _packed_tiles(kv_refs[2 * i + 1], n_heads, head)
            else:
                n_tiles = kv_refs[2 * i].shape[1] // KEY_TILE
                k_tiles += _tiles(kv_refs[2 * i], n_tiles, j)
                v_tiles += _tiles(kv_refs[2 * i + 1], n_tiles, j)
        for r in range(0, tq, ROW_BLK):
            q_blocks += list(_split_lane_halves(q_ref[0, r:r + ROW_BLK, j:j + LANES], low))
            kv_blocks += [(k_tiles, v_tiles)] * 2
    res = _attend(q_blocks, kv_blocks, s_ref)
    i = 0
    for j in lane_groups:
        for r in range(0, tq, ROW_BLK):
            (o1, l1), (o2, l2) = res[i], res[i + 1]
            i += 2
            o = o1 / l1 - o2 * (lam / l2)
            o = o * lax.rsqrt(jnp.mean(o * o, axis=-1, keepdims=True) + EPS) * subg_ref[...] * (1.0 - lam_init)
            gate = _silu(g_ref[0, r:r + ROW_BLK, j:j + LANES].astype(F32))
            o_ref[0, r:r + ROW_BLK, j:j + LANES] = (o * gate).astype(o_ref.dtype)


def _diff_attn(q, kv_list, g, lam_params, sub_g, lam_init, tq, heads_per_step):
    b, t, w = q.shape
    wb = heads_per_step * LANES
    qspec = pl.BlockSpec((1, tq, wb), lambda bi, h, qi: (bi, qi, h))
    n_heads = w // LANES
    in_specs = [qspec]
    args = [q]
    n_keys = 0
    for k, v, packed in kv_list:
        if packed:
            spec = pl.BlockSpec((1, k.shape[1], LANES), lambda bi, h, qi: (bi, 0, 0))
            n_keys += k.shape[1] // n_heads
        else:
            spec = pl.BlockSpec((1, k.shape[1], wb), lambda bi, h, qi: (bi, 0, h))
            n_keys += k.shape[1]
        in_specs += [spec, spec]
        args += [k, v]
    in_specs += [qspec,
                 pl.BlockSpec(lam_params.shape, lambda bi, h, qi: (0, 0)),
                 pl.BlockSpec((1, LANES), lambda bi, h, qi: (0, 0))]
    args += [g, lam_params, sub_g.reshape(1, LANES)]
    return pl.pallas_call(
        functools.partial(_diff_kernel, packed=tuple(p for _, _, p in kv_list), n_heads=n_heads,
                          lam_init=lam_init),
        grid=(b, w // wb, t // tq),
        in_specs=in_specs,
        out_specs=qspec,
        out_shape=jax.ShapeDtypeStruct((b, t, w), BF16),
        scratch_shapes=[pltpu.VMEM((heads_per_step * 2 * tq, n_keys), F32)],
        compiler_params=_params("parallel", "parallel", "parallel"),
        name="diff_attn",
    )(*args)


def _rope_tables(t_len):
    m = DIFF_HEAD_DIM // 4
    t = jnp.arange(t_len)
    rows = (t // GRID_W).astype(F32)
    cols = (t % GRID_W).astype(F32)
    inv = ROPE_BASE ** (-jnp.arange(m, dtype=F32) / m)
    ang_r = rows[:, None] * inv[None, :]
    ang_c = cols[:, None] * inv[None, :]
    cos = jnp.concatenate([jnp.cos(ang_r)] * 2 + [jnp.cos(ang_c)] * 2, axis=1)
    sin = jnp.concatenate([-jnp.sin(ang_r), jnp.sin(ang_r), -jnp.sin(ang_c), jnp.sin(ang_c)], axis=1)
    reps = LANES // DIFF_HEAD_DIM
    return jnp.tile(cos, (1, reps)), jnp.tile(sin, (1, reps))


def kernel(x_prompt, x_sample, c, cache_na_k, cache_na_v, state_lru, cache_diff_k, cache_diff_v, c_ctx,
           e_norm, e_ada_w, e_ada_b, e_w_in, e_rpb, e_conv_w, e_conv_b, e_lru_wa, e_lru_ba, e_lru_wx, e_lru_bx,
           e_lru_lam, e_w_out, o_norm, o_ada_w, o_ada_b, o_w_in, o_lq1, o_lk1, o_lq2, o_lk2, o_sub_g, o_w_out,
           final_norm):
    bp, tp, d = x_prompt.shape
    bs, ts, _ = x_sample.shape
    past = cache_na_k.shape[2]
    pad_rows = 2 * SUBLANES - bs - 1
    cc = jnp.concatenate([c, c_ctx[None, :], jnp.zeros((pad_rows, d), F32)], axis=0)

    xp, xs = x_prompt, x_sample
    na_k, na_v, lru_s, df_k, df_v = [], [], [], [], []
    for i in range(DEPTH):
        j = i // 2
        last = i == DEPTH - 1
        fin = final_norm if last else None
        if i % 2 == 0:
            m = _modulation(cc, e_ada_w[j], e_ada_b[j])
            mod_s = m[:bs].reshape(bs, 1, 3 * d)
            mod_p = m[bs:bs + 1].reshape(1, 1, 3 * d)
            wg, bg = _lru_gate_weights(e_lru_wa[j], e_lru_ba[j], e_lru_wx[j], e_lru_bx[j])
            pair_tiles = _na_bias_tiles(e_rpb[j])
            nw, lw = NA_WIDTH, LRU_WIDTH
            qmul = NA_HEAD_DIM ** -0.5 * LOG2E

            def even_outs(kv_dtype):
                return [(nw, BF16, False, qmul, False), (nw, kv_dtype, False, None, False),
                        (nw, kv_dtype, False, None, False), (nw, BF16, False, None, False),
                        (lw, F32, False, None, False), (lw, BF16, False, None, False)]

            q, k, v, ga, xb, gb = _inproj(xp, mod_p, e_norm[j], e_w_in[j], even_outs(F32))
            oa = _ctx_attn(q, k, v, ga)
            zeros = jnp.zeros((bp, lw), F32)
            ob, ff, fb = _lru_branch(xb, gb, zeros, zeros, e_conv_w[j], e_conv_b[j], wg, bg, e_lru_lam[j])
            xp = _outproj([oa, ob], e_w_out[j], xp, mod_p, fin)
            na_k.append(k.reshape(bp, tp, NA_HEADS, NA_HEAD_DIM))
            na_v.append(v.reshape(bp, tp, NA_HEADS, NA_HEAD_DIM))
            lru_s.append(jnp.stack([ff, fb], axis=1))

            q, k, v, ga, xb, gb = _inproj(xs, mod_s, e_norm[j], e_w_in[j], even_outs(BF16))
            ck = cache_na_k[:, j].reshape(bs, past, nw)
            cv = cache_na_v[:, j].reshape(bs, past, nw)
            oa = _na_attn(q, k, v, ck, cv, pair_tiles, ga)
            ob, _, _ = _lru_branch(xb, gb, state_lru[:, j, 0], state_lru[:, j, 1],
                                   e_conv_w[j], e_conv_b[j], wg, bg, e_lru_lam[j])
            xs = _outproj([oa, ob], e_w_out[j], xs, mod_s, fin)
        else:
            lam_init = 0.8 - 0.6 * math.exp(-0.3 * i)
            m = _modulation(cc, o_ada_w[j], o_ada_b[j])
            mod_s = m[:bs].reshape(bs, 1, 3 * d)
            mod_p = m[bs:bs + 1].reshape(1, 1, 3 * d)
            lam_params = jnp.stack([o_lq1[j], o_lk1[j], o_lq2[j], o_lk2[j]], axis=0)
            dw = DIFF_WIDTH
            qmul = DIFF_HEAD_DIM ** -0.5 * LOG2E

            q, k, v, g = _inproj(
                xp, mod_p, o_norm[j], o_w_in[j],
                [(dw, BF16, False, qmul, False), (dw, F32, False, None, True), (dw, F32, False, None, True),
                 (dw, BF16, False, None, False)])
            o = _diff_attn(q, [(k, v, True)], g, lam_params, o_sub_g[j], lam_init, tq=tp,
                           heads_per_step=DIFF_HEADS)
            xp = _outproj([o], o_w_out[j], xp, mod_p, fin)
            df_k.append(k.reshape(bp, tp, DIFF_HEADS, 2 * DIFF_HEAD_DIM))
            df_v.append(v.reshape(bp, tp, DIFF_HEADS, 2 * DIFF_HEAD_DIM))

            q, k, v, g = _inproj(
                xs, mod_s, o_norm[j], o_w_in[j],
                [(dw, BF16, True, qmul, False), (dw, BF16, True, None, False), (dw, BF16, False, None, False),
                 (dw, BF16, False, None, False)],
                rope_tables=_rope_tables(ts))
            ck = cache_diff_k[:, j].reshape(bs, past * DIFF_HEADS, 2 * DIFF_HEAD_DIM)
            cv = cache_diff_v[:, j].reshape(bs, past * DIFF_HEADS, 2 * DIFF_HEAD_DIM)
            o = _diff_attn(q, [(ck, cv, True), (k, v, False)], g, lam_params, o_sub_g[j], lam_init, tq=ts,
                           heads_per_step=1)
            xs = _outproj([o], o_w_out[j], xs, mod_s, fin)

    return (xp, xs, jnp.stack(na_k, axis=1), jnp.stack(na_v, axis=1), jnp.stack(lru_s, axis=1),
            jnp.stack(df_k, axis=1), jnp.stack(df_v, axis=1))
```

```python
import functools
import math

import jax
import jax.numpy as jnp
from jax import lax
from jax.experimental import pallas as pl
from jax.experimental.pallas import tpu as pltpu

D_MODEL = 1024
DEPTH = 2
GRID_W = 64
NA_HEADS = 8
NA_HEAD_DIM = 64
NA_WIDTH = NA_HEADS * NA_HEAD_DIM
NA_ROWS_MAX = 8
NA_COLS = 16
RPB_ROWS = 2 * NA_ROWS_MAX - 1
RPB_COLS = 2 * NA_COLS - 1
LRU_WIDTH = 512
LRU_BLOCKS = 8
LRU_BLOCK = LRU_WIDTH // LRU_BLOCKS
LRU_C = 8.0
DIFF_HEADS = 8
DIFF_HEAD_DIM = 64
DIFF_WIDTH = DIFF_HEADS * 2 * DIFF_HEAD_DIM
ROPE_BASE = 10000.0
EPS = 1e-6
NEG_INF = -1e30
LOG2E = math.log2(math.e)

LANES = 128
SUBLANES = 8
MXU_DIM = 256
VMEM_LIMIT = 56 * 1024 * 1024

CAST_COLS = 512
LRU_SEGMENTS = 4
LRU_UNROLL = 8
LRU_CHUNK = 128
CONV_PAD_L = 2
CONV_PAD_R = 1
ROW_BLK = MXU_DIM
KEY_TILE = MXU_DIM

NA_Q_ROWS = 4
NA_K_ROWS = 12
NA_PAIR_TILES = 16

F32 = jnp.float32
BF16 = jnp.bfloat16


def _params(*semantics):
    return pltpu.CompilerParams(dimension_semantics=semantics, vmem_limit_bytes=VMEM_LIMIT)


def _sigmoid(x):
    return 0.5 * jnp.tanh(0.5 * x) + 0.5


def _silu(x):
    return x * _sigmoid(x)


def _dot(a, b):
    return jnp.dot(a, b, preferred_element_type=F32)


def _dot_nt(a, b):
    return lax.dot_general(a, b, (((1,), (1,)), ((), ())), preferred_element_type=F32)


def _mod_kernel(c_ref, w_ref, b_ref, o_ref):
    s = _silu(c_ref[...])
    o_ref[...] = _dot(s.astype(BF16), w_ref[...].astype(BF16)) + b_ref[...]


def _modulation(cc, w, b):
    rows, d = cc.shape
    n = w.shape[1]
    tn = 1024
    return pl.pallas_call(
        _mod_kernel,
        grid=(n // tn,),
        in_specs=[pl.BlockSpec((rows, d), lambda j: (0, 0)),
                  pl.BlockSpec((d, tn), lambda j: (0, j)),
                  pl.BlockSpec((1, tn), lambda j: (0, j))],
        out_specs=pl.BlockSpec((rows, tn), lambda j: (0, j)),
        out_shape=jax.ShapeDtypeStruct((rows, n), F32),
        compiler_params=_params("arbitrary"),
        name="modulation",
    )(cc, w, b.reshape(1, n))


def _rope_block(x, cos, sin_signed, low16):
    partner = jnp.where(low16, pltpu.roll(x, LANES - 16, axis=1), pltpu.roll(x, 16, axis=1))
    return x * cos + partner * sin_signed


def _cast_weights_once(w_ref, wbf_ref):
    @pl.when((pl.program_id(0) == 0) & (pl.program_id(1) == 0))
    def _():
        for off in range(0, w_ref.shape[1], CAST_COLS):
            wbf_ref[:, off:off + CAST_COLS] = w_ref[:, off:off + CAST_COLS].astype(BF16)


def _inproj_kernel(*refs, outs, rope):
    if rope:
        x_ref, mod_ref, g_ref, w_ref, cos_ref, sin_ref = refs[:6]
        out_refs = refs[6:-1]
    else:
        x_ref, mod_ref, g_ref, w_ref = refs[:4]
        out_refs = refs[4:-1]
    wbf_ref = refs[-1]
    _cast_weights_once(w_ref, wbf_ref)
    d = x_ref.shape[-1]
    tm = x_ref.shape[1]
    x = x_ref[0]
    y = x * lax.rsqrt(jnp.mean(x * x, axis=-1, keepdims=True) + EPS) * g_ref[...]
    mod = mod_ref[0]
    h = (y * (1.0 + mod[:, d:2 * d]) + mod[:, :d]).astype(BF16)
    if rope:
        cos = cos_ref[...]
        sin = sin_ref[...]
        lane = lax.broadcasted_iota(jnp.int32, (1, LANES), 1)
        low16 = (lane & 16) == 0
    off = 0
    for o_ref, (width, roped, mul, packed) in zip(out_refs, outs):
        acc = _dot(h, wbf_ref[:, off:off + width])
        if roped:
            acc = jnp.concatenate(
                [_rope_block(acc[:, j:j + LANES], cos, sin, low16) for j in range(0, width, LANES)], axis=1)
        if mul is not None:
            acc = acc * mul
        if packed:
            n_heads = width // LANES
            for hh in range(n_heads):
                o_ref[0, pl.ds(hh, tm, stride=n_heads), :] = acc[:, hh * LANES:(hh + 1) * LANES].astype(o_ref.dtype)
        else:
            o_ref[0] = acc.astype(o_ref.dtype)
        off += width


def _inproj(x, mod, norm_g, w, outs, rope_tables=None, tm=512):
    b, t, d = x.shape
    tm = min(tm, t)
    n = w.shape[1]
    per_batch = mod.shape[0] > 1
    mod_map = (lambda i, j: (i, 0, 0)) if per_batch else (lambda i, j: (0, 0, 0))
    in_specs = [pl.BlockSpec((1, tm, d), lambda i, j: (i, j, 0)),
                pl.BlockSpec((1, 1, 3 * d), mod_map),
                pl.BlockSpec((1, d), lambda i, j: (0, 0)),
                pl.BlockSpec((d, n), lambda i, j: (0, 0), pipeline_mode=pl.Buffered(1))]
    args = [x, mod, norm_g.reshape(1, d), w]
    rope = rope_tables is not None
    if rope:
        in_specs += [pl.BlockSpec((tm, LANES), lambda i, j: (j, 0))] * 2
        args += list(rope_tables)
    out_specs, out_shape = [], []
    for width, dt, _, _, packed in outs:
        if packed:
            heads = width // LANES
            out_specs.append(pl.BlockSpec((1, tm * heads, LANES), lambda i, j: (i, j, 0)))
            out_shape.append(jax.ShapeDtypeStruct((b, t * heads, LANES), dt))
        else:
            out_specs.append(pl.BlockSpec((1, tm, width), lambda i, j: (i, j, 0)))
            out_shape.append(jax.ShapeDtypeStruct((b, t, width), dt))
    kern = functools.partial(_inproj_kernel, outs=tuple((w_, r, m, p) for w_, _, r, m, p in outs), rope=rope)
    return pl.pallas_call(
        kern,
        grid=(b, t // tm),
        in_specs=in_specs,
        out_specs=out_specs,
        out_shape=out_shape,
        scratch_shapes=[pltpu.VMEM((d, n), BF16)],
        compiler_params=_params("arbitrary", "arbitrary"),
        name="inproj",
    )(*args)


def _outproj_kernel(*refs, n_in, final):
    o_refs = refs[:n_in]
    w_ref, x_ref, mod_ref = refs[n_in:n_in + 3]
    fin_ref = refs[n_in + 3] if final else None
    out_ref, wbf_ref = refs[-2:]
    _cast_weights_once(w_ref, wbf_ref)
    d = x_ref.shape[-1]
    acc, row = None, 0
    for o_ref in o_refs:
        k = o_ref.shape[-1]
        part = _dot(o_ref[0], wbf_ref[row:row + k, :])
        acc = part if acc is None else acc + part
        row += k
    y = x_ref[0] + mod_ref[0][:, 2 * d:] * acc
    if final:
        y = y * lax.rsqrt(jnp.mean(y * y, axis=-1, keepdims=True) + EPS) * fin_ref[...]
    out_ref[0] = y


def _outproj(o_list, w, x, mod, final_g=None, tm=512):
    b, t, d = x.shape
    tm = min(tm, t)
    per_batch = mod.shape[0] > 1
    mod_map = (lambda i, j: (i, 0, 0)) if per_batch else (lambda i, j: (0, 0, 0))
    in_specs = [pl.BlockSpec((1, tm, o.shape[-1]), lambda i, j: (i, j, 0)) for o in o_list]
    in_specs += [pl.BlockSpec(w.shape, lambda i, j: (0, 0), pipeline_mode=pl.Buffered(1)),
                 pl.BlockSpec((1, tm, d), lambda i, j: (i, j, 0)),
                 pl.BlockSpec((1, 1, 3 * d), mod_map)]
    args = list(o_list) + [w, x, mod]
    final = final_g is not None
    if final:
        in_specs.append(pl.BlockSpec((1, d), lambda i, j: (0, 0)))
        args.append(final_g.reshape(1, d))
    kern = functools.partial(_outproj_kernel, n_in=len(o_list), final=final)
    return pl.pallas_call(
        kern,
        grid=(b, t // tm),
        in_specs=in_specs,
        out_specs=pl.BlockSpec((1, tm, d), lambda i, j: (i, j, 0)),
        out_shape=jax.ShapeDtypeStruct((b, t, d), F32),
        scratch_shapes=[pltpu.VMEM(w.shape, BF16)],
        compiler_params=_params("arbitrary", "arbitrary"),
        name="outproj",
    )(*args)


def _attend(q_blocks, kv_blocks, s_ref, score_fn=None):
    n_rb = len(q_blocks)
    ones = jnp.ones((KEY_TILE, LANES), BF16)

    def tile_of(rb, c):
        return (slice(rb * ROW_BLK, (rb + 1) * ROW_BLK), slice(c * KEY_TILE, (c + 1) * KEY_TILE))

    def score_tile(rb, c):
        s_ref[tile_of(rb, c)] = _dot_nt(q_blocks[rb], kv_blocks[rb][0][c])

    def row_max(rb):
        m_acc = None
        for c in range(len(kv_blocks[rb][0])):
            s = s_ref[tile_of(rb, c)]
            adjusted = None if score_fn is None else score_fn(rb, c, s)
            if adjusted is not None:
                s = adjusted
                s_ref[tile_of(rb, c)] = s
            mt = jnp.maximum(s[:, :LANES], s[:, LANES:])
            m_acc = mt if m_acc is None else jnp.maximum(m_acc, mt)
        return m_acc.max(axis=-1, keepdims=True)

    for c in range(len(kv_blocks[0][0])):
        score_tile(0, c)
    outs = []
    for rb in range(n_rb):
        m = row_max(rb)
        n_tiles = len(kv_blocks[rb][0])
        n_next = len(kv_blocks[rb + 1][0]) if rb + 1 < n_rb else 0
        acc = None
        for c in range(max(n_tiles, n_next)):
            if c < n_next:
                score_tile(rb + 1, c)
            if c < n_tiles:
                e = jnp.exp2(s_ref[tile_of(rb, c)] - m)
                pv = _dot(e.astype(BF16), jnp.concatenate([kv_blocks[rb][1][c], ones], axis=1))
                acc = pv if acc is None else acc + pv
        outs.append((acc[:, :LANES], acc[:, LANES:]))
    return outs


def _lane_low_half():
    return lax.broadcasted_iota(jnp.int32, (1, LANES), 1) < (LANES // 2)


def _split_lane_halves(q, low):
    zero = jnp.zeros_like(q)
    return jnp.where(low, q, zero), jnp.where(low, zero, q)


def _tiles(ref, n_tiles, j):
    return [ref[0, c * KEY_TILE:(c + 1) * KEY_TILE, j:j + LANES].astype(BF16) for c in range(n_tiles)]


def _ctx_attn_kernel(q_ref, k_ref, v_ref, g_ref, o_ref, s_ref):
    low = _lane_low_half()
    tq = q_ref.shape[1]
    n_tiles = k_ref.shape[1] // KEY_TILE
    lane_groups = range(0, q_ref.shape[2], LANES)
    q_blocks, kv_blocks = [], []
    for j in lane_groups:
        kv = (_tiles(k_ref, n_tiles, j), _tiles(v_ref, n_tiles, j))
        for r in range(0, tq, ROW_BLK):
            q_blocks += list(_split_lane_halves(q_ref[0, r:r + ROW_BLK, j:j + LANES], low))
            kv_blocks += [kv, kv]
    res = _attend(q_blocks, kv_blocks, s_ref)
    i = 0
    for j in lane_groups:
        for r in range(0, tq, ROW_BLK):
            (o_lo, l_lo), (o_hi, l_hi) = res[i], res[i + 1]
            i += 2
            o = jnp.where(low, o_lo / l_lo, o_hi / l_hi)
            gate = _silu(g_ref[0, r:r + ROW_BLK, j:j + LANES].astype(F32))
            o_ref[0, r:r + ROW_BLK, j:j + LANES] = (o * gate).astype(o_ref.dtype)


def _ctx_attn(q, k, v, g):
    b, t, w = q.shape
    spec = pl.BlockSpec((1, t, w), lambda i: (i, 0, 0))
    return pl.pallas_call(
        _ctx_attn_kernel,
        grid=(b,),
        in_specs=[spec] * 4,
        out_specs=spec,
        out_shape=jax.ShapeDtypeStruct((b, t, w), BF16),
        scratch_shapes=[pltpu.VMEM((2 * t * (w // LANES), k.shape[1]), F32)],
        compiler_params=_params("parallel"),
        name="ctx_attn",
    )(q, k, v, g)


def _na_bias_kernel(rpb_ref, o_ref):
    h = pl.program_id(0)
    qcol = lax.broadcasted_iota(jnp.int32, (GRID_W, LANES), 0)
    lane = lax.broadcasted_iota(jnp.int32, (GRID_W, LANES), 1)
    right = lane >= GRID_W
    dc = (lane & (GRID_W - 1)) - qcol + (NA_COLS - 1)
    base = h * (RPB_ROWS * RPB_COLS)
    for p in range(NA_PAIR_TILES):
        acc = jnp.zeros((GRID_W, LANES), F32)
        for c in range(RPB_COLS):
            left_v = rpb_ref[base + (p - 1) * RPB_COLS + c] if p >= 1 else 0.0
            right_v = rpb_ref[base + p * RPB_COLS + c] if p < RPB_ROWS else 0.0
            acc = jnp.where(dc == c, jnp.where(right, right_v, left_v), acc)
        o_ref[0, p] = acc * LOG2E


def _na_bias_tiles(rpb):
    h = rpb.shape[0]
    return pl.pallas_call(
        _na_bias_kernel,
        grid=(h,),
        in_specs=[pl.BlockSpec(memory_space=pltpu.SMEM)],
        out_specs=pl.BlockSpec((1, NA_PAIR_TILES, GRID_W, LANES), lambda i: (i, 0, 0, 0)),
        out_shape=jax.ShapeDtypeStruct((h, NA_PAIR_TILES, GRID_W, LANES), F32),
        compiler_params=_params("arbitrary"),
        name="na_bias_tiles",
    )(rpb.reshape(-1))


def _na_window(grp, n_rows):
    return min(max(grp * NA_Q_ROWS - NA_Q_ROWS, 0), n_rows - NA_K_ROWS)


def _na_kernel(q_ref, k_ref, v_ref, ck_ref, cv_ref, pt_ref, g_ref, o_ref, badd_ref, bcap_ref, s_ref, *, n_rows):
    kr = min(NA_ROWS_MAX, n_rows)
    n_grp = n_rows // NA_Q_ROWS
    n_win = (NA_K_ROWS * GRID_W) // KEY_TILE
    low = _lane_low_half()

    @pl.when(pl.program_id(1) == 0)
    def _build_bias():
        qcol = lax.broadcasted_iota(jnp.int32, (GRID_W, LANES), 0)
        lane = lax.broadcasted_iota(jnp.int32, (GRID_W, LANES), 1)
        right = lane >= GRID_W
        kcol = lane & (GRID_W - 1)
        col_start = jnp.clip(qcol - NA_COLS // 2, 0, GRID_W - NA_COLS)
        col_ok = (kcol >= col_start) & (kcol < col_start + NA_COLS)
        for grp in range(n_grp):
            krow0 = _na_window(grp, n_rows)
            for i in range(NA_Q_ROWS):
                r = grp * NA_Q_ROWS + i
                row_start = min(max(r - kr // 2, 0), n_rows - kr)
                for m in range(NA_K_ROWS // 2):
                    key_row = krow0 + 2 * m
                    ok_l = row_start <= key_row < row_start + kr
                    ok_r = row_start <= key_row + 1 < row_start + kr
                    rows = slice(i * GRID_W, (i + 1) * GRID_W)
                    cols = slice(m * LANES, (m + 1) * LANES)
                    if not (ok_l or ok_r):
                        bcap_ref[grp, rows, cols] = jnp.full((GRID_W, LANES), NEG_INF, F32)
                        for half in range(2):
                            badd_ref[2 * grp + half, rows, cols] = jnp.zeros((GRID_W, LANES), F32)
                        continue
                    ok = col_ok
                    if not ok_r:
                        ok = ok & jnp.logical_not(right)
                    if not ok_l:
                        ok = ok & right
                    bcap_ref[grp, rows, cols] = jnp.where(ok, jnp.inf, NEG_INF)
                    p = key_row - r + NA_ROWS_MAX
                    for half in range(2):
                        badd_ref[2 * grp + half, rows, cols] = jnp.where(ok, pt_ref[half, p], 0.0)

    n_ctx = ck_ref.shape[1] // KEY_TILE
    ctx_k = _tiles(ck_ref, n_ctx, 0)
    ctx_v = _tiles(cv_ref, n_ctx, 0)
    q_blocks, kv_blocks = [], []
    for grp in range(n_grp):
        k0 = _na_window(grp, n_rows) * GRID_W
        win = [slice(k0 + c * KEY_TILE, k0 + (c + 1) * KEY_TILE) for c in range(n_win)]
        kv = ([k_ref[0, w, :] for w in win] + ctx_k, [v_ref[0, w, :] for w in win] + ctx_v)
        q_blocks += list(_split_lane_halves(q_ref[0, grp * ROW_BLK:(grp + 1) * ROW_BLK, :], low))
        kv_blocks += [kv, kv]

    def masked_bias(rb, c, s):
        if c >= n_win:
            return None
        cols = slice(c * KEY_TILE, (c + 1) * KEY_TILE)
        return jnp.minimum(s + badd_ref[rb, :, cols], bcap_ref[rb // 2, :, cols])

    res = _attend(q_blocks, kv_blocks, s_ref, masked_bias)
    for grp in range(n_grp):
        (o_lo, l_lo), (o_hi, l_hi) = res[2 * grp], res[2 * grp + 1]
        o = jnp.where(low, o_lo / l_lo, o_hi / l_hi)
        rows = slice(grp * ROW_BLK, (grp + 1) * ROW_BLK)
        o_ref[0, rows, :] = (o * _silu(g_ref[0, rows, :].astype(F32))).astype(o_ref.dtype)


def _na_attn(q, k, v, ck, cv, pair_tiles, g):
    b, t, w = q.shape
    c = ck.shape[1]
    n_rows = t // GRID_W
    assert NA_Q_ROWS * GRID_W == ROW_BLK and n_rows % NA_Q_ROWS == 0 and n_rows >= NA_K_ROWS
    n_grp = n_rows // NA_Q_ROWS
    n_win = NA_K_ROWS * GRID_W
    tspec = pl.BlockSpec((1, t, LANES), lambda hp, bi: (bi, 0, hp))
    cspec = pl.BlockSpec((1, c, LANES), lambda hp, bi: (bi, 0, hp))
    ptspec = pl.BlockSpec((2, NA_PAIR_TILES, GRID_W, LANES), lambda hp, bi: (hp, 0, 0, 0))
    return pl.pallas_call(
        functools.partial(_na_kernel, n_rows=n_rows),
        grid=(w // LANES, b),
        in_specs=[tspec, tspec, tspec, cspec, cspec, ptspec, tspec],
        out_specs=tspec,
        out_shape=jax.ShapeDtypeStruct((b, t, w), BF16),
        scratch_shapes=[pltpu.VMEM((2 * n_grp, ROW_BLK, n_win), F32),
                        pltpu.VMEM((n_grp, ROW_BLK, n_win), F32),
                        pltpu.VMEM((2 * t, n_win + c), F32)],
        compiler_params=_params("parallel", "arbitrary"),
        name="na_attn",
    )(q, k, v, ck, cv, pair_tiles, g)


def _softplus(x):
    return jnp.maximum(x, 0.0) + jnp.log(1.0 + jnp.exp(-jnp.abs(x)))


def _lru_kernel(x_ref, gb_ref, h0f_ref, h0b_ref, cw_ref, cb_ref, wg_ref, lam_ref,
                o_ref, ff_ref, fb_ref, xt_s, af_s, uf_s, ab_s, ub_s, hf_s, hb_s):
    nb, t, _ = x_ref.shape
    pad = CONV_PAD_L * nb
    cw = cw_ref[...]
    cb = cb_ref[...]
    wg = wg_ref[0]
    kk = (-0.5 * LRU_C * LOG2E) * _softplus(-lam_ref[...])

    xt_s[0:pad, :] = jnp.zeros((pad, LANES), F32)
    xt_s[pad + t * nb:pad + (t + CONV_PAD_R) * nb, :] = jnp.zeros((CONV_PAD_R * nb, LANES), F32)

    def to_time_major(b, carry):
        xt_s[pl.ds(pad + b, t, stride=nb), :] = x_ref[b]
        return carry

    lax.fori_loop(0, nb, to_time_major, 0)

    rows_per_chunk = LRU_CHUNK * nb
    lane = lax.broadcasted_iota(jnp.int32, (rows_per_chunk, LANES), 1)
    bias_cols = jnp.where(lane < 2, 1.0, 0.0).astype(BF16)

    def gates(c, carry):
        r0 = pl.multiple_of(c * rows_per_chunk, rows_per_chunk)

        def tap(i):
            return xt_s[pl.ds(r0 + i * nb, rows_per_chunk), :]

        x = tap(2)
        xc = cw[0:1] * tap(0) + cw[1:2] * tap(1) + cw[2:3] * x + cw[3:4] * tap(3) + cb
        th = jnp.tanh(_dot(jnp.concatenate([xc.astype(BF16), bias_cols], axis=1), wg))
        xh = 0.5 * xc
        rows = pl.ds(r0, rows_per_chunk)
        for d, (a_s, u_s) in enumerate(((af_s, uf_s), (ab_s, ub_s))):
            t_a = th[:, (2 * d) * LANES:(2 * d + 1) * LANES]
            t_x = th[:, (2 * d + 1) * LANES:(2 * d + 2) * LANES]
            a = jnp.exp2(kk[d:d + 1] * t_a + kk[d:d + 1])
            y = 1.0 - a * a
            sq = y * lax.rsqrt(jnp.maximum(y, 1e-30))
            a_s[rows, :] = a
            u_s[rows, :] = sq * (xh * t_x + xh)
        return carry

    lax.fori_loop(0, t // LRU_CHUNK, gates, 0)

    ts = t // LRU_SEGMENTS
    refs = ((af_s, uf_s), (ab_s, ub_s))

    slab = nb * LRU_UNROLL

    def slab_rows(d, j, i):
        base = (j * ts + i * LRU_UNROLL) * nb
        return pl.ds(pl.multiple_of(base if d == 0 else t * nb - slab - base, slab), slab)

    def step_of(d, block, k):
        kk_ = k if d == 0 else LRU_UNROLL - 1 - k
        return block[kk_ * nb:(kk_ + 1) * nb]

    def compose(i, carry):
        state = list(carry)
        for d in range(2):
            for j in range(LRU_SEGMENTS - 1):
                n = d * (LRU_SEGMENTS - 1) + j
                a_blk = refs[d][0][slab_rows(d, j, i), :]
                u_blk = refs[d][1][slab_rows(d, j, i), :]
                g, big_a = state[n]
                for k in range(LRU_UNROLL):
                    a = step_of(d, a_blk, k)
                    g, big_a = a * g + step_of(d, u_blk, k), a * big_a
                state[n] = (g, big_a)
        return tuple(state)

    ones = jnp.ones((nb, LANES), F32)
    h0 = (h0f_ref[...], h0b_ref[...])
    init = tuple((h0[d] if j == 0 else jnp.zeros((nb, LANES), F32), ones)
                 for d in range(2) for j in range(LRU_SEGMENTS - 1))
    ends = lax.fori_loop(0, ts // LRU_UNROLL, compose, init)
    starts = []
    for d in range(2):
        h = h0[d]
        for j in range(LRU_SEGMENTS):
            starts.append(h)
            if j < LRU_SEGMENTS - 1:
                g, big_a = ends[d * (LRU_SEGMENTS - 1) + j]
                h = g if j == 0 else g + big_a * h

    def rerun(i, carry):
        state = list(carry)
        for d, h_s in enumerate((hf_s, hb_s)):
            for j in range(LRU_SEGMENTS):
                n = d * LRU_SEGMENTS + j
                a_blk = refs[d][0][slab_rows(d, j, i), :]
                u_blk = refs[d][1][slab_rows(d, j, i), :]
                hs = []
                for k in range(LRU_UNROLL):
                    state[n] = step_of(d, a_blk, k) * state[n] + step_of(d, u_blk, k)
                    hs.append(state[n])
                h_s[slab_rows(d, j, i), :] = jnp.concatenate(hs if d == 0 else hs[::-1], axis=0)
        return tuple(state)

    final = lax.fori_loop(0, ts // LRU_UNROLL, rerun, tuple(starts))
    ff_ref[...] = final[LRU_SEGMENTS - 1]
    fb_ref[...] = final[2 * LRU_SEGMENTS - 1]

    def add_directions(c, carry):
        rows = pl.ds(pl.multiple_of(c * rows_per_chunk, rows_per_chunk), rows_per_chunk)
        hf_s[rows, :] = hf_s[rows, :] + hb_s[rows, :]
        return carry

    lax.fori_loop(0, t // LRU_CHUNK, add_directions, 0)

    def emit(b, carry):
        hs = hf_s[pl.ds(b, t, stride=nb), :]
        o_ref[b] = (hs * _silu(gb_ref[b].astype(F32))).astype(o_ref.dtype)
        return carry

    lax.fori_loop(0, nb, emit, 0)


def _lru_branch(xb, gb, h0f, h0b, conv_w, conv_b, wg, lam):
    b, t, w = xb.shape
    nb = SUBLANES
    assert t % (LRU_SEGMENTS * LRU_UNROLL) == 0 and t % LRU_CHUNK == 0
    assert conv_w.shape[0] == CONV_PAD_L + 1 + CONV_PAD_R
    xspec = pl.BlockSpec((nb, t, LANES), lambda i, c: (i, 0, c))
    hspec = pl.BlockSpec((nb, LANES), lambda i, c: (i, c))
    scratch = [pltpu.VMEM(((t + CONV_PAD_L + CONV_PAD_R) * nb, LANES), F32)]
    scratch += [pltpu.VMEM((t * nb, LANES), F32) for _ in range(6)]
    return pl.pallas_call(
        _lru_kernel,
        grid=(b // nb, w // LANES),
        in_specs=[xspec, xspec, hspec, hspec,
                  pl.BlockSpec((conv_w.shape[0], LANES), lambda i, c: (0, c)),
                  pl.BlockSpec((1, LANES), lambda i, c: (0, c)),
                  pl.BlockSpec((1, 2 * LANES, 4 * LANES), lambda i, c: (c, 0, 0)),
                  pl.BlockSpec((2, LANES), lambda i, c: (0, c))],
        out_specs=[xspec, hspec, hspec],
        out_shape=[jax.ShapeDtypeStruct((b, t, w), BF16),
                   jax.ShapeDtypeStruct((b, w), F32),
                   jax.ShapeDtypeStruct((b, w), F32)],
        scratch_shapes=scratch,
        compiler_params=_params("parallel", "parallel"),
        name="lru_branch",
    )(xb, gb, h0f, h0b, conv_w, conv_b.reshape(1, w), wg, lam)


def _lru_gate_weights(wa, ba, wx, bx):
    def tiles(w):
        w2 = w.reshape(LRU_BLOCKS // 2, 2, LRU_BLOCK, LRU_BLOCK)
        z = jnp.zeros_like(w2[:, 0])
        return jnp.concatenate([jnp.concatenate([w2[:, 0], z], axis=2),
                                jnp.concatenate([z, w2[:, 1]], axis=2)], axis=1)
    wg = 0.5 * jnp.concatenate([tiles(wa[0]), tiles(wx[0]), tiles(wa[1]), tiles(wx[1])], axis=2)
    n_tiles = LRU_WIDTH // LANES
    bg = jnp.stack([ba[0], bx[0], ba[1], bx[1]], axis=0).reshape(4, n_tiles, LANES)
    bg = 0.5 * jnp.transpose(bg, (1, 0, 2)).reshape(n_tiles, 1, 4 * LANES)
    hi = bg.astype(BF16)
    lo = (bg - hi.astype(F32)).astype(BF16)
    rest = jnp.zeros((n_tiles, LANES - 2, 4 * LANES), BF16)
    return jnp.concatenate([wg.astype(BF16), hi, lo, rest], axis=1)


def _packed_tiles(ref, n_heads, head):
    n_tiles = ref.shape[1] // (n_heads * KEY_TILE)
    return [ref[0, pl.ds(c * KEY_TILE * n_heads + head, KEY_TILE, stride=n_heads), :].astype(BF16)
            for c in range(n_tiles)]


def _diff_kernel(*refs, packed, n_heads, lam_init):
    n_kv = len(packed)
    q_ref = refs[0]
    kv_refs = refs[1:1 + 2 * n_kv]
    g_ref, lamp_ref, subg_ref, o_ref, s_ref = refs[1 + 2 * n_kv:]
    low = _lane_low_half()
    tq = q_ref.shape[1]
    first_head = pl.program_id(1) * (q_ref.shape[2] // LANES)
    lp = lamp_ref[...]
    lam = (jnp.exp(jnp.sum(lp[0:1] * lp[1:2], axis=-1, keepdims=True))
           - jnp.exp(jnp.sum(lp[2:3] * lp[3:4], axis=-1, keepdims=True)) + lam_init)
    lane_groups = range(0, q_ref.shape[2], LANES)
    q_blocks, kv_blocks = [], []
    for j in lane_groups:
        k_tiles, v_tiles = [], []
        for i in range(n_kv):
            if packed[i]:
                head = first_head + j // LANES
                k_tiles += _packed_tiles(kv_refs[2 * i], n_heads, head)
                v_tiles += _packed_tiles(kv_refs[2 * i + 1], n_heads, head)
            else:
                n_tiles = kv_refs[2 * i].shape[1] // KEY_TILE
                k_tiles += _tiles(kv_refs[2 * i], n_tiles, j)
                v_tiles += _tiles(kv_refs[2 * i + 1], n_tiles, j)
        for r in range(0, tq, ROW_BLK):
            q_blocks += list(_split_lane_halves(q_ref[0, r:r + ROW_BLK, j:j + LANES], low))
            kv_blocks += [(k_tiles, v_tiles)] * 2
    res = _attend(q_blocks, kv_blocks, s_ref)
    i = 0
    for j in lane_groups:
        for r in range(0, tq, ROW_BLK):
            (o1, l1), (o2, l2) = res[i], res[i + 1]
            i += 2
            o = o1 / l1 - o2 * (lam / l2)
            o = o * lax.rsqrt(jnp.mean(o * o, axis=-1, keepdims=True) + EPS) * subg_ref[...] * (1.0 - lam_init)
            gate = _silu(g_ref[0, r:r + ROW_BLK, j:j + LANES].astype(F32))
            o_ref[0, r:r + ROW_BLK, j:j + LANES] = (o * gate).astype(o_ref.dtype)


def _diff_attn(q, kv_list, g, lam_params, sub_g, lam_init, tq, heads_per_step):
    b, t, w = q.shape
    wb = heads_per_step * LANES
    qspec = pl.BlockSpec((1, tq, wb), lambda bi, h, qi: (bi, qi, h))
    n_heads = w // LANES
    in_specs = [qspec]
    args = [q]
    n_keys = 0
    for k, v, packed in kv_list:
        if packed:
            spec = pl.BlockSpec((1, k.shape[1], LANES), lambda bi, h, qi: (bi, 0, 0))
            n_keys += k.shape[1] // n_heads
        else:
            spec = pl.BlockSpec((1, k.shape[1], wb), lambda bi, h, qi: (bi, 0, h))
            n_keys += k.shape[1]
        in_specs += [spec, spec]
        args += [k, v]
    in_specs += [qspec,
                 pl.BlockSpec(lam_params.shape, lambda bi, h, qi: (0, 0)),
                 pl.BlockSpec((1, LANES), lambda bi, h, qi: (0, 0))]
    args += [g, lam_params, sub_g.reshape(1, LANES)]
    return pl.pallas_call(
        functools.partial(_diff_kernel, packed=tuple(p for _, _, p in kv_list), n_heads=n_heads,
                          lam_init=lam_init),
        grid=(b, w // wb, t // tq),
        in_specs=in_specs,
        out_specs=qspec,
        out_shape=jax.ShapeDtypeStruct((b, t, w), BF16),
        scratch_shapes=[pltpu.VMEM((heads_per_step * 2 * tq, n_keys), F32)],
        compiler_params=_params("parallel", "parallel", "parallel"),
        name="diff_attn",
    )(*args)


def _rope_tables(t_len):
    m = DIFF_HEAD_DIM // 4
    t = jnp.arange(t_len)
    rows = (t // GRID_W).astype(F32)
    cols = (t % GRID_W).astype(F32)
    inv = ROPE_BASE ** (-jnp.arange(m, dtype=F32) / m)
    ang_r = rows[:, None] * inv[None, :]
    ang_c = cols[:, None] * inv[None, :]
    cos = jnp.concatenate([jnp.cos(ang_r)] * 2 + [jnp.cos(ang_c)] * 2, axis=1)
    sin = jnp.concatenate([-jnp.sin(ang_r), jnp.sin(ang_r), -jnp.sin(ang_c), jnp.sin(ang_c)], axis=1)
    reps = LANES // DIFF_HEAD_DIM
    return jnp.tile(cos, (1, reps)), jnp.tile(sin, (1, reps))


def kernel(x_prompt, x_sample, c, cache_na_k, cache_na_v, state_lru, cache_diff_k, cache_diff_v, c_ctx,
           e_norm, e_ada_w, e_ada_b, e_w_in, e_rpb, e_conv_w, e_conv_b, e_lru_wa, e_lru_ba, e_lru_wx, e_lru_bx,
           e_lru_lam, e_w_out, o_norm, o_ada_w, o_ada_b, o_w_in, o_lq1, o_lk1, o_lq2, o_lk2, o_sub_g, o_w_out,
           final_norm):
    bp, tp, d = x_prompt.shape
    bs, ts, _ = x_sample.shape
    past = cache_na_k.shape[2]
    pad_rows = 2 * SUBLANES - bs - 1
    cc = jnp.concatenate([c, c_ctx[None, :], jnp.zeros((pad_rows, d), F32)], axis=0)

    xp, xs = x_prompt, x_sample
    na_k, na_v, lru_s, df_k, df_v = [], [], [], [], []
    for i in range(DEPTH):
        j = i // 2
        last = i == DEPTH - 1
        fin = final_norm if last else None
        if i % 2 == 0:
            m = _modulation(cc, e_ada_w[j], e_ada_b[j])
            mod_s = m[:bs].reshape(bs, 1, 3 * d)
            mod_p = m[bs:bs + 1].reshape(1, 1, 3 * d)
            wg = _lru_gate_weights(e_lru_wa[j], e_lru_ba[j], e_lru_wx[j], e_lru_bx[j])
            pair_tiles = _na_bias_tiles(e_rpb[j])
            nw, lw = NA_WIDTH, LRU_WIDTH
            qmul = NA_HEAD_DIM ** -0.5 * LOG2E

            def even_outs(kv_dtype):
                return [(nw, BF16, False, qmul, False), (nw, kv_dtype, False, None, False),
                        (nw, kv_dtype, False, None, False), (nw, BF16, False, None, False),
                        (lw, F32, False, None, False), (lw, BF16, False, None, False)]

            q, k, v, ga, xb, gb = _inproj(xp, mod_p, e_norm[j], e_w_in[j], even_outs(F32))
            oa = _ctx_attn(q, k, v, ga)
            zeros = jnp.zeros((bp, lw), F32)
            ob, ff, fb = _lru_branch(xb, gb, zeros, zeros, e_conv_w[j], e_conv_b[j], wg, e_lru_lam[j])
            xp = _outproj([oa, ob], e_w_out[j], xp, mod_p, fin)
            na_k.append(k.reshape(bp, tp, NA_HEADS, NA_HEAD_DIM))
            na_v.append(v.reshape(bp, tp, NA_HEADS, NA_HEAD_DIM))
            lru_s.append(jnp.stack([ff, fb], axis=1))

            q, k, v, ga, xb, gb = _inproj(xs, mod_s, e_norm[j], e_w_in[j], even_outs(BF16))
            ck = cache_na_k[:, j].reshape(bs, past, nw)
            cv = cache_na_v[:, j].reshape(bs, past, nw)
            oa = _na_attn(q, k, v, ck, cv, pair_tiles, ga)
            ob, _, _ = _lru_branch(xb, gb, state_lru[:, j, 0], state_lru[:, j, 1],
                                   e_conv_w[j], e_conv_b[j], wg, e_lru_lam[j])
            xs = _outproj([oa, ob], e_w_out[j], xs, mod_s, fin)
        else:
            lam_init = 0.8 - 0.6 * math.exp(-0.3 * i)
            m = _modulation(cc, o_ada_w[j], o_ada_b[j])
            mod_s = m[:bs].reshape(bs, 1, 3 * d)
            mod_p = m[bs:bs + 1].reshape(1, 1, 3 * d)
            lam_params = jnp.stack([o_lq1[j], o_lk1[j], o_lq2[j], o_lk2[j]], axis=0)
            dw = DIFF_WIDTH
            qmul = DIFF_HEAD_DIM ** -0.5 * LOG2E

            q, k, v, g = _inproj(
                xp, mod_p, o_norm[j], o_w_in[j],
                [(dw, BF16, False, qmul, False), (dw, F32, False, None, True), (dw, F32, False, None, True),
                 (dw, BF16, False, None, False)])
            o = _diff_attn(q, [(k, v, True)], g, lam_params, o_sub_g[j], lam_init, tq=tp,
                           heads_per_step=DIFF_HEADS)
            xp = _outproj([o], o_w_out[j], xp, mod_p, fin)
            df_k.append(k.reshape(bp, tp, DIFF_HEADS, 2 * DIFF_HEAD_DIM))
            df_v.append(v.reshape(bp, tp, DIFF_HEADS, 2 * DIFF_HEAD_DIM))

            q, k, v, g = _inproj(
                xs, mod_s, o_norm[j], o_w_in[j],
                [(dw, BF16, True, qmul, False), (dw, BF16, True, None, False), (dw, BF16, False, None, False),
                 (dw, BF16, False, None, False)],
                rope_tables=_rope_tables(ts))
            ck = cache_diff_k[:, j].reshape(bs, past * DIFF_HEADS, 2 * DIFF_HEAD_DIM)
            cv = cache_diff_v[:, j].reshape(bs, past * DIFF_HEADS, 2 * DIFF_HEAD_DIM)
            o = _diff_attn(q, [(ck, cv, True), (k, v, False)], g, lam_params, o_sub_g[j], lam_init, tq=ts,
                           heads_per_step=1)
            xs = _outproj([o], o_w_out[j], xs, mod_s, fin)

    return (xp, xs, jnp.stack(na_k, axis=1), jnp.stack(na_v, axis=1), jnp.stack(lru_s, axis=1),
            jnp.stack(df_k, axis=1), jnp.stack(df_v, axis=1))
```

```python
import functools
import math

import jax
import jax.numpy as jnp
from jax import lax
from jax.experimental import pallas as pl
from jax.experimental.pallas import tpu as pltpu

D_MODEL = 1024
DEPTH = 2
GRID_W = 64
NA_HEADS = 8
NA_HEAD_DIM = 64
NA_WIDTH = NA_HEADS * NA_HEAD_DIM
NA_ROWS_MAX = 8
NA_COLS = 16
RPB_ROWS = 2 * NA_ROWS_MAX - 1
RPB_COLS = 2 * NA_COLS - 1
LRU_WIDTH = 512
LRU_BLOCKS = 8
LRU_BLOCK = LRU_WIDTH // LRU_BLOCKS
LRU_C = 8.0
DIFF_HEADS = 8
DIFF_HEAD_DIM = 64
DIFF_WIDTH = DIFF_HEADS * 2 * DIFF_HEAD_DIM
ROPE_BASE = 10000.0
EPS = 1e-6
NEG_INF = -1e30
LOG2E = math.log2(math.e)

LANES = 128
SUBLANES = 8
MXU_DIM = 256
VMEM_LIMIT = 56 * 1024 * 1024

CAST_COLS = 512
LRU_SEGMENTS = 4
LRU_UNROLL = 8
LRU_CHUNK = 128
CONV_PAD_L = 2
CONV_PAD_R = 1
ROW_BLK = MXU_DIM
KEY_TILE = MXU_DIM

NA_Q_ROWS = 4
NA_K_ROWS = 12
NA_PAIR_TILES = 16

F32 = jnp.float32
BF16 = jnp.bfloat16


def _params(*semantics):
    return pltpu.CompilerParams(dimension_semantics=semantics, vmem_limit_bytes=VMEM_LIMIT)


def _sigmoid(x):
    return 0.5 * jnp.tanh(0.5 * x) + 0.5


def _silu(x):
    return x * _sigmoid(x)


def _dot(a, b):
    return jnp.dot(a, b, preferred_element_type=F32)


def _dot_nt(a, b):
    return lax.dot_general(a, b, (((1,), (1,)), ((), ())), preferred_element_type=F32)


def _mod_kernel(c_ref, w_ref, b_ref, o_ref):
    s = _silu(c_ref[...])
    o_ref[...] = _dot(s.astype(BF16), w_ref[...].astype(BF16)) + b_ref[...]


def _modulation(cc, w, b):
    rows, d = cc.shape
    n = w.shape[1]
    tn = 1024
    return pl.pallas_call(
        _mod_kernel,
        grid=(n // tn,),
        in_specs=[pl.BlockSpec((rows, d), lambda j: (0, 0)),
                  pl.BlockSpec((d, tn), lambda j: (0, j)),
                  pl.BlockSpec((1, tn), lambda j: (0, j))],
        out_specs=pl.BlockSpec((rows, tn), lambda j: (0, j)),
        out_shape=jax.ShapeDtypeStruct((rows, n), F32),
        compiler_params=_params("arbitrary"),
        name="modulation",
    )(cc, w, b.reshape(1, n))


def _rope_block(x, cos, sin_signed, low16):
    partner = jnp.where(low16, pltpu.roll(x, LANES - 16, axis=1), pltpu.roll(x, 16, axis=1))
    return x * cos + partner * sin_signed


def _cast_weights_once(w_ref, wbf_ref):
    @pl.when((pl.program_id(0) == 0) & (pl.program_id(1) == 0))
    def _():
        for off in range(0, w_ref.shape[1], CAST_COLS):
            wbf_ref[:, off:off + CAST_COLS] = w_ref[:, off:off + CAST_COLS].astype(BF16)


def _inproj_kernel(*refs, outs, rope):
    if rope:
        x_ref, mod_ref, g_ref, w_ref, cos_ref, sin_ref = refs[:6]
        out_refs = refs[6:-1]
    else:
        x_ref, mod_ref, g_ref, w_ref = refs[:4]
        out_refs = refs[4:-1]
    wbf_ref = refs[-1]
    _cast_weights_once(w_ref, wbf_ref)
    d = x_ref.shape[-1]
    tm = x_ref.shape[1]
    x = x_ref[0]
    y = x * lax.rsqrt(jnp.mean(x * x, axis=-1, keepdims=True) + EPS) * g_ref[...]
    mod = mod_ref[0]
    h = (y * (1.0 + mod[:, d:2 * d]) + mod[:, :d]).astype(BF16)
    if rope:
        cos = cos_ref[...]
        sin = sin_ref[...]
        lane = lax.broadcasted_iota(jnp.int32, (1, LANES), 1)
        low16 = (lane & 16) == 0
    off = 0
    for o_ref, (width, roped, mul, packed) in zip(out_refs, outs):
        acc = _dot(h, wbf_ref[:, off:off + width])
        if roped:
            acc = jnp.concatenate(
                [_rope_block(acc[:, j:j + LANES], cos, sin, low16) for j in range(0, width, LANES)], axis=1)
        if mul is not None:
            acc = acc * mul
        if packed:
            n_heads = width // LANES
            for hh in range(n_heads):
                o_ref[0, pl.ds(hh, tm, stride=n_heads), :] = acc[:, hh * LANES:(hh + 1) * LANES].astype(o_ref.dtype)
        else:
            o_ref[0] = acc.astype(o_ref.dtype)
        off += width


def _inproj(x, mod, norm_g, w, outs, rope_tables=None, tm=512):
    b, t, d = x.shape
    tm = min(tm, t)
    n = w.shape[1]
    per_batch = mod.shape[0] > 1
    mod_map = (lambda i, j: (i, 0, 0)) if per_batch else (lambda i, j: (0, 0, 0))
    in_specs = [pl.BlockSpec((1, tm, d), lambda i, j: (i, j, 0)),
                pl.BlockSpec((1, 1, 3 * d), mod_map),
                pl.BlockSpec((1, d), lambda i, j: (0, 0)),
                pl.BlockSpec((d, n), lambda i, j: (0, 0), pipeline_mode=pl.Buffered(1))]
    args = [x, mod, norm_g.reshape(1, d), w]
    rope = rope_tables is not None
    if rope:
        in_specs += [pl.BlockSpec((tm, LANES), lambda i, j: (j, 0))] * 2
        args += list(rope_tables)
    out_specs, out_shape = [], []
    for width, dt, _, _, packed in outs:
        if packed:
            heads = width // LANES
            out_specs.append(pl.BlockSpec((1, tm * heads, LANES), lambda i, j: (i, j, 0)))
            out_shape.append(jax.ShapeDtypeStruct((b, t * heads, LANES), dt))
        else:
            out_specs.append(pl.BlockSpec((1, tm, width), lambda i, j: (i, j, 0)))
            out_shape.append(jax.ShapeDtypeStruct((b, t, width), dt))
    kern = functools.partial(_inproj_kernel, outs=tuple((w_, r, m, p) for w_, _, r, m, p in outs), rope=rope)
    return pl.pallas_call(
        kern,
        grid=(b, t // tm),
        in_specs=in_specs,
        out_specs=out_specs,
        out_shape=out_shape,
        scratch_shapes=[pltpu.VMEM((d, n), BF16)],
        compiler_params=_params("arbitrary", "arbitrary"),
        name="inproj",
    )(*args)


def _outproj_kernel(*refs, n_in, final):
    o_refs = refs[:n_in]
    w_ref, x_ref, mod_ref = refs[n_in:n_in + 3]
    fin_ref = refs[n_in + 3] if final else None
    out_ref, wbf_ref = refs[-2:]
    _cast_weights_once(w_ref, wbf_ref)
    d = x_ref.shape[-1]
    acc, row = None, 0
    for o_ref in o_refs:
        k = o_ref.shape[-1]
        part = _dot(o_ref[0], wbf_ref[row:row + k, :])
        acc = part if acc is None else acc + part
        row += k
    y = x_ref[0] + mod_ref[0][:, 2 * d:] * acc
    if final:
        y = y * lax.rsqrt(jnp.mean(y * y, axis=-1, keepdims=True) + EPS) * fin_ref[...]
    out_ref[0] = y


def _outproj(o_list, w, x, mod, final_g=None, tm=512):
    b, t, d = x.shape
    tm = min(tm, t)
    per_batch = mod.shape[0] > 1
    mod_map = (lambda i, j: (i, 0, 0)) if per_batch else (lambda i, j: (0, 0, 0))
    in_specs = [pl.BlockSpec((1, tm, o.shape[-1]), lambda i, j: (i, j, 0)) for o in o_list]
    in_specs += [pl.BlockSpec(w.shape, lambda i, j: (0, 0), pipeline_mode=pl.Buffered(1)),
                 pl.BlockSpec((1, tm, d), lambda i, j: (i, j, 0)),
                 pl.BlockSpec((1, 1, 3 * d), mod_map)]
    args = list(o_list) + [w, x, mod]
    final = final_g is not None
    if final:
        in_specs.append(pl.BlockSpec((1, d), lambda i, j: (0, 0)))
        args.append(final_g.reshape(1, d))
    kern = functools.partial(_outproj_kernel, n_in=len(o_list), final=final)
    return pl.pallas_call(
        kern,
        grid=(b, t // tm),
        in_specs=in_specs,
        out_specs=pl.BlockSpec((1, tm, d), lambda i, j: (i, j, 0)),
        out_shape=jax.ShapeDtypeStruct((b, t, d), F32),
        scratch_shapes=[pltpu.VMEM(w.shape, BF16)],
        compiler_params=_params("arbitrary", "arbitrary"),
        name="outproj",
    )(*args)


def _attend(q_blocks, kv_blocks, s_ref, score_fn=None):
    n_rb = len(q_blocks)

    def tile_of(rb, c):
        return (slice(rb * ROW_BLK, (rb + 1) * ROW_BLK), slice(c * KEY_TILE, (c + 1) * KEY_TILE))

    def score_tile(rb, c):
        s_ref[tile_of(rb, c)] = _dot_nt(q_blocks[rb], kv_blocks[rb][0][c])

    def row_max(rb):
        m_acc = None
        for c in range(len(kv_blocks[rb][0])):
            s = s_ref[tile_of(rb, c)]
            adjusted = None if score_fn is None else score_fn(rb, c, s)
            if adjusted is not None:
                s = adjusted
                s_ref[tile_of(rb, c)] = s
            mt = jnp.maximum(s[:, :LANES], s[:, LANES:])
            m_acc = mt if m_acc is None else jnp.maximum(m_acc, mt)
        return m_acc.max(axis=-1, keepdims=True)

    for c in range(len(kv_blocks[0][0])):
        score_tile(0, c)
    outs = []
    for rb in range(n_rb):
        m = row_max(rb)
        n_tiles = len(kv_blocks[rb][0])
        n_next = len(kv_blocks[rb + 1][0]) if rb + 1 < n_rb else 0
        acc = None
        for c in range(max(n_tiles, n_next)):
            if c < n_next:
                score_tile(rb + 1, c)
            if c < n_tiles:
                e = jnp.exp2(s_ref[tile_of(rb, c)] - m)
                pv = _dot(e.astype(BF16), kv_blocks[rb][1][c])
                acc = pv if acc is None else acc + pv
        outs.append(acc)
    return outs


def _lane_low_half():
    return lax.broadcasted_iota(jnp.int32, (1, LANES), 1) < (LANES // 2)


def _values_and_ones(v_tiles):
    ones = jnp.ones((KEY_TILE, LANES), BF16)
    return [jnp.concatenate([v, ones], axis=1) for v in v_tiles]


def _normalise_pair(acc_lo, acc_hi, low):
    return jnp.where(low, acc_lo[:, :LANES] / acc_lo[:, LANES:], acc_hi[:, :LANES] / acc_hi[:, LANES:])


def _split_lane_halves(q, low):
    zero = jnp.zeros_like(q)
    return jnp.where(low, q, zero), jnp.where(low, zero, q)


def _tiles(ref, n_tiles, j):
    return [ref[0, c * KEY_TILE:(c + 1) * KEY_TILE, j:j + LANES].astype(BF16) for c in range(n_tiles)]


def _ctx_attn_kernel(q_ref, k_ref, v_ref, g_ref, o_ref, s_ref):
    low = _lane_low_half()
    tq = q_ref.shape[1]
    n_tiles = k_ref.shape[1] // KEY_TILE
    lane_groups = range(0, q_ref.shape[2], LANES)
    q_blocks, kv_blocks = [], []
    for j in lane_groups:
        kv = (_tiles(k_ref, n_tiles, j), _values_and_ones(_tiles(v_ref, n_tiles, j)))
        for r in range(0, tq, ROW_BLK):
            q_blocks += list(_split_lane_halves(q_ref[0, r:r + ROW_BLK, j:j + LANES], low))
            kv_blocks += [kv, kv]
    res = _attend(q_blocks, kv_blocks, s_ref)
    i = 0
    for j in lane_groups:
        for r in range(0, tq, ROW_BLK):
            o = _normalise_pair(res[i], res[i + 1], low)
            i += 2
            gate = _silu(g_ref[0, r:r + ROW_BLK, j:j + LANES].astype(F32))
            o_ref[0, r:r + ROW_BLK, j:j + LANES] = (o * gate).astype(o_ref.dtype)


def _ctx_attn(q, k, v, g):
    b, t, w = q.shape
    spec = pl.BlockSpec((1, t, w), lambda i: (i, 0, 0))
    return pl.pallas_call(
        _ctx_attn_kernel,
        grid=(b,),
        in_specs=[spec] * 4,
        out_specs=spec,
        out_shape=jax.ShapeDtypeStruct((b, t, w), BF16),
        scratch_shapes=[pltpu.VMEM((2 * t * (w // LANES), k.shape[1]), F32)],
        compiler_params=_params("parallel"),
        name="ctx_attn",
    )(q, k, v, g)


def _na_bias_kernel(rpb_ref, o_ref):
    right = lax.broadcasted_iota(jnp.int32, (GRID_W, LANES), 1) >= GRID_W
    zero_row = jnp.zeros((1, LANES), F32)

    def toeplitz(row, lane0):
        shift = (lane0 - (NA_COLS - 1)) % LANES
        return pltpu.roll(jnp.broadcast_to(row, (GRID_W, LANES)), shift, axis=1, stride=1, stride_axis=0)

    for p in range(NA_PAIR_TILES):
        left_row = rpb_ref[0, p - 1:p, :] if p >= 1 else zero_row
        right_row = rpb_ref[0, p:p + 1, :]
        o_ref[0, p] = jnp.where(right, toeplitz(right_row, GRID_W), toeplitz(left_row, 0)) * LOG2E


def _na_bias_tiles(rpb):
    h, rows, cols = rpb.shape
    padded = jnp.pad(rpb, ((0, 0), (0, NA_PAIR_TILES - rows), (0, LANES - cols)))
    return pl.pallas_call(
        _na_bias_kernel,
        grid=(h,),
        in_specs=[pl.BlockSpec((1, NA_PAIR_TILES, LANES), lambda i: (i, 0, 0))],
        out_specs=pl.BlockSpec((1, NA_PAIR_TILES, GRID_W, LANES), lambda i: (i, 0, 0, 0)),
        out_shape=jax.ShapeDtypeStruct((h, NA_PAIR_TILES, GRID_W, LANES), F32),
        compiler_params=_params("arbitrary"),
        name="na_bias_tiles",
    )(padded)


def _na_row_start(r, n_rows):
    kr = min(NA_ROWS_MAX, n_rows)
    return min(max(r - kr // 2, 0), n_rows - kr)


def _na_window(grp, n_rows):
    kr = min(NA_ROWS_MAX, n_rows)
    starts = [_na_row_start(grp * NA_Q_ROWS + i, n_rows) for i in range(NA_Q_ROWS)]
    rows_per_tile = KEY_TILE // GRID_W
    need = -(-(max(starts) + kr - min(starts)) // rows_per_tile) * rows_per_tile
    return min(min(starts), n_rows - need), need


def _na_kernel(q_ref, k_ref, v_ref, ck_ref, cv_ref, pt_ref, g_ref, o_ref, badd_ref, bcap_ref, s_ref, *, n_rows):
    kr = min(NA_ROWS_MAX, n_rows)
    n_grp = n_rows // NA_Q_ROWS
    low = _lane_low_half()

    @pl.when(pl.program_id(1) == 0)
    def _build_bias():
        qcol = lax.broadcasted_iota(jnp.int32, (GRID_W, LANES), 0)
        lane = lax.broadcasted_iota(jnp.int32, (GRID_W, LANES), 1)
        right = lane >= GRID_W
        kcol = lane & (GRID_W - 1)
        col_start = jnp.clip(qcol - NA_COLS // 2, 0, GRID_W - NA_COLS)
        col_ok = (kcol >= col_start) & (kcol < col_start + NA_COLS)
        for grp in range(n_grp):
            krow0, key_rows = _na_window(grp, n_rows)
            for i in range(NA_Q_ROWS):
                r = grp * NA_Q_ROWS + i
                row_start = _na_row_start(r, n_rows)
                for m in range(key_rows // 2):
                    key_row = krow0 + 2 * m
                    ok_l = row_start <= key_row < row_start + kr
                    ok_r = row_start <= key_row + 1 < row_start + kr
                    rows = slice(i * GRID_W, (i + 1) * GRID_W)
                    cols = slice(m * LANES, (m + 1) * LANES)
                    if not (ok_l or ok_r):
                        bcap_ref[grp, rows, cols] = jnp.full((GRID_W, LANES), NEG_INF, F32)
                        for half in range(2):
                            badd_ref[2 * grp + half, rows, cols] = jnp.zeros((GRID_W, LANES), F32)
                        continue
                    ok = col_ok
                    if not ok_r:
                        ok = ok & jnp.logical_not(right)
                    if not ok_l:
                        ok = ok & right
                    bcap_ref[grp, rows, cols] = jnp.where(ok, jnp.inf, NEG_INF)
                    p = key_row - r + NA_ROWS_MAX
                    for half in range(2):
                        badd_ref[2 * grp + half, rows, cols] = jnp.where(ok, pt_ref[half, p], 0.0)

    n_ctx = ck_ref.shape[1] // KEY_TILE
    ctx_k = _tiles(ck_ref, n_ctx, 0)
    ctx_v = _tiles(cv_ref, n_ctx, 0)
    q_blocks, kv_blocks, n_win = [], [], []
    for grp in range(n_grp):
        krow0, key_rows = _na_window(grp, n_rows)
        k0 = krow0 * GRID_W
        win = [slice(k0 + c * KEY_TILE, k0 + (c + 1) * KEY_TILE) for c in range(key_rows * GRID_W // KEY_TILE)]
        kv = ([k_ref[0, w, :] for w in win] + ctx_k, _values_and_ones([v_ref[0, w, :] for w in win] + ctx_v))
        q_blocks += list(_split_lane_halves(q_ref[0, grp * ROW_BLK:(grp + 1) * ROW_BLK, :], low))
        kv_blocks += [kv, kv]
        n_win += [len(win)] * 2

    def masked_bias(rb, c, s):
        if c >= n_win[rb]:
            return None
        cols = slice(c * KEY_TILE, (c + 1) * KEY_TILE)
        return jnp.minimum(s + badd_ref[rb, :, cols], bcap_ref[rb // 2, :, cols])

    res = _attend(q_blocks, kv_blocks, s_ref, masked_bias)
    for grp in range(n_grp):
        o = _normalise_pair(res[2 * grp], res[2 * grp + 1], low)
        rows = slice(grp * ROW_BLK, (grp + 1) * ROW_BLK)
        o_ref[0, rows, :] = (o * _silu(g_ref[0, rows, :].astype(F32))).astype(o_ref.dtype)


def _na_attn(q, k, v, ck, cv, pair_tiles, g):
    b, t, w = q.shape
    c = ck.shape[1]
    n_rows = t // GRID_W
    assert NA_Q_ROWS * GRID_W == ROW_BLK and n_rows % NA_Q_ROWS == 0 and n_rows >= NA_K_ROWS
    n_grp = n_rows // NA_Q_ROWS
    n_win = NA_K_ROWS * GRID_W
    tspec = pl.BlockSpec((1, t, LANES), lambda hp, bi: (bi, 0, hp))
    cspec = pl.BlockSpec((1, c, LANES), lambda hp, bi: (bi, 0, hp))
    ptspec = pl.BlockSpec((2, NA_PAIR_TILES, GRID_W, LANES), lambda hp, bi: (hp, 0, 0, 0))
    return pl.pallas_call(
        functools.partial(_na_kernel, n_rows=n_rows),
        grid=(w // LANES, b),
        in_specs=[tspec, tspec, tspec, cspec, cspec, ptspec, tspec],
        out_specs=tspec,
        out_shape=jax.ShapeDtypeStruct((b, t, w), BF16),
        scratch_shapes=[pltpu.VMEM((2 * n_grp, ROW_BLK, n_win), F32),
                        pltpu.VMEM((n_grp, ROW_BLK, n_win), F32),
                        pltpu.VMEM((2 * t, n_win + c), F32)],
        compiler_params=_params("parallel", "arbitrary"),
        name="na_attn",
    )(q, k, v, ck, cv, pair_tiles, g)


def _softplus(x):
    return jnp.maximum(x, 0.0) + jnp.log(1.0 + jnp.exp(-jnp.abs(x)))


def _lru_kernel(x_ref, gb_ref, h0f_ref, h0b_ref, cw_ref, cb_ref, wg_ref, lam_ref,
                o_ref, ff_ref, fb_ref, xt_s, af_s, uf_s, ab_s, ub_s, hf_s, hb_s):
    nb, t, _ = x_ref.shape
    pad = CONV_PAD_L * nb
    cw = cw_ref[...]
    cb = cb_ref[...]
    wg = wg_ref[0]
    kk = (-0.5 * LRU_C * LOG2E) * _softplus(-lam_ref[...])

    xt_s[0:pad, :] = jnp.zeros((pad, LANES), F32)
    xt_s[pad + t * nb:pad + (t + CONV_PAD_R) * nb, :] = jnp.zeros((CONV_PAD_R * nb, LANES), F32)

    def to_time_major(b, carry):
        xt_s[pl.ds(pad + b, t, stride=nb), :] = x_ref[b]
        return carry

    lax.fori_loop(0, nb, to_time_major, 0)

    rows_per_chunk = LRU_CHUNK * nb
    lane = lax.broadcasted_iota(jnp.int32, (rows_per_chunk, LANES), 1)
    bias_cols = jnp.where(lane < 2, 1.0, 0.0).astype(BF16)

    def gates(c, carry):
        r0 = pl.multiple_of(c * rows_per_chunk, rows_per_chunk)

        def tap(i):
            return xt_s[pl.ds(r0 + i * nb, rows_per_chunk), :]

        x = tap(2)
        xc = cw[0:1] * tap(0) + cw[1:2] * tap(1) + cw[2:3] * x + cw[3:4] * tap(3) + cb
        th = jnp.tanh(_dot(jnp.concatenate([xc.astype(BF16), bias_cols], axis=1), wg))
        xh = 0.5 * xc
        rows = pl.ds(r0, rows_per_chunk)
        for d, (a_s, u_s) in enumerate(((af_s, uf_s), (ab_s, ub_s))):
            t_a = th[:, (2 * d) * LANES:(2 * d + 1) * LANES]
            t_x = th[:, (2 * d + 1) * LANES:(2 * d + 2) * LANES]
            a = jnp.exp2(kk[d:d + 1] * t_a + kk[d:d + 1])
            y = 1.0 - a * a
            sq = y * lax.rsqrt(jnp.maximum(y, 1e-30))
            a_s[rows, :] = a
            u_s[rows, :] = sq * (xh * t_x + xh)
        return carry

    lax.fori_loop(0, t // LRU_CHUNK, gates, 0)

    ts = t // LRU_SEGMENTS
    refs = ((af_s, uf_s), (ab_s, ub_s))

    slab = nb * LRU_UNROLL

    def slab_rows(d, j, i):
        base = (j * ts + i * LRU_UNROLL) * nb
        return pl.ds(pl.multiple_of(base if d == 0 else t * nb - slab - base, slab), slab)

    def step_of(d, block, k):
        kk_ = k if d == 0 else LRU_UNROLL - 1 - k
        return block[kk_ * nb:(kk_ + 1) * nb]

    def compose(i, carry):
        state = list(carry)
        for d in range(2):
            for j in range(LRU_SEGMENTS - 1):
                n = d * (LRU_SEGMENTS - 1) + j
                a_blk = refs[d][0][slab_rows(d, j, i), :]
                u_blk = refs[d][1][slab_rows(d, j, i), :]
                g, big_a = state[n]
                for k in range(LRU_UNROLL):
                    a = step_of(d, a_blk, k)
                    g, big_a = a * g + step_of(d, u_blk, k), a * big_a
                state[n] = (g, big_a)
        return tuple(state)

    ones = jnp.ones((nb, LANES), F32)
    h0 = (h0f_ref[...], h0b_ref[...])
    init = tuple((h0[d] if j == 0 else jnp.zeros((nb, LANES), F32), ones)
                 for d in range(2) for j in range(LRU_SEGMENTS - 1))
    ends = lax.fori_loop(0, ts // LRU_UNROLL, compose, init)
    starts = []
    for d in range(2):
        h = h0[d]
        for j in range(LRU_SEGMENTS):
            starts.append(h)
            if j < LRU_SEGMENTS - 1:
                g, big_a = ends[d * (LRU_SEGMENTS - 1) + j]
                h = g if j == 0 else g + big_a * h

    def rerun(i, carry):
        state = list(carry)
        for d, h_s in enumerate((hf_s, hb_s)):
            for j in range(LRU_SEGMENTS):
                n = d * LRU_SEGMENTS + j
                a_blk = refs[d][0][slab_rows(d, j, i), :]
                u_blk = refs[d][1][slab_rows(d, j, i), :]
                hs = []
                for k in range(LRU_UNROLL):
                    state[n] = step_of(d, a_blk, k) * state[n] + step_of(d, u_blk, k)
                    hs.append(state[n])
                h_s[slab_rows(d, j, i), :] = jnp.concatenate(hs if d == 0 else hs[::-1], axis=0)
        return tuple(state)

    final = lax.fori_loop(0, ts // LRU_UNROLL, rerun, tuple(starts))
    ff_ref[...] = final[LRU_SEGMENTS - 1]
    fb_ref[...] = final[2 * LRU_SEGMENTS - 1]

    def add_directions(c, carry):
        rows = pl.ds(pl.multiple_of(c * rows_per_chunk, rows_per_chunk), rows_per_chunk)
        hf_s[rows, :] = hf_s[rows, :] + hb_s[rows, :]
        return carry

    lax.fori_loop(0, t // LRU_CHUNK, add_directions, 0)

    def emit(b, carry):
        hs = hf_s[pl.ds(b, t, stride=nb), :]
        o_ref[b] = (hs * _silu(gb_ref[b].astype(F32))).astype(o_ref.dtype)
        return carry

    lax.fori_loop(0, nb, emit, 0)


def _lru_branch(xb, gb, h0f, h0b, conv_w, conv_b, wg, lam):
    b, t, w = xb.shape
    nb = SUBLANES
    assert t % (LRU_SEGMENTS * LRU_UNROLL) == 0 and t % LRU_CHUNK == 0
    assert conv_w.shape[0] == CONV_PAD_L + 1 + CONV_PAD_R
    xspec = pl.BlockSpec((nb, t, LANES), lambda i, c: (i, 0, c))
    hspec = pl.BlockSpec((nb, LANES), lambda i, c: (i, c))
    scratch = [pltpu.VMEM(((t + CONV_PAD_L + CONV_PAD_R) * nb, LANES), F32)]
    scratch += [pltpu.VMEM((t * nb, LANES), F32) for _ in range(6)]
    return pl.pallas_call(
        _lru_kernel,
        grid=(b // nb, w // LANES),
        in_specs=[xspec, xspec, hspec, hspec,
                  pl.BlockSpec((conv_w.shape[0], LANES), lambda i, c: (0, c)),
                  pl.BlockSpec((1, LANES), lambda i, c: (0, c)),
                  pl.BlockSpec((1, 2 * LANES, 4 * LANES), lambda i, c: (c, 0, 0)),
                  pl.BlockSpec((2, LANES), lambda i, c: (0, c))],
        out_specs=[xspec, hspec, hspec],
        out_shape=[jax.ShapeDtypeStruct((b, t, w), BF16),
                   jax.ShapeDtypeStruct((b, w), F32),
                   jax.ShapeDtypeStruct((b, w), F32)],
        scratch_shapes=scratch,
        compiler_params=_params("parallel", "parallel"),
        name="lru_branch",
    )(xb, gb, h0f, h0b, conv_w, conv_b.reshape(1, w), wg, lam)


def _lru_gate_weights(wa, ba, wx, bx):
    def tiles(w):
        w2 = w.reshape(LRU_BLOCKS // 2, 2, LRU_BLOCK, LRU_BLOCK)
        z = jnp.zeros_like(w2[:, 0])
        return jnp.concatenate([jnp.concatenate([w2[:, 0], z], axis=2),
                                jnp.concatenate([z, w2[:, 1]], axis=2)], axis=1)
    wg = 0.5 * jnp.concatenate([tiles(wa[0]), tiles(wx[0]), tiles(wa[1]), tiles(wx[1])], axis=2)
    n_tiles = LRU_WIDTH // LANES
    bg = jnp.stack([ba[0], bx[0], ba[1], bx[1]], axis=0).reshape(4, n_tiles, LANES)
    bg = 0.5 * jnp.transpose(bg, (1, 0, 2)).reshape(n_tiles, 1, 4 * LANES)
    hi = bg.astype(BF16)
    lo = (bg - hi.astype(F32)).astype(BF16)
    rest = jnp.zeros((n_tiles, LANES - 2, 4 * LANES), BF16)
    return jnp.concatenate([wg.astype(BF16), hi, lo, rest], axis=1)


def _packed_tiles(ref, n_heads, head):
    n_tiles = ref.shape[1] // (n_heads * KEY_TILE)
    return [ref[0, pl.ds(c * KEY_TILE * n_heads + head, KEY_TILE, stride=n_heads), :].astype(BF16)
            for c in range(n_tiles)]


def _diff_kernel(*refs, packed, n_heads, lam_init):
    n_kv = len(packed)
    q_ref = refs[0]
    kv_refs = refs[1:1 + 2 * n_kv]
    g_ref, lamp_ref, subg_ref, o_ref, s_ref = refs[1 + 2 * n_kv:]
    low = _lane_low_half()
    tq = q_ref.shape[1]
    first_head = pl.program_id(1) * (q_ref.shape[2] // LANES)
    lp = lamp_ref[...]
    lam = (jnp.exp(jnp.sum(lp[0:1] * lp[1:2], axis=-1, keepdims=True))
           - jnp.exp(jnp.sum(lp[2:3] * lp[3:4], axis=-1, keepdims=True)) + lam_init)
    lane_groups = range(0, q_ref.shape[2], LANES)
    q_blocks, kv_blocks = [], []
    for j in lane_groups:
        k_tiles, v_tiles = [], []
        for i in range(n_kv):
            if packed[i]:
                head = first_head + j // LANES
                k_tiles += _packed_tiles(kv_refs[2 * i], n_heads, head)
                v_tiles += _packed_tiles(kv_refs[2 * i + 1], n_heads, head)
            else:
                n_tiles = kv_refs[2 * i].shape[1] // KEY_TILE
                k_tiles += _tiles(kv_refs[2 * i], n_tiles, j)
                v_tiles += _tiles(kv_refs[2 * i + 1], n_tiles, j)
        v_tiles = _values_and_ones(v_tiles)
        for r in range(0, tq, ROW_BLK):
            q_blocks += list(_split_lane_halves(q_ref[0, r:r + ROW_BLK, j:j + LANES], low))
            kv_blocks += [(k_tiles, v_tiles)] * 2
    res = _attend(q_blocks, kv_blocks, s_ref)
    i = 0
    for j in lane_groups:
        for r in range(0, tq, ROW_BLK):
            acc1, acc2 = res[i], res[i + 1]
            i += 2
            o = acc1[:, :LANES] / acc1[:, LANES:] - acc2[:, :LANES] * (lam / acc2[:, LANES:])
            o = o * lax.rsqrt(jnp.mean(o * o, axis=-1, keepdims=True) + EPS) * subg_ref[...] * (1.0 - lam_init)
            gate = _silu(g_ref[0, r:r + ROW_BLK, j:j + LANES].astype(F32))
            o_ref[0, r:r + ROW_BLK, j:j + LANES] = (o * gate).astype(o_ref.dtype)


def _diff_attn(q, kv_list, g, lam_params, sub_g, lam_init, tq, heads_per_step):
    b, t, w = q.shape
    wb = heads_per_step * LANES
    qspec = pl.BlockSpec((1, tq, wb), lambda bi, h, qi: (bi, qi, h))
    n_heads = w // LANES
    in_specs = [qspec]
    args = [q]
    n_keys = 0
    for k, v, packed in kv_list:
        if packed:
            spec = pl.BlockSpec((1, k.shape[1], LANES), lambda bi, h, qi: (bi, 0, 0))
            n_keys += k.shape[1] // n_heads
        else:
            spec = pl.BlockSpec((1, k.shape[1], wb), lambda bi, h, qi: (bi, 0, h))
            n_keys += k.shape[1]
        in_specs += [spec, spec]
        args += [k, v]
    in_specs += [qspec,
                 pl.BlockSpec(lam_params.shape, lambda bi, h, qi: (0, 0)),
                 pl.BlockSpec((1, LANES), lambda bi, h, qi: (0, 0))]
    args += [g, lam_params, sub_g.reshape(1, LANES)]
    return pl.pallas_call(
        functools.partial(_diff_kernel, packed=tuple(p for _, _, p in kv_list), n_heads=n_heads,
                          lam_init=lam_init),
        grid=(b, w // wb, t // tq),
        in_specs=in_specs,
        out_specs=qspec,
        out_shape=jax.ShapeDtypeStruct((b, t, w), BF16),
        scratch_shapes=[pltpu.VMEM((heads_per_step * 2 * tq, n_keys), F32)],
        compiler_params=_params("parallel", "parallel", "parallel"),
        name="diff_attn",
    )(*args)


def _rope_tables(t_len):
    m = DIFF_HEAD_DIM // 4
    t = jnp.arange(t_len)
    rows = (t // GRID_W).astype(F32)
    cols = (t % GRID_W).astype(F32)
    inv = ROPE_BASE ** (-jnp.arange(m, dtype=F32) / m)
    ang_r = rows[:, None] * inv[None, :]
    ang_c = cols[:, None] * inv[None, :]
    cos = jnp.concatenate([jnp.cos(ang_r)] * 2 + [jnp.cos(ang_c)] * 2, axis=1)
    sin = jnp.concatenate([-jnp.sin(ang_r), jnp.sin(ang_r), -jnp.sin(ang_c), jnp.sin(ang_c)], axis=1)
    reps = LANES // DIFF_HEAD_DIM
    return jnp.tile(cos, (1, reps)), jnp.tile(sin, (1, reps))


def kernel(x_prompt, x_sample, c, cache_na_k, cache_na_v, state_lru, cache_diff_k, cache_diff_v, c_ctx,
           e_norm, e_ada_w, e_ada_b, e_w_in, e_rpb, e_conv_w, e_conv_b, e_lru_wa, e_lru_ba, e_lru_wx, e_lru_bx,
           e_lru_lam, e_w_out, o_norm, o_ada_w, o_ada_b, o_w_in, o_lq1, o_lk1, o_lq2, o_lk2, o_sub_g, o_w_out,
           final_norm):
    bp, tp, d = x_prompt.shape
    bs, ts, _ = x_sample.shape
    past = cache_na_k.shape[2]
    pad_rows = 2 * SUBLANES - bs - 1
    cc = jnp.concatenate([c, c_ctx[None, :], jnp.zeros((pad_rows, d), F32)], axis=0)

    xp, xs = x_prompt, x_sample
    na_k, na_v, lru_s, df_k, df_v = [], [], [], [], []
    for i in range(DEPTH):
        j = i // 2
        last = i == DEPTH - 1
        fin = final_norm if last else None
        if i % 2 == 0:
            m = _modulation(cc, e_ada_w[j], e_ada_b[j])
            mod_s = m[:bs].reshape(bs, 1, 3 * d)
            mod_p = m[bs:bs + 1].reshape(1, 1, 3 * d)
            wg = _lru_gate_weights(e_lru_wa[j], e_lru_ba[j], e_lru_wx[j], e_lru_bx[j])
            pair_tiles = _na_bias_tiles(e_rpb[j])
            nw, lw = NA_WIDTH, LRU_WIDTH
            qmul = NA_HEAD_DIM ** -0.5 * LOG2E

            def even_outs(kv_dtype):
                return [(nw, BF16, False, qmul, False), (nw, kv_dtype, False, None, False),
                        (nw, kv_dtype, False, None, False), (nw, BF16, False, None, False),
                        (lw, F32, False, None, False), (lw, BF16, False, None, False)]

            q, k, v, ga, xb, gb = _inproj(xp, mod_p, e_norm[j], e_w_in[j], even_outs(F32))
            oa = _ctx_attn(q, k, v, ga)
            zeros = jnp.zeros((bp, lw), F32)
            ob, ff, fb = _lru_branch(xb, gb, zeros, zeros, e_conv_w[j], e_conv_b[j], wg, e_lru_lam[j])
            xp = _outproj([oa, ob], e_w_out[j], xp, mod_p, fin)
            na_k.append(k.reshape(bp, tp, NA_HEADS, NA_HEAD_DIM))
            na_v.append(v.reshape(bp, tp, NA_HEADS, NA_HEAD_DIM))
            lru_s.append(jnp.stack([ff, fb], axis=1))

            q, k, v, ga, xb, gb = _inproj(xs, mod_s, e_norm[j], e_w_in[j], even_outs(BF16))
            ck = cache_na_k[:, j].reshape(bs, past, nw)
            cv = cache_na_v[:, j].reshape(bs, past, nw)
            oa = _na_attn(q, k, v, ck, cv, pair_tiles, ga)
            ob, _, _ = _lru_branch(xb, gb, state_lru[:, j, 0], state_lru[:, j, 1],
                                   e_conv_w[j], e_conv_b[j], wg, e_lru_lam[j])
            xs = _outproj([oa, ob], e_w_out[j], xs, mod_s, fin)
        else:
            lam_init = 0.8 - 0.6 * math.exp(-0.3 * i)
            m = _modulation(cc, o_ada_w[j], o_ada_b[j])
            mod_s = m[:bs].reshape(bs, 1, 3 * d)
            mod_p = m[bs:bs + 1].reshape(1, 1, 3 * d)
            lam_params = jnp.stack([o_lq1[j], o_lk1[j], o_lq2[j], o_lk2[j]], axis=0)
            dw = DIFF_WIDTH
            qmul = DIFF_HEAD_DIM ** -0.5 * LOG2E

            q, k, v, g = _inproj(
                xp, mod_p, o_norm[j], o_w_in[j],
                [(dw, BF16, False, qmul, False), (dw, F32, False, None, True), (dw, F32, False, None, True),
                 (dw, BF16, False, None, False)])
            o = _diff_attn(q, [(k, v, True)], g, lam_params, o_sub_g[j], lam_init, tq=tp,
                           heads_per_step=DIFF_HEADS)
            xp = _outproj([o], o_w_out[j], xp, mod_p, fin)
            df_k.append(k.reshape(bp, tp, DIFF_HEADS, 2 * DIFF_HEAD_DIM))
            df_v.append(v.reshape(bp, tp, DIFF_HEADS, 2 * DIFF_HEAD_DIM))

            q, k, v, g = _inproj(
                xs, mod_s, o_norm[j], o_w_in[j],
                [(dw, BF16, True, qmul, False), (dw, BF16, True, None, False), (dw, BF16, False, None, False),
                 (dw, BF16, False, None, False)],
                rope_tables=_rope_tables(ts))
            ck = cache_diff_k[:, j].reshape(bs, past * DIFF_HEADS, 2 * DIFF_HEAD_DIM)
            cv = cache_diff_v[:, j].reshape(bs, past * DIFF_HEADS, 2 * DIFF_HEAD_DIM)
            o = _diff_attn(q, [(ck, cv, True), (k, v, False)], g, lam_params, o_sub_g[j], lam_init, tq=ts,
                           heads_per_step=1)
            xs = _outproj([o], o_w_out[j], xs, mod_s, fin)

    return (xp, xs, jnp.stack(na_k, axis=1), jnp.stack(na_v, axis=1), jnp.stack(lru_s, axis=1),
            jnp.stack(df_k, axis=1), jnp.stack(df_v, axis=1))
```

```python
import functools
import math

import jax
import jax.numpy as jnp
from jax import lax
from jax.experimental import pallas as pl
from jax.experimental.pallas import tpu as pltpu

D_MODEL = 1024
DEPTH = 2
GRID_W = 64
NA_HEADS = 8
NA_HEAD_DIM = 64
NA_WIDTH = NA_HEADS * NA_HEAD_DIM
NA_ROWS_MAX = 8
NA_COLS = 16
RPB_ROWS = 2 * NA_ROWS_MAX - 1
RPB_COLS = 2 * NA_COLS - 1
LRU_WIDTH = 512
LRU_BLOCKS = 8
LRU_BLOCK = LRU_WIDTH // LRU_BLOCKS
LRU_C = 8.0
DIFF_HEADS = 8
DIFF_HEAD_DIM = 64
DIFF_WIDTH = DIFF_HEADS * 2 * DIFF_HEAD_DIM
ROPE_BASE = 10000.0
EPS = 1e-6
NEG_INF = -1e30
LOG2E = math.log2(math.e)

LANES = 128
SUBLANES = 8
MXU_DIM = 256
VMEM_LIMIT = 56 * 1024 * 1024

CAST_COLS = 512
LRU_SEGMENTS = 4
LRU_UNROLL = 8
LRU_CHUNK = 128
CONV_PAD_L = 2
CONV_PAD_R = 1
ROW_BLK = MXU_DIM
KEY_TILE = MXU_DIM

NA_Q_ROWS = 4
NA_K_ROWS = 12
NA_PAIR_TILES = 16

F32 = jnp.float32
BF16 = jnp.bfloat16


def _params(*semantics):
    return pltpu.CompilerParams(dimension_semantics=semantics, vmem_limit_bytes=VMEM_LIMIT)


def _sigmoid(x):
    return 0.5 * jnp.tanh(0.5 * x) + 0.5


def _silu(x):
    return x * _sigmoid(x)


def _dot(a, b):
    return jnp.dot(a, b, preferred_element_type=F32)


def _dot_nt(a, b):
    return lax.dot_general(a, b, (((1,), (1,)), ((), ())), preferred_element_type=F32)


def _mod_kernel(c_ref, w_ref, b_ref, o_ref):
    s = _silu(c_ref[...])
    o_ref[...] = _dot(s.astype(BF16), w_ref[...].astype(BF16)) + b_ref[...]


def _modulation(cc, w, b):
    rows, d = cc.shape
    n = w.shape[1]
    tn = 1024
    return pl.pallas_call(
        _mod_kernel,
        grid=(n // tn,),
        in_specs=[pl.BlockSpec((rows, d), lambda j: (0, 0)),
                  pl.BlockSpec((d, tn), lambda j: (0, j)),
                  pl.BlockSpec((1, tn), lambda j: (0, j))],
        out_specs=pl.BlockSpec((rows, tn), lambda j: (0, j)),
        out_shape=jax.ShapeDtypeStruct((rows, n), F32),
        compiler_params=_params("arbitrary"),
        name="modulation",
    )(cc, w, b.reshape(1, n))


def _rope_block(x, cos, sin_signed, low16):
    partner = jnp.where(low16, pltpu.roll(x, LANES - 16, axis=1), pltpu.roll(x, 16, axis=1))
    return x * cos + partner * sin_signed


def _cast_weights_once(w_ref, wbf_ref):
    @pl.when((pl.program_id(0) == 0) & (pl.program_id(1) == 0))
    def _():
        for off in range(0, w_ref.shape[1], CAST_COLS):
            wbf_ref[:, off:off + CAST_COLS] = w_ref[:, off:off + CAST_COLS].astype(BF16)


def _norm_project(x, mod_ref, g_ref, wbf_ref, rope_refs, out_refs, outs):
    tm, d = x.shape
    y = x * lax.rsqrt(jnp.mean(x * x, axis=-1, keepdims=True) + EPS) * g_ref[...]
    mod = mod_ref[0]
    h = (y * (1.0 + mod[:, d:2 * d]) + mod[:, :d]).astype(BF16)
    if rope_refs:
        cos = rope_refs[0][...]
        sin = rope_refs[1][...]
        lane = lax.broadcasted_iota(jnp.int32, (1, LANES), 1)
        low16 = (lane & 16) == 0
    off = 0
    for o_ref, (width, roped, mul, packed) in zip(out_refs, outs):
        acc = _dot(h, wbf_ref[:, off:off + width])
        if roped:
            acc = jnp.concatenate(
                [_rope_block(acc[:, j:j + LANES], cos, sin, low16) for j in range(0, width, LANES)], axis=1)
        if mul is not None:
            acc = acc * mul
        if packed:
            n_heads = width // LANES
            for hh in range(n_heads):
                o_ref[0, pl.ds(hh, tm, stride=n_heads), :] = acc[:, hh * LANES:(hh + 1) * LANES].astype(o_ref.dtype)
        else:
            o_ref[0] = acc.astype(o_ref.dtype)
        off += width


def _gated_residual(o_refs, wbf_ref, x_ref, mod_ref):
    d = x_ref.shape[-1]
    acc, row = None, 0
    for o_ref in o_refs:
        k = o_ref.shape[-1]
        part = _dot(o_ref[0], wbf_ref[row:row + k, :])
        acc = part if acc is None else acc + part
        row += k
    return x_ref[0] + mod_ref[0][:, 2 * d:] * acc


def _stage_kernel(*refs, n_o, final, outs, rope):
    refs = list(refs)
    take = lambda n: [refs.pop(0) for _ in range(n)]
    if n_o:
        o_refs = take(n_o)
        wout_ref, x_ref, mod_prev_ref = take(3)
        fin_ref = take(1)[0] if final else None
    else:
        x_ref, = take(1)
    if outs:
        mod_ref, g_ref, win_ref = take(3)
        rope_refs = take(2) if rope else []
    xout_ref = take(1)[0] if n_o else None
    out_refs = take(len(outs))
    if n_o:
        wout_bf = take(1)[0]
        _cast_weights_once(wout_ref, wout_bf)
        x = _gated_residual(o_refs, wout_bf, x_ref, mod_prev_ref)
        if final:
            xout_ref[0] = x * lax.rsqrt(jnp.mean(x * x, axis=-1, keepdims=True) + EPS) * fin_ref[...]
        else:
            xout_ref[0] = x
    else:
        x = x_ref[0]
    if outs:
        win_bf = take(1)[0]
        _cast_weights_once(win_ref, win_bf)
        _norm_project(x, mod_ref, g_ref, win_bf, rope_refs, out_refs, outs)


def _stage(x, prev=None, nxt=None, final_g=None, tm=256):
    b, t, d = x.shape
    tm = min(tm, t)
    tile = lambda width: pl.BlockSpec((1, tm, width), lambda i, j: (i, j, 0))
    mod_spec = lambda mod: pl.BlockSpec(
        (1, 1, 3 * d), (lambda i, j: (i, 0, 0)) if mod.shape[0] > 1 else (lambda i, j: (0, 0, 0)))
    resident = lambda w: pl.BlockSpec(w.shape, lambda i, j: (0, 0), pipeline_mode=pl.Buffered(1))
    row_spec = pl.BlockSpec((1, d), lambda i, j: (0, 0))
    in_specs, args, out_specs, out_shape, scratch = [], [], [], [], []
    n_o, outs, rope = 0, (), False
    if prev is not None:
        o_list, w_out, mod_prev = prev
        n_o = len(o_list)
        in_specs += [tile(o.shape[-1]) for o in o_list] + [resident(w_out), tile(d), mod_spec(mod_prev)]
        args += list(o_list) + [w_out, x, mod_prev]
        if final_g is not None:
            in_specs.append(row_spec)
            args.append(final_g.reshape(1, d))
        out_specs.append(tile(d))
        out_shape.append(jax.ShapeDtypeStruct((b, t, d), F32))
        scratch.append(pltpu.VMEM(w_out.shape, BF16))
    else:
        in_specs.append(tile(d))
        args.append(x)
    if nxt is not None:
        mod_next, norm_g, w_in, out_list, rope_tables = nxt
        in_specs += [mod_spec(mod_next), row_spec, resident(w_in)]
        args += [mod_next, norm_g.reshape(1, d), w_in]
        rope = rope_tables is not None
        if rope:
            in_specs += [pl.BlockSpec((tm, LANES), lambda i, j: (j, 0))] * 2
            args += list(rope_tables)
        for width, dt, _, _, packed in out_list:
            if packed:
                heads = width // LANES
                out_specs.append(pl.BlockSpec((1, tm * heads, LANES), lambda i, j: (i, j, 0)))
                out_shape.append(jax.ShapeDtypeStruct((b, t * heads, LANES), dt))
            else:
                out_specs.append(tile(width))
                out_shape.append(jax.ShapeDtypeStruct((b, t, width), dt))
        outs = tuple((w_, r, m, p) for w_, _, r, m, p in out_list)
        scratch.append(pltpu.VMEM(w_in.shape, BF16))
    res = pl.pallas_call(
        functools.partial(_stage_kernel, n_o=n_o, final=final_g is not None, outs=outs, rope=rope),
        grid=(b, t // tm),
        in_specs=in_specs,
        out_specs=out_specs,
        out_shape=out_shape,
        scratch_shapes=scratch,
        compiler_params=_params("arbitrary", "arbitrary"),
        name="stage",
    )(*args)
    return (res[0], list(res[1:])) if n_o else (None, list(res))


def _attend(q_blocks, kv_blocks, s_ref, score_fn=None):
    n_rb = len(q_blocks)

    def tile_of(rb, c):
        return (slice(rb * ROW_BLK, (rb + 1) * ROW_BLK), slice(c * KEY_TILE, (c + 1) * KEY_TILE))

    def score_tile(rb, c):
        s_ref[tile_of(rb, c)] = _dot_nt(q_blocks[rb], kv_blocks[rb][0][c])

    def row_max(rb):
        m_acc = None
        for c in range(len(kv_blocks[rb][0])):
            s = s_ref[tile_of(rb, c)]
            adjusted = None if score_fn is None else score_fn(rb, c, s)
            if adjusted is not None:
                s = adjusted
                s_ref[tile_of(rb, c)] = s
            mt = jnp.maximum(s[:, :LANES], s[:, LANES:])
            m_acc = mt if m_acc is None else jnp.maximum(m_acc, mt)
        return m_acc.max(axis=-1, keepdims=True)

    for c in range(len(kv_blocks[0][0])):
        score_tile(0, c)
    outs = []
    for rb in range(n_rb):
        m = row_max(rb)
        n_tiles = len(kv_blocks[rb][0])
        n_next = len(kv_blocks[rb + 1][0]) if rb + 1 < n_rb else 0
        acc = None
        for c in range(max(n_tiles, n_next)):
            if c < n_next:
                score_tile(rb + 1, c)
            if c < n_tiles:
                e = jnp.exp2(s_ref[tile_of(rb, c)] - m)
                pv = _dot(e.astype(BF16), kv_blocks[rb][1][c])
                acc = pv if acc is None else acc + pv
        outs.append(acc)
    return outs


def _lane_low_half():
    return lax.broadcasted_iota(jnp.int32, (1, LANES), 1) < (LANES // 2)


def _values_and_ones(v_tiles):
    ones = jnp.ones((KEY_TILE, LANES), BF16)
    return [jnp.concatenate([v, ones], axis=1) for v in v_tiles]


def _normalise_pair(acc_lo, acc_hi, low):
    return jnp.where(low, acc_lo[:, :LANES] / acc_lo[:, LANES:], acc_hi[:, :LANES] / acc_hi[:, LANES:])


def _split_lane_halves(q, low):
    zero = jnp.zeros_like(q)
    return jnp.where(low, q, zero), jnp.where(low, zero, q)


def _tiles(ref, n_tiles, j):
    return [ref[0, c * KEY_TILE:(c + 1) * KEY_TILE, j:j + LANES].astype(BF16) for c in range(n_tiles)]


def _ctx_attn_kernel(q_ref, k_ref, v_ref, g_ref, o_ref, s_ref):
    low = _lane_low_half()
    tq = q_ref.shape[1]
    n_tiles = k_ref.shape[1] // KEY_TILE
    lane_groups = range(0, q_ref.shape[2], LANES)
    q_blocks, kv_blocks = [], []
    for j in lane_groups:
        kv = (_tiles(k_ref, n_tiles, j), _values_and_ones(_tiles(v_ref, n_tiles, j)))
        for r in range(0, tq, ROW_BLK):
            q_blocks += list(_split_lane_halves(q_ref[0, r:r + ROW_BLK, j:j + LANES], low))
            kv_blocks += [kv, kv]
    res = _attend(q_blocks, kv_blocks, s_ref)
    i = 0
    for j in lane_groups:
        for r in range(0, tq, ROW_BLK):
            o = _normalise_pair(res[i], res[i + 1], low)
            i += 2
            gate = _silu(g_ref[0, r:r + ROW_BLK, j:j + LANES].astype(F32))
            o_ref[0, r:r + ROW_BLK, j:j + LANES] = (o * gate).astype(o_ref.dtype)


def _ctx_attn(q, k, v, g):
    b, t, w = q.shape
    spec = pl.BlockSpec((1, t, w), lambda i: (i, 0, 0))
    return pl.pallas_call(
        _ctx_attn_kernel,
        grid=(b,),
        in_specs=[spec] * 4,
        out_specs=spec,
        out_shape=jax.ShapeDtypeStruct((b, t, w), BF16),
        scratch_shapes=[pltpu.VMEM((2 * t * (w // LANES), k.shape[1]), F32)],
        compiler_params=_params("parallel"),
        name="ctx_attn",
    )(q, k, v, g)


def _na_pair_tiles(rpb_ref, head, pt_ref):
    right = lax.broadcasted_iota(jnp.int32, (GRID_W, LANES), 1) >= GRID_W
    zero_row = jnp.zeros((1, LANES), F32)

    def toeplitz(row, lane0):
        shift = (lane0 - (NA_COLS - 1)) % LANES
        return pltpu.roll(jnp.broadcast_to(row, (GRID_W, LANES)), shift, axis=1, stride=1, stride_axis=0)

    for p in range(NA_PAIR_TILES):
        left_row = rpb_ref[head, p - 1:p, :] if p >= 1 else zero_row
        right_row = rpb_ref[head, p:p + 1, :]
        pt_ref[head, p] = jnp.where(right, toeplitz(right_row, GRID_W), toeplitz(left_row, 0)) * LOG2E


def _na_row_start(r, n_rows):
    kr = min(NA_ROWS_MAX, n_rows)
    return min(max(r - kr // 2, 0), n_rows - kr)


def _na_window(grp, n_rows):
    kr = min(NA_ROWS_MAX, n_rows)
    starts = [_na_row_start(grp * NA_Q_ROWS + i, n_rows) for i in range(NA_Q_ROWS)]
    rows_per_tile = KEY_TILE // GRID_W
    need = -(-(max(starts) + kr - min(starts)) // rows_per_tile) * rows_per_tile
    return min(min(starts), n_rows - need), need


def _na_kernel(q_ref, k_ref, v_ref, ck_ref, cv_ref, rpb_ref, g_ref, o_ref, pt_ref, badd_ref, bcap_ref, s_ref, *,
               n_rows):
    kr = min(NA_ROWS_MAX, n_rows)
    n_grp = n_rows // NA_Q_ROWS
    low = _lane_low_half()

    @pl.when(pl.program_id(1) == 0)
    def _build_bias():
        for half in range(2):
            _na_pair_tiles(rpb_ref, half, pt_ref)
        qcol = lax.broadcasted_iota(jnp.int32, (GRID_W, LANES), 0)
        lane = lax.broadcasted_iota(jnp.int32, (GRID_W, LANES), 1)
        right = lane >= GRID_W
        kcol = lane & (GRID_W - 1)
        col_start = jnp.clip(qcol - NA_COLS // 2, 0, GRID_W - NA_COLS)
        col_ok = (kcol >= col_start) & (kcol < col_start + NA_COLS)
        for grp in range(n_grp):
            krow0, key_rows = _na_window(grp, n_rows)
            for i in range(NA_Q_ROWS):
                r = grp * NA_Q_ROWS + i
                row_start = _na_row_start(r, n_rows)
                for m in range(key_rows // 2):
                    key_row = krow0 + 2 * m
                    ok_l = row_start <= key_row < row_start + kr
                    ok_r = row_start <= key_row + 1 < row_start + kr
                    rows = slice(i * GRID_W, (i + 1) * GRID_W)
                    cols = slice(m * LANES, (m + 1) * LANES)
                    if not (ok_l or ok_r):
                        bcap_ref[grp, rows, cols] = jnp.full((GRID_W, LANES), NEG_INF, F32)
                        for half in range(2):
                            badd_ref[2 * grp + half, rows, cols] = jnp.zeros((GRID_W, LANES), F32)
                        continue
                    ok = col_ok
                    if not ok_r:
                        ok = ok & jnp.logical_not(right)
                    if not ok_l:
                        ok = ok & right
                    bcap_ref[grp, rows, cols] = jnp.where(ok, jnp.inf, NEG_INF)
                    p = key_row - r + NA_ROWS_MAX
                    for half in range(2):
                        badd_ref[2 * grp + half, rows, cols] = jnp.where(ok, pt_ref[half, p], 0.0)

    n_ctx = ck_ref.shape[1] // KEY_TILE
    ctx_k = _tiles(ck_ref, n_ctx, 0)
    ctx_v = _tiles(cv_ref, n_ctx, 0)
    q_blocks, kv_blocks, n_win = [], [], []
    for grp in range(n_grp):
        krow0, key_rows = _na_window(grp, n_rows)
        k0 = krow0 * GRID_W
        win = [slice(k0 + c * KEY_TILE, k0 + (c + 1) * KEY_TILE) for c in range(key_rows * GRID_W // KEY_TILE)]
        kv = ([k_ref[0, w, :] for w in win] + ctx_k, _values_and_ones([v_ref[0, w, :] for w in win] + ctx_v))
        q_blocks += list(_split_lane_halves(q_ref[0, grp * ROW_BLK:(grp + 1) * ROW_BLK, :], low))
        kv_blocks += [kv, kv]
        n_win += [len(win)] * 2

    def masked_bias(rb, c, s):
        if c >= n_win[rb]:
            return None
        cols = slice(c * KEY_TILE, (c + 1) * KEY_TILE)
        return jnp.minimum(s + badd_ref[rb, :, cols], bcap_ref[rb // 2, :, cols])

    res = _attend(q_blocks, kv_blocks, s_ref, masked_bias)
    for grp in range(n_grp):
        o = _normalise_pair(res[2 * grp], res[2 * grp + 1], low)
        rows = slice(grp * ROW_BLK, (grp + 1) * ROW_BLK)
        o_ref[0, rows, :] = (o * _silu(g_ref[0, rows, :].astype(F32))).astype(o_ref.dtype)


def _na_attn(q, k, v, ck, cv, rpb, g):
    heads, rpb_rows, rpb_cols = rpb.shape
    rpb = jnp.pad(rpb, ((0, 0), (0, NA_PAIR_TILES - rpb_rows), (0, LANES - rpb_cols)))
    b, t, w = q.shape
    c = ck.shape[1]
    n_rows = t // GRID_W
    assert NA_Q_ROWS * GRID_W == ROW_BLK and n_rows % NA_Q_ROWS == 0 and n_rows >= NA_K_ROWS
    n_grp = n_rows // NA_Q_ROWS
    n_win = NA_K_ROWS * GRID_W
    tspec = pl.BlockSpec((1, t, LANES), lambda hp, bi: (bi, 0, hp))
    cspec = pl.BlockSpec((1, c, LANES), lambda hp, bi: (bi, 0, hp))
    ptspec = pl.BlockSpec((2, NA_PAIR_TILES, LANES), lambda hp, bi: (hp, 0, 0))
    return pl.pallas_call(
        functools.partial(_na_kernel, n_rows=n_rows),
        grid=(w // LANES, b),
        in_specs=[tspec, tspec, tspec, cspec, cspec, ptspec, tspec],
        out_specs=tspec,
        out_shape=jax.ShapeDtypeStruct((b, t, w), BF16),
        scratch_shapes=[pltpu.VMEM((2, NA_PAIR_TILES, GRID_W, LANES), F32),
                        pltpu.VMEM((2 * n_grp, ROW_BLK, n_win), F32),
                        pltpu.VMEM((n_grp, ROW_BLK, n_win), F32),
                        pltpu.VMEM((2 * t, n_win + c), F32)],
        compiler_params=_params("parallel", "arbitrary"),
        name="na_attn",
    )(q, k, v, ck, cv, rpb, g)


def _softplus(x):
    return jnp.maximum(x, 0.0) + jnp.log(1.0 + jnp.exp(-jnp.abs(x)))


def _lru_kernel(x_ref, gb_ref, h0f_ref, h0b_ref, cw_ref, cb_ref, wg_ref, lam_ref,
                o_ref, ff_ref, fb_ref, xt_s, af_s, uf_s, ab_s, ub_s, hf_s, hb_s):
    nb, t, _ = x_ref.shape
    pad = CONV_PAD_L * nb
    cw = cw_ref[...]
    cb = cb_ref[...]
    wg = wg_ref[0]
    kk = (-0.5 * LRU_C * LOG2E) * _softplus(-lam_ref[...])

    xt_s[0:pad, :] = jnp.zeros((pad, LANES), F32)
    xt_s[pad + t * nb:pad + (t + CONV_PAD_R) * nb, :] = jnp.zeros((CONV_PAD_R * nb, LANES), F32)

    def to_time_major(b, carry):
        xt_s[pl.ds(pad + b, t, stride=nb), :] = x_ref[b]
        return carry

    lax.fori_loop(0, nb, to_time_major, 0)

    rows_per_chunk = LRU_CHUNK * nb
    lane = lax.broadcasted_iota(jnp.int32, (rows_per_chunk, LANES), 1)
    bias_cols = jnp.where(lane < 2, 1.0, 0.0).astype(BF16)

    def gates(c, carry):
        r0 = pl.multiple_of(c * rows_per_chunk, rows_per_chunk)

        def tap(i):
            return xt_s[pl.ds(r0 + i * nb, rows_per_chunk), :]

        x = tap(2)
        xc = cw[0:1] * tap(0) + cw[1:2] * tap(1) + cw[2:3] * x + cw[3:4] * tap(3) + cb
        th = jnp.tanh(_dot(jnp.concatenate([xc.astype(BF16), bias_cols], axis=1), wg))
        xh = 0.5 * xc
        rows = pl.ds(r0, rows_per_chunk)
        for d, (a_s, u_s) in enumerate(((af_s, uf_s), (ab_s, ub_s))):
            t_a = th[:, (2 * d) * LANES:(2 * d + 1) * LANES]
            t_x = th[:, (2 * d + 1) * LANES:(2 * d + 2) * LANES]
            a = jnp.exp2(kk[d:d + 1] * t_a + kk[d:d + 1])
            y = 1.0 - a * a
            sq = y * lax.rsqrt(jnp.maximum(y, 1e-30))
            a_s[rows, :] = a
            u_s[rows, :] = sq * (xh * t_x + xh)
        return carry

    lax.fori_loop(0, t // LRU_CHUNK, gates, 0)

    ts = t // LRU_SEGMENTS
    refs = ((af_s, uf_s), (ab_s, ub_s))

    slab = nb * LRU_UNROLL

    def slab_rows(d, j, i):
        base = (j * ts + i * LRU_UNROLL) * nb
        return pl.ds(pl.multiple_of(base if d == 0 else t * nb - slab - base, slab), slab)

    def step_of(d, block, k):
        kk_ = k if d == 0 else LRU_UNROLL - 1 - k
        return block[kk_ * nb:(kk_ + 1) * nb]

    def compose(i, carry):
        state = list(carry)
        for d in range(2):
            for j in range(LRU_SEGMENTS - 1):
                n = d * (LRU_SEGMENTS - 1) + j
                a_blk = refs[d][0][slab_rows(d, j, i), :]
                u_blk = refs[d][1][slab_rows(d, j, i), :]
                g, big_a = state[n]
                for k in range(LRU_UNROLL):
                    a = step_of(d, a_blk, k)
                    g, big_a = a * g + step_of(d, u_blk, k), a * big_a
                state[n] = (g, big_a)
        return tuple(state)

    ones = jnp.ones((nb, LANES), F32)
    h0 = (h0f_ref[...], h0b_ref[...])
    init = tuple((h0[d] if j == 0 else jnp.zeros((nb, LANES), F32), ones)
                 for d in range(2) for j in range(LRU_SEGMENTS - 1))
    ends = lax.fori_loop(0, ts // LRU_UNROLL, compose, init)
    starts = []
    for d in range(2):
        h = h0[d]
        for j in range(LRU_SEGMENTS):
            starts.append(h)
            if j < LRU_SEGMENTS - 1:
                g, big_a = ends[d * (LRU_SEGMENTS - 1) + j]
                h = g if j == 0 else g + big_a * h

    def rerun(i, carry):
        state = list(carry)
        for d, h_s in enumerate((hf_s, hb_s)):
            for j in range(LRU_SEGMENTS):
                n = d * LRU_SEGMENTS + j
                a_blk = refs[d][0][slab_rows(d, j, i), :]
                u_blk = refs[d][1][slab_rows(d, j, i), :]
                hs = []
                for k in range(LRU_UNROLL):
                    state[n] = step_of(d, a_blk, k) * state[n] + step_of(d, u_blk, k)
                    hs.append(state[n])
                h_s[slab_rows(d, j, i), :] = jnp.concatenate(hs if d == 0 else hs[::-1], axis=0)
        return tuple(state)

    final = lax.fori_loop(0, ts // LRU_UNROLL, rerun, tuple(starts))
    ff_ref[...] = final[LRU_SEGMENTS - 1]
    fb_ref[...] = final[2 * LRU_SEGMENTS - 1]

    def add_directions(c, carry):
        rows = pl.ds(pl.multiple_of(c * rows_per_chunk, rows_per_chunk), rows_per_chunk)
        hf_s[rows, :] = hf_s[rows, :] + hb_s[rows, :]
        return carry

    lax.fori_loop(0, t // LRU_CHUNK, add_directions, 0)

    def emit(b, carry):
        hs = hf_s[pl.ds(b, t, stride=nb), :]
        o_ref[b] = (hs * _silu(gb_ref[b].astype(F32))).astype(o_ref.dtype)
        return carry

    lax.fori_loop(0, nb, emit, 0)


def _lru_branch(xb, gb, h0f, h0b, conv_w, conv_b, wg, lam):
    b, t, w = xb.shape
    nb = SUBLANES
    assert t % (LRU_SEGMENTS * LRU_UNROLL) == 0 and t % LRU_CHUNK == 0
    assert conv_w.shape[0] == CONV_PAD_L + 1 + CONV_PAD_R
    xspec = pl.BlockSpec((nb, t, LANES), lambda i, c: (i, 0, c))
    hspec = pl.BlockSpec((nb, LANES), lambda i, c: (i, c))
    scratch = [pltpu.VMEM(((t + CONV_PAD_L + CONV_PAD_R) * nb, LANES), F32)]
    scratch += [pltpu.VMEM((t * nb, LANES), F32) for _ in range(6)]
    return pl.pallas_call(
        _lru_kernel,
        grid=(b // nb, w // LANES),
        in_specs=[xspec, xspec, hspec, hspec,
                  pl.BlockSpec((conv_w.shape[0], LANES), lambda i, c: (0, c)),
                  pl.BlockSpec((1, LANES), lambda i, c: (0, c)),
                  pl.BlockSpec((1, 2 * LANES, 4 * LANES), lambda i, c: (c, 0, 0)),
                  pl.BlockSpec((2, LANES), lambda i, c: (0, c))],
        out_specs=[xspec, hspec, hspec],
        out_shape=[jax.ShapeDtypeStruct((b, t, w), BF16),
                   jax.ShapeDtypeStruct((b, w), F32),
                   jax.ShapeDtypeStruct((b, w), F32)],
        scratch_shapes=scratch,
        compiler_params=_params("parallel", "parallel"),
        name="lru_branch",
    )(xb, gb, h0f, h0b, conv_w, conv_b.reshape(1, w), wg, lam)


def _lru_gate_weights(wa, ba, wx, bx):
    def tiles(w):
        w2 = w.reshape(LRU_BLOCKS // 2, 2, LRU_BLOCK, LRU_BLOCK)
        z = jnp.zeros_like(w2[:, 0])
        return jnp.concatenate([jnp.concatenate([w2[:, 0], z], axis=2),
                                jnp.concatenate([z, w2[:, 1]], axis=2)], axis=1)
    wg = 0.5 * jnp.concatenate([tiles(wa[0]), tiles(wx[0]), tiles(wa[1]), tiles(wx[1])], axis=2)
    n_tiles = LRU_WIDTH // LANES
    bg = jnp.stack([ba[0], bx[0], ba[1], bx[1]], axis=0).reshape(4, n_tiles, LANES)
    bg = 0.5 * jnp.transpose(bg, (1, 0, 2)).reshape(n_tiles, 1, 4 * LANES)
    hi = bg.astype(BF16)
    lo = (bg - hi.astype(F32)).astype(BF16)
    rest = jnp.zeros((n_tiles, LANES - 2, 4 * LANES), BF16)
    return jnp.concatenate([wg.astype(BF16), hi, lo, rest], axis=1)


def _packed_tiles(ref, n_heads, head):
    n_tiles = ref.shape[1] // (n_heads * KEY_TILE)
    return [ref[0, pl.ds(c * KEY_TILE * n_heads + head, KEY_TILE, stride=n_heads), :].astype(BF16)
            for c in range(n_tiles)]


def _diff_kernel(*refs, packed, n_heads, lam_init):
    n_kv = len(packed)
    q_ref = refs[0]
    kv_refs = refs[1:1 + 2 * n_kv]
    g_ref, lamp_ref, subg_ref, o_ref, s_ref = refs[1 + 2 * n_kv:]
    low = _lane_low_half()
    tq = q_ref.shape[1]
    first_head = pl.program_id(1) * (q_ref.shape[2] // LANES)
    lp = lamp_ref[...]
    lam = (jnp.exp(jnp.sum(lp[0:1] * lp[1:2], axis=-1, keepdims=True))
           - jnp.exp(jnp.sum(lp[2:3] * lp[3:4], axis=-1, keepdims=True)) + lam_init)
    lane_groups = range(0, q_ref.shape[2], LANES)
    q_blocks, kv_blocks = [], []
    for j in lane_groups:
        k_tiles, v_tiles = [], []
        for i in range(n_kv):
            if packed[i]:
                head = first_head + j // LANES
                k_tiles += _packed_tiles(kv_refs[2 * i], n_heads, head)
                v_tiles += _packed_tiles(kv_refs[2 * i + 1], n_heads, head)
            else:
                n_tiles = kv_refs[2 * i].shape[1] // KEY_TILE
                k_tiles += _tiles(kv_refs[2 * i], n_tiles, j)
                v_tiles += _tiles(kv_refs[2 * i + 1], n_tiles, j)
        v_tiles = _values_and_ones(v_tiles)
        for r in range(0, tq, ROW_BLK):
            q_blocks += list(_split_lane_halves(q_ref[0, r:r + ROW_BLK, j:j + LANES], low))
            kv_blocks += [(k_tiles, v_tiles)] * 2
    res = _attend(q_blocks, kv_blocks, s_ref)
    i = 0
    for j in lane_groups:
        for r in range(0, tq, ROW_BLK):
            acc1, acc2 = res[i], res[i + 1]
            i += 2
            o = acc1[:, :LANES] / acc1[:, LANES:] - acc2[:, :LANES] * (lam / acc2[:, LANES:])
            o = o * lax.rsqrt(jnp.mean(o * o, axis=-1, keepdims=True) + EPS) * subg_ref[...] * (1.0 - lam_init)
            gate = _silu(g_ref[0, r:r + ROW_BLK, j:j + LANES].astype(F32))
            o_ref[0, r:r + ROW_BLK, j:j + LANES] = (o * gate).astype(o_ref.dtype)


def _diff_attn(q, kv_list, g, lam_params, sub_g, lam_init, tq, heads_per_step):
    b, t, w = q.shape
    wb = heads_per_step * LANES
    qspec = pl.BlockSpec((1, tq, wb), lambda bi, h, qi: (bi, qi, h))
    n_heads = w // LANES
    in_specs = [qspec]
    args = [q]
    n_keys = 0
    for k, v, packed in kv_list:
        if packed:
            spec = pl.BlockSpec((1, k.shape[1], LANES), lambda bi, h, qi: (bi, 0, 0))
            n_keys += k.shape[1] // n_heads
        else:
            spec = pl.BlockSpec((1, k.shape[1], wb), lambda bi, h, qi: (bi, 0, h))
            n_keys += k.shape[1]
        in_specs += [spec, spec]
        args += [k, v]
    in_specs += [qspec,
                 pl.BlockSpec(lam_params.shape, lambda bi, h, qi: (0, 0)),
                 pl.BlockSpec((1, LANES), lambda bi, h, qi: (0, 0))]
    args += [g, lam_params, sub_g.reshape(1, LANES)]
    return pl.pallas_call(
        functools.partial(_diff_kernel, packed=tuple(p for _, _, p in kv_list), n_heads=n_heads,
                          lam_init=lam_init),
        grid=(b, w // wb, t // tq),
        in_specs=in_specs,
        out_specs=qspec,
        out_shape=jax.ShapeDtypeStruct((b, t, w), BF16),
        scratch_shapes=[pltpu.VMEM((heads_per_step * 2 * tq, n_keys), F32)],
        compiler_params=_params("parallel", "parallel", "parallel"),
        name="diff_attn",
    )(*args)


def _rope_tables(t_len):
    m = DIFF_HEAD_DIM // 4
    t = jnp.arange(t_len)
    rows = (t // GRID_W).astype(F32)
    cols = (t % GRID_W).astype(F32)
    inv = ROPE_BASE ** (-jnp.arange(m, dtype=F32) / m)
    ang_r = rows[:, None] * inv[None, :]
    ang_c = cols[:, None] * inv[None, :]
    cos = jnp.concatenate([jnp.cos(ang_r)] * 2 + [jnp.cos(ang_c)] * 2, axis=1)
    sin = jnp.concatenate([-jnp.sin(ang_r), jnp.sin(ang_r), -jnp.sin(ang_c), jnp.sin(ang_c)], axis=1)
    reps = LANES // DIFF_HEAD_DIM
    return jnp.tile(cos, (1, reps)), jnp.tile(sin, (1, reps))


def kernel(x_prompt, x_sample, c, cache_na_k, cache_na_v, state_lru, cache_diff_k, cache_diff_v, c_ctx,
           e_norm, e_ada_w, e_ada_b, e_w_in, e_rpb, e_conv_w, e_conv_b, e_lru_wa, e_lru_ba, e_lru_wx, e_lru_bx,
           e_lru_lam, e_w_out, o_norm, o_ada_w, o_ada_b, o_w_in, o_lq1, o_lk1, o_lq2, o_lk2, o_sub_g, o_w_out,
           final_norm):
    bp, tp, d = x_prompt.shape
    bs, ts, _ = x_sample.shape
    past = cache_na_k.shape[2]
    pad_rows = 2 * SUBLANES - bs - 1
    cc = jnp.concatenate([c, c_ctx[None, :], jnp.zeros((pad_rows, d), F32)], axis=0)
    nw, lw, dw = NA_WIDTH, LRU_WIDTH, DIFF_WIDTH
    rope_tables = _rope_tables(ts) if DEPTH > 1 else None

    def layer_params(i):
        j = i // 2
        if i % 2 == 0:
            return e_ada_w[j], e_ada_b[j], e_norm[j], e_w_in[j], e_w_out[j]
        return o_ada_w[j], o_ada_b[j], o_norm[j], o_w_in[j], o_w_out[j]

    mods = []
    for i in range(DEPTH):
        ada_w, ada_b = layer_params(i)[:2]
        m = _modulation(cc, ada_w, ada_b)
        mods.append((m[:bs].reshape(bs, 1, 3 * d), m[bs:bs + 1].reshape(1, 1, 3 * d)))

    def projection(i, latent):
        norm_g, w_in = layer_params(i)[2:4]
        mod = mods[i][0 if latent else 1]
        kv_dtype = BF16 if latent else F32
        if i % 2 == 0:
            qmul = NA_HEAD_DIM ** -0.5 * LOG2E
            outs = [(nw, BF16, False, qmul, False), (nw, kv_dtype, False, None, False),
                    (nw, kv_dtype, False, None, False), (nw, BF16, False, None, False),
                    (lw, F32, False, None, False), (lw, BF16, False, None, False)]
            return mod, norm_g, w_in, outs, None
        qmul = DIFF_HEAD_DIM ** -0.5 * LOG2E
        outs = [(dw, BF16, latent, qmul, False), (dw, kv_dtype, latent, None, not latent),
                (dw, kv_dtype, False, None, not latent), (dw, BF16, False, None, False)]
        return mod, norm_g, w_in, outs, rope_tables if latent else None

    xp, xs = x_prompt, x_sample
    _, proj_p = _stage(xp, nxt=projection(0, False), tm=512)
    _, proj_s = _stage(xs, nxt=projection(0, True), tm=512)
    na_k, na_v, lru_s, df_k, df_v = [], [], [], [], []
    for i in range(DEPTH):
        j = i // 2
        w_out = layer_params(i)[4]
        if i % 2 == 0:
            wg = _lru_gate_weights(e_lru_wa[j], e_lru_ba[j], e_lru_wx[j], e_lru_bx[j])

            q, k, v, ga, xb, gb = proj_p
            oa = _ctx_attn(q, k, v, ga)
            zeros = jnp.zeros((bp, lw), F32)
            ob, ff, fb = _lru_branch(xb, gb, zeros, zeros, e_conv_w[j], e_conv_b[j], wg, e_lru_lam[j])
            mixed_p = [oa, ob]
            na_k.append(k.reshape(bp, tp, NA_HEADS, NA_HEAD_DIM))
            na_v.append(v.reshape(bp, tp, NA_HEADS, NA_HEAD_DIM))
            lru_s.append(jnp.stack([ff, fb], axis=1))

            q, k, v, ga, xb, gb = proj_s
            ck = cache_na_k[:, j].reshape(bs, past, nw)
            cv = cache_na_v[:, j].reshape(bs, past, nw)
            oa = _na_attn(q, k, v, ck, cv, e_rpb[j], ga)
            ob, _, _ = _lru_branch(xb, gb, state_lru[:, j, 0], state_lru[:, j, 1],
                                   e_conv_w[j], e_conv_b[j], wg, e_lru_lam[j])
            mixed_s = [oa, ob]
        else:
            lam_init = 0.8 - 0.6 * math.exp(-0.3 * i)
            lam_params = jnp.stack([o_lq1[j], o_lk1[j], o_lq2[j], o_lk2[j]], axis=0)

            q, k, v, g = proj_p
            mixed_p = [_diff_attn(q, [(k, v, True)], g, lam_params, o_sub_g[j], lam_init, tq=tp,
                                  heads_per_step=DIFF_HEADS)]
            df_k.append(k.reshape(bp, tp, DIFF_HEADS, 2 * DIFF_HEAD_DIM))
            df_v.append(v.reshape(bp, tp, DIFF_HEADS, 2 * DIFF_HEAD_DIM))

            q, k, v, g = proj_s
            ck = cache_diff_k[:, j].reshape(bs, past * DIFF_HEADS, 2 * DIFF_HEAD_DIM)
            cv = cache_diff_v[:, j].reshape(bs, past * DIFF_HEADS, 2 * DIFF_HEAD_DIM)
            mixed_s = [_diff_attn(q, [(ck, cv, True), (k, v, False)], g, lam_params, o_sub_g[j], lam_init, tq=ts,
                                  heads_per_step=1)]

        if i == DEPTH - 1:
            xp, _ = _stage(xp, prev=(mixed_p, w_out, mods[i][1]), final_g=final_norm, tm=512)
            xs, _ = _stage(xs, prev=(mixed_s, w_out, mods[i][0]), final_g=final_norm, tm=512)
        else:
            xp, proj_p = _stage(xp, prev=(mixed_p, w_out, mods[i][1]), nxt=projection(i + 1, False))
            xs, proj_s = _stage(xs, prev=(mixed_s, w_out, mods[i][0]), nxt=projection(i + 1, True))

    return (xp, xs, jnp.stack(na_k, axis=1), jnp.stack(na_v, axis=1), jnp.stack(lru_s, axis=1),
            jnp.stack(df_k, axis=1), jnp.stack(df_v, axis=1))
```

```python
import functools
import math

import jax
import jax.numpy as jnp
from jax import lax
from jax.experimental import pallas as pl
from jax.experimental.pallas import tpu as pltpu

D_MODEL = 1024
DEPTH = 2
GRID_W = 64
NA_HEADS = 8
NA_HEAD_DIM = 64
NA_WIDTH = NA_HEADS * NA_HEAD_DIM
NA_ROWS_MAX = 8
NA_COLS = 16
RPB_ROWS = 2 * NA_ROWS_MAX - 1
RPB_COLS = 2 * NA_COLS - 1
LRU_WIDTH = 512
LRU_BLOCKS = 8
LRU_BLOCK = LRU_WIDTH // LRU_BLOCKS
LRU_C = 8.0
DIFF_HEADS = 8
DIFF_HEAD_DIM = 64
DIFF_WIDTH = DIFF_HEADS * 2 * DIFF_HEAD_DIM
ROPE_BASE = 10000.0
EPS = 1e-6
NEG_INF = -1e30
LOG2E = math.log2(math.e)

LANES = 128
SUBLANES = 8
MXU_DIM = 256
VMEM_LIMIT = 56 * 1024 * 1024

CAST_COLS = 512
LRU_SEGMENTS = 4
LRU_UNROLL = 8
LRU_CHUNK = 128
CONV_PAD_L = 2
CONV_PAD_R = 1
ROW_BLK = MXU_DIM
KEY_TILE = MXU_DIM

NA_Q_ROWS = 4
NA_K_ROWS = 12
NA_PAIR_TILES = 16

F32 = jnp.float32
BF16 = jnp.bfloat16


def _params(*semantics):
    return pltpu.CompilerParams(dimension_semantics=semantics, vmem_limit_bytes=VMEM_LIMIT)


def _sigmoid(x):
    return 0.5 * jnp.tanh(0.5 * x) + 0.5


def _silu(x):
    return x * _sigmoid(x)


def _dot(a, b):
    return jnp.dot(a, b, preferred_element_type=F32)


def _dot_nt(a, b):
    return lax.dot_general(a, b, (((1,), (1,)), ((), ())), preferred_element_type=F32)


def _mod_kernel(c_ref, w_ref, b_ref, o_ref):
    s = _silu(c_ref[...])
    o_ref[...] = _dot(s.astype(BF16), w_ref[...].astype(BF16)) + b_ref[...]


def _modulation(cc, w, b):
    rows, d = cc.shape
    n = w.shape[1]
    tn = 1024
    return pl.pallas_call(
        _mod_kernel,
        grid=(n // tn,),
        in_specs=[pl.BlockSpec((rows, d), lambda j: (0, 0)),
                  pl.BlockSpec((d, tn), lambda j: (0, j)),
                  pl.BlockSpec((1, tn), lambda j: (0, j))],
        out_specs=pl.BlockSpec((rows, tn), lambda j: (0, j)),
        out_shape=jax.ShapeDtypeStruct((rows, n), F32),
        compiler_params=_params("arbitrary"),
        name="modulation",
    )(cc, w, b.reshape(1, n))


def _rope_block(x, cos, sin_signed, low16):
    partner = jnp.where(low16, pltpu.roll(x, LANES - 16, axis=1), pltpu.roll(x, 16, axis=1))
    return x * cos + partner * sin_signed


def _cast_weights_once(w_ref, wbf_ref):
    @pl.when((pl.program_id(0) == 0) & (pl.program_id(1) == 0))
    def _():
        for off in range(0, w_ref.shape[1], CAST_COLS):
            wbf_ref[:, off:off + CAST_COLS] = w_ref[:, off:off + CAST_COLS].astype(BF16)


def _norm_project(x, mod_ref, g_ref, wbf_ref, rope_refs, out_refs, outs):
    tm, d = x.shape
    y = x * lax.rsqrt(jnp.mean(x * x, axis=-1, keepdims=True) + EPS) * g_ref[...]
    mod = mod_ref[0]
    h = (y * (1.0 + mod[:, d:2 * d]) + mod[:, :d]).astype(BF16)
    if rope_refs:
        cos = rope_refs[0][...]
        sin = rope_refs[1][...]
        lane = lax.broadcasted_iota(jnp.int32, (1, LANES), 1)
        low16 = (lane & 16) == 0
    off = 0
    for o_ref, (width, roped, mul, packed) in zip(out_refs, outs):
        acc = _dot(h, wbf_ref[:, off:off + width])
        if roped:
            acc = jnp.concatenate(
                [_rope_block(acc[:, j:j + LANES], cos, sin, low16) for j in range(0, width, LANES)], axis=1)
        if mul is not None:
            acc = acc * mul
        if packed:
            n_heads = width // LANES
            for hh in range(n_heads):
                o_ref[0, pl.ds(hh, tm, stride=n_heads), :] = acc[:, hh * LANES:(hh + 1) * LANES].astype(o_ref.dtype)
        else:
            o_ref[0] = acc.astype(o_ref.dtype)
        off += width


def _gated_residual(o_refs, wbf_ref, x_ref, mod_ref):
    d = x_ref.shape[-1]
    acc, row = None, 0
    for o_ref in o_refs:
        k = o_ref.shape[-1]
        part = _dot(o_ref[0], wbf_ref[row:row + k, :])
        acc = part if acc is None else acc + part
        row += k
    return x_ref[0] + mod_ref[0][:, 2 * d:] * acc


def _stage_kernel(*refs, n_o, final, outs, rope):
    refs = list(refs)
    take = lambda n: [refs.pop(0) for _ in range(n)]
    if n_o:
        o_refs = take(n_o)
        wout_ref, x_ref, mod_prev_ref = take(3)
        fin_ref = take(1)[0] if final else None
    else:
        x_ref, = take(1)
    if outs:
        mod_ref, g_ref, win_ref = take(3)
        rope_refs = take(2) if rope else []
    xout_ref = take(1)[0] if n_o else None
    out_refs = take(len(outs))
    if n_o:
        wout_bf = take(1)[0]
        _cast_weights_once(wout_ref, wout_bf)
        x = _gated_residual(o_refs, wout_bf, x_ref, mod_prev_ref)
        if final:
            xout_ref[0] = x * lax.rsqrt(jnp.mean(x * x, axis=-1, keepdims=True) + EPS) * fin_ref[...]
        else:
            xout_ref[0] = x
    else:
        x = x_ref[0]
    if outs:
        win_bf = take(1)[0]
        _cast_weights_once(win_ref, win_bf)
        _norm_project(x, mod_ref, g_ref, win_bf, rope_refs, out_refs, outs)


def _stage(x, prev=None, nxt=None, final_g=None, tm=256):
    b, t, d = x.shape
    tm = min(tm, t)
    tile = lambda width: pl.BlockSpec((1, tm, width), lambda i, j: (i, j, 0))
    mod_spec = lambda mod: pl.BlockSpec(
        (1, 1, 3 * d), (lambda i, j: (i, 0, 0)) if mod.shape[0] > 1 else (lambda i, j: (0, 0, 0)))
    resident = lambda w: pl.BlockSpec(w.shape, lambda i, j: (0, 0), pipeline_mode=pl.Buffered(1))
    row_spec = pl.BlockSpec((1, d), lambda i, j: (0, 0))
    in_specs, args, out_specs, out_shape, scratch = [], [], [], [], []
    n_o, outs, rope = 0, (), False
    if prev is not None:
        o_list, w_out, mod_prev = prev
        n_o = len(o_list)
        in_specs += [tile(o.shape[-1]) for o in o_list] + [resident(w_out), tile(d), mod_spec(mod_prev)]
        args += list(o_list) + [w_out, x, mod_prev]
        if final_g is not None:
            in_specs.append(row_spec)
            args.append(final_g.reshape(1, d))
        out_specs.append(tile(d))
        out_shape.append(jax.ShapeDtypeStruct((b, t, d), F32))
        scratch.append(pltpu.VMEM(w_out.shape, BF16))
    else:
        in_specs.append(tile(d))
        args.append(x)
    if nxt is not None:
        mod_next, norm_g, w_in, out_list, rope_tables = nxt
        in_specs += [mod_spec(mod_next), row_spec, resident(w_in)]
        args += [mod_next, norm_g.reshape(1, d), w_in]
        rope = rope_tables is not None
        if rope:
            in_specs += [pl.BlockSpec((tm, LANES), lambda i, j: (j, 0))] * 2
            args += list(rope_tables)
        for width, dt, _, _, packed in out_list:
            if packed:
                heads = width // LANES
                out_specs.append(pl.BlockSpec((1, tm * heads, LANES), lambda i, j: (i, j, 0)))
                out_shape.append(jax.ShapeDtypeStruct((b, t * heads, LANES), dt))
            else:
                out_specs.append(tile(width))
                out_shape.append(jax.ShapeDtypeStruct((b, t, width), dt))
        outs = tuple((w_, r, m, p) for w_, _, r, m, p in out_list)
        scratch.append(pltpu.VMEM(w_in.shape, BF16))
    res = pl.pallas_call(
        functools.partial(_stage_kernel, n_o=n_o, final=final_g is not None, outs=outs, rope=rope),
        grid=(b, t // tm),
        in_specs=in_specs,
        out_specs=out_specs,
        out_shape=out_shape,
        scratch_shapes=scratch,
        compiler_params=_params("arbitrary", "arbitrary"),
        name="stage",
    )(*args)
    return (res[0], list(res[1:])) if n_o else (None, list(res))


def _attend(q_blocks, kv_blocks, s_ref, score_fn=None):
    n_rb = len(q_blocks)

    def tile_of(rb, c):
        return (slice(rb * ROW_BLK, (rb + 1) * ROW_BLK), slice(c * KEY_TILE, (c + 1) * KEY_TILE))

    def score_tile(rb, c):
        s_ref[tile_of(rb, c)] = _dot_nt(q_blocks[rb], kv_blocks[rb][0][c])

    def row_max(rb):
        m_acc = None
        for c in range(len(kv_blocks[rb][0])):
            s = s_ref[tile_of(rb, c)]
            adjusted = None if score_fn is None else score_fn(rb, c, s)
            if adjusted is not None:
                s = adjusted
                s_ref[tile_of(rb, c)] = s
            mt = jnp.maximum(s[:, :LANES], s[:, LANES:])
            m_acc = mt if m_acc is None else jnp.maximum(m_acc, mt)
        return m_acc.max(axis=-1, keepdims=True)

    for c in range(len(kv_blocks[0][0])):
        score_tile(0, c)
    outs = []
    for rb in range(n_rb):
        m = row_max(rb)
        n_tiles = len(kv_blocks[rb][0])
        n_next = len(kv_blocks[rb + 1][0]) if rb + 1 < n_rb else 0
        acc = None
        for c in range(max(n_tiles, n_next)):
            if c < n_next:
                score_tile(rb + 1, c)
            if c < n_tiles:
                e = jnp.exp2(s_ref[tile_of(rb, c)] - m)
                pv = _dot(e.astype(BF16), kv_blocks[rb][1][c])
                acc = pv if acc is None else acc + pv
        outs.append(acc)
    return outs


def _lane_low_half():
    return lax.broadcasted_iota(jnp.int32, (1, LANES), 1) < (LANES // 2)


def _values_and_ones(v_tiles):
    ones = jnp.ones((KEY_TILE, LANES), BF16)
    return [jnp.concatenate([v, ones], axis=1) for v in v_tiles]


def _normalise_pair(acc_lo, acc_hi, low):
    return jnp.where(low, acc_lo[:, :LANES] / acc_lo[:, LANES:], acc_hi[:, :LANES] / acc_hi[:, LANES:])


def _split_lane_halves(q, low):
    zero = jnp.zeros_like(q)
    return jnp.where(low, q, zero), jnp.where(low, zero, q)


def _tiles(ref, n_tiles, j):
    return [ref[0, c * KEY_TILE:(c + 1) * KEY_TILE, j:j + LANES].astype(BF16) for c in range(n_tiles)]


def _ctx_attn_kernel(q_ref, k_ref, v_ref, g_ref, o_ref, s_ref):
    low = _lane_low_half()
    tq = q_ref.shape[1]
    n_tiles = k_ref.shape[1] // KEY_TILE
    lane_groups = range(0, q_ref.shape[2], LANES)
    q_blocks, kv_blocks = [], []
    for j in lane_groups:
        kv = (_tiles(k_ref, n_tiles, j), _values_and_ones(_tiles(v_ref, n_tiles, j)))
        for r in range(0, tq, ROW_BLK):
            q_blocks += list(_split_lane_halves(q_ref[0, r:r + ROW_BLK, j:j + LANES], low))
            kv_blocks += [kv, kv]
    res = _attend(q_blocks, kv_blocks, s_ref)
    i = 0
    for j in lane_groups:
        for r in range(0, tq, ROW_BLK):
            o = _normalise_pair(res[i], res[i + 1], low)
            i += 2
            gate = _silu(g_ref[0, r:r + ROW_BLK, j:j + LANES].astype(F32))
            o_ref[0, r:r + ROW_BLK, j:j + LANES] = (o * gate).astype(o_ref.dtype)


def _ctx_attn(q, k, v, g):
    b, t, w = q.shape
    spec = pl.BlockSpec((1, t, w), lambda i: (i, 0, 0))
    return pl.pallas_call(
        _ctx_attn_kernel,
        grid=(b,),
        in_specs=[spec] * 4,
        out_specs=spec,
        out_shape=jax.ShapeDtypeStruct((b, t, w), BF16),
        scratch_shapes=[pltpu.VMEM((2 * t * (w // LANES), k.shape[1]), F32)],
        compiler_params=_params("parallel"),
        name="ctx_attn",
    )(q, k, v, g)


def _na_pair_tiles(rpb_ref, head, pt_ref):
    right = lax.broadcasted_iota(jnp.int32, (GRID_W, LANES), 1) >= GRID_W
    zero_row = jnp.zeros((1, LANES), F32)

    def toeplitz(row, lane0):
        shift = (lane0 - (NA_COLS - 1)) % LANES
        return pltpu.roll(jnp.broadcast_to(row, (GRID_W, LANES)), shift, axis=1, stride=1, stride_axis=0)

    for p in range(NA_PAIR_TILES):
        left_row = rpb_ref[head, p - 1:p, :] if p >= 1 else zero_row
        right_row = rpb_ref[head, p:p + 1, :]
        pt_ref[head, p] = jnp.where(right, toeplitz(right_row, GRID_W), toeplitz(left_row, 0)) * LOG2E


def _na_row_start(r, n_rows):
    kr = min(NA_ROWS_MAX, n_rows)
    return min(max(r - kr // 2, 0), n_rows - kr)


def _na_window(grp, n_rows):
    kr = min(NA_ROWS_MAX, n_rows)
    starts = [_na_row_start(grp * NA_Q_ROWS + i, n_rows) for i in range(NA_Q_ROWS)]
    rows_per_tile = KEY_TILE // GRID_W
    need = -(-(max(starts) + kr - min(starts)) // rows_per_tile) * rows_per_tile
    return min(min(starts), n_rows - need), need


def _na_kernel(q_ref, k_ref, v_ref, ck_ref, cv_ref, rpb_ref, g_ref, o_ref, pt_ref, badd_ref, bcap_ref, s_ref, *,
               n_rows):
    kr = min(NA_ROWS_MAX, n_rows)
    n_grp = n_rows // NA_Q_ROWS
    low = _lane_low_half()

    @pl.when(pl.program_id(1) == 0)
    def _build_bias():
        for half in range(2):
            _na_pair_tiles(rpb_ref, half, pt_ref)
        qcol = lax.broadcasted_iota(jnp.int32, (GRID_W, LANES), 0)
        lane = lax.broadcasted_iota(jnp.int32, (GRID_W, LANES), 1)
        right = lane >= GRID_W
        kcol = lane & (GRID_W - 1)
        col_start = jnp.clip(qcol - NA_COLS // 2, 0, GRID_W - NA_COLS)
        col_ok = (kcol >= col_start) & (kcol < col_start + NA_COLS)
        for grp in range(n_grp):
            krow0, key_rows = _na_window(grp, n_rows)
            for i in range(NA_Q_ROWS):
                r = grp * NA_Q_ROWS + i
                row_start = _na_row_start(r, n_rows)
                for m in range(key_rows // 2):
                    key_row = krow0 + 2 * m
                    ok_l = row_start <= key_row < row_start + kr
                    ok_r = row_start <= key_row + 1 < row_start + kr
                    rows = slice(i * GRID_W, (i + 1) * GRID_W)
                    cols = slice(m * LANES, (m + 1) * LANES)
                    if not (ok_l or ok_r):
                        bcap_ref[grp, rows, cols] = jnp.full((GRID_W, LANES), NEG_INF, F32)
                        for half in range(2):
                            badd_ref[2 * grp + half, rows, cols] = jnp.zeros((GRID_W, LANES), F32)
                        continue
                    ok = col_ok
                    if not ok_r:
                        ok = ok & jnp.logical_not(right)
                    if not ok_l:
                        ok = ok & right
                    bcap_ref[grp, rows, cols] = jnp.where(ok, jnp.inf, NEG_INF)
                    p = key_row - r + NA_ROWS_MAX
                    for half in range(2):
                        badd_ref[2 * grp + half, rows, cols] = jnp.where(ok, pt_ref[half, p], 0.0)

    n_ctx = ck_ref.shape[1] // KEY_TILE
    ctx_k = _tiles(ck_ref, n_ctx, 0)
    ctx_v = _tiles(cv_ref, n_ctx, 0)
    q_blocks, kv_blocks, n_win = [], [], []
    for grp in range(n_grp):
        krow0, key_rows = _na_window(grp, n_rows)
        k0 = krow0 * GRID_W
        win = [slice(k0 + c * KEY_TILE, k0 + (c + 1) * KEY_TILE) for c in range(key_rows * GRID_W // KEY_TILE)]
        kv = ([k_ref[0, w, :] for w in win] + ctx_k, _values_and_ones([v_ref[0, w, :] for w in win] + ctx_v))
        q_blocks += list(_split_lane_halves(q_ref[0, grp * ROW_BLK:(grp + 1) * ROW_BLK, :], low))
        kv_blocks += [kv, kv]
        n_win += [len(win)] * 2

    def masked_bias(rb, c, s):
        if c >= n_win[rb]:
            return None
        cols = slice(c * KEY_TILE, (c + 1) * KEY_TILE)
        return jnp.minimum(s + badd_ref[rb, :, cols], bcap_ref[rb // 2, :, cols])

    res = _attend(q_blocks, kv_blocks, s_ref, masked_bias)
    for grp in range(n_grp):
        o = _normalise_pair(res[2 * grp], res[2 * grp + 1], low)
        rows = slice(grp * ROW_BLK, (grp + 1) * ROW_BLK)
        o_ref[0, rows, :] = (o * _silu(g_ref[0, rows, :].astype(F32))).astype(o_ref.dtype)


def _na_attn(q, k, v, ck, cv, rpb, g):
    heads, rpb_rows, rpb_cols = rpb.shape
    rpb = jnp.pad(rpb, ((0, 0), (0, NA_PAIR_TILES - rpb_rows), (0, LANES - rpb_cols)))
    b, t, w = q.shape
    c = ck.shape[1]
    n_rows = t // GRID_W
    assert NA_Q_ROWS * GRID_W == ROW_BLK and n_rows % NA_Q_ROWS == 0 and n_rows >= NA_K_ROWS
    n_grp = n_rows // NA_Q_ROWS
    n_win = NA_K_ROWS * GRID_W
    tspec = pl.BlockSpec((1, t, LANES), lambda hp, bi: (bi, 0, hp))
    cspec = pl.BlockSpec((1, c, LANES), lambda hp, bi: (bi, 0, hp))
    ptspec = pl.BlockSpec((2, NA_PAIR_TILES, LANES), lambda hp, bi: (hp, 0, 0))
    return pl.pallas_call(
        functools.partial(_na_kernel, n_rows=n_rows),
        grid=(w // LANES, b),
        in_specs=[tspec, tspec, tspec, cspec, cspec, ptspec, tspec],
        out_specs=tspec,
        out_shape=jax.ShapeDtypeStruct((b, t, w), BF16),
        scratch_shapes=[pltpu.VMEM((2, NA_PAIR_TILES, GRID_W, LANES), F32),
                        pltpu.VMEM((2 * n_grp, ROW_BLK, n_win), F32),
                        pltpu.VMEM((n_grp, ROW_BLK, n_win), F32),
                        pltpu.VMEM((2 * t, n_win + c), F32)],
        compiler_params=_params("parallel", "arbitrary"),
        name="na_attn",
    )(q, k, v, ck, cv, rpb, g)


def _softplus(x):
    return jnp.maximum(x, 0.0) + jnp.log(1.0 + jnp.exp(-jnp.abs(x)))


def _lru_kernel(x_ref, gb_ref, h0f_ref, h0b_ref, cw_ref, cb_ref, wg_ref, lam_ref,
                o_ref, ff_ref, fb_ref, xt_s, af_s, uf_s, ab_s, ub_s, hf_s, hb_s):
    nb, t, _ = x_ref.shape
    pad = CONV_PAD_L * nb
    cw = cw_ref[...]
    cb = cb_ref[...]
    wg = wg_ref[0]
    kk = (-0.5 * LRU_C * LOG2E) * _softplus(-lam_ref[...])

    xt_s[0:pad, :] = jnp.zeros((pad, LANES), F32)
    xt_s[pad + t * nb:pad + (t + CONV_PAD_R) * nb, :] = jnp.zeros((CONV_PAD_R * nb, LANES), F32)

    def to_time_major(b, carry):
        xt_s[pl.ds(pad + b, t, stride=nb), :] = x_ref[b]
        return carry

    lax.fori_loop(0, nb, to_time_major, 0)

    rows_per_chunk = LRU_CHUNK * nb
    lane = lax.broadcasted_iota(jnp.int32, (rows_per_chunk, LANES), 1)
    bias_cols = jnp.where(lane < 2, 1.0, 0.0).astype(BF16)

    def gates(c, carry):
        r0 = pl.multiple_of(c * rows_per_chunk, rows_per_chunk)

        def tap(i):
            return xt_s[pl.ds(r0 + i * nb, rows_per_chunk), :]

        x = tap(2)
        xc = cw[0:1] * tap(0) + cw[1:2] * tap(1) + cw[2:3] * x + cw[3:4] * tap(3) + cb
        th = jnp.tanh(_dot(jnp.concatenate([xc.astype(BF16), bias_cols], axis=1), wg))
        xh = 0.5 * xc
        rows = pl.ds(r0, rows_per_chunk)
        for d, (a_s, u_s) in enumerate(((af_s, uf_s), (ab_s, ub_s))):
            t_a = th[:, (2 * d) * LANES:(2 * d + 1) * LANES]
            t_x = th[:, (2 * d + 1) * LANES:(2 * d + 2) * LANES]
            a = jnp.exp2(kk[d:d + 1] * t_a + kk[d:d + 1])
            y = 1.0 - a * a
            sq = y * lax.rsqrt(jnp.maximum(y, 1e-30))
            a_s[rows, :] = a
            u_s[rows, :] = sq * (xh * t_x + xh)
        return carry

    lax.fori_loop(0, t // LRU_CHUNK, gates, 0)

    ts = t // LRU_SEGMENTS
    refs = ((af_s, uf_s), (ab_s, ub_s))

    slab = nb * LRU_UNROLL

    def slab_rows(d, j, i):
        base = (j * ts + i * LRU_UNROLL) * nb
        return pl.ds(pl.multiple_of(base if d == 0 else t * nb - slab - base, slab), slab)

    def step_of(d, block, k):
        kk_ = k if d == 0 else LRU_UNROLL - 1 - k
        return block[kk_ * nb:(kk_ + 1) * nb]

    def compose(i, carry):
        state = list(carry)
        for d in range(2):
            for j in range(LRU_SEGMENTS - 1):
                n = d * (LRU_SEGMENTS - 1) + j
                a_blk = refs[d][0][slab_rows(d, j, i), :]
                u_blk = refs[d][1][slab_rows(d, j, i), :]
                g, big_a = state[n]
                for k in range(LRU_UNROLL):
                    a = step_of(d, a_blk, k)
                    g, big_a = a * g + step_of(d, u_blk, k), a * big_a
                state[n] = (g, big_a)
        return tuple(state)

    ones = jnp.ones((nb, LANES), F32)
    h0 = (h0f_ref[...], h0b_ref[...])
    init = tuple((h0[d] if j == 0 else jnp.zeros((nb, LANES), F32), ones)
                 for d in range(2) for j in range(LRU_SEGMENTS - 1))
    ends = lax.fori_loop(0, ts // LRU_UNROLL, compose, init)
    starts = []
    for d in range(2):
        h = h0[d]
        for j in range(LRU_SEGMENTS):
            starts.append(h)
            if j < LRU_SEGMENTS - 1:
                g, big_a = ends[d * (LRU_SEGMENTS - 1) + j]
                h = g if j == 0 else g + big_a * h

    def rerun(i, carry):
        state = list(carry)
        for d, h_s in enumerate((hf_s, hb_s)):
            for j in range(LRU_SEGMENTS):
                n = d * LRU_SEGMENTS + j
                a_blk = refs[d][0][slab_rows(d, j, i), :]
                u_blk = refs[d][1][slab_rows(d, j, i), :]
                hs = []
                for k in range(LRU_UNROLL):
                    state[n] = step_of(d, a_blk, k) * state[n] + step_of(d, u_blk, k)
                    hs.append(state[n])
                h_s[slab_rows(d, j, i), :] = jnp.concatenate(hs if d == 0 else hs[::-1], axis=0)
        return tuple(state)

    final = lax.fori_loop(0, ts // LRU_UNROLL, rerun, tuple(starts))
    ff_ref[...] = final[LRU_SEGMENTS - 1]
    fb_ref[...] = final[2 * LRU_SEGMENTS - 1]

    def add_directions(c, carry):
        rows = pl.ds(pl.multiple_of(c * rows_per_chunk, rows_per_chunk), rows_per_chunk)
        hf_s[rows, :] = hf_s[rows, :] + hb_s[rows, :]
        return carry

    lax.fori_loop(0, t // LRU_CHUNK, add_directions, 0)

    def emit(b, carry):
        hs = hf_s[pl.ds(b, t, stride=nb), :]
        o_ref[b] = (hs * _silu(gb_ref[b].astype(F32))).astype(o_ref.dtype)
        return carry

    lax.fori_loop(0, nb, emit, 0)


def _lru_branch(xb, gb, h0f, h0b, conv_w, conv_b, wg, lam):
    b, t, w = xb.shape
    nb = SUBLANES
    assert t % (LRU_SEGMENTS * LRU_UNROLL) == 0 and t % LRU_CHUNK == 0
    assert conv_w.shape[0] == CONV_PAD_L + 1 + CONV_PAD_R
    xspec = pl.BlockSpec((nb, t, LANES), lambda i, c: (i, 0, c))
    hspec = pl.BlockSpec((nb, LANES), lambda i, c: (i, c))
    scratch = [pltpu.VMEM(((t + CONV_PAD_L + CONV_PAD_R) * nb, LANES), F32)]
    scratch += [pltpu.VMEM((t * nb, LANES), F32) for _ in range(6)]
    return pl.pallas_call(
        _lru_kernel,
        grid=(b // nb, w // LANES),
        in_specs=[xspec, xspec, hspec, hspec,
                  pl.BlockSpec((conv_w.shape[0], LANES), lambda i, c: (0, c)),
                  pl.BlockSpec((1, LANES), lambda i, c: (0, c)),
                  pl.BlockSpec((1, 2 * LANES, 4 * LANES), lambda i, c: (c, 0, 0)),
                  pl.BlockSpec((2, LANES), lambda i, c: (0, c))],
        out_specs=[xspec, hspec, hspec],
        out_shape=[jax.ShapeDtypeStruct((b, t, w), BF16),
                   jax.ShapeDtypeStruct((b, w), F32),
                   jax.ShapeDtypeStruct((b, w), F32)],
        scratch_shapes=scratch,
        compiler_params=_params("parallel", "parallel"),
        name="lru_branch",
    )(xb, gb, h0f, h0b, conv_w, conv_b.reshape(1, w), wg, lam)


def _lru_gate_weights(wa, ba, wx, bx):
    def tiles(w):
        w2 = w.reshape(LRU_BLOCKS // 2, 2, LRU_BLOCK, LRU_BLOCK)
        z = jnp.zeros_like(w2[:, 0])
        return jnp.concatenate([jnp.concatenate([w2[:, 0], z], axis=2),
                                jnp.concatenate([z, w2[:, 1]], axis=2)], axis=1)
    wg = 0.5 * jnp.concatenate([tiles(wa[0]), tiles(wx[0]), tiles(wa[1]), tiles(wx[1])], axis=2)
    n_tiles = LRU_WIDTH // LANES
    bg = jnp.stack([ba[0], bx[0], ba[1], bx[1]], axis=0).reshape(4, n_tiles, LANES)
    bg = 0.5 * jnp.transpose(bg, (1, 0, 2)).reshape(n_tiles, 1, 4 * LANES)
    hi = bg.astype(BF16)
    lo = (bg - hi.astype(F32)).astype(BF16)
    rest = jnp.zeros((n_tiles, LANES - 2, 4 * LANES), BF16)
    return jnp.concatenate([wg.astype(BF16), hi, lo, rest], axis=1)


def _packed_tiles(ref, n_heads, head):
    n_tiles = ref.shape[1] // (n_heads * KEY_TILE)
    return [ref[0, pl.ds(c * KEY_TILE * n_heads + head, KEY_TILE, stride=n_heads), :].astype(BF16)
            for c in range(n_tiles)]


def _diff_kernel(*refs, packed, n_heads, lam_init):
    n_kv = len(packed)
    q_ref = refs[0]
    kv_refs = refs[1:1 + 2 * n_kv]
    g_ref, lamp_ref, subg_ref, o_ref, s_ref = refs[1 + 2 * n_kv:]
    low = _lane_low_half()
    tq = q_ref.shape[1]
    first_head = pl.program_id(1) * (q_ref.shape[2] // LANES)
    lp = lamp_ref[...]
    lam = (jnp.exp(jnp.sum(lp[0:1] * lp[1:2], axis=-1, keepdims=True))
           - jnp.exp(jnp.sum(lp[2:3] * lp[3:4], axis=-1, keepdims=True)) + lam_init)
    lane_groups = range(0, q_ref.shape[2], LANES)
    q_blocks, kv_blocks = [], []
    for j in lane_groups:
        k_tiles, v_tiles = [], []
        for i in range(n_kv):
            if packed[i]:
                head = first_head + j // LANES
                k_tiles += _packed_tiles(kv_refs[2 * i], n_heads, head)
                v_tiles += _packed_tiles(kv_refs[2 * i + 1], n_heads, head)
            else:
                n_tiles = kv_refs[2 * i].shape[1] // KEY_TILE
                k_tiles += _tiles(kv_refs[2 * i], n_tiles, j)
                v_tiles += _tiles(kv_refs[2 * i + 1], n_tiles, j)
        v_tiles = _values_and_ones(v_tiles)
        for r in range(0, tq, ROW_BLK):
            q_blocks += list(_split_lane_halves(q_ref[0, r:r + ROW_BLK, j:j + LANES], low))
            kv_blocks += [(k_tiles, v_tiles)] * 2
    res = _attend(q_blocks, kv_blocks, s_ref)
    i = 0
    for j in lane_groups:
        for r in range(0, tq, ROW_BLK):
            acc1, acc2 = res[i], res[i + 1]
            i += 2
            o = acc1[:, :LANES] / acc1[:, LANES:] - acc2[:, :LANES] * (lam / acc2[:, LANES:])
            o = o * lax.rsqrt(jnp.mean(o * o, axis=-1, keepdims=True) + EPS) * subg_ref[...] * (1.0 - lam_init)
            gate = _silu(g_ref[0, r:r + ROW_BLK, j:j + LANES].astype(F32))
            o_ref[0, r:r + ROW_BLK, j:j + LANES] = (o * gate).astype(o_ref.dtype)


def _diff_attn(q, kv_list, g, lam_params, sub_g, lam_init, tq, heads_per_step):
    b, t, w = q.shape
    wb = heads_per_step * LANES
    qspec = pl.BlockSpec((1, tq, wb), lambda bi, h, qi: (bi, qi, h))
    n_heads = w // LANES
    in_specs = [qspec]
    args = [q]
    n_keys = 0
    for k, v, packed in kv_list:
        if packed:
            spec = pl.BlockSpec((1, k.shape[1], LANES), lambda bi, h, qi: (bi, 0, 0))
            n_keys += k.shape[1] // n_heads
        else:
            spec = pl.BlockSpec((1, k.shape[1], wb), lambda bi, h, qi: (bi, 0, h))
            n_keys += k.shape[1]
        in_specs += [spec, spec]
        args += [k, v]
    in_specs += [qspec,
                 pl.BlockSpec(lam_params.shape, lambda bi, h, qi: (0, 0)),
                 pl.BlockSpec((1, LANES), lambda bi, h, qi: (0, 0))]
    args += [g, lam_params, sub_g.reshape(1, LANES)]
    return pl.pallas_call(
        functools.partial(_diff_kernel, packed=tuple(p for _, _, p in kv_list), n_heads=n_heads,
                          lam_init=lam_init),
        grid=(b, w // wb, t // tq),
        in_specs=in_specs,
        out_specs=qspec,
        out_shape=jax.ShapeDtypeStruct((b, t, w), BF16),
        scratch_shapes=[pltpu.VMEM((heads_per_step * 2 * tq, n_keys), F32)],
        compiler_params=_params("parallel", "parallel", "parallel"),
        name="diff_attn",
    )(*args)


def _rope_tables(t_len):
    m = DIFF_HEAD_DIM // 4
    t = jnp.arange(t_len)
    rows = (t // GRID_W).astype(F32)
    cols = (t % GRID_W).astype(F32)
    inv = ROPE_BASE ** (-jnp.arange(m, dtype=F32) / m)
    ang_r = rows[:, None] * inv[None, :]
    ang_c = cols[:, None] * inv[None, :]
    cos = jnp.concatenate([jnp.cos(ang_r)] * 2 + [jnp.cos(ang_c)] * 2, axis=1)
    sin = jnp.concatenate([-jnp.sin(ang_r), jnp.sin(ang_r), -jnp.sin(ang_c), jnp.sin(ang_c)], axis=1)
    reps = LANES // DIFF_HEAD_DIM
    return jnp.tile(cos, (1, reps)), jnp.tile(sin, (1, reps))


def kernel(x_prompt, x_sample, c, cache_na_k, cache_na_v, state_lru, cache_diff_k, cache_diff_v, c_ctx,
           e_norm, e_ada_w, e_ada_b, e_w_in, e_rpb, e_conv_w, e_conv_b, e_lru_wa, e_lru_ba, e_lru_wx, e_lru_bx,
           e_lru_lam, e_w_out, o_norm, o_ada_w, o_ada_b, o_w_in, o_lq1, o_lk1, o_lq2, o_lk2, o_sub_g, o_w_out,
           final_norm):
    bp, tp, d = x_prompt.shape
    bs, ts, _ = x_sample.shape
    past = cache_na_k.shape[2]
    pad_rows = 2 * SUBLANES - bs - 1
    cc = jnp.concatenate([c, c_ctx[None, :], jnp.zeros((pad_rows, d), F32)], axis=0)
    nw, lw, dw = NA_WIDTH, LRU_WIDTH, DIFF_WIDTH
    rope_tables = _rope_tables(ts) if DEPTH > 1 else None

    def layer_params(i):
        j = i // 2
        if i % 2 == 0:
            return e_ada_w[j], e_ada_b[j], e_norm[j], e_w_in[j], e_w_out[j]
        return o_ada_w[j], o_ada_b[j], o_norm[j], o_w_in[j], o_w_out[j]

    mods = []
    for i in range(DEPTH):
        ada_w, ada_b = layer_params(i)[:2]
        m = _modulation(cc, ada_w, ada_b)
        mods.append((m[:bs].reshape(bs, 1, 3 * d), m[bs:bs + 1].reshape(1, 1, 3 * d)))

    def projection(i, latent):
        norm_g, w_in = layer_params(i)[2:4]
        mod = mods[i][0 if latent else 1]
        kv_dtype = BF16 if latent else F32
        if i % 2 == 0:
            qmul = NA_HEAD_DIM ** -0.5 * LOG2E
            outs = [(nw, BF16, False, qmul, False), (nw, kv_dtype, False, None, False),
                    (nw, kv_dtype, False, None, False), (nw, BF16, False, None, False),
                    (lw, F32, False, None, False), (lw, BF16, False, None, False)]
            return mod, norm_g, w_in, outs, None
        qmul = DIFF_HEAD_DIM ** -0.5 * LOG2E
        outs = [(dw, BF16, latent, qmul, False), (dw, kv_dtype, latent, None, not latent),
                (dw, kv_dtype, False, None, not latent), (dw, BF16, False, None, False)]
        return mod, norm_g, w_in, outs, rope_tables if latent else None

    xp, xs = x_prompt, x_sample
    _, proj_p = _stage(xp, nxt=projection(0, False), tm=512)
    _, proj_s = _stage(xs, nxt=projection(0, True), tm=512)
    na_k, na_v, lru_s, df_k, df_v = [], [], [], [], []
    for i in range(DEPTH):
        j = i // 2
        w_out = layer_params(i)[4]
        if i % 2 == 0:
            wg = _lru_gate_weights(e_lru_wa[j], e_lru_ba[j], e_lru_wx[j], e_lru_bx[j])

            q, k, v, ga, xb, gb = proj_p
            oa = _ctx_attn(q, k, v, ga)
            zeros = jnp.zeros((bp, lw), F32)
            ob, ff, fb = _lru_branch(xb, gb, zeros, zeros, e_conv_w[j], e_conv_b[j], wg, e_lru_lam[j])
            mixed_p = [oa, ob]
            na_k.append(k.reshape(bp, tp, NA_HEADS, NA_HEAD_DIM))
            na_v.append(v.reshape(bp, tp, NA_HEADS, NA_HEAD_DIM))
            lru_s.append(jnp.stack([ff, fb], axis=1))

            q, k, v, ga, xb, gb = proj_s
            ck = cache_na_k[:, j].reshape(bs, past, nw)
            cv = cache_na_v[:, j].reshape(bs, past, nw)
            oa = _na_attn(q, k, v, ck, cv, e_rpb[j], ga)
            ob, _, _ = _lru_branch(xb, gb, state_lru[:, j, 0], state_lru[:, j, 1],
                                   e_conv_w[j], e_conv_b[j], wg, e_lru_lam[j])
            mixed_s = [oa, ob]
        else:
            lam_init = 0.8 - 0.6 * math.exp(-0.3 * i)
            lam_params = jnp.stack([o_lq1[j], o_lk1[j], o_lq2[j], o_lk2[j]], axis=0)

            q, k, v, g = proj_p
            mixed_p = [_diff_attn(q, [(k, v, True)], g, lam_params, o_sub_g[j], lam_init, tq=tp,
                                  heads_per_step=DIFF_HEADS)]
            df_k.append(k.reshape(bp, tp, DIFF_HEADS, 2 * DIFF_HEAD_DIM))
            df_v.append(v.reshape(bp, tp, DIFF_HEADS, 2 * DIFF_HEAD_DIM))

            q, k, v, g = proj_s
            ck = cache_diff_k[:, j].reshape(bs, past * DIFF_HEADS, 2 * DIFF_HEAD_DIM)
            cv = cache_diff_v[:, j].reshape(bs, past * DIFF_HEADS, 2 * DIFF_HEAD_DIM)
            mixed_s = [_diff_attn(q, [(ck, cv, True), (k, v, False)], g, lam_params, o_sub_g[j], lam_init, tq=ts,
                                  heads_per_step=1)]

        if i == DEPTH - 1:
            xp, _ = _stage(xp, prev=(mixed_p, w_out, mods[i][1]), final_g=final_norm, tm=512)
            xs, _ = _stage(xs, prev=(mixed_s, w_out, mods[i][0]), final_g=final_norm, tm=512)
        else:
            xp, proj_p = _stage(xp, prev=(mixed_p, w_out, mods[i][1]), nxt=projection(i + 1, False), tm=512)
            xs, proj_s = _stage(xs, prev=(mixed_s, w_out, mods[i][0]), nxt=projection(i + 1, True), tm=512)

    return (xp, xs, jnp.stack(na_k, axis=1), jnp.stack(na_v, axis=1), jnp.stack(lru_s, axis=1),
            jnp.stack(df_k, axis=1), jnp.stack(df_v, axis=1))
```

```python
import functools
import math

import jax
import jax.numpy as jnp
from jax import lax
from jax.experimental import pallas as pl
from jax.experimental.pallas import tpu as pltpu

D_MODEL = 1024
DEPTH = 2
GRID_W = 64
NA_HEADS = 8
NA_HEAD_DIM = 64
NA_WIDTH = NA_HEADS * NA_HEAD_DIM
NA_ROWS_MAX = 8
NA_COLS = 16
RPB_ROWS = 2 * NA_ROWS_MAX - 1
RPB_COLS = 2 * NA_COLS - 1
LRU_WIDTH = 512
LRU_BLOCKS = 8
LRU_BLOCK = LRU_WIDTH // LRU_BLOCKS
LRU_C = 8.0
DIFF_HEADS = 8
DIFF_HEAD_DIM = 64
DIFF_WIDTH = DIFF_HEADS * 2 * DIFF_HEAD_DIM
ROPE_BASE = 10000.0
EPS = 1e-6
NEG_INF = -1e30
LOG2E = math.log2(math.e)

LANES = 128
SUBLANES = 8
MXU_DIM = 256
VMEM_LIMIT = 56 * 1024 * 1024

CAST_COLS = 512
LRU_SEGMENTS = 4
LRU_UNROLL = 8
LRU_CHUNK = 128
CONV_PAD_L = 2
CONV_PAD_R = 1
ROW_BLK = MXU_DIM
KEY_TILE = MXU_DIM
SCORE_SLOTS = 4

NA_Q_ROWS = 4
NA_K_ROWS = 12
NA_PAIR_TILES = 16
NA_PAIRS_PER_STEP = 2

F32 = jnp.float32
BF16 = jnp.bfloat16


def _params(*semantics):
    return pltpu.CompilerParams(dimension_semantics=semantics, vmem_limit_bytes=VMEM_LIMIT)


def _sigmoid(x):
    return 0.5 * jnp.tanh(0.5 * x) + 0.5


def _silu(x):
    return x * _sigmoid(x)


def _dot(a, b):
    return jnp.dot(a, b, preferred_element_type=F32)


def _dot_nt(a, b):
    return lax.dot_general(a, b, (((1,), (1,)), ((), ())), preferred_element_type=F32)


def _mod_kernel(c_ref, w_ref, b_ref, o_ref):
    s = _silu(c_ref[...])
    o_ref[...] = _dot(s.astype(BF16), w_ref[...].astype(BF16)) + b_ref[...]


def _modulation(cc, w, b):
    rows, d = cc.shape
    n = w.shape[1]
    tn = 1024
    return pl.pallas_call(
        _mod_kernel,
        grid=(n // tn,),
        in_specs=[pl.BlockSpec((rows, d), lambda j: (0, 0)),
                  pl.BlockSpec((d, tn), lambda j: (0, j)),
                  pl.BlockSpec((1, tn), lambda j: (0, j))],
        out_specs=pl.BlockSpec((rows, tn), lambda j: (0, j)),
        out_shape=jax.ShapeDtypeStruct((rows, n), F32),
        compiler_params=_params("arbitrary"),
        name="modulation",
    )(cc, w, b.reshape(1, n))


def _rope_block(x, cos, sin_signed, low16):
    partner = jnp.where(low16, pltpu.roll(x, LANES - 16, axis=1), pltpu.roll(x, 16, axis=1))
    return x * cos + partner * sin_signed


def _cast_weights_once(w_ref, wbf_ref):
    @pl.when((pl.program_id(0) == 0) & (pl.program_id(1) == 0))
    def _():
        for off in range(0, w_ref.shape[1], CAST_COLS):
            wbf_ref[:, off:off + CAST_COLS] = w_ref[:, off:off + CAST_COLS].astype(BF16)


def _norm_project(x, mod_ref, g_ref, wbf_ref, rope_refs, out_refs, outs):
    tm, d = x.shape
    y = x * lax.rsqrt(jnp.mean(x * x, axis=-1, keepdims=True) + EPS) * g_ref[...]
    mod = mod_ref[0]
    h = (y * (1.0 + mod[:, d:2 * d]) + mod[:, :d]).astype(BF16)
    if rope_refs:
        cos = rope_refs[0][...]
        sin = rope_refs[1][...]
        lane = lax.broadcasted_iota(jnp.int32, (1, LANES), 1)
        low16 = (lane & 16) == 0
    off = 0
    for o_ref, (width, roped, mul, packed) in zip(out_refs, outs):
        acc = _dot(h, wbf_ref[:, off:off + width])
        if roped:
            acc = jnp.concatenate(
                [_rope_block(acc[:, j:j + LANES], cos, sin, low16) for j in range(0, width, LANES)], axis=1)
        if mul is not None:
            acc = acc * mul
        if packed:
            n_heads = width // LANES
            for hh in range(n_heads):
                o_ref[0, pl.ds(hh, tm, stride=n_heads), :] = acc[:, hh * LANES:(hh + 1) * LANES].astype(o_ref.dtype)
        else:
            o_ref[0] = acc.astype(o_ref.dtype)
        off += width


def _gated_residual(o_refs, wbf_ref, x_ref, mod_ref):
    d = x_ref.shape[-1]
    acc, row = None, 0
    for o_ref in o_refs:
        k = o_ref.shape[-1]
        part = _dot(o_ref[0], wbf_ref[row:row + k, :])
        acc = part if acc is None else acc + part
        row += k
    return x_ref[0] + mod_ref[0][:, 2 * d:] * acc


def _stage_kernel(*refs, n_o, final, outs, rope):
    refs = list(refs)
    take = lambda n: [refs.pop(0) for _ in range(n)]
    if n_o:
        o_refs = take(n_o)
        wout_ref, x_ref, mod_prev_ref = take(3)
        fin_ref = take(1)[0] if final else None
    else:
        x_ref, = take(1)
    if outs:
        mod_ref, g_ref, win_ref = take(3)
        rope_refs = take(2) if rope else []
    xout_ref = take(1)[0] if n_o else None
    out_refs = take(len(outs))
    if n_o:
        wout_bf = take(1)[0]
        _cast_weights_once(wout_ref, wout_bf)
        x = _gated_residual(o_refs, wout_bf, x_ref, mod_prev_ref)
        if final:
            xout_ref[0] = x * lax.rsqrt(jnp.mean(x * x, axis=-1, keepdims=True) + EPS) * fin_ref[...]
        else:
            xout_ref[0] = x
    else:
        x = x_ref[0]
    if outs:
        win_bf = take(1)[0]
        _cast_weights_once(win_ref, win_bf)
        _norm_project(x, mod_ref, g_ref, win_bf, rope_refs, out_refs, outs)


def _stage(x, prev=None, nxt=None, final_g=None, tm=256):
    b, t, d = x.shape
    tm = min(tm, t)
    tile = lambda width: pl.BlockSpec((1, tm, width), lambda i, j: (i, j, 0))
    mod_spec = lambda mod: pl.BlockSpec(
        (1, 1, 3 * d), (lambda i, j: (i, 0, 0)) if mod.shape[0] > 1 else (lambda i, j: (0, 0, 0)))
    resident = lambda w: pl.BlockSpec(w.shape, lambda i, j: (0, 0), pipeline_mode=pl.Buffered(1))
    row_spec = pl.BlockSpec((1, d), lambda i, j: (0, 0))
    in_specs, args, out_specs, out_shape, scratch = [], [], [], [], []
    n_o, outs, rope = 0, (), False
    if prev is not None:
        o_list, w_out, mod_prev = prev
        n_o = len(o_list)
        in_specs += [tile(o.shape[-1]) for o in o_list] + [resident(w_out), tile(d), mod_spec(mod_prev)]
        args += list(o_list) + [w_out, x, mod_prev]
        if final_g is not None:
            in_specs.append(row_spec)
            args.append(final_g.reshape(1, d))
        out_specs.append(tile(d))
        out_shape.append(jax.ShapeDtypeStruct((b, t, d), F32))
        scratch.append(pltpu.VMEM(w_out.shape, BF16))
    else:
        in_specs.append(tile(d))
        args.append(x)
    if nxt is not None:
        mod_next, norm_g, w_in, out_list, rope_tables = nxt
        in_specs += [mod_spec(mod_next), row_spec, resident(w_in)]
        args += [mod_next, norm_g.reshape(1, d), w_in]
        rope = rope_tables is not None
        if rope:
            in_specs += [pl.BlockSpec((tm, LANES), lambda i, j: (j, 0))] * 2
            args += list(rope_tables)
        for width, dt, _, _, packed in out_list:
            if packed:
                heads = width // LANES
                out_specs.append(pl.BlockSpec((1, tm * heads, LANES), lambda i, j: (i, j, 0)))
                out_shape.append(jax.ShapeDtypeStruct((b, t * heads, LANES), dt))
            else:
                out_specs.append(tile(width))
                out_shape.append(jax.ShapeDtypeStruct((b, t, width), dt))
        outs = tuple((w_, r, m, p) for w_, _, r, m, p in out_list)
        scratch.append(pltpu.VMEM(w_in.shape, BF16))
    res = pl.pallas_call(
        functools.partial(_stage_kernel, n_o=n_o, final=final_g is not None, outs=outs, rope=rope),
        grid=(b, t // tm),
        in_specs=in_specs,
        out_specs=out_specs,
        out_shape=out_shape,
        scratch_shapes=scratch,
        compiler_params=_params("arbitrary", "arbitrary"),
        name="stage",
    )(*args)
    return (res[0], list(res[1:])) if n_o else (None, list(res))


def _attend(q_blocks, kv_blocks, s_ref, score_fn=None):
    n_rb = len(q_blocks)

    def tile_of(rb, c):
        slot = rb % SCORE_SLOTS
        return (slice(slot * ROW_BLK, (slot + 1) * ROW_BLK), slice(c * KEY_TILE, (c + 1) * KEY_TILE))

    def score_tile(rb, c):
        s_ref[tile_of(rb, c)] = _dot_nt(q_blocks[rb], kv_blocks[rb][0][c])

    def max_tile(rb, c, m_acc):
        s = s_ref[tile_of(rb, c)]
        adjusted = None if score_fn is None else score_fn(rb, c, s)
        if adjusted is not None:
            s = adjusted
            s_ref[tile_of(rb, c)] = s
        mt = jnp.maximum(s[:, :LANES], s[:, LANES:])
        return mt if m_acc is None else jnp.maximum(m_acc, mt)

    def n_tiles(rb):
        return len(kv_blocks[rb][0]) if 0 <= rb < n_rb else 0

    outs = []
    row_max = {}
    for step in range(n_rb + 2):
        rb_score, rb_max, rb_exp = step, step - 1, step - 2
        m_acc, acc = None, None
        for c in range(max(n_tiles(rb_score), n_tiles(rb_max), n_tiles(rb_exp))):
            if c < n_tiles(rb_score):
                score_tile(rb_score, c)
            if c < n_tiles(rb_max):
                m_acc = max_tile(rb_max, c, m_acc)
            if c < n_tiles(rb_exp):
                e = jnp.exp2(s_ref[tile_of(rb_exp, c)] - row_max[rb_exp])
                pv = _dot(e.astype(BF16), kv_blocks[rb_exp][1][c])
                acc = pv if acc is None else acc + pv
        if m_acc is not None:
            row_max[rb_max] = m_acc.max(axis=-1, keepdims=True)
        if acc is not None:
            outs.append(acc)
    return outs


def _lane_low_half():
    return lax.broadcasted_iota(jnp.int32, (1, LANES), 1) < (LANES // 2)


def _values_and_ones(v_tiles):
    ones = jnp.ones((KEY_TILE, LANES), BF16)
    return [jnp.concatenate([v, ones], axis=1) for v in v_tiles]


def _normalise_pair(acc_lo, acc_hi, low):
    return jnp.where(low, acc_lo[:, :LANES] / acc_lo[:, LANES:], acc_hi[:, :LANES] / acc_hi[:, LANES:])


def _split_lane_halves(q, low):
    zero = jnp.zeros_like(q)
    return jnp.where(low, q, zero), jnp.where(low, zero, q)


def _tiles(ref, n_tiles, j):
    return [ref[0, c * KEY_TILE:(c + 1) * KEY_TILE, j:j + LANES].astype(BF16) for c in range(n_tiles)]


def _ctx_attn_kernel(q_ref, k_ref, v_ref, g_ref, o_ref, s_ref):
    low = _lane_low_half()
    tq = q_ref.shape[1]
    n_tiles = k_ref.shape[1] // KEY_TILE
    lane_groups = range(0, q_ref.shape[2], LANES)
    q_blocks, kv_blocks = [], []
    for j in lane_groups:
        kv = (_tiles(k_ref, n_tiles, j), _values_and_ones(_tiles(v_ref, n_tiles, j)))
        for r in range(0, tq, ROW_BLK):
            q_blocks += list(_split_lane_halves(q_ref[0, r:r + ROW_BLK, j:j + LANES], low))
            kv_blocks += [kv, kv]
    res = _attend(q_blocks, kv_blocks, s_ref)
    i = 0
    for j in lane_groups:
        for r in range(0, tq, ROW_BLK):
            o = _normalise_pair(res[i], res[i + 1], low)
            i += 2
            gate = _silu(g_ref[0, r:r + ROW_BLK, j:j + LANES].astype(F32))
            o_ref[0, r:r + ROW_BLK, j:j + LANES] = (o * gate).astype(o_ref.dtype)


def _ctx_attn(q, k, v, g):
    b, t, w = q.shape
    spec = pl.BlockSpec((1, t, w), lambda i: (i, 0, 0))
    return pl.pallas_call(
        _ctx_attn_kernel,
        grid=(b,),
        in_specs=[spec] * 4,
        out_specs=spec,
        out_shape=jax.ShapeDtypeStruct((b, t, w), BF16),
        scratch_shapes=[pltpu.VMEM((SCORE_SLOTS * ROW_BLK, k.shape[1]), F32)],
        compiler_params=_params("parallel"),
        name="ctx_attn",
    )(q, k, v, g)


def _na_pair_tiles(rpb_ref, head, pt_ref):
    right = lax.broadcasted_iota(jnp.int32, (GRID_W, LANES), 1) >= GRID_W
    zero_row = jnp.zeros((1, LANES), F32)

    def toeplitz(row, lane0):
        shift = (lane0 - (NA_COLS - 1)) % LANES
        return pltpu.roll(jnp.broadcast_to(row, (GRID_W, LANES)), shift, axis=1, stride=1, stride_axis=0)

    for p in range(NA_PAIR_TILES):
        left_row = rpb_ref[head, p - 1:p, :] if p >= 1 else zero_row
        right_row = rpb_ref[head, p:p + 1, :]
        pt_ref[head, p] = jnp.where(right, toeplitz(right_row, GRID_W), toeplitz(left_row, 0)) * LOG2E


def _na_row_start(r, n_rows):
    kr = min(NA_ROWS_MAX, n_rows)
    return min(max(r - kr // 2, 0), n_rows - kr)


def _na_window(grp, n_rows):
    kr = min(NA_ROWS_MAX, n_rows)
    starts = [_na_row_start(grp * NA_Q_ROWS + i, n_rows) for i in range(NA_Q_ROWS)]
    rows_per_tile = KEY_TILE // GRID_W
    need = -(-(max(starts) + kr - min(starts)) // rows_per_tile) * rows_per_tile
    return min(min(starts), n_rows - need), need


def _na_kernel(q_ref, k_ref, v_ref, ck_ref, cv_ref, rpb_ref, g_ref, o_ref, pt_ref, badd_ref, bcap_ref, s_ref, *,
               n_rows):
    kr = min(NA_ROWS_MAX, n_rows)
    n_grp = n_rows // NA_Q_ROWS
    n_pairs = q_ref.shape[2] // LANES
    low = _lane_low_half()

    @pl.when(pl.program_id(1) == 0)
    def _build_bias():
        for head in range(2 * n_pairs):
            _na_pair_tiles(rpb_ref, head, pt_ref)
        qcol = lax.broadcasted_iota(jnp.int32, (GRID_W, LANES), 0)
        lane = lax.broadcasted_iota(jnp.int32, (GRID_W, LANES), 1)
        right = lane >= GRID_W
        kcol = lane & (GRID_W - 1)
        col_start = jnp.clip(qcol - NA_COLS // 2, 0, GRID_W - NA_COLS)
        col_ok = (kcol >= col_start) & (kcol < col_start + NA_COLS)
        for grp in range(n_grp):
            krow0, key_rows = _na_window(grp, n_rows)
            for i in range(NA_Q_ROWS):
                r = grp * NA_Q_ROWS + i
                row_start = _na_row_start(r, n_rows)
                for m in range(key_rows // 2):
                    key_row = krow0 + 2 * m
                    ok_l = row_start <= key_row < row_start + kr
                    ok_r = row_start <= key_row + 1 < row_start + kr
                    rows = slice(i * GRID_W, (i + 1) * GRID_W)
                    cols = slice(m * LANES, (m + 1) * LANES)
                    if not (ok_l or ok_r):
                        bcap_ref[grp, rows, cols] = jnp.full((GRID_W, LANES), NEG_INF, F32)
                        for head in range(2 * n_pairs):
                            badd_ref[head * n_grp + grp, rows, cols] = jnp.zeros((GRID_W, LANES), F32)
                        continue
                    ok = col_ok
                    if not ok_r:
                        ok = ok & jnp.logical_not(right)
                    if not ok_l:
                        ok = ok & right
                    bcap_ref[grp, rows, cols] = jnp.where(ok, jnp.inf, NEG_INF)
                    p = key_row - r + NA_ROWS_MAX
                    for head in range(2 * n_pairs):
                        badd_ref[head * n_grp + grp, rows, cols] = jnp.where(ok, pt_ref[head, p], 0.0)

    n_ctx = ck_ref.shape[1] // KEY_TILE
    q_blocks, kv_blocks, blocks = [], [], []
    for pair in range(n_pairs):
        lanes = slice(pair * LANES, (pair + 1) * LANES)
        ctx_k = _tiles(ck_ref, n_ctx, pair * LANES)
        ctx_v = _tiles(cv_ref, n_ctx, pair * LANES)
        for grp in range(n_grp):
            krow0, key_rows = _na_window(grp, n_rows)
            k0 = krow0 * GRID_W
            win = [slice(k0 + c * KEY_TILE, k0 + (c + 1) * KEY_TILE) for c in range(key_rows * GRID_W // KEY_TILE)]
            kv = ([k_ref[0, w, lanes] for w in win] + ctx_k,
                  _values_and_ones([v_ref[0, w, lanes] for w in win] + ctx_v))
            q_blocks += list(_split_lane_halves(q_ref[0, grp * ROW_BLK:(grp + 1) * ROW_BLK, lanes], low))
            kv_blocks += [kv, kv]
            blocks += [(2 * pair, grp, len(win)), (2 * pair + 1, grp, len(win))]

    def masked_bias(rb, c, s):
        head, grp, n_win = blocks[rb]
        if c >= n_win:
            return None
        cols = slice(c * KEY_TILE, (c + 1) * KEY_TILE)
        return jnp.minimum(s + badd_ref[head * n_grp + grp, :, cols], bcap_ref[grp, :, cols])

    res = _attend(q_blocks, kv_blocks, s_ref, masked_bias)
    for pair in range(n_pairs):
        lanes = slice(pair * LANES, (pair + 1) * LANES)
        for grp in range(n_grp):
            rb = 2 * (pair * n_grp + grp)
            o = _normalise_pair(res[rb], res[rb + 1], low)
            rows = slice(grp * ROW_BLK, (grp + 1) * ROW_BLK)
            o_ref[0, rows, lanes] = (o * _silu(g_ref[0, rows, lanes].astype(F32))).astype(o_ref.dtype)


def _na_attn(q, k, v, ck, cv, rpb, g):
    heads, rpb_rows, rpb_cols = rpb.shape
    rpb = jnp.pad(rpb, ((0, 0), (0, NA_PAIR_TILES - rpb_rows), (0, LANES - rpb_cols)))
    b, t, w = q.shape
    c = ck.shape[1]
    n_rows = t // GRID_W
    assert NA_Q_ROWS * GRID_W == ROW_BLK and n_rows % NA_Q_ROWS == 0 and n_rows >= NA_K_ROWS
    n_grp = n_rows // NA_Q_ROWS
    n_win = NA_K_ROWS * GRID_W
    wb = NA_PAIRS_PER_STEP * LANES
    hb = 2 * NA_PAIRS_PER_STEP
    tspec = pl.BlockSpec((1, t, wb), lambda hp, bi: (bi, 0, hp))
    cspec = pl.BlockSpec((1, c, wb), lambda hp, bi: (bi, 0, hp))
    ptspec = pl.BlockSpec((hb, NA_PAIR_TILES, LANES), lambda hp, bi: (hp, 0, 0))
    return pl.pallas_call(
        functools.partial(_na_kernel, n_rows=n_rows),
        grid=(w // wb, b),
        in_specs=[tspec, tspec, tspec, cspec, cspec, ptspec, tspec],
        out_specs=tspec,
        out_shape=jax.ShapeDtypeStruct((b, t, w), BF16),
        scratch_shapes=[pltpu.VMEM((hb, NA_PAIR_TILES, GRID_W, LANES), F32),
                        pltpu.VMEM((hb * n_grp, ROW_BLK, n_win), F32),
                        pltpu.VMEM((n_grp, ROW_BLK, n_win), F32),
                        pltpu.VMEM((SCORE_SLOTS * ROW_BLK, n_win + c), F32)],
        compiler_params=_params("parallel", "arbitrary"),
        name="na_attn",
    )(q, k, v, ck, cv, rpb, g)


def _softplus(x):
    return jnp.maximum(x, 0.0) + jnp.log(1.0 + jnp.exp(-jnp.abs(x)))


def _lru_kernel(x_ref, gb_ref, h0f_ref, h0b_ref, cw_ref, cb_ref, wg_ref, lam_ref,
                o_ref, ff_ref, fb_ref, xt_s, af_s, uf_s, ab_s, ub_s, hf_s, hb_s):
    nb, t, _ = x_ref.shape
    pad = CONV_PAD_L * nb
    cw = cw_ref[...]
    cb = cb_ref[...]
    wg = wg_ref[0]
    kk = (-0.5 * LRU_C * LOG2E) * _softplus(-lam_ref[...])

    xt_s[0:pad, :] = jnp.zeros((pad, LANES), F32)
    xt_s[pad + t * nb:pad + (t + CONV_PAD_R) * nb, :] = jnp.zeros((CONV_PAD_R * nb, LANES), F32)

    def to_time_major(b, carry):
        xt_s[pl.ds(pad + b, t, stride=nb), :] = x_ref[b]
        return carry

    lax.fori_loop(0, nb, to_time_major, 0)

    rows_per_chunk = LRU_CHUNK * nb
    lane = lax.broadcasted_iota(jnp.int32, (rows_per_chunk, LANES), 1)
    bias_cols = jnp.where(lane < 2, 1.0, 0.0).astype(BF16)

    def gates(c, carry):
        r0 = pl.multiple_of(c * rows_per_chunk, rows_per_chunk)

        def tap(i):
            return xt_s[pl.ds(r0 + i * nb, rows_per_chunk), :]

        x = tap(2)
        xc = cw[0:1] * tap(0) + cw[1:2] * tap(1) + cw[2:3] * x + cw[3:4] * tap(3) + cb
        th = jnp.tanh(_dot(jnp.concatenate([xc.astype(BF16), bias_cols], axis=1), wg))
        xh = 0.5 * xc
        rows = pl.ds(r0, rows_per_chunk)
        for d, (a_s, u_s) in enumerate(((af_s, uf_s), (ab_s, ub_s))):
            t_a = th[:, (2 * d) * LANES:(2 * d + 1) * LANES]
            t_x = th[:, (2 * d + 1) * LANES:(2 * d + 2) * LANES]
            a = jnp.exp2(kk[d:d + 1] * t_a + kk[d:d + 1])
            y = 1.0 - a * a
            sq = y * lax.rsqrt(jnp.maximum(y, 1e-30))
            a_s[rows, :] = a
            u_s[rows, :] = sq * (xh * t_x + xh)
        return carry

    lax.fori_loop(0, t // LRU_CHUNK, gates, 0)

    ts = t // LRU_SEGMENTS
    refs = ((af_s, uf_s), (ab_s, ub_s))

    slab = nb * LRU_UNROLL

    def slab_rows(d, j, i):
        base = (j * ts + i * LRU_UNROLL) * nb
        return pl.ds(pl.multiple_of(base if d == 0 else t * nb - slab - base, slab), slab)

    def step_of(d, block, k):
        kk_ = k if d == 0 else LRU_UNROLL - 1 - k
        return block[kk_ * nb:(kk_ + 1) * nb]

    def compose(i, carry):
        state = list(carry)
        for d in range(2):
            for j in range(LRU_SEGMENTS - 1):
                n = d * (LRU_SEGMENTS - 1) + j
                a_blk = refs[d][0][slab_rows(d, j, i), :]
                u_blk = refs[d][1][slab_rows(d, j, i), :]
                g, big_a = state[n]
                for k in range(LRU_UNROLL):
                    a = step_of(d, a_blk, k)
                    g, big_a = a * g + step_of(d, u_blk, k), a * big_a
                state[n] = (g, big_a)
        return tuple(state)

    ones = jnp.ones((nb, LANES), F32)
    h0 = (h0f_ref[...], h0b_ref[...])
    init = tuple((h0[d] if j == 0 else jnp.zeros((nb, LANES), F32), ones)
                 for d in range(2) for j in range(LRU_SEGMENTS - 1))
    ends = lax.fori_loop(0, ts // LRU_UNROLL, compose, init)
    starts = []
    for d in range(2):
        h = h0[d]
        for j in range(LRU_SEGMENTS):
            starts.append(h)
            if j < LRU_SEGMENTS - 1:
                g, big_a = ends[d * (LRU_SEGMENTS - 1) + j]
                h = g if j == 0 else g + big_a * h

    def rerun(i, carry):
        state = list(carry)
        for d, h_s in enumerate((hf_s, hb_s)):
            for j in range(LRU_SEGMENTS):
                n = d * LRU_SEGMENTS + j
                a_blk = refs[d][0][slab_rows(d, j, i), :]
                u_blk = refs[d][1][slab_rows(d, j, i), :]
                hs = []
                for k in range(LRU_UNROLL):
                    state[n] = step_of(d, a_blk, k) * state[n] + step_of(d, u_blk, k)
                    hs.append(state[n])
                h_s[slab_rows(d, j, i), :] = jnp.concatenate(hs if d == 0 else hs[::-1], axis=0)
        return tuple(state)

    final = lax.fori_loop(0, ts // LRU_UNROLL, rerun, tuple(starts))
    ff_ref[...] = final[LRU_SEGMENTS - 1]
    fb_ref[...] = final[2 * LRU_SEGMENTS - 1]

    def add_directions(c, carry):
        rows = pl.ds(pl.multiple_of(c * rows_per_chunk, rows_per_chunk), rows_per_chunk)
        hf_s[rows, :] = hf_s[rows, :] + hb_s[rows, :]
        return carry

    lax.fori_loop(0, t // LRU_CHUNK, add_directions, 0)

    def emit(b, carry):
        hs = hf_s[pl.ds(b, t, stride=nb), :]
        o_ref[b] = (hs * _silu(gb_ref[b].astype(F32))).astype(o_ref.dtype)
        return carry

    lax.fori_loop(0, nb, emit, 0)


def _lru_branch(xb, gb, h0f, h0b, conv_w, conv_b, wg, lam):
    b, t, w = xb.shape
    nb = SUBLANES
    assert t % (LRU_SEGMENTS * LRU_UNROLL) == 0 and t % LRU_CHUNK == 0
    assert conv_w.shape[0] == CONV_PAD_L + 1 + CONV_PAD_R
    xspec = pl.BlockSpec((nb, t, LANES), lambda i, c: (i, 0, c))
    hspec = pl.BlockSpec((nb, LANES), lambda i, c: (i, c))
    scratch = [pltpu.VMEM(((t + CONV_PAD_L + CONV_PAD_R) * nb, LANES), F32)]
    scratch += [pltpu.VMEM((t * nb, LANES), F32) for _ in range(6)]
    return pl.pallas_call(
        _lru_kernel,
        grid=(b // nb, w // LANES),
        in_specs=[xspec, xspec, hspec, hspec,
                  pl.BlockSpec((conv_w.shape[0], LANES), lambda i, c: (0, c)),
                  pl.BlockSpec((1, LANES), lambda i, c: (0, c)),
                  pl.BlockSpec((1, 2 * LANES, 4 * LANES), lambda i, c: (c, 0, 0)),
                  pl.BlockSpec((2, LANES), lambda i, c: (0, c))],
        out_specs=[xspec, hspec, hspec],
        out_shape=[jax.ShapeDtypeStruct((b, t, w), BF16),
                   jax.ShapeDtypeStruct((b, w), F32),
                   jax.ShapeDtypeStruct((b, w), F32)],
        scratch_shapes=scratch,
        compiler_params=_params("parallel", "parallel"),
        name="lru_branch",
    )(xb, gb, h0f, h0b, conv_w, conv_b.reshape(1, w), wg, lam)


def _lru_gate_weights(wa, ba, wx, bx):
    def tiles(w):
        w2 = w.reshape(LRU_BLOCKS // 2, 2, LRU_BLOCK, LRU_BLOCK)
        z = jnp.zeros_like(w2[:, 0])
        return jnp.concatenate([jnp.concatenate([w2[:, 0], z], axis=2),
                                jnp.concatenate([z, w2[:, 1]], axis=2)], axis=1)
    wg = 0.5 * jnp.concatenate([tiles(wa[0]), tiles(wx[0]), tiles(wa[1]), tiles(wx[1])], axis=2)
    n_tiles = LRU_WIDTH // LANES
    bg = jnp.stack([ba[0], bx[0], ba[1], bx[1]], axis=0).reshape(4, n_tiles, LANES)
    bg = 0.5 * jnp.transpose(bg, (1, 0, 2)).reshape(n_tiles, 1, 4 * LANES)
    hi = bg.astype(BF16)
    lo = (bg - hi.astype(F32)).astype(BF16)
    rest = jnp.zeros((n_tiles, LANES - 2, 4 * LANES), BF16)
    return jnp.concatenate([wg.astype(BF16), hi, lo, rest], axis=1)


def _packed_tiles(ref, n_heads, head):
    n_tiles = ref.shape[1] // (n_heads * KEY_TILE)
    return [ref[0, pl.ds(c * KEY_TILE * n_heads + head, KEY_TILE, stride=n_heads), :].astype(BF16)
            for c in range(n_tiles)]


def _diff_kernel(*refs, packed, n_heads, lam_init):
    n_kv = len(packed)
    q_ref = refs[0]
    kv_refs = refs[1:1 + 2 * n_kv]
    g_ref, lamp_ref, subg_ref, o_ref, s_ref = refs[1 + 2 * n_kv:]
    low = _lane_low_half()
    tq = q_ref.shape[1]
    first_head = pl.program_id(1) * (q_ref.shape[2] // LANES)
    lp = lamp_ref[...]
    lam = (jnp.exp(jnp.sum(lp[0:1] * lp[1:2], axis=-1, keepdims=True))
           - jnp.exp(jnp.sum(lp[2:3] * lp[3:4], axis=-1, keepdims=True)) + lam_init)
    lane_groups = range(0, q_ref.shape[2], LANES)
    q_blocks, kv_blocks = [], []
    for j in lane_groups:
        k_tiles, v_tiles = [], []
        for i in range(n_kv):
            if packed[i]:
                head = first_head + j // LANES
                k_tiles += _packed_tiles(kv_refs[2 * i], n_heads, head)
                v_tiles += _packed_tiles(kv_refs[2 * i + 1], n_heads, head)
            else:
                n_tiles = kv_refs[2 * i].shape[1] // KEY_TILE
                k_tiles += _tiles(kv_refs[2 * i], n_tiles, j)
                v_tiles += _tiles(kv_refs[2 * i + 1], n_tiles, j)
        v_tiles = _values_and_ones(v_tiles)
        for r in range(0, tq, ROW_BLK):
            q_blocks += list(_split_lane_halves(q_ref[0, r:r + ROW_BLK, j:j + LANES], low))
            kv_blocks += [(k_tiles, v_tiles)] * 2
    res = _attend(q_blocks, kv_blocks, s_ref)
    i = 0
    for j in lane_groups:
        for r in range(0, tq, ROW_BLK):
            acc1, acc2 = res[i], res[i + 1]
            i += 2
            o = acc1[:, :LANES] / acc1[:, LANES:] - acc2[:, :LANES] * (lam / acc2[:, LANES:])
            o = o * lax.rsqrt(jnp.mean(o * o, axis=-1, keepdims=True) + EPS) * subg_ref[...] * (1.0 - lam_init)
            gate = _silu(g_ref[0, r:r + ROW_BLK, j:j + LANES].astype(F32))
            o_ref[0, r:r + ROW_BLK, j:j + LANES] = (o * gate).astype(o_ref.dtype)


def _diff_attn(q, kv_list, g, lam_params, sub_g, lam_init, tq, heads_per_step):
    b, t, w = q.shape
    wb = heads_per_step * LANES
    qspec = pl.BlockSpec((1, tq, wb), lambda bi, h, qi: (bi, qi, h))
    n_heads = w // LANES
    in_specs = [qspec]
    args = [q]
    n_keys = 0
    for k, v, packed in kv_list:
        if packed:
            spec = pl.BlockSpec((1, k.shape[1], LANES), lambda bi, h, qi: (bi, 0, 0))
            n_keys += k.shape[1] // n_heads
        else:
            spec = pl.BlockSpec((1, k.shape[1], wb), lambda bi, h, qi: (bi, 0, h))
            n_keys += k.shape[1]
        in_specs += [spec, spec]
        args += [k, v]
    in_specs += [qspec,
                 pl.BlockSpec(lam_params.shape, lambda bi, h, qi: (0, 0)),
                 pl.BlockSpec((1, LANES), lambda bi, h, qi: (0, 0))]
    args += [g, lam_params, sub_g.reshape(1, LANES)]
    return pl.pallas_call(
        functools.partial(_diff_kernel, packed=tuple(p for _, _, p in kv_list), n_heads=n_heads,
                          lam_init=lam_init),
        grid=(b, w // wb, t // tq),
        in_specs=in_specs,
        out_specs=qspec,
        out_shape=jax.ShapeDtypeStruct((b, t, w), BF16),
        scratch_shapes=[pltpu.VMEM((SCORE_SLOTS * ROW_BLK, n_keys), F32)],
        compiler_params=_params("parallel", "parallel", "parallel"),
        name="diff_attn",
    )(*args)


def _rope_tables(t_len):
    m = DIFF_HEAD_DIM // 4
    t = jnp.arange(t_len)
    rows = (t // GRID_W).astype(F32)
    cols = (t % GRID_W).astype(F32)
    inv = ROPE_BASE ** (-jnp.arange(m, dtype=F32) / m)
    ang_r = rows[:, None] * inv[None, :]
    ang_c = cols[:, None] * inv[None, :]
    cos = jnp.concatenate([jnp.cos(ang_r)] * 2 + [jnp.cos(ang_c)] * 2, axis=1)
    sin = jnp.concatenate([-jnp.sin(ang_r), jnp.sin(ang_r), -jnp.sin(ang_c), jnp.sin(ang_c)], axis=1)
    reps = LANES // DIFF_HEAD_DIM
    return jnp.tile(cos, (1, reps)), jnp.tile(sin, (1, reps))


def kernel(x_prompt, x_sample, c, cache_na_k, cache_na_v, state_lru, cache_diff_k, cache_diff_v, c_ctx,
           e_norm, e_ada_w, e_ada_b, e_w_in, e_rpb, e_conv_w, e_conv_b, e_lru_wa, e_lru_ba, e_lru_wx, e_lru_bx,
           e_lru_lam, e_w_out, o_norm, o_ada_w, o_ada_b, o_w_in, o_lq1, o_lk1, o_lq2, o_lk2, o_sub_g, o_w_out,
           final_norm):
    bp, tp, d = x_prompt.shape
    bs, ts, _ = x_sample.shape
    past = cache_na_k.shape[2]
    pad_rows = 2 * SUBLANES - bs - 1
    cc = jnp.concatenate([c, c_ctx[None, :], jnp.zeros((pad_rows, d), F32)], axis=0)
    nw, lw, dw = NA_WIDTH, LRU_WIDTH, DIFF_WIDTH
    rope_tables = _rope_tables(ts) if DEPTH > 1 else None

    def layer_params(i):
        j = i // 2
        if i % 2 == 0:
            return e_ada_w[j], e_ada_b[j], e_norm[j], e_w_in[j], e_w_out[j]
        return o_ada_w[j], o_ada_b[j], o_norm[j], o_w_in[j], o_w_out[j]

    mods = []
    for i in range(DEPTH):
        ada_w, ada_b = layer_params(i)[:2]
        m = _modulation(cc, ada_w, ada_b)
        mods.append((m[:bs].reshape(bs, 1, 3 * d), m[bs:bs + 1].reshape(1, 1, 3 * d)))

    def projection(i, latent):
        norm_g, w_in = layer_params(i)[2:4]
        mod = mods[i][0 if latent else 1]
        kv_dtype = BF16 if latent else F32
        if i % 2 == 0:
            qmul = NA_HEAD_DIM ** -0.5 * LOG2E
            outs = [(nw, BF16, False, qmul, False), (nw, kv_dtype, False, None, False),
                    (nw, kv_dtype, False, None, False), (nw, BF16, False, None, False),
                    (lw, F32, False, None, False), (lw, BF16, False, None, False)]
            return mod, norm_g, w_in, outs, None
        qmul = DIFF_HEAD_DIM ** -0.5 * LOG2E
        outs = [(dw, BF16, latent, qmul, False), (dw, kv_dtype, latent, None, not latent),
                (dw, kv_dtype, False, None, not latent), (dw, BF16, False, None, False)]
        return mod, norm_g, w_in, outs, rope_tables if latent else None

    xp, xs = x_prompt, x_sample
    _, proj_p = _stage(xp, nxt=projection(0, False), tm=512)
    _, proj_s = _stage(xs, nxt=projection(0, True), tm=512)
    na_k, na_v, lru_s, df_k, df_v = [], [], [], [], []
    for i in range(DEPTH):
        j = i // 2
        w_out = layer_params(i)[4]
        if i % 2 == 0:
            wg = _lru_gate_weights(e_lru_wa[j], e_lru_ba[j], e_lru_wx[j], e_lru_bx[j])

            q, k, v, ga, xb, gb = proj_p
            oa = _ctx_attn(q, k, v, ga)
            zeros = jnp.zeros((bp, lw), F32)
            ob, ff, fb = _lru_branch(xb, gb, zeros, zeros, e_conv_w[j], e_conv_b[j], wg, e_lru_lam[j])
            mixed_p = [oa, ob]
            na_k.append(k.reshape(bp, tp, NA_HEADS, NA_HEAD_DIM))
            na_v.append(v.reshape(bp, tp, NA_HEADS, NA_HEAD_DIM))
            lru_s.append(jnp.stack([ff, fb], axis=1))

            q, k, v, ga, xb, gb = proj_s
            ck = cache_na_k[:, j].reshape(bs, past, nw)
            cv = cache_na_v[:, j].reshape(bs, past, nw)
            oa = _na_attn(q, k, v, ck, cv, e_rpb[j], ga)
            ob, _, _ = _lru_branch(xb, gb, state_lru[:, j, 0], state_lru[:, j, 1],
                                   e_conv_w[j], e_conv_b[j], wg, e_lru_lam[j])
            mixed_s = [oa, ob]
        else:
            lam_init = 0.8 - 0.6 * math.exp(-0.3 * i)
            lam_params = jnp.stack([o_lq1[j], o_lk1[j], o_lq2[j], o_lk2[j]], axis=0)

            q, k, v, g = proj_p
            mixed_p = [_diff_attn(q, [(k, v, True)], g, lam_params, o_sub_g[j], lam_init, tq=tp,
                                  heads_per_step=DIFF_HEADS)]
            df_k.append(k.reshape(bp, tp, DIFF_HEADS, 2 * DIFF_HEAD_DIM))
            df_v.append(v.reshape(bp, tp, DIFF_HEADS, 2 * DIFF_HEAD_DIM))

            q, k, v, g = proj_s
            ck = cache_diff_k[:, j].reshape(bs, past * DIFF_HEADS, 2 * DIFF_HEAD_DIM)
            cv = cache_diff_v[:, j].reshape(bs, past * DIFF_HEADS, 2 * DIFF_HEAD_DIM)
            mixed_s = [_diff_attn(q, [(ck, cv, True), (k, v, False)], g, lam_params, o_sub_g[j], lam_init, tq=ts,
                                  heads_per_step=4)]

        if i == DEPTH - 1:
            xp, _ = _stage(xp, prev=(mixed_p, w_out, mods[i][1]), final_g=final_norm, tm=512)
            xs, _ = _stage(xs, prev=(mixed_s, w_out, mods[i][0]), final_g=final_norm, tm=512)
        else:
            xp, proj_p = _stage(xp, prev=(mixed_p, w_out, mods[i][1]), nxt=projection(i + 1, False), tm=512)
            xs, proj_s = _stage(xs, prev=(mixed_s, w_out, mods[i][0]), nxt=projection(i + 1, True), tm=512)

    return (xp, xs, jnp.stack(na_k, axis=1), jnp.stack(na_v, axis=1), jnp.stack(lru_s, axis=1),
            jnp.stack(df_k, axis=1), jnp.stack(df_v, axis=1))
```

```python
import functools
import math

import jax
import jax.numpy as jnp
from jax import lax
from jax.experimental import pallas as pl
from jax.experimental.pallas import tpu as pltpu

D_MODEL = 1024
DEPTH = 2
GRID_W = 64
NA_HEADS = 8
NA_HEAD_DIM = 64
NA_WIDTH = NA_HEADS * NA_HEAD_DIM
NA_ROWS_MAX = 8
NA_COLS = 16
RPB_ROWS = 2 * NA_ROWS_MAX - 1
RPB_COLS = 2 * NA_COLS - 1
LRU_WIDTH = 512
LRU_BLOCKS = 8
LRU_BLOCK = LRU_WIDTH // LRU_BLOCKS
LRU_C = 8.0
DIFF_HEADS = 8
DIFF_HEAD_DIM = 64
DIFF_WIDTH = DIFF_HEADS * 2 * DIFF_HEAD_DIM
ROPE_BASE = 10000.0
EPS = 1e-6
NEG_INF = -1e30
LOG2E = math.log2(math.e)

LANES = 128
SUBLANES = 8
MXU_DIM = 256
VMEM_LIMIT = 56 * 1024 * 1024

CAST_COLS = 512
LRU_SEGMENTS = 4
LRU_UNROLL = 8
LRU_CHUNK = 128
CONV_PAD_L = 2
CONV_PAD_R = 1
ROW_BLK = MXU_DIM
KEY_TILE = MXU_DIM
SCORE_SLOTS = 4

NA_Q_ROWS = 4
NA_K_ROWS = 12
NA_PAIR_TILES = 16
NA_PAIRS_PER_STEP = 2
CTX_BATCH_PER_STEP = 2

F32 = jnp.float32
BF16 = jnp.bfloat16


def _params(*semantics):
    return pltpu.CompilerParams(dimension_semantics=semantics, vmem_limit_bytes=VMEM_LIMIT)


def _sigmoid(x):
    return 0.5 * jnp.tanh(0.5 * x) + 0.5


def _silu(x):
    return x * _sigmoid(x)


def _dot(a, b):
    return jnp.dot(a, b, preferred_element_type=F32)


def _dot_nt(a, b):
    return lax.dot_general(a, b, (((1,), (1,)), ((), ())), preferred_element_type=F32)


def _mod_kernel(c_ref, w_ref, b_ref, o_ref):
    s = _silu(c_ref[...])
    o_ref[...] = _dot(s.astype(BF16), w_ref[...].astype(BF16)) + b_ref[...]


def _modulation(cc, w, b):
    rows, d = cc.shape
    n = w.shape[1]
    tn = 256
    return pl.pallas_call(
        _mod_kernel,
        grid=(n // tn,),
        in_specs=[pl.BlockSpec((rows, d), lambda j: (0, 0)),
                  pl.BlockSpec((d, tn), lambda j: (0, j)),
                  pl.BlockSpec((1, tn), lambda j: (0, j))],
        out_specs=pl.BlockSpec((rows, tn), lambda j: (0, j)),
        out_shape=jax.ShapeDtypeStruct((rows, n), F32),
        compiler_params=_params("arbitrary"),
        name="modulation",
    )(cc, w, b.reshape(1, n))


def _rope_block(x, cos, sin_signed, low16):
    partner = jnp.where(low16, pltpu.roll(x, LANES - 16, axis=1), pltpu.roll(x, 16, axis=1))
    return x * cos + partner * sin_signed


def _cast_weights_once(w_ref, wbf_ref):
    @pl.when((pl.program_id(0) == 0) & (pl.program_id(1) == 0))
    def _():
        for off in range(0, w_ref.shape[1], CAST_COLS):
            wbf_ref[:, off:off + CAST_COLS] = w_ref[:, off:off + CAST_COLS].astype(BF16)


def _norm_project(x, mod_ref, g_ref, wbf_ref, rope_refs, out_refs, outs):
    tm, d = x.shape
    y = x * lax.rsqrt(jnp.mean(x * x, axis=-1, keepdims=True) + EPS) * g_ref[...]
    mod = mod_ref[0]
    h = (y * (1.0 + mod[:, d:2 * d]) + mod[:, :d]).astype(BF16)
    if rope_refs:
        cos = rope_refs[0][...]
        sin = rope_refs[1][...]
        lane = lax.broadcasted_iota(jnp.int32, (1, LANES), 1)
        low16 = (lane & 16) == 0
    off = 0
    for o_ref, (width, roped, mul, packed) in zip(out_refs, outs):
        acc = _dot(h, wbf_ref[:, off:off + width])
        if roped:
            acc = jnp.concatenate(
                [_rope_block(acc[:, j:j + LANES], cos, sin, low16) for j in range(0, width, LANES)], axis=1)
        if mul is not None:
            acc = acc * mul
        if packed:
            n_heads = width // LANES
            for hh in range(n_heads):
                o_ref[0, pl.ds(hh, tm, stride=n_heads), :] = acc[:, hh * LANES:(hh + 1) * LANES].astype(o_ref.dtype)
        else:
            o_ref[0] = acc.astype(o_ref.dtype)
        off += width


def _gated_residual(o_refs, wbf_ref, x_ref, mod_ref):
    d = x_ref.shape[-1]
    acc, row = None, 0
    for o_ref in o_refs:
        k = o_ref.shape[-1]
        part = _dot(o_ref[0], wbf_ref[row:row + k, :])
        acc = part if acc is None else acc + part
        row += k
    return x_ref[0] + mod_ref[0][:, 2 * d:] * acc


def _stage_kernel(*refs, n_o, final, outs, rope):
    refs = list(refs)
    take = lambda n: [refs.pop(0) for _ in range(n)]
    if n_o:
        o_refs = take(n_o)
        wout_ref, x_ref, mod_prev_ref = take(3)
        fin_ref = take(1)[0] if final else None
    else:
        x_ref, = take(1)
    if outs:
        mod_ref, g_ref, win_ref = take(3)
        rope_refs = take(2) if rope else []
    xout_ref = take(1)[0] if n_o else None
    out_refs = take(len(outs))
    if n_o:
        wout_bf = take(1)[0]
        _cast_weights_once(wout_ref, wout_bf)
        x = _gated_residual(o_refs, wout_bf, x_ref, mod_prev_ref)
        if final:
            xout_ref[0] = x * lax.rsqrt(jnp.mean(x * x, axis=-1, keepdims=True) + EPS) * fin_ref[...]
        else:
            xout_ref[0] = x
    else:
        x = x_ref[0]
    if outs:
        win_bf = take(1)[0]
        _cast_weights_once(win_ref, win_bf)
        _norm_project(x, mod_ref, g_ref, win_bf, rope_refs, out_refs, outs)


def _stage(x, prev=None, nxt=None, final_g=None, tm=256):
    b0, t0, d = x.shape
    used_mods = ([prev[2]] if prev is not None else []) + ([nxt[0]] if nxt is not None else [])
    if all(m.shape[0] == 1 for m in used_mods) and (nxt is None or nxt[4] is None):
        x = x.reshape(1, b0 * t0, d)
        if prev is not None:
            prev = ([o.reshape(1, b0 * t0, o.shape[-1]) for o in prev[0]],) + tuple(prev[1:])
    b, t, _ = x.shape
    tm = min(tm, t)
    tile = lambda width: pl.BlockSpec((1, tm, width), lambda i, j: (i, j, 0))
    mod_spec = lambda mod: pl.BlockSpec(
        (1, 1, 3 * d), (lambda i, j: (i, 0, 0)) if mod.shape[0] > 1 else (lambda i, j: (0, 0, 0)))
    resident = lambda w: pl.BlockSpec(w.shape, lambda i, j: (0, 0), pipeline_mode=pl.Buffered(1))
    row_spec = pl.BlockSpec((1, d), lambda i, j: (0, 0))
    in_specs, args, out_specs, out_shape, scratch = [], [], [], [], []
    n_o, outs, rope = 0, (), False
    if prev is not None:
        o_list, w_out, mod_prev = prev
        n_o = len(o_list)
        in_specs += [tile(o.shape[-1]) for o in o_list] + [resident(w_out), tile(d), mod_spec(mod_prev)]
        args += list(o_list) + [w_out, x, mod_prev]
        if final_g is not None:
            in_specs.append(row_spec)
            args.append(final_g.reshape(1, d))
        out_specs.append(tile(d))
        out_shape.append(jax.ShapeDtypeStruct((b, t, d), F32))
        scratch.append(pltpu.VMEM(w_out.shape, BF16))
    else:
        in_specs.append(tile(d))
        args.append(x)
    if nxt is not None:
        mod_next, norm_g, w_in, out_list, rope_tables = nxt
        in_specs += [mod_spec(mod_next), row_spec, resident(w_in)]
        args += [mod_next, norm_g.reshape(1, d), w_in]
        rope = rope_tables is not None
        if rope:
            in_specs += [pl.BlockSpec((tm, LANES), lambda i, j: (j, 0))] * 2
            args += list(rope_tables)
        for width, dt, _, _, packed in out_list:
            if packed:
                heads = width // LANES
                out_specs.append(pl.BlockSpec((1, tm * heads, LANES), lambda i, j: (i, j, 0)))
                out_shape.append(jax.ShapeDtypeStruct((b, t * heads, LANES), dt))
            else:
                out_specs.append(tile(width))
                out_shape.append(jax.ShapeDtypeStruct((b, t, width), dt))
        outs = tuple((w_, r, m, p) for w_, _, r, m, p in out_list)
        scratch.append(pltpu.VMEM(w_in.shape, BF16))
    res = pl.pallas_call(
        functools.partial(_stage_kernel, n_o=n_o, final=final_g is not None, outs=outs, rope=rope),
        grid=(b, t // tm),
        in_specs=in_specs,
        out_specs=out_specs,
        out_shape=out_shape,
        scratch_shapes=scratch,
        compiler_params=_params("arbitrary", "arbitrary"),
        name="stage",
    )(*args)
    res = [r.reshape((b0, r.shape[1] // (b0 // b)) + r.shape[2:]) for r in res]
    return (res[0], list(res[1:])) if n_o else (None, list(res))


def _attend(q_blocks, kv_blocks, s_ref, score_fn=None):
    n_rb = len(q_blocks)

    def tile_of(rb, c):
        slot = rb % SCORE_SLOTS
        return (slice(slot * ROW_BLK, (slot + 1) * ROW_BLK), slice(c * KEY_TILE, (c + 1) * KEY_TILE))

    def score_tile(rb, c):
        s_ref[tile_of(rb, c)] = _dot_nt(q_blocks[rb], kv_blocks[rb][0][c])

    def max_tile(rb, c, m_acc):
        s = s_ref[tile_of(rb, c)]
        adjusted = None if score_fn is None else score_fn(rb, c, s)
        if adjusted is not None:
            s = adjusted
            s_ref[tile_of(rb, c)] = s
        mt = jnp.maximum(s[:, :LANES], s[:, LANES:])
        return mt if m_acc is None else jnp.maximum(m_acc, mt)

    def n_tiles(rb):
        return len(kv_blocks[rb][0]) if 0 <= rb < n_rb else 0

    outs = []
    row_max = {}
    for step in range(n_rb + 2):
        rb_score, rb_max, rb_exp = step, step - 1, step - 2
        m_acc, acc = None, None
        for c in range(max(n_tiles(rb_score), n_tiles(rb_max), n_tiles(rb_exp))):
            if c < n_tiles(rb_score):
                score_tile(rb_score, c)
            if c < n_tiles(rb_max):
                m_acc = max_tile(rb_max, c, m_acc)
            if c < n_tiles(rb_exp):
                e = jnp.exp2(s_ref[tile_of(rb_exp, c)] - row_max[rb_exp])
                pv = _dot(e.astype(BF16), kv_blocks[rb_exp][1][c])
                acc = pv if acc is None else acc + pv
        if m_acc is not None:
            row_max[rb_max] = m_acc.max(axis=-1, keepdims=True)
        if acc is not None:
            outs.append(acc)
    return outs


def _lane_low_half():
    return lax.broadcasted_iota(jnp.int32, (1, LANES), 1) < (LANES // 2)


def _values_and_ones(v_tiles):
    ones = jnp.ones((KEY_TILE, LANES), BF16)
    return [jnp.concatenate([v, ones], axis=1) for v in v_tiles]


def _normalise_pair(acc_lo, acc_hi, low):
    return jnp.where(low, acc_lo[:, :LANES] / acc_lo[:, LANES:], acc_hi[:, :LANES] / acc_hi[:, LANES:])


def _split_lane_halves(q, low):
    zero = jnp.zeros_like(q)
    return jnp.where(low, q, zero), jnp.where(low, zero, q)


def _tiles(ref, n_tiles, j, bi=0):
    return [ref[bi, c * KEY_TILE:(c + 1) * KEY_TILE, j:j + LANES].astype(BF16) for c in range(n_tiles)]


def _ctx_attn_kernel(q_ref, k_ref, v_ref, g_ref, o_ref, s_ref):
    low = _lane_low_half()
    nb, tq, w = q_ref.shape
    n_tiles = k_ref.shape[1] // KEY_TILE
    tiles = [(bi, j, r) for bi in range(nb) for j in range(0, w, LANES) for r in range(0, tq, ROW_BLK)]
    q_blocks, kv_blocks, kv = [], [], {}
    for bi, j, r in tiles:
        if (bi, j) not in kv:
            kv[bi, j] = (_tiles(k_ref, n_tiles, j, bi), _values_and_ones(_tiles(v_ref, n_tiles, j, bi)))
        q_blocks += list(_split_lane_halves(q_ref[bi, r:r + ROW_BLK, j:j + LANES], low))
        kv_blocks += [kv[bi, j]] * 2
    res = _attend(q_blocks, kv_blocks, s_ref)
    for i, (bi, j, r) in enumerate(tiles):
        o = _normalise_pair(res[2 * i], res[2 * i + 1], low)
        gate = _silu(g_ref[bi, r:r + ROW_BLK, j:j + LANES].astype(F32))
        o_ref[bi, r:r + ROW_BLK, j:j + LANES] = (o * gate).astype(o_ref.dtype)


def _ctx_attn(q, k, v, g):
    b, t, w = q.shape
    nb = CTX_BATCH_PER_STEP
    spec = pl.BlockSpec((nb, t, w), lambda i: (i, 0, 0))
    return pl.pallas_call(
        _ctx_attn_kernel,
        grid=(b // nb,),
        in_specs=[spec] * 4,
        out_specs=spec,
        out_shape=jax.ShapeDtypeStruct((b, t, w), BF16),
        scratch_shapes=[pltpu.VMEM((SCORE_SLOTS * ROW_BLK, k.shape[1]), F32)],
        compiler_params=_params("parallel"),
        name="ctx_attn",
    )(q, k, v, g)


def _na_pair_tiles(rpb_ref, head, pt_ref):
    right = lax.broadcasted_iota(jnp.int32, (GRID_W, LANES), 1) >= GRID_W
    zero_row = jnp.zeros((1, LANES), F32)

    def toeplitz(row, lane0):
        shift = (lane0 - (NA_COLS - 1)) % LANES
        return pltpu.roll(jnp.broadcast_to(row, (GRID_W, LANES)), shift, axis=1, stride=1, stride_axis=0)

    for p in range(NA_PAIR_TILES):
        left_row = rpb_ref[head, p - 1:p, :] if p >= 1 else zero_row
        right_row = rpb_ref[head, p:p + 1, :]
        pt_ref[head, p] = jnp.where(right, toeplitz(right_row, GRID_W), toeplitz(left_row, 0)) * LOG2E


def _na_row_start(r, n_rows):
    kr = min(NA_ROWS_MAX, n_rows)
    return min(max(r - kr // 2, 0), n_rows - kr)


def _na_window(grp, n_rows):
    kr = min(NA_ROWS_MAX, n_rows)
    starts = [_na_row_start(grp * NA_Q_ROWS + i, n_rows) for i in range(NA_Q_ROWS)]
    rows_per_tile = KEY_TILE // GRID_W
    need = -(-(max(starts) + kr - min(starts)) // rows_per_tile) * rows_per_tile
    return min(min(starts), n_rows - need), need


def _na_kernel(q_ref, k_ref, v_ref, ck_ref, cv_ref, rpb_ref, g_ref, o_ref, pt_ref, badd_ref, bcap_ref, s_ref, *,
               n_rows):
    kr = min(NA_ROWS_MAX, n_rows)
    n_grp = n_rows // NA_Q_ROWS
    n_pairs = q_ref.shape[2] // LANES
    low = _lane_low_half()

    @pl.when(pl.program_id(1) == 0)
    def _build_bias():
        for head in range(2 * n_pairs):
            _na_pair_tiles(rpb_ref, head, pt_ref)
        qcol = lax.broadcasted_iota(jnp.int32, (GRID_W, LANES), 0)
        lane = lax.broadcasted_iota(jnp.int32, (GRID_W, LANES), 1)
        right = lane >= GRID_W
        kcol = lane & (GRID_W - 1)
        col_start = jnp.clip(qcol - NA_COLS // 2, 0, GRID_W - NA_COLS)
        col_ok = (kcol >= col_start) & (kcol < col_start + NA_COLS)
        for grp in range(n_grp):
            krow0, key_rows = _na_window(grp, n_rows)
            for i in range(NA_Q_ROWS):
                r = grp * NA_Q_ROWS + i
                row_start = _na_row_start(r, n_rows)
                for m in range(key_rows // 2):
                    key_row = krow0 + 2 * m
                    ok_l = row_start <= key_row < row_start + kr
                    ok_r = row_start <= key_row + 1 < row_start + kr
                    rows = slice(i * GRID_W, (i + 1) * GRID_W)
                    cols = slice(m * LANES, (m + 1) * LANES)
                    if not (ok_l or ok_r):
                        bcap_ref[grp, rows, cols] = jnp.full((GRID_W, LANES), NEG_INF, F32)
                        for head in range(2 * n_pairs):
                            badd_ref[head * n_grp + grp, rows, cols] = jnp.zeros((GRID_W, LANES), F32)
                        continue
                    ok = col_ok
                    if not ok_r:
                        ok = ok & jnp.logical_not(right)
                    if not ok_l:
                        ok = ok & right
                    bcap_ref[grp, rows, cols] = jnp.where(ok, jnp.inf, NEG_INF)
                    p = key_row - r + NA_ROWS_MAX
                    for head in range(2 * n_pairs):
                        badd_ref[head * n_grp + grp, rows, cols] = jnp.where(ok, pt_ref[head, p], 0.0)

    n_ctx = ck_ref.shape[1] // KEY_TILE
    q_blocks, kv_blocks, blocks = [], [], []
    for pair in range(n_pairs):
        lanes = slice(pair * LANES, (pair + 1) * LANES)
        ctx_k = _tiles(ck_ref, n_ctx, pair * LANES)
        ctx_v = _tiles(cv_ref, n_ctx, pair * LANES)
        for grp in range(n_grp):
            krow0, key_rows = _na_window(grp, n_rows)
            k0 = krow0 * GRID_W
            win = [slice(k0 + c * KEY_TILE, k0 + (c + 1) * KEY_TILE) for c in range(key_rows * GRID_W // KEY_TILE)]
            kv = ([k_ref[0, w, lanes] for w in win] + ctx_k,
                  _values_and_ones([v_ref[0, w, lanes] for w in win] + ctx_v))
            q_blocks += list(_split_lane_halves(q_ref[0, grp * ROW_BLK:(grp + 1) * ROW_BLK, lanes], low))
            kv_blocks += [kv, kv]
            blocks += [(2 * pair, grp, len(win)), (2 * pair + 1, grp, len(win))]

    def masked_bias(rb, c, s):
        head, grp, n_win = blocks[rb]
        if c >= n_win:
            return None
        cols = slice(c * KEY_TILE, (c + 1) * KEY_TILE)
        return jnp.minimum(s + badd_ref[head * n_grp + grp, :, cols], bcap_ref[grp, :, cols])

    res = _attend(q_blocks, kv_blocks, s_ref, masked_bias)
    for pair in range(n_pairs):
        lanes = slice(pair * LANES, (pair + 1) * LANES)
        for grp in range(n_grp):
            rb = 2 * (pair * n_grp + grp)
            o = _normalise_pair(res[rb], res[rb + 1], low)
            rows = slice(grp * ROW_BLK, (grp + 1) * ROW_BLK)
            o_ref[0, rows, lanes] = (o * _silu(g_ref[0, rows, lanes].astype(F32))).astype(o_ref.dtype)


def _na_attn(q, k, v, ck, cv, rpb, g):
    heads, rpb_rows, rpb_cols = rpb.shape
    rpb = jnp.pad(rpb, ((0, 0), (0, NA_PAIR_TILES - rpb_rows), (0, LANES - rpb_cols)))
    b, t, w = q.shape
    c = ck.shape[1]
    n_rows = t // GRID_W
    assert NA_Q_ROWS * GRID_W == ROW_BLK and n_rows % NA_Q_ROWS == 0 and n_rows >= NA_K_ROWS
    n_grp = n_rows // NA_Q_ROWS
    n_win = NA_K_ROWS * GRID_W
    wb = NA_PAIRS_PER_STEP * LANES
    hb = 2 * NA_PAIRS_PER_STEP
    tspec = pl.BlockSpec((1, t, wb), lambda hp, bi: (bi, 0, hp))
    cspec = pl.BlockSpec((1, c, wb), lambda hp, bi: (bi, 0, hp))
    ptspec = pl.BlockSpec((hb, NA_PAIR_TILES, LANES), lambda hp, bi: (hp, 0, 0))
    return pl.pallas_call(
        functools.partial(_na_kernel, n_rows=n_rows),
        grid=(w // wb, b),
        in_specs=[tspec, tspec, tspec, cspec, cspec, ptspec, tspec],
        out_specs=tspec,
        out_shape=jax.ShapeDtypeStruct((b, t, w), BF16),
        scratch_shapes=[pltpu.VMEM((hb, NA_PAIR_TILES, GRID_W, LANES), F32),
                        pltpu.VMEM((hb * n_grp, ROW_BLK, n_win), F32),
                        pltpu.VMEM((n_grp, ROW_BLK, n_win), F32),
                        pltpu.VMEM((SCORE_SLOTS * ROW_BLK, n_win + c), F32)],
        compiler_params=_params("parallel", "arbitrary"),
        name="na_attn",
    )(q, k, v, ck, cv, rpb, g)


def _softplus(x):
    return jnp.maximum(x, 0.0) + jnp.log(1.0 + jnp.exp(-jnp.abs(x)))


def _lru_kernel(x_ref, gb_ref, h0f_ref, h0b_ref, cw_ref, cb_ref, wg_ref, lam_ref,
                o_ref, ff_ref, fb_ref, xt_s, af_s, uf_s, ab_s, ub_s, hf_s, hb_s):
    nb, t, _ = x_ref.shape
    pad = CONV_PAD_L * nb
    cw = cw_ref[...]
    cb = cb_ref[...]
    wg = wg_ref[0]
    kk = (-0.5 * LRU_C * LOG2E) * _softplus(-lam_ref[...])

    xt_s[0:pad, :] = jnp.zeros((pad, LANES), F32)
    xt_s[pad + t * nb:pad + (t + CONV_PAD_R) * nb, :] = jnp.zeros((CONV_PAD_R * nb, LANES), F32)

    def to_time_major(b, carry):
        xt_s[pl.ds(pad + b, t, stride=nb), :] = x_ref[b]
        return carry

    lax.fori_loop(0, nb, to_time_major, 0)

    rows_per_chunk = LRU_CHUNK * nb
    lane = lax.broadcasted_iota(jnp.int32, (rows_per_chunk, LANES), 1)
    bias_cols = jnp.where(lane < 2, 1.0, 0.0).astype(BF16)

    def gates(c, carry):
        r0 = pl.multiple_of(c * rows_per_chunk, rows_per_chunk)

        def tap(i):
            return xt_s[pl.ds(r0 + i * nb, rows_per_chunk), :]

        x = tap(2)
        xc = cw[0:1] * tap(0) + cw[1:2] * tap(1) + cw[2:3] * x + cw[3:4] * tap(3) + cb
        th = jnp.tanh(_dot(jnp.concatenate([xc.astype(BF16), bias_cols], axis=1), wg))
        xh = 0.5 * xc
        rows = pl.ds(r0, rows_per_chunk)
        for d, (a_s, u_s) in enumerate(((af_s, uf_s), (ab_s, ub_s))):
            t_a = th[:, (2 * d) * LANES:(2 * d + 1) * LANES]
            t_x = th[:, (2 * d + 1) * LANES:(2 * d + 2) * LANES]
            a = jnp.exp2(kk[d:d + 1] * t_a + kk[d:d + 1])
            y = 1.0 - a * a
            sq = y * lax.rsqrt(jnp.maximum(y, 1e-30))
            a_s[rows, :] = a
            u_s[rows, :] = sq * (xh * t_x + xh)
        return carry

    lax.fori_loop(0, t // LRU_CHUNK, gates, 0)

    ts = t // LRU_SEGMENTS
    refs = ((af_s, uf_s), (ab_s, ub_s))

    slab = nb * LRU_UNROLL

    def slab_rows(d, j, i):
        base = (j * ts + i * LRU_UNROLL) * nb
        return pl.ds(pl.multiple_of(base if d == 0 else t * nb - slab - base, slab), slab)

    def step_of(d, block, k):
        kk_ = k if d == 0 else LRU_UNROLL - 1 - k
        return block[kk_ * nb:(kk_ + 1) * nb]

    def compose(i, carry):
        state = list(carry)
        for d in range(2):
            for j in range(LRU_SEGMENTS - 1):
                n = d * (LRU_SEGMENTS - 1) + j
                a_blk = refs[d][0][slab_rows(d, j, i), :]
                u_blk = refs[d][1][slab_rows(d, j, i), :]
                g, big_a = state[n]
                for k in range(LRU_UNROLL):
                    a = step_of(d, a_blk, k)
                    g, big_a = a * g + step_of(d, u_blk, k), a * big_a
                state[n] = (g, big_a)
        return tuple(state)

    ones = jnp.ones((nb, LANES), F32)
    h0 = (h0f_ref[...], h0b_ref[...])
    init = tuple((h0[d] if j == 0 else jnp.zeros((nb, LANES), F32), ones)
                 for d in range(2) for j in range(LRU_SEGMENTS - 1))
    ends = lax.fori_loop(0, ts // LRU_UNROLL, compose, init)
    starts = []
    for d in range(2):
        h = h0[d]
        for j in range(LRU_SEGMENTS):
            starts.append(h)
            if j < LRU_SEGMENTS - 1:
                g, big_a = ends[d * (LRU_SEGMENTS - 1) + j]
                h = g if j == 0 else g + big_a * h

    def rerun(i, carry):
        state = list(carry)
        for d, h_s in enumerate((hf_s, hb_s)):
            for j in range(LRU_SEGMENTS):
                n = d * LRU_SEGMENTS + j
                a_blk = refs[d][0][slab_rows(d, j, i), :]
                u_blk = refs[d][1][slab_rows(d, j, i), :]
                hs = []
                for k in range(LRU_UNROLL):
                    state[n] = step_of(d, a_blk, k) * state[n] + step_of(d, u_blk, k)
                    hs.append(state[n])
                h_s[slab_rows(d, j, i), :] = jnp.concatenate(hs if d == 0 else hs[::-1], axis=0)
        return tuple(state)

    final = lax.fori_loop(0, ts // LRU_UNROLL, rerun, tuple(starts))
    ff_ref[...] = final[LRU_SEGMENTS - 1]
    fb_ref[...] = final[2 * LRU_SEGMENTS - 1]

    def add_directions(c, carry):
        rows = pl.ds(pl.multiple_of(c * rows_per_chunk, rows_per_chunk), rows_per_chunk)
        hf_s[rows, :] = hf_s[rows, :] + hb_s[rows, :]
        return carry

    lax.fori_loop(0, t // LRU_CHUNK, add_directions, 0)

    def emit(b, carry):
        hs = hf_s[pl.ds(b, t, stride=nb), :]
        o_ref[b] = (hs * _silu(gb_ref[b].astype(F32))).astype(o_ref.dtype)
        return carry

    lax.fori_loop(0, nb, emit, 0)


def _lru_branch(xb, gb, h0f, h0b, conv_w, conv_b, wg, lam):
    b, t, w = xb.shape
    nb = SUBLANES
    assert t % (LRU_SEGMENTS * LRU_UNROLL) == 0 and t % LRU_CHUNK == 0
    assert conv_w.shape[0] == CONV_PAD_L + 1 + CONV_PAD_R
    xspec = pl.BlockSpec((nb, t, LANES), lambda i, c: (i, 0, c))
    hspec = pl.BlockSpec((nb, LANES), lambda i, c: (i, c))
    scratch = [pltpu.VMEM(((t + CONV_PAD_L + CONV_PAD_R) * nb, LANES), F32)]
    scratch += [pltpu.VMEM((t * nb, LANES), F32) for _ in range(6)]
    return pl.pallas_call(
        _lru_kernel,
        grid=(b // nb, w // LANES),
        in_specs=[xspec, xspec, hspec, hspec,
                  pl.BlockSpec((conv_w.shape[0], LANES), lambda i, c: (0, c)),
                  pl.BlockSpec((1, LANES), lambda i, c: (0, c)),
                  pl.BlockSpec((1, 2 * LANES, 4 * LANES), lambda i, c: (c, 0, 0)),
                  pl.BlockSpec((2, LANES), lambda i, c: (0, c))],
        out_specs=[xspec, hspec, hspec],
        out_shape=[jax.ShapeDtypeStruct((b, t, w), BF16),
                   jax.ShapeDtypeStruct((b, w), F32),
                   jax.ShapeDtypeStruct((b, w), F32)],
        scratch_shapes=scratch,
        compiler_params=_params("parallel", "parallel"),
        name="lru_branch",
    )(xb, gb, h0f, h0b, conv_w, conv_b.reshape(1, w), wg, lam)


def _lru_gate_weights(wa, ba, wx, bx):
    def tiles(w):
        w2 = w.reshape(LRU_BLOCKS // 2, 2, LRU_BLOCK, LRU_BLOCK)
        z = jnp.zeros_like(w2[:, 0])
        return jnp.concatenate([jnp.concatenate([w2[:, 0], z], axis=2),
                                jnp.concatenate([z, w2[:, 1]], axis=2)], axis=1)
    wg = 0.5 * jnp.concatenate([tiles(wa[0]), tiles(wx[0]), tiles(wa[1]), tiles(wx[1])], axis=2)
    n_tiles = LRU_WIDTH // LANES
    bg = jnp.stack([ba[0], bx[0], ba[1], bx[1]], axis=0).reshape(4, n_tiles, LANES)
    bg = 0.5 * jnp.transpose(bg, (1, 0, 2)).reshape(n_tiles, 1, 4 * LANES)
    hi = bg.astype(BF16)
    lo = (bg - hi.astype(F32)).astype(BF16)
    rest = jnp.zeros((n_tiles, LANES - 2, 4 * LANES), BF16)
    return jnp.concatenate([wg.astype(BF16), hi, lo, rest], axis=1)


def _packed_tiles(ref, n_heads, head, bi=0):
    n_tiles = ref.shape[1] // (n_heads * KEY_TILE)
    return [ref[bi, pl.ds(c * KEY_TILE * n_heads + head, KEY_TILE, stride=n_heads), :].astype(BF16)
            for c in range(n_tiles)]


def _diff_kernel(*refs, packed, n_heads, lam_init):
    n_kv = len(packed)
    q_ref = refs[0]
    kv_refs = refs[1:1 + 2 * n_kv]
    g_ref, lamp_ref, subg_ref, o_ref, s_ref = refs[1 + 2 * n_kv:]
    low = _lane_low_half()
    tq = q_ref.shape[1]
    first_head = pl.program_id(1) * (q_ref.shape[2] // LANES)
    lp = lamp_ref[...]
    lam = (jnp.exp(jnp.sum(lp[0:1] * lp[1:2], axis=-1, keepdims=True))
           - jnp.exp(jnp.sum(lp[2:3] * lp[3:4], axis=-1, keepdims=True)) + lam_init)
    nb = q_ref.shape[0]
    tiles = [(bi, j) for bi in range(nb) for j in range(0, q_ref.shape[2], LANES)]
    q_blocks, kv_blocks = [], []
    for bi, j in tiles:
        k_tiles, v_tiles = [], []
        for i in range(n_kv):
            if packed[i]:
                head = first_head + j // LANES
                k_tiles += _packed_tiles(kv_refs[2 * i], n_heads, head, bi)
                v_tiles += _packed_tiles(kv_refs[2 * i + 1], n_heads, head, bi)
            else:
                n_tiles = kv_refs[2 * i].shape[1] // KEY_TILE
                k_tiles += _tiles(kv_refs[2 * i], n_tiles, j, bi)
                v_tiles += _tiles(kv_refs[2 * i + 1], n_tiles, j, bi)
        v_tiles = _values_and_ones(v_tiles)
        for r in range(0, tq, ROW_BLK):
            q_blocks += list(_split_lane_halves(q_ref[bi, r:r + ROW_BLK, j:j + LANES], low))
            kv_blocks += [(k_tiles, v_tiles)] * 2
    res = _attend(q_blocks, kv_blocks, s_ref)
    i = 0
    for bi, j in tiles:
        for r in range(0, tq, ROW_BLK):
            acc1, acc2 = res[i], res[i + 1]
            i += 2
            o = acc1[:, :LANES] / acc1[:, LANES:] - acc2[:, :LANES] * (lam / acc2[:, LANES:])
            o = o * lax.rsqrt(jnp.mean(o * o, axis=-1, keepdims=True) + EPS) * subg_ref[...] * (1.0 - lam_init)
            gate = _silu(g_ref[bi, r:r + ROW_BLK, j:j + LANES].astype(F32))
            o_ref[bi, r:r + ROW_BLK, j:j + LANES] = (o * gate).astype(o_ref.dtype)


def _diff_attn(q, kv_list, g, lam_params, sub_g, lam_init, tq, heads_per_step, batch_per_step=1):
    b, t, w = q.shape
    wb = heads_per_step * LANES
    nb = batch_per_step
    qspec = pl.BlockSpec((nb, tq, wb), lambda bi, h, qi: (bi, qi, h))
    n_heads = w // LANES
    in_specs = [qspec]
    args = [q]
    n_keys = 0
    for k, v, packed in kv_list:
        if packed:
            spec = pl.BlockSpec((nb, k.shape[1], LANES), lambda bi, h, qi: (bi, 0, 0))
            n_keys += k.shape[1] // n_heads
        else:
            spec = pl.BlockSpec((nb, k.shape[1], wb), lambda bi, h, qi: (bi, 0, h))
            n_keys += k.shape[1]
        in_specs += [spec, spec]
        args += [k, v]
    in_specs += [qspec,
                 pl.BlockSpec(lam_params.shape, lambda bi, h, qi: (0, 0)),
                 pl.BlockSpec((1, LANES), lambda bi, h, qi: (0, 0))]
    args += [g, lam_params, sub_g.reshape(1, LANES)]
    return pl.pallas_call(
        functools.partial(_diff_kernel, packed=tuple(p for _, _, p in kv_list), n_heads=n_heads,
                          lam_init=lam_init),
        grid=(b // nb, w // wb, t // tq),
        in_specs=in_specs,
        out_specs=qspec,
        out_shape=jax.ShapeDtypeStruct((b, t, w), BF16),
        scratch_shapes=[pltpu.VMEM((SCORE_SLOTS * ROW_BLK, n_keys), F32)],
        compiler_params=_params("parallel", "parallel", "parallel"),
        name="diff_attn",
    )(*args)


def _rope_tables(t_len):
    m = DIFF_HEAD_DIM // 4
    t = jnp.arange(t_len)
    rows = (t // GRID_W).astype(F32)
    cols = (t % GRID_W).astype(F32)
    inv = ROPE_BASE ** (-jnp.arange(m, dtype=F32) / m)
    ang_r = rows[:, None] * inv[None, :]
    ang_c = cols[:, None] * inv[None, :]
    cos = jnp.concatenate([jnp.cos(ang_r)] * 2 + [jnp.cos(ang_c)] * 2, axis=1)
    sin = jnp.concatenate([-jnp.sin(ang_r), jnp.sin(ang_r), -jnp.sin(ang_c), jnp.sin(ang_c)], axis=1)
    reps = LANES // DIFF_HEAD_DIM
    return jnp.tile(cos, (1, reps)), jnp.tile(sin, (1, reps))


def kernel(x_prompt, x_sample, c, cache_na_k, cache_na_v, state_lru, cache_diff_k, cache_diff_v, c_ctx,
           e_norm, e_ada_w, e_ada_b, e_w_in, e_rpb, e_conv_w, e_conv_b, e_lru_wa, e_lru_ba, e_lru_wx, e_lru_bx,
           e_lru_lam, e_w_out, o_norm, o_ada_w, o_ada_b, o_w_in, o_lq1, o_lk1, o_lq2, o_lk2, o_sub_g, o_w_out,
           final_norm):
    bp, tp, d = x_prompt.shape
    bs, ts, _ = x_sample.shape
    past = cache_na_k.shape[2]
    pad_rows = 2 * SUBLANES - bs - 1
    cc = jnp.concatenate([c, c_ctx[None, :], jnp.zeros((pad_rows, d), F32)], axis=0)
    nw, lw, dw = NA_WIDTH, LRU_WIDTH, DIFF_WIDTH
    rope_tables = _rope_tables(ts) if DEPTH > 1 else None

    def layer_params(i):
        j = i // 2
        if i % 2 == 0:
            return e_ada_w[j], e_ada_b[j], e_norm[j], e_w_in[j], e_w_out[j]
        return o_ada_w[j], o_ada_b[j], o_norm[j], o_w_in[j], o_w_out[j]

    mods = []
    for i in range(DEPTH):
        ada_w, ada_b = layer_params(i)[:2]
        m = _modulation(cc, ada_w, ada_b)
        mods.append((m[:bs].reshape(bs, 1, 3 * d), m[bs:bs + 1].reshape(1, 1, 3 * d)))

    def projection(i, latent):
        norm_g, w_in = layer_params(i)[2:4]
        mod = mods[i][0 if latent else 1]
        kv_dtype = BF16 if latent else F32
        if i % 2 == 0:
            qmul = NA_HEAD_DIM ** -0.5 * LOG2E
            outs = [(nw, BF16, False, qmul, False), (nw, kv_dtype, False, None, False),
                    (nw, kv_dtype, False, None, False), (nw, BF16, False, None, False),
                    (lw, F32, False, None, False), (lw, BF16, False, None, False)]
            return mod, norm_g, w_in, outs, None
        qmul = DIFF_HEAD_DIM ** -0.5 * LOG2E
        outs = [(dw, BF16, latent, qmul, False), (dw, kv_dtype, latent, None, not latent),
                (dw, kv_dtype, False, None, not latent), (dw, BF16, False, None, False)]
        return mod, norm_g, w_in, outs, rope_tables if latent else None

    xp, xs = x_prompt, x_sample
    _, proj_p = _stage(xp, nxt=projection(0, False), tm=512)
    _, proj_s = _stage(xs, nxt=projection(0, True), tm=512)
    na_k, na_v, lru_s, df_k, df_v = [], [], [], [], []
    for i in range(DEPTH):
        j = i // 2
        w_out = layer_params(i)[4]
        if i % 2 == 0:
            wg = _lru_gate_weights(e_lru_wa[j], e_lru_ba[j], e_lru_wx[j], e_lru_bx[j])

            q, k, v, ga, xb, gb = proj_p
            oa = _ctx_attn(q, k, v, ga)
            zeros = jnp.zeros((bp, lw), F32)
            ob, ff, fb = _lru_branch(xb, gb, zeros, zeros, e_conv_w[j], e_conv_b[j], wg, e_lru_lam[j])
            mixed_p = [oa, ob]
            na_k.append(k.reshape(bp, tp, NA_HEADS, NA_HEAD_DIM))
            na_v.append(v.reshape(bp, tp, NA_HEADS, NA_HEAD_DIM))
            lru_s.append(jnp.stack([ff, fb], axis=1))

            q, k, v, ga, xb, gb = proj_s
            ck = cache_na_k[:, j].reshape(bs, past, nw)
            cv = cache_na_v[:, j].reshape(bs, past, nw)
            oa = _na_attn(q, k, v, ck, cv, e_rpb[j], ga)
            ob, _, _ = _lru_branch(xb, gb, state_lru[:, j, 0], state_lru[:, j, 1],
                                   e_conv_w[j], e_conv_b[j], wg, e_lru_lam[j])
            mixed_s = [oa, ob]
        else:
            lam_init = 0.8 - 0.6 * math.exp(-0.3 * i)
            lam_params = jnp.stack([o_lq1[j], o_lk1[j], o_lq2[j], o_lk2[j]], axis=0)

            q, k, v, g = proj_p
            mixed_p = [_diff_attn(q, [(k, v, True)], g, lam_params, o_sub_g[j], lam_init, tq=tp,
                                  heads_per_step=DIFF_HEADS, batch_per_step=CTX_BATCH_PER_STEP)]
            df_k.append(k.reshape(bp, tp, DIFF_HEADS, 2 * DIFF_HEAD_DIM))
            df_v.append(v.reshape(bp, tp, DIFF_HEADS, 2 * DIFF_HEAD_DIM))

            q, k, v, g = proj_s
            ck = cache_diff_k[:, j].reshape(bs, past * DIFF_HEADS, 2 * DIFF_HEAD_DIM)
            cv = cache_diff_v[:, j].reshape(bs, past * DIFF_HEADS, 2 * DIFF_HEAD_DIM)
            mixed_s = [_diff_attn(q, [(ck, cv, True), (k, v, False)], g, lam_params, o_sub_g[j], lam_init, tq=ts,
                                  heads_per_step=4)]

        if i == DEPTH - 1:
            xp, _ = _stage(xp, prev=(mixed_p, w_out, mods[i][1]), final_g=final_norm, tm=512)
            xs, _ = _stage(xs, prev=(mixed_s, w_out, mods[i][0]), final_g=final_norm, tm=512)
        else:
            xp, proj_p = _stage(xp, prev=(mixed_p, w_out, mods[i][1]), nxt=projection(i + 1, False), tm=512)
            xs, proj_s = _stage(xs, prev=(mixed_s, w_out, mods[i][0]), nxt=projection(i + 1, True), tm=512)

    return (xp, xs, jnp.stack(na_k, axis=1), jnp.stack(na_v, axis=1), jnp.stack(lru_s, axis=1),
            jnp.stack(df_k, axis=1), jnp.stack(df_v, axis=1))
```

```python
import functools
import math

import jax
import jax.numpy as jnp
from jax import lax
from jax.experimental import pallas as pl
from jax.experimental.pallas import tpu as pltpu

D_MODEL = 1024
DEPTH = 2
GRID_W = 64
NA_HEADS = 8
NA_HEAD_DIM = 64
NA_WIDTH = NA_HEADS * NA_HEAD_DIM
NA_ROWS_MAX = 8
NA_COLS = 16
RPB_ROWS = 2 * NA_ROWS_MAX - 1
RPB_COLS = 2 * NA_COLS - 1
LRU_WIDTH = 512
LRU_BLOCKS = 8
LRU_BLOCK = LRU_WIDTH // LRU_BLOCKS
LRU_C = 8.0
DIFF_HEADS = 8
DIFF_HEAD_DIM = 64
DIFF_WIDTH = DIFF_HEADS * 2 * DIFF_HEAD_DIM
ROPE_BASE = 10000.0
EPS = 1e-6
NEG_INF = -1e30
LOG2E = math.log2(math.e)

LANES = 128
SUBLANES = 8
MXU_DIM = 256
VMEM_LIMIT = 56 * 1024 * 1024

CAST_COLS = 512
LRU_SEGMENTS = 4
LRU_UNROLL = 8
LRU_CHUNK = 128
CONV_PAD_L = 2
CONV_PAD_R = 1
ROW_BLK = MXU_DIM
KEY_TILE = MXU_DIM
SCORE_SLOTS = 4

NA_Q_ROWS = 4
NA_K_ROWS = 12
NA_PAIR_TILES = 16
NA_PAIRS_PER_STEP = 2
CTX_BATCH_PER_STEP = 2

F32 = jnp.float32
BF16 = jnp.bfloat16


def _params(*semantics):
    return pltpu.CompilerParams(dimension_semantics=semantics, vmem_limit_bytes=VMEM_LIMIT)


def _sigmoid(x):
    return 0.5 * jnp.tanh(0.5 * x) + 0.5


def _silu(x):
    return x * _sigmoid(x)


def _dot(a, b):
    return jnp.dot(a, b, preferred_element_type=F32)


def _dot_nt(a, b):
    return lax.dot_general(a, b, (((1,), (1,)), ((), ())), preferred_element_type=F32)


def _mod_kernel(c_ref, w_ref, b_ref, o_ref):
    s = _silu(c_ref[...])
    o_ref[...] = _dot(s.astype(BF16), w_ref[...].astype(BF16)) + b_ref[...]


def _modulation(cc, w, b):
    rows, d = cc.shape
    n = w.shape[1]
    tn = 1024
    return pl.pallas_call(
        _mod_kernel,
        grid=(n // tn,),
        in_specs=[pl.BlockSpec((rows, d), lambda j: (0, 0)),
                  pl.BlockSpec((d, tn), lambda j: (0, j)),
                  pl.BlockSpec((1, tn), lambda j: (0, j))],
        out_specs=pl.BlockSpec((rows, tn), lambda j: (0, j)),
        out_shape=jax.ShapeDtypeStruct((rows, n), F32),
        compiler_params=_params("arbitrary"),
        name="modulation",
    )(cc, w, b.reshape(1, n))


def _rope_block(x, cos, sin_signed, low16):
    partner = jnp.where(low16, pltpu.roll(x, LANES - 16, axis=1), pltpu.roll(x, 16, axis=1))
    return x * cos + partner * sin_signed


def _cast_weights_once(w_ref, wbf_ref):
    @pl.when((pl.program_id(0) == 0) & (pl.program_id(1) == 0))
    def _():
        for off in range(0, w_ref.shape[1], CAST_COLS):
            wbf_ref[:, off:off + CAST_COLS] = w_ref[:, off:off + CAST_COLS].astype(BF16)


def _norm_project(x, mod_ref, g_ref, wbf_ref, rope_refs, out_refs, outs):
    tm, d = x.shape
    y = x * lax.rsqrt(jnp.mean(x * x, axis=-1, keepdims=True) + EPS) * g_ref[...]
    mod = mod_ref[0]
    h = (y * (1.0 + mod[:, d:2 * d]) + mod[:, :d]).astype(BF16)
    if rope_refs:
        cos = rope_refs[0][...]
        sin = rope_refs[1][...]
        lane = lax.broadcasted_iota(jnp.int32, (1, LANES), 1)
        low16 = (lane & 16) == 0
    off = 0
    for o_ref, (width, roped, mul, packed) in zip(out_refs, outs):
        acc = _dot(h, wbf_ref[:, off:off + width])
        if roped:
            acc = jnp.concatenate(
                [_rope_block(acc[:, j:j + LANES], cos, sin, low16) for j in range(0, width, LANES)], axis=1)
        if mul is not None:
            acc = acc * mul
        if packed:
            n_heads = width // LANES
            t = o_ref.shape[1] // n_heads
            for bi in range(o_ref.shape[0]):
                for hh in range(n_heads):
                    o_ref[bi, pl.ds(hh, t, stride=n_heads), :] = (
                        acc[bi * t:(bi + 1) * t, hh * LANES:(hh + 1) * LANES].astype(o_ref.dtype))
        else:
            o_ref[...] = acc.astype(o_ref.dtype).reshape(o_ref.shape)
        off += width


def _token_rows(ref):
    return ref[...].reshape(-1, ref.shape[-1])


def _gated_residual(o_refs, wbf_ref, x_ref, mod_ref):
    d = x_ref.shape[-1]
    acc, row = None, 0
    for o_ref in o_refs:
        k = o_ref.shape[-1]
        part = _dot(_token_rows(o_ref), wbf_ref[row:row + k, :])
        acc = part if acc is None else acc + part
        row += k
    return _token_rows(x_ref) + mod_ref[0][:, 2 * d:] * acc


def _stage_kernel(*refs, n_o, final, outs, rope):
    refs = list(refs)
    take = lambda n: [refs.pop(0) for _ in range(n)]
    if n_o:
        o_refs = take(n_o)
        wout_ref, x_ref, mod_prev_ref = take(3)
        fin_ref = take(1)[0] if final else None
    else:
        x_ref, = take(1)
    if outs:
        mod_ref, g_ref, win_ref = take(3)
        rope_refs = take(2) if rope else []
    xout_ref = take(1)[0] if n_o else None
    out_refs = take(len(outs))
    if n_o:
        wout_bf = take(1)[0]
        _cast_weights_once(wout_ref, wout_bf)
        x = _gated_residual(o_refs, wout_bf, x_ref, mod_prev_ref)
        if final:
            x_new = x * lax.rsqrt(jnp.mean(x * x, axis=-1, keepdims=True) + EPS) * fin_ref[...]
        else:
            x_new = x
        xout_ref[...] = x_new.reshape(xout_ref.shape)
    else:
        x = _token_rows(x_ref)
    if outs:
        win_bf = take(1)[0]
        _cast_weights_once(win_ref, win_bf)
        _norm_project(x, mod_ref, g_ref, win_bf, rope_refs, out_refs, outs)


def _stage(x, prev=None, nxt=None, final_g=None, tm=256):
    b, t, d = x.shape
    shared_mod = all(m.shape[0] == 1 for m in ([prev[2]] if prev else []) + ([nxt[0]] if nxt else []))
    bb = max(1, tm // t) if shared_mod else 1
    tm = min(tm, t)
    tile = lambda width: pl.BlockSpec((bb, tm, width), lambda i, j: (i, j, 0))
    mod_spec = lambda mod: pl.BlockSpec(
        (1, 1, 3 * d), (lambda i, j: (i, 0, 0)) if mod.shape[0] > 1 else (lambda i, j: (0, 0, 0)))
    resident = lambda w: pl.BlockSpec(w.shape, lambda i, j: (0, 0), pipeline_mode=pl.Buffered(1))
    row_spec = pl.BlockSpec((1, d), lambda i, j: (0, 0))
    in_specs, args, out_specs, out_shape, scratch = [], [], [], [], []
    n_o, outs, rope = 0, (), False
    if prev is not None:
        o_list, w_out, mod_prev = prev
        n_o = len(o_list)
        in_specs += [tile(o.shape[-1]) for o in o_list] + [resident(w_out), tile(d), mod_spec(mod_prev)]
        args += list(o_list) + [w_out, x, mod_prev]
        if final_g is not None:
            in_specs.append(row_spec)
            args.append(final_g.reshape(1, d))
        out_specs.append(tile(d))
        out_shape.append(jax.ShapeDtypeStruct((b, t, d), F32))
        scratch.append(pltpu.VMEM(w_out.shape, BF16))
    else:
        in_specs.append(tile(d))
        args.append(x)
    if nxt is not None:
        mod_next, norm_g, w_in, out_list, rope_tables = nxt
        in_specs += [mod_spec(mod_next), row_spec, resident(w_in)]
        args += [mod_next, norm_g.reshape(1, d), w_in]
        rope = rope_tables is not None
        if rope:
            in_specs += [pl.BlockSpec((tm, LANES), lambda i, j: (j, 0))] * 2
            args += list(rope_tables)
        for width, dt, _, _, packed in out_list:
            if packed:
                heads = width // LANES
                out_specs.append(pl.BlockSpec((bb, tm * heads, LANES), lambda i, j: (i, j, 0)))
                out_shape.append(jax.ShapeDtypeStruct((b, t * heads, LANES), dt))
            else:
                out_specs.append(tile(width))
                out_shape.append(jax.ShapeDtypeStruct((b, t, width), dt))
        outs = tuple((w_, r, m, p) for w_, _, r, m, p in out_list)
        scratch.append(pltpu.VMEM(w_in.shape, BF16))
    res = pl.pallas_call(
        functools.partial(_stage_kernel, n_o=n_o, final=final_g is not None, outs=outs, rope=rope),
        grid=(b // bb, t // tm),
        in_specs=in_specs,
        out_specs=out_specs,
        out_shape=out_shape,
        scratch_shapes=scratch,
        compiler_params=_params("arbitrary", "arbitrary"),
        name="stage",
    )(*args)
    return (res[0], list(res[1:])) if n_o else (None, list(res))


def _attend(q_blocks, kv_blocks, s_ref, score_fn=None):
    n_rb = len(q_blocks)

    def tile_of(rb, c):
        slot = rb % SCORE_SLOTS
        return (slice(slot * ROW_BLK, (slot + 1) * ROW_BLK), slice(c * KEY_TILE, (c + 1) * KEY_TILE))

    def score_tile(rb, c):
        s_ref[tile_of(rb, c)] = _dot_nt(q_blocks[rb], kv_blocks[rb][0][c])

    def max_tile(rb, c, m_acc):
        s = s_ref[tile_of(rb, c)]
        adjusted = None if score_fn is None else score_fn(rb, c, s)
        if adjusted is not None:
            s = adjusted
            s_ref[tile_of(rb, c)] = s
        mt = jnp.maximum(s[:, :LANES], s[:, LANES:])
        return mt if m_acc is None else jnp.maximum(m_acc, mt)

    def n_tiles(rb):
        return len(kv_blocks[rb][0]) if 0 <= rb < n_rb else 0

    outs = []
    row_max = {}
    for step in range(n_rb + 2):
        rb_score, rb_max, rb_exp = step, step - 1, step - 2
        m_acc, acc = None, None
        for c in range(max(n_tiles(rb_score), n_tiles(rb_max), n_tiles(rb_exp))):
            if c < n_tiles(rb_score):
                score_tile(rb_score, c)
            if c < n_tiles(rb_max):
                m_acc = max_tile(rb_max, c, m_acc)
            if c < n_tiles(rb_exp):
                e = jnp.exp2(s_ref[tile_of(rb_exp, c)] - row_max[rb_exp])
                pv = _dot(e.astype(BF16), kv_blocks[rb_exp][1][c])
                acc = pv if acc is None else acc + pv
        if m_acc is not None:
            row_max[rb_max] = m_acc.max(axis=-1, keepdims=True)
        if acc is not None:
            outs.append(acc)
    return outs


def _lane_low_half():
    return lax.broadcasted_iota(jnp.int32, (1, LANES), 1) < (LANES // 2)


def _values_and_ones(v_tiles):
    ones = jnp.ones((KEY_TILE, LANES), BF16)
    return [jnp.concatenate([v, ones], axis=1) for v in v_tiles]


def _normalise_pair(acc_lo, acc_hi, low):
    return jnp.where(low, acc_lo[:, :LANES] / acc_lo[:, LANES:], acc_hi[:, :LANES] / acc_hi[:, LANES:])


def _split_lane_halves(q, low):
    zero = jnp.zeros_like(q)
    return jnp.where(low, q, zero), jnp.where(low, zero, q)


def _tiles(ref, n_tiles, j, bi=0):
    return [ref[bi, c * KEY_TILE:(c + 1) * KEY_TILE, j:j + LANES].astype(BF16) for c in range(n_tiles)]


def _ctx_attn_kernel(q_ref, k_ref, v_ref, g_ref, o_ref, s_ref):
    low = _lane_low_half()
    nb, tq, w = q_ref.shape
    n_tiles = k_ref.shape[1] // KEY_TILE
    tiles = [(bi, j, r) for bi in range(nb) for j in range(0, w, LANES) for r in range(0, tq, ROW_BLK)]
    q_blocks, kv_blocks, kv = [], [], {}
    for bi, j, r in tiles:
        if (bi, j) not in kv:
            kv[bi, j] = (_tiles(k_ref, n_tiles, j, bi), _values_and_ones(_tiles(v_ref, n_tiles, j, bi)))
        q_blocks += list(_split_lane_halves(q_ref[bi, r:r + ROW_BLK, j:j + LANES], low))
        kv_blocks += [kv[bi, j]] * 2
    res = _attend(q_blocks, kv_blocks, s_ref)
    for i, (bi, j, r) in enumerate(tiles):
        o = _normalise_pair(res[2 * i], res[2 * i + 1], low)
        gate = _silu(g_ref[bi, r:r + ROW_BLK, j:j + LANES].astype(F32))
        o_ref[bi, r:r + ROW_BLK, j:j + LANES] = (o * gate).astype(o_ref.dtype)


def _ctx_attn(q, k, v, g):
    b, t, w = q.shape
    nb = CTX_BATCH_PER_STEP
    spec = pl.BlockSpec((nb, t, w), lambda i: (i, 0, 0))
    return pl.pallas_call(
        _ctx_attn_kernel,
        grid=(b // nb,),
        in_specs=[spec] * 4,
        out_specs=spec,
        out_shape=jax.ShapeDtypeStruct((b, t, w), BF16),
        scratch_shapes=[pltpu.VMEM((SCORE_SLOTS * ROW_BLK, k.shape[1]), F32)],
        compiler_params=_params("parallel"),
        name="ctx_attn",
    )(q, k, v, g)


def _na_pair_tiles(rpb_ref, head, pt_ref):
    right = lax.broadcasted_iota(jnp.int32, (GRID_W, LANES), 1) >= GRID_W
    zero_row = jnp.zeros((1, LANES), F32)

    def toeplitz(row, lane0):
        shift = (lane0 - (NA_COLS - 1)) % LANES
        return pltpu.roll(jnp.broadcast_to(row, (GRID_W, LANES)), shift, axis=1, stride=1, stride_axis=0)

    for p in range(NA_PAIR_TILES):
        left_row = rpb_ref[head, p - 1:p, :] if p >= 1 else zero_row
        right_row = rpb_ref[head, p:p + 1, :]
        pt_ref[head, p] = jnp.where(right, toeplitz(right_row, GRID_W), toeplitz(left_row, 0)) * LOG2E


def _na_row_start(r, n_rows):
    kr = min(NA_ROWS_MAX, n_rows)
    return min(max(r - kr // 2, 0), n_rows - kr)


def _na_window(grp, n_rows):
    kr = min(NA_ROWS_MAX, n_rows)
    starts = [_na_row_start(grp * NA_Q_ROWS + i, n_rows) for i in range(NA_Q_ROWS)]
    rows_per_tile = KEY_TILE // GRID_W
    need = -(-(max(starts) + kr - min(starts)) // rows_per_tile) * rows_per_tile
    return min(min(starts), n_rows - need), need


def _na_kernel(q_ref, k_ref, v_ref, ck_ref, cv_ref, rpb_ref, g_ref, o_ref, pt_ref, badd_ref, bcap_ref, s_ref, *,
               n_rows):
    kr = min(NA_ROWS_MAX, n_rows)
    n_grp = n_rows // NA_Q_ROWS
    n_pairs = q_ref.shape[2] // LANES
    low = _lane_low_half()

    @pl.when(pl.program_id(1) == 0)
    def _build_bias():
        for head in range(2 * n_pairs):
            _na_pair_tiles(rpb_ref, head, pt_ref)
        qcol = lax.broadcasted_iota(jnp.int32, (GRID_W, LANES), 0)
        lane = lax.broadcasted_iota(jnp.int32, (GRID_W, LANES), 1)
        right = lane >= GRID_W
        kcol = lane & (GRID_W - 1)
        col_start = jnp.clip(qcol - NA_COLS // 2, 0, GRID_W - NA_COLS)
        col_ok = (kcol >= col_start) & (kcol < col_start + NA_COLS)
        for grp in range(n_grp):
            krow0, key_rows = _na_window(grp, n_rows)
            for i in range(NA_Q_ROWS):
                r = grp * NA_Q_ROWS + i
                row_start = _na_row_start(r, n_rows)
                for m in range(key_rows // 2):
                    key_row = krow0 + 2 * m
                    ok_l = row_start <= key_row < row_start + kr
                    ok_r = row_start <= key_row + 1 < row_start + kr
                    rows = slice(i * GRID_W, (i + 1) * GRID_W)
                    cols = slice(m * LANES, (m + 1) * LANES)
                    if not (ok_l or ok_r):
                        bcap_ref[grp, rows, cols] = jnp.full((GRID_W, LANES), NEG_INF, F32)
                        for head in range(2 * n_pairs):
                            badd_ref[head * n_grp + grp, rows, cols] = jnp.zeros((GRID_W, LANES), F32)
                        continue
                    ok = col_ok
                    if not ok_r:
                        ok = ok & jnp.logical_not(right)
                    if not ok_l:
                        ok = ok & right
                    bcap_ref[grp, rows, cols] = jnp.where(ok, jnp.inf, NEG_INF)
                    p = key_row - r + NA_ROWS_MAX
                    for head in range(2 * n_pairs):
                        badd_ref[head * n_grp + grp, rows, cols] = jnp.where(ok, pt_ref[head, p], 0.0)

    n_ctx = ck_ref.shape[1] // KEY_TILE
    q_blocks, kv_blocks, blocks = [], [], []
    for pair in range(n_pairs):
        lanes = slice(pair * LANES, (pair + 1) * LANES)
        ctx_k = _tiles(ck_ref, n_ctx, pair * LANES)
        ctx_v = _tiles(cv_ref, n_ctx, pair * LANES)
        for grp in range(n_grp):
            krow0, key_rows = _na_window(grp, n_rows)
            k0 = krow0 * GRID_W
            win = [slice(k0 + c * KEY_TILE, k0 + (c + 1) * KEY_TILE) for c in range(key_rows * GRID_W // KEY_TILE)]
            kv = ([k_ref[0, w, lanes] for w in win] + ctx_k,
                  _values_and_ones([v_ref[0, w, lanes] for w in win] + ctx_v))
            q_blocks += list(_split_lane_halves(q_ref[0, grp * ROW_BLK:(grp + 1) * ROW_BLK, lanes], low))
            kv_blocks += [kv, kv]
            blocks += [(2 * pair, grp, len(win)), (2 * pair + 1, grp, len(win))]

    def masked_bias(rb, c, s):
        head, grp, n_win = blocks[rb]
        if c >= n_win:
            return None
        cols = slice(c * KEY_TILE, (c + 1) * KEY_TILE)
        return jnp.minimum(s + badd_ref[head * n_grp + grp, :, cols], bcap_ref[grp, :, cols])

    res = _attend(q_blocks, kv_blocks, s_ref, masked_bias)
    for pair in range(n_pairs):
        lanes = slice(pair * LANES, (pair + 1) * LANES)
        for grp in range(n_grp):
            rb = 2 * (pair * n_grp + grp)
            o = _normalise_pair(res[rb], res[rb + 1], low)
            rows = slice(grp * ROW_BLK, (grp + 1) * ROW_BLK)
            o_ref[0, rows, lanes] = (o * _silu(g_ref[0, rows, lanes].astype(F32))).astype(o_ref.dtype)


def _na_attn(q, k, v, ck, cv, rpb, g):
    heads, rpb_rows, rpb_cols = rpb.shape
    rpb = jnp.pad(rpb, ((0, 0), (0, NA_PAIR_TILES - rpb_rows), (0, LANES - rpb_cols)))
    b, t, w = q.shape
    c = ck.shape[1]
    n_rows = t // GRID_W
    assert NA_Q_ROWS * GRID_W == ROW_BLK and n_rows % NA_Q_ROWS == 0 and n_rows >= NA_K_ROWS
    n_grp = n_rows // NA_Q_ROWS
    n_win = NA_K_ROWS * GRID_W
    wb = NA_PAIRS_PER_STEP * LANES
    hb = 2 * NA_PAIRS_PER_STEP
    tspec = pl.BlockSpec((1, t, wb), lambda hp, bi: (bi, 0, hp))
    cspec = pl.BlockSpec((1, c, wb), lambda hp, bi: (bi, 0, hp))
    ptspec = pl.BlockSpec((hb, NA_PAIR_TILES, LANES), lambda hp, bi: (hp, 0, 0))
    return pl.pallas_call(
        functools.partial(_na_kernel, n_rows=n_rows),
        grid=(w // wb, b),
        in_specs=[tspec, tspec, tspec, cspec, cspec, ptspec, tspec],
        out_specs=tspec,
        out_shape=jax.ShapeDtypeStruct((b, t, w), BF16),
        scratch_shapes=[pltpu.VMEM((hb, NA_PAIR_TILES, GRID_W, LANES), F32),
                        pltpu.VMEM((hb * n_grp, ROW_BLK, n_win), F32),
                        pltpu.VMEM((n_grp, ROW_BLK, n_win), F32),
                        pltpu.VMEM((SCORE_SLOTS * ROW_BLK, n_win + c), F32)],
        compiler_params=_params("parallel", "arbitrary"),
        name="na_attn",
    )(q, k, v, ck, cv, rpb, g)


def _softplus(x):
    return jnp.maximum(x, 0.0) + jnp.log(1.0 + jnp.exp(-jnp.abs(x)))


def _lru_kernel(x_ref, gb_ref, h0f_ref, h0b_ref, cw_ref, cb_ref, wg_ref, lam_ref,
                o_ref, ff_ref, fb_ref, xt_s, af_s, uf_s, ab_s, ub_s, hf_s, hb_s):
    nb, t, _ = x_ref.shape
    pad = CONV_PAD_L * nb
    cw = cw_ref[...]
    cb = cb_ref[...]
    wg = wg_ref[0]
    kk = (-0.5 * LRU_C * LOG2E) * _softplus(-lam_ref[...])

    xt_s[0:pad, :] = jnp.zeros((pad, LANES), F32)
    xt_s[pad + t * nb:pad + (t + CONV_PAD_R) * nb, :] = jnp.zeros((CONV_PAD_R * nb, LANES), F32)

    def to_time_major(b, carry):
        xt_s[pl.ds(pad + b, t, stride=nb), :] = x_ref[b]
        return carry

    lax.fori_loop(0, nb, to_time_major, 0)

    rows_per_chunk = LRU_CHUNK * nb
    lane = lax.broadcasted_iota(jnp.int32, (rows_per_chunk, LANES), 1)
    bias_cols = jnp.where(lane < 2, 1.0, 0.0).astype(BF16)

    def gates(c, carry):
        r0 = pl.multiple_of(c * rows_per_chunk, rows_per_chunk)

        def tap(i):
            return xt_s[pl.ds(r0 + i * nb, rows_per_chunk), :]

        x = tap(2)
        xc = cw[0:1] * tap(0) + cw[1:2] * tap(1) + cw[2:3] * x + cw[3:4] * tap(3) + cb
        th = jnp.tanh(_dot(jnp.concatenate([xc.astype(BF16), bias_cols], axis=1), wg))
        xh = 0.5 * xc
        rows = pl.ds(r0, rows_per_chunk)
        for d, (a_s, u_s) in enumerate(((af_s, uf_s), (ab_s, ub_s))):
            t_a = th[:, (2 * d) * LANES:(2 * d + 1) * LANES]
            t_x = th[:, (2 * d + 1) * LANES:(2 * d + 2) * LANES]
            a = jnp.exp2(kk[d:d + 1] * t_a + kk[d:d + 1])
            y = 1.0 - a * a
            sq = y * lax.rsqrt(jnp.maximum(y, 1e-30))
            a_s[rows, :] = a
            u_s[rows, :] = sq * (xh * t_x + xh)
        return carry

    lax.fori_loop(0, t // LRU_CHUNK, gates, 0)

    ts = t // LRU_SEGMENTS
    refs = ((af_s, uf_s), (ab_s, ub_s))

    slab = nb * LRU_UNROLL

    def slab_rows(d, j, i):
        base = (j * ts + i * LRU_UNROLL) * nb
        return pl.ds(pl.multiple_of(base if d == 0 else t * nb - slab - base, slab), slab)

    def step_of(d, block, k):
        kk_ = k if d == 0 else LRU_UNROLL - 1 - k
        return block[kk_ * nb:(kk_ + 1) * nb]

    def compose(i, carry):
        state = list(carry)
        for d in range(2):
            for j in range(LRU_SEGMENTS - 1):
                n = d * (LRU_SEGMENTS - 1) + j
                a_blk = refs[d][0][slab_rows(d, j, i), :]
                u_blk = refs[d][1][slab_rows(d, j, i), :]
                g, big_a = state[n]
                for k in range(LRU_UNROLL):
                    a = step_of(d, a_blk, k)
                    g, big_a = a * g + step_of(d, u_blk, k), a * big_a
                state[n] = (g, big_a)
        return tuple(state)

    ones = jnp.ones((nb, LANES), F32)
    h0 = (h0f_ref[...], h0b_ref[...])
    init = tuple((h0[d] if j == 0 else jnp.zeros((nb, LANES), F32), ones)
                 for d in range(2) for j in range(LRU_SEGMENTS - 1))
    ends = lax.fori_loop(0, ts // LRU_UNROLL, compose, init)
    starts = []
    for d in range(2):
        h = h0[d]
        for j in range(LRU_SEGMENTS):
            starts.append(h)
            if j < LRU_SEGMENTS - 1:
                g, big_a = ends[d * (LRU_SEGMENTS - 1) + j]
                h = g if j == 0 else g + big_a * h

    def rerun(i, carry):
        state = list(carry)
        for d, h_s in enumerate((hf_s, hb_s)):
            for j in range(LRU_SEGMENTS):
                n = d * LRU_SEGMENTS + j
                a_blk = refs[d][0][slab_rows(d, j, i), :]
                u_blk = refs[d][1][slab_rows(d, j, i), :]
                hs = []
                for k in range(LRU_UNROLL):
                    state[n] = step_of(d, a_blk, k) * state[n] + step_of(d, u_blk, k)
                    hs.append(state[n])
                h_s[slab_rows(d, j, i), :] = jnp.concatenate(hs if d == 0 else hs[::-1], axis=0)
        return tuple(state)

    final = lax.fori_loop(0, ts // LRU_UNROLL, rerun, tuple(starts))
    ff_ref[...] = final[LRU_SEGMENTS - 1]
    fb_ref[...] = final[2 * LRU_SEGMENTS - 1]

    def add_directions(c, carry):
        rows = pl.ds(pl.multiple_of(c * rows_per_chunk, rows_per_chunk), rows_per_chunk)
        hf_s[rows, :] = hf_s[rows, :] + hb_s[rows, :]
        return carry

    lax.fori_loop(0, t // LRU_CHUNK, add_directions, 0)

    def emit(b, carry):
        hs = hf_s[pl.ds(b, t, stride=nb), :]
        o_ref[b] = (hs * _silu(gb_ref[b].astype(F32))).astype(o_ref.dtype)
        return carry

    lax.fori_loop(0, nb, emit, 0)


def _lru_branch(xb, gb, h0f, h0b, conv_w, conv_b, wg, lam):
    b, t, w = xb.shape
    nb = SUBLANES
    assert t % (LRU_SEGMENTS * LRU_UNROLL) == 0 and t % LRU_CHUNK == 0
    assert conv_w.shape[0] == CONV_PAD_L + 1 + CONV_PAD_R
    xspec = pl.BlockSpec((nb, t, LANES), lambda i, c: (i, 0, c))
    hspec = pl.BlockSpec((nb, LANES), lambda i, c: (i, c))
    scratch = [pltpu.VMEM(((t + CONV_PAD_L + CONV_PAD_R) * nb, LANES), F32)]
    scratch += [pltpu.VMEM((t * nb, LANES), F32) for _ in range(6)]
    return pl.pallas_call(
        _lru_kernel,
        grid=(b // nb, w // LANES),
        in_specs=[xspec, xspec, hspec, hspec,
                  pl.BlockSpec((conv_w.shape[0], LANES), lambda i, c: (0, c)),
                  pl.BlockSpec((1, LANES), lambda i, c: (0, c)),
                  pl.BlockSpec((1, 2 * LANES, 4 * LANES), lambda i, c: (c, 0, 0)),
                  pl.BlockSpec((2, LANES), lambda i, c: (0, c))],
        out_specs=[xspec, hspec, hspec],
        out_shape=[jax.ShapeDtypeStruct((b, t, w), BF16),
                   jax.ShapeDtypeStruct((b, w), F32),
                   jax.ShapeDtypeStruct((b, w), F32)],
        scratch_shapes=scratch,
        compiler_params=_params("parallel", "parallel"),
        name="lru_branch",
    )(xb, gb, h0f, h0b, conv_w, conv_b.reshape(1, w), wg, lam)


def _lru_gate_weights(wa, ba, wx, bx):
    def tiles(w):
        w2 = w.reshape(LRU_BLOCKS // 2, 2, LRU_BLOCK, LRU_BLOCK)
        z = jnp.zeros_like(w2[:, 0])
        return jnp.concatenate([jnp.concatenate([w2[:, 0], z], axis=2),
                                jnp.concatenate([z, w2[:, 1]], axis=2)], axis=1)
    wg = 0.5 * jnp.concatenate([tiles(wa[0]), tiles(wx[0]), tiles(wa[1]), tiles(wx[1])], axis=2)
    n_tiles = LRU_WIDTH // LANES
    bg = jnp.stack([ba[0], bx[0], ba[1], bx[1]], axis=0).reshape(4, n_tiles, LANES)
    bg = 0.5 * jnp.transpose(bg, (1, 0, 2)).reshape(n_tiles, 1, 4 * LANES)
    hi = bg.astype(BF16)
    lo = (bg - hi.astype(F32)).astype(BF16)
    rest = jnp.zeros((n_tiles, LANES - 2, 4 * LANES), BF16)
    return jnp.concatenate([wg.astype(BF16), hi, lo, rest], axis=1)


def _packed_tiles(ref, n_heads, head, bi=0):
    n_tiles = ref.shape[1] // (n_heads * KEY_TILE)
    return [ref[bi, pl.ds(c * KEY_TILE * n_heads + head, KEY_TILE, stride=n_heads), :].astype(BF16)
            for c in range(n_tiles)]


def _diff_kernel(*refs, packed, n_heads, lam_init):
    n_kv = len(packed)
    q_ref = refs[0]
    kv_refs = refs[1:1 + 2 * n_kv]
    g_ref, lamp_ref, subg_ref, o_ref, s_ref = refs[1 + 2 * n_kv:]
    low = _lane_low_half()
    tq = q_ref.shape[1]
    first_head = pl.program_id(1) * (q_ref.shape[2] // LANES)
    lp = lamp_ref[...]
    lam = (jnp.exp(jnp.sum(lp[0:1] * lp[1:2], axis=-1, keepdims=True))
           - jnp.exp(jnp.sum(lp[2:3] * lp[3:4], axis=-1, keepdims=True)) + lam_init)
    nb = q_ref.shape[0]
    tiles = [(bi, j) for bi in range(nb) for j in range(0, q_ref.shape[2], LANES)]
    q_blocks, kv_blocks = [], []
    for bi, j in tiles:
        k_tiles, v_tiles = [], []
        for i in range(n_kv):
            if packed[i]:
                head = first_head + j // LANES
                k_tiles += _packed_tiles(kv_refs[2 * i], n_heads, head, bi)
                v_tiles += _packed_tiles(kv_refs[2 * i + 1], n_heads, head, bi)
            else:
                n_tiles = kv_refs[2 * i].shape[1] // KEY_TILE
                k_tiles += _tiles(kv_refs[2 * i], n_tiles, j, bi)
                v_tiles += _tiles(kv_refs[2 * i + 1], n_tiles, j, bi)
        v_tiles = _values_and_ones(v_tiles)
        for r in range(0, tq, ROW_BLK):
            q_blocks += list(_split_lane_halves(q_ref[bi, r:r + ROW_BLK, j:j + LANES], low))
            kv_blocks += [(k_tiles, v_tiles)] * 2
    res = _attend(q_blocks, kv_blocks, s_ref)
    i = 0
    for bi, j in tiles:
        for r in range(0, tq, ROW_BLK):
            acc1, acc2 = res[i], res[i + 1]
            i += 2
            o = acc1[:, :LANES] / acc1[:, LANES:] - acc2[:, :LANES] * (lam / acc2[:, LANES:])
            o = o * lax.rsqrt(jnp.mean(o * o, axis=-1, keepdims=True) + EPS) * subg_ref[...] * (1.0 - lam_init)
            gate = _silu(g_ref[bi, r:r + ROW_BLK, j:j + LANES].astype(F32))
            o_ref[bi, r:r + ROW_BLK, j:j + LANES] = (o * gate).astype(o_ref.dtype)


def _diff_attn(q, kv_list, g, lam_params, sub_g, lam_init, tq, heads_per_step, batch_per_step=1):
    b, t, w = q.shape
    wb = heads_per_step * LANES
    nb = batch_per_step
    qspec = pl.BlockSpec((nb, tq, wb), lambda bi, h, qi: (bi, qi, h))
    n_heads = w // LANES
    in_specs = [qspec]
    args = [q]
    n_keys = 0
    for k, v, packed in kv_list:
        if packed:
            spec = pl.BlockSpec((nb, k.shape[1], LANES), lambda bi, h, qi: (bi, 0, 0))
            n_keys += k.shape[1] // n_heads
        else:
            spec = pl.BlockSpec((nb, k.shape[1], wb), lambda bi, h, qi: (bi, 0, h))
            n_keys += k.shape[1]
        in_specs += [spec, spec]
        args += [k, v]
    in_specs += [qspec,
                 pl.BlockSpec(lam_params.shape, lambda bi, h, qi: (0, 0)),
                 pl.BlockSpec((1, LANES), lambda bi, h, qi: (0, 0))]
    args += [g, lam_params, sub_g.reshape(1, LANES)]
    return pl.pallas_call(
        functools.partial(_diff_kernel, packed=tuple(p for _, _, p in kv_list), n_heads=n_heads,
                          lam_init=lam_init),
        grid=(b // nb, w // wb, t // tq),
        in_specs=in_specs,
        out_specs=qspec,
        out_shape=jax.ShapeDtypeStruct((b, t, w), BF16),
        scratch_shapes=[pltpu.VMEM((SCORE_SLOTS * ROW_BLK, n_keys), F32)],
        compiler_params=_params("parallel", "parallel", "parallel"),
        name="diff_attn",
    )(*args)


def _rope_tables(t_len):
    m = DIFF_HEAD_DIM // 4
    t = jnp.arange(t_len)
    rows = (t // GRID_W).astype(F32)
    cols = (t % GRID_W).astype(F32)
    inv = ROPE_BASE ** (-jnp.arange(m, dtype=F32) / m)
    ang_r = rows[:, None] * inv[None, :]
    ang_c = cols[:, None] * inv[None, :]
    cos = jnp.concatenate([jnp.cos(ang_r)] * 2 + [jnp.cos(ang_c)] * 2, axis=1)
    sin = jnp.concatenate([-jnp.sin(ang_r), jnp.sin(ang_r), -jnp.sin(ang_c), jnp.sin(ang_c)], axis=1)
    reps = LANES // DIFF_HEAD_DIM
    return jnp.tile(cos, (1, reps)), jnp.tile(sin, (1, reps))


def kernel(x_prompt, x_sample, c, cache_na_k, cache_na_v, state_lru, cache_diff_k, cache_diff_v, c_ctx,
           e_norm, e_ada_w, e_ada_b, e_w_in, e_rpb, e_conv_w, e_conv_b, e_lru_wa, e_lru_ba, e_lru_wx, e_lru_bx,
           e_lru_lam, e_w_out, o_norm, o_ada_w, o_ada_b, o_w_in, o_lq1, o_lk1, o_lq2, o_lk2, o_sub_g, o_w_out,
           final_norm):
    bp, tp, d = x_prompt.shape
    bs, ts, _ = x_sample.shape
    past = cache_na_k.shape[2]
    pad_rows = 2 * SUBLANES - bs - 1
    cc = jnp.concatenate([c, c_ctx[None, :], jnp.zeros((pad_rows, d), F32)], axis=0)
    nw, lw, dw = NA_WIDTH, LRU_WIDTH, DIFF_WIDTH
    rope_tables = _rope_tables(ts) if DEPTH > 1 else None

    def layer_params(i):
        j = i // 2
        if i % 2 == 0:
            return e_ada_w[j], e_ada_b[j], e_norm[j], e_w_in[j], e_w_out[j]
        return o_ada_w[j], o_ada_b[j], o_norm[j], o_w_in[j], o_w_out[j]

    mods = []
    for i in range(DEPTH):
        ada_w, ada_b = layer_params(i)[:2]
        m = _modulation(cc, ada_w, ada_b)
        mods.append((m[:bs].reshape(bs, 1, 3 * d), m[bs:bs + 1].reshape(1, 1, 3 * d)))

    def projection(i, latent):
        norm_g, w_in = layer_params(i)[2:4]
        mod = mods[i][0 if latent else 1]
        kv_dtype = BF16 if latent else F32
        if i % 2 == 0:
            qmul = NA_HEAD_DIM ** -0.5 * LOG2E
            outs = [(nw, BF16, False, qmul, False), (nw, kv_dtype, False, None, False),
                    (nw, kv_dtype, False, None, False), (nw, BF16, False, None, False),
                    (lw, F32, False, None, False), (lw, BF16, False, None, False)]
            return mod, norm_g, w_in, outs, None
        qmul = DIFF_HEAD_DIM ** -0.5 * LOG2E
        outs = [(dw, BF16, latent, qmul, False), (dw, kv_dtype, latent, None, not latent),
                (dw, kv_dtype, False, None, not latent), (dw, BF16, False, None, False)]
        return mod, norm_g, w_in, outs, rope_tables if latent else None

    xp, xs = x_prompt, x_sample
    _, proj_p = _stage(xp, nxt=projection(0, False), tm=512)
    _, proj_s = _stage(xs, nxt=projection(0, True), tm=512)
    na_k, na_v, lru_s, df_k, df_v = [], [], [], [], []
    for i in range(DEPTH):
        j = i // 2
        w_out = layer_params(i)[4]
        if i % 2 == 0:
            wg = _lru_gate_weights(e_lru_wa[j], e_lru_ba[j], e_lru_wx[j], e_lru_bx[j])

            q, k, v, ga, xb, gb = proj_p
            oa = _ctx_attn(q, k, v, ga)
            zeros = jnp.zeros((bp, lw), F32)
            ob, ff, fb = _lru_branch(xb, gb, zeros, zeros, e_conv_w[j], e_conv_b[j], wg, e_lru_lam[j])
            mixed_p = [oa, ob]
            na_k.append(k.reshape(bp, tp, NA_HEADS, NA_HEAD_DIM))
            na_v.append(v.reshape(bp, tp, NA_HEADS, NA_HEAD_DIM))
            lru_s.append(jnp.stack([ff, fb], axis=1))

            q, k, v, ga, xb, gb = proj_s
            ck = cache_na_k[:, j].reshape(bs, past, nw)
            cv = cache_na_v[:, j].reshape(bs, past, nw)
            oa = _na_attn(q, k, v, ck, cv, e_rpb[j], ga)
            ob, _, _ = _lru_branch(xb, gb, state_lru[:, j, 0], state_lru[:, j, 1],
                                   e_conv_w[j], e_conv_b[j], wg, e_lru_lam[j])
            mixed_s = [oa, ob]
        else:
            lam_init = 0.8 - 0.6 * math.exp(-0.3 * i)
            lam_params = jnp.stack([o_lq1[j], o_lk1[j], o_lq2[j], o_lk2[j]], axis=0)

            q, k, v, g = proj_p
            mixed_p = [_diff_attn(q, [(k, v, True)], g, lam_params, o_sub_g[j], lam_init, tq=tp,
                                  heads_per_step=DIFF_HEADS, batch_per_step=CTX_BATCH_PER_STEP)]
            df_k.append(k.reshape(bp, tp, DIFF_HEADS, 2 * DIFF_HEAD_DIM))
            df_v.append(v.reshape(bp, tp, DIFF_HEADS, 2 * DIFF_HEAD_DIM))

            q, k, v, g = proj_s
            ck = cache_diff_k[:, j].reshape(bs, past * DIFF_HEADS, 2 * DIFF_HEAD_DIM)
            cv = cache_diff_v[:, j].reshape(bs, past * DIFF_HEADS, 2 * DIFF_HEAD_DIM)
            mixed_s = [_diff_attn(q, [(ck, cv, True), (k, v, False)], g, lam_params, o_sub_g[j], lam_init, tq=ts,
                                  heads_per_step=4)]

        if i == DEPTH - 1:
            xp, _ = _stage(xp, prev=(mixed_p, w_out, mods[i][1]), final_g=final_norm, tm=512)
            xs, _ = _stage(xs, prev=(mixed_s, w_out, mods[i][0]), final_g=final_norm, tm=512)
        else:
            xp, proj_p = _stage(xp, prev=(mixed_p, w_out, mods[i][1]), nxt=projection(i + 1, False), tm=512)
            xs, proj_s = _stage(xs, prev=(mixed_s, w_out, mods[i][0]), nxt=projection(i + 1, True), tm=512)

    return (xp, xs, jnp.stack(na_k, axis=1), jnp.stack(na_v, axis=1), jnp.stack(lru_s, axis=1),
            jnp.stack(df_k, axis=1), jnp.stack(df_v, axis=1))
```

```python
import functools
import math

import jax
import jax.numpy as jnp
from jax import lax
from jax.experimental import pallas as pl
from jax.experimental.pallas import tpu as pltpu

D_MODEL = 1024
DEPTH = 2
GRID_W = 64
NA_HEADS = 8
NA_HEAD_DIM = 64
NA_WIDTH = NA_HEADS * NA_HEAD_DIM
NA_ROWS_MAX = 8
NA_COLS = 16
RPB_ROWS = 2 * NA_ROWS_MAX - 1
RPB_COLS = 2 * NA_COLS - 1
LRU_WIDTH = 512
LRU_BLOCKS = 8
LRU_BLOCK = LRU_WIDTH // LRU_BLOCKS
LRU_C = 8.0
DIFF_HEADS = 8
DIFF_HEAD_DIM = 64
DIFF_WIDTH = DIFF_HEADS * 2 * DIFF_HEAD_DIM
ROPE_BASE = 10000.0
EPS = 1e-6
LOG2E = math.log2(math.e)

LANES = 128
SUBLANES = 8
MXU_DIM = 256
VMEM_LIMIT = 56 * 1024 * 1024

CAST_COLS = 512
LRU_SEGMENTS = 4
LRU_UNROLL = 8
LRU_CHUNK = 128
CONV_PAD_L = 2
CONV_PAD_R = 1
ROW_BLK = MXU_DIM
KEY_TILE = MXU_DIM
SCORE_SLOTS = 4

NA_Q_ROWS = 4
NA_K_ROWS = 12
NA_PAIR_TILES = 16
NA_PAIRS_PER_STEP = 4
CTX_BATCH_PER_STEP = 2

F32 = jnp.float32
BF16 = jnp.bfloat16


def _params(*semantics):
    return pltpu.CompilerParams(dimension_semantics=semantics, vmem_limit_bytes=VMEM_LIMIT)


def _sigmoid(x):
    return 0.5 * jnp.tanh(0.5 * x) + 0.5


def _silu(x):
    return x * _sigmoid(x)


def _dot(a, b):
    return jnp.dot(a, b, preferred_element_type=F32)


def _dot_nt(a, b):
    return lax.dot_general(a, b, (((1,), (1,)), ((), ())), preferred_element_type=F32)


def _mod_kernel(c_ref, w_ref, b_ref, o_ref):
    s = _silu(c_ref[...])
    o_ref[...] = _dot(s.astype(BF16), w_ref[...].astype(BF16)) + b_ref[...]


def _modulation(cc, w, b):
    rows, d = cc.shape
    n = w.shape[1]
    tn = 1024
    return pl.pallas_call(
        _mod_kernel,
        grid=(n // tn,),
        in_specs=[pl.BlockSpec((rows, d), lambda j: (0, 0)),
                  pl.BlockSpec((d, tn), lambda j: (0, j)),
                  pl.BlockSpec((1, tn), lambda j: (0, j))],
        out_specs=pl.BlockSpec((rows, tn), lambda j: (0, j)),
        out_shape=jax.ShapeDtypeStruct((rows, n), F32),
        compiler_params=_params("arbitrary"),
        name="modulation",
    )(cc, w, b.reshape(1, n))


def _rope_block(x, cos, sin_signed, low16):
    partner = jnp.where(low16, pltpu.roll(x, LANES - 16, axis=1), pltpu.roll(x, 16, axis=1))
    return x * cos + partner * sin_signed


def _cast_weights_once(w_ref, wbf_ref):
    @pl.when((pl.program_id(0) == 0) & (pl.program_id(1) == 0))
    def _():
        for off in range(0, w_ref.shape[1], CAST_COLS):
            wbf_ref[:, off:off + CAST_COLS] = w_ref[:, off:off + CAST_COLS].astype(BF16)


def _norm_project(x, mod_ref, g_ref, wbf_ref, rope_refs, out_refs, outs):
    tm, d = x.shape
    y = x * lax.rsqrt(jnp.mean(x * x, axis=-1, keepdims=True) + EPS) * g_ref[...]
    mod = mod_ref[0]
    h = (y * (1.0 + mod[:, d:2 * d]) + mod[:, :d]).astype(BF16)
    if rope_refs:
        cos = rope_refs[0][...]
        sin = rope_refs[1][...]
        lane = lax.broadcasted_iota(jnp.int32, (1, LANES), 1)
        low16 = (lane & 16) == 0
    off = 0
    for o_ref, (width, roped, mul, packed) in zip(out_refs, outs):
        acc = _dot(h, wbf_ref[:, off:off + width])
        if roped:
            acc = jnp.concatenate(
                [_rope_block(acc[:, j:j + LANES], cos, sin, low16) for j in range(0, width, LANES)], axis=1)
        if mul is not None:
            acc = acc * mul
        if packed:
            n_heads = width // LANES
            t = o_ref.shape[1] // n_heads
            for bi in range(o_ref.shape[0]):
                for hh in range(n_heads):
                    o_ref[bi, pl.ds(hh, t, stride=n_heads), :] = (
                        acc[bi * t:(bi + 1) * t, hh * LANES:(hh + 1) * LANES].astype(o_ref.dtype))
        else:
            o_ref[...] = acc.astype(o_ref.dtype).reshape(o_ref.shape)
        off += width


def _token_rows(ref):
    return ref[...].reshape(-1, ref.shape[-1])


def _gated_residual(o_refs, wbf_ref, x_ref, mod_ref):
    d = x_ref.shape[-1]
    acc, row = None, 0
    for o_ref in o_refs:
        k = o_ref.shape[-1]
        part = _dot(_token_rows(o_ref), wbf_ref[row:row + k, :])
        acc = part if acc is None else acc + part
        row += k
    return _token_rows(x_ref) + mod_ref[0][:, 2 * d:] * acc


def _stage_kernel(*refs, n_o, final, outs, rope):
    refs = list(refs)
    take = lambda n: [refs.pop(0) for _ in range(n)]
    if n_o:
        o_refs = take(n_o)
        wout_ref, x_ref, mod_prev_ref = take(3)
        fin_ref = take(1)[0] if final else None
    else:
        x_ref, = take(1)
    if outs:
        mod_ref, g_ref, win_ref = take(3)
        rope_refs = take(2) if rope else []
    xout_ref = take(1)[0] if n_o else None
    out_refs = take(len(outs))
    if n_o:
        wout_bf = take(1)[0]
        _cast_weights_once(wout_ref, wout_bf)
        x = _gated_residual(o_refs, wout_bf, x_ref, mod_prev_ref)
        if final:
            x_new = x * lax.rsqrt(jnp.mean(x * x, axis=-1, keepdims=True) + EPS) * fin_ref[...]
        else:
            x_new = x
        xout_ref[...] = x_new.reshape(xout_ref.shape)
    else:
        x = _token_rows(x_ref)
    if outs:
        win_bf = take(1)[0]
        _cast_weights_once(win_ref, win_bf)
        _norm_project(x, mod_ref, g_ref, win_bf, rope_refs, out_refs, outs)


def _stage(x, prev=None, nxt=None, final_g=None, tm=256):
    b, t, d = x.shape
    shared_mod = all(m.shape[0] == 1 for m in ([prev[2]] if prev else []) + ([nxt[0]] if nxt else []))
    bb = max(1, tm // t) if shared_mod else 1
    tm = min(tm, t)
    tile = lambda width: pl.BlockSpec((bb, tm, width), lambda i, j: (i, j, 0))
    mod_spec = lambda mod: pl.BlockSpec(
        (1, 1, 3 * d), (lambda i, j: (i, 0, 0)) if mod.shape[0] > 1 else (lambda i, j: (0, 0, 0)))
    resident = lambda w: pl.BlockSpec(w.shape, lambda i, j: (0, 0), pipeline_mode=pl.Buffered(1))
    row_spec = pl.BlockSpec((1, d), lambda i, j: (0, 0))
    in_specs, args, out_specs, out_shape, scratch = [], [], [], [], []
    n_o, outs, rope = 0, (), False
    if prev is not None:
        o_list, w_out, mod_prev = prev
        n_o = len(o_list)
        in_specs += [tile(o.shape[-1]) for o in o_list] + [resident(w_out), tile(d), mod_spec(mod_prev)]
        args += list(o_list) + [w_out, x, mod_prev]
        if final_g is not None:
            in_specs.append(row_spec)
            args.append(final_g.reshape(1, d))
        out_specs.append(tile(d))
        out_shape.append(jax.ShapeDtypeStruct((b, t, d), F32))
        scratch.append(pltpu.VMEM(w_out.shape, BF16))
    else:
        in_specs.append(tile(d))
        args.append(x)
    if nxt is not None:
        mod_next, norm_g, w_in, out_list, rope_tables = nxt
        in_specs += [mod_spec(mod_next), row_spec, resident(w_in)]
        args += [mod_next, norm_g.reshape(1, d), w_in]
        rope = rope_tables is not None
        if rope:
            in_specs += [pl.BlockSpec((tm, LANES), lambda i, j: (j, 0))] * 2
            args += list(rope_tables)
        for width, dt, _, _, packed in out_list:
            if packed:
                heads = width // LANES
                out_specs.append(pl.BlockSpec((bb, tm * heads, LANES), lambda i, j: (i, j, 0)))
                out_shape.append(jax.ShapeDtypeStruct((b, t * heads, LANES), dt))
            else:
                out_specs.append(tile(width))
                out_shape.append(jax.ShapeDtypeStruct((b, t, width), dt))
        outs = tuple((w_, r, m, p) for w_, _, r, m, p in out_list)
        scratch.append(pltpu.VMEM(w_in.shape, BF16))
    res = pl.pallas_call(
        functools.partial(_stage_kernel, n_o=n_o, final=final_g is not None, outs=outs, rope=rope),
        grid=(b // bb, t // tm),
        in_specs=in_specs,
        out_specs=out_specs,
        out_shape=out_shape,
        scratch_shapes=scratch,
        compiler_params=_params("arbitrary", "arbitrary"),
        name="stage",
    )(*args)
    return (res[0], list(res[1:])) if n_o else (None, list(res))


def _attend(q_blocks, kv_blocks, s_ref, score_fn=None):
    n_rb = len(q_blocks)

    def tile_of(rb, c):
        slot = rb % SCORE_SLOTS
        return (slice(slot * ROW_BLK, (slot + 1) * ROW_BLK), slice(c * KEY_TILE, (c + 1) * KEY_TILE))

    def score_tile(rb, c):
        s_ref[tile_of(rb, c)] = _dot_nt(q_blocks[rb], kv_blocks[rb][0][c])

    def max_tile(rb, c, m_acc):
        s = s_ref[tile_of(rb, c)]
        adjusted = None if score_fn is None else score_fn(rb, c, s)
        if adjusted is not None:
            s = adjusted
            s_ref[tile_of(rb, c)] = s
        mt = jnp.maximum(s[:, :LANES], s[:, LANES:])
        return mt if m_acc is None else jnp.maximum(m_acc, mt)

    def n_tiles(rb):
        return len(kv_blocks[rb][0]) if 0 <= rb < n_rb else 0

    outs = []
    row_max = {}
    for step in range(n_rb + 2):
        rb_score, rb_max, rb_exp = step, step - 1, step - 2
        m_acc, acc = None, None
        for c in range(max(n_tiles(rb_score), n_tiles(rb_max), n_tiles(rb_exp))):
            if c < n_tiles(rb_score):
                score_tile(rb_score, c)
            if c < n_tiles(rb_max):
                m_acc = max_tile(rb_max, c, m_acc)
            if c < n_tiles(rb_exp):
                e = jnp.exp2(s_ref[tile_of(rb_exp, c)] - row_max[rb_exp])
                pv = _dot(e.astype(BF16), kv_blocks[rb_exp][1][c])
                acc = pv if acc is None else acc + pv
        if m_acc is not None:
            row_max[rb_max] = m_acc.max(axis=-1, keepdims=True)
        if acc is not None:
            outs.append(acc)
    return outs


def _lane_low_half():
    return lax.broadcasted_iota(jnp.int32, (1, LANES), 1) < (LANES // 2)


def _values_and_ones(v_tiles):
    ones = jnp.ones((KEY_TILE, LANES), BF16)
    return [jnp.concatenate([v, ones], axis=1) for v in v_tiles]


def _normalise_pair(acc_lo, acc_hi, low):
    return jnp.where(low, acc_lo[:, :LANES] / acc_lo[:, LANES:], acc_hi[:, :LANES] / acc_hi[:, LANES:])


def _split_lane_halves(q, low):
    zero = jnp.zeros_like(q)
    return jnp.where(low, q, zero), jnp.where(low, zero, q)


def _tiles(ref, n_tiles, j, bi=0):
    return [ref[bi, c * KEY_TILE:(c + 1) * KEY_TILE, j:j + LANES].astype(BF16) for c in range(n_tiles)]


def _ctx_attn_kernel(q_ref, k_ref, v_ref, g_ref, o_ref, s_ref):
    low = _lane_low_half()
    nb, tq, w = q_ref.shape
    n_tiles = k_ref.shape[1] // KEY_TILE
    tiles = [(bi, j, r) for bi in range(nb) for j in range(0, w, LANES) for r in range(0, tq, ROW_BLK)]
    q_blocks, kv_blocks, kv = [], [], {}
    for bi, j, r in tiles:
        if (bi, j) not in kv:
            kv[bi, j] = (_tiles(k_ref, n_tiles, j, bi), _values_and_ones(_tiles(v_ref, n_tiles, j, bi)))
        q_blocks += list(_split_lane_halves(q_ref[bi, r:r + ROW_BLK, j:j + LANES], low))
        kv_blocks += [kv[bi, j]] * 2
    res = _attend(q_blocks, kv_blocks, s_ref)
    for i, (bi, j, r) in enumerate(tiles):
        o = _normalise_pair(res[2 * i], res[2 * i + 1], low)
        gate = _silu(g_ref[bi, r:r + ROW_BLK, j:j + LANES].astype(F32))
        o_ref[bi, r:r + ROW_BLK, j:j + LANES] = (o * gate).astype(o_ref.dtype)


def _ctx_attn(q, k, v, g):
    b, t, w = q.shape
    nb = CTX_BATCH_PER_STEP
    spec = pl.BlockSpec((nb, t, w), lambda i: (i, 0, 0))
    return pl.pallas_call(
        _ctx_attn_kernel,
        grid=(b // nb,),
        in_specs=[spec] * 4,
        out_specs=spec,
        out_shape=jax.ShapeDtypeStruct((b, t, w), BF16),
        scratch_shapes=[pltpu.VMEM((SCORE_SLOTS * ROW_BLK, k.shape[1]), F32)],
        compiler_params=_params("parallel"),
        name="ctx_attn",
    )(q, k, v, g)


def _na_pair_tiles(rpb_ref, head, pt_ref):
    right = lax.broadcasted_iota(jnp.int32, (GRID_W, LANES), 1) >= GRID_W
    zero_row = jnp.zeros((1, LANES), F32)

    def toeplitz(row, lane0):
        shift = (lane0 - (NA_COLS - 1)) % LANES
        return pltpu.roll(jnp.broadcast_to(row, (GRID_W, LANES)), shift, axis=1, stride=1, stride_axis=0)

    for p in range(NA_PAIR_TILES):
        left_row = rpb_ref[head, p - 1:p, :] if p >= 1 else zero_row
        right_row = rpb_ref[head, p:p + 1, :]
        pt_ref[head, p] = jnp.where(right, toeplitz(right_row, GRID_W), toeplitz(left_row, 0)) * LOG2E


def _na_row_start(r, n_rows):
    kr = min(NA_ROWS_MAX, n_rows)
    return min(max(r - kr // 2, 0), n_rows - kr)


def _na_window(grp, n_rows):
    kr = min(NA_ROWS_MAX, n_rows)
    starts = [_na_row_start(grp * NA_Q_ROWS + i, n_rows) for i in range(NA_Q_ROWS)]
    rows_per_tile = KEY_TILE // GRID_W
    need = -(-(max(starts) + kr - min(starts)) // rows_per_tile) * rows_per_tile
    return min(min(starts), n_rows - need), need


def _na_kernel(q_ref, k_ref, v_ref, ck_ref, cv_ref, rpb_ref, g_ref, o_ref, pt_ref, badd_ref, s_ref, *, n_rows):
    kr = min(NA_ROWS_MAX, n_rows)
    n_grp = n_rows // NA_Q_ROWS
    n_pairs = q_ref.shape[2] // LANES
    low = _lane_low_half()

    @pl.when(pl.program_id(1) == 0)
    def _build_bias():
        for head in range(2 * n_pairs):
            _na_pair_tiles(rpb_ref, head, pt_ref)
        qcol = lax.broadcasted_iota(jnp.int32, (GRID_W, LANES), 0)
        lane = lax.broadcasted_iota(jnp.int32, (GRID_W, LANES), 1)
        right = lane >= GRID_W
        kcol = lane & (GRID_W - 1)
        col_start = jnp.clip(qcol - NA_COLS // 2, 0, GRID_W - NA_COLS)
        col_ok = (kcol >= col_start) & (kcol < col_start + NA_COLS)
        for grp in range(n_grp):
            krow0, key_rows = _na_window(grp, n_rows)
            for i in range(NA_Q_ROWS):
                r = grp * NA_Q_ROWS + i
                row_start = _na_row_start(r, n_rows)
                for m in range(key_rows // 2):
                    key_row = krow0 + 2 * m
                    ok_l = row_start <= key_row < row_start + kr
                    ok_r = row_start <= key_row + 1 < row_start + kr
                    rows = slice(i * GRID_W, (i + 1) * GRID_W)
                    cols = slice(m * LANES, (m + 1) * LANES)
                    if not (ok_l or ok_r):
                        for head in range(2 * n_pairs):
                            badd_ref[head * n_grp + grp, rows, cols] = jnp.full((GRID_W, LANES), -jnp.inf, F32)
                        continue
                    ok = col_ok
                    if not ok_r:
                        ok = ok & jnp.logical_not(right)
                    if not ok_l:
                        ok = ok & right
                    p = key_row - r + NA_ROWS_MAX
                    for head in range(2 * n_pairs):
                        badd_ref[head * n_grp + grp, rows, cols] = jnp.where(ok, pt_ref[head, p], -jnp.inf)

    n_ctx = ck_ref.shape[1] // KEY_TILE
    q_blocks, kv_blocks, blocks = [], [], []
    for pair in range(n_pairs):
        lanes = slice(pair * LANES, (pair + 1) * LANES)
        ctx_k = _tiles(ck_ref, n_ctx, pair * LANES)
        ctx_v = _tiles(cv_ref, n_ctx, pair * LANES)
        for grp in range(n_grp):
            krow0, key_rows = _na_window(grp, n_rows)
            k0 = krow0 * GRID_W
            win = [slice(k0 + c * KEY_TILE, k0 + (c + 1) * KEY_TILE) for c in range(key_rows * GRID_W // KEY_TILE)]
            kv = ([k_ref[0, w, lanes] for w in win] + ctx_k,
                  _values_and_ones([v_ref[0, w, lanes] for w in win] + ctx_v))
            q_blocks += list(_split_lane_halves(q_ref[0, grp * ROW_BLK:(grp + 1) * ROW_BLK, lanes], low))
            kv_blocks += [kv, kv]
            blocks += [(2 * pair, grp, len(win)), (2 * pair + 1, grp, len(win))]

    def masked_bias(rb, c, s):
        head, grp, n_win = blocks[rb]
        if c >= n_win:
            return None
        cols = slice(c * KEY_TILE, (c + 1) * KEY_TILE)
        return s + badd_ref[head * n_grp + grp, :, cols]

    res = _attend(q_blocks, kv_blocks, s_ref, masked_bias)
    for pair in range(n_pairs):
        lanes = slice(pair * LANES, (pair + 1) * LANES)
        for grp in range(n_grp):
            rb = 2 * (pair * n_grp + grp)
            o = _normalise_pair(res[rb], res[rb + 1], low)
            rows = slice(grp * ROW_BLK, (grp + 1) * ROW_BLK)
            o_ref[0, rows, lanes] = (o * _silu(g_ref[0, rows, lanes].astype(F32))).astype(o_ref.dtype)


def _na_attn(q, k, v, ck, cv, rpb, g):
    heads, rpb_rows, rpb_cols = rpb.shape
    rpb = jnp.pad(rpb, ((0, 0), (0, NA_PAIR_TILES - rpb_rows), (0, LANES - rpb_cols)))
    b, t, w = q.shape
    c = ck.shape[1]
    n_rows = t // GRID_W
    assert NA_Q_ROWS * GRID_W == ROW_BLK and n_rows % NA_Q_ROWS == 0 and n_rows >= NA_K_ROWS
    n_grp = n_rows // NA_Q_ROWS
    n_win = NA_K_ROWS * GRID_W
    wb = NA_PAIRS_PER_STEP * LANES
    hb = 2 * NA_PAIRS_PER_STEP
    tspec = pl.BlockSpec((1, t, wb), lambda hp, bi: (bi, 0, hp))
    cspec = pl.BlockSpec((1, c, wb), lambda hp, bi: (bi, 0, hp))
    ptspec = pl.BlockSpec((hb, NA_PAIR_TILES, LANES), lambda hp, bi: (hp, 0, 0))
    return pl.pallas_call(
        functools.partial(_na_kernel, n_rows=n_rows),
        grid=(w // wb, b),
        in_specs=[tspec, tspec, tspec, cspec, cspec, ptspec, tspec],
        out_specs=tspec,
        out_shape=jax.ShapeDtypeStruct((b, t, w), BF16),
        scratch_shapes=[pltpu.VMEM((hb, NA_PAIR_TILES, GRID_W, LANES), F32),
                        pltpu.VMEM((hb * n_grp, ROW_BLK, n_win), F32),
                        pltpu.VMEM((SCORE_SLOTS * ROW_BLK, n_win + c), F32)],
        compiler_params=_params("parallel", "arbitrary"),
        name="na_attn",
    )(q, k, v, ck, cv, rpb, g)


def _softplus(x):
    return jnp.maximum(x, 0.0) + jnp.log(1.0 + jnp.exp(-jnp.abs(x)))


def _lru_kernel(x_ref, gb_ref, h0f_ref, h0b_ref, cw_ref, cb_ref, wg_ref, lam_ref,
                o_ref, ff_ref, fb_ref, xt_s, af_s, uf_s, ab_s, ub_s, hf_s, hb_s):
    nb, t, _ = x_ref.shape
    pad = CONV_PAD_L * nb
    cw = cw_ref[...]
    cb = cb_ref[...]
    wg = wg_ref[0]
    kk = (-0.5 * LRU_C * LOG2E) * _softplus(-lam_ref[...])

    xt_s[0:pad, :] = jnp.zeros((pad, LANES), F32)
    xt_s[pad + t * nb:pad + (t + CONV_PAD_R) * nb, :] = jnp.zeros((CONV_PAD_R * nb, LANES), F32)

    def to_time_major(b, carry):
        xt_s[pl.ds(pad + b, t, stride=nb), :] = x_ref[b]
        return carry

    lax.fori_loop(0, nb, to_time_major, 0)

    rows_per_chunk = LRU_CHUNK * nb
    lane = lax.broadcasted_iota(jnp.int32, (rows_per_chunk, LANES), 1)
    bias_cols = jnp.where(lane < 2, 1.0, 0.0).astype(BF16)

    def gates(c, carry):
        r0 = pl.multiple_of(c * rows_per_chunk, rows_per_chunk)

        def tap(i):
            return xt_s[pl.ds(r0 + i * nb, rows_per_chunk), :]

        x = tap(2)
        xc = cw[0:1] * tap(0) + cw[1:2] * tap(1) + cw[2:3] * x + cw[3:4] * tap(3) + cb
        th = jnp.tanh(_dot(jnp.concatenate([xc.astype(BF16), bias_cols], axis=1), wg))
        xh = 0.5 * xc
        rows = pl.ds(r0, rows_per_chunk)
        for d, (a_s, u_s) in enumerate(((af_s, uf_s), (ab_s, ub_s))):
            t_a = th[:, (2 * d) * LANES:(2 * d + 1) * LANES]
            t_x = th[:, (2 * d + 1) * LANES:(2 * d + 2) * LANES]
            a = jnp.exp2(kk[d:d + 1] * t_a + kk[d:d + 1])
            y = 1.0 - a * a
            sq = y * lax.rsqrt(jnp.maximum(y, 1e-30))
            a_s[rows, :] = a
            u_s[rows, :] = sq * (xh * t_x + xh)
        return carry

    lax.fori_loop(0, t // LRU_CHUNK, gates, 0)

    ts = t // LRU_SEGMENTS
    refs = ((af_s, uf_s), (ab_s, ub_s))

    slab = nb * LRU_UNROLL

    def slab_rows(d, j, i):
        base = (j * ts + i * LRU_UNROLL) * nb
        return pl.ds(pl.multiple_of(base if d == 0 else t * nb - slab - base, slab), slab)

    def step_of(d, block, k):
        kk_ = k if d == 0 else LRU_UNROLL - 1 - k
        return block[kk_ * nb:(kk_ + 1) * nb]

    def compose(i, carry):
        state = list(carry)
        for d in range(2):
            for j in range(LRU_SEGMENTS - 1):
                n = d * (LRU_SEGMENTS - 1) + j
                a_blk = refs[d][0][slab_rows(d, j, i), :]
                u_blk = refs[d][1][slab_rows(d, j, i), :]
                g, big_a = state[n]
                for k in range(LRU_UNROLL):
                    a = step_of(d, a_blk, k)
                    g, big_a = a * g + step_of(d, u_blk, k), a * big_a
                state[n] = (g, big_a)
        return tuple(state)

    ones = jnp.ones((nb, LANES), F32)
    h0 = (h0f_ref[...], h0b_ref[...])
    init = tuple((h0[d] if j == 0 else jnp.zeros((nb, LANES), F32), ones)
                 for d in range(2) for j in range(LRU_SEGMENTS - 1))
    ends = lax.fori_loop(0, ts // LRU_UNROLL, compose, init)
    starts = []
    for d in range(2):
        h = h0[d]
        for j in range(LRU_SEGMENTS):
            starts.append(h)
            if j < LRU_SEGMENTS - 1:
                g, big_a = ends[d * (LRU_SEGMENTS - 1) + j]
                h = g if j == 0 else g + big_a * h

    def rerun(i, carry):
        state = list(carry)
        for d, h_s in enumerate((hf_s, hb_s)):
            for j in range(LRU_SEGMENTS):
                n = d * LRU_SEGMENTS + j
                a_blk = refs[d][0][slab_rows(d, j, i), :]
                u_blk = refs[d][1][slab_rows(d, j, i), :]
                hs = []
                for k in range(LRU_UNROLL):
                    state[n] = step_of(d, a_blk, k) * state[n] + step_of(d, u_blk, k)
                    hs.append(state[n])
                h_s[slab_rows(d, j, i), :] = jnp.concatenate(hs if d == 0 else hs[::-1], axis=0)
        return tuple(state)

    final = lax.fori_loop(0, ts // LRU_UNROLL, rerun, tuple(starts))
    ff_ref[...] = final[LRU_SEGMENTS - 1]
    fb_ref[...] = final[2 * LRU_SEGMENTS - 1]

    def add_directions(c, carry):
        rows = pl.ds(pl.multiple_of(c * rows_per_chunk, rows_per_chunk), rows_per_chunk)
        hf_s[rows, :] = hf_s[rows, :] + hb_s[rows, :]
        return carry

    lax.fori_loop(0, t // LRU_CHUNK, add_directions, 0)

    def emit(b, carry):
        hs = hf_s[pl.ds(b, t, stride=nb), :]
        o_ref[b] = (hs * _silu(gb_ref[b].astype(F32))).astype(o_ref.dtype)
        return carry

    lax.fori_loop(0, nb, emit, 0)


def _lru_branch(xb, gb, h0f, h0b, conv_w, conv_b, wg, lam):
    b, t, w = xb.shape
    nb = SUBLANES
    assert t % (LRU_SEGMENTS * LRU_UNROLL) == 0 and t % LRU_CHUNK == 0
    assert conv_w.shape[0] == CONV_PAD_L + 1 + CONV_PAD_R
    xspec = pl.BlockSpec((nb, t, LANES), lambda i, c: (i, 0, c))
    hspec = pl.BlockSpec((nb, LANES), lambda i, c: (i, c))
    scratch = [pltpu.VMEM(((t + CONV_PAD_L + CONV_PAD_R) * nb, LANES), F32)]
    scratch += [pltpu.VMEM((t * nb, LANES), F32) for _ in range(6)]
    return pl.pallas_call(
        _lru_kernel,
        grid=(b // nb, w // LANES),
        in_specs=[xspec, xspec, hspec, hspec,
                  pl.BlockSpec((conv_w.shape[0], LANES), lambda i, c: (0, c)),
                  pl.BlockSpec((1, LANES), lambda i, c: (0, c)),
                  pl.BlockSpec((1, 2 * LANES, 4 * LANES), lambda i, c: (c, 0, 0)),
                  pl.BlockSpec((2, LANES), lambda i, c: (0, c))],
        out_specs=[xspec, hspec, hspec],
        out_shape=[jax.ShapeDtypeStruct((b, t, w), BF16),
                   jax.ShapeDtypeStruct((b, w), F32),
                   jax.ShapeDtypeStruct((b, w), F32)],
        scratch_shapes=scratch,
        compiler_params=_params("parallel", "parallel"),
        name="lru_branch",
    )(xb, gb, h0f, h0b, conv_w, conv_b.reshape(1, w), wg, lam)


def _lru_gate_weights(wa, ba, wx, bx):
    def tiles(w):
        w2 = w.reshape(LRU_BLOCKS // 2, 2, LRU_BLOCK, LRU_BLOCK)
        z = jnp.zeros_like(w2[:, 0])
        return jnp.concatenate([jnp.concatenate([w2[:, 0], z], axis=2),
                                jnp.concatenate([z, w2[:, 1]], axis=2)], axis=1)
    wg = 0.5 * jnp.concatenate([tiles(wa[0]), tiles(wx[0]), tiles(wa[1]), tiles(wx[1])], axis=2)
    n_tiles = LRU_WIDTH // LANES
    bg = jnp.stack([ba[0], bx[0], ba[1], bx[1]], axis=0).reshape(4, n_tiles, LANES)
    bg = 0.5 * jnp.transpose(bg, (1, 0, 2)).reshape(n_tiles, 1, 4 * LANES)
    hi = bg.astype(BF16)
    lo = (bg - hi.astype(F32)).astype(BF16)
    rest = jnp.zeros((n_tiles, LANES - 2, 4 * LANES), BF16)
    return jnp.concatenate([wg.astype(BF16), hi, lo, rest], axis=1)


def _packed_tiles(ref, n_heads, head, bi=0):
    n_tiles = ref.shape[1] // (n_heads * KEY_TILE)
    return [ref[bi, pl.ds(c * KEY_TILE * n_heads + head, KEY_TILE, stride=n_heads), :].astype(BF16)
            for c in range(n_tiles)]


def _diff_kernel(*refs, packed, n_heads, lam_init):
    n_kv = len(packed)
    q_ref = refs[0]
    kv_refs = refs[1:1 + 2 * n_kv]
    g_ref, lamp_ref, subg_ref, o_ref, s_ref = refs[1 + 2 * n_kv:]
    low = _lane_low_half()
    tq = q_ref.shape[1]
    first_head = pl.program_id(1) * (q_ref.shape[2] // LANES)
    lp = lamp_ref[...]
    lam = (jnp.exp(jnp.sum(lp[0:1] * lp[1:2], axis=-1, keepdims=True))
           - jnp.exp(jnp.sum(lp[2:3] * lp[3:4], axis=-1, keepdims=True)) + lam_init)
    nb = q_ref.shape[0]
    tiles = [(bi, j) for bi in range(nb) for j in range(0, q_ref.shape[2], LANES)]
    q_blocks, kv_blocks = [], []
    for bi, j in tiles:
        k_tiles, v_tiles = [], []
        for i in range(n_kv):
            if packed[i]:
                head = first_head + j // LANES
                k_tiles += _packed_tiles(kv_refs[2 * i], n_heads, head, bi)
                v_tiles += _packed_tiles(kv_refs[2 * i + 1], n_heads, head, bi)
            else:
                n_tiles = kv_refs[2 * i].shape[1] // KEY_TILE
                k_tiles += _tiles(kv_refs[2 * i], n_tiles, j, bi)
                v_tiles += _tiles(kv_refs[2 * i + 1], n_tiles, j, bi)
        v_tiles = _values_and_ones(v_tiles)
        for r in range(0, tq, ROW_BLK):
            q_blocks += list(_split_lane_halves(q_ref[bi, r:r + ROW_BLK, j:j + LANES], low))
            kv_blocks += [(k_tiles, v_tiles)] * 2
    res = _attend(q_blocks, kv_blocks, s_ref)
    i = 0
    for bi, j in tiles:
        for r in range(0, tq, ROW_BLK):
            acc1, acc2 = res[i], res[i + 1]
            i += 2
            o = acc1[:, :LANES] / acc1[:, LANES:] - acc2[:, :LANES] * (lam / acc2[:, LANES:])
            o = o * lax.rsqrt(jnp.mean(o * o, axis=-1, keepdims=True) + EPS) * subg_ref[...] * (1.0 - lam_init)
            gate = _silu(g_ref[bi, r:r + ROW_BLK, j:j + LANES].astype(F32))
            o_ref[bi, r:r + ROW_BLK, j:j + LANES] = (o * gate).astype(o_ref.dtype)


def _diff_attn(q, kv_list, g, lam_params, sub_g, lam_init, tq, heads_per_step, batch_per_step=1):
    b, t, w = q.shape
    wb = heads_per_step * LANES
    nb = batch_per_step
    qspec = pl.BlockSpec((nb, tq, wb), lambda bi, h, qi: (bi, qi, h))
    n_heads = w // LANES
    in_specs = [qspec]
    args = [q]
    n_keys = 0
    for k, v, packed in kv_list:
        if packed:
            spec = pl.BlockSpec((nb, k.shape[1], LANES), lambda bi, h, qi: (bi, 0, 0))
            n_keys += k.shape[1] // n_heads
        else:
            spec = pl.BlockSpec((nb, k.shape[1], wb), lambda bi, h, qi: (bi, 0, h))
            n_keys += k.shape[1]
        in_specs += [spec, spec]
        args += [k, v]
    in_specs += [qspec,
                 pl.BlockSpec(lam_params.shape, lambda bi, h, qi: (0, 0)),
                 pl.BlockSpec((1, LANES), lambda bi, h, qi: (0, 0))]
    args += [g, lam_params, sub_g.reshape(1, LANES)]
    return pl.pallas_call(
        functools.partial(_diff_kernel, packed=tuple(p for _, _, p in kv_list), n_heads=n_heads,
                          lam_init=lam_init),
        grid=(b // nb, w // wb, t // tq),
        in_specs=in_specs,
        out_specs=qspec,
        out_shape=jax.ShapeDtypeStruct((b, t, w), BF16),
        scratch_shapes=[pltpu.VMEM((SCORE_SLOTS * ROW_BLK, n_keys), F32)],
        compiler_params=_params("parallel", "parallel", "parallel"),
        name="diff_attn",
    )(*args)


def _rope_tables(t_len):
    m = DIFF_HEAD_DIM // 4
    t = jnp.arange(t_len)
    rows = (t // GRID_W).astype(F32)
    cols = (t % GRID_W).astype(F32)
    inv = ROPE_BASE ** (-jnp.arange(m, dtype=F32) / m)
    ang_r = rows[:, None] * inv[None, :]
    ang_c = cols[:, None] * inv[None, :]
    cos = jnp.concatenate([jnp.cos(ang_r)] * 2 + [jnp.cos(ang_c)] * 2, axis=1)
    sin = jnp.concatenate([-jnp.sin(ang_r), jnp.sin(ang_r), -jnp.sin(ang_c), jnp.sin(ang_c)], axis=1)
    reps = LANES // DIFF_HEAD_DIM
    return jnp.tile(cos, (1, reps)), jnp.tile(sin, (1, reps))


def kernel(x_prompt, x_sample, c, cache_na_k, cache_na_v, state_lru, cache_diff_k, cache_diff_v, c_ctx,
           e_norm, e_ada_w, e_ada_b, e_w_in, e_rpb, e_conv_w, e_conv_b, e_lru_wa, e_lru_ba, e_lru_wx, e_lru_bx,
           e_lru_lam, e_w_out, o_norm, o_ada_w, o_ada_b, o_w_in, o_lq1, o_lk1, o_lq2, o_lk2, o_sub_g, o_w_out,
           final_norm):
    bp, tp, d = x_prompt.shape
    bs, ts, _ = x_sample.shape
    past = cache_na_k.shape[2]
    pad_rows = 2 * SUBLANES - bs - 1
    cc = jnp.concatenate([c, c_ctx[None, :], jnp.zeros((pad_rows, d), F32)], axis=0)
    nw, lw, dw = NA_WIDTH, LRU_WIDTH, DIFF_WIDTH
    rope_tables = _rope_tables(ts) if DEPTH > 1 else None

    def layer_params(i):
        j = i // 2
        if i % 2 == 0:
            return e_ada_w[j], e_ada_b[j], e_norm[j], e_w_in[j], e_w_out[j]
        return o_ada_w[j], o_ada_b[j], o_norm[j], o_w_in[j], o_w_out[j]

    mods = []
    for i in range(DEPTH):
        ada_w, ada_b = layer_params(i)[:2]
        m = _modulation(cc, ada_w, ada_b)
        mods.append((m[:bs].reshape(bs, 1, 3 * d), m[bs:bs + 1].reshape(1, 1, 3 * d)))

    def projection(i, latent):
        norm_g, w_in = layer_params(i)[2:4]
        mod = mods[i][0 if latent else 1]
        kv_dtype = BF16 if latent else F32
        if i % 2 == 0:
            qmul = NA_HEAD_DIM ** -0.5 * LOG2E
            outs = [(nw, BF16, False, qmul, False), (nw, kv_dtype, False, None, False),
                    (nw, kv_dtype, False, None, False), (nw, BF16, False, None, False),
                    (lw, F32, False, None, False), (lw, BF16, False, None, False)]
            return mod, norm_g, w_in, outs, None
        qmul = DIFF_HEAD_DIM ** -0.5 * LOG2E
        outs = [(dw, BF16, latent, qmul, False), (dw, kv_dtype, latent, None, not latent),
                (dw, kv_dtype, False, None, not latent), (dw, BF16, False, None, False)]
        return mod, norm_g, w_in, outs, rope_tables if latent else None

    xp, xs = x_prompt, x_sample
    _, proj_p = _stage(xp, nxt=projection(0, False), tm=512)
    _, proj_s = _stage(xs, nxt=projection(0, True), tm=512)
    na_k, na_v, lru_s, df_k, df_v = [], [], [], [], []
    for i in range(DEPTH):
        j = i // 2
        w_out = layer_params(i)[4]
        if i % 2 == 0:
            wg = _lru_gate_weights(e_lru_wa[j], e_lru_ba[j], e_lru_wx[j], e_lru_bx[j])

            q, k, v, ga, xb, gb = proj_p
            oa = _ctx_attn(q, k, v, ga)
            zeros = jnp.zeros((bp, lw), F32)
            ob, ff, fb = _lru_branch(xb, gb, zeros, zeros, e_conv_w[j], e_conv_b[j], wg, e_lru_lam[j])
            mixed_p = [oa, ob]
            na_k.append(k.reshape(bp, tp, NA_HEADS, NA_HEAD_DIM))
            na_v.append(v.reshape(bp, tp, NA_HEADS, NA_HEAD_DIM))
            lru_s.append(jnp.stack([ff, fb], axis=1))

            q, k, v, ga, xb, gb = proj_s
            ck = cache_na_k[:, j].reshape(bs, past, nw)
            cv = cache_na_v[:, j].reshape(bs, past, nw)
            oa = _na_attn(q, k, v, ck, cv, e_rpb[j], ga)
            ob, _, _ = _lru_branch(xb, gb, state_lru[:, j, 0], state_lru[:, j, 1],
                                   e_conv_w[j], e_conv_b[j], wg, e_lru_lam[j])
            mixed_s = [oa, ob]
        else:
            lam_init = 0.8 - 0.6 * math.exp(-0.3 * i)
            lam_params = jnp.stack([o_lq1[j], o_lk1[j], o_lq2[j], o_lk2[j]], axis=0)

            q, k, v, g = proj_p
            mixed_p = [_diff_attn(q, [(k, v, True)], g, lam_params, o_sub_g[j], lam_init, tq=tp,
                                  heads_per_step=DIFF_HEADS, batch_per_step=CTX_BATCH_PER_STEP)]
            df_k.append(k.reshape(bp, tp, DIFF_HEADS, 2 * DIFF_HEAD_DIM))
            df_v.append(v.reshape(bp, tp, DIFF_HEADS, 2 * DIFF_HEAD_DIM))

            q, k, v, g = proj_s
            ck = cache_diff_k[:, j].reshape(bs, past * DIFF_HEADS, 2 * DIFF_HEAD_DIM)
            cv = cache_diff_v[:, j].reshape(bs, past * DIFF_HEADS, 2 * DIFF_HEAD_DIM)
            mixed_s = [_diff_attn(q, [(ck, cv, True), (k, v, False)], g, lam_params, o_sub_g[j], lam_init, tq=ts,
                                  heads_per_step=4)]

        if i == DEPTH - 1:
            xp, _ = _stage(xp, prev=(mixed_p, w_out, mods[i][1]), final_g=final_norm, tm=512)
            xs, _ = _stage(xs, prev=(mixed_s, w_out, mods[i][0]), final_g=final_norm, tm=1024)
        else:
            xp, proj_p = _stage(xp, prev=(mixed_p, w_out, mods[i][1]), nxt=projection(i + 1, False), tm=512)
            xs, proj_s = _stage(xs, prev=(mixed_s, w_out, mods[i][0]), nxt=projection(i + 1, True), tm=512)

    return (xp, xs, jnp.stack(na_k, axis=1), jnp.stack(na_v, axis=1), jnp.stack(lru_s, axis=1),
            jnp.stack(df_k, axis=1), jnp.stack(df_v, axis=1))
```

```python
import functools
import math

import jax
import jax.numpy as jnp
from jax import lax
from jax.experimental import pallas as pl
from jax.experimental.pallas import tpu as pltpu

D_MODEL = 1024
DEPTH = 2
GRID_W = 64
NA_HEADS = 8
NA_HEAD_DIM = 64
NA_WIDTH = NA_HEADS * NA_HEAD_DIM
NA_ROWS_MAX = 8
NA_COLS = 16
RPB_ROWS = 2 * NA_ROWS_MAX - 1
RPB_COLS = 2 * NA_COLS - 1
LRU_WIDTH = 512
LRU_BLOCKS = 8
LRU_BLOCK = LRU_WIDTH // LRU_BLOCKS
LRU_C = 8.0
DIFF_HEADS = 8
DIFF_HEAD_DIM = 64
DIFF_WIDTH = DIFF_HEADS * 2 * DIFF_HEAD_DIM
ROPE_BASE = 10000.0
EPS = 1e-6
LOG2E = math.log2(math.e)

LANES = 128
SUBLANES = 8
MXU_DIM = 256
VMEM_LIMIT = 56 * 1024 * 1024

CAST_COLS = 512
LRU_SEGMENTS = 4
LRU_UNROLL = 8
LRU_CHUNK = 128
CONV_PAD_L = 2
CONV_PAD_R = 1
ROW_BLK = MXU_DIM
KEY_TILE = MXU_DIM
SCORE_SLOTS = 4

NA_Q_ROWS = 4
NA_K_ROWS = 12
NA_PAIR_TILES = 16
NA_PAIRS_PER_STEP = 4
CTX_BATCH_PER_STEP = 2

F32 = jnp.float32
BF16 = jnp.bfloat16


def _params(*semantics):
    return pltpu.CompilerParams(dimension_semantics=semantics, vmem_limit_bytes=VMEM_LIMIT)


def _sigmoid(x):
    return 0.5 * jnp.tanh(0.5 * x) + 0.5


def _silu(x):
    return x * _sigmoid(x)


def _dot(a, b):
    return jnp.dot(a, b, preferred_element_type=F32)


def _dot_nt(a, b):
    return lax.dot_general(a, b, (((1,), (1,)), ((), ())), preferred_element_type=F32)


def _mod_kernel(c_ref, w_ref, b_ref, o_ref):
    s = _silu(c_ref[...])
    o_ref[...] = _dot(s.astype(BF16), w_ref[...].astype(BF16)) + b_ref[...]


def _modulation(cc, w, b):
    rows, d = cc.shape
    n = w.shape[1]
    tn = 1024
    return pl.pallas_call(
        _mod_kernel,
        grid=(n // tn,),
        in_specs=[pl.BlockSpec((rows, d), lambda j: (0, 0)),
                  pl.BlockSpec((d, tn), lambda j: (0, j)),
                  pl.BlockSpec((1, tn), lambda j: (0, j))],
        out_specs=pl.BlockSpec((rows, tn), lambda j: (0, j)),
        out_shape=jax.ShapeDtypeStruct((rows, n), F32),
        compiler_params=_params("arbitrary"),
        name="modulation",
    )(cc, w, b.reshape(1, n))


def _rope_block(x, cos, sin_signed, low16):
    partner = jnp.where(low16, pltpu.roll(x, LANES - 16, axis=1), pltpu.roll(x, 16, axis=1))
    return x * cos + partner * sin_signed


def _first_step():
    return (pl.program_id(0) == 0) & (pl.program_id(1) == 0)


def _cast_weights_once(w_ref, wbf_ref):
    @pl.when(_first_step())
    def _():
        for off in range(0, w_ref.shape[1], CAST_COLS):
            wbf_ref[:, off:off + CAST_COLS] = w_ref[:, off:off + CAST_COLS].astype(BF16)


def _weight_chunk_copy(w_hbm, stage_ref, sem, c):
    cols = slice(c * CAST_COLS, (c + 1) * CAST_COLS)
    return pltpu.make_async_copy(w_hbm.at[:, cols], stage_ref.at[:, cols], sem.at[c])


def _start_weight_stream(w_hbm, stage_ref, sem):
    @pl.when(_first_step())
    def _():
        for c in range(w_hbm.shape[1] // CAST_COLS):
            _weight_chunk_copy(w_hbm, stage_ref, sem, c).start()


def _land_weight_chunks(w_hbm, stage_ref, sem, wbf_ref, off, width):
    @pl.when(_first_step())
    def _():
        for c in range(off // CAST_COLS, (off + width) // CAST_COLS):
            _weight_chunk_copy(w_hbm, stage_ref, sem, c).wait()
            cols = slice(c * CAST_COLS, (c + 1) * CAST_COLS)
            wbf_ref[:, cols] = stage_ref[:, cols].astype(BF16)


def _norm_project(x, mod_ref, g_ref, weights, rope_refs, out_refs, outs):
    w_hbm, stage_ref, sem, wbf_ref = weights
    tm, d = x.shape
    y = x * lax.rsqrt(jnp.mean(x * x, axis=-1, keepdims=True) + EPS) * g_ref[...]
    mod = mod_ref[0]
    h = (y * (1.0 + mod[:, d:2 * d]) + mod[:, :d]).astype(BF16)
    if rope_refs:
        cos = rope_refs[0][...]
        sin = rope_refs[1][...]
        lane = lax.broadcasted_iota(jnp.int32, (1, LANES), 1)
        low16 = (lane & 16) == 0
    off = 0
    for o_ref, (width, roped, mul, packed) in zip(out_refs, outs):
        _land_weight_chunks(w_hbm, stage_ref, sem, wbf_ref, off, width)
        acc = _dot(h, wbf_ref[:, off:off + width])
        if roped:
            acc = jnp.concatenate(
                [_rope_block(acc[:, j:j + LANES], cos, sin, low16) for j in range(0, width, LANES)], axis=1)
        if mul is not None:
            acc = acc * mul
        if packed:
            n_heads = width // LANES
            t = o_ref.shape[1] // n_heads
            for bi in range(o_ref.shape[0]):
                for hh in range(n_heads):
                    o_ref[bi, pl.ds(hh, t, stride=n_heads), :] = (
                        acc[bi * t:(bi + 1) * t, hh * LANES:(hh + 1) * LANES].astype(o_ref.dtype))
        else:
            o_ref[...] = acc.astype(o_ref.dtype).reshape(o_ref.shape)
        off += width


def _token_rows(ref):
    return ref[...].reshape(-1, ref.shape[-1])


def _gated_residual(o_refs, wbf_ref, x_ref, mod_ref):
    d = x_ref.shape[-1]
    acc, row = None, 0
    for o_ref in o_refs:
        k = o_ref.shape[-1]
        part = _dot(_token_rows(o_ref), wbf_ref[row:row + k, :])
        acc = part if acc is None else acc + part
        row += k
    return _token_rows(x_ref) + mod_ref[0][:, 2 * d:] * acc


def _stage_kernel(*refs, n_o, final, outs, rope):
    refs = list(refs)
    take = lambda n: [refs.pop(0) for _ in range(n)]
    if n_o:
        o_refs = take(n_o)
        wout_ref, x_ref, mod_prev_ref = take(3)
        fin_ref = take(1)[0] if final else None
    else:
        x_ref, = take(1)
    if outs:
        mod_ref, g_ref, win_ref = take(3)
        rope_refs = take(2) if rope else []
    xout_ref = take(1)[0] if n_o else None
    out_refs = take(len(outs))
    if outs:
        stage_ref, win_bf, sem = refs[-3:]
        _start_weight_stream(win_ref, stage_ref, sem)
    if n_o:
        wout_bf = take(1)[0]
        _cast_weights_once(wout_ref, wout_bf)
        x = _gated_residual(o_refs, wout_bf, x_ref, mod_prev_ref)
        if final:
            x_new = x * lax.rsqrt(jnp.mean(x * x, axis=-1, keepdims=True) + EPS) * fin_ref[...]
        else:
            x_new = x
        xout_ref[...] = x_new.reshape(xout_ref.shape)
    else:
        x = _token_rows(x_ref)
    if outs:
        _norm_project(x, mod_ref, g_ref, (win_ref, stage_ref, sem, win_bf), rope_refs, out_refs, outs)


def _stage(x, prev=None, nxt=None, final_g=None, tm=256):
    b, t, d = x.shape
    shared_mod = all(m.shape[0] == 1 for m in ([prev[2]] if prev else []) + ([nxt[0]] if nxt else []))
    bb = max(1, tm // t) if shared_mod else 1
    tm = min(tm, t)
    tile = lambda width: pl.BlockSpec((bb, tm, width), lambda i, j: (i, j, 0))
    mod_spec = lambda mod: pl.BlockSpec(
        (1, 1, 3 * d), (lambda i, j: (i, 0, 0)) if mod.shape[0] > 1 else (lambda i, j: (0, 0, 0)))
    resident = lambda w: pl.BlockSpec(w.shape, lambda i, j: (0, 0), pipeline_mode=pl.Buffered(1))
    row_spec = pl.BlockSpec((1, d), lambda i, j: (0, 0))
    in_specs, args, out_specs, out_shape, scratch = [], [], [], [], []
    n_o, outs, rope = 0, (), False
    if prev is not None:
        o_list, w_out, mod_prev = prev
        n_o = len(o_list)
        in_specs += [tile(o.shape[-1]) for o in o_list] + [resident(w_out), tile(d), mod_spec(mod_prev)]
        args += list(o_list) + [w_out, x, mod_prev]
        if final_g is not None:
            in_specs.append(row_spec)
            args.append(final_g.reshape(1, d))
        out_specs.append(tile(d))
        out_shape.append(jax.ShapeDtypeStruct((b, t, d), F32))
        scratch.append(pltpu.VMEM(w_out.shape, BF16))
    else:
        in_specs.append(tile(d))
        args.append(x)
    if nxt is not None:
        mod_next, norm_g, w_in, out_list, rope_tables = nxt
        assert all(wd % CAST_COLS == 0 for wd, _, _, _, _ in out_list)
        in_specs += [mod_spec(mod_next), row_spec, pl.BlockSpec(memory_space=pl.ANY)]
        args += [mod_next, norm_g.reshape(1, d), w_in]
        rope = rope_tables is not None
        if rope:
            in_specs += [pl.BlockSpec((tm, LANES), lambda i, j: (j, 0))] * 2
            args += list(rope_tables)
        for width, dt, _, _, packed in out_list:
            if packed:
                heads = width // LANES
                out_specs.append(pl.BlockSpec((bb, tm * heads, LANES), lambda i, j: (i, j, 0)))
                out_shape.append(jax.ShapeDtypeStruct((b, t * heads, LANES), dt))
            else:
                out_specs.append(tile(width))
                out_shape.append(jax.ShapeDtypeStruct((b, t, width), dt))
        outs = tuple((w_, r, m, p) for w_, _, r, m, p in out_list)
        scratch += [pltpu.VMEM(w_in.shape, F32), pltpu.VMEM(w_in.shape, BF16),
                    pltpu.SemaphoreType.DMA((w_in.shape[1] // CAST_COLS,))]
    res = pl.pallas_call(
        functools.partial(_stage_kernel, n_o=n_o, final=final_g is not None, outs=outs, rope=rope),
        grid=(b // bb, t // tm),
        in_specs=in_specs,
        out_specs=out_specs,
        out_shape=out_shape,
        scratch_shapes=scratch,
        compiler_params=_params("arbitrary", "arbitrary"),
        name="stage",
    )(*args)
    return (res[0], list(res[1:])) if n_o else (None, list(res))


def _attend(q_blocks, kv_blocks, s_ref, score_fn=None):
    n_rb = len(q_blocks)

    def tile_of(rb, c):
        slot = rb % SCORE_SLOTS
        return (slice(slot * ROW_BLK, (slot + 1) * ROW_BLK), slice(c * KEY_TILE, (c + 1) * KEY_TILE))

    def score_tile(rb, c):
        s_ref[tile_of(rb, c)] = _dot_nt(q_blocks[rb], kv_blocks[rb][0][c])

    def max_tile(rb, c, m_acc):
        s = s_ref[tile_of(rb, c)]
        adjusted = None if score_fn is None else score_fn(rb, c, s)
        if adjusted is not None:
            s = adjusted
            s_ref[tile_of(rb, c)] = s
        mt = jnp.maximum(s[:, :LANES], s[:, LANES:])
        return mt if m_acc is None else jnp.maximum(m_acc, mt)

    def n_tiles(rb):
        return len(kv_blocks[rb][0]) if 0 <= rb < n_rb else 0

    outs = []
    row_max = {}
    for step in range(n_rb + 2):
        rb_score, rb_max, rb_exp = step, step - 1, step - 2
        m_acc, acc = None, None
        for c in range(max(n_tiles(rb_score), n_tiles(rb_max), n_tiles(rb_exp))):
            if c < n_tiles(rb_score):
                score_tile(rb_score, c)
            if c < n_tiles(rb_max):
                m_acc = max_tile(rb_max, c, m_acc)
            if c < n_tiles(rb_exp):
                e = jnp.exp2(s_ref[tile_of(rb_exp, c)] - row_max[rb_exp])
                pv = _dot(e.astype(BF16), kv_blocks[rb_exp][1][c])
                acc = pv if acc is None else acc + pv
        if m_acc is not None:
            row_max[rb_max] = m_acc.max(axis=-1, keepdims=True)
        if acc is not None:
            outs.append(acc)
    return outs


def _lane_low_half():
    return lax.broadcasted_iota(jnp.int32, (1, LANES), 1) < (LANES // 2)


def _values_and_ones(v_tiles):
    ones = jnp.ones((KEY_TILE, LANES), BF16)
    return [jnp.concatenate([v, ones], axis=1) for v in v_tiles]


def _normalise_pair(acc_lo, acc_hi, low):
    return jnp.where(low, acc_lo[:, :LANES] / acc_lo[:, LANES:], acc_hi[:, :LANES] / acc_hi[:, LANES:])


def _split_lane_halves(q, low):
    zero = jnp.zeros_like(q)
    return jnp.where(low, q, zero), jnp.where(low, zero, q)


def _tiles(ref, n_tiles, j, bi=0):
    return [ref[bi, c * KEY_TILE:(c + 1) * KEY_TILE, j:j + LANES].astype(BF16) for c in range(n_tiles)]


def _ctx_attn_kernel(q_ref, k_ref, v_ref, g_ref, o_ref, s_ref):
    low = _lane_low_half()
    nb, tq, w = q_ref.shape
    n_tiles = k_ref.shape[1] // KEY_TILE
    tiles = [(bi, j, r) for bi in range(nb) for j in range(0, w, LANES) for r in range(0, tq, ROW_BLK)]
    q_blocks, kv_blocks, kv = [], [], {}
    for bi, j, r in tiles:
        if (bi, j) not in kv:
            kv[bi, j] = (_tiles(k_ref, n_tiles, j, bi), _values_and_ones(_tiles(v_ref, n_tiles, j, bi)))
        q_blocks += list(_split_lane_halves(q_ref[bi, r:r + ROW_BLK, j:j + LANES], low))
        kv_blocks += [kv[bi, j]] * 2
    res = _attend(q_blocks, kv_blocks, s_ref)
    for i, (bi, j, r) in enumerate(tiles):
        o = _normalise_pair(res[2 * i], res[2 * i + 1], low)
        gate = _silu(g_ref[bi, r:r + ROW_BLK, j:j + LANES].astype(F32))
        o_ref[bi, r:r + ROW_BLK, j:j + LANES] = (o * gate).astype(o_ref.dtype)


def _ctx_attn(q, k, v, g):
    b, t, w = q.shape
    nb = CTX_BATCH_PER_STEP
    spec = pl.BlockSpec((nb, t, w), lambda i: (i, 0, 0))
    return pl.pallas_call(
        _ctx_attn_kernel,
        grid=(b // nb,),
        in_specs=[spec] * 4,
        out_specs=spec,
        out_shape=jax.ShapeDtypeStruct((b, t, w), BF16),
        scratch_shapes=[pltpu.VMEM((SCORE_SLOTS * ROW_BLK, k.shape[1]), F32)],
        compiler_params=_params("parallel"),
        name="ctx_attn",
    )(q, k, v, g)


def _na_pair_tiles(rpb_ref, head, pt_ref):
    right = lax.broadcasted_iota(jnp.int32, (GRID_W, LANES), 1) >= GRID_W
    zero_row = jnp.zeros((1, LANES), F32)

    def toeplitz(row, lane0):
        shift = (lane0 - (NA_COLS - 1)) % LANES
        return pltpu.roll(jnp.broadcast_to(row, (GRID_W, LANES)), shift, axis=1, stride=1, stride_axis=0)

    for p in range(NA_PAIR_TILES):
        left_row = rpb_ref[head, p - 1:p, :] if p >= 1 else zero_row
        right_row = rpb_ref[head, p:p + 1, :]
        pt_ref[head, p] = jnp.where(right, toeplitz(right_row, GRID_W), toeplitz(left_row, 0)) * LOG2E


def _na_row_start(r, n_rows):
    kr = min(NA_ROWS_MAX, n_rows)
    return min(max(r - kr // 2, 0), n_rows - kr)


def _na_window(grp, n_rows):
    kr = min(NA_ROWS_MAX, n_rows)
    starts = [_na_row_start(grp * NA_Q_ROWS + i, n_rows) for i in range(NA_Q_ROWS)]
    rows_per_tile = KEY_TILE // GRID_W
    need = -(-(max(starts) + kr - min(starts)) // rows_per_tile) * rows_per_tile
    return min(min(starts), n_rows - need), need


def _na_kernel(q_ref, k_ref, v_ref, ck_ref, cv_ref, rpb_ref, g_ref, o_ref, pt_ref, badd_ref, s_ref, *, n_rows):
    kr = min(NA_ROWS_MAX, n_rows)
    n_grp = n_rows // NA_Q_ROWS
    n_pairs = q_ref.shape[2] // LANES
    low = _lane_low_half()

    @pl.when(pl.program_id(1) == 0)
    def _build_bias():
        for head in range(2 * n_pairs):
            _na_pair_tiles(rpb_ref, head, pt_ref)
        qcol = lax.broadcasted_iota(jnp.int32, (GRID_W, LANES), 0)
        lane = lax.broadcasted_iota(jnp.int32, (GRID_W, LANES), 1)
        right = lane >= GRID_W
        kcol = lane & (GRID_W - 1)
        col_start = jnp.clip(qcol - NA_COLS // 2, 0, GRID_W - NA_COLS)
        col_ok = (kcol >= col_start) & (kcol < col_start + NA_COLS)
        for grp in range(n_grp):
            krow0, key_rows = _na_window(grp, n_rows)
            for i in range(NA_Q_ROWS):
                r = grp * NA_Q_ROWS + i
                row_start = _na_row_start(r, n_rows)
                for m in range(key_rows // 2):
                    key_row = krow0 + 2 * m
                    ok_l = row_start <= key_row < row_start + kr
                    ok_r = row_start <= key_row + 1 < row_start + kr
                    rows = slice(i * GRID_W, (i + 1) * GRID_W)
                    cols = slice(m * LANES, (m + 1) * LANES)
                    if not (ok_l or ok_r):
                        for head in range(2 * n_pairs):
                            badd_ref[head * n_grp + grp, rows, cols] = jnp.full((GRID_W, LANES), -jnp.inf, F32)
                        continue
                    ok = col_ok
                    if not ok_r:
                        ok = ok & jnp.logical_not(right)
                    if not ok_l:
                        ok = ok & right
                    p = key_row - r + NA_ROWS_MAX
                    for head in range(2 * n_pairs):
                        badd_ref[head * n_grp + grp, rows, cols] = jnp.where(ok, pt_ref[head, p], -jnp.inf)

    n_ctx = ck_ref.shape[1] // KEY_TILE
    q_blocks, kv_blocks, blocks = [], [], []
    for pair in range(n_pairs):
        lanes = slice(pair * LANES, (pair + 1) * LANES)
        ctx_k = _tiles(ck_ref, n_ctx, pair * LANES)
        ctx_v = _tiles(cv_ref, n_ctx, pair * LANES)
        for grp in range(n_grp):
            krow0, key_rows = _na_window(grp, n_rows)
            k0 = krow0 * GRID_W
            win = [slice(k0 + c * KEY_TILE, k0 + (c + 1) * KEY_TILE) for c in range(key_rows * GRID_W // KEY_TILE)]
            kv = ([k_ref[0, w, lanes] for w in win] + ctx_k,
                  _values_and_ones([v_ref[0, w, lanes] for w in win] + ctx_v))
            q_blocks += list(_split_lane_halves(q_ref[0, grp * ROW_BLK:(grp + 1) * ROW_BLK, lanes], low))
            kv_blocks += [kv, kv]
            blocks += [(2 * pair, grp, len(win)), (2 * pair + 1, grp, len(win))]

    def masked_bias(rb, c, s):
        head, grp, n_win = blocks[rb]
        if c >= n_win:
            return None
        cols = slice(c * KEY_TILE, (c + 1) * KEY_TILE)
        return s + badd_ref[head * n_grp + grp, :, cols]

    res = _attend(q_blocks, kv_blocks, s_ref, masked_bias)
    for pair in range(n_pairs):
        lanes = slice(pair * LANES, (pair + 1) * LANES)
        for grp in range(n_grp):
            rb = 2 * (pair * n_grp + grp)
            o = _normalise_pair(res[rb], res[rb + 1], low)
            rows = slice(grp * ROW_BLK, (grp + 1) * ROW_BLK)
            o_ref[0, rows, lanes] = (o * _silu(g_ref[0, rows, lanes].astype(F32))).astype(o_ref.dtype)


def _na_attn(q, k, v, ck, cv, rpb, g):
    heads, rpb_rows, rpb_cols = rpb.shape
    rpb = jnp.pad(rpb, ((0, 0), (0, NA_PAIR_TILES - rpb_rows), (0, LANES - rpb_cols)))
    b, t, w = q.shape
    c = ck.shape[1]
    n_rows = t // GRID_W
    assert NA_Q_ROWS * GRID_W == ROW_BLK and n_rows % NA_Q_ROWS == 0 and n_rows >= NA_K_ROWS
    n_grp = n_rows // NA_Q_ROWS
    n_win = NA_K_ROWS * GRID_W
    wb = NA_PAIRS_PER_STEP * LANES
    hb = 2 * NA_PAIRS_PER_STEP
    tspec = pl.BlockSpec((1, t, wb), lambda hp, bi: (bi, 0, hp))
    cspec = pl.BlockSpec((1, c, wb), lambda hp, bi: (bi, 0, hp))
    ptspec = pl.BlockSpec((hb, NA_PAIR_TILES, LANES), lambda hp, bi: (hp, 0, 0))
    return pl.pallas_call(
        functools.partial(_na_kernel, n_rows=n_rows),
        grid=(w // wb, b),
        in_specs=[tspec, tspec, tspec, cspec, cspec, ptspec, tspec],
        out_specs=tspec,
        out_shape=jax.ShapeDtypeStruct((b, t, w), BF16),
        scratch_shapes=[pltpu.VMEM((hb, NA_PAIR_TILES, GRID_W, LANES), F32),
                        pltpu.VMEM((hb * n_grp, ROW_BLK, n_win), F32),
                        pltpu.VMEM((SCORE_SLOTS * ROW_BLK, n_win + c), F32)],
        compiler_params=_params("parallel", "arbitrary"),
        name="na_attn",
    )(q, k, v, ck, cv, rpb, g)


def _softplus(x):
    return jnp.maximum(x, 0.0) + jnp.log(1.0 + jnp.exp(-jnp.abs(x)))


def _lru_kernel(x_ref, gb_ref, h0f_ref, h0b_ref, cw_ref, cb_ref, wg_ref, lam_ref,
                o_ref, ff_ref, fb_ref, xt_s, af_s, uf_s, ab_s, ub_s, hf_s, hb_s):
    nb, t, _ = x_ref.shape
    pad = CONV_PAD_L * nb
    cw = cw_ref[...]
    cb = cb_ref[...]
    wg = wg_ref[0]
    kk = (-0.5 * LRU_C * LOG2E) * _softplus(-lam_ref[...])

    xt_s[0:pad, :] = jnp.zeros((pad, LANES), F32)
    xt_s[pad + t * nb:pad + (t + CONV_PAD_R) * nb, :] = jnp.zeros((CONV_PAD_R * nb, LANES), F32)

    def to_time_major(b, carry):
        xt_s[pl.ds(pad + b, t, stride=nb), :] = x_ref[b]
        return carry

    lax.fori_loop(0, nb, to_time_major, 0)

    rows_per_chunk = LRU_CHUNK * nb
    lane = lax.broadcasted_iota(jnp.int32, (rows_per_chunk, LANES), 1)
    bias_cols = jnp.where(lane < 2, 1.0, 0.0).astype(BF16)

    def gates(c, carry):
        r0 = pl.multiple_of(c * rows_per_chunk, rows_per_chunk)

        def tap(i):
            return xt_s[pl.ds(r0 + i * nb, rows_per_chunk), :]

        x = tap(2)
        xc = cw[0:1] * tap(0) + cw[1:2] * tap(1) + cw[2:3] * x + cw[3:4] * tap(3) + cb
        th = jnp.tanh(_dot(jnp.concatenate([xc.astype(BF16), bias_cols], axis=1), wg))
        xh = 0.5 * xc
        rows = pl.ds(r0, rows_per_chunk)
        for d, (a_s, u_s) in enumerate(((af_s, uf_s), (ab_s, ub_s))):
            t_a = th[:, (2 * d) * LANES:(2 * d + 1) * LANES]
            t_x = th[:, (2 * d + 1) * LANES:(2 * d + 2) * LANES]
            a = jnp.exp2(kk[d:d + 1] * t_a + kk[d:d + 1])
            y = 1.0 - a * a
            sq = y * lax.rsqrt(jnp.maximum(y, 1e-30))
            a_s[rows, :] = a
            u_s[rows, :] = sq * (xh * t_x + xh)
        return carry

    lax.fori_loop(0, t // LRU_CHUNK, gates, 0)

    ts = t // LRU_SEGMENTS
    refs = ((af_s, uf_s), (ab_s, ub_s))

    slab = nb * LRU_UNROLL

    def slab_rows(d, j, i):
        base = (j * ts + i * LRU_UNROLL) * nb
        return pl.ds(pl.multiple_of(base if d == 0 else t * nb - slab - base, slab), slab)

    def step_of(d, block, k):
        kk_ = k if d == 0 else LRU_UNROLL - 1 - k
        return block[kk_ * nb:(kk_ + 1) * nb]

    def compose(i, carry):
        state = list(carry)
        for d in range(2):
            for j in range(LRU_SEGMENTS - 1):
                n = d * (LRU_SEGMENTS - 1) + j
                a_blk = refs[d][0][slab_rows(d, j, i), :]
                u_blk = refs[d][1][slab_rows(d, j, i), :]
                g, big_a = state[n]
                for k in range(LRU_UNROLL):
                    a = step_of(d, a_blk, k)
                    g, big_a = a * g + step_of(d, u_blk, k), a * big_a
                state[n] = (g, big_a)
        return tuple(state)

    ones = jnp.ones((nb, LANES), F32)
    h0 = (h0f_ref[...], h0b_ref[...])
    init = tuple((h0[d] if j == 0 else jnp.zeros((nb, LANES), F32), ones)
                 for d in range(2) for j in range(LRU_SEGMENTS - 1))
    ends = lax.fori_loop(0, ts // LRU_UNROLL, compose, init)
    starts = []
    for d in range(2):
        h = h0[d]
        for j in range(LRU_SEGMENTS):
            starts.append(h)
            if j < LRU_SEGMENTS - 1:
                g, big_a = ends[d * (LRU_SEGMENTS - 1) + j]
                h = g if j == 0 else g + big_a * h

    def rerun(i, carry):
        state = list(carry)
        for d, h_s in enumerate((hf_s, hb_s)):
            for j in range(LRU_SEGMENTS):
                n = d * LRU_SEGMENTS + j
                a_blk = refs[d][0][slab_rows(d, j, i), :]
                u_blk = refs[d][1][slab_rows(d, j, i), :]
                hs = []
                for k in range(LRU_UNROLL):
                    state[n] = step_of(d, a_blk, k) * state[n] + step_of(d, u_blk, k)
                    hs.append(state[n])
                h_s[slab_rows(d, j, i), :] = jnp.concatenate(hs if d == 0 else hs[::-1], axis=0)
        return tuple(state)

    final = lax.fori_loop(0, ts // LRU_UNROLL, rerun, tuple(starts))
    ff_ref[...] = final[LRU_SEGMENTS - 1]
    fb_ref[...] = final[2 * LRU_SEGMENTS - 1]

    def add_directions(c, carry):
        rows = pl.ds(pl.multiple_of(c * rows_per_chunk, rows_per_chunk), rows_per_chunk)
        hf_s[rows, :] = hf_s[rows, :] + hb_s[rows, :]
        return carry

    lax.fori_loop(0, t // LRU_CHUNK, add_directions, 0)

    def emit(b, carry):
        hs = hf_s[pl.ds(b, t, stride=nb), :]
        o_ref[b] = (hs * _silu(gb_ref[b].astype(F32))).astype(o_ref.dtype)
        return carry

    lax.fori_loop(0, nb, emit, 0)


def _lru_branch(xb, gb, h0f, h0b, conv_w, conv_b, wg, lam):
    b, t, w = xb.shape
    nb = SUBLANES
    assert t % (LRU_SEGMENTS * LRU_UNROLL) == 0 and t % LRU_CHUNK == 0
    assert conv_w.shape[0] == CONV_PAD_L + 1 + CONV_PAD_R
    xspec = pl.BlockSpec((nb, t, LANES), lambda i, c: (i, 0, c))
    hspec = pl.BlockSpec((nb, LANES), lambda i, c: (i, c))
    scratch = [pltpu.VMEM(((t + CONV_PAD_L + CONV_PAD_R) * nb, LANES), F32)]
    scratch += [pltpu.VMEM((t * nb, LANES), F32) for _ in range(6)]
    return pl.pallas_call(
        _lru_kernel,
        grid=(b // nb, w // LANES),
        in_specs=[xspec, xspec, hspec, hspec,
                  pl.BlockSpec((conv_w.shape[0], LANES), lambda i, c: (0, c)),
                  pl.BlockSpec((1, LANES), lambda i, c: (0, c)),
                  pl.BlockSpec((1, 2 * LANES, 4 * LANES), lambda i, c: (c, 0, 0)),
                  pl.BlockSpec((2, LANES), lambda i, c: (0, c))],
        out_specs=[xspec, hspec, hspec],
        out_shape=[jax.ShapeDtypeStruct((b, t, w), BF16),
                   jax.ShapeDtypeStruct((b, w), F32),
                   jax.ShapeDtypeStruct((b, w), F32)],
        scratch_shapes=scratch,
        compiler_params=_params("parallel", "parallel"),
        name="lru_branch",
    )(xb, gb, h0f, h0b, conv_w, conv_b.reshape(1, w), wg, lam)


def _lru_gate_weights(wa, ba, wx, bx):
    def tiles(w):
        w2 = w.reshape(LRU_BLOCKS // 2, 2, LRU_BLOCK, LRU_BLOCK)
        z = jnp.zeros_like(w2[:, 0])
        return jnp.concatenate([jnp.concatenate([w2[:, 0], z], axis=2),
                                jnp.concatenate([z, w2[:, 1]], axis=2)], axis=1)
    wg = 0.5 * jnp.concatenate([tiles(wa[0]), tiles(wx[0]), tiles(wa[1]), tiles(wx[1])], axis=2)
    n_tiles = LRU_WIDTH // LANES
    bg = jnp.stack([ba[0], bx[0], ba[1], bx[1]], axis=0).reshape(4, n_tiles, LANES)
    bg = 0.5 * jnp.transpose(bg, (1, 0, 2)).reshape(n_tiles, 1, 4 * LANES)
    hi = bg.astype(BF16)
    lo = (bg - hi.astype(F32)).astype(BF16)
    rest = jnp.zeros((n_tiles, LANES - 2, 4 * LANES), BF16)
    return jnp.concatenate([wg.astype(BF16), hi, lo, rest], axis=1)


def _packed_tiles(ref, n_heads, head, bi=0):
    n_tiles = ref.shape[1] // (n_heads * KEY_TILE)
    return [ref[bi, pl.ds(c * KEY_TILE * n_heads + head, KEY_TILE, stride=n_heads), :].astype(BF16)
            for c in range(n_tiles)]


def _diff_kernel(*refs, packed, n_heads, lam_init):
    n_kv = len(packed)
    q_ref = refs[0]
    kv_refs = refs[1:1 + 2 * n_kv]
    g_ref, lamp_ref, subg_ref, o_ref, s_ref = refs[1 + 2 * n_kv:]
    low = _lane_low_half()
    tq = q_ref.shape[1]
    first_head = pl.program_id(1) * (q_ref.shape[2] // LANES)
    lp = lamp_ref[...]
    lam = (jnp.exp(jnp.sum(lp[0:1] * lp[1:2], axis=-1, keepdims=True))
           - jnp.exp(jnp.sum(lp[2:3] * lp[3:4], axis=-1, keepdims=True)) + lam_init)
    nb = q_ref.shape[0]
    tiles = [(bi, j) for bi in range(nb) for j in range(0, q_ref.shape[2], LANES)]
    q_blocks, kv_blocks = [], []
    for bi, j in tiles:
        k_tiles, v_tiles = [], []
        for i in range(n_kv):
            if packed[i]:
                head = first_head + j // LANES
                k_tiles += _packed_tiles(kv_refs[2 * i], n_heads, head, bi)
                v_tiles += _packed_tiles(kv_refs[2 * i + 1], n_heads, head, bi)
            else:
                n_tiles = kv_refs[2 * i].shape[1] // KEY_TILE
                k_tiles += _tiles(kv_refs[2 * i], n_tiles, j, bi)
                v_tiles += _tiles(kv_refs[2 * i + 1], n_tiles, j, bi)
        v_tiles = _values_and_ones(v_tiles)
        for r in range(0, tq, ROW_BLK):
            q_blocks += list(_split_lane_halves(q_ref[bi, r:r + ROW_BLK, j:j + LANES], low))
            kv_blocks += [(k_tiles, v_tiles)] * 2
    res = _attend(q_blocks, kv_blocks, s_ref)
    i = 0
    for bi, j in tiles:
        for r in range(0, tq, ROW_BLK):
            acc1, acc2 = res[i], res[i + 1]
            i += 2
            o = acc1[:, :LANES] / acc1[:, LANES:] - acc2[:, :LANES] * (lam / acc2[:, LANES:])
            o = o * lax.rsqrt(jnp.mean(o * o, axis=-1, keepdims=True) + EPS) * subg_ref[...] * (1.0 - lam_init)
            gate = _silu(g_ref[bi, r:r + ROW_BLK, j:j + LANES].astype(F32))
            o_ref[bi, r:r + ROW_BLK, j:j + LANES] = (o * gate).astype(o_ref.dtype)


def _diff_attn(q, kv_list, g, lam_params, sub_g, lam_init, tq, heads_per_step, batch_per_step=1):
    b, t, w = q.shape
    wb = heads_per_step * LANES
    nb = batch_per_step
    qspec = pl.BlockSpec((nb, tq, wb), lambda bi, h, qi: (bi, qi, h))
    n_heads = w // LANES
    in_specs = [qspec]
    args = [q]
    n_keys = 0
    for k, v, packed in kv_list:
        if packed:
            spec = pl.BlockSpec((nb, k.shape[1], LANES), lambda bi, h, qi: (bi, 0, 0))
            n_keys += k.shape[1] // n_heads
        else:
            spec = pl.BlockSpec((nb, k.shape[1], wb), lambda bi, h, qi: (bi, 0, h))
            n_keys += k.shape[1]
        in_specs += [spec, spec]
        args += [k, v]
    in_specs += [qspec,
                 pl.BlockSpec(lam_params.shape, lambda bi, h, qi: (0, 0)),
                 pl.BlockSpec((1, LANES), lambda bi, h, qi: (0, 0))]
    args += [g, lam_params, sub_g.reshape(1, LANES)]
    return pl.pallas_call(
        functools.partial(_diff_kernel, packed=tuple(p for _, _, p in kv_list), n_heads=n_heads,
                          lam_init=lam_init),
        grid=(b // nb, w // wb, t // tq),
        in_specs=in_specs,
        out_specs=qspec,
        out_shape=jax.ShapeDtypeStruct((b, t, w), BF16),
        scratch_shapes=[pltpu.VMEM((SCORE_SLOTS * ROW_BLK, n_keys), F32)],
        compiler_params=_params("parallel", "parallel", "parallel"),
        name="diff_attn",
    )(*args)


def _rope_tables(t_len):
    m = DIFF_HEAD_DIM // 4
    t = jnp.arange(t_len)
    rows = (t // GRID_W).astype(F32)
    cols = (t % GRID_W).astype(F32)
    inv = ROPE_BASE ** (-jnp.arange(m, dtype=F32) / m)
    ang_r = rows[:, None] * inv[None, :]
    ang_c = cols[:, None] * inv[None, :]
    cos = jnp.concatenate([jnp.cos(ang_r)] * 2 + [jnp.cos(ang_c)] * 2, axis=1)
    sin = jnp.concatenate([-jnp.sin(ang_r), jnp.sin(ang_r), -jnp.sin(ang_c), jnp.sin(ang_c)], axis=1)
    reps = LANES // DIFF_HEAD_DIM
    return jnp.tile(cos, (1, reps)), jnp.tile(sin, (1, reps))


def kernel(x_prompt, x_sample, c, cache_na_k, cache_na_v, state_lru, cache_diff_k, cache_diff_v, c_ctx,
           e_norm, e_ada_w, e_ada_b, e_w_in, e_rpb, e_conv_w, e_conv_b, e_lru_wa, e_lru_ba, e_lru_wx, e_lru_bx,
           e_lru_lam, e_w_out, o_norm, o_ada_w, o_ada_b, o_w_in, o_lq1, o_lk1, o_lq2, o_lk2, o_sub_g, o_w_out,
           final_norm):
    bp, tp, d = x_prompt.shape
    bs, ts, _ = x_sample.shape
    past = cache_na_k.shape[2]
    pad_rows = 2 * SUBLANES - bs - 1
    cc = jnp.concatenate([c, c_ctx[None, :], jnp.zeros((pad_rows, d), F32)], axis=0)
    nw, lw, dw = NA_WIDTH, LRU_WIDTH, DIFF_WIDTH
    rope_tables = _rope_tables(ts) if DEPTH > 1 else None

    def layer_params(i):
        j = i // 2
        if i % 2 == 0:
            return e_ada_w[j], e_ada_b[j], e_norm[j], e_w_in[j], e_w_out[j]
        return o_ada_w[j], o_ada_b[j], o_norm[j], o_w_in[j], o_w_out[j]

    mods = []
    for i in range(DEPTH):
        ada_w, ada_b = layer_params(i)[:2]
        m = _modulation(cc, ada_w, ada_b)
        mods.append((m[:bs].reshape(bs, 1, 3 * d), m[bs:bs + 1].reshape(1, 1, 3 * d)))

    def projection(i, latent):
        norm_g, w_in = layer_params(i)[2:4]
        mod = mods[i][0 if latent else 1]
        kv_dtype = BF16 if latent else F32
        if i % 2 == 0:
            qmul = NA_HEAD_DIM ** -0.5 * LOG2E
            outs = [(nw, BF16, False, qmul, False), (nw, kv_dtype, False, None, False),
                    (nw, kv_dtype, False, None, False), (nw, BF16, False, None, False),
                    (lw, F32, False, None, False), (lw, BF16, False, None, False)]
            return mod, norm_g, w_in, outs, None
        qmul = DIFF_HEAD_DIM ** -0.5 * LOG2E
        outs = [(dw, BF16, latent, qmul, False), (dw, kv_dtype, latent, None, not latent),
                (dw, kv_dtype, False, None, not latent), (dw, BF16, False, None, False)]
        return mod, norm_g, w_in, outs, rope_tables if latent else None

    xp, xs = x_prompt, x_sample
    _, proj_p = _stage(xp, nxt=projection(0, False), tm=512)
    _, proj_s = _stage(xs, nxt=projection(0, True), tm=512)
    na_k, na_v, lru_s, df_k, df_v = [], [], [], [], []
    for i in range(DEPTH):
        j = i // 2
        w_out = layer_params(i)[4]
        if i % 2 == 0:
            wg = _lru_gate_weights(e_lru_wa[j], e_lru_ba[j], e_lru_wx[j], e_lru_bx[j])

            q, k, v, ga, xb, gb = proj_p
            oa = _ctx_attn(q, k, v, ga)
            zeros = jnp.zeros((bp, lw), F32)
            ob, ff, fb = _lru_branch(xb, gb, zeros, zeros, e_conv_w[j], e_conv_b[j], wg, e_lru_lam[j])
            mixed_p = [oa, ob]
            na_k.append(k.reshape(bp, tp, NA_HEADS, NA_HEAD_DIM))
            na_v.append(v.reshape(bp, tp, NA_HEADS, NA_HEAD_DIM))
            lru_s.append(jnp.stack([ff, fb], axis=1))

            q, k, v, ga, xb, gb = proj_s
            ck = cache_na_k[:, j].reshape(bs, past, nw)
            cv = cache_na_v[:, j].reshape(bs, past, nw)
            oa = _na_attn(q, k, v, ck, cv, e_rpb[j], ga)
            ob, _, _ = _lru_branch(xb, gb, state_lru[:, j, 0], state_lru[:, j, 1],
                                   e_conv_w[j], e_conv_b[j], wg, e_lru_lam[j])
            mixed_s = [oa, ob]
        else:
            lam_init = 0.8 - 0.6 * math.exp(-0.3 * i)
            lam_params = jnp.stack([o_lq1[j], o_lk1[j], o_lq2[j], o_lk2[j]], axis=0)

            q, k, v, g = proj_p
            mixed_p = [_diff_attn(q, [(k, v, True)], g, lam_params, o_sub_g[j], lam_init, tq=tp,
                                  heads_per_step=DIFF_HEADS, batch_per_step=CTX_BATCH_PER_STEP)]
            df_k.append(k.reshape(bp, tp, DIFF_HEADS, 2 * DIFF_HEAD_DIM))
            df_v.append(v.reshape(bp, tp, DIFF_HEADS, 2 * DIFF_HEAD_DIM))

            q, k, v, g = proj_s
            ck = cache_diff_k[:, j].reshape(bs, past * DIFF_HEADS, 2 * DIFF_HEAD_DIM)
            cv = cache_diff_v[:, j].reshape(bs, past * DIFF_HEADS, 2 * DIFF_HEAD_DIM)
            mixed_s = [_diff_attn(q, [(ck, cv, True), (k, v, False)], g, lam_params, o_sub_g[j], lam_init, tq=ts,
                                  heads_per_step=4)]

        if i == DEPTH - 1:
            xp, _ = _stage(xp, prev=(mixed_p, w_out, mods[i][1]), final_g=final_norm, tm=512)
            xs, _ = _stage(xs, prev=(mixed_s, w_out, mods[i][0]), final_g=final_norm, tm=1024)
        else:
            xp, proj_p = _stage(xp, prev=(mixed_p, w_out, mods[i][1]), nxt=projection(i + 1, False), tm=512)
            xs, proj_s = _stage(xs, prev=(mixed_s, w_out, mods[i][0]), nxt=projection(i + 1, True), tm=512)

    return (xp, xs, jnp.stack(na_k, axis=1), jnp.stack(na_v, axis=1), jnp.stack(lru_s, axis=1),
            jnp.stack(df_k, axis=1), jnp.stack(df_v, axis=1))
```

```python
import functools
import math

import jax
import jax.numpy as jnp
from jax import lax
from jax.experimental import pallas as pl
from jax.experimental.pallas import tpu as pltpu

D_MODEL = 1024
DEPTH = 2
GRID_W = 64
NA_HEADS = 8
NA_HEAD_DIM = 64
NA_WIDTH = NA_HEADS * NA_HEAD_DIM
NA_ROWS_MAX = 8
NA_COLS = 16
RPB_ROWS = 2 * NA_ROWS_MAX - 1
RPB_COLS = 2 * NA_COLS - 1
LRU_WIDTH = 512
LRU_BLOCKS = 8
LRU_BLOCK = LRU_WIDTH // LRU_BLOCKS
LRU_C = 8.0
DIFF_HEADS = 8
DIFF_HEAD_DIM = 64
DIFF_WIDTH = DIFF_HEADS * 2 * DIFF_HEAD_DIM
ROPE_BASE = 10000.0
EPS = 1e-6
LOG2E = math.log2(math.e)

LANES = 128
SUBLANES = 8
MXU_DIM = 256
VMEM_LIMIT = 56 * 1024 * 1024

CAST_COLS = 512
LRU_SEGMENTS = 4
LRU_UNROLL = 8
LRU_CHUNK = 128
CONV_PAD_L = 2
CONV_PAD_R = 1
ROW_BLK = MXU_DIM
KEY_TILE = MXU_DIM
SCORE_SLOTS = 4

NA_Q_ROWS = 4
NA_K_ROWS = 12
NA_PAIR_TILES = 16
NA_PAIRS_PER_STEP = 4
CTX_BATCH_PER_STEP = 4

F32 = jnp.float32
BF16 = jnp.bfloat16


def _params(*semantics):
    return pltpu.CompilerParams(dimension_semantics=semantics, vmem_limit_bytes=VMEM_LIMIT)


def _silu(x):
    h = 0.5 * x
    return h * jnp.tanh(h) + h


def _dot(a, b):
    return jnp.dot(a, b, preferred_element_type=F32)


def _dot_nt(a, b):
    return lax.dot_general(a, b, (((1,), (1,)), ((), ())), preferred_element_type=F32)


def _mod_kernel(c_ref, w_ref, b_ref, o_ref):
    s = _silu(c_ref[...])
    o_ref[...] = _dot(s.astype(BF16), w_ref[...].astype(BF16)) + b_ref[...]


def _modulation(cc, w, b):
    rows, d = cc.shape
    n = w.shape[1]
    tn = 1024
    return pl.pallas_call(
        _mod_kernel,
        grid=(n // tn,),
        in_specs=[pl.BlockSpec((rows, d), lambda j: (0, 0)),
                  pl.BlockSpec((d, tn), lambda j: (0, j)),
                  pl.BlockSpec((1, tn), lambda j: (0, j))],
        out_specs=pl.BlockSpec((rows, tn), lambda j: (0, j)),
        out_shape=jax.ShapeDtypeStruct((rows, n), F32),
        compiler_params=_params("arbitrary"),
        name="modulation",
    )(cc, w, b.reshape(1, n))


def _rope_block(x, cos, sin_signed, low16):
    partner = jnp.where(low16, pltpu.roll(x, LANES - 16, axis=1), pltpu.roll(x, 16, axis=1))
    return x * cos + partner * sin_signed


def _cast_weights_once(w_ref, wbf_ref):
    @pl.when((pl.program_id(0) == 0) & (pl.program_id(1) == 0))
    def _():
        for off in range(0, w_ref.shape[1], CAST_COLS):
            wbf_ref[:, off:off + CAST_COLS] = w_ref[:, off:off + CAST_COLS].astype(BF16)


def _norm_project(x, mod_ref, g_ref, wbf_ref, rope_refs, out_refs, outs):
    tm, d = x.shape
    y = x * lax.rsqrt(jnp.mean(x * x, axis=-1, keepdims=True) + EPS) * g_ref[...]
    mod = mod_ref[0]
    h = (y * (1.0 + mod[:, d:2 * d]) + mod[:, :d]).astype(BF16)
    if rope_refs:
        cos = rope_refs[0][...]
        sin = rope_refs[1][...]
        lane = lax.broadcasted_iota(jnp.int32, (1, LANES), 1)
        low16 = (lane & 16) == 0
    off = 0
    for o_ref, (width, roped, mul, packed) in zip(out_refs, outs):
        acc = _dot(h, wbf_ref[:, off:off + width])
        if roped:
            acc = jnp.concatenate(
                [_rope_block(acc[:, j:j + LANES], cos, sin, low16) for j in range(0, width, LANES)], axis=1)
        if mul is not None:
            acc = acc * mul
        if packed:
            n_heads = width // LANES
            t = o_ref.shape[1] // n_heads
            for bi in range(o_ref.shape[0]):
                for hh in range(n_heads):
                    o_ref[bi, pl.ds(hh, t, stride=n_heads), :] = (
                        acc[bi * t:(bi + 1) * t, hh * LANES:(hh + 1) * LANES].astype(o_ref.dtype))
        else:
            o_ref[...] = acc.astype(o_ref.dtype).reshape(o_ref.shape)
        off += width


def _token_rows(ref):
    return ref[...].reshape(-1, ref.shape[-1])


def _gated_residual(o_refs, wbf_ref, x_ref, mod_ref):
    d = x_ref.shape[-1]
    acc, row = None, 0
    for o_ref in o_refs:
        k = o_ref.shape[-1]
        part = _dot(_token_rows(o_ref), wbf_ref[row:row + k, :])
        acc = part if acc is None else acc + part
        row += k
    return _token_rows(x_ref) + mod_ref[0][:, 2 * d:] * acc


def _stage_kernel(*refs, n_o, final, outs, rope):
    refs = list(refs)
    take = lambda n: [refs.pop(0) for _ in range(n)]
    if n_o:
        o_refs = take(n_o)
        wout_ref, x_ref, mod_prev_ref = take(3)
        fin_ref = take(1)[0] if final else None
    else:
        x_ref, = take(1)
    if outs:
        mod_ref, g_ref, win_ref = take(3)
        rope_refs = take(2) if rope else []
    xout_ref = take(1)[0] if n_o else None
    out_refs = take(len(outs))
    if n_o:
        wout_bf = take(1)[0]
        _cast_weights_once(wout_ref, wout_bf)
        x = _gated_residual(o_refs, wout_bf, x_ref, mod_prev_ref)
        if final:
            x_new = x * lax.rsqrt(jnp.mean(x * x, axis=-1, keepdims=True) + EPS) * fin_ref[...]
        else:
            x_new = x
        xout_ref[...] = x_new.reshape(xout_ref.shape)
    else:
        x = _token_rows(x_ref)
    if outs:
        win_bf = take(1)[0]
        _cast_weights_once(win_ref, win_bf)
        _norm_project(x, mod_ref, g_ref, win_bf, rope_refs, out_refs, outs)


def _stage(x, prev=None, nxt=None, final_g=None, tm=256):
    b, t, d = x.shape
    shared_mod = all(m.shape[0] == 1 for m in ([prev[2]] if prev else []) + ([nxt[0]] if nxt else []))
    bb = max(1, tm // t) if shared_mod else 1
    tm = min(tm, t)
    tile = lambda width: pl.BlockSpec((bb, tm, width), lambda i, j: (i, j, 0))
    mod_spec = lambda mod: pl.BlockSpec(
        (1, 1, 3 * d), (lambda i, j: (i, 0, 0)) if mod.shape[0] > 1 else (lambda i, j: (0, 0, 0)))
    resident = lambda w: pl.BlockSpec(w.shape, lambda i, j: (0, 0), pipeline_mode=pl.Buffered(1))
    row_spec = pl.BlockSpec((1, d), lambda i, j: (0, 0))
    in_specs, args, out_specs, out_shape, scratch = [], [], [], [], []
    n_o, outs, rope = 0, (), False
    if prev is not None:
        o_list, w_out, mod_prev = prev
        n_o = len(o_list)
        in_specs += [tile(o.shape[-1]) for o in o_list] + [resident(w_out), tile(d), mod_spec(mod_prev)]
        args += list(o_list) + [w_out, x, mod_prev]
        if final_g is not None:
            in_specs.append(row_spec)
            args.append(final_g.reshape(1, d))
        out_specs.append(tile(d))
        out_shape.append(jax.ShapeDtypeStruct((b, t, d), F32))
        scratch.append(pltpu.VMEM(w_out.shape, BF16))
    else:
        in_specs.append(tile(d))
        args.append(x)
    if nxt is not None:
        mod_next, norm_g, w_in, out_list, rope_tables = nxt
        in_specs += [mod_spec(mod_next), row_spec, resident(w_in)]
        args += [mod_next, norm_g.reshape(1, d), w_in]
        rope = rope_tables is not None
        if rope:
            in_specs += [pl.BlockSpec((tm, LANES), lambda i, j: (j, 0))] * 2
            args += list(rope_tables)
        for width, dt, _, _, packed in out_list:
            if packed:
                heads = width // LANES
                out_specs.append(pl.BlockSpec((bb, tm * heads, LANES), lambda i, j: (i, j, 0)))
                out_shape.append(jax.ShapeDtypeStruct((b, t * heads, LANES), dt))
            else:
                out_specs.append(tile(width))
                out_shape.append(jax.ShapeDtypeStruct((b, t, width), dt))
        outs = tuple((w_, r, m, p) for w_, _, r, m, p in out_list)
        scratch.append(pltpu.VMEM(w_in.shape, BF16))
    res = pl.pallas_call(
        functools.partial(_stage_kernel, n_o=n_o, final=final_g is not None, outs=outs, rope=rope),
        grid=(b // bb, t // tm),
        in_specs=in_specs,
        out_specs=out_specs,
        out_shape=out_shape,
        scratch_shapes=scratch,
        compiler_params=_params("arbitrary", "arbitrary"),
        name="stage",
    )(*args)
    return (res[0], list(res[1:])) if n_o else (None, list(res))


def _attend(q_blocks, kv_blocks, s_ref, score_fn=None):
    n_rb = len(q_blocks)

    def tile_of(rb, c):
        slot = rb % SCORE_SLOTS
        return (slice(slot * ROW_BLK, (slot + 1) * ROW_BLK), slice(c * KEY_TILE, (c + 1) * KEY_TILE))

    def score_tile(rb, c):
        s_ref[tile_of(rb, c)] = _dot_nt(q_blocks[rb], kv_blocks[rb][0][c])

    def max_tile(rb, c, m_acc):
        s = s_ref[tile_of(rb, c)]
        adjusted = None if score_fn is None else score_fn(rb, c, s)
        if adjusted is not None:
            s = adjusted
            s_ref[tile_of(rb, c)] = s
        mt = jnp.maximum(s[:, :LANES], s[:, LANES:])
        return mt if m_acc is None else jnp.maximum(m_acc, mt)

    def n_tiles(rb):
        return len(kv_blocks[rb][0]) if 0 <= rb < n_rb else 0

    outs = []
    row_max = {}
    for step in range(n_rb + 2):
        rb_score, rb_max, rb_exp = step, step - 1, step - 2
        m_acc, acc = None, None
        for c in range(max(n_tiles(rb_score), n_tiles(rb_max), n_tiles(rb_exp))):
            if c < n_tiles(rb_score):
                score_tile(rb_score, c)
            if c < n_tiles(rb_max):
                m_acc = max_tile(rb_max, c, m_acc)
            if c < n_tiles(rb_exp):
                e = jnp.exp2(s_ref[tile_of(rb_exp, c)] - row_max[rb_exp])
                pv = _dot(e.astype(BF16), kv_blocks[rb_exp][1][c])
                acc = pv if acc is None else acc + pv
        if m_acc is not None:
            row_max[rb_max] = m_acc.max(axis=-1, keepdims=True)
        if acc is not None:
            outs.append(acc)
    return outs


def _lane_low_half():
    return lax.broadcasted_iota(jnp.int32, (1, LANES), 1) < (LANES // 2)


def _values_and_ones(v_tiles):
    ones = jnp.ones((KEY_TILE, LANES), BF16)
    return [jnp.concatenate([v, ones], axis=1) for v in v_tiles]


def _normalise_pair(acc_lo, acc_hi, low):
    return jnp.where(low, acc_lo[:, :LANES] / acc_lo[:, LANES:], acc_hi[:, :LANES] / acc_hi[:, LANES:])


def _split_lane_halves(q, low):
    zero = jnp.zeros_like(q)
    return jnp.where(low, q, zero), jnp.where(low, zero, q)


def _tiles(ref, n_tiles, j, bi=0):
    return [ref[bi, c * KEY_TILE:(c + 1) * KEY_TILE, j:j + LANES].astype(BF16) for c in range(n_tiles)]


def _ctx_attn_kernel(q_ref, k_ref, v_ref, g_ref, o_ref, s_ref):
    low = _lane_low_half()
    nb, tq, w = q_ref.shape
    n_tiles = k_ref.shape[1] // KEY_TILE
    tiles = [(bi, j, r) for bi in range(nb) for j in range(0, w, LANES) for r in range(0, tq, ROW_BLK)]
    q_blocks, kv_blocks, kv = [], [], {}
    for bi, j, r in tiles:
        if (bi, j) not in kv:
            kv[bi, j] = (_tiles(k_ref, n_tiles, j, bi), _values_and_ones(_tiles(v_ref, n_tiles, j, bi)))
        q_blocks += list(_split_lane_halves(q_ref[bi, r:r + ROW_BLK, j:j + LANES], low))
        kv_blocks += [kv[bi, j]] * 2
    res = _attend(q_blocks, kv_blocks, s_ref)
    for i, (bi, j, r) in enumerate(tiles):
        o = _normalise_pair(res[2 * i], res[2 * i + 1], low)
        gate = _silu(g_ref[bi, r:r + ROW_BLK, j:j + LANES].astype(F32))
        o_ref[bi, r:r + ROW_BLK, j:j + LANES] = (o * gate).astype(o_ref.dtype)


def _ctx_attn(q, k, v, g):
    b, t, w = q.shape
    nb = CTX_BATCH_PER_STEP
    spec = pl.BlockSpec((nb, t, w), lambda i: (i, 0, 0))
    return pl.pallas_call(
        _ctx_attn_kernel,
        grid=(b // nb,),
        in_specs=[spec] * 4,
        out_specs=spec,
        out_shape=jax.ShapeDtypeStruct((b, t, w), BF16),
        scratch_shapes=[pltpu.VMEM((SCORE_SLOTS * ROW_BLK, k.shape[1]), F32)],
        compiler_params=_params("parallel"),
        name="ctx_attn",
    )(q, k, v, g)


def _na_pair_tiles(rpb_ref, head, pt_ref):
    right = lax.broadcasted_iota(jnp.int32, (GRID_W, LANES), 1) >= GRID_W
    zero_row = jnp.zeros((1, LANES), F32)

    def toeplitz(row, lane0):
        shift = (lane0 - (NA_COLS - 1)) % LANES
        return pltpu.roll(jnp.broadcast_to(row, (GRID_W, LANES)), shift, axis=1, stride=1, stride_axis=0)

    for p in range(NA_PAIR_TILES):
        left_row = rpb_ref[head, p - 1:p, :] if p >= 1 else zero_row
        right_row = rpb_ref[head, p:p + 1, :]
        pt_ref[head, p] = jnp.where(right, toeplitz(right_row, GRID_W), toeplitz(left_row, 0)) * LOG2E


def _na_row_start(r, n_rows):
    kr = min(NA_ROWS_MAX, n_rows)
    return min(max(r - kr // 2, 0), n_rows - kr)


def _na_window(grp, n_rows):
    kr = min(NA_ROWS_MAX, n_rows)
    starts = [_na_row_start(grp * NA_Q_ROWS + i, n_rows) for i in range(NA_Q_ROWS)]
    rows_per_tile = KEY_TILE // GRID_W
    need = -(-(max(starts) + kr - min(starts)) // rows_per_tile) * rows_per_tile
    return min(min(starts), n_rows - need), need


def _na_kernel(q_ref, k_ref, v_ref, ck_ref, cv_ref, rpb_ref, g_ref, o_ref, pt_ref, badd_ref, s_ref, *, n_rows):
    kr = min(NA_ROWS_MAX, n_rows)
    n_grp = n_rows // NA_Q_ROWS
    n_pairs = q_ref.shape[2] // LANES
    low = _lane_low_half()

    @pl.when(pl.program_id(1) == 0)
    def _build_bias():
        for head in range(2 * n_pairs):
            _na_pair_tiles(rpb_ref, head, pt_ref)
        qcol = lax.broadcasted_iota(jnp.int32, (GRID_W, LANES), 0)
        lane = lax.broadcasted_iota(jnp.int32, (GRID_W, LANES), 1)
        right = lane >= GRID_W
        kcol = lane & (GRID_W - 1)
        col_start = jnp.clip(qcol - NA_COLS // 2, 0, GRID_W - NA_COLS)
        col_ok = (kcol >= col_start) & (kcol < col_start + NA_COLS)
        for grp in range(n_grp):
            krow0, key_rows = _na_window(grp, n_rows)
            for i in range(NA_Q_ROWS):
                r = grp * NA_Q_ROWS + i
                row_start = _na_row_start(r, n_rows)
                for m in range(key_rows // 2):
                    key_row = krow0 + 2 * m
                    ok_l = row_start <= key_row < row_start + kr
                    ok_r = row_start <= key_row + 1 < row_start + kr
                    rows = slice(i * GRID_W, (i + 1) * GRID_W)
                    cols = slice(m * LANES, (m + 1) * LANES)
                    if not (ok_l or ok_r):
                        for head in range(2 * n_pairs):
                            badd_ref[head * n_grp + grp, rows, cols] = jnp.full((GRID_W, LANES), -jnp.inf, F32)
                        continue
                    ok = col_ok
                    if not ok_r:
                        ok = ok & jnp.logical_not(right)
                    if not ok_l:
                        ok = ok & right
                    p = key_row - r + NA_ROWS_MAX
                    for head in range(2 * n_pairs):
                        badd_ref[head * n_grp + grp, rows, cols] = jnp.where(ok, pt_ref[head, p], -jnp.inf)

    n_ctx = ck_ref.shape[1] // KEY_TILE
    q_blocks, kv_blocks, blocks = [], [], []
    for pair in range(n_pairs):
        lanes = slice(pair * LANES, (pair + 1) * LANES)
        ctx_k = _tiles(ck_ref, n_ctx, pair * LANES)
        ctx_v = _tiles(cv_ref, n_ctx, pair * LANES)
        for grp in range(n_grp):
            krow0, key_rows = _na_window(grp, n_rows)
            k0 = krow0 * GRID_W
            win = [slice(k0 + c * KEY_TILE, k0 + (c + 1) * KEY_TILE) for c in range(key_rows * GRID_W // KEY_TILE)]
            kv = ([k_ref[0, w, lanes] for w in win] + ctx_k,
                  _values_and_ones([v_ref[0, w, lanes] for w in win] + ctx_v))
            q_blocks += list(_split_lane_halves(q_ref[0, grp * ROW_BLK:(grp + 1) * ROW_BLK, lanes], low))
            kv_blocks += [kv, kv]
            blocks += [(2 * pair, grp, len(win)), (2 * pair + 1, grp, len(win))]

    def masked_bias(rb, c, s):
        head, grp, n_win = blocks[rb]
        if c >= n_win:
            return None
        cols = slice(c * KEY_TILE, (c + 1) * KEY_TILE)
        return s + badd_ref[head * n_grp + grp, :, cols]

    res = _attend(q_blocks, kv_blocks, s_ref, masked_bias)
    for pair in range(n_pairs):
        lanes = slice(pair * LANES, (pair + 1) * LANES)
        for grp in range(n_grp):
            rb = 2 * (pair * n_grp + grp)
            o = _normalise_pair(res[rb], res[rb + 1], low)
            rows = slice(grp * ROW_BLK, (grp + 1) * ROW_BLK)
            o_ref[0, rows, lanes] = (o * _silu(g_ref[0, rows, lanes].astype(F32))).astype(o_ref.dtype)


def _na_attn(q, k, v, ck, cv, rpb, g):
    heads, rpb_rows, rpb_cols = rpb.shape
    rpb = jnp.pad(rpb, ((0, 0), (0, NA_PAIR_TILES - rpb_rows), (0, LANES - rpb_cols)))
    b, t, w = q.shape
    c = ck.shape[1]
    n_rows = t // GRID_W
    assert NA_Q_ROWS * GRID_W == ROW_BLK and n_rows % NA_Q_ROWS == 0 and n_rows >= NA_K_ROWS
    n_grp = n_rows // NA_Q_ROWS
    n_win = NA_K_ROWS * GRID_W
    wb = NA_PAIRS_PER_STEP * LANES
    hb = 2 * NA_PAIRS_PER_STEP
    tspec = pl.BlockSpec((1, t, wb), lambda hp, bi: (bi, 0, hp))
    cspec = pl.BlockSpec((1, c, wb), lambda hp, bi: (bi, 0, hp))
    ptspec = pl.BlockSpec((hb, NA_PAIR_TILES, LANES), lambda hp, bi: (hp, 0, 0))
    return pl.pallas_call(
        functools.partial(_na_kernel, n_rows=n_rows),
        grid=(w // wb, b),
        in_specs=[tspec, tspec, tspec, cspec, cspec, ptspec, tspec],
        out_specs=tspec,
        out_shape=jax.ShapeDtypeStruct((b, t, w), BF16),
        scratch_shapes=[pltpu.VMEM((hb, NA_PAIR_TILES, GRID_W, LANES), F32),
                        pltpu.VMEM((hb * n_grp, ROW_BLK, n_win), F32),
                        pltpu.VMEM((SCORE_SLOTS * ROW_BLK, n_win + c), F32)],
        compiler_params=_params("parallel", "arbitrary"),
        name="na_attn",
    )(q, k, v, ck, cv, rpb, g)


def _softplus(x):
    return jnp.maximum(x, 0.0) + jnp.log(1.0 + jnp.exp(-jnp.abs(x)))


def _lru_kernel(x_ref, gb_ref, h0f_ref, h0b_ref, cw_ref, cb_ref, wg_ref, lam_ref,
                o_ref, ff_ref, fb_ref, xt_s, af_s, uf_s, ab_s, ub_s, hf_s, hb_s):
    nb, t, _ = x_ref.shape
    pad = CONV_PAD_L * nb
    cw = 0.5 * cw_ref[...]
    cb = 0.5 * cb_ref[...]
    wg = wg_ref[0]
    kk = (-0.5 * LRU_C * LOG2E) * _softplus(-lam_ref[...])

    xt_s[0:pad, :] = jnp.zeros((pad, LANES), F32)
    xt_s[pad + t * nb:pad + (t + CONV_PAD_R) * nb, :] = jnp.zeros((CONV_PAD_R * nb, LANES), F32)

    def to_time_major(b, carry):
        xt_s[pl.ds(pad + b, t, stride=nb), :] = x_ref[b]
        return carry

    lax.fori_loop(0, nb, to_time_major, 0)

    rows_per_chunk = LRU_CHUNK * nb
    lane = lax.broadcasted_iota(jnp.int32, (rows_per_chunk, LANES), 1)
    bias_cols = jnp.where(lane < 2, 1.0, 0.0).astype(BF16)

    def gates(c, carry):
        r0 = pl.multiple_of(c * rows_per_chunk, rows_per_chunk)

        def tap(i):
            return xt_s[pl.ds(r0 + i * nb, rows_per_chunk), :]

        xh = cw[0:1] * tap(0) + cw[1:2] * tap(1) + cw[2:3] * tap(2) + cw[3:4] * tap(3) + cb
        th = jnp.tanh(_dot(jnp.concatenate([xh.astype(BF16), bias_cols], axis=1), wg))
        rows = pl.ds(r0, rows_per_chunk)
        for d, (a_s, u_s) in enumerate(((af_s, uf_s), (ab_s, ub_s))):
            t_a = th[:, (2 * d) * LANES:(2 * d + 1) * LANES]
            t_x = th[:, (2 * d + 1) * LANES:(2 * d + 2) * LANES]
            a = jnp.exp2(kk[d:d + 1] * t_a + kk[d:d + 1])
            y = 1.0 - a * a
            sq = y * lax.rsqrt(jnp.maximum(y, 1e-30))
            a_s[rows, :] = a
            u_s[rows, :] = sq * (xh * t_x + xh)
        return carry

    lax.fori_loop(0, t // LRU_CHUNK, gates, 0)

    ts = t // LRU_SEGMENTS
    refs = ((af_s, uf_s), (ab_s, ub_s))

    slab = nb * LRU_UNROLL

    def slab_rows(d, j, i):
        base = (j * ts + i * LRU_UNROLL) * nb
        return pl.ds(pl.multiple_of(base if d == 0 else t * nb - slab - base, slab), slab)

    def step_of(d, block, k):
        kk_ = k if d == 0 else LRU_UNROLL - 1 - k
        return block[kk_ * nb:(kk_ + 1) * nb]

    def compose(i, carry):
        state = list(carry)
        for d in range(2):
            for j in range(LRU_SEGMENTS - 1):
                n = d * (LRU_SEGMENTS - 1) + j
                a_blk = refs[d][0][slab_rows(d, j, i), :]
                u_blk = refs[d][1][slab_rows(d, j, i), :]
                g, big_a = state[n]
                for k in range(LRU_UNROLL):
                    a = step_of(d, a_blk, k)
                    g, big_a = a * g + step_of(d, u_blk, k), a * big_a
                state[n] = (g, big_a)
        return tuple(state)

    ones = jnp.ones((nb, LANES), F32)
    h0 = (h0f_ref[...], h0b_ref[...])
    init = tuple((h0[d] if j == 0 else jnp.zeros((nb, LANES), F32), ones)
                 for d in range(2) for j in range(LRU_SEGMENTS - 1))
    ends = lax.fori_loop(0, ts // LRU_UNROLL, compose, init)
    starts = []
    for d in range(2):
        h = h0[d]
        for j in range(LRU_SEGMENTS):
            starts.append(h)
            if j < LRU_SEGMENTS - 1:
                g, big_a = ends[d * (LRU_SEGMENTS - 1) + j]
                h = g if j == 0 else g + big_a * h

    def rerun(i, carry):
        state = list(carry)
        for d, h_s in enumerate((hf_s, hb_s)):
            for j in range(LRU_SEGMENTS):
                n = d * LRU_SEGMENTS + j
                a_blk = refs[d][0][slab_rows(d, j, i), :]
                u_blk = refs[d][1][slab_rows(d, j, i), :]
                hs = []
                for k in range(LRU_UNROLL):
                    state[n] = step_of(d, a_blk, k) * state[n] + step_of(d, u_blk, k)
                    hs.append(state[n])
                h_s[slab_rows(d, j, i), :] = jnp.concatenate(hs if d == 0 else hs[::-1], axis=0)
        return tuple(state)

    final = lax.fori_loop(0, ts // LRU_UNROLL, rerun, tuple(starts))
    ff_ref[...] = final[LRU_SEGMENTS - 1]
    fb_ref[...] = final[2 * LRU_SEGMENTS - 1]

    def add_directions(c, carry):
        rows = pl.ds(pl.multiple_of(c * rows_per_chunk, rows_per_chunk), rows_per_chunk)
        hf_s[rows, :] = hf_s[rows, :] + hb_s[rows, :]
        return carry

    lax.fori_loop(0, t // LRU_CHUNK, add_directions, 0)

    def emit(b, carry):
        hs = hf_s[pl.ds(b, t, stride=nb), :]
        o_ref[b] = (hs * _silu(gb_ref[b].astype(F32))).astype(o_ref.dtype)
        return carry

    lax.fori_loop(0, nb, emit, 0)


def _lru_branch(xb, gb, h0f, h0b, conv_w, conv_b, wg, lam):
    b, t, w = xb.shape
    nb = SUBLANES
    assert t % (LRU_SEGMENTS * LRU_UNROLL) == 0 and t % LRU_CHUNK == 0
    assert conv_w.shape[0] == CONV_PAD_L + 1 + CONV_PAD_R
    xspec = pl.BlockSpec((nb, t, LANES), lambda i, c: (i, 0, c))
    hspec = pl.BlockSpec((nb, LANES), lambda i, c: (i, c))
    scratch = [pltpu.VMEM(((t + CONV_PAD_L + CONV_PAD_R) * nb, LANES), F32)]
    scratch += [pltpu.VMEM((t * nb, LANES), F32) for _ in range(6)]
    return pl.pallas_call(
        _lru_kernel,
        grid=(b // nb, w // LANES),
        in_specs=[xspec, xspec, hspec, hspec,
                  pl.BlockSpec((conv_w.shape[0], LANES), lambda i, c: (0, c)),
                  pl.BlockSpec((1, LANES), lambda i, c: (0, c)),
                  pl.BlockSpec((1, 2 * LANES, 4 * LANES), lambda i, c: (c, 0, 0)),
                  pl.BlockSpec((2, LANES), lambda i, c: (0, c))],
        out_specs=[xspec, hspec, hspec],
        out_shape=[jax.ShapeDtypeStruct((b, t, w), BF16),
                   jax.ShapeDtypeStruct((b, w), F32),
                   jax.ShapeDtypeStruct((b, w), F32)],
        scratch_shapes=scratch,
        compiler_params=_params("parallel", "parallel"),
        name="lru_branch",
    )(xb, gb, h0f, h0b, conv_w, conv_b.reshape(1, w), wg, lam)


def _lru_gate_weights(wa, ba, wx, bx):
    def tiles(w):
        w2 = w.reshape(LRU_BLOCKS // 2, 2, LRU_BLOCK, LRU_BLOCK)
        z = jnp.zeros_like(w2[:, 0])
        return jnp.concatenate([jnp.concatenate([w2[:, 0], z], axis=2),
                                jnp.concatenate([z, w2[:, 1]], axis=2)], axis=1)
    wg = jnp.concatenate([tiles(wa[0]), tiles(wx[0]), tiles(wa[1]), tiles(wx[1])], axis=2)
    n_tiles = LRU_WIDTH // LANES
    bg = jnp.stack([ba[0], bx[0], ba[1], bx[1]], axis=0).reshape(4, n_tiles, LANES)
    bg = 0.5 * jnp.transpose(bg, (1, 0, 2)).reshape(n_tiles, 1, 4 * LANES)
    hi = bg.astype(BF16)
    lo = (bg - hi.astype(F32)).astype(BF16)
    rest = jnp.zeros((n_tiles, LANES - 2, 4 * LANES), BF16)
    return jnp.concatenate([wg.astype(BF16), hi, lo, rest], axis=1)


def _packed_tiles(ref, n_heads, head, bi=0):
    n_tiles = ref.shape[1] // (n_heads * KEY_TILE)
    return [ref[bi, pl.ds(c * KEY_TILE * n_heads + head, KEY_TILE, stride=n_heads), :].astype(BF16)
            for c in range(n_tiles)]


def _diff_kernel(*refs, packed, n_heads, lam_init):
    n_kv = len(packed)
    q_ref = refs[0]
    kv_refs = refs[1:1 + 2 * n_kv]
    g_ref, lamp_ref, subg_ref, o_ref, s_ref = refs[1 + 2 * n_kv:]
    low = _lane_low_half()
    tq = q_ref.shape[1]
    first_head = pl.program_id(1) * (q_ref.shape[2] // LANES)
    lp = lamp_ref[...]
    lam = (jnp.exp(jnp.sum(lp[0:1] * lp[1:2], axis=-1, keepdims=True))
           - jnp.exp(jnp.sum(lp[2:3] * lp[3:4], axis=-1, keepdims=True)) + lam_init)
    sub_gain = subg_ref[...] * (1.0 - lam_init)
    nb = q_ref.shape[0]
    tiles = [(bi, j) for bi in range(nb) for j in range(0, q_ref.shape[2], LANES)]
    q_blocks, kv_blocks = [], []
    for bi, j in tiles:
        k_tiles, v_tiles = [], []
        for i in range(n_kv):
            if packed[i]:
                head = first_head + j // LANES
                k_tiles += _packed_tiles(kv_refs[2 * i], n_heads, head, bi)
                v_tiles += _packed_tiles(kv_refs[2 * i + 1], n_heads, head, bi)
            else:
                n_tiles = kv_refs[2 * i].shape[1] // KEY_TILE
                k_tiles += _tiles(kv_refs[2 * i], n_tiles, j, bi)
                v_tiles += _tiles(kv_refs[2 * i + 1], n_tiles, j, bi)
        v_tiles = _values_and_ones(v_tiles)
        for r in range(0, tq, ROW_BLK):
            q_blocks += list(_split_lane_halves(q_ref[bi, r:r + ROW_BLK, j:j + LANES], low))
            kv_blocks += [(k_tiles, v_tiles)] * 2
    res = _attend(q_blocks, kv_blocks, s_ref)
    i = 0
    for bi, j in tiles:
        for r in range(0, tq, ROW_BLK):
            acc1, acc2 = res[i], res[i + 1]
            i += 2
            o = acc1[:, :LANES] / acc1[:, LANES:] - acc2[:, :LANES] * (lam / acc2[:, LANES:])
            o = o * lax.rsqrt(jnp.mean(o * o, axis=-1, keepdims=True) + EPS) * sub_gain
            gate = _silu(g_ref[bi, r:r + ROW_BLK, j:j + LANES].astype(F32))
            o_ref[bi, r:r + ROW_BLK, j:j + LANES] = (o * gate).astype(o_ref.dtype)


def _diff_attn(q, kv_list, g, lam_params, sub_g, lam_init, tq, heads_per_step, batch_per_step=1):
    b, t, w = q.shape
    wb = heads_per_step * LANES
    nb = batch_per_step
    qspec = pl.BlockSpec((nb, tq, wb), lambda bi, h, qi: (bi, qi, h))
    n_heads = w // LANES
    in_specs = [qspec]
    args = [q]
    n_keys = 0
    for k, v, packed in kv_list:
        if packed:
            spec = pl.BlockSpec((nb, k.shape[1], LANES), lambda bi, h, qi: (bi, 0, 0))
            n_keys += k.shape[1] // n_heads
        else:
            spec = pl.BlockSpec((nb, k.shape[1], wb), lambda bi, h, qi: (bi, 0, h))
            n_keys += k.shape[1]
        in_specs += [spec, spec]
        args += [k, v]
    in_specs += [qspec,
                 pl.BlockSpec(lam_params.shape, lambda bi, h, qi: (0, 0)),
                 pl.BlockSpec((1, LANES), lambda bi, h, qi: (0, 0))]
    args += [g, lam_params, sub_g.reshape(1, LANES)]
    return pl.pallas_call(
        functools.partial(_diff_kernel, packed=tuple(p for _, _, p in kv_list), n_heads=n_heads,
                          lam_init=lam_init),
        grid=(b // nb, w // wb, t // tq),
        in_specs=in_specs,
        out_specs=qspec,
        out_shape=jax.ShapeDtypeStruct((b, t, w), BF16),
        scratch_shapes=[pltpu.VMEM((SCORE_SLOTS * ROW_BLK, n_keys), F32)],
        compiler_params=_params("parallel", "parallel", "parallel"),
        name="diff_attn",
    )(*args)


def _rope_tables(t_len):
    m = DIFF_HEAD_DIM // 4
    t = jnp.arange(t_len)
    rows = (t // GRID_W).astype(F32)
    cols = (t % GRID_W).astype(F32)
    inv = ROPE_BASE ** (-jnp.arange(m, dtype=F32) / m)
    ang_r = rows[:, None] * inv[None, :]
    ang_c = cols[:, None] * inv[None, :]
    cos = jnp.concatenate([jnp.cos(ang_r)] * 2 + [jnp.cos(ang_c)] * 2, axis=1)
    sin = jnp.concatenate([-jnp.sin(ang_r), jnp.sin(ang_r), -jnp.sin(ang_c), jnp.sin(ang_c)], axis=1)
    reps = LANES // DIFF_HEAD_DIM
    return jnp.tile(cos, (1, reps)), jnp.tile(sin, (1, reps))


def kernel(x_prompt, x_sample, c, cache_na_k, cache_na_v, state_lru, cache_diff_k, cache_diff_v, c_ctx,
           e_norm, e_ada_w, e_ada_b, e_w_in, e_rpb, e_conv_w, e_conv_b, e_lru_wa, e_lru_ba, e_lru_wx, e_lru_bx,
           e_lru_lam, e_w_out, o_norm, o_ada_w, o_ada_b, o_w_in, o_lq1, o_lk1, o_lq2, o_lk2, o_sub_g, o_w_out,
           final_norm):
    bp, tp, d = x_prompt.shape
    bs, ts, _ = x_sample.shape
    past = cache_na_k.shape[2]
    pad_rows = 2 * SUBLANES - bs - 1
    cc = jnp.concatenate([c, c_ctx[None, :], jnp.zeros((pad_rows, d), F32)], axis=0)
    nw, lw, dw = NA_WIDTH, LRU_WIDTH, DIFF_WIDTH
    rope_tables = _rope_tables(ts) if DEPTH > 1 else None

    def layer_params(i):
        j = i // 2
        if i % 2 == 0:
            return e_ada_w[j], e_ada_b[j], e_norm[j], e_w_in[j], e_w_out[j]
        return o_ada_w[j], o_ada_b[j], o_norm[j], o_w_in[j], o_w_out[j]

    mods = []
    for i in range(DEPTH):
        ada_w, ada_b = layer_params(i)[:2]
        m = _modulation(cc, ada_w, ada_b)
        mods.append((m[:bs].reshape(bs, 1, 3 * d), m[bs:bs + 1].reshape(1, 1, 3 * d)))

    def projection(i, latent):
        norm_g, w_in = layer_params(i)[2:4]
        mod = mods[i][0 if latent else 1]
        kv_dtype = BF16 if latent else F32
        if i % 2 == 0:
            qmul = NA_HEAD_DIM ** -0.5 * LOG2E
            outs = [(nw, BF16, False, qmul, False), (nw, kv_dtype, False, None, False),
                    (nw, kv_dtype, False, None, False), (nw, BF16, False, None, False),
                    (lw, F32, False, None, False), (lw, BF16, False, None, False)]
            return mod, norm_g, w_in, outs, None
        qmul = DIFF_HEAD_DIM ** -0.5 * LOG2E
        outs = [(dw, BF16, latent, qmul, False), (dw, kv_dtype, latent, None, not latent),
                (dw, kv_dtype, False, None, not latent), (dw, BF16, False, None, False)]
        return mod, norm_g, w_in, outs, rope_tables if latent else None

    xp, xs = x_prompt, x_sample
    _, proj_p = _stage(xp, nxt=projection(0, False), tm=512)
    _, proj_s = _stage(xs, nxt=projection(0, True), tm=512)
    na_k, na_v, lru_s, df_k, df_v = [], [], [], [], []
    for i in range(DEPTH):
        j = i // 2
        w_out = layer_params(i)[4]
        if i % 2 == 0:
            wg = _lru_gate_weights(e_lru_wa[j], e_lru_ba[j], e_lru_wx[j], e_lru_bx[j])

            q, k, v, ga, xb, gb = proj_p
            oa = _ctx_attn(q, k, v, ga)
            zeros = jnp.zeros((bp, lw), F32)
            ob, ff, fb = _lru_branch(xb, gb, zeros, zeros, e_conv_w[j], e_conv_b[j], wg, e_lru_lam[j])
            mixed_p = [oa, ob]
            na_k.append(k.reshape(bp, tp, NA_HEADS, NA_HEAD_DIM))
            na_v.append(v.reshape(bp, tp, NA_HEADS, NA_HEAD_DIM))
            lru_s.append(jnp.stack([ff, fb], axis=1))

            q, k, v, ga, xb, gb = proj_s
            ck = cache_na_k[:, j].reshape(bs, past, nw)
            cv = cache_na_v[:, j].reshape(bs, past, nw)
            oa = _na_attn(q, k, v, ck, cv, e_rpb[j], ga)
            ob, _, _ = _lru_branch(xb, gb, state_lru[:, j, 0], state_lru[:, j, 1],
                                   e_conv_w[j], e_conv_b[j], wg, e_lru_lam[j])
            mixed_s = [oa, ob]
        else:
            lam_init = 0.8 - 0.6 * math.exp(-0.3 * i)
            lam_params = jnp.stack([o_lq1[j], o_lk1[j], o_lq2[j], o_lk2[j]], axis=0)

            q, k, v, g = proj_p
            mixed_p = [_diff_attn(q, [(k, v, True)], g, lam_params, o_sub_g[j], lam_init, tq=tp,
                                  heads_per_step=DIFF_HEADS, batch_per_step=CTX_BATCH_PER_STEP)]
            df_k.append(k.reshape(bp, tp, DIFF_HEADS, 2 * DIFF_HEAD_DIM))
            df_v.append(v.reshape(bp, tp, DIFF_HEADS, 2 * DIFF_HEAD_DIM))

            q, k, v, g = proj_s
            ck = cache_diff_k[:, j].reshape(bs, past * DIFF_HEADS, 2 * DIFF_HEAD_DIM)
            cv = cache_diff_v[:, j].reshape(bs, past * DIFF_HEADS, 2 * DIFF_HEAD_DIM)
            mixed_s = [_diff_attn(q, [(ck, cv, True), (k, v, False)], g, lam_params, o_sub_g[j], lam_init, tq=ts,
                                  heads_per_step=4)]

        if i == DEPTH - 1:
            xp, _ = _stage(xp, prev=(mixed_p, w_out, mods[i][1]), final_g=final_norm, tm=512)
            xs, _ = _stage(xs, prev=(mixed_s, w_out, mods[i][0]), final_g=final_norm, tm=1024)
        else:
            xp, proj_p = _stage(xp, prev=(mixed_p, w_out, mods[i][1]), nxt=projection(i + 1, False), tm=512)
            xs, proj_s = _stage(xs, prev=(mixed_s, w_out, mods[i][0]), nxt=projection(i + 1, True), tm=512)

    return (xp, xs, jnp.stack(na_k, axis=1), jnp.stack(na_v, axis=1), jnp.stack(lru_s, axis=1),
            jnp.stack(df_k, axis=1), jnp.stack(df_v, axis=1))
```

```python
import functools
import math

import jax
import jax.numpy as jnp
from jax import lax
from jax.experimental import pallas as pl
from jax.experimental.pallas import tpu as pltpu

D_MODEL = 1024
DEPTH = 2
GRID_W = 64
NA_HEADS = 8
NA_HEAD_DIM = 64
NA_WIDTH = NA_HEADS * NA_HEAD_DIM
NA_ROWS_MAX = 8
NA_COLS = 16
RPB_ROWS = 2 * NA_ROWS_MAX - 1
RPB_COLS = 2 * NA_COLS - 1
LRU_WIDTH = 512
LRU_BLOCKS = 8
LRU_BLOCK = LRU_WIDTH // LRU_BLOCKS
LRU_C = 8.0
DIFF_HEADS = 8
DIFF_HEAD_DIM = 64
DIFF_WIDTH = DIFF_HEADS * 2 * DIFF_HEAD_DIM
ROPE_BASE = 10000.0
EPS = 1e-6
LOG2E = math.log2(math.e)

LANES = 128
SUBLANES = 8
MXU_DIM = 256
VMEM_LIMIT = 56 * 1024 * 1024

CAST_COLS = 512
LRU_SEGMENTS = 4
LRU_UNROLL = 8
LRU_CHUNK = 128
CONV_PAD_L = 2
CONV_PAD_R = 1
ROW_BLK = MXU_DIM
KEY_TILE = MXU_DIM
SCORE_SLOTS = 4

NA_Q_ROWS = 4
NA_K_ROWS = 12
NA_PAIR_TILES = 16
NA_PAIRS_PER_STEP = 4
CTX_BATCH_PER_STEP = 4

F32 = jnp.float32
BF16 = jnp.bfloat16


def _params(*semantics):
    return pltpu.CompilerParams(dimension_semantics=semantics, vmem_limit_bytes=VMEM_LIMIT)


def _silu(x):
    h = 0.5 * x
    return h * jnp.tanh(h) + h


def _dot(a, b):
    return jnp.dot(a, b, preferred_element_type=F32)


def _dot_nt(a, b):
    return lax.dot_general(a, b, (((1,), (1,)), ((), ())), preferred_element_type=F32)


def _mod_kernel(c_ref, w_ref, b_ref, o_ref):
    s = _silu(c_ref[...])
    o_ref[...] = _dot(s.astype(BF16), w_ref[...].astype(BF16)) + b_ref[...]


def _modulation(cc, w, b):
    rows, d = cc.shape
    n = w.shape[1]
    tn = 1024
    return pl.pallas_call(
        _mod_kernel,
        grid=(n // tn,),
        in_specs=[pl.BlockSpec((rows, d), lambda j: (0, 0)),
                  pl.BlockSpec((d, tn), lambda j: (0, j)),
                  pl.BlockSpec((1, tn), lambda j: (0, j))],
        out_specs=pl.BlockSpec((rows, tn), lambda j: (0, j)),
        out_shape=jax.ShapeDtypeStruct((rows, n), F32),
        compiler_params=_params("arbitrary"),
        name="modulation",
    )(cc, w, b.reshape(1, n))


def _rope_block(x, cos, sin_signed, low16):
    partner = jnp.where(low16, pltpu.roll(x, LANES - 16, axis=1), pltpu.roll(x, 16, axis=1))
    return x * cos + partner * sin_signed


def _cast_weights_once(w_ref, wbf_ref):
    @pl.when((pl.program_id(0) == 0) & (pl.program_id(1) == 0))
    def _():
        for off in range(0, w_ref.shape[1], CAST_COLS):
            wbf_ref[:, off:off + CAST_COLS] = w_ref[:, off:off + CAST_COLS].astype(BF16)


def _norm_project(x, mod_ref, g_ref, wbf_ref, rope_refs, out_refs, outs):
    tm, d = x.shape
    y = x * lax.rsqrt(jnp.mean(x * x, axis=-1, keepdims=True) + EPS) * g_ref[...]
    mod = mod_ref[0]
    h = (y * (1.0 + mod[:, d:2 * d]) + mod[:, :d]).astype(BF16)
    if rope_refs:
        cos = rope_refs[0][...]
        sin = rope_refs[1][...]
        lane = lax.broadcasted_iota(jnp.int32, (1, LANES), 1)
        low16 = (lane & 16) == 0
    off = 0
    for o_ref, (width, roped, mul, packed) in zip(out_refs, outs):
        acc = _dot(h, wbf_ref[:, off:off + width])
        if roped:
            acc = jnp.concatenate(
                [_rope_block(acc[:, j:j + LANES], cos, sin, low16) for j in range(0, width, LANES)], axis=1)
        if mul is not None:
            acc = acc * mul
        if packed:
            n_heads = width // LANES
            t = o_ref.shape[1] // n_heads
            for bi in range(o_ref.shape[0]):
                for hh in range(n_heads):
                    o_ref[bi, pl.ds(hh, t, stride=n_heads), :] = (
                        acc[bi * t:(bi + 1) * t, hh * LANES:(hh + 1) * LANES].astype(o_ref.dtype))
        else:
            o_ref[...] = acc.astype(o_ref.dtype).reshape(o_ref.shape)
        off += width


def _token_rows(ref):
    return ref[...].reshape(-1, ref.shape[-1])


def _gated_residual(o_refs, wbf_ref, x_ref, mod_ref):
    d = x_ref.shape[-1]
    acc, row = None, 0
    for o_ref in o_refs:
        k = o_ref.shape[-1]
        part = _dot(_token_rows(o_ref), wbf_ref[row:row + k, :])
        acc = part if acc is None else acc + part
        row += k
    return _token_rows(x_ref) + mod_ref[0][:, 2 * d:] * acc


def _stage_kernel(*refs, n_o, final, outs, rope):
    refs = list(refs)
    take = lambda n: [refs.pop(0) for _ in range(n)]
    if n_o:
        o_refs = take(n_o)
        wout_ref, x_ref, mod_prev_ref = take(3)
        fin_ref = take(1)[0] if final else None
    else:
        x_ref, = take(1)
    if outs:
        mod_ref, g_ref, win_ref = take(3)
        rope_refs = take(2) if rope else []
    xout_ref = take(1)[0] if n_o else None
    out_refs = take(len(outs))
    if n_o:
        wout_bf = take(1)[0]
        _cast_weights_once(wout_ref, wout_bf)
        x = _gated_residual(o_refs, wout_bf, x_ref, mod_prev_ref)
        if final:
            x_new = x * lax.rsqrt(jnp.mean(x * x, axis=-1, keepdims=True) + EPS) * fin_ref[...]
        else:
            x_new = x
        xout_ref[...] = x_new.reshape(xout_ref.shape)
    else:
        x = _token_rows(x_ref)
    if outs:
        win_bf = take(1)[0]
        _cast_weights_once(win_ref, win_bf)
        _norm_project(x, mod_ref, g_ref, win_bf, rope_refs, out_refs, outs)


def _stage(x, prev=None, nxt=None, final_g=None, tm=256):
    b, t, d = x.shape
    shared_mod = all(m.shape[0] == 1 for m in ([prev[2]] if prev else []) + ([nxt[0]] if nxt else []))
    bb = max(1, tm // t) if shared_mod else 1
    tm = min(tm, t)
    tile = lambda width: pl.BlockSpec((bb, tm, width), lambda i, j: (i, j, 0))
    mod_spec = lambda mod: pl.BlockSpec(
        (1, 1, 3 * d), (lambda i, j: (i, 0, 0)) if mod.shape[0] > 1 else (lambda i, j: (0, 0, 0)))
    resident = lambda w: pl.BlockSpec(w.shape, lambda i, j: (0, 0), pipeline_mode=pl.Buffered(1))
    row_spec = pl.BlockSpec((1, d), lambda i, j: (0, 0))
    in_specs, args, out_specs, out_shape, scratch = [], [], [], [], []
    n_o, outs, rope = 0, (), False
    if prev is not None:
        o_list, w_out, mod_prev = prev
        n_o = len(o_list)
        in_specs += [tile(o.shape[-1]) for o in o_list] + [resident(w_out), tile(d), mod_spec(mod_prev)]
        args += list(o_list) + [w_out, x, mod_prev]
        if final_g is not None:
            in_specs.append(row_spec)
            args.append(final_g.reshape(1, d))
        out_specs.append(tile(d))
        out_shape.append(jax.ShapeDtypeStruct((b, t, d), F32))
        scratch.append(pltpu.VMEM(w_out.shape, BF16))
    else:
        in_specs.append(tile(d))
        args.append(x)
    if nxt is not None:
        mod_next, norm_g, w_in, out_list, rope_tables = nxt
        in_specs += [mod_spec(mod_next), row_spec, resident(w_in)]
        args += [mod_next, norm_g.reshape(1, d), w_in]
        rope = rope_tables is not None
        if rope:
            in_specs += [pl.BlockSpec((tm, LANES), lambda i, j: (j, 0))] * 2
            args += list(rope_tables)
        for width, dt, _, _, packed in out_list:
            if packed:
                heads = width // LANES
                out_specs.append(pl.BlockSpec((bb, tm * heads, LANES), lambda i, j: (i, j, 0)))
                out_shape.append(jax.ShapeDtypeStruct((b, t * heads, LANES), dt))
            else:
                out_specs.append(tile(width))
                out_shape.append(jax.ShapeDtypeStruct((b, t, width), dt))
        outs = tuple((w_, r, m, p) for w_, _, r, m, p in out_list)
        scratch.append(pltpu.VMEM(w_in.shape, BF16))
    res = pl.pallas_call(
        functools.partial(_stage_kernel, n_o=n_o, final=final_g is not None, outs=outs, rope=rope),
        grid=(b // bb, t // tm),
        in_specs=in_specs,
        out_specs=out_specs,
        out_shape=out_shape,
        scratch_shapes=scratch,
        compiler_params=_params("arbitrary", "arbitrary"),
        name="stage",
    )(*args)
    return (res[0], list(res[1:])) if n_o else (None, list(res))


def _attend(q_blocks, kv_blocks, s_ref, score_fn=None):
    n_rb = len(q_blocks)

    def tile_of(rb, c):
        slot = rb % SCORE_SLOTS
        return (slice(slot * ROW_BLK, (slot + 1) * ROW_BLK), slice(c * KEY_TILE, (c + 1) * KEY_TILE))

    def score_tile(rb, c):
        s_ref[tile_of(rb, c)] = _dot_nt(q_blocks[rb], kv_blocks[rb][0][c])

    def max_tile(rb, c, m_acc):
        s = s_ref[tile_of(rb, c)]
        adjusted = None if score_fn is None else score_fn(rb, c, s)
        if adjusted is not None:
            s = adjusted
            s_ref[tile_of(rb, c)] = s
        mt = jnp.maximum(s[:, :LANES], s[:, LANES:])
        return mt if m_acc is None else jnp.maximum(m_acc, mt)

    def n_tiles(rb):
        return len(kv_blocks[rb][0]) if 0 <= rb < n_rb else 0

    outs = []
    row_max = {}
    for step in range(n_rb + 2):
        rb_score, rb_max, rb_exp = step, step - 1, step - 2
        m_acc, acc = None, None
        for c in range(max(n_tiles(rb_score), n_tiles(rb_max), n_tiles(rb_exp))):
            if c < n_tiles(rb_score):
                score_tile(rb_score, c)
            if c < n_tiles(rb_max):
                m_acc = max_tile(rb_max, c, m_acc)
            if c < n_tiles(rb_exp):
                e = jnp.exp2(s_ref[tile_of(rb_exp, c)] - row_max[rb_exp])
                pv = _dot(e.astype(BF16), kv_blocks[rb_exp][1][c])
                acc = pv if acc is None else acc + pv
        if m_acc is not None:
            row_max[rb_max] = m_acc.max(axis=-1, keepdims=True)
        if acc is not None:
            outs.append(acc)
    return outs


def _lane_low_half():
    return lax.broadcasted_iota(jnp.int32, (1, LANES), 1) < (LANES // 2)


def _values_and_ones(v_tiles):
    ones = jnp.ones((KEY_TILE, LANES), BF16)
    return [jnp.concatenate([v, ones], axis=1) for v in v_tiles]


def _normalise_pair(acc_lo, acc_hi, low):
    return jnp.where(low, acc_lo[:, :LANES] / acc_lo[:, LANES:], acc_hi[:, :LANES] / acc_hi[:, LANES:])


def _split_lane_halves(q, low):
    zero = jnp.zeros_like(q)
    return jnp.where(low, q, zero), jnp.where(low, zero, q)


def _tiles(ref, n_tiles, j, bi=0):
    return [ref[bi, c * KEY_TILE:(c + 1) * KEY_TILE, j:j + LANES].astype(BF16) for c in range(n_tiles)]


def _ctx_attn_kernel(q_ref, k_ref, v_ref, g_ref, o_ref, s_ref):
    low = _lane_low_half()
    nb, tq, w = q_ref.shape
    n_tiles = k_ref.shape[1] // KEY_TILE
    tiles = [(bi, j, r) for bi in range(nb) for j in range(0, w, LANES) for r in range(0, tq, ROW_BLK)]
    q_blocks, kv_blocks, kv = [], [], {}
    for bi, j, r in tiles:
        if (bi, j) not in kv:
            kv[bi, j] = (_tiles(k_ref, n_tiles, j, bi), _values_and_ones(_tiles(v_ref, n_tiles, j, bi)))
        q_blocks += list(_split_lane_halves(q_ref[bi, r:r + ROW_BLK, j:j + LANES], low))
        kv_blocks += [kv[bi, j]] * 2
    res = _attend(q_blocks, kv_blocks, s_ref)
    for i, (bi, j, r) in enumerate(tiles):
        o = _normalise_pair(res[2 * i], res[2 * i + 1], low)
        gate = _silu(g_ref[bi, r:r + ROW_BLK, j:j + LANES].astype(F32))
        o_ref[bi, r:r + ROW_BLK, j:j + LANES] = (o * gate).astype(o_ref.dtype)


def _ctx_attn(q, k, v, g):
    b, t, w = q.shape
    nb = CTX_BATCH_PER_STEP
    spec = pl.BlockSpec((nb, t, w), lambda i: (i, 0, 0))
    return pl.pallas_call(
        _ctx_attn_kernel,
        grid=(b // nb,),
        in_specs=[spec] * 4,
        out_specs=spec,
        out_shape=jax.ShapeDtypeStruct((b, t, w), BF16),
        scratch_shapes=[pltpu.VMEM((SCORE_SLOTS * ROW_BLK, k.shape[1]), F32)],
        compiler_params=_params("parallel"),
        name="ctx_attn",
    )(q, k, v, g)


def _na_pair_tiles(rpb_ref, head, pt_ref):
    right = lax.broadcasted_iota(jnp.int32, (GRID_W, LANES), 1) >= GRID_W
    zero_row = jnp.zeros((1, LANES), F32)

    def toeplitz(row, lane0):
        shift = (lane0 - (NA_COLS - 1)) % LANES
        return pltpu.roll(jnp.broadcast_to(row, (GRID_W, LANES)), shift, axis=1, stride=1, stride_axis=0)

    for p in range(NA_PAIR_TILES):
        left_row = rpb_ref[head, p - 1:p, :] if p >= 1 else zero_row
        right_row = rpb_ref[head, p:p + 1, :]
        pt_ref[head, p] = jnp.where(right, toeplitz(right_row, GRID_W), toeplitz(left_row, 0)) * LOG2E


def _na_row_start(r, n_rows):
    kr = min(NA_ROWS_MAX, n_rows)
    return min(max(r - kr // 2, 0), n_rows - kr)


def _na_window(grp, n_rows):
    kr = min(NA_ROWS_MAX, n_rows)
    starts = [_na_row_start(grp * NA_Q_ROWS + i, n_rows) for i in range(NA_Q_ROWS)]
    rows_per_tile = KEY_TILE // GRID_W
    need = -(-(max(starts) + kr - min(starts)) // rows_per_tile) * rows_per_tile
    return min(min(starts), n_rows - need), need


def _na_kernel(q_ref, k_ref, v_ref, ck_ref, cv_ref, rpb_ref, g_ref, o_ref, pt_ref, badd_ref, s_ref, *, n_rows):
    kr = min(NA_ROWS_MAX, n_rows)
    n_grp = n_rows // NA_Q_ROWS
    n_pairs = q_ref.shape[2] // LANES
    low = _lane_low_half()

    @pl.when(pl.program_id(1) == 0)
    def _build_bias():
        for head in range(2 * n_pairs):
            _na_pair_tiles(rpb_ref, head, pt_ref)
        qcol = lax.broadcasted_iota(jnp.int32, (GRID_W, LANES), 0)
        lane = lax.broadcasted_iota(jnp.int32, (GRID_W, LANES), 1)
        right = lane >= GRID_W
        kcol = lane & (GRID_W - 1)
        col_start = jnp.clip(qcol - NA_COLS // 2, 0, GRID_W - NA_COLS)
        col_ok = (kcol >= col_start) & (kcol < col_start + NA_COLS)
        for grp in range(n_grp):
            krow0, key_rows = _na_window(grp, n_rows)
            for i in range(NA_Q_ROWS):
                r = grp * NA_Q_ROWS + i
                row_start = _na_row_start(r, n_rows)
                for m in range(key_rows // 2):
                    key_row = krow0 + 2 * m
                    ok_l = row_start <= key_row < row_start + kr
                    ok_r = row_start <= key_row + 1 < row_start + kr
                    rows = slice(i * GRID_W, (i + 1) * GRID_W)
                    cols = slice(m * LANES, (m + 1) * LANES)
                    if not (ok_l or ok_r):
                        for head in range(2 * n_pairs):
                            badd_ref[head * n_grp + grp, rows, cols] = jnp.full((GRID_W, LANES), -jnp.inf, F32)
                        continue
                    ok = col_ok
                    if not ok_r:
                        ok = ok & jnp.logical_not(right)
                    if not ok_l:
                        ok = ok & right
                    p = key_row - r + NA_ROWS_MAX
                    for head in range(2 * n_pairs):
                        badd_ref[head * n_grp + grp, rows, cols] = jnp.where(ok, pt_ref[head, p], -jnp.inf)

    n_ctx = ck_ref.shape[1] // KEY_TILE
    q_blocks, kv_blocks, blocks = [], [], []
    for pair in range(n_pairs):
        lanes = slice(pair * LANES, (pair + 1) * LANES)
        ctx_k = _tiles(ck_ref, n_ctx, pair * LANES)
        ctx_v = _tiles(cv_ref, n_ctx, pair * LANES)
        for grp in range(n_grp):
            krow0, key_rows = _na_window(grp, n_rows)
            k0 = krow0 * GRID_W
            win = [slice(k0 + c * KEY_TILE, k0 + (c + 1) * KEY_TILE) for c in range(key_rows * GRID_W // KEY_TILE)]
            kv = ([k_ref[0, w, lanes] for w in win] + ctx_k,
                  _values_and_ones([v_ref[0, w, lanes] for w in win] + ctx_v))
            q_blocks += list(_split_lane_halves(q_ref[0, grp * ROW_BLK:(grp + 1) * ROW_BLK, lanes], low))
            kv_blocks += [kv, kv]
            blocks += [(2 * pair, grp, len(win)), (2 * pair + 1, grp, len(win))]

    def masked_bias(rb, c, s):
        head, grp, n_win = blocks[rb]
        if c >= n_win:
            return None
        cols = slice(c * KEY_TILE, (c + 1) * KEY_TILE)
        return s + badd_ref[head * n_grp + grp, :, cols]

    res = _attend(q_blocks, kv_blocks, s_ref, masked_bias)
    for pair in range(n_pairs):
        lanes = slice(pair * LANES, (pair + 1) * LANES)
        for grp in range(n_grp):
            rb = 2 * (pair * n_grp + grp)
            o = _normalise_pair(res[rb], res[rb + 1], low)
            rows = slice(grp * ROW_BLK, (grp + 1) * ROW_BLK)
            o_ref[0, rows, lanes] = (o * _silu(g_ref[0, rows, lanes].astype(F32))).astype(o_ref.dtype)


def _na_attn(q, k, v, ck, cv, rpb, g):
    heads, rpb_rows, rpb_cols = rpb.shape
    rpb = jnp.pad(rpb, ((0, 0), (0, NA_PAIR_TILES - rpb_rows), (0, LANES - rpb_cols)))
    b, t, w = q.shape
    c = ck.shape[1]
    n_rows = t // GRID_W
    assert NA_Q_ROWS * GRID_W == ROW_BLK and n_rows % NA_Q_ROWS == 0 and n_rows >= NA_K_ROWS
    n_grp = n_rows // NA_Q_ROWS
    n_win = NA_K_ROWS * GRID_W
    wb = NA_PAIRS_PER_STEP * LANES
    hb = 2 * NA_PAIRS_PER_STEP
    tspec = pl.BlockSpec((1, t, wb), lambda hp, bi: (bi, 0, hp))
    cspec = pl.BlockSpec((1, c, wb), lambda hp, bi: (bi, 0, hp))
    ptspec = pl.BlockSpec((hb, NA_PAIR_TILES, LANES), lambda hp, bi: (hp, 0, 0))
    return pl.pallas_call(
        functools.partial(_na_kernel, n_rows=n_rows),
        grid=(w // wb, b),
        in_specs=[tspec, tspec, tspec, cspec, cspec, ptspec, tspec],
        out_specs=tspec,
        out_shape=jax.ShapeDtypeStruct((b, t, w), BF16),
        scratch_shapes=[pltpu.VMEM((hb, NA_PAIR_TILES, GRID_W, LANES), F32),
                        pltpu.VMEM((hb * n_grp, ROW_BLK, n_win), F32),
                        pltpu.VMEM((SCORE_SLOTS * ROW_BLK, n_win + c), F32)],
        compiler_params=_params("parallel", "arbitrary"),
        name="na_attn",
    )(q, k, v, ck, cv, rpb, g)


def _softplus(x):
    return jnp.maximum(x, 0.0) + jnp.log(1.0 + jnp.exp(-jnp.abs(x)))


def _lru_kernel(x_ref, gb_ref, h0f_ref, h0b_ref, cw_ref, cb_ref, wg_ref, lam_ref,
                o_ref, ff_ref, fb_ref, xt_s, af_s, uf_s, ab_s, ub_s, hf_s, hb_s):
    nb, t, _ = x_ref.shape
    pad = CONV_PAD_L * nb
    cw = 0.5 * cw_ref[...]
    cb = 0.5 * cb_ref[...]
    wg = wg_ref[0]
    kk = (-0.5 * LRU_C * LOG2E) * _softplus(-lam_ref[...])

    xt_s[0:pad, :] = jnp.zeros((pad, LANES), F32)
    xt_s[pad + t * nb:pad + (t + CONV_PAD_R) * nb, :] = jnp.zeros((CONV_PAD_R * nb, LANES), F32)

    def to_time_major(b, carry):
        xt_s[pl.ds(pad + b, t, stride=nb), :] = x_ref[b]
        return carry

    lax.fori_loop(0, nb, to_time_major, 0)

    rows_per_chunk = LRU_CHUNK * nb
    lane = lax.broadcasted_iota(jnp.int32, (rows_per_chunk, LANES), 1)
    bias_cols = jnp.where(lane < 2, 1.0, 0.0).astype(BF16)

    def gates(c, carry):
        r0 = pl.multiple_of(c * rows_per_chunk, rows_per_chunk)

        def tap(i):
            return xt_s[pl.ds(r0 + i * nb, rows_per_chunk), :]

        xh = cw[0:1] * tap(0) + cw[1:2] * tap(1) + cw[2:3] * tap(2) + cw[3:4] * tap(3) + cb
        th = jnp.tanh(_dot(jnp.concatenate([xh.astype(BF16), bias_cols], axis=1), wg))
        rows = pl.ds(r0, rows_per_chunk)
        for d, (a_s, u_s) in enumerate(((af_s, uf_s), (ab_s, ub_s))):
            t_a = th[:, (2 * d) * LANES:(2 * d + 1) * LANES]
            t_x = th[:, (2 * d + 1) * LANES:(2 * d + 2) * LANES]
            a = jnp.exp2(kk[d:d + 1] * t_a + kk[d:d + 1])
            y = 1.0 - a * a
            sq = y * lax.rsqrt(jnp.maximum(y, 1e-30))
            a_s[rows, :] = a
            u_s[rows, :] = sq * (xh * t_x + xh)
        return carry

    lax.fori_loop(0, t // LRU_CHUNK, gates, 0)

    ts = t // LRU_SEGMENTS
    refs = ((af_s, uf_s), (ab_s, ub_s))

    slab = nb * LRU_UNROLL

    def slab_rows(d, j, i):
        base = (j * ts + i * LRU_UNROLL) * nb
        return pl.ds(pl.multiple_of(base if d == 0 else t * nb - slab - base, slab), slab)

    def step_of(d, block, k):
        kk_ = k if d == 0 else LRU_UNROLL - 1 - k
        return block[kk_ * nb:(kk_ + 1) * nb]

    def compose(i, carry):
        state = list(carry)
        for d in range(2):
            for j in range(LRU_SEGMENTS - 1):
                n = d * (LRU_SEGMENTS - 1) + j
                a_blk = refs[d][0][slab_rows(d, j, i), :]
                u_blk = refs[d][1][slab_rows(d, j, i), :]
                g, big_a = state[n]
                for k in range(LRU_UNROLL):
                    a = step_of(d, a_blk, k)
                    g, big_a = a * g + step_of(d, u_blk, k), a * big_a
                state[n] = (g, big_a)
        return tuple(state)

    ones = jnp.ones((nb, LANES), F32)
    h0 = (h0f_ref[...], h0b_ref[...])
    init = tuple((h0[d] if j == 0 else jnp.zeros((nb, LANES), F32), ones)
                 for d in range(2) for j in range(LRU_SEGMENTS - 1))
    ends = lax.fori_loop(0, ts // LRU_UNROLL, compose, init)
    starts = []
    for d in range(2):
        h = h0[d]
        for j in range(LRU_SEGMENTS):
            starts.append(h)
            if j < LRU_SEGMENTS - 1:
                g, big_a = ends[d * (LRU_SEGMENTS - 1) + j]
                h = g if j == 0 else g + big_a * h

    def rerun(i, carry):
        state = list(carry)
        for d, h_s in enumerate((hf_s, hb_s)):
            for j in range(LRU_SEGMENTS):
                n = d * LRU_SEGMENTS + j
                a_blk = refs[d][0][slab_rows(d, j, i), :]
                u_blk = refs[d][1][slab_rows(d, j, i), :]
                hs = []
                for k in range(LRU_UNROLL):
                    state[n] = step_of(d, a_blk, k) * state[n] + step_of(d, u_blk, k)
                    hs.append(state[n])
                h_s[slab_rows(d, j, i), :] = jnp.concatenate(hs if d == 0 else hs[::-1], axis=0)
        return tuple(state)

    final = lax.fori_loop(0, ts // LRU_UNROLL, rerun, tuple(starts))
    ff_ref[...] = final[LRU_SEGMENTS - 1]
    fb_ref[...] = final[2 * LRU_SEGMENTS - 1]

    def add_directions(c, carry):
        rows = pl.ds(pl.multiple_of(c * rows_per_chunk, rows_per_chunk), rows_per_chunk)
        hf_s[rows, :] = hf_s[rows, :] + hb_s[rows, :]
        return carry

    lax.fori_loop(0, t // LRU_CHUNK, add_directions, 0)

    def emit(b, carry):
        hs = hf_s[pl.ds(b, t, stride=nb), :]
        o_ref[b] = (hs * _silu(gb_ref[b].astype(F32))).astype(o_ref.dtype)
        return carry

    lax.fori_loop(0, nb, emit, 0)


def _lru_branch(xb, gb, h0f, h0b, conv_w, conv_b, wg, lam):
    b, t, w = xb.shape
    nb = SUBLANES
    assert t % (LRU_SEGMENTS * LRU_UNROLL) == 0 and t % LRU_CHUNK == 0
    assert conv_w.shape[0] == CONV_PAD_L + 1 + CONV_PAD_R
    xspec = pl.BlockSpec((nb, t, LANES), lambda i, c: (i, 0, c))
    hspec = pl.BlockSpec((nb, LANES), lambda i, c: (i, c))
    scratch = [pltpu.VMEM(((t + CONV_PAD_L + CONV_PAD_R) * nb, LANES), F32)]
    scratch += [pltpu.VMEM((t * nb, LANES), F32) for _ in range(6)]
    return pl.pallas_call(
        _lru_kernel,
        grid=(b // nb, w // LANES),
        in_specs=[xspec, xspec, hspec, hspec,
                  pl.BlockSpec((conv_w.shape[0], LANES), lambda i, c: (0, c)),
                  pl.BlockSpec((1, LANES), lambda i, c: (0, c)),
                  pl.BlockSpec((1, 2 * LANES, 4 * LANES), lambda i, c: (c, 0, 0)),
                  pl.BlockSpec((2, LANES), lambda i, c: (0, c))],
        out_specs=[xspec, hspec, hspec],
        out_shape=[jax.ShapeDtypeStruct((b, t, w), BF16),
                   jax.ShapeDtypeStruct((b, w), F32),
                   jax.ShapeDtypeStruct((b, w), F32)],
        scratch_shapes=scratch,
        compiler_params=_params("parallel", "parallel"),
        name="lru_branch",
    )(xb, gb, h0f, h0b, conv_w, conv_b.reshape(1, w), wg, lam)


def _lru_gate_weights(wa, ba, wx, bx):
    def tiles(w):
        w2 = w.reshape(LRU_BLOCKS // 2, 2, LRU_BLOCK, LRU_BLOCK)
        z = jnp.zeros_like(w2[:, 0])
        return jnp.concatenate([jnp.concatenate([w2[:, 0], z], axis=2),
                                jnp.concatenate([z, w2[:, 1]], axis=2)], axis=1)
    wg = jnp.concatenate([tiles(wa[0]), tiles(wx[0]), tiles(wa[1]), tiles(wx[1])], axis=2)
    n_tiles = LRU_WIDTH // LANES
    bg = jnp.stack([ba[0], bx[0], ba[1], bx[1]], axis=0).reshape(4, n_tiles, LANES)
    bg = 0.5 * jnp.transpose(bg, (1, 0, 2)).reshape(n_tiles, 1, 4 * LANES)
    hi = bg.astype(BF16)
    lo = (bg - hi.astype(F32)).astype(BF16)
    rest = jnp.zeros((n_tiles, LANES - 2, 4 * LANES), BF16)
    return jnp.concatenate([wg.astype(BF16), hi, lo, rest], axis=1)


def _packed_tiles(ref, n_heads, head, bi=0):
    n_tiles = ref.shape[1] // (n_heads * KEY_TILE)
    return [ref[bi, pl.ds(c * KEY_TILE * n_heads + head, KEY_TILE, stride=n_heads), :].astype(BF16)
            for c in range(n_tiles)]


def _diff_kernel(*refs, packed, n_heads, lam_init):
    n_kv = len(packed)
    q_ref = refs[0]
    kv_refs = refs[1:1 + 2 * n_kv]
    g_ref, lamp_ref, subg_ref, o_ref, s_ref = refs[1 + 2 * n_kv:]
    low = _lane_low_half()
    tq = q_ref.shape[1]
    first_head = pl.program_id(1) * (q_ref.shape[2] // LANES)
    lp = lamp_ref[...]
    lam = (jnp.exp(jnp.sum(lp[0:1] * lp[1:2], axis=-1, keepdims=True))
           - jnp.exp(jnp.sum(lp[2:3] * lp[3:4], axis=-1, keepdims=True)) + lam_init)
    sub_gain = subg_ref[...] * (1.0 - lam_init)
    nb = q_ref.shape[0]
    tiles = [(bi, j) for bi in range(nb) for j in range(0, q_ref.shape[2], LANES)]
    q_blocks, kv_blocks = [], []
    for bi, j in tiles:
        k_tiles, v_tiles = [], []
        for i in range(n_kv):
            if packed[i]:
                head = first_head + j // LANES
                k_tiles += _packed_tiles(kv_refs[2 * i], n_heads, head, bi)
                v_tiles += _packed_tiles(kv_refs[2 * i + 1], n_heads, head, bi)
            else:
                n_tiles = kv_refs[2 * i].shape[1] // KEY_TILE
                k_tiles += _tiles(kv_refs[2 * i], n_tiles, j, bi)
                v_tiles += _tiles(kv_refs[2 * i + 1], n_tiles, j, bi)
        v_tiles = _values_and_ones(v_tiles)
        for r in range(0, tq, ROW_BLK):
            q_blocks += list(_split_lane_halves(q_ref[bi, r:r + ROW_BLK, j:j + LANES], low))
            kv_blocks += [(k_tiles, v_tiles)] * 2
    res = _attend(q_blocks, kv_blocks, s_ref)
    i = 0
    for bi, j in tiles:
        for r in range(0, tq, ROW_BLK):
            acc1, acc2 = res[i], res[i + 1]
            i += 2
            o = acc1[:, :LANES] / acc1[:, LANES:] - acc2[:, :LANES] * (lam / acc2[:, LANES:])
            o = o * lax.rsqrt(jnp.mean(o * o, axis=-1, keepdims=True) + EPS) * sub_gain
            gate = _silu(g_ref[bi, r:r + ROW_BLK, j:j + LANES].astype(F32))
            o_ref[bi, r:r + ROW_BLK, j:j + LANES] = (o * gate).astype(o_ref.dtype)


def _diff_attn(q, kv_list, g, lam_params, sub_g, lam_init, tq, heads_per_step, batch_per_step=1):
    b, t, w = q.shape
    wb = heads_per_step * LANES
    nb = batch_per_step
    qspec = pl.BlockSpec((nb, tq, wb), lambda bi, h, qi: (bi, qi, h))
    n_heads = w // LANES
    in_specs = [qspec]
    args = [q]
    n_keys = 0
    for k, v, packed in kv_list:
        if packed:
            spec = pl.BlockSpec((nb, k.shape[1], LANES), lambda bi, h, qi: (bi, 0, 0))
            n_keys += k.shape[1] // n_heads
        else:
            spec = pl.BlockSpec((nb, k.shape[1], wb), lambda bi, h, qi: (bi, 0, h))
            n_keys += k.shape[1]
        in_specs += [spec, spec]
        args += [k, v]
    in_specs += [qspec,
                 pl.BlockSpec(lam_params.shape, lambda bi, h, qi: (0, 0)),
                 pl.BlockSpec((1, LANES), lambda bi, h, qi: (0, 0))]
    args += [g, lam_params, sub_g.reshape(1, LANES)]
    return pl.pallas_call(
        functools.partial(_diff_kernel, packed=tuple(p for _, _, p in kv_list), n_heads=n_heads,
                          lam_init=lam_init),
        grid=(b // nb, w // wb, t // tq),
        in_specs=in_specs,
        out_specs=qspec,
        out_shape=jax.ShapeDtypeStruct((b, t, w), BF16),
        scratch_shapes=[pltpu.VMEM((SCORE_SLOTS * ROW_BLK, n_keys), F32)],
        compiler_params=_params("parallel", "parallel", "parallel"),
        name="diff_attn",
    )(*args)


def _rope_tables(t_len):
    m = DIFF_HEAD_DIM // 4
    t = jnp.arange(t_len)
    rows = (t // GRID_W).astype(F32)
    cols = (t % GRID_W).astype(F32)
    inv = ROPE_BASE ** (-jnp.arange(m, dtype=F32) / m)
    ang_r = rows[:, None] * inv[None, :]
    ang_c = cols[:, None] * inv[None, :]
    cos = jnp.concatenate([jnp.cos(ang_r)] * 2 + [jnp.cos(ang_c)] * 2, axis=1)
    sin = jnp.concatenate([-jnp.sin(ang_r), jnp.sin(ang_r), -jnp.sin(ang_c), jnp.sin(ang_c)], axis=1)
    reps = LANES // DIFF_HEAD_DIM
    return jnp.tile(cos, (1, reps)), jnp.tile(sin, (1, reps))


def kernel(x_prompt, x_sample, c, cache_na_k, cache_na_v, state_lru, cache_diff_k, cache_diff_v, c_ctx,
           e_norm, e_ada_w, e_ada_b, e_w_in, e_rpb, e_conv_w, e_conv_b, e_lru_wa, e_lru_ba, e_lru_wx, e_lru_bx,
           e_lru_lam, e_w_out, o_norm, o_ada_w, o_ada_b, o_w_in, o_lq1, o_lk1, o_lq2, o_lk2, o_sub_g, o_w_out,
           final_norm):
    bp, tp, d = x_prompt.shape
    bs, ts, _ = x_sample.shape
    past = cache_na_k.shape[2]
    pad_rows = 2 * SUBLANES - bs - 1
    cc = jnp.concatenate([c, c_ctx[None, :], jnp.zeros((pad_rows, d), F32)], axis=0)
    nw, lw, dw = NA_WIDTH, LRU_WIDTH, DIFF_WIDTH
    rope_tables = _rope_tables(ts) if DEPTH > 1 else None

    def layer_params(i):
        j = i // 2
        if i % 2 == 0:
            return e_ada_w[j], e_ada_b[j], e_norm[j], e_w_in[j], e_w_out[j]
        return o_ada_w[j], o_ada_b[j], o_norm[j], o_w_in[j], o_w_out[j]

    mods = []
    for i in range(DEPTH):
        ada_w, ada_b = layer_params(i)[:2]
        m = _modulation(cc, ada_w, ada_b)
        mods.append((m[:bs].reshape(bs, 1, 3 * d), m[bs:bs + 1].reshape(1, 1, 3 * d)))

    def projection(i, latent):
        norm_g, w_in = layer_params(i)[2:4]
        mod = mods[i][0 if latent else 1]
        kv_dtype = BF16 if latent else F32
        if i % 2 == 0:
            qmul = NA_HEAD_DIM ** -0.5 * LOG2E
            outs = [(nw, BF16, False, qmul, False), (nw, kv_dtype, False, None, False),
                    (nw, kv_dtype, False, None, False), (nw, BF16, False, None, False),
                    (lw, F32, False, None, False), (lw, BF16, False, None, False)]
            return mod, norm_g, w_in, outs, None
        qmul = DIFF_HEAD_DIM ** -0.5 * LOG2E
        outs = [(dw, BF16, latent, qmul, False), (dw, kv_dtype, latent, None, not latent),
                (dw, kv_dtype, False, None, not latent), (dw, BF16, False, None, False)]
        return mod, norm_g, w_in, outs, rope_tables if latent else None

    xp, xs = x_prompt, x_sample
    _, proj_p = _stage(xp, nxt=projection(0, False), tm=512)
    _, proj_s = _stage(xs, nxt=projection(0, True), tm=512)
    na_k, na_v, lru_s, df_k, df_v = [], [], [], [], []
    for i in range(DEPTH):
        j = i // 2
        w_out = layer_params(i)[4]
        if i % 2 == 0:
            wg = _lru_gate_weights(e_lru_wa[j], e_lru_ba[j], e_lru_wx[j], e_lru_bx[j])

            q, k, v, ga, xb, gb = proj_p
            oa = _ctx_attn(q, k, v, ga)
            zeros = jnp.zeros((bp, lw), F32)
            ob, ff, fb = _lru_branch(xb, gb, zeros, zeros, e_conv_w[j], e_conv_b[j], wg, e_lru_lam[j])
            mixed_p = [oa, ob]
            na_k.append(k.reshape(bp, tp, NA_HEADS, NA_HEAD_DIM))
            na_v.append(v.reshape(bp, tp, NA_HEADS, NA_HEAD_DIM))
            lru_s.append(jnp.stack([ff, fb], axis=1))

            q, k, v, ga, xb, gb = proj_s
            ck = cache_na_k[:, j].reshape(bs, past, nw)
            cv = cache_na_v[:, j].reshape(bs, past, nw)
            oa = _na_attn(q, k, v, ck, cv, e_rpb[j], ga)
            ob, _, _ = _lru_branch(xb, gb, state_lru[:, j, 0], state_lru[:, j, 1],
                                   e_conv_w[j], e_conv_b[j], wg, e_lru_lam[j])
            mixed_s = [oa, ob]
        else:
            lam_init = 0.8 - 0.6 * math.exp(-0.3 * i)
            lam_params = jnp.stack([o_lq1[j], o_lk1[j], o_lq2[j], o_lk2[j]], axis=0)

            q, k, v, g = proj_p
            mixed_p = [_diff_attn(q, [(k, v, True)], g, lam_params, o_sub_g[j], lam_init, tq=tp,
                                  heads_per_step=DIFF_HEADS, batch_per_step=CTX_BATCH_PER_STEP)]
            df_k.append(k.reshape(bp, tp, DIFF_HEADS, 2 * DIFF_HEAD_DIM))
            df_v.append(v.reshape(bp, tp, DIFF_HEADS, 2 * DIFF_HEAD_DIM))

            q, k, v, g = proj_s
            ck = cache_diff_k[:, j].reshape(bs, past * DIFF_HEADS, 2 * DIFF_HEAD_DIM)
            cv = cache_diff_v[:, j].reshape(bs, past * DIFF_HEADS, 2 * DIFF_HEAD_DIM)
            mixed_s = [_diff_attn(q, [(ck, cv, True), (k, v, False)], g, lam_params, o_sub_g[j], lam_init, tq=ts,
                                  heads_per_step=DIFF_HEADS)]

        if i == DEPTH - 1:
            xp, _ = _stage(xp, prev=(mixed_p, w_out, mods[i][1]), final_g=final_norm, tm=512)
            xs, _ = _stage(xs, prev=(mixed_s, w_out, mods[i][0]), final_g=final_norm, tm=1024)
        else:
            xp, proj_p = _stage(xp, prev=(mixed_p, w_out, mods[i][1]), nxt=projection(i + 1, False), tm=512)
            xs, proj_s = _stage(xs, prev=(mixed_s, w_out, mods[i][0]), nxt=projection(i + 1, True), tm=512)

    return (xp, xs, jnp.stack(na_k, axis=1), jnp.stack(na_v, axis=1), jnp.stack(lru_s, axis=1),
            jnp.stack(df_k, axis=1), jnp.stack(df_v, axis=1))
```

```python
import functools
import math

import jax
import jax.numpy as jnp
from jax import lax
from jax.experimental import pallas as pl
from jax.experimental.pallas import tpu as pltpu

DEPTH = 2
GRID_W = 64
NA_HEADS = 8
NA_HEAD_DIM = 64
NA_WIDTH = NA_HEADS * NA_HEAD_DIM
NA_ROWS_MAX = 8
NA_COLS = 16
RPB_ROWS = 2 * NA_ROWS_MAX - 1
RPB_COLS = 2 * NA_COLS - 1
LRU_WIDTH = 512
LRU_BLOCKS = 8
LRU_BLOCK = LRU_WIDTH // LRU_BLOCKS
LRU_C = 8.0
DIFF_HEADS = 8
DIFF_HEAD_DIM = 64
DIFF_WIDTH = DIFF_HEADS * 2 * DIFF_HEAD_DIM
ROPE_BASE = 10000.0
EPS = 1e-6
LOG2E = math.log2(math.e)

LANES = 128
SUBLANES = 8
MXU_DIM = 256
VMEM_LIMIT = 56 * 1024 * 1024

CAST_COLS = 512
LRU_SEGMENTS = 4
LRU_UNROLL = 8
LRU_CHUNK = 128
CONV_PAD_L = 2
CONV_PAD_R = 1
ROW_BLK = MXU_DIM
KEY_TILE = MXU_DIM
SCORE_SLOTS = 4

NA_Q_ROWS = 4
NA_K_ROWS = 12
NA_PAIR_TILES = 16
NA_PAIRS_PER_STEP = 4
CTX_BATCH_PER_STEP = 4

F32 = jnp.float32
BF16 = jnp.bfloat16


def _params(*semantics):
    return pltpu.CompilerParams(dimension_semantics=semantics, vmem_limit_bytes=VMEM_LIMIT)


def _silu(x):
    h = 0.5 * x
    return h * jnp.tanh(h) + h


def _dot(a, b):
    return jnp.dot(a, b, preferred_element_type=F32)


def _dot_nt(a, b):
    return lax.dot_general(a, b, (((1,), (1,)), ((), ())), preferred_element_type=F32)


def _mod_kernel(c_ref, w_ref, b_ref, o_ref):
    s = _silu(c_ref[...])
    o_ref[...] = _dot(s.astype(BF16), w_ref[...].astype(BF16)) + b_ref[...]


def _modulation(cc, w, b):
    rows, d = cc.shape
    n = w.shape[1]
    tn = 1024
    return pl.pallas_call(
        _mod_kernel,
        grid=(n // tn,),
        in_specs=[pl.BlockSpec((rows, d), lambda j: (0, 0)),
                  pl.BlockSpec((d, tn), lambda j: (0, j)),
                  pl.BlockSpec((1, tn), lambda j: (0, j))],
        out_specs=pl.BlockSpec((rows, tn), lambda j: (0, j)),
        out_shape=jax.ShapeDtypeStruct((rows, n), F32),
        compiler_params=_params("arbitrary"),
        name="modulation",
    )(cc, w, b.reshape(1, n))


def _rope_block(x, cos, sin_signed, low16):
    partner = jnp.where(low16, pltpu.roll(x, LANES - 16, axis=1), pltpu.roll(x, 16, axis=1))
    return x * cos + partner * sin_signed


def _cast_weights_once(w_ref, wbf_ref):
    @pl.when((pl.program_id(0) == 0) & (pl.program_id(1) == 0))
    def _():
        for off in range(0, w_ref.shape[1], CAST_COLS):
            wbf_ref[:, off:off + CAST_COLS] = w_ref[:, off:off + CAST_COLS].astype(BF16)


def _norm_project(x, mod_ref, g_ref, wbf_ref, rope_refs, out_refs, outs):
    tm, d = x.shape
    y = x * lax.rsqrt(jnp.mean(x * x, axis=-1, keepdims=True) + EPS) * g_ref[...]
    mod = mod_ref[0]
    h = (y * (1.0 + mod[:, d:2 * d]) + mod[:, :d]).astype(BF16)
    if rope_refs:
        cos = rope_refs[0][...]
        sin = rope_refs[1][...]
        lane = lax.broadcasted_iota(jnp.int32, (1, LANES), 1)
        low16 = (lane & 16) == 0
    off = 0
    for o_ref, (width, roped, mul, packed) in zip(out_refs, outs):
        acc = _dot(h, wbf_ref[:, off:off + width])
        if roped:
            acc = jnp.concatenate(
                [_rope_block(acc[:, j:j + LANES], cos, sin, low16) for j in range(0, width, LANES)], axis=1)
        if mul is not None:
            acc = acc * mul
        if packed:
            n_heads = width // LANES
            t = o_ref.shape[1] // n_heads
            for bi in range(o_ref.shape[0]):
                for hh in range(n_heads):
                    o_ref[bi, pl.ds(hh, t, stride=n_heads), :] = (
                        acc[bi * t:(bi + 1) * t, hh * LANES:(hh + 1) * LANES].astype(o_ref.dtype))
        else:
            o_ref[...] = acc.astype(o_ref.dtype).reshape(o_ref.shape)
        off += width


def _token_rows(ref):
    return ref[...].reshape(-1, ref.shape[-1])


def _gated_residual(o_refs, wbf_ref, x_ref, mod_ref):
    d = x_ref.shape[-1]
    acc, row = None, 0
    for o_ref in o_refs:
        k = o_ref.shape[-1]
        part = _dot(_token_rows(o_ref), wbf_ref[row:row + k, :])
        acc = part if acc is None else acc + part
        row += k
    return _token_rows(x_ref) + mod_ref[0][:, 2 * d:] * acc


def _stage_kernel(*refs, n_o, final, outs, rope):
    refs = list(refs)
    take = lambda n: [refs.pop(0) for _ in range(n)]
    if n_o:
        o_refs = take(n_o)
        wout_ref, x_ref, mod_prev_ref = take(3)
        fin_ref = take(1)[0] if final else None
    else:
        x_ref, = take(1)
    if outs:
        mod_ref, g_ref, win_ref = take(3)
        rope_refs = take(2) if rope else []
    xout_ref = take(1)[0] if n_o else None
    out_refs = take(len(outs))
    if n_o:
        wout_bf = take(1)[0]
        _cast_weights_once(wout_ref, wout_bf)
        x = _gated_residual(o_refs, wout_bf, x_ref, mod_prev_ref)
        if final:
            x_new = x * lax.rsqrt(jnp.mean(x * x, axis=-1, keepdims=True) + EPS) * fin_ref[...]
        else:
            x_new = x
        xout_ref[...] = x_new.reshape(xout_ref.shape)
    else:
        x = _token_rows(x_ref)
    if outs:
        win_bf = take(1)[0]
        _cast_weights_once(win_ref, win_bf)
        _norm_project(x, mod_ref, g_ref, win_bf, rope_refs, out_refs, outs)


def _stage(x, prev=None, nxt=None, final_g=None, tm=256):
    b, t, d = x.shape
    shared_mod = all(m.shape[0] == 1 for m in ([prev[2]] if prev else []) + ([nxt[0]] if nxt else []))
    bb = max(1, tm // t) if shared_mod else 1
    tm = min(tm, t)
    tile = lambda width: pl.BlockSpec((bb, tm, width), lambda i, j: (i, j, 0))
    mod_spec = lambda mod: pl.BlockSpec(
        (1, 1, 3 * d), (lambda i, j: (i, 0, 0)) if mod.shape[0] > 1 else (lambda i, j: (0, 0, 0)))
    resident = lambda w: pl.BlockSpec(w.shape, lambda i, j: (0, 0), pipeline_mode=pl.Buffered(1))
    row_spec = pl.BlockSpec((1, d), lambda i, j: (0, 0))
    in_specs, args, out_specs, out_shape, scratch = [], [], [], [], []
    n_o, outs, rope = 0, (), False
    if prev is not None:
        o_list, w_out, mod_prev = prev
        n_o = len(o_list)
        in_specs += [tile(o.shape[-1]) for o in o_list] + [resident(w_out), tile(d), mod_spec(mod_prev)]
        args += list(o_list) + [w_out, x, mod_prev]
        if final_g is not None:
            in_specs.append(row_spec)
            args.append(final_g.reshape(1, d))
        out_specs.append(tile(d))
        out_shape.append(jax.ShapeDtypeStruct((b, t, d), F32))
        scratch.append(pltpu.VMEM(w_out.shape, BF16))
    else:
        in_specs.append(tile(d))
        args.append(x)
    if nxt is not None:
        mod_next, norm_g, w_in, out_list, rope_tables = nxt
        in_specs += [mod_spec(mod_next), row_spec, resident(w_in)]
        args += [mod_next, norm_g.reshape(1, d), w_in]
        rope = rope_tables is not None
        if rope:
            in_specs += [pl.BlockSpec((tm, LANES), lambda i, j: (j, 0))] * 2
            args += list(rope_tables)
        for width, dt, _, _, packed in out_list:
            if packed:
                heads = width // LANES
                out_specs.append(pl.BlockSpec((bb, tm * heads, LANES), lambda i, j: (i, j, 0)))
                out_shape.append(jax.ShapeDtypeStruct((b, t * heads, LANES), dt))
            else:
                out_specs.append(tile(width))
                out_shape.append(jax.ShapeDtypeStruct((b, t, width), dt))
        outs = tuple((w_, r, m, p) for w_, _, r, m, p in out_list)
        scratch.append(pltpu.VMEM(w_in.shape, BF16))
    res = pl.pallas_call(
        functools.partial(_stage_kernel, n_o=n_o, final=final_g is not None, outs=outs, rope=rope),
        grid=(b // bb, t // tm),
        in_specs=in_specs,
        out_specs=out_specs,
        out_shape=out_shape,
        scratch_shapes=scratch,
        compiler_params=_params("arbitrary", "arbitrary"),
        name="stage",
    )(*args)
    return (res[0], list(res[1:])) if n_o else (None, list(res))


def _attend(q_blocks, kv_blocks, s_ref, score_fn=None):
    n_rb = len(q_blocks)

    def tile_of(rb, c):
        slot = rb % SCORE_SLOTS
        return (slice(slot * ROW_BLK, (slot + 1) * ROW_BLK), slice(c * KEY_TILE, (c + 1) * KEY_TILE))

    def score_tile(rb, c):
        s_ref[tile_of(rb, c)] = _dot_nt(q_blocks[rb], kv_blocks[rb][0][c])

    def max_tile(rb, c, m_acc):
        s = s_ref[tile_of(rb, c)]
        adjusted = None if score_fn is None else score_fn(rb, c, s)
        if adjusted is not None:
            s = adjusted
            s_ref[tile_of(rb, c)] = s
        mt = jnp.maximum(s[:, :LANES], s[:, LANES:])
        return mt if m_acc is None else jnp.maximum(m_acc, mt)

    def n_tiles(rb):
        return len(kv_blocks[rb][0]) if 0 <= rb < n_rb else 0

    outs = []
    row_max = {}
    for step in range(n_rb + 2):
        rb_score, rb_max, rb_exp = step, step - 1, step - 2
        m_acc, acc = None, None
        for c in range(max(n_tiles(rb_score), n_tiles(rb_max), n_tiles(rb_exp))):
            if c < n_tiles(rb_score):
                score_tile(rb_score, c)
            if c < n_tiles(rb_max):
                m_acc = max_tile(rb_max, c, m_acc)
            if c < n_tiles(rb_exp):
                e = jnp.exp2(s_ref[tile_of(rb_exp, c)] - row_max[rb_exp])
                pv = _dot(e.astype(BF16), kv_blocks[rb_exp][1][c])
                acc = pv if acc is None else acc + pv
        if m_acc is not None:
            row_max[rb_max] = m_acc.max(axis=-1, keepdims=True)
        if acc is not None:
            outs.append(acc)
    return outs


def _lane_low_half():
    return lax.broadcasted_iota(jnp.int32, (1, LANES), 1) < (LANES // 2)


def _values_and_ones(v_tiles):
    ones = jnp.ones((KEY_TILE, LANES), BF16)
    return [jnp.concatenate([v, ones], axis=1) for v in v_tiles]


def _normalise_pair(acc_lo, acc_hi, low):
    return jnp.where(low, acc_lo[:, :LANES] / acc_lo[:, LANES:], acc_hi[:, :LANES] / acc_hi[:, LANES:])


def _split_lane_halves(q, low):
    zero = jnp.zeros_like(q)
    return jnp.where(low, q, zero), jnp.where(low, zero, q)


def _tiles(ref, n_tiles, j, bi=0):
    return [ref[bi, c * KEY_TILE:(c + 1) * KEY_TILE, j:j + LANES].astype(BF16) for c in range(n_tiles)]


def _ctx_attn_kernel(q_ref, k_ref, v_ref, g_ref, o_ref, s_ref):
    low = _lane_low_half()
    nb, tq, w = q_ref.shape
    n_tiles = k_ref.shape[1] // KEY_TILE
    tiles = [(bi, j, r) for bi in range(nb) for j in range(0, w, LANES) for r in range(0, tq, ROW_BLK)]
    q_blocks, kv_blocks, kv = [], [], {}
    for bi, j, r in tiles:
        if (bi, j) not in kv:
            kv[bi, j] = (_tiles(k_ref, n_tiles, j, bi), _values_and_ones(_tiles(v_ref, n_tiles, j, bi)))
        q_blocks += list(_split_lane_halves(q_ref[bi, r:r + ROW_BLK, j:j + LANES], low))
        kv_blocks += [kv[bi, j]] * 2
    res = _attend(q_blocks, kv_blocks, s_ref)
    for i, (bi, j, r) in enumerate(tiles):
        o = _normalise_pair(res[2 * i], res[2 * i + 1], low)
        gate = _silu(g_ref[bi, r:r + ROW_BLK, j:j + LANES].astype(F32))
        o_ref[bi, r:r + ROW_BLK, j:j + LANES] = (o * gate).astype(o_ref.dtype)


def _ctx_attn(q, k, v, g):
    b, t, w = q.shape
    nb = CTX_BATCH_PER_STEP
    spec = pl.BlockSpec((nb, t, w), lambda i: (i, 0, 0))
    return pl.pallas_call(
        _ctx_attn_kernel,
        grid=(b // nb,),
        in_specs=[spec] * 4,
        out_specs=spec,
        out_shape=jax.ShapeDtypeStruct((b, t, w), BF16),
        scratch_shapes=[pltpu.VMEM((SCORE_SLOTS * ROW_BLK, k.shape[1]), F32)],
        compiler_params=_params("parallel"),
        name="ctx_attn",
    )(q, k, v, g)


def _na_pair_tiles(rpb_ref, head, pt_ref):
    right = lax.broadcasted_iota(jnp.int32, (GRID_W, LANES), 1) >= GRID_W
    zero_row = jnp.zeros((1, LANES), F32)

    def toeplitz(row, lane0):
        shift = (lane0 - (NA_COLS - 1)) % LANES
        return pltpu.roll(jnp.broadcast_to(row, (GRID_W, LANES)), shift, axis=1, stride=1, stride_axis=0)

    for p in range(NA_PAIR_TILES):
        left_row = rpb_ref[head, p - 1:p, :] if p >= 1 else zero_row
        right_row = rpb_ref[head, p:p + 1, :]
        pt_ref[head, p] = jnp.where(right, toeplitz(right_row, GRID_W), toeplitz(left_row, 0)) * LOG2E


def _na_row_start(r, n_rows):
    kr = min(NA_ROWS_MAX, n_rows)
    return min(max(r - kr // 2, 0), n_rows - kr)


def _na_window(grp, n_rows):
    kr = min(NA_ROWS_MAX, n_rows)
    starts = [_na_row_start(grp * NA_Q_ROWS + i, n_rows) for i in range(NA_Q_ROWS)]
    rows_per_tile = KEY_TILE // GRID_W
    need = -(-(max(starts) + kr - min(starts)) // rows_per_tile) * rows_per_tile
    return min(min(starts), n_rows - need), need


def _na_kernel(q_ref, k_ref, v_ref, ck_ref, cv_ref, rpb_ref, g_ref, o_ref, pt_ref, badd_ref, s_ref, *, n_rows):
    kr = min(NA_ROWS_MAX, n_rows)
    n_grp = n_rows // NA_Q_ROWS
    n_pairs = q_ref.shape[2] // LANES
    low = _lane_low_half()

    @pl.when(pl.program_id(1) == 0)
    def _build_bias():
        for head in range(2 * n_pairs):
            _na_pair_tiles(rpb_ref, head, pt_ref)
        qcol = lax.broadcasted_iota(jnp.int32, (GRID_W, LANES), 0)
        lane = lax.broadcasted_iota(jnp.int32, (GRID_W, LANES), 1)
        right = lane >= GRID_W
        kcol = lane & (GRID_W - 1)
        col_start = jnp.clip(qcol - NA_COLS // 2, 0, GRID_W - NA_COLS)
        col_ok = (kcol >= col_start) & (kcol < col_start + NA_COLS)
        for grp in range(n_grp):
            krow0, key_rows = _na_window(grp, n_rows)
            for i in range(NA_Q_ROWS):
                r = grp * NA_Q_ROWS + i
                row_start = _na_row_start(r, n_rows)
                for m in range(key_rows // 2):
                    key_row = krow0 + 2 * m
                    ok_l = row_start <= key_row < row_start + kr
                    ok_r = row_start <= key_row + 1 < row_start + kr
                    rows = slice(i * GRID_W, (i + 1) * GRID_W)
                    cols = slice(m * LANES, (m + 1) * LANES)
                    if not (ok_l or ok_r):
                        for head in range(2 * n_pairs):
                            badd_ref[head * n_grp + grp, rows, cols] = jnp.full((GRID_W, LANES), -jnp.inf, F32)
                        continue
                    ok = col_ok
                    if not ok_r:
                        ok = ok & jnp.logical_not(right)
                    if not ok_l:
                        ok = ok & right
                    p = key_row - r + NA_ROWS_MAX
                    for head in range(2 * n_pairs):
                        badd_ref[head * n_grp + grp, rows, cols] = jnp.where(ok, pt_ref[head, p], -jnp.inf)

    n_ctx = ck_ref.shape[1] // KEY_TILE
    q_blocks, kv_blocks, blocks = [], [], []
    for pair in range(n_pairs):
        lanes = slice(pair * LANES, (pair + 1) * LANES)
        ctx_k = _tiles(ck_ref, n_ctx, pair * LANES)
        ctx_v = _tiles(cv_ref, n_ctx, pair * LANES)
        for grp in range(n_grp):
            krow0, key_rows = _na_window(grp, n_rows)
            k0 = krow0 * GRID_W
            win = [slice(k0 + c * KEY_TILE, k0 + (c + 1) * KEY_TILE) for c in range(key_rows * GRID_W // KEY_TILE)]
            kv = ([k_ref[0, w, lanes] for w in win] + ctx_k,
                  _values_and_ones([v_ref[0, w, lanes] for w in win] + ctx_v))
            q_blocks += list(_split_lane_halves(q_ref[0, grp * ROW_BLK:(grp + 1) * ROW_BLK, lanes], low))
            kv_blocks += [kv, kv]
            blocks += [(2 * pair, grp, len(win)), (2 * pair + 1, grp, len(win))]

    def masked_bias(rb, c, s):
        head, grp, n_win = blocks[rb]
        if c >= n_win:
            return None
        cols = slice(c * KEY_TILE, (c + 1) * KEY_TILE)
        return s + badd_ref[head * n_grp + grp, :, cols]

    res = _attend(q_blocks, kv_blocks, s_ref, masked_bias)
    for pair in range(n_pairs):
        lanes = slice(pair * LANES, (pair + 1) * LANES)
        for grp in range(n_grp):
            rb = 2 * (pair * n_grp + grp)
            o = _normalise_pair(res[rb], res[rb + 1], low)
            rows = slice(grp * ROW_BLK, (grp + 1) * ROW_BLK)
            o_ref[0, rows, lanes] = (o * _silu(g_ref[0, rows, lanes].astype(F32))).astype(o_ref.dtype)


def _na_attn(q, k, v, ck, cv, rpb, g):
    _, rpb_rows, rpb_cols = rpb.shape
    assert rpb_rows == RPB_ROWS and rpb_cols == RPB_COLS
    rpb = jnp.pad(rpb, ((0, 0), (0, NA_PAIR_TILES - rpb_rows), (0, LANES - rpb_cols)))
    b, t, w = q.shape
    c = ck.shape[1]
    n_rows = t // GRID_W
    assert NA_Q_ROWS * GRID_W == ROW_BLK and n_rows % NA_Q_ROWS == 0 and n_rows >= NA_K_ROWS
    n_grp = n_rows // NA_Q_ROWS
    n_win = NA_K_ROWS * GRID_W
    wb = NA_PAIRS_PER_STEP * LANES
    hb = 2 * NA_PAIRS_PER_STEP
    tspec = pl.BlockSpec((1, t, wb), lambda hp, bi: (bi, 0, hp))
    cspec = pl.BlockSpec((1, c, wb), lambda hp, bi: (bi, 0, hp))
    ptspec = pl.BlockSpec((hb, NA_PAIR_TILES, LANES), lambda hp, bi: (hp, 0, 0))
    return pl.pallas_call(
        functools.partial(_na_kernel, n_rows=n_rows),
        grid=(w // wb, b),
        in_specs=[tspec, tspec, tspec, cspec, cspec, ptspec, tspec],
        out_specs=tspec,
        out_shape=jax.ShapeDtypeStruct((b, t, w), BF16),
        scratch_shapes=[pltpu.VMEM((hb, NA_PAIR_TILES, GRID_W, LANES), F32),
                        pltpu.VMEM((hb * n_grp, ROW_BLK, n_win), F32),
                        pltpu.VMEM((SCORE_SLOTS * ROW_BLK, n_win + c), F32)],
        compiler_params=_params("parallel", "arbitrary"),
        name="na_attn",
    )(q, k, v, ck, cv, rpb, g)


def _softplus(x):
    return jnp.maximum(x, 0.0) + jnp.log(1.0 + jnp.exp(-jnp.abs(x)))


def _lru_kernel(x_ref, gb_ref, h0f_ref, h0b_ref, cw_ref, cb_ref, wg_ref, lam_ref,
                o_ref, ff_ref, fb_ref, xt_s, af_s, uf_s, ab_s, ub_s, hf_s, hb_s):
    nb, t, _ = x_ref.shape
    pad = CONV_PAD_L * nb
    cw = 0.5 * cw_ref[...]
    cb = 0.5 * cb_ref[...]
    wg = wg_ref[0]
    kk = (-0.5 * LRU_C * LOG2E) * _softplus(-lam_ref[...])

    xt_s[0:pad, :] = jnp.zeros((pad, LANES), F32)
    xt_s[pad + t * nb:pad + (t + CONV_PAD_R) * nb, :] = jnp.zeros((CONV_PAD_R * nb, LANES), F32)

    def to_time_major(b, carry):
        xt_s[pl.ds(pad + b, t, stride=nb), :] = x_ref[b]
        return carry

    lax.fori_loop(0, nb, to_time_major, 0)

    rows_per_chunk = LRU_CHUNK * nb
    lane = lax.broadcasted_iota(jnp.int32, (rows_per_chunk, LANES), 1)
    bias_cols = jnp.where(lane < 2, 1.0, 0.0).astype(BF16)

    def gates(c, carry):
        r0 = pl.multiple_of(c * rows_per_chunk, rows_per_chunk)

        def tap(i):
            return xt_s[pl.ds(r0 + i * nb, rows_per_chunk), :]

        xh = cw[0:1] * tap(0) + cw[1:2] * tap(1) + cw[2:3] * tap(2) + cw[3:4] * tap(3) + cb
        th = jnp.tanh(_dot(jnp.concatenate([xh.astype(BF16), bias_cols], axis=1), wg))
        rows = pl.ds(r0, rows_per_chunk)
        for d, (a_s, u_s) in enumerate(((af_s, uf_s), (ab_s, ub_s))):
            t_a = th[:, (2 * d) * LANES:(2 * d + 1) * LANES]
            t_x = th[:, (2 * d + 1) * LANES:(2 * d + 2) * LANES]
            a = jnp.exp2(kk[d:d + 1] * t_a + kk[d:d + 1])
            y = 1.0 - a * a
            sq = y * lax.rsqrt(jnp.maximum(y, 1e-30))
            a_s[rows, :] = a
            u_s[rows, :] = sq * (xh * t_x + xh)
        return carry

    lax.fori_loop(0, t // LRU_CHUNK, gates, 0)

    ts = t // LRU_SEGMENTS
    refs = ((af_s, uf_s), (ab_s, ub_s))

    slab = nb * LRU_UNROLL

    def slab_rows(d, j, i):
        base = (j * ts + i * LRU_UNROLL) * nb
        return pl.ds(pl.multiple_of(base if d == 0 else t * nb - slab - base, slab), slab)

    def step_of(d, block, k):
        kk_ = k if d == 0 else LRU_UNROLL - 1 - k
        return block[kk_ * nb:(kk_ + 1) * nb]

    def compose(i, carry):
        state = list(carry)
        for d in range(2):
            for j in range(LRU_SEGMENTS - 1):
                n = d * (LRU_SEGMENTS - 1) + j
                a_blk = refs[d][0][slab_rows(d, j, i), :]
                u_blk = refs[d][1][slab_rows(d, j, i), :]
                g, big_a = state[n]
                for k in range(LRU_UNROLL):
                    a = step_of(d, a_blk, k)
                    g, big_a = a * g + step_of(d, u_blk, k), a * big_a
                state[n] = (g, big_a)
        return tuple(state)

    ones = jnp.ones((nb, LANES), F32)
    h0 = (h0f_ref[...], h0b_ref[...])
    init = tuple((h0[d] if j == 0 else jnp.zeros((nb, LANES), F32), ones)
                 for d in range(2) for j in range(LRU_SEGMENTS - 1))
    ends = lax.fori_loop(0, ts // LRU_UNROLL, compose, init)
    starts = []
    for d in range(2):
        h = h0[d]
        for j in range(LRU_SEGMENTS):
            starts.append(h)
            if j < LRU_SEGMENTS - 1:
                g, big_a = ends[d * (LRU_SEGMENTS - 1) + j]
                h = g if j == 0 else g + big_a * h

    def rerun(i, carry):
        state = list(carry)
        for d, h_s in enumerate((hf_s, hb_s)):
            for j in range(LRU_SEGMENTS):
                n = d * LRU_SEGMENTS + j
                a_blk = refs[d][0][slab_rows(d, j, i), :]
                u_blk = refs[d][1][slab_rows(d, j, i), :]
                hs = []
                for k in range(LRU_UNROLL):
                    state[n] = step_of(d, a_blk, k) * state[n] + step_of(d, u_blk, k)
                    hs.append(state[n])
                h_s[slab_rows(d, j, i), :] = jnp.concatenate(hs if d == 0 else hs[::-1], axis=0)
        return tuple(state)

    final = lax.fori_loop(0, ts // LRU_UNROLL, rerun, tuple(starts))
    ff_ref[...] = final[LRU_SEGMENTS - 1]
    fb_ref[...] = final[2 * LRU_SEGMENTS - 1]

    def add_directions(c, carry):
        rows = pl.ds(pl.multiple_of(c * rows_per_chunk, rows_per_chunk), rows_per_chunk)
        hf_s[rows, :] = hf_s[rows, :] + hb_s[rows, :]
        return carry

    lax.fori_loop(0, t // LRU_CHUNK, add_directions, 0)

    def emit(b, carry):
        hs = hf_s[pl.ds(b, t, stride=nb), :]
        o_ref[b] = (hs * _silu(gb_ref[b].astype(F32))).astype(o_ref.dtype)
        return carry

    lax.fori_loop(0, nb, emit, 0)


def _lru_branch(xb, gb, h0f, h0b, conv_w, conv_b, wg, lam):
    b, t, w = xb.shape
    nb = SUBLANES
    assert t % (LRU_SEGMENTS * LRU_UNROLL) == 0 and t % LRU_CHUNK == 0
    assert conv_w.shape[0] == CONV_PAD_L + 1 + CONV_PAD_R
    xspec = pl.BlockSpec((nb, t, LANES), lambda i, c: (i, 0, c))
    hspec = pl.BlockSpec((nb, LANES), lambda i, c: (i, c))
    scratch = [pltpu.VMEM(((t + CONV_PAD_L + CONV_PAD_R) * nb, LANES), F32)]
    scratch += [pltpu.VMEM((t * nb, LANES), F32) for _ in range(6)]
    return pl.pallas_call(
        _lru_kernel,
        grid=(b // nb, w // LANES),
        in_specs=[xspec, xspec, hspec, hspec,
                  pl.BlockSpec((conv_w.shape[0], LANES), lambda i, c: (0, c)),
                  pl.BlockSpec((1, LANES), lambda i, c: (0, c)),
                  pl.BlockSpec((1, 2 * LANES, 4 * LANES), lambda i, c: (c, 0, 0)),
                  pl.BlockSpec((2, LANES), lambda i, c: (0, c))],
        out_specs=[xspec, hspec, hspec],
        out_shape=[jax.ShapeDtypeStruct((b, t, w), BF16),
                   jax.ShapeDtypeStruct((b, w), F32),
                   jax.ShapeDtypeStruct((b, w), F32)],
        scratch_shapes=scratch,
        compiler_params=_params("parallel", "parallel"),
        name="lru_branch",
    )(xb, gb, h0f, h0b, conv_w, conv_b.reshape(1, w), wg, lam)


def _lru_gate_weights(wa, ba, wx, bx):
    def tiles(w):
        w2 = w.reshape(LRU_BLOCKS // 2, 2, LRU_BLOCK, LRU_BLOCK)
        z = jnp.zeros_like(w2[:, 0])
        return jnp.concatenate([jnp.concatenate([w2[:, 0], z], axis=2),
                                jnp.concatenate([z, w2[:, 1]], axis=2)], axis=1)
    wg = jnp.concatenate([tiles(wa[0]), tiles(wx[0]), tiles(wa[1]), tiles(wx[1])], axis=2)
    n_tiles = LRU_WIDTH // LANES
    bg = jnp.stack([ba[0], bx[0], ba[1], bx[1]], axis=0).reshape(4, n_tiles, LANES)
    bg = 0.5 * jnp.transpose(bg, (1, 0, 2)).reshape(n_tiles, 1, 4 * LANES)
    hi = bg.astype(BF16)
    lo = (bg - hi.astype(F32)).astype(BF16)
    rest = jnp.zeros((n_tiles, LANES - 2, 4 * LANES), BF16)
    return jnp.concatenate([wg.astype(BF16), hi, lo, rest], axis=1)


def _packed_tiles(ref, n_heads, head, bi=0):
    n_tiles = ref.shape[1] // (n_heads * KEY_TILE)
    return [ref[bi, pl.ds(c * KEY_TILE * n_heads + head, KEY_TILE, stride=n_heads), :].astype(BF16)
            for c in range(n_tiles)]


def _diff_kernel(*refs, packed, n_heads, lam_init):
    n_kv = len(packed)
    q_ref = refs[0]
    kv_refs = refs[1:1 + 2 * n_kv]
    g_ref, lamp_ref, subg_ref, o_ref, s_ref = refs[1 + 2 * n_kv:]
    low = _lane_low_half()
    tq = q_ref.shape[1]
    first_head = pl.program_id(1) * (q_ref.shape[2] // LANES)
    lp = lamp_ref[...]
    lam = (jnp.exp(jnp.sum(lp[0:1] * lp[1:2], axis=-1, keepdims=True))
           - jnp.exp(jnp.sum(lp[2:3] * lp[3:4], axis=-1, keepdims=True)) + lam_init)
    sub_gain = subg_ref[...] * (1.0 - lam_init)
    nb = q_ref.shape[0]
    tiles = [(bi, j) for bi in range(nb) for j in range(0, q_ref.shape[2], LANES)]
    q_blocks, kv_blocks = [], []
    for bi, j in tiles:
        k_tiles, v_tiles = [], []
        for i in range(n_kv):
            if packed[i]:
                head = first_head + j // LANES
                k_tiles += _packed_tiles(kv_refs[2 * i], n_heads, head, bi)
                v_tiles += _packed_tiles(kv_refs[2 * i + 1], n_heads, head, bi)
            else:
                n_tiles = kv_refs[2 * i].shape[1] // KEY_TILE
                k_tiles += _tiles(kv_refs[2 * i], n_tiles, j, bi)
                v_tiles += _tiles(kv_refs[2 * i + 1], n_tiles, j, bi)
        v_tiles = _values_and_ones(v_tiles)
        for r in range(0, tq, ROW_BLK):
            q_blocks += list(_split_lane_halves(q_ref[bi, r:r + ROW_BLK, j:j + LANES], low))
            kv_blocks += [(k_tiles, v_tiles)] * 2
    res = _attend(q_blocks, kv_blocks, s_ref)
    i = 0
    for bi, j in tiles:
        for r in range(0, tq, ROW_BLK):
            acc1, acc2 = res[i], res[i + 1]
            i += 2
            o = acc1[:, :LANES] / acc1[:, LANES:] - acc2[:, :LANES] * (lam / acc2[:, LANES:])
            o = o * lax.rsqrt(jnp.mean(o * o, axis=-1, keepdims=True) + EPS) * sub_gain
            gate = _silu(g_ref[bi, r:r + ROW_BLK, j:j + LANES].astype(F32))
            o_ref[bi, r:r + ROW_BLK, j:j + LANES] = (o * gate).astype(o_ref.dtype)


def _diff_attn(q, kv_list, g, lam_params, sub_g, lam_init, tq, heads_per_step, batch_per_step=1):
    b, t, w = q.shape
    wb = heads_per_step * LANES
    nb = batch_per_step
    qspec = pl.BlockSpec((nb, tq, wb), lambda bi, h, qi: (bi, qi, h))
    n_heads = w // LANES
    in_specs = [qspec]
    args = [q]
    n_keys = 0
    for k, v, packed in kv_list:
        if packed:
            spec = pl.BlockSpec((nb, k.shape[1], LANES), lambda bi, h, qi: (bi, 0, 0))
            n_keys += k.shape[1] // n_heads
        else:
            spec = pl.BlockSpec((nb, k.shape[1], wb), lambda bi, h, qi: (bi, 0, h))
            n_keys += k.shape[1]
        in_specs += [spec, spec]
        args += [k, v]
    in_specs += [qspec,
                 pl.BlockSpec(lam_params.shape, lambda bi, h, qi: (0, 0)),
                 pl.BlockSpec((1, LANES), lambda bi, h, qi: (0, 0))]
    args += [g, lam_params, sub_g.reshape(1, LANES)]
    return pl.pallas_call(
        functools.partial(_diff_kernel, packed=tuple(p for _, _, p in kv_list), n_heads=n_heads,
                          lam_init=lam_init),
        grid=(b // nb, w // wb, t // tq),
        in_specs=in_specs,
        out_specs=qspec,
        out_shape=jax.ShapeDtypeStruct((b, t, w), BF16),
        scratch_shapes=[pltpu.VMEM((SCORE_SLOTS * ROW_BLK, n_keys), F32)],
        compiler_params=_params("parallel", "parallel", "parallel"),
        name="diff_attn",
    )(*args)


def _rope_tables(t_len):
    m = DIFF_HEAD_DIM // 4
    t = jnp.arange(t_len)
    rows = (t // GRID_W).astype(F32)
    cols = (t % GRID_W).astype(F32)
    inv = ROPE_BASE ** (-jnp.arange(m, dtype=F32) / m)
    ang_r = rows[:, None] * inv[None, :]
    ang_c = cols[:, None] * inv[None, :]
    cos = jnp.concatenate([jnp.cos(ang_r)] * 2 + [jnp.cos(ang_c)] * 2, axis=1)
    sin = jnp.concatenate([-jnp.sin(ang_r), jnp.sin(ang_r), -jnp.sin(ang_c), jnp.sin(ang_c)], axis=1)
    reps = LANES // DIFF_HEAD_DIM
    return jnp.tile(cos, (1, reps)), jnp.tile(sin, (1, reps))


def kernel(x_prompt, x_sample, c, cache_na_k, cache_na_v, state_lru, cache_diff_k, cache_diff_v, c_ctx,
           e_norm, e_ada_w, e_ada_b, e_w_in, e_rpb, e_conv_w, e_conv_b, e_lru_wa, e_lru_ba, e_lru_wx, e_lru_bx,
           e_lru_lam, e_w_out, o_norm, o_ada_w, o_ada_b, o_w_in, o_lq1, o_lk1, o_lq2, o_lk2, o_sub_g, o_w_out,
           final_norm):
    bp, tp, d = x_prompt.shape
    bs, ts, _ = x_sample.shape
    past = cache_na_k.shape[2]
    pad_rows = 2 * SUBLANES - bs - 1
    cc = jnp.concatenate([c, c_ctx[None, :], jnp.zeros((pad_rows, d), F32)], axis=0)
    nw, lw, dw = NA_WIDTH, LRU_WIDTH, DIFF_WIDTH
    rope_tables = _rope_tables(ts) if DEPTH > 1 else None

    def layer_params(i):
        j = i // 2
        if i % 2 == 0:
            return e_ada_w[j], e_ada_b[j], e_norm[j], e_w_in[j], e_w_out[j]
        return o_ada_w[j], o_ada_b[j], o_norm[j], o_w_in[j], o_w_out[j]

    mods = []
    for i in range(DEPTH):
        ada_w, ada_b = layer_params(i)[:2]
        m = _modulation(cc, ada_w, ada_b)
        mods.append((m[:bs].reshape(bs, 1, 3 * d), m[bs:bs + 1].reshape(1, 1, 3 * d)))

    def projection(i, latent):
        norm_g, w_in = layer_params(i)[2:4]
        mod = mods[i][0 if latent else 1]
        kv_dtype = BF16 if latent else F32
        if i % 2 == 0:
            qmul = NA_HEAD_DIM ** -0.5 * LOG2E
            outs = [(nw, BF16, False, qmul, False), (nw, kv_dtype, False, None, False),
                    (nw, kv_dtype, False, None, False), (nw, BF16, False, None, False),
                    (lw, F32, False, None, False), (lw, BF16, False, None, False)]
            return mod, norm_g, w_in, outs, None
        qmul = DIFF_HEAD_DIM ** -0.5 * LOG2E
        outs = [(dw, BF16, latent, qmul, False), (dw, kv_dtype, latent, None, not latent),
                (dw, kv_dtype, False, None, not latent), (dw, BF16, False, None, False)]
        return mod, norm_g, w_in, outs, rope_tables if latent else None

    xp, xs = x_prompt, x_sample
    _, proj_p = _stage(xp, nxt=projection(0, False), tm=512)
    _, proj_s = _stage(xs, nxt=projection(0, True), tm=512)
    na_k, na_v, lru_s, df_k, df_v = [], [], [], [], []
    for i in range(DEPTH):
        j = i // 2
        w_out = layer_params(i)[4]
        if i % 2 == 0:
            wg = _lru_gate_weights(e_lru_wa[j], e_lru_ba[j], e_lru_wx[j], e_lru_bx[j])

            q, k, v, ga, xb, gb = proj_p
            oa = _ctx_attn(q, k, v, ga)
            zeros = jnp.zeros((bp, lw), F32)
            ob, ff, fb = _lru_branch(xb, gb, zeros, zeros, e_conv_w[j], e_conv_b[j], wg, e_lru_lam[j])
            mixed_p = [oa, ob]
            na_k.append(k.reshape(bp, tp, NA_HEADS, NA_HEAD_DIM))
            na_v.append(v.reshape(bp, tp, NA_HEADS, NA_HEAD_DIM))
            lru_s.append(jnp.stack([ff, fb], axis=1))

            q, k, v, ga, xb, gb = proj_s
            ck = cache_na_k[:, j].reshape(bs, past, nw)
            cv = cache_na_v[:, j].reshape(bs, past, nw)
            oa = _na_attn(q, k, v, ck, cv, e_rpb[j], ga)
            ob, _, _ = _lru_branch(xb, gb, state_lru[:, j, 0], state_lru[:, j, 1],
                                   e_conv_w[j], e_conv_b[j], wg, e_lru_lam[j])
            mixed_s = [oa, ob]
        else:
            lam_init = 0.8 - 0.6 * math.exp(-0.3 * i)
            lam_params = jnp.stack([o_lq1[j], o_lk1[j], o_lq2[j], o_lk2[j]], axis=0)

            q, k, v, g = proj_p
            mixed_p = [_diff_attn(q, [(k, v, True)], g, lam_params, o_sub_g[j], lam_init, tq=tp,
                                  heads_per_step=DIFF_HEADS, batch_per_step=CTX_BATCH_PER_STEP)]
            df_k.append(k.reshape(bp, tp, DIFF_HEADS, 2 * DIFF_HEAD_DIM))
            df_v.append(v.reshape(bp, tp, DIFF_HEADS, 2 * DIFF_HEAD_DIM))

            q, k, v, g = proj_s
            ck = cache_diff_k[:, j].reshape(bs, past * DIFF_HEADS, 2 * DIFF_HEAD_DIM)
            cv = cache_diff_v[:, j].reshape(bs, past * DIFF_HEADS, 2 * DIFF_HEAD_DIM)
            mixed_s = [_diff_attn(q, [(ck, cv, True), (k, v, False)], g, lam_params, o_sub_g[j], lam_init, tq=ts,
                                  heads_per_step=4)]

        if i == DEPTH - 1:
            xp, _ = _stage(xp, prev=(mixed_p, w_out, mods[i][1]), final_g=final_norm, tm=512)
            xs, _ = _stage(xs, prev=(mixed_s, w_out, mods[i][0]), final_g=final_norm, tm=1024)
        else:
            xp, proj_p = _stage(xp, prev=(mixed_p, w_out, mods[i][1]), nxt=projection(i + 1, False), tm=512)
            xs, proj_s = _stage(xs, prev=(mixed_s, w_out, mods[i][0]), nxt=projection(i + 1, True), tm=512)

    return (xp, xs, jnp.stack(na_k, axis=1), jnp.stack(na_v, axis=1), jnp.stack(lru_s, axis=1),
            jnp.stack(df_k, axis=1), jnp.stack(df_v, axis=1))
```

```python
import functools
import math

import jax
import jax.numpy as jnp
from jax import lax
from jax.experimental import pallas as pl
from jax.experimental.pallas import tpu as pltpu

DEPTH = 2
GRID_W = 64
NA_HEADS = 8
NA_HEAD_DIM = 64
NA_WIDTH = NA_HEADS * NA_HEAD_DIM
NA_ROWS_MAX = 8
NA_COLS = 16
RPB_ROWS = 2 * NA_ROWS_MAX - 1
RPB_COLS = 2 * NA_COLS - 1
LRU_WIDTH = 512
LRU_BLOCKS = 8
LRU_BLOCK = LRU_WIDTH // LRU_BLOCKS
LRU_C = 8.0
DIFF_HEADS = 8
DIFF_HEAD_DIM = 64
DIFF_WIDTH = DIFF_HEADS * 2 * DIFF_HEAD_DIM
ROPE_BASE = 10000.0
EPS = 1e-6
LOG2E = math.log2(math.e)

LANES = 128
SUBLANES = 8
MXU_DIM = 256
VMEM_LIMIT = 56 * 1024 * 1024

CAST_COLS = 512
LRU_SEGMENTS = 4
LRU_UNROLL = 8
LRU_CHUNK = 256
CONV_PAD_L = 2
CONV_PAD_R = 1
ROW_BLK = MXU_DIM
KEY_TILE = MXU_DIM
SCORE_SLOTS = 4

NA_Q_ROWS = 4
NA_K_ROWS = 12
NA_PAIR_TILES = 16
NA_PAIRS_PER_STEP = 4
CTX_BATCH_PER_STEP = 4

F32 = jnp.float32
BF16 = jnp.bfloat16


def _params(*semantics):
    return pltpu.CompilerParams(dimension_semantics=semantics, vmem_limit_bytes=VMEM_LIMIT)


def _silu(x):
    h = 0.5 * x
    return h * jnp.tanh(h) + h


def _dot(a, b):
    return jnp.dot(a, b, preferred_element_type=F32)


def _dot_nt(a, b):
    return lax.dot_general(a, b, (((1,), (1,)), ((), ())), preferred_element_type=F32)


def _mod_kernel(c_ref, w_ref, b_ref, o_ref):
    s = _silu(c_ref[...])
    o_ref[...] = _dot(s.astype(BF16), w_ref[...].astype(BF16)) + b_ref[...]


def _modulation(cc, w, b):
    rows, d = cc.shape
    n = w.shape[1]
    tn = 1024
    return pl.pallas_call(
        _mod_kernel,
        grid=(n // tn,),
        in_specs=[pl.BlockSpec((rows, d), lambda j: (0, 0)),
                  pl.BlockSpec((d, tn), lambda j: (0, j)),
                  pl.BlockSpec((1, tn), lambda j: (0, j))],
        out_specs=pl.BlockSpec((rows, tn), lambda j: (0, j)),
        out_shape=jax.ShapeDtypeStruct((rows, n), F32),
        compiler_params=_params("arbitrary"),
        name="modulation",
    )(cc, w, b.reshape(1, n))


def _rope_block(x, cos, sin_signed, low16):
    partner = jnp.where(low16, pltpu.roll(x, LANES - 16, axis=1), pltpu.roll(x, 16, axis=1))
    return x * cos + partner * sin_signed


def _cast_weights_once(w_ref, wbf_ref):
    @pl.when((pl.program_id(0) == 0) & (pl.program_id(1) == 0))
    def _():
        for off in range(0, w_ref.shape[1], CAST_COLS):
            wbf_ref[:, off:off + CAST_COLS] = w_ref[:, off:off + CAST_COLS].astype(BF16)


def _norm_project(x, mod_ref, g_ref, wbf_ref, rope_refs, out_refs, outs):
    tm, d = x.shape
    y = x * lax.rsqrt(jnp.mean(x * x, axis=-1, keepdims=True) + EPS) * g_ref[...]
    mod = mod_ref[0]
    h = (y * (1.0 + mod[:, d:2 * d]) + mod[:, :d]).astype(BF16)
    if rope_refs:
        cos = rope_refs[0][...]
        sin = rope_refs[1][...]
        lane = lax.broadcasted_iota(jnp.int32, (1, LANES), 1)
        low16 = (lane & 16) == 0
    off = 0
    for o_ref, (width, roped, mul, packed) in zip(out_refs, outs):
        acc = _dot(h, wbf_ref[:, off:off + width])
        if roped:
            acc = jnp.concatenate(
                [_rope_block(acc[:, j:j + LANES], cos, sin, low16) for j in range(0, width, LANES)], axis=1)
        if mul is not None:
            acc = acc * mul
        if packed:
            n_heads = width // LANES
            t = o_ref.shape[1] // n_heads
            for bi in range(o_ref.shape[0]):
                for hh in range(n_heads):
                    o_ref[bi, pl.ds(hh, t, stride=n_heads), :] = (
                        acc[bi * t:(bi + 1) * t, hh * LANES:(hh + 1) * LANES].astype(o_ref.dtype))
        else:
            o_ref[...] = acc.astype(o_ref.dtype).reshape(o_ref.shape)
        off += width


def _token_rows(ref):
    return ref[...].reshape(-1, ref.shape[-1])


def _gated_residual(o_refs, wbf_ref, x_ref, mod_ref):
    d = x_ref.shape[-1]
    acc, row = None, 0
    for o_ref in o_refs:
        k = o_ref.shape[-1]
        part = _dot(_token_rows(o_ref), wbf_ref[row:row + k, :])
        acc = part if acc is None else acc + part
        row += k
    return _token_rows(x_ref) + mod_ref[0][:, 2 * d:] * acc


def _stage_kernel(*refs, n_o, final, outs, rope):
    refs = list(refs)
    take = lambda n: [refs.pop(0) for _ in range(n)]
    if n_o:
        o_refs = take(n_o)
        wout_ref, x_ref, mod_prev_ref = take(3)
        fin_ref = take(1)[0] if final else None
    else:
        x_ref, = take(1)
    if outs:
        mod_ref, g_ref, win_ref = take(3)
        rope_refs = take(2) if rope else []
    xout_ref = take(1)[0] if n_o else None
    out_refs = take(len(outs))
    if n_o:
        wout_bf = take(1)[0]
        _cast_weights_once(wout_ref, wout_bf)
        x = _gated_residual(o_refs, wout_bf, x_ref, mod_prev_ref)
        if final:
            x_new = x * lax.rsqrt(jnp.mean(x * x, axis=-1, keepdims=True) + EPS) * fin_ref[...]
        else:
            x_new = x
        xout_ref[...] = x_new.reshape(xout_ref.shape)
    else:
        x = _token_rows(x_ref)
    if outs:
        win_bf = take(1)[0]
        _cast_weights_once(win_ref, win_bf)
        _norm_project(x, mod_ref, g_ref, win_bf, rope_refs, out_refs, outs)


def _stage(x, prev=None, nxt=None, final_g=None, tm=256):
    b, t, d = x.shape
    shared_mod = all(m.shape[0] == 1 for m in ([prev[2]] if prev else []) + ([nxt[0]] if nxt else []))
    bb = max(1, tm // t) if shared_mod else 1
    tm = min(tm, t)
    tile = lambda width: pl.BlockSpec((bb, tm, width), lambda i, j: (i, j, 0))
    mod_spec = lambda mod: pl.BlockSpec(
        (1, 1, 3 * d), (lambda i, j: (i, 0, 0)) if mod.shape[0] > 1 else (lambda i, j: (0, 0, 0)))
    resident = lambda w: pl.BlockSpec(w.shape, lambda i, j: (0, 0), pipeline_mode=pl.Buffered(1))
    row_spec = pl.BlockSpec((1, d), lambda i, j: (0, 0))
    in_specs, args, out_specs, out_shape, scratch = [], [], [], [], []
    n_o, outs, rope = 0, (), False
    if prev is not None:
        o_list, w_out, mod_prev = prev
        n_o = len(o_list)
        in_specs += [tile(o.shape[-1]) for o in o_list] + [resident(w_out), tile(d), mod_spec(mod_prev)]
        args += list(o_list) + [w_out, x, mod_prev]
        if final_g is not None:
            in_specs.append(row_spec)
            args.append(final_g.reshape(1, d))
        out_specs.append(tile(d))
        out_shape.append(jax.ShapeDtypeStruct((b, t, d), F32))
        scratch.append(pltpu.VMEM(w_out.shape, BF16))
    else:
        in_specs.append(tile(d))
        args.append(x)
    if nxt is not None:
        mod_next, norm_g, w_in, out_list, rope_tables = nxt
        in_specs += [mod_spec(mod_next), row_spec, resident(w_in)]
        args += [mod_next, norm_g.reshape(1, d), w_in]
        rope = rope_tables is not None
        if rope:
            in_specs += [pl.BlockSpec((tm, LANES), lambda i, j: (j, 0))] * 2
            args += list(rope_tables)
        for width, dt, _, _, packed in out_list:
            if packed:
                heads = width // LANES
                out_specs.append(pl.BlockSpec((bb, tm * heads, LANES), lambda i, j: (i, j, 0)))
                out_shape.append(jax.ShapeDtypeStruct((b, t * heads, LANES), dt))
            else:
                out_specs.append(tile(width))
                out_shape.append(jax.ShapeDtypeStruct((b, t, width), dt))
        outs = tuple((w_, r, m, p) for w_, _, r, m, p in out_list)
        scratch.append(pltpu.VMEM(w_in.shape, BF16))
    res = pl.pallas_call(
        functools.partial(_stage_kernel, n_o=n_o, final=final_g is not None, outs=outs, rope=rope),
        grid=(b // bb, t // tm),
        in_specs=in_specs,
        out_specs=out_specs,
        out_shape=out_shape,
        scratch_shapes=scratch,
        compiler_params=_params("arbitrary", "arbitrary"),
        name="stage",
    )(*args)
    return (res[0], list(res[1:])) if n_o else (None, list(res))


def _attend(q_blocks, kv_blocks, s_ref, score_fn=None):
    n_rb = len(q_blocks)

    def tile_of(rb, c):
        slot = rb % SCORE_SLOTS
        return (slice(slot * ROW_BLK, (slot + 1) * ROW_BLK), slice(c * KEY_TILE, (c + 1) * KEY_TILE))

    def score_tile(rb, c):
        s_ref[tile_of(rb, c)] = _dot_nt(q_blocks[rb], kv_blocks[rb][0][c])

    def max_tile(rb, c, m_acc):
        s = s_ref[tile_of(rb, c)]
        adjusted = None if score_fn is None else score_fn(rb, c, s)
        if adjusted is not None:
            s = adjusted
            s_ref[tile_of(rb, c)] = s
        mt = jnp.maximum(s[:, :LANES], s[:, LANES:])
        return mt if m_acc is None else jnp.maximum(m_acc, mt)

    def n_tiles(rb):
        return len(kv_blocks[rb][0]) if 0 <= rb < n_rb else 0

    outs = []
    row_max = {}
    for step in range(n_rb + 2):
        rb_score, rb_max, rb_exp = step, step - 1, step - 2
        m_acc, acc = None, None
        for c in range(max(n_tiles(rb_score), n_tiles(rb_max), n_tiles(rb_exp))):
            if c < n_tiles(rb_score):
                score_tile(rb_score, c)
            if c < n_tiles(rb_max):
                m_acc = max_tile(rb_max, c, m_acc)
            if c < n_tiles(rb_exp):
                e = jnp.exp2(s_ref[tile_of(rb_exp, c)] - row_max[rb_exp])
                pv = _dot(e.astype(BF16), kv_blocks[rb_exp][1][c])
                acc = pv if acc is None else acc + pv
        if m_acc is not None:
            row_max[rb_max] = m_acc.max(axis=-1, keepdims=True)
        if acc is not None:
            outs.append(acc)
    return outs


def _lane_low_half():
    return lax.broadcasted_iota(jnp.int32, (1, LANES), 1) < (LANES // 2)


def _values_and_ones(v_tiles):
    ones = jnp.ones((KEY_TILE, LANES), BF16)
    return [jnp.concatenate([v, ones], axis=1) for v in v_tiles]


def _normalise_pair(acc_lo, acc_hi, low):
    return jnp.where(low, acc_lo[:, :LANES] / acc_lo[:, LANES:], acc_hi[:, :LANES] / acc_hi[:, LANES:])


def _split_lane_halves(q, low):
    zero = jnp.zeros_like(q)
    return jnp.where(low, q, zero), jnp.where(low, zero, q)


def _tiles(ref, n_tiles, j, bi=0):
    return [ref[bi, c * KEY_TILE:(c + 1) * KEY_TILE, j:j + LANES].astype(BF16) for c in range(n_tiles)]


def _ctx_attn_kernel(q_ref, k_ref, v_ref, g_ref, o_ref, s_ref):
    low = _lane_low_half()
    nb, tq, w = q_ref.shape
    n_tiles = k_ref.shape[1] // KEY_TILE
    tiles = [(bi, j, r) for bi in range(nb) for j in range(0, w, LANES) for r in range(0, tq, ROW_BLK)]
    q_blocks, kv_blocks, kv = [], [], {}
    for bi, j, r in tiles:
        if (bi, j) not in kv:
            kv[bi, j] = (_tiles(k_ref, n_tiles, j, bi), _values_and_ones(_tiles(v_ref, n_tiles, j, bi)))
        q_blocks += list(_split_lane_halves(q_ref[bi, r:r + ROW_BLK, j:j + LANES], low))
        kv_blocks += [kv[bi, j]] * 2
    res = _attend(q_blocks, kv_blocks, s_ref)
    for i, (bi, j, r) in enumerate(tiles):
        o = _normalise_pair(res[2 * i], res[2 * i + 1], low)
        gate = _silu(g_ref[bi, r:r + ROW_BLK, j:j + LANES].astype(F32))
        o_ref[bi, r:r + ROW_BLK, j:j + LANES] = (o * gate).astype(o_ref.dtype)


def _ctx_attn(q, k, v, g):
    b, t, w = q.shape
    nb = CTX_BATCH_PER_STEP
    spec = pl.BlockSpec((nb, t, w), lambda i: (i, 0, 0))
    return pl.pallas_call(
        _ctx_attn_kernel,
        grid=(b // nb,),
        in_specs=[spec] * 4,
        out_specs=spec,
        out_shape=jax.ShapeDtypeStruct((b, t, w), BF16),
        scratch_shapes=[pltpu.VMEM((SCORE_SLOTS * ROW_BLK, k.shape[1]), F32)],
        compiler_params=_params("parallel"),
        name="ctx_attn",
    )(q, k, v, g)


def _na_pair_tiles(rpb_ref, head, pt_ref):
    right = lax.broadcasted_iota(jnp.int32, (GRID_W, LANES), 1) >= GRID_W
    zero_row = jnp.zeros((1, LANES), F32)

    def toeplitz(row, lane0):
        shift = (lane0 - (NA_COLS - 1)) % LANES
        return pltpu.roll(jnp.broadcast_to(row, (GRID_W, LANES)), shift, axis=1, stride=1, stride_axis=0)

    for p in range(NA_PAIR_TILES):
        left_row = rpb_ref[head, p - 1:p, :] if p >= 1 else zero_row
        right_row = rpb_ref[head, p:p + 1, :]
        pt_ref[head, p] = jnp.where(right, toeplitz(right_row, GRID_W), toeplitz(left_row, 0)) * LOG2E


def _na_row_start(r, n_rows):
    kr = min(NA_ROWS_MAX, n_rows)
    return min(max(r - kr // 2, 0), n_rows - kr)


def _na_window(grp, n_rows):
    kr = min(NA_ROWS_MAX, n_rows)
    starts = [_na_row_start(grp * NA_Q_ROWS + i, n_rows) for i in range(NA_Q_ROWS)]
    rows_per_tile = KEY_TILE // GRID_W
    need = -(-(max(starts) + kr - min(starts)) // rows_per_tile) * rows_per_tile
    return min(min(starts), n_rows - need), need


def _na_kernel(q_ref, k_ref, v_ref, ck_ref, cv_ref, rpb_ref, g_ref, o_ref, pt_ref, badd_ref, s_ref, *, n_rows):
    kr = min(NA_ROWS_MAX, n_rows)
    n_grp = n_rows // NA_Q_ROWS
    n_pairs = q_ref.shape[2] // LANES
    low = _lane_low_half()

    @pl.when(pl.program_id(1) == 0)
    def _build_bias():
        for head in range(2 * n_pairs):
            _na_pair_tiles(rpb_ref, head, pt_ref)
        qcol = lax.broadcasted_iota(jnp.int32, (GRID_W, LANES), 0)
        lane = lax.broadcasted_iota(jnp.int32, (GRID_W, LANES), 1)
        right = lane >= GRID_W
        kcol = lane & (GRID_W - 1)
        col_start = jnp.clip(qcol - NA_COLS // 2, 0, GRID_W - NA_COLS)
        col_ok = (kcol >= col_start) & (kcol < col_start + NA_COLS)
        for grp in range(n_grp):
            krow0, key_rows = _na_window(grp, n_rows)
            for i in range(NA_Q_ROWS):
                r = grp * NA_Q_ROWS + i
                row_start = _na_row_start(r, n_rows)
                for m in range(key_rows // 2):
                    key_row = krow0 + 2 * m
                    ok_l = row_start <= key_row < row_start + kr
                    ok_r = row_start <= key_row + 1 < row_start + kr
                    rows = slice(i * GRID_W, (i + 1) * GRID_W)
                    cols = slice(m * LANES, (m + 1) * LANES)
                    if not (ok_l or ok_r):
                        for head in range(2 * n_pairs):
                            badd_ref[head * n_grp + grp, rows, cols] = jnp.full((GRID_W, LANES), -jnp.inf, F32)
                        continue
                    ok = col_ok
                    if not ok_r:
                        ok = ok & jnp.logical_not(right)
                    if not ok_l:
                        ok = ok & right
                    p = key_row - r + NA_ROWS_MAX
                    for head in range(2 * n_pairs):
                        badd_ref[head * n_grp + grp, rows, cols] = jnp.where(ok, pt_ref[head, p], -jnp.inf)

    n_ctx = ck_ref.shape[1] // KEY_TILE
    q_blocks, kv_blocks, blocks = [], [], []
    for pair in range(n_pairs):
        lanes = slice(pair * LANES, (pair + 1) * LANES)
        ctx_k = _tiles(ck_ref, n_ctx, pair * LANES)
        ctx_v = _tiles(cv_ref, n_ctx, pair * LANES)
        for grp in range(n_grp):
            krow0, key_rows = _na_window(grp, n_rows)
            k0 = krow0 * GRID_W
            win = [slice(k0 + c * KEY_TILE, k0 + (c + 1) * KEY_TILE) for c in range(key_rows * GRID_W // KEY_TILE)]
            kv = ([k_ref[0, w, lanes] for w in win] + ctx_k,
                  _values_and_ones([v_ref[0, w, lanes] for w in win] + ctx_v))
            q_blocks += list(_split_lane_halves(q_ref[0, grp * ROW_BLK:(grp + 1) * ROW_BLK, lanes], low))
            kv_blocks += [kv, kv]
            blocks += [(2 * pair, grp, len(win)), (2 * pair + 1, grp, len(win))]

    def masked_bias(rb, c, s):
        head, grp, n_win = blocks[rb]
        if c >= n_win:
            return None
        cols = slice(c * KEY_TILE, (c + 1) * KEY_TILE)
        return s + badd_ref[head * n_grp + grp, :, cols]

    res = _attend(q_blocks, kv_blocks, s_ref, masked_bias)
    for pair in range(n_pairs):
        lanes = slice(pair * LANES, (pair + 1) * LANES)
        for grp in range(n_grp):
            rb = 2 * (pair * n_grp + grp)
            o = _normalise_pair(res[rb], res[rb + 1], low)
            rows = slice(grp * ROW_BLK, (grp + 1) * ROW_BLK)
            o_ref[0, rows, lanes] = (o * _silu(g_ref[0, rows, lanes].astype(F32))).astype(o_ref.dtype)


def _na_attn(q, k, v, ck, cv, rpb, g):
    _, rpb_rows, rpb_cols = rpb.shape
    assert rpb_rows == RPB_ROWS and rpb_cols == RPB_COLS
    rpb = jnp.pad(rpb, ((0, 0), (0, NA_PAIR_TILES - rpb_rows), (0, LANES - rpb_cols)))
    b, t, w = q.shape
    c = ck.shape[1]
    n_rows = t // GRID_W
    assert NA_Q_ROWS * GRID_W == ROW_BLK and n_rows % NA_Q_ROWS == 0 and n_rows >= NA_K_ROWS
    n_grp = n_rows // NA_Q_ROWS
    n_win = NA_K_ROWS * GRID_W
    wb = NA_PAIRS_PER_STEP * LANES
    hb = 2 * NA_PAIRS_PER_STEP
    tspec = pl.BlockSpec((1, t, wb), lambda hp, bi: (bi, 0, hp))
    cspec = pl.BlockSpec((1, c, wb), lambda hp, bi: (bi, 0, hp))
    ptspec = pl.BlockSpec((hb, NA_PAIR_TILES, LANES), lambda hp, bi: (hp, 0, 0))
    return pl.pallas_call(
        functools.partial(_na_kernel, n_rows=n_rows),
        grid=(w // wb, b),
        in_specs=[tspec, tspec, tspec, cspec, cspec, ptspec, tspec],
        out_specs=tspec,
        out_shape=jax.ShapeDtypeStruct((b, t, w), BF16),
        scratch_shapes=[pltpu.VMEM((hb, NA_PAIR_TILES, GRID_W, LANES), F32),
                        pltpu.VMEM((hb * n_grp, ROW_BLK, n_win), F32),
                        pltpu.VMEM((SCORE_SLOTS * ROW_BLK, n_win + c), F32)],
        compiler_params=_params("parallel", "arbitrary"),
        name="na_attn",
    )(q, k, v, ck, cv, rpb, g)


def _softplus(x):
    return jnp.maximum(x, 0.0) + jnp.log(1.0 + jnp.exp(-jnp.abs(x)))


def _lru_kernel(x_ref, gb_ref, h0f_ref, h0b_ref, cw_ref, cb_ref, wg_ref, lam_ref,
                o_ref, ff_ref, fb_ref, xt_s, af_s, uf_s, ab_s, ub_s, hf_s, hb_s):
    nb, t, _ = x_ref.shape
    pad = CONV_PAD_L * nb
    cw = 0.5 * cw_ref[...]
    cb = 0.5 * cb_ref[...]
    wg = wg_ref[0]
    kk = (-0.5 * LRU_C * LOG2E) * _softplus(-lam_ref[...])

    xt_s[0:pad, :] = jnp.zeros((pad, LANES), F32)
    xt_s[pad + t * nb:pad + (t + CONV_PAD_R) * nb, :] = jnp.zeros((CONV_PAD_R * nb, LANES), F32)

    def to_time_major(b, carry):
        xt_s[pl.ds(pad + b, t, stride=nb), :] = x_ref[b]
        return carry

    lax.fori_loop(0, nb, to_time_major, 0)

    rows_per_chunk = LRU_CHUNK * nb
    lane = lax.broadcasted_iota(jnp.int32, (rows_per_chunk, LANES), 1)
    bias_cols = jnp.where(lane < 2, 1.0, 0.0).astype(BF16)

    def gates(c, carry):
        r0 = pl.multiple_of(c * rows_per_chunk, rows_per_chunk)

        def tap(i):
            return xt_s[pl.ds(r0 + i * nb, rows_per_chunk), :]

        xh = cw[0:1] * tap(0) + cw[1:2] * tap(1) + cw[2:3] * tap(2) + cw[3:4] * tap(3) + cb
        th = jnp.tanh(_dot(jnp.concatenate([xh.astype(BF16), bias_cols], axis=1), wg))
        rows = pl.ds(r0, rows_per_chunk)
        for d, (a_s, u_s) in enumerate(((af_s, uf_s), (ab_s, ub_s))):
            t_a = th[:, (2 * d) * LANES:(2 * d + 1) * LANES]
            t_x = th[:, (2 * d + 1) * LANES:(2 * d + 2) * LANES]
            a = jnp.exp2(kk[d:d + 1] * t_a + kk[d:d + 1])
            y = 1.0 - a * a
            sq = y * lax.rsqrt(jnp.maximum(y, 1e-30))
            a_s[rows, :] = a
            u_s[rows, :] = sq * (xh * t_x + xh)
        return carry

    lax.fori_loop(0, t // LRU_CHUNK, gates, 0)

    ts = t // LRU_SEGMENTS
    refs = ((af_s, uf_s), (ab_s, ub_s))

    slab = nb * LRU_UNROLL

    def slab_rows(d, j, i):
        base = (j * ts + i * LRU_UNROLL) * nb
        return pl.ds(pl.multiple_of(base if d == 0 else t * nb - slab - base, slab), slab)

    def step_of(d, block, k):
        kk_ = k if d == 0 else LRU_UNROLL - 1 - k
        return block[kk_ * nb:(kk_ + 1) * nb]

    def compose(i, carry):
        state = list(carry)
        for d in range(2):
            for j in range(LRU_SEGMENTS - 1):
                n = d * (LRU_SEGMENTS - 1) + j
                a_blk = refs[d][0][slab_rows(d, j, i), :]
                u_blk = refs[d][1][slab_rows(d, j, i), :]
                g, big_a = state[n]
                for k in range(LRU_UNROLL):
                    a = step_of(d, a_blk, k)
                    g, big_a = a * g + step_of(d, u_blk, k), a * big_a
                state[n] = (g, big_a)
        return tuple(state)

    ones = jnp.ones((nb, LANES), F32)
    h0 = (h0f_ref[...], h0b_ref[...])
    init = tuple((h0[d] if j == 0 else jnp.zeros((nb, LANES), F32), ones)
                 for d in range(2) for j in range(LRU_SEGMENTS - 1))
    ends = lax.fori_loop(0, ts // LRU_UNROLL, compose, init)
    starts = []
    for d in range(2):
        h = h0[d]
        for j in range(LRU_SEGMENTS):
            starts.append(h)
            if j < LRU_SEGMENTS - 1:
                g, big_a = ends[d * (LRU_SEGMENTS - 1) + j]
                h = g if j == 0 else g + big_a * h

    def rerun(i, carry):
        state = list(carry)
        for d, h_s in enumerate((hf_s, hb_s)):
            for j in range(LRU_SEGMENTS):
                n = d * LRU_SEGMENTS + j
                a_blk = refs[d][0][slab_rows(d, j, i), :]
                u_blk = refs[d][1][slab_rows(d, j, i), :]
                hs = []
                for k in range(LRU_UNROLL):
                    state[n] = step_of(d, a_blk, k) * state[n] + step_of(d, u_blk, k)
                    hs.append(state[n])
                h_s[slab_rows(d, j, i), :] = jnp.concatenate(hs if d == 0 else hs[::-1], axis=0)
        return tuple(state)

    final = lax.fori_loop(0, ts // LRU_UNROLL, rerun, tuple(starts))
    ff_ref[...] = final[LRU_SEGMENTS - 1]
    fb_ref[...] = final[2 * LRU_SEGMENTS - 1]

    def add_directions(c, carry):
        rows = pl.ds(pl.multiple_of(c * rows_per_chunk, rows_per_chunk), rows_per_chunk)
        hf_s[rows, :] = hf_s[rows, :] + hb_s[rows, :]
        return carry

    lax.fori_loop(0, t // LRU_CHUNK, add_directions, 0)

    def emit(b, carry):
        hs = hf_s[pl.ds(b, t, stride=nb), :]
        o_ref[b] = (hs * _silu(gb_ref[b].astype(F32))).astype(o_ref.dtype)
        return carry

    lax.fori_loop(0, nb, emit, 0)


def _lru_branch(xb, gb, h0f, h0b, conv_w, conv_b, wg, lam):
    b, t, w = xb.shape
    nb = SUBLANES
    assert t % (LRU_SEGMENTS * LRU_UNROLL) == 0 and t % LRU_CHUNK == 0
    assert conv_w.shape[0] == CONV_PAD_L + 1 + CONV_PAD_R
    xspec = pl.BlockSpec((nb, t, LANES), lambda i, c: (i, 0, c))
    hspec = pl.BlockSpec((nb, LANES), lambda i, c: (i, c))
    scratch = [pltpu.VMEM(((t + CONV_PAD_L + CONV_PAD_R) * nb, LANES), F32)]
    scratch += [pltpu.VMEM((t * nb, LANES), F32) for _ in range(6)]
    return pl.pallas_call(
        _lru_kernel,
        grid=(b // nb, w // LANES),
        in_specs=[xspec, xspec, hspec, hspec,
                  pl.BlockSpec((conv_w.shape[0], LANES), lambda i, c: (0, c)),
                  pl.BlockSpec((1, LANES), lambda i, c: (0, c)),
                  pl.BlockSpec((1, 2 * LANES, 4 * LANES), lambda i, c: (c, 0, 0)),
                  pl.BlockSpec((2, LANES), lambda i, c: (0, c))],
        out_specs=[xspec, hspec, hspec],
        out_shape=[jax.ShapeDtypeStruct((b, t, w), BF16),
                   jax.ShapeDtypeStruct((b, w), F32),
                   jax.ShapeDtypeStruct((b, w), F32)],
        scratch_shapes=scratch,
        compiler_params=_params("parallel", "parallel"),
        name="lru_branch",
    )(xb, gb, h0f, h0b, conv_w, conv_b.reshape(1, w), wg, lam)


def _lru_gate_weights(wa, ba, wx, bx):
    def tiles(w):
        w2 = w.reshape(LRU_BLOCKS // 2, 2, LRU_BLOCK, LRU_BLOCK)
        z = jnp.zeros_like(w2[:, 0])
        return jnp.concatenate([jnp.concatenate([w2[:, 0], z], axis=2),
                                jnp.concatenate([z, w2[:, 1]], axis=2)], axis=1)
    wg = jnp.concatenate([tiles(wa[0]), tiles(wx[0]), tiles(wa[1]), tiles(wx[1])], axis=2)
    n_tiles = LRU_WIDTH // LANES
    bg = jnp.stack([ba[0], bx[0], ba[1], bx[1]], axis=0).reshape(4, n_tiles, LANES)
    bg = 0.5 * jnp.transpose(bg, (1, 0, 2)).reshape(n_tiles, 1, 4 * LANES)
    hi = bg.astype(BF16)
    lo = (bg - hi.astype(F32)).astype(BF16)
    rest = jnp.zeros((n_tiles, LANES - 2, 4 * LANES), BF16)
    return jnp.concatenate([wg.astype(BF16), hi, lo, rest], axis=1)


def _packed_tiles(ref, n_heads, head, bi=0):
    n_tiles = ref.shape[1] // (n_heads * KEY_TILE)
    return [ref[bi, pl.ds(c * KEY_TILE * n_heads + head, KEY_TILE, stride=n_heads), :].astype(BF16)
            for c in range(n_tiles)]


def _diff_kernel(*refs, packed, n_heads, lam_init):
    n_kv = len(packed)
    q_ref = refs[0]
    kv_refs = refs[1:1 + 2 * n_kv]
    g_ref, lamp_ref, subg_ref, o_ref, s_ref = refs[1 + 2 * n_kv:]
    low = _lane_low_half()
    tq = q_ref.shape[1]
    first_head = pl.program_id(1) * (q_ref.shape[2] // LANES)
    lp = lamp_ref[...]
    lam = (jnp.exp(jnp.sum(lp[0:1] * lp[1:2], axis=-1, keepdims=True))
           - jnp.exp(jnp.sum(lp[2:3] * lp[3:4], axis=-1, keepdims=True)) + lam_init)
    sub_gain = subg_ref[...] * (1.0 - lam_init)
    nb = q_ref.shape[0]
    tiles = [(bi, j) for bi in range(nb) for j in range(0, q_ref.shape[2], LANES)]
    q_blocks, kv_blocks = [], []
    for bi, j in tiles:
        k_tiles, v_tiles = [], []
        for i in range(n_kv):
            if packed[i]:
                head = first_head + j // LANES
                k_tiles += _packed_tiles(kv_refs[2 * i], n_heads, head, bi)
                v_tiles += _packed_tiles(kv_refs[2 * i + 1], n_heads, head, bi)
            else:
                n_tiles = kv_refs[2 * i].shape[1] // KEY_TILE
                k_tiles += _tiles(kv_refs[2 * i], n_tiles, j, bi)
                v_tiles += _tiles(kv_refs[2 * i + 1], n_tiles, j, bi)
        v_tiles = _values_and_ones(v_tiles)
        for r in range(0, tq, ROW_BLK):
            q_blocks += list(_split_lane_halves(q_ref[bi, r:r + ROW_BLK, j:j + LANES], low))
            kv_blocks += [(k_tiles, v_tiles)] * 2
    res = _attend(q_blocks, kv_blocks, s_ref)
    i = 0
    for bi, j in tiles:
        for r in range(0, tq, ROW_BLK):
            acc1, acc2 = res[i], res[i + 1]
            i += 2
            o = acc1[:, :LANES] / acc1[:, LANES:] - acc2[:, :LANES] * (lam / acc2[:, LANES:])
            o = o * lax.rsqrt(jnp.mean(o * o, axis=-1, keepdims=True) + EPS) * sub_gain
            gate = _silu(g_ref[bi, r:r + ROW_BLK, j:j + LANES].astype(F32))
            o_ref[bi, r:r + ROW_BLK, j:j + LANES] = (o * gate).astype(o_ref.dtype)


def _diff_attn(q, kv_list, g, lam_params, sub_g, lam_init, tq, heads_per_step, batch_per_step=1):
    b, t, w = q.shape
    wb = heads_per_step * LANES
    nb = batch_per_step
    qspec = pl.BlockSpec((nb, tq, wb), lambda bi, h, qi: (bi, qi, h))
    n_heads = w // LANES
    in_specs = [qspec]
    args = [q]
    n_keys = 0
    for k, v, packed in kv_list:
        if packed:
            spec = pl.BlockSpec((nb, k.shape[1], LANES), lambda bi, h, qi: (bi, 0, 0))
            n_keys += k.shape[1] // n_heads
        else:
            spec = pl.BlockSpec((nb, k.shape[1], wb), lambda bi, h, qi: (bi, 0, h))
            n_keys += k.shape[1]
        in_specs += [spec, spec]
        args += [k, v]
    in_specs += [qspec,
                 pl.BlockSpec(lam_params.shape, lambda bi, h, qi: (0, 0)),
                 pl.BlockSpec((1, LANES), lambda bi, h, qi: (0, 0))]
    args += [g, lam_params, sub_g.reshape(1, LANES)]
    return pl.pallas_call(
        functools.partial(_diff_kernel, packed=tuple(p for _, _, p in kv_list), n_heads=n_heads,
                          lam_init=lam_init),
        grid=(b // nb, w // wb, t // tq),
        in_specs=in_specs,
        out_specs=qspec,
        out_shape=jax.ShapeDtypeStruct((b, t, w), BF16),
        scratch_shapes=[pltpu.VMEM((SCORE_SLOTS * ROW_BLK, n_keys), F32)],
        compiler_params=_params("parallel", "parallel", "parallel"),
        name="diff_attn",
    )(*args)


def _rope_tables(t_len):
    m = DIFF_HEAD_DIM // 4
    t = jnp.arange(t_len)
    rows = (t // GRID_W).astype(F32)
    cols = (t % GRID_W).astype(F32)
    inv = ROPE_BASE ** (-jnp.arange(m, dtype=F32) / m)
    ang_r = rows[:, None] * inv[None, :]
    ang_c = cols[:, None] * inv[None, :]
    cos = jnp.concatenate([jnp.cos(ang_r)] * 2 + [jnp.cos(ang_c)] * 2, axis=1)
    sin = jnp.concatenate([-jnp.sin(ang_r), jnp.sin(ang_r), -jnp.sin(ang_c), jnp.sin(ang_c)], axis=1)
    reps = LANES // DIFF_HEAD_DIM
    return jnp.tile(cos, (1, reps)), jnp.tile(sin, (1, reps))


def kernel(x_prompt, x_sample, c, cache_na_k, cache_na_v, state_lru, cache_diff_k, cache_diff_v, c_ctx,
           e_norm, e_ada_w, e_ada_b, e_w_in, e_rpb, e_conv_w, e_conv_b, e_lru_wa, e_lru_ba, e_lru_wx, e_lru_bx,
           e_lru_lam, e_w_out, o_norm, o_ada_w, o_ada_b, o_w_in, o_lq1, o_lk1, o_lq2, o_lk2, o_sub_g, o_w_out,
           final_norm):
    bp, tp, d = x_prompt.shape
    bs, ts, _ = x_sample.shape
    past = cache_na_k.shape[2]
    pad_rows = 2 * SUBLANES - bs - 1
    cc = jnp.concatenate([c, c_ctx[None, :], jnp.zeros((pad_rows, d), F32)], axis=0)
    nw, lw, dw = NA_WIDTH, LRU_WIDTH, DIFF_WIDTH
    rope_tables = _rope_tables(ts) if DEPTH > 1 else None

    def layer_params(i):
        j = i // 2
        if i % 2 == 0:
            return e_ada_w[j], e_ada_b[j], e_norm[j], e_w_in[j], e_w_out[j]
        return o_ada_w[j], o_ada_b[j], o_norm[j], o_w_in[j], o_w_out[j]

    mods = []
    for i in range(DEPTH):
        ada_w, ada_b = layer_params(i)[:2]
        m = _modulation(cc, ada_w, ada_b)
        mods.append((m[:bs].reshape(bs, 1, 3 * d), m[bs:bs + 1].reshape(1, 1, 3 * d)))

    def projection(i, latent):
        norm_g, w_in = layer_params(i)[2:4]
        mod = mods[i][0 if latent else 1]
        kv_dtype = BF16 if latent else F32
        if i % 2 == 0:
            qmul = NA_HEAD_DIM ** -0.5 * LOG2E
            outs = [(nw, BF16, False, qmul, False), (nw, kv_dtype, False, None, False),
                    (nw, kv_dtype, False, None, False), (nw, BF16, False, None, False),
                    (lw, F32, False, None, False), (lw, BF16, False, None, False)]
            return mod, norm_g, w_in, outs, None
        qmul = DIFF_HEAD_DIM ** -0.5 * LOG2E
        outs = [(dw, BF16, latent, qmul, False), (dw, kv_dtype, latent, None, not latent),
                (dw, kv_dtype, False, None, not latent), (dw, BF16, False, None, False)]
        return mod, norm_g, w_in, outs, rope_tables if latent else None

    xp, xs = x_prompt, x_sample
    _, proj_p = _stage(xp, nxt=projection(0, False), tm=512)
    _, proj_s = _stage(xs, nxt=projection(0, True), tm=512)
    na_k, na_v, lru_s, df_k, df_v = [], [], [], [], []
    for i in range(DEPTH):
        j = i // 2
        w_out = layer_params(i)[4]
        if i % 2 == 0:
            wg = _lru_gate_weights(e_lru_wa[j], e_lru_ba[j], e_lru_wx[j], e_lru_bx[j])

            q, k, v, ga, xb, gb = proj_p
            oa = _ctx_attn(q, k, v, ga)
            zeros = jnp.zeros((bp, lw), F32)
            ob, ff, fb = _lru_branch(xb, gb, zeros, zeros, e_conv_w[j], e_conv_b[j], wg, e_lru_lam[j])
            mixed_p = [oa, ob]
            na_k.append(k.reshape(bp, tp, NA_HEADS, NA_HEAD_DIM))
            na_v.append(v.reshape(bp, tp, NA_HEADS, NA_HEAD_DIM))
            lru_s.append(jnp.stack([ff, fb], axis=1))

            q, k, v, ga, xb, gb = proj_s
            ck = cache_na_k[:, j].reshape(bs, past, nw)
            cv = cache_na_v[:, j].reshape(bs, past, nw)
            oa = _na_attn(q, k, v, ck, cv, e_rpb[j], ga)
            ob, _, _ = _lru_branch(xb, gb, state_lru[:, j, 0], state_lru[:, j, 1],
                                   e_conv_w[j], e_conv_b[j], wg, e_lru_lam[j])
            mixed_s = [oa, ob]
        else:
            lam_init = 0.8 - 0.6 * math.exp(-0.3 * i)
            lam_params = jnp.stack([o_lq1[j], o_lk1[j], o_lq2[j], o_lk2[j]], axis=0)

            q, k, v, g = proj_p
            mixed_p = [_diff_attn(q, [(k, v, True)], g, lam_params, o_sub_g[j], lam_init, tq=tp,
                                  heads_per_step=DIFF_HEADS, batch_per_step=CTX_BATCH_PER_STEP)]
            df_k.append(k.reshape(bp, tp, DIFF_HEADS, 2 * DIFF_HEAD_DIM))
            df_v.append(v.reshape(bp, tp, DIFF_HEADS, 2 * DIFF_HEAD_DIM))

            q, k, v, g = proj_s
            ck = cache_diff_k[:, j].reshape(bs, past * DIFF_HEADS, 2 * DIFF_HEAD_DIM)
            cv = cache_diff_v[:, j].reshape(bs, past * DIFF_HEADS, 2 * DIFF_HEAD_DIM)
            mixed_s = [_diff_attn(q, [(ck, cv, True), (k, v, False)], g, lam_params, o_sub_g[j], lam_init, tq=ts,
                                  heads_per_step=4)]

        if i == DEPTH - 1:
            xp, _ = _stage(xp, prev=(mixed_p, w_out, mods[i][1]), final_g=final_norm, tm=1024)
            xs, _ = _stage(xs, prev=(mixed_s, w_out, mods[i][0]), final_g=final_norm, tm=1024)
        else:
            xp, proj_p = _stage(xp, prev=(mixed_p, w_out, mods[i][1]), nxt=projection(i + 1, False), tm=512)
            xs, proj_s = _stage(xs, prev=(mixed_s, w_out, mods[i][0]), nxt=projection(i + 1, True), tm=512)

    return (xp, xs, jnp.stack(na_k, axis=1), jnp.stack(na_v, axis=1), jnp.stack(lru_s, axis=1),
            jnp.stack(df_k, axis=1), jnp.stack(df_v, axis=1))
```

```python
import functools
import math

import jax
import jax.numpy as jnp
from jax import lax
from jax.experimental import pallas as pl
from jax.experimental.pallas import tpu as pltpu

DEPTH = 2
GRID_W = 64
NA_HEADS = 8
NA_HEAD_DIM = 64
NA_WIDTH = NA_HEADS * NA_HEAD_DIM
NA_ROWS_MAX = 8
NA_COLS = 16
RPB_ROWS = 2 * NA_ROWS_MAX - 1
RPB_COLS = 2 * NA_COLS - 1
LRU_WIDTH = 512
LRU_BLOCKS = 8
LRU_BLOCK = LRU_WIDTH // LRU_BLOCKS
LRU_C = 8.0
DIFF_HEADS = 8
DIFF_HEAD_DIM = 64
DIFF_WIDTH = DIFF_HEADS * 2 * DIFF_HEAD_DIM
ROPE_BASE = 10000.0
EPS = 1e-6
LOG2E = math.log2(math.e)

LANES = 128
SUBLANES = 8
MXU_DIM = 256
VMEM_LIMIT = 56 * 1024 * 1024

CAST_COLS = 512
LRU_SEGMENTS = 4
LRU_UNROLL = 8
LRU_CHUNK = 256
CONV_PAD_L = 2
CONV_PAD_R = 1
ROW_BLK = MXU_DIM
KEY_TILE = MXU_DIM
SCORE_SLOTS = 4

NA_Q_ROWS = 4
NA_K_ROWS = 12
NA_PAIR_TILES = 16
NA_PAIRS_PER_STEP = 4
CTX_BATCH_PER_STEP = 4

F32 = jnp.float32
BF16 = jnp.bfloat16


def _params(*semantics):
    return pltpu.CompilerParams(dimension_semantics=semantics, vmem_limit_bytes=VMEM_LIMIT)


def _silu(x):
    h = 0.5 * x
    return h * jnp.tanh(h) + h


def _dot(a, b):
    return jnp.dot(a, b, preferred_element_type=F32)


def _dot_nt(a, b):
    return lax.dot_general(a, b, (((1,), (1,)), ((), ())), preferred_element_type=F32)


def _mod_kernel(c_ref, w_ref, b_ref, o_ref):
    s = _silu(c_ref[...])
    o_ref[...] = _dot(s.astype(BF16), w_ref[...].astype(BF16)) + b_ref[...]


def _modulation(cc, w, b):
    rows, d = cc.shape
    n = w.shape[1]
    tn = 1024
    return pl.pallas_call(
        _mod_kernel,
        grid=(n // tn,),
        in_specs=[pl.BlockSpec((rows, d), lambda j: (0, 0)),
                  pl.BlockSpec((d, tn), lambda j: (0, j)),
                  pl.BlockSpec((1, tn), lambda j: (0, j))],
        out_specs=pl.BlockSpec((rows, tn), lambda j: (0, j)),
        out_shape=jax.ShapeDtypeStruct((rows, n), F32),
        compiler_params=_params("arbitrary"),
        name="modulation",
    )(cc, w, b.reshape(1, n))


def _rope_block(x, cos, sin_signed, low16):
    partner = jnp.where(low16, pltpu.roll(x, LANES - 16, axis=1), pltpu.roll(x, 16, axis=1))
    return x * cos + partner * sin_signed


def _cast_weights_once(w_ref, wbf_ref):
    @pl.when((pl.program_id(0) == 0) & (pl.program_id(1) == 0))
    def _():
        for off in range(0, w_ref.shape[1], CAST_COLS):
            wbf_ref[:, off:off + CAST_COLS] = w_ref[:, off:off + CAST_COLS].astype(BF16)


def _weight_chunk_copy(w_hbm, stage_ref, sem, c):
    cols = slice(c * CAST_COLS, (c + 1) * CAST_COLS)
    return pltpu.make_async_copy(w_hbm.at[:, cols], stage_ref.at[:, cols], sem.at[c])


def _land_weight_chunks(w_hbm, stage_ref, sem, wbf_ref, off, width):
    for c in range(off // CAST_COLS, (off + width) // CAST_COLS):
        _weight_chunk_copy(w_hbm, stage_ref, sem, c).wait()
        cols = slice(c * CAST_COLS, (c + 1) * CAST_COLS)
        wbf_ref[:, cols] = stage_ref[:, cols].astype(BF16)


def _norm_project(x, mod_ref, g_ref, wbf_ref, rope_refs, out_refs, outs, land=None):
    tm, d = x.shape
    y = x * lax.rsqrt(jnp.mean(x * x, axis=-1, keepdims=True) + EPS) * g_ref[...]
    mod = mod_ref[0]
    h = (y * (1.0 + mod[:, d:2 * d]) + mod[:, :d]).astype(BF16)
    if rope_refs:
        cos = rope_refs[0][...]
        sin = rope_refs[1][...]
        lane = lax.broadcasted_iota(jnp.int32, (1, LANES), 1)
        low16 = (lane & 16) == 0
    off = 0
    for o_ref, (width, roped, mul, packed) in zip(out_refs, outs):
        if land is not None:
            land(off, width)
        acc = _dot(h, wbf_ref[:, off:off + width])
        if roped:
            acc = jnp.concatenate(
                [_rope_block(acc[:, j:j + LANES], cos, sin, low16) for j in range(0, width, LANES)], axis=1)
        if mul is not None:
            acc = acc * mul
        if packed:
            n_heads = width // LANES
            t = o_ref.shape[1] // n_heads
            for bi in range(o_ref.shape[0]):
                for hh in range(n_heads):
                    o_ref[bi, pl.ds(hh, t, stride=n_heads), :] = (
                        acc[bi * t:(bi + 1) * t, hh * LANES:(hh + 1) * LANES].astype(o_ref.dtype))
        else:
            o_ref[...] = acc.astype(o_ref.dtype).reshape(o_ref.shape)
        off += width


def _token_rows(ref):
    return ref[...].reshape(-1, ref.shape[-1])


def _gated_residual(o_refs, wbf_ref, x_ref, mod_ref):
    d = x_ref.shape[-1]
    acc, row = None, 0
    for o_ref in o_refs:
        k = o_ref.shape[-1]
        part = _dot(_token_rows(o_ref), wbf_ref[row:row + k, :])
        acc = part if acc is None else acc + part
        row += k
    return _token_rows(x_ref) + mod_ref[0][:, 2 * d:] * acc


def _stage_kernel(*refs, n_o, final, outs, rope):
    refs = list(refs)
    take = lambda n: [refs.pop(0) for _ in range(n)]
    if n_o:
        o_refs = take(n_o)
        wout_ref, x_ref, mod_prev_ref = take(3)
        fin_ref = take(1)[0] if final else None
    else:
        x_ref, = take(1)
    if outs:
        mod_ref, g_ref, win_ref = take(3)
        rope_refs = take(2) if rope else []
    xout_ref = take(1)[0] if n_o else None
    out_refs = take(len(outs))
    wout_bf = take(1)[0] if n_o else None
    if n_o:
        _cast_weights_once(wout_ref, wout_bf)

    def step(land):
        if n_o:
            x = _gated_residual(o_refs, wout_bf, x_ref, mod_prev_ref)
            if final:
                x_new = x * lax.rsqrt(jnp.mean(x * x, axis=-1, keepdims=True) + EPS) * fin_ref[...]
            else:
                x_new = x
            xout_ref[...] = x_new.reshape(xout_ref.shape)
        else:
            x = _token_rows(x_ref)
        if outs:
            _norm_project(x, mod_ref, g_ref, win_bf, rope_refs, out_refs, outs, land)

    if not outs:
        step(None)
        return
    stage_ref, win_bf, sem = take(3)
    first = (pl.program_id(0) == 0) & (pl.program_id(1) == 0)

    @pl.when(first)
    def _():
        for c in range(win_ref.shape[1] // CAST_COLS):
            _weight_chunk_copy(win_ref, stage_ref, sem, c).start()
        step(functools.partial(_land_weight_chunks, win_ref, stage_ref, sem, win_bf))

    @pl.when(jnp.logical_not(first))
    def _():
        step(None)


def _stage(x, prev=None, nxt=None, final_g=None, tm=256):
    b, t, d = x.shape
    shared_mod = all(m.shape[0] == 1 for m in ([prev[2]] if prev else []) + ([nxt[0]] if nxt else []))
    bb = max(1, tm // t) if shared_mod else 1
    tm = min(tm, t)
    tile = lambda width: pl.BlockSpec((bb, tm, width), lambda i, j: (i, j, 0))
    mod_spec = lambda mod: pl.BlockSpec(
        (1, 1, 3 * d), (lambda i, j: (i, 0, 0)) if mod.shape[0] > 1 else (lambda i, j: (0, 0, 0)))
    resident = lambda w: pl.BlockSpec(w.shape, lambda i, j: (0, 0), pipeline_mode=pl.Buffered(1))
    row_spec = pl.BlockSpec((1, d), lambda i, j: (0, 0))
    in_specs, args, out_specs, out_shape, scratch = [], [], [], [], []
    n_o, outs, rope = 0, (), False
    if prev is not None:
        o_list, w_out, mod_prev = prev
        n_o = len(o_list)
        in_specs += [tile(o.shape[-1]) for o in o_list] + [resident(w_out), tile(d), mod_spec(mod_prev)]
        args += list(o_list) + [w_out, x, mod_prev]
        if final_g is not None:
            in_specs.append(row_spec)
            args.append(final_g.reshape(1, d))
        out_specs.append(tile(d))
        out_shape.append(jax.ShapeDtypeStruct((b, t, d), F32))
        scratch.append(pltpu.VMEM(w_out.shape, BF16))
    else:
        in_specs.append(tile(d))
        args.append(x)
    if nxt is not None:
        mod_next, norm_g, w_in, out_list, rope_tables = nxt
        assert all(wd % CAST_COLS == 0 for wd, _, _, _, _ in out_list)
        in_specs += [mod_spec(mod_next), row_spec, pl.BlockSpec(memory_space=pl.ANY)]
        args += [mod_next, norm_g.reshape(1, d), w_in]
        rope = rope_tables is not None
        if rope:
            in_specs += [pl.BlockSpec((tm, LANES), lambda i, j: (j, 0))] * 2
            args += list(rope_tables)
        for width, dt, _, _, packed in out_list:
            if packed:
                heads = width // LANES
                out_specs.append(pl.BlockSpec((bb, tm * heads, LANES), lambda i, j: (i, j, 0)))
                out_shape.append(jax.ShapeDtypeStruct((b, t * heads, LANES), dt))
            else:
                out_specs.append(tile(width))
                out_shape.append(jax.ShapeDtypeStruct((b, t, width), dt))
        outs = tuple((w_, r, m, p) for w_, _, r, m, p in out_list)
        scratch += [pltpu.VMEM(w_in.shape, F32), pltpu.VMEM(w_in.shape, BF16),
                    pltpu.SemaphoreType.DMA((w_in.shape[1] // CAST_COLS,))]
    res = pl.pallas_call(
        functools.partial(_stage_kernel, n_o=n_o, final=final_g is not None, outs=outs, rope=rope),
        grid=(b // bb, t // tm),
        in_specs=in_specs,
        out_specs=out_specs,
        out_shape=out_shape,
        scratch_shapes=scratch,
        compiler_params=_params("arbitrary", "arbitrary"),
        name="stage",
    )(*args)
    return (res[0], list(res[1:])) if n_o else (None, list(res))


def _attend(q_blocks, kv_blocks, s_ref, score_fn=None):
    n_rb = len(q_blocks)

    def tile_of(rb, c):
        slot = rb % SCORE_SLOTS
        return (slice(slot * ROW_BLK, (slot + 1) * ROW_BLK), slice(c * KEY_TILE, (c + 1) * KEY_TILE))

    def score_tile(rb, c):
        s_ref[tile_of(rb, c)] = _dot_nt(q_blocks[rb], kv_blocks[rb][0][c])

    def max_tile(rb, c, m_acc):
        s = s_ref[tile_of(rb, c)]
        adjusted = None if score_fn is None else score_fn(rb, c, s)
        if adjusted is not None:
            s = adjusted
            s_ref[tile_of(rb, c)] = s
        mt = jnp.maximum(s[:, :LANES], s[:, LANES:])
        return mt if m_acc is None else jnp.maximum(m_acc, mt)

    def n_tiles(rb):
        return len(kv_blocks[rb][0]) if 0 <= rb < n_rb else 0

    outs = []
    row_max = {}
    for step in range(n_rb + 2):
        rb_score, rb_max, rb_exp = step, step - 1, step - 2
        m_acc, acc = None, None
        for c in range(max(n_tiles(rb_score), n_tiles(rb_max), n_tiles(rb_exp))):
            if c < n_tiles(rb_score):
                score_tile(rb_score, c)
            if c < n_tiles(rb_max):
                m_acc = max_tile(rb_max, c, m_acc)
            if c < n_tiles(rb_exp):
                e = jnp.exp2(s_ref[tile_of(rb_exp, c)] - row_max[rb_exp])
                pv = _dot(e.astype(BF16), kv_blocks[rb_exp][1][c])
                acc = pv if acc is None else acc + pv
        if m_acc is not None:
            row_max[rb_max] = m_acc.max(axis=-1, keepdims=True)
        if acc is not None:
            outs.append(acc)
    return outs


def _lane_low_half():
    return lax.broadcasted_iota(jnp.int32, (1, LANES), 1) < (LANES // 2)


def _values_and_ones(v_tiles):
    ones = jnp.ones((KEY_TILE, LANES), BF16)
    return [jnp.concatenate([v, ones], axis=1) for v in v_tiles]


def _normalise_pair(acc_lo, acc_hi, low):
    return jnp.where(low, acc_lo[:, :LANES] / acc_lo[:, LANES:], acc_hi[:, :LANES] / acc_hi[:, LANES:])


def _split_lane_halves(q, low):
    zero = jnp.zeros_like(q)
    return jnp.where(low, q, zero), jnp.where(low, zero, q)


def _tiles(ref, n_tiles, j, bi=0):
    return [ref[bi, c * KEY_TILE:(c + 1) * KEY_TILE, j:j + LANES].astype(BF16) for c in range(n_tiles)]


def _ctx_attn_kernel(q_ref, k_ref, v_ref, g_ref, o_ref, s_ref):
    low = _lane_low_half()
    nb, tq, w = q_ref.shape
    n_tiles = k_ref.shape[1] // KEY_TILE
    tiles = [(bi, j, r) for bi in range(nb) for j in range(0, w, LANES) for r in range(0, tq, ROW_BLK)]
    q_blocks, kv_blocks, kv = [], [], {}
    for bi, j, r in tiles:
        if (bi, j) not in kv:
            kv[bi, j] = (_tiles(k_ref, n_tiles, j, bi), _values_and_ones(_tiles(v_ref, n_tiles, j, bi)))
        q_blocks += list(_split_lane_halves(q_ref[bi, r:r + ROW_BLK, j:j + LANES], low))
        kv_blocks += [kv[bi, j]] * 2
    res = _attend(q_blocks, kv_blocks, s_ref)
    for i, (bi, j, r) in enumerate(tiles):
        o = _normalise_pair(res[2 * i], res[2 * i + 1], low)
        gate = _silu(g_ref[bi, r:r + ROW_BLK, j:j + LANES].astype(F32))
        o_ref[bi, r:r + ROW_BLK, j:j + LANES] = (o * gate).astype(o_ref.dtype)


def _ctx_attn(q, k, v, g):
    b, t, w = q.shape
    nb = CTX_BATCH_PER_STEP
    spec = pl.BlockSpec((nb, t, w), lambda i: (i, 0, 0))
    return pl.pallas_call(
        _ctx_attn_kernel,
        grid=(b // nb,),
        in_specs=[spec] * 4,
        out_specs=spec,
        out_shape=jax.ShapeDtypeStruct((b, t, w), BF16),
        scratch_shapes=[pltpu.VMEM((SCORE_SLOTS * ROW_BLK, k.shape[1]), F32)],
        compiler_params=_params("parallel"),
        name="ctx_attn",
    )(q, k, v, g)


def _na_pair_tiles(rpb_ref, head, pt_ref):
    right = lax.broadcasted_iota(jnp.int32, (GRID_W, LANES), 1) >= GRID_W
    zero_row = jnp.zeros((1, LANES), F32)

    def toeplitz(row, lane0):
        shift = (lane0 - (NA_COLS - 1)) % LANES
        return pltpu.roll(jnp.broadcast_to(row, (GRID_W, LANES)), shift, axis=1, stride=1, stride_axis=0)

    for p in range(NA_PAIR_TILES):
        left_row = rpb_ref[head, p - 1:p, :] if p >= 1 else zero_row
        right_row = rpb_ref[head, p:p + 1, :]
        pt_ref[head, p] = jnp.where(right, toeplitz(right_row, GRID_W), toeplitz(left_row, 0)) * LOG2E


def _na_row_start(r, n_rows):
    kr = min(NA_ROWS_MAX, n_rows)
    return min(max(r - kr // 2, 0), n_rows - kr)


def _na_window(grp, n_rows):
    kr = min(NA_ROWS_MAX, n_rows)
    starts = [_na_row_start(grp * NA_Q_ROWS + i, n_rows) for i in range(NA_Q_ROWS)]
    rows_per_tile = KEY_TILE // GRID_W
    need = -(-(max(starts) + kr - min(starts)) // rows_per_tile) * rows_per_tile
    return min(min(starts), n_rows - need), need


def _na_kernel(q_ref, k_ref, v_ref, ck_ref, cv_ref, rpb_ref, g_ref, o_ref, pt_ref, badd_ref, s_ref, *, n_rows):
    kr = min(NA_ROWS_MAX, n_rows)
    n_grp = n_rows // NA_Q_ROWS
    n_pairs = q_ref.shape[2] // LANES
    low = _lane_low_half()

    @pl.when(pl.program_id(1) == 0)
    def _build_bias():
        for head in range(2 * n_pairs):
            _na_pair_tiles(rpb_ref, head, pt_ref)
        qcol = lax.broadcasted_iota(jnp.int32, (GRID_W, LANES), 0)
        lane = lax.broadcasted_iota(jnp.int32, (GRID_W, LANES), 1)
        right = lane >= GRID_W
        kcol = lane & (GRID_W - 1)
        col_start = jnp.clip(qcol - NA_COLS // 2, 0, GRID_W - NA_COLS)
        col_ok = (kcol >= col_start) & (kcol < col_start + NA_COLS)
        for grp in range(n_grp):
            krow0, key_rows = _na_window(grp, n_rows)
            for i in range(NA_Q_ROWS):
                r = grp * NA_Q_ROWS + i
                row_start = _na_row_start(r, n_rows)
                for m in range(key_rows // 2):
                    key_row = krow0 + 2 * m
                    ok_l = row_start <= key_row < row_start + kr
                    ok_r = row_start <= key_row + 1 < row_start + kr
                    rows = slice(i * GRID_W, (i + 1) * GRID_W)
                    cols = slice(m * LANES, (m + 1) * LANES)
                    if not (ok_l or ok_r):
                        for head in range(2 * n_pairs):
                            badd_ref[head * n_grp + grp, rows, cols] = jnp.full((GRID_W, LANES), -jnp.inf, F32)
                        continue
                    ok = col_ok
                    if not ok_r:
                        ok = ok & jnp.logical_not(right)
                    if not ok_l:
                        ok = ok & right
                    p = key_row - r + NA_ROWS_MAX
                    for head in range(2 * n_pairs):
                        badd_ref[head * n_grp + grp, rows, cols] = jnp.where(ok, pt_ref[head, p], -jnp.inf)

    n_ctx = ck_ref.shape[1] // KEY_TILE
    q_blocks, kv_blocks, blocks = [], [], []
    for pair in range(n_pairs):
        lanes = slice(pair * LANES, (pair + 1) * LANES)
        ctx_k = _tiles(ck_ref, n_ctx, pair * LANES)
        ctx_v = _tiles(cv_ref, n_ctx, pair * LANES)
        for grp in range(n_grp):
            krow0, key_rows = _na_window(grp, n_rows)
            k0 = krow0 * GRID_W
            win = [slice(k0 + c * KEY_TILE, k0 + (c + 1) * KEY_TILE) for c in range(key_rows * GRID_W // KEY_TILE)]
            kv = ([k_ref[0, w, lanes] for w in win] + ctx_k,
                  _values_and_ones([v_ref[0, w, lanes] for w in win] + ctx_v))
            q_blocks += list(_split_lane_halves(q_ref[0, grp * ROW_BLK:(grp + 1) * ROW_BLK, lanes], low))
            kv_blocks += [kv, kv]
            blocks += [(2 * pair, grp, len(win)), (2 * pair + 1, grp, len(win))]

    def masked_bias(rb, c, s):
        head, grp, n_win = blocks[rb]
        if c >= n_win:
            return None
        cols = slice(c * KEY_TILE, (c + 1) * KEY_TILE)
        return s + badd_ref[head * n_grp + grp, :, cols]

    res = _attend(q_blocks, kv_blocks, s_ref, masked_bias)
    for pair in range(n_pairs):
        lanes = slice(pair * LANES, (pair + 1) * LANES)
        for grp in range(n_grp):
            rb = 2 * (pair * n_grp + grp)
            o = _normalise_pair(res[rb], res[rb + 1], low)
            rows = slice(grp * ROW_BLK, (grp + 1) * ROW_BLK)
            o_ref[0, rows, lanes] = (o * _silu(g_ref[0, rows, lanes].astype(F32))).astype(o_ref.dtype)


def _na_attn(q, k, v, ck, cv, rpb, g):
    _, rpb_rows, rpb_cols = rpb.shape
    assert rpb_rows == RPB_ROWS and rpb_cols == RPB_COLS
    rpb = jnp.pad(rpb, ((0, 0), (0, NA_PAIR_TILES - rpb_rows), (0, LANES - rpb_cols)))
    b, t, w = q.shape
    c = ck.shape[1]
    n_rows = t // GRID_W
    assert NA_Q_ROWS * GRID_W == ROW_BLK and n_rows % NA_Q_ROWS == 0 and n_rows >= NA_K_ROWS
    n_grp = n_rows // NA_Q_ROWS
    n_win = NA_K_ROWS * GRID_W
    wb = NA_PAIRS_PER_STEP * LANES
    hb = 2 * NA_PAIRS_PER_STEP
    tspec = pl.BlockSpec((1, t, wb), lambda hp, bi: (bi, 0, hp))
    cspec = pl.BlockSpec((1, c, wb), lambda hp, bi: (bi, 0, hp))
    ptspec = pl.BlockSpec((hb, NA_PAIR_TILES, LANES), lambda hp, bi: (hp, 0, 0))
    return pl.pallas_call(
        functools.partial(_na_kernel, n_rows=n_rows),
        grid=(w // wb, b),
        in_specs=[tspec, tspec, tspec, cspec, cspec, ptspec, tspec],
        out_specs=tspec,
        out_shape=jax.ShapeDtypeStruct((b, t, w), BF16),
        scratch_shapes=[pltpu.VMEM((hb, NA_PAIR_TILES, GRID_W, LANES), F32),
                        pltpu.VMEM((hb * n_grp, ROW_BLK, n_win), F32),
                        pltpu.VMEM((SCORE_SLOTS * ROW_BLK, n_win + c), F32)],
        compiler_params=_params("parallel", "arbitrary"),
        name="na_attn",
    )(q, k, v, ck, cv, rpb, g)


def _softplus(x):
    return jnp.maximum(x, 0.0) + jnp.log(1.0 + jnp.exp(-jnp.abs(x)))


def _lru_kernel(x_ref, gb_ref, h0f_ref, h0b_ref, cw_ref, cb_ref, wg_ref, lam_ref,
                o_ref, ff_ref, fb_ref, xt_s, af_s, uf_s, ab_s, ub_s, hf_s, hb_s):
    nb, t, _ = x_ref.shape
    pad = CONV_PAD_L * nb
    cw = 0.5 * cw_ref[...]
    cb = 0.5 * cb_ref[...]
    wg = wg_ref[0]
    kk = (-0.5 * LRU_C * LOG2E) * _softplus(-lam_ref[...])

    xt_s[0:pad, :] = jnp.zeros((pad, LANES), F32)
    xt_s[pad + t * nb:pad + (t + CONV_PAD_R) * nb, :] = jnp.zeros((CONV_PAD_R * nb, LANES), F32)

    def to_time_major(b, carry):
        xt_s[pl.ds(pad + b, t, stride=nb), :] = x_ref[b]
        return carry

    lax.fori_loop(0, nb, to_time_major, 0)

    rows_per_chunk = LRU_CHUNK * nb
    lane = lax.broadcasted_iota(jnp.int32, (rows_per_chunk, LANES), 1)
    bias_cols = jnp.where(lane < 2, 1.0, 0.0).astype(BF16)

    def gates(c, carry):
        r0 = pl.multiple_of(c * rows_per_chunk, rows_per_chunk)

        def tap(i):
            return xt_s[pl.ds(r0 + i * nb, rows_per_chunk), :]

        xh = cw[0:1] * tap(0) + cw[1:2] * tap(1) + cw[2:3] * tap(2) + cw[3:4] * tap(3) + cb
        th = jnp.tanh(_dot(jnp.concatenate([xh.astype(BF16), bias_cols], axis=1), wg))
        rows = pl.ds(r0, rows_per_chunk)
        for d, (a_s, u_s) in enumerate(((af_s, uf_s), (ab_s, ub_s))):
            t_a = th[:, (2 * d) * LANES:(2 * d + 1) * LANES]
            t_x = th[:, (2 * d + 1) * LANES:(2 * d + 2) * LANES]
            a = jnp.exp2(kk[d:d + 1] * t_a + kk[d:d + 1])
            y = 1.0 - a * a
            sq = y * lax.rsqrt(jnp.maximum(y, 1e-30))
            a_s[rows, :] = a
            u_s[rows, :] = sq * (xh * t_x + xh)
        return carry

    lax.fori_loop(0, t // LRU_CHUNK, gates, 0)

    ts = t // LRU_SEGMENTS
    refs = ((af_s, uf_s), (ab_s, ub_s))

    slab = nb * LRU_UNROLL

    def slab_rows(d, j, i):
        base = (j * ts + i * LRU_UNROLL) * nb
        return pl.ds(pl.multiple_of(base if d == 0 else t * nb - slab - base, slab), slab)

    def step_of(d, block, k):
        kk_ = k if d == 0 else LRU_UNROLL - 1 - k
        return block[kk_ * nb:(kk_ + 1) * nb]

    def compose(i, carry):
        state = list(carry)
        for d in range(2):
            for j in range(LRU_SEGMENTS - 1):
                n = d * (LRU_SEGMENTS - 1) + j
                a_blk = refs[d][0][slab_rows(d, j, i), :]
                u_blk = refs[d][1][slab_rows(d, j, i), :]
                g, big_a = state[n]
                for k in range(LRU_UNROLL):
                    a = step_of(d, a_blk, k)
                    g, big_a = a * g + step_of(d, u_blk, k), a * big_a
                state[n] = (g, big_a)
        return tuple(state)

    ones = jnp.ones((nb, LANES), F32)
    h0 = (h0f_ref[...], h0b_ref[...])
    init = tuple((h0[d] if j == 0 else jnp.zeros((nb, LANES), F32), ones)
                 for d in range(2) for j in range(LRU_SEGMENTS - 1))
    ends = lax.fori_loop(0, ts // LRU_UNROLL, compose, init)
    starts = []
    for d in range(2):
        h = h0[d]
        for j in range(LRU_SEGMENTS):
            starts.append(h)
            if j < LRU_SEGMENTS - 1:
                g, big_a = ends[d * (LRU_SEGMENTS - 1) + j]
                h = g if j == 0 else g + big_a * h

    def rerun(i, carry):
        state = list(carry)
        for d, h_s in enumerate((hf_s, hb_s)):
            for j in range(LRU_SEGMENTS):
                n = d * LRU_SEGMENTS + j
                a_blk = refs[d][0][slab_rows(d, j, i), :]
                u_blk = refs[d][1][slab_rows(d, j, i), :]
                hs = []
                for k in range(LRU_UNROLL):
                    state[n] = step_of(d, a_blk, k) * state[n] + step_of(d, u_blk, k)
                    hs.append(state[n])
                h_s[slab_rows(d, j, i), :] = jnp.concatenate(hs if d == 0 else hs[::-1], axis=0)
        return tuple(state)

    final = lax.fori_loop(0, ts // LRU_UNROLL, rerun, tuple(starts))
    ff_ref[...] = final[LRU_SEGMENTS - 1]
    fb_ref[...] = final[2 * LRU_SEGMENTS - 1]

    def add_directions(c, carry):
        rows = pl.ds(pl.multiple_of(c * rows_per_chunk, rows_per_chunk), rows_per_chunk)
        hf_s[rows, :] = hf_s[rows, :] + hb_s[rows, :]
        return carry

    lax.fori_loop(0, t // LRU_CHUNK, add_directions, 0)

    def emit(b, carry):
        hs = hf_s[pl.ds(b, t, stride=nb), :]
        o_ref[b] = (hs * _silu(gb_ref[b].astype(F32))).astype(o_ref.dtype)
        return carry

    lax.fori_loop(0, nb, emit, 0)


def _lru_branch(xb, gb, h0f, h0b, conv_w, conv_b, wg, lam):
    b, t, w = xb.shape
    nb = SUBLANES
    assert t % (LRU_SEGMENTS * LRU_UNROLL) == 0 and t % LRU_CHUNK == 0
    assert conv_w.shape[0] == CONV_PAD_L + 1 + CONV_PAD_R
    xspec = pl.BlockSpec((nb, t, LANES), lambda i, c: (i, 0, c))
    hspec = pl.BlockSpec((nb, LANES), lambda i, c: (i, c))
    scratch = [pltpu.VMEM(((t + CONV_PAD_L + CONV_PAD_R) * nb, LANES), F32)]
    scratch += [pltpu.VMEM((t * nb, LANES), F32) for _ in range(6)]
    return pl.pallas_call(
        _lru_kernel,
        grid=(b // nb, w // LANES),
        in_specs=[xspec, xspec, hspec, hspec,
                  pl.BlockSpec((conv_w.shape[0], LANES), lambda i, c: (0, c)),
                  pl.BlockSpec((1, LANES), lambda i, c: (0, c)),
                  pl.BlockSpec((1, 2 * LANES, 4 * LANES), lambda i, c: (c, 0, 0)),
                  pl.BlockSpec((2, LANES), lambda i, c: (0, c))],
        out_specs=[xspec, hspec, hspec],
        out_shape=[jax.ShapeDtypeStruct((b, t, w), BF16),
                   jax.ShapeDtypeStruct((b, w), F32),
                   jax.ShapeDtypeStruct((b, w), F32)],
        scratch_shapes=scratch,
        compiler_params=_params("parallel", "parallel"),
        name="lru_branch",
    )(xb, gb, h0f, h0b, conv_w, conv_b.reshape(1, w), wg, lam)


def _lru_gate_weights(wa, ba, wx, bx):
    def tiles(w):
        w2 = w.reshape(LRU_BLOCKS // 2, 2, LRU_BLOCK, LRU_BLOCK)
        z = jnp.zeros_like(w2[:, 0])
        return jnp.concatenate([jnp.concatenate([w2[:, 0], z], axis=2),
                                jnp.concatenate([z, w2[:, 1]], axis=2)], axis=1)
    wg = jnp.concatenate([tiles(wa[0]), tiles(wx[0]), tiles(wa[1]), tiles(wx[1])], axis=2)
    n_tiles = LRU_WIDTH // LANES
    bg = jnp.stack([ba[0], bx[0], ba[1], bx[1]], axis=0).reshape(4, n_tiles, LANES)
    bg = 0.5 * jnp.transpose(bg, (1, 0, 2)).reshape(n_tiles, 1, 4 * LANES)
    hi = bg.astype(BF16)
    lo = (bg - hi.astype(F32)).astype(BF16)
    rest = jnp.zeros((n_tiles, LANES - 2, 4 * LANES), BF16)
    return jnp.concatenate([wg.astype(BF16), hi, lo, rest], axis=1)


def _packed_tiles(ref, n_heads, head, bi=0):
    n_tiles = ref.shape[1] // (n_heads * KEY_TILE)
    return [ref[bi, pl.ds(c * KEY_TILE * n_heads + head, KEY_TILE, stride=n_heads), :].astype(BF16)
            for c in range(n_tiles)]


def _diff_kernel(*refs, packed, n_heads, lam_init):
    n_kv = len(packed)
    q_ref = refs[0]
    kv_refs = refs[1:1 + 2 * n_kv]
    g_ref, lamp_ref, subg_ref, o_ref, s_ref = refs[1 + 2 * n_kv:]
    low = _lane_low_half()
    tq = q_ref.shape[1]
    first_head = pl.program_id(1) * (q_ref.shape[2] // LANES)
    lp = lamp_ref[...]
    lam = (jnp.exp(jnp.sum(lp[0:1] * lp[1:2], axis=-1, keepdims=True))
           - jnp.exp(jnp.sum(lp[2:3] * lp[3:4], axis=-1, keepdims=True)) + lam_init)
    sub_gain = subg_ref[...] * (1.0 - lam_init)
    nb = q_ref.shape[0]
    tiles = [(bi, j) for bi in range(nb) for j in range(0, q_ref.shape[2], LANES)]
    q_blocks, kv_blocks = [], []
    for bi, j in tiles:
        k_tiles, v_tiles = [], []
        for i in range(n_kv):
            if packed[i]:
                head = first_head + j // LANES
                k_tiles += _packed_tiles(kv_refs[2 * i], n_heads, head, bi)
                v_tiles += _packed_tiles(kv_refs[2 * i + 1], n_heads, head, bi)
            else:
                n_tiles = kv_refs[2 * i].shape[1] // KEY_TILE
                k_tiles += _tiles(kv_refs[2 * i], n_tiles, j, bi)
                v_tiles += _tiles(kv_refs[2 * i + 1], n_tiles, j, bi)
        v_tiles = _values_and_ones(v_tiles)
        for r in range(0, tq, ROW_BLK):
            q_blocks += list(_split_lane_halves(q_ref[bi, r:r + ROW_BLK, j:j + LANES], low))
            kv_blocks += [(k_tiles, v_tiles)] * 2
    res = _attend(q_blocks, kv_blocks, s_ref)
    i = 0
    for bi, j in tiles:
        for r in range(0, tq, ROW_BLK):
            acc1, acc2 = res[i], res[i + 1]
            i += 2
            o = acc1[:, :LANES] / acc1[:, LANES:] - acc2[:, :LANES] * (lam / acc2[:, LANES:])
            o = o * lax.rsqrt(jnp.mean(o * o, axis=-1, keepdims=True) + EPS) * sub_gain
            gate = _silu(g_ref[bi, r:r + ROW_BLK, j:j + LANES].astype(F32))
            o_ref[bi, r:r + ROW_BLK, j:j + LANES] = (o * gate).astype(o_ref.dtype)


def _diff_attn(q, kv_list, g, lam_params, sub_g, lam_init, tq, heads_per_step, batch_per_step=1):
    b, t, w = q.shape
    wb = heads_per_step * LANES
    nb = batch_per_step
    qspec = pl.BlockSpec((nb, tq, wb), lambda bi, h, qi: (bi, qi, h))
    n_heads = w // LANES
    in_specs = [qspec]
    args = [q]
    n_keys = 0
    for k, v, packed in kv_list:
        if packed:
            spec = pl.BlockSpec((nb, k.shape[1], LANES), lambda bi, h, qi: (bi, 0, 0))
            n_keys += k.shape[1] // n_heads
        else:
            spec = pl.BlockSpec((nb, k.shape[1], wb), lambda bi, h, qi: (bi, 0, h))
            n_keys += k.shape[1]
        in_specs += [spec, spec]
        args += [k, v]
    in_specs += [qspec,
                 pl.BlockSpec(lam_params.shape, lambda bi, h, qi: (0, 0)),
                 pl.BlockSpec((1, LANES), lambda bi, h, qi: (0, 0))]
    args += [g, lam_params, sub_g.reshape(1, LANES)]
    return pl.pallas_call(
        functools.partial(_diff_kernel, packed=tuple(p for _, _, p in kv_list), n_heads=n_heads,
                          lam_init=lam_init),
        grid=(b // nb, w // wb, t // tq),
        in_specs=in_specs,
        out_specs=qspec,
        out_shape=jax.ShapeDtypeStruct((b, t, w), BF16),
        scratch_shapes=[pltpu.VMEM((SCORE_SLOTS * ROW_BLK, n_keys), F32)],
        compiler_params=_params("parallel", "parallel", "parallel"),
        name="diff_attn",
    )(*args)


def _rope_tables(t_len):
    m = DIFF_HEAD_DIM // 4
    t = jnp.arange(t_len)
    rows = (t // GRID_W).astype(F32)
    cols = (t % GRID_W).astype(F32)
    inv = ROPE_BASE ** (-jnp.arange(m, dtype=F32) / m)
    ang_r = rows[:, None] * inv[None, :]
    ang_c = cols[:, None] * inv[None, :]
    cos = jnp.concatenate([jnp.cos(ang_r)] * 2 + [jnp.cos(ang_c)] * 2, axis=1)
    sin = jnp.concatenate([-jnp.sin(ang_r), jnp.sin(ang_r), -jnp.sin(ang_c), jnp.sin(ang_c)], axis=1)
    reps = LANES // DIFF_HEAD_DIM
    return jnp.tile(cos, (1, reps)), jnp.tile(sin, (1, reps))


def kernel(x_prompt, x_sample, c, cache_na_k, cache_na_v, state_lru, cache_diff_k, cache_diff_v, c_ctx,
           e_norm, e_ada_w, e_ada_b, e_w_in, e_rpb, e_conv_w, e_conv_b, e_lru_wa, e_lru_ba, e_lru_wx, e_lru_bx,
           e_lru_lam, e_w_out, o_norm, o_ada_w, o_ada_b, o_w_in, o_lq1, o_lk1, o_lq2, o_lk2, o_sub_g, o_w_out,
           final_norm):
    bp, tp, d = x_prompt.shape
    bs, ts, _ = x_sample.shape
    past = cache_na_k.shape[2]
    pad_rows = 2 * SUBLANES - bs - 1
    cc = jnp.concatenate([c, c_ctx[None, :], jnp.zeros((pad_rows, d), F32)], axis=0)
    nw, lw, dw = NA_WIDTH, LRU_WIDTH, DIFF_WIDTH
    rope_tables = _rope_tables(ts) if DEPTH > 1 else None

    def layer_params(i):
        j = i // 2
        if i % 2 == 0:
            return e_ada_w[j], e_ada_b[j], e_norm[j], e_w_in[j], e_w_out[j]
        return o_ada_w[j], o_ada_b[j], o_norm[j], o_w_in[j], o_w_out[j]

    mods = []
    for i in range(DEPTH):
        ada_w, ada_b = layer_params(i)[:2]
        m = _modulation(cc, ada_w, ada_b)
        mods.append((m[:bs].reshape(bs, 1, 3 * d), m[bs:bs + 1].reshape(1, 1, 3 * d)))

    def projection(i, latent):
        norm_g, w_in = layer_params(i)[2:4]
        mod = mods[i][0 if latent else 1]
        kv_dtype = BF16 if latent else F32
        if i % 2 == 0:
            qmul = NA_HEAD_DIM ** -0.5 * LOG2E
            outs = [(nw, BF16, False, qmul, False), (nw, kv_dtype, False, None, False),
                    (nw, kv_dtype, False, None, False), (nw, BF16, False, None, False),
                    (lw, F32, False, None, False), (lw, BF16, False, None, False)]
            return mod, norm_g, w_in, outs, None
        qmul = DIFF_HEAD_DIM ** -0.5 * LOG2E
        outs = [(dw, BF16, latent, qmul, False), (dw, kv_dtype, latent, None, not latent),
                (dw, kv_dtype, False, None, not latent), (dw, BF16, False, None, False)]
        return mod, norm_g, w_in, outs, rope_tables if latent else None

    xp, xs = x_prompt, x_sample
    _, proj_p = _stage(xp, nxt=projection(0, False), tm=512)
    _, proj_s = _stage(xs, nxt=projection(0, True), tm=512)
    na_k, na_v, lru_s, df_k, df_v = [], [], [], [], []
    for i in range(DEPTH):
        j = i // 2
        w_out = layer_params(i)[4]
        if i % 2 == 0:
            wg = _lru_gate_weights(e_lru_wa[j], e_lru_ba[j], e_lru_wx[j], e_lru_bx[j])

            q, k, v, ga, xb, gb = proj_p
            oa = _ctx_attn(q, k, v, ga)
            zeros = jnp.zeros((bp, lw), F32)
            ob, ff, fb = _lru_branch(xb, gb, zeros, zeros, e_conv_w[j], e_conv_b[j], wg, e_lru_lam[j])
            mixed_p = [oa, ob]
            na_k.append(k.reshape(bp, tp, NA_HEADS, NA_HEAD_DIM))
            na_v.append(v.reshape(bp, tp, NA_HEADS, NA_HEAD_DIM))
            lru_s.append(jnp.stack([ff, fb], axis=1))

            q, k, v, ga, xb, gb = proj_s
            ck = cache_na_k[:, j].reshape(bs, past, nw)
            cv = cache_na_v[:, j].reshape(bs, past, nw)
            oa = _na_attn(q, k, v, ck, cv, e_rpb[j], ga)
            ob, _, _ = _lru_branch(xb, gb, state_lru[:, j, 0], state_lru[:, j, 1],
                                   e_conv_w[j], e_conv_b[j], wg, e_lru_lam[j])
            mixed_s = [oa, ob]
        else:
            lam_init = 0.8 - 0.6 * math.exp(-0.3 * i)
            lam_params = jnp.stack([o_lq1[j], o_lk1[j], o_lq2[j], o_lk2[j]], axis=0)

            q, k, v, g = proj_p
            mixed_p = [_diff_attn(q, [(k, v, True)], g, lam_params, o_sub_g[j], lam_init, tq=tp,
                                  heads_per_step=DIFF_HEADS, batch_per_step=CTX_BATCH_PER_STEP)]
            df_k.append(k.reshape(bp, tp, DIFF_HEADS, 2 * DIFF_HEAD_DIM))
            df_v.append(v.reshape(bp, tp, DIFF_HEADS, 2 * DIFF_HEAD_DIM))

            q, k, v, g = proj_s
            ck = cache_diff_k[:, j].reshape(bs, past * DIFF_HEADS, 2 * DIFF_HEAD_DIM)
            cv = cache_diff_v[:, j].reshape(bs, past * DIFF_HEADS, 2 * DIFF_HEAD_DIM)
            mixed_s = [_diff_attn(q, [(ck, cv, True), (k, v, False)], g, lam_params, o_sub_g[j], lam_init, tq=ts,
                                  heads_per_step=4)]

        if i == DEPTH - 1:
            xp, _ = _stage(xp, prev=(mixed_p, w_out, mods[i][1]), final_g=final_norm, tm=1024)
            xs, _ = _stage(xs, prev=(mixed_s, w_out, mods[i][0]), final_g=final_norm, tm=1024)
        else:
            xp, proj_p = _stage(xp, prev=(mixed_p, w_out, mods[i][1]), nxt=projection(i + 1, False), tm=512)
            xs, proj_s = _stage(xs, prev=(mixed_s, w_out, mods[i][0]), nxt=projection(i + 1, True), tm=512)

    return (xp, xs, jnp.stack(na_k, axis=1), jnp.stack(na_v, axis=1), jnp.stack(lru_s, axis=1),
            jnp.stack(df_k, axis=1), jnp.stack(df_v, axis=1))
```

```python
import functools
import math

import jax
import jax.numpy as jnp
from jax import lax
from jax.experimental import pallas as pl
from jax.experimental.pallas import tpu as pltpu

DEPTH = 2
GRID_W = 64
NA_HEADS = 8
NA_HEAD_DIM = 64
NA_WIDTH = NA_HEADS * NA_HEAD_DIM
NA_ROWS_MAX = 8
NA_COLS = 16
RPB_ROWS = 2 * NA_ROWS_MAX - 1
RPB_COLS = 2 * NA_COLS - 1
LRU_WIDTH = 512
LRU_BLOCKS = 8
LRU_BLOCK = LRU_WIDTH // LRU_BLOCKS
LRU_C = 8.0
DIFF_HEADS = 8
DIFF_HEAD_DIM = 64
DIFF_WIDTH = DIFF_HEADS * 2 * DIFF_HEAD_DIM
ROPE_BASE = 10000.0
EPS = 1e-6
LOG2E = math.log2(math.e)

LANES = 128
SUBLANES = 8
MXU_DIM = 256
VMEM_LIMIT = 56 * 1024 * 1024

CAST_COLS = 512
IN_BUFS = 3
LRU_SEGMENTS = 4
LRU_UNROLL = 8
LRU_CHUNK = 256
CONV_PAD_L = 2
CONV_PAD_R = 1
ROW_BLK = MXU_DIM
KEY_TILE = MXU_DIM
SCORE_SLOTS = 4

NA_Q_ROWS = 4
NA_K_ROWS = 12
NA_PAIR_TILES = 16
NA_PAIRS_PER_STEP = 4
CTX_BATCH_PER_STEP = 4

F32 = jnp.float32
BF16 = jnp.bfloat16


def _params(*semantics):
    return pltpu.CompilerParams(dimension_semantics=semantics, vmem_limit_bytes=VMEM_LIMIT)


def _silu(x):
    h = 0.5 * x
    return h * jnp.tanh(h) + h


def _dot(a, b):
    return jnp.dot(a, b, preferred_element_type=F32)


def _dot_nt(a, b):
    return lax.dot_general(a, b, (((1,), (1,)), ((), ())), preferred_element_type=F32)


def _mod_kernel(c_ref, w_ref, b_ref, o_ref):
    s = _silu(c_ref[...])
    o_ref[...] = _dot(s.astype(BF16), w_ref[...].astype(BF16)) + b_ref[...]


def _modulation(cc, w, b):
    rows, d = cc.shape
    n = w.shape[1]
    tn = 1024
    return pl.pallas_call(
        _mod_kernel,
        grid=(n // tn,),
        in_specs=[pl.BlockSpec((rows, d), lambda j: (0, 0)),
                  pl.BlockSpec((d, tn), lambda j: (0, j)),
                  pl.BlockSpec((1, tn), lambda j: (0, j))],
        out_specs=pl.BlockSpec((rows, tn), lambda j: (0, j)),
        out_shape=jax.ShapeDtypeStruct((rows, n), F32),
        compiler_params=_params("arbitrary"),
        name="modulation",
    )(cc, w, b.reshape(1, n))


def _rope_block(x, cos, sin_signed, low16):
    partner = jnp.where(low16, pltpu.roll(x, LANES - 16, axis=1), pltpu.roll(x, 16, axis=1))
    return x * cos + partner * sin_signed


def _cast_weights_once(w_ref, wbf_ref):
    @pl.when((pl.program_id(0) == 0) & (pl.program_id(1) == 0))
    def _():
        for off in range(0, w_ref.shape[1], CAST_COLS):
            wbf_ref[:, off:off + CAST_COLS] = w_ref[:, off:off + CAST_COLS].astype(BF16)


def _weight_chunk_copy(w_hbm, stage_ref, sem, c):
    cols = slice(c * CAST_COLS, (c + 1) * CAST_COLS)
    return pltpu.make_async_copy(w_hbm.at[:, cols], stage_ref.at[:, cols], sem.at[c])


def _land_weight_chunks(w_hbm, stage_ref, sem, wbf_ref, off, width):
    for c in range(off // CAST_COLS, (off + width) // CAST_COLS):
        _weight_chunk_copy(w_hbm, stage_ref, sem, c).wait()
        cols = slice(c * CAST_COLS, (c + 1) * CAST_COLS)
        wbf_ref[:, cols] = stage_ref[:, cols].astype(BF16)


def _norm_project(x, mod_ref, g_ref, wbf_ref, rope_refs, out_refs, outs, land=None):
    tm, d = x.shape
    y = x * lax.rsqrt(jnp.mean(x * x, axis=-1, keepdims=True) + EPS) * g_ref[...]
    mod = mod_ref[0]
    h = (y * (1.0 + mod[:, d:2 * d]) + mod[:, :d]).astype(BF16)
    if rope_refs:
        cos = rope_refs[0][...]
        sin = rope_refs[1][...]
        lane = lax.broadcasted_iota(jnp.int32, (1, LANES), 1)
        low16 = (lane & 16) == 0
    off = 0
    for o_ref, (width, roped, mul, packed) in zip(out_refs, outs):
        if land is not None:
            land(off, width)
        acc = _dot(h, wbf_ref[:, off:off + width])
        if roped:
            acc = jnp.concatenate(
                [_rope_block(acc[:, j:j + LANES], cos, sin, low16) for j in range(0, width, LANES)], axis=1)
        if mul is not None:
            acc = acc * mul
        if packed:
            n_heads = width // LANES
            t = o_ref.shape[1] // n_heads
            for bi in range(o_ref.shape[0]):
                for hh in range(n_heads):
                    o_ref[bi, pl.ds(hh, t, stride=n_heads), :] = (
                        acc[bi * t:(bi + 1) * t, hh * LANES:(hh + 1) * LANES].astype(o_ref.dtype))
        else:
            o_ref[...] = acc.astype(o_ref.dtype).reshape(o_ref.shape)
        off += width


def _token_rows(ref):
    return ref[...].reshape(-1, ref.shape[-1])


def _gated_residual(o_refs, wbf_ref, x_ref, mod_ref):
    d = x_ref.shape[-1]
    acc, row = None, 0
    for o_ref in o_refs:
        k = o_ref.shape[-1]
        part = _dot(_token_rows(o_ref), wbf_ref[row:row + k, :])
        acc = part if acc is None else acc + part
        row += k
    return _token_rows(x_ref) + mod_ref[0][:, 2 * d:] * acc


def _stage_kernel(*refs, n_o, final, outs, rope):
    refs = list(refs)
    take = lambda n: [refs.pop(0) for _ in range(n)]
    if n_o:
        o_refs = take(n_o)
        wout_ref, x_ref, mod_prev_ref = take(3)
        fin_ref = take(1)[0] if final else None
    else:
        x_ref, = take(1)
    if outs:
        mod_ref, g_ref, win_ref = take(3)
        rope_refs = take(2) if rope else []
    xout_ref = take(1)[0] if n_o else None
    out_refs = take(len(outs))
    wout_bf = take(1)[0] if n_o else None
    if n_o:
        _cast_weights_once(wout_ref, wout_bf)

    def step(land):
        if n_o:
            x = _gated_residual(o_refs, wout_bf, x_ref, mod_prev_ref)
            if final:
                x_new = x * lax.rsqrt(jnp.mean(x * x, axis=-1, keepdims=True) + EPS) * fin_ref[...]
            else:
                x_new = x
            xout_ref[...] = x_new.reshape(xout_ref.shape)
        else:
            x = _token_rows(x_ref)
        if outs:
            _norm_project(x, mod_ref, g_ref, win_bf, rope_refs, out_refs, outs, land)

    if not outs:
        step(None)
        return
    stage_ref, win_bf, sem = take(3)
    first = (pl.program_id(0) == 0) & (pl.program_id(1) == 0)

    @pl.when(first)
    def _():
        for c in range(win_ref.shape[1] // CAST_COLS):
            _weight_chunk_copy(win_ref, stage_ref, sem, c).start()
        step(functools.partial(_land_weight_chunks, win_ref, stage_ref, sem, win_bf))

    @pl.when(jnp.logical_not(first))
    def _():
        step(None)


def _stage(x, prev=None, nxt=None, final_g=None, tm=256):
    b, t, d = x.shape
    shared_mod = all(m.shape[0] == 1 for m in ([prev[2]] if prev else []) + ([nxt[0]] if nxt else []))
    bb = max(1, tm // t) if shared_mod else 1
    tm = min(tm, t)
    tile = lambda width: pl.BlockSpec((bb, tm, width), lambda i, j: (i, j, 0))
    mod_spec = lambda mod: pl.BlockSpec(
        (1, 1, 3 * d), (lambda i, j: (i, 0, 0)) if mod.shape[0] > 1 else (lambda i, j: (0, 0, 0)))
    resident = lambda w: pl.BlockSpec(w.shape, lambda i, j: (0, 0), pipeline_mode=pl.Buffered(1))
    row_spec = pl.BlockSpec((1, d), lambda i, j: (0, 0))
    in_specs, args, out_specs, out_shape, scratch = [], [], [], [], []
    n_o, outs, rope = 0, (), False
    if prev is not None:
        o_list, w_out, mod_prev = prev
        n_o = len(o_list)
        in_specs += [tile(o.shape[-1]) for o in o_list] + [resident(w_out), tile(d), mod_spec(mod_prev)]
        args += list(o_list) + [w_out, x, mod_prev]
        if final_g is not None:
            in_specs.append(row_spec)
            args.append(final_g.reshape(1, d))
        out_specs.append(tile(d))
        out_shape.append(jax.ShapeDtypeStruct((b, t, d), F32))
        scratch.append(pltpu.VMEM(w_out.shape, BF16))
    else:
        in_specs.append(tile(d))
        args.append(x)
    if nxt is not None:
        mod_next, norm_g, w_in, out_list, rope_tables = nxt
        assert all(wd % CAST_COLS == 0 for wd, _, _, _, _ in out_list)
        in_specs += [mod_spec(mod_next), row_spec, pl.BlockSpec(memory_space=pl.ANY)]
        args += [mod_next, norm_g.reshape(1, d), w_in]
        rope = rope_tables is not None
        if rope:
            in_specs += [pl.BlockSpec((tm, LANES), lambda i, j: (j, 0))] * 2
            args += list(rope_tables)
        for width, dt, _, _, packed in out_list:
            if packed:
                heads = width // LANES
                out_specs.append(pl.BlockSpec((bb, tm * heads, LANES), lambda i, j: (i, j, 0)))
                out_shape.append(jax.ShapeDtypeStruct((b, t * heads, LANES), dt))
            else:
                out_specs.append(tile(width))
                out_shape.append(jax.ShapeDtypeStruct((b, t, width), dt))
        outs = tuple((w_, r, m, p) for w_, _, r, m, p in out_list)
        scratch += [pltpu.VMEM(w_in.shape, F32), pltpu.VMEM(w_in.shape, BF16),
                    pltpu.SemaphoreType.DMA((w_in.shape[1] // CAST_COLS,))]
    res = pl.pallas_call(
        functools.partial(_stage_kernel, n_o=n_o, final=final_g is not None, outs=outs, rope=rope),
        grid=(b // bb, t // tm),
        in_specs=in_specs,
        out_specs=out_specs,
        out_shape=out_shape,
        scratch_shapes=scratch,
        compiler_params=_params("arbitrary", "arbitrary"),
        name="stage",
    )(*args)
    return (res[0], list(res[1:])) if n_o else (None, list(res))


def _ring_copy(hbm_ref, buf_ref, sem, step, bb, tm, nt):
    slot = lax.rem(step, IN_BUFS)
    rows = pl.ds(lax.div(step, nt) * bb, bb)
    toks = pl.ds(lax.rem(step, nt) * tm, tm)
    return pltpu.make_async_copy(hbm_ref.at[rows, toks, :], buf_ref.at[slot], sem.at[slot])


def _final_kernel(*refs, n_o, bb, tm, nt, n_steps):
    hbm_refs = refs[:n_o + 1]
    wout_ref, mod_ref, fin_ref, out_ref, wout_bf = refs[n_o + 1:n_o + 6]
    bufs = refs[n_o + 6:2 * n_o + 7]
    sems = refs[2 * n_o + 7:]
    step = pl.program_id(0) * nt + pl.program_id(1)

    def start(s):
        for h, b, sm in zip(hbm_refs, bufs, sems):
            _ring_copy(h, b, sm, s, bb, tm, nt).start()

    @pl.when(step == 0)
    def _():
        for s in range(min(IN_BUFS - 1, n_steps)):
            start(jnp.int32(s))

    @pl.when(step + (IN_BUFS - 1) < n_steps)
    def _():
        start(step + (IN_BUFS - 1))

    for h, b, sm in zip(hbm_refs, bufs, sems):
        _ring_copy(h, b, sm, step, bb, tm, nt).wait()
    _cast_weights_once(wout_ref, wout_bf)
    slot = lax.rem(step, IN_BUFS)
    d = out_ref.shape[-1]
    acc, row = None, 0
    for b in bufs[:n_o]:
        k = b.shape[-1]
        part = _dot(b[slot].reshape(-1, k), wout_bf[row:row + k, :])
        acc = part if acc is None else acc + part
        row += k
    x = bufs[n_o][slot].reshape(-1, d) + mod_ref[0][:, 2 * d:] * acc
    y = x * lax.rsqrt(jnp.mean(x * x, axis=-1, keepdims=True) + EPS) * fin_ref[...]
    out_ref[...] = y.reshape(out_ref.shape)


def _final_stage(x, o_list, w_out, mod, final_g, tm=1024):
    b, t, d = x.shape
    bb = max(1, tm // t) if mod.shape[0] == 1 else 1
    tm = min(tm, t)
    nt = t // tm
    n_steps = (b // bb) * nt
    mod_map = (lambda i, j: (i, 0, 0)) if mod.shape[0] > 1 else (lambda i, j: (0, 0, 0))
    streamed = list(o_list) + [x]
    in_specs = [pl.BlockSpec(memory_space=pl.ANY)] * len(streamed)
    in_specs += [pl.BlockSpec(w_out.shape, lambda i, j: (0, 0), pipeline_mode=pl.Buffered(1)),
                 pl.BlockSpec((1, 1, 3 * d), mod_map),
                 pl.BlockSpec((1, d), lambda i, j: (0, 0))]
    scratch = [pltpu.VMEM(w_out.shape, BF16)]
    scratch += [pltpu.VMEM((IN_BUFS, bb, tm, a.shape[-1]), a.dtype) for a in streamed]
    scratch += [pltpu.SemaphoreType.DMA((IN_BUFS,)) for _ in streamed]
    return pl.pallas_call(
        functools.partial(_final_kernel, n_o=len(o_list), bb=bb, tm=tm, nt=nt, n_steps=n_steps),
        grid=(b // bb, nt),
        in_specs=in_specs,
        out_specs=pl.BlockSpec((bb, tm, d), lambda i, j: (i, j, 0)),
        out_shape=jax.ShapeDtypeStruct((b, t, d), F32),
        scratch_shapes=scratch,
        compiler_params=_params("arbitrary", "arbitrary"),
        name="final_stage",
    )(*streamed, w_out, mod, final_g.reshape(1, d))


def _attend(q_blocks, kv_blocks, s_ref, score_fn=None):
    n_rb = len(q_blocks)

    def tile_of(rb, c):
        slot = rb % SCORE_SLOTS
        return (slice(slot * ROW_BLK, (slot + 1) * ROW_BLK), slice(c * KEY_TILE, (c + 1) * KEY_TILE))

    def score_tile(rb, c):
        s_ref[tile_of(rb, c)] = _dot_nt(q_blocks[rb], kv_blocks[rb][0][c])

    def max_tile(rb, c, m_acc):
        s = s_ref[tile_of(rb, c)]
        adjusted = None if score_fn is None else score_fn(rb, c, s)
        if adjusted is not None:
            s = adjusted
            s_ref[tile_of(rb, c)] = s
        mt = jnp.maximum(s[:, :LANES], s[:, LANES:])
        return mt if m_acc is None else jnp.maximum(m_acc, mt)

    def n_tiles(rb):
        return len(kv_blocks[rb][0]) if 0 <= rb < n_rb else 0

    outs = []
    row_max = {}
    for step in range(n_rb + 2):
        rb_score, rb_max, rb_exp = step, step - 1, step - 2
        m_acc, acc = None, None
        for c in range(max(n_tiles(rb_score), n_tiles(rb_max), n_tiles(rb_exp))):
            if c < n_tiles(rb_score):
                score_tile(rb_score, c)
            if c < n_tiles(rb_max):
                m_acc = max_tile(rb_max, c, m_acc)
            if c < n_tiles(rb_exp):
                e = jnp.exp2(s_ref[tile_of(rb_exp, c)] - row_max[rb_exp])
                pv = _dot(e.astype(BF16), kv_blocks[rb_exp][1][c])
                acc = pv if acc is None else acc + pv
        if m_acc is not None:
            row_max[rb_max] = m_acc.max(axis=-1, keepdims=True)
        if acc is not None:
            outs.append(acc)
    return outs


def _lane_low_half():
    return lax.broadcasted_iota(jnp.int32, (1, LANES), 1) < (LANES // 2)


def _values_and_ones(v_tiles):
    ones = jnp.ones((KEY_TILE, LANES), BF16)
    return [jnp.concatenate([v, ones], axis=1) for v in v_tiles]


def _normalise_pair(acc_lo, acc_hi, low):
    return jnp.where(low, acc_lo[:, :LANES] / acc_lo[:, LANES:], acc_hi[:, :LANES] / acc_hi[:, LANES:])


def _split_lane_halves(q, low):
    zero = jnp.zeros_like(q)
    return jnp.where(low, q, zero), jnp.where(low, zero, q)


def _tiles(ref, n_tiles, j, bi=0):
    return [ref[bi, c * KEY_TILE:(c + 1) * KEY_TILE, j:j + LANES].astype(BF16) for c in range(n_tiles)]


def _ctx_attn_kernel(q_ref, k_ref, v_ref, g_ref, o_ref, s_ref):
    low = _lane_low_half()
    nb, tq, w = q_ref.shape
    n_tiles = k_ref.shape[1] // KEY_TILE
    tiles = [(bi, j, r) for bi in range(nb) for j in range(0, w, LANES) for r in range(0, tq, ROW_BLK)]
    q_blocks, kv_blocks, kv = [], [], {}
    for bi, j, r in tiles:
        if (bi, j) not in kv:
            kv[bi, j] = (_tiles(k_ref, n_tiles, j, bi), _values_and_ones(_tiles(v_ref, n_tiles, j, bi)))
        q_blocks += list(_split_lane_halves(q_ref[bi, r:r + ROW_BLK, j:j + LANES], low))
        kv_blocks += [kv[bi, j]] * 2
    res = _attend(q_blocks, kv_blocks, s_ref)
    for i, (bi, j, r) in enumerate(tiles):
        o = _normalise_pair(res[2 * i], res[2 * i + 1], low)
        gate = _silu(g_ref[bi, r:r + ROW_BLK, j:j + LANES].astype(F32))
        o_ref[bi, r:r + ROW_BLK, j:j + LANES] = (o * gate).astype(o_ref.dtype)


def _ctx_attn(q, k, v, g):
    b, t, w = q.shape
    nb = CTX_BATCH_PER_STEP
    spec = pl.BlockSpec((nb, t, w), lambda i: (i, 0, 0))
    return pl.pallas_call(
        _ctx_attn_kernel,
        grid=(b // nb,),
        in_specs=[spec] * 4,
        out_specs=spec,
        out_shape=jax.ShapeDtypeStruct((b, t, w), BF16),
        scratch_shapes=[pltpu.VMEM((SCORE_SLOTS * ROW_BLK, k.shape[1]), F32)],
        compiler_params=_params("parallel"),
        name="ctx_attn",
    )(q, k, v, g)


def _na_pair_tiles(rpb_ref, head, pt_ref):
    right = lax.broadcasted_iota(jnp.int32, (GRID_W, LANES), 1) >= GRID_W
    zero_row = jnp.zeros((1, LANES), F32)

    def toeplitz(row, lane0):
        shift = (lane0 - (NA_COLS - 1)) % LANES
        return pltpu.roll(jnp.broadcast_to(row, (GRID_W, LANES)), shift, axis=1, stride=1, stride_axis=0)

    for p in range(NA_PAIR_TILES):
        left_row = rpb_ref[head, p - 1:p, :] if p >= 1 else zero_row
        right_row = rpb_ref[head, p:p + 1, :]
        pt_ref[head, p] = jnp.where(right, toeplitz(right_row, GRID_W), toeplitz(left_row, 0)) * LOG2E


def _na_row_start(r, n_rows):
    kr = min(NA_ROWS_MAX, n_rows)
    return min(max(r - kr // 2, 0), n_rows - kr)


def _na_window(grp, n_rows):
    kr = min(NA_ROWS_MAX, n_rows)
    starts = [_na_row_start(grp * NA_Q_ROWS + i, n_rows) for i in range(NA_Q_ROWS)]
    rows_per_tile = KEY_TILE // GRID_W
    need = -(-(max(starts) + kr - min(starts)) // rows_per_tile) * rows_per_tile
    return min(min(starts), n_rows - need), need


def _na_kernel(q_ref, k_ref, v_ref, ck_ref, cv_ref, rpb_ref, g_ref, o_ref, pt_ref, badd_ref, s_ref, *, n_rows):
    kr = min(NA_ROWS_MAX, n_rows)
    n_grp = n_rows // NA_Q_ROWS
    n_pairs = q_ref.shape[2] // LANES
    low = _lane_low_half()

    @pl.when(pl.program_id(1) == 0)
    def _build_bias():
        for head in range(2 * n_pairs):
            _na_pair_tiles(rpb_ref, head, pt_ref)
        qcol = lax.broadcasted_iota(jnp.int32, (GRID_W, LANES), 0)
        lane = lax.broadcasted_iota(jnp.int32, (GRID_W, LANES), 1)
        right = lane >= GRID_W
        kcol = lane & (GRID_W - 1)
        col_start = jnp.clip(qcol - NA_COLS // 2, 0, GRID_W - NA_COLS)
        col_ok = (kcol >= col_start) & (kcol < col_start + NA_COLS)
        for grp in range(n_grp):
            krow0, key_rows = _na_window(grp, n_rows)
            for i in range(NA_Q_ROWS):
                r = grp * NA_Q_ROWS + i
                row_start = _na_row_start(r, n_rows)
                for m in range(key_rows // 2):
                    key_row = krow0 + 2 * m
                    ok_l = row_start <= key_row < row_start + kr
                    ok_r = row_start <= key_row + 1 < row_start + kr
                    rows = slice(i * GRID_W, (i + 1) * GRID_W)
                    cols = slice(m * LANES, (m + 1) * LANES)
                    if not (ok_l or ok_r):
                        for head in range(2 * n_pairs):
                            badd_ref[head * n_grp + grp, rows, cols] = jnp.full((GRID_W, LANES), -jnp.inf, F32)
                        continue
                    ok = col_ok
                    if not ok_r:
                        ok = ok & jnp.logical_not(right)
                    if not ok_l:
                        ok = ok & right
                    p = key_row - r + NA_ROWS_MAX
                    for head in range(2 * n_pairs):
                        badd_ref[head * n_grp + grp, rows, cols] = jnp.where(ok, pt_ref[head, p], -jnp.inf)

    n_ctx = ck_ref.shape[1] // KEY_TILE
    q_blocks, kv_blocks, blocks = [], [], []
    for pair in range(n_pairs):
        lanes = slice(pair * LANES, (pair + 1) * LANES)
        ctx_k = _tiles(ck_ref, n_ctx, pair * LANES)
        ctx_v = _tiles(cv_ref, n_ctx, pair * LANES)
        for grp in range(n_grp):
            krow0, key_rows = _na_window(grp, n_rows)
            k0 = krow0 * GRID_W
            win = [slice(k0 + c * KEY_TILE, k0 + (c + 1) * KEY_TILE) for c in range(key_rows * GRID_W // KEY_TILE)]
            kv = ([k_ref[0, w, lanes] for w in win] + ctx_k,
                  _values_and_ones([v_ref[0, w, lanes] for w in win] + ctx_v))
            q_blocks += list(_split_lane_halves(q_ref[0, grp * ROW_BLK:(grp + 1) * ROW_BLK, lanes], low))
            kv_blocks += [kv, kv]
            blocks += [(2 * pair, grp, len(win)), (2 * pair + 1, grp, len(win))]

    def masked_bias(rb, c, s):
        head, grp, n_win = blocks[rb]
        if c >= n_win:
            return None
        cols = slice(c * KEY_TILE, (c + 1) * KEY_TILE)
        return s + badd_ref[head * n_grp + grp, :, cols]

    res = _attend(q_blocks, kv_blocks, s_ref, masked_bias)
    for pair in range(n_pairs):
        lanes = slice(pair * LANES, (pair + 1) * LANES)
        for grp in range(n_grp):
            rb = 2 * (pair * n_grp + grp)
            o = _normalise_pair(res[rb], res[rb + 1], low)
            rows = slice(grp * ROW_BLK, (grp + 1) * ROW_BLK)
            o_ref[0, rows, lanes] = (o * _silu(g_ref[0, rows, lanes].astype(F32))).astype(o_ref.dtype)


def _na_attn(q, k, v, ck, cv, rpb, g):
    _, rpb_rows, rpb_cols = rpb.shape
    assert rpb_rows == RPB_ROWS and rpb_cols == RPB_COLS
    rpb = jnp.pad(rpb, ((0, 0), (0, NA_PAIR_TILES - rpb_rows), (0, LANES - rpb_cols)))
    b, t, w = q.shape
    c = ck.shape[1]
    n_rows = t // GRID_W
    assert NA_Q_ROWS * GRID_W == ROW_BLK and n_rows % NA_Q_ROWS == 0 and n_rows >= NA_K_ROWS
    n_grp = n_rows // NA_Q_ROWS
    n_win = NA_K_ROWS * GRID_W
    wb = NA_PAIRS_PER_STEP * LANES
    hb = 2 * NA_PAIRS_PER_STEP
    tspec = pl.BlockSpec((1, t, wb), lambda hp, bi: (bi, 0, hp))
    cspec = pl.BlockSpec((1, c, wb), lambda hp, bi: (bi, 0, hp))
    ptspec = pl.BlockSpec((hb, NA_PAIR_TILES, LANES), lambda hp, bi: (hp, 0, 0))
    return pl.pallas_call(
        functools.partial(_na_kernel, n_rows=n_rows),
        grid=(w // wb, b),
        in_specs=[tspec, tspec, tspec, cspec, cspec, ptspec, tspec],
        out_specs=tspec,
        out_shape=jax.ShapeDtypeStruct((b, t, w), BF16),
        scratch_shapes=[pltpu.VMEM((hb, NA_PAIR_TILES, GRID_W, LANES), F32),
                        pltpu.VMEM((hb * n_grp, ROW_BLK, n_win), F32),
                        pltpu.VMEM((SCORE_SLOTS * ROW_BLK, n_win + c), F32)],
        compiler_params=_params("parallel", "arbitrary"),
        name="na_attn",
    )(q, k, v, ck, cv, rpb, g)


def _softplus(x):
    return jnp.maximum(x, 0.0) + jnp.log(1.0 + jnp.exp(-jnp.abs(x)))


def _lru_kernel(x_ref, gb_ref, h0f_ref, h0b_ref, cw_ref, cb_ref, wg_ref, lam_ref,
                o_ref, ff_ref, fb_ref, xt_s, af_s, uf_s, ab_s, ub_s, hf_s, hb_s):
    nb, t, _ = x_ref.shape
    pad = CONV_PAD_L * nb
    cw = 0.5 * cw_ref[...]
    cb = 0.5 * cb_ref[...]
    wg = wg_ref[0]
    kk = (-0.5 * LRU_C * LOG2E) * _softplus(-lam_ref[...])

    xt_s[0:pad, :] = jnp.zeros((pad, LANES), F32)
    xt_s[pad + t * nb:pad + (t + CONV_PAD_R) * nb, :] = jnp.zeros((CONV_PAD_R * nb, LANES), F32)

    def to_time_major(b, carry):
        xt_s[pl.ds(pad + b, t, stride=nb), :] = x_ref[b]
        return carry

    lax.fori_loop(0, nb, to_time_major, 0)

    rows_per_chunk = LRU_CHUNK * nb
    lane = lax.broadcasted_iota(jnp.int32, (rows_per_chunk, LANES), 1)
    bias_cols = jnp.where(lane < 2, 1.0, 0.0).astype(BF16)

    def gates(c, carry):
        r0 = pl.multiple_of(c * rows_per_chunk, rows_per_chunk)

        def tap(i):
            return xt_s[pl.ds(r0 + i * nb, rows_per_chunk), :]

        xh = cw[0:1] * tap(0) + cw[1:2] * tap(1) + cw[2:3] * tap(2) + cw[3:4] * tap(3) + cb
        th = jnp.tanh(_dot(jnp.concatenate([xh.astype(BF16), bias_cols], axis=1), wg))
        rows = pl.ds(r0, rows_per_chunk)
        for d, (a_s, u_s) in enumerate(((af_s, uf_s), (ab_s, ub_s))):
            t_a = th[:, (2 * d) * LANES:(2 * d + 1) * LANES]
            t_x = th[:, (2 * d + 1) * LANES:(2 * d + 2) * LANES]
            a = jnp.exp2(kk[d:d + 1] * t_a + kk[d:d + 1])
            y = 1.0 - a * a
            sq = y * lax.rsqrt(jnp.maximum(y, 1e-30))
            a_s[rows, :] = a
            u_s[rows, :] = sq * (xh * t_x + xh)
        return carry

    lax.fori_loop(0, t // LRU_CHUNK, gates, 0)

    ts = t // LRU_SEGMENTS
    refs = ((af_s, uf_s), (ab_s, ub_s))

    slab = nb * LRU_UNROLL

    def slab_rows(d, j, i):
        base = (j * ts + i * LRU_UNROLL) * nb
        return pl.ds(pl.multiple_of(base if d == 0 else t * nb - slab - base, slab), slab)

    def step_of(d, block, k):
        kk_ = k if d == 0 else LRU_UNROLL - 1 - k
        return block[kk_ * nb:(kk_ + 1) * nb]

    def compose(i, carry):
        state = list(carry)
        for d in range(2):
            for j in range(LRU_SEGMENTS - 1):
                n = d * (LRU_SEGMENTS - 1) + j
                a_blk = refs[d][0][slab_rows(d, j, i), :]
                u_blk = refs[d][1][slab_rows(d, j, i), :]
                g, big_a = state[n]
                for k in range(LRU_UNROLL):
                    a = step_of(d, a_blk, k)
                    g, big_a = a * g + step_of(d, u_blk, k), a * big_a
                state[n] = (g, big_a)
        return tuple(state)

    ones = jnp.ones((nb, LANES), F32)
    h0 = (h0f_ref[...], h0b_ref[...])
    init = tuple((h0[d] if j == 0 else jnp.zeros((nb, LANES), F32), ones)
                 for d in range(2) for j in range(LRU_SEGMENTS - 1))
    ends = lax.fori_loop(0, ts // LRU_UNROLL, compose, init)
    starts = []
    for d in range(2):
        h = h0[d]
        for j in range(LRU_SEGMENTS):
            starts.append(h)
            if j < LRU_SEGMENTS - 1:
                g, big_a = ends[d * (LRU_SEGMENTS - 1) + j]
                h = g if j == 0 else g + big_a * h

    def rerun(i, carry):
        state = list(carry)
        for d, h_s in enumerate((hf_s, hb_s)):
            for j in range(LRU_SEGMENTS):
                n = d * LRU_SEGMENTS + j
                a_blk = refs[d][0][slab_rows(d, j, i), :]
                u_blk = refs[d][1][slab_rows(d, j, i), :]
                hs = []
                for k in range(LRU_UNROLL):
                    state[n] = step_of(d, a_blk, k) * state[n] + step_of(d, u_blk, k)
                    hs.append(state[n])
                h_s[slab_rows(d, j, i), :] = jnp.concatenate(hs if d == 0 else hs[::-1], axis=0)
        return tuple(state)

    final = lax.fori_loop(0, ts // LRU_UNROLL, rerun, tuple(starts))
    ff_ref[...] = final[LRU_SEGMENTS - 1]
    fb_ref[...] = final[2 * LRU_SEGMENTS - 1]

    def add_directions(c, carry):
        rows = pl.ds(pl.multiple_of(c * rows_per_chunk, rows_per_chunk), rows_per_chunk)
        hf_s[rows, :] = hf_s[rows, :] + hb_s[rows, :]
        return carry

    lax.fori_loop(0, t // LRU_CHUNK, add_directions, 0)

    def emit(b, carry):
        hs = hf_s[pl.ds(b, t, stride=nb), :]
        o_ref[b] = (hs * _silu(gb_ref[b].astype(F32))).astype(o_ref.dtype)
        return carry

    lax.fori_loop(0, nb, emit, 0)


def _lru_branch(xb, gb, h0f, h0b, conv_w, conv_b, wg, lam):
    b, t, w = xb.shape
    nb = SUBLANES
    assert t % (LRU_SEGMENTS * LRU_UNROLL) == 0 and t % LRU_CHUNK == 0
    assert conv_w.shape[0] == CONV_PAD_L + 1 + CONV_PAD_R
    xspec = pl.BlockSpec((nb, t, LANES), lambda i, c: (i, 0, c))
    hspec = pl.BlockSpec((nb, LANES), lambda i, c: (i, c))
    scratch = [pltpu.VMEM(((t + CONV_PAD_L + CONV_PAD_R) * nb, LANES), F32)]
    scratch += [pltpu.VMEM((t * nb, LANES), F32) for _ in range(6)]
    return pl.pallas_call(
        _lru_kernel,
        grid=(b // nb, w // LANES),
        in_specs=[xspec, xspec, hspec, hspec,
                  pl.BlockSpec((conv_w.shape[0], LANES), lambda i, c: (0, c)),
                  pl.BlockSpec((1, LANES), lambda i, c: (0, c)),
                  pl.BlockSpec((1, 2 * LANES, 4 * LANES), lambda i, c: (c, 0, 0)),
                  pl.BlockSpec((2, LANES), lambda i, c: (0, c))],
        out_specs=[xspec, hspec, hspec],
        out_shape=[jax.ShapeDtypeStruct((b, t, w), BF16),
                   jax.ShapeDtypeStruct((b, w), F32),
                   jax.ShapeDtypeStruct((b, w), F32)],
        scratch_shapes=scratch,
        compiler_params=_params("parallel", "parallel"),
        name="lru_branch",
    )(xb, gb, h0f, h0b, conv_w, conv_b.reshape(1, w), wg, lam)


def _lru_gate_weights(wa, ba, wx, bx):
    def tiles(w):
        w2 = w.reshape(LRU_BLOCKS // 2, 2, LRU_BLOCK, LRU_BLOCK)
        z = jnp.zeros_like(w2[:, 0])
        return jnp.concatenate([jnp.concatenate([w2[:, 0], z], axis=2),
                                jnp.concatenate([z, w2[:, 1]], axis=2)], axis=1)
    wg = jnp.concatenate([tiles(wa[0]), tiles(wx[0]), tiles(wa[1]), tiles(wx[1])], axis=2)
    n_tiles = LRU_WIDTH // LANES
    bg = jnp.stack([ba[0], bx[0], ba[1], bx[1]], axis=0).reshape(4, n_tiles, LANES)
    bg = 0.5 * jnp.transpose(bg, (1, 0, 2)).reshape(n_tiles, 1, 4 * LANES)
    hi = bg.astype(BF16)
    lo = (bg - hi.astype(F32)).astype(BF16)
    rest = jnp.zeros((n_tiles, LANES - 2, 4 * LANES), BF16)
    return jnp.concatenate([wg.astype(BF16), hi, lo, rest], axis=1)


def _packed_tiles(ref, n_heads, head, bi=0):
    n_tiles = ref.shape[1] // (n_heads * KEY_TILE)
    return [ref[bi, pl.ds(c * KEY_TILE * n_heads + head, KEY_TILE, stride=n_heads), :].astype(BF16)
            for c in range(n_tiles)]


def _diff_kernel(*refs, packed, n_heads, lam_init):
    n_kv = len(packed)
    q_ref = refs[0]
    kv_refs = refs[1:1 + 2 * n_kv]
    g_ref, lamp_ref, subg_ref, o_ref, s_ref = refs[1 + 2 * n_kv:]
    low = _lane_low_half()
    tq = q_ref.shape[1]
    first_head = pl.program_id(1) * (q_ref.shape[2] // LANES)
    lp = lamp_ref[...]
    lam = (jnp.exp(jnp.sum(lp[0:1] * lp[1:2], axis=-1, keepdims=True))
           - jnp.exp(jnp.sum(lp[2:3] * lp[3:4], axis=-1, keepdims=True)) + lam_init)
    sub_gain = subg_ref[...] * (1.0 - lam_init)
    nb = q_ref.shape[0]
    tiles = [(bi, j) for bi in range(nb) for j in range(0, q_ref.shape[2], LANES)]
    q_blocks, kv_blocks = [], []
    for bi, j in tiles:
        k_tiles, v_tiles = [], []
        for i in range(n_kv):
            if packed[i]:
                head = first_head + j // LANES
                k_tiles += _packed_tiles(kv_refs[2 * i], n_heads, head, bi)
                v_tiles += _packed_tiles(kv_refs[2 * i + 1], n_heads, head, bi)
            else:
                n_tiles = kv_refs[2 * i].shape[1] // KEY_TILE
                k_tiles += _tiles(kv_refs[2 * i], n_tiles, j, bi)
                v_tiles += _tiles(kv_refs[2 * i + 1], n_tiles, j, bi)
        v_tiles = _values_and_ones(v_tiles)
        for r in range(0, tq, ROW_BLK):
            q_blocks += list(_split_lane_halves(q_ref[bi, r:r + ROW_BLK, j:j + LANES], low))
            kv_blocks += [(k_tiles, v_tiles)] * 2
    res = _attend(q_blocks, kv_blocks, s_ref)
    i = 0
    for bi, j in tiles:
        for r in range(0, tq, ROW_BLK):
            acc1, acc2 = res[i], res[i + 1]
            i += 2
            o = acc1[:, :LANES] / acc1[:, LANES:] - acc2[:, :LANES] * (lam / acc2[:, LANES:])
            o = o * lax.rsqrt(jnp.mean(o * o, axis=-1, keepdims=True) + EPS) * sub_gain
            gate = _silu(g_ref[bi, r:r + ROW_BLK, j:j + LANES].astype(F32))
            o_ref[bi, r:r + ROW_BLK, j:j + LANES] = (o * gate).astype(o_ref.dtype)


def _diff_attn(q, kv_list, g, lam_params, sub_g, lam_init, tq, heads_per_step, batch_per_step=1):
    b, t, w = q.shape
    wb = heads_per_step * LANES
    nb = batch_per_step
    qspec = pl.BlockSpec((nb, tq, wb), lambda bi, h, qi: (bi, qi, h))
    n_heads = w // LANES
    in_specs = [qspec]
    args = [q]
    n_keys = 0
    for k, v, packed in kv_list:
        if packed:
            spec = pl.BlockSpec((nb, k.shape[1], LANES), lambda bi, h, qi: (bi, 0, 0))
            n_keys += k.shape[1] // n_heads
        else:
            spec = pl.BlockSpec((nb, k.shape[1], wb), lambda bi, h, qi: (bi, 0, h))
            n_keys += k.shape[1]
        in_specs += [spec, spec]
        args += [k, v]
    in_specs += [qspec,
                 pl.BlockSpec(lam_params.shape, lambda bi, h, qi: (0, 0)),
                 pl.BlockSpec((1, LANES), lambda bi, h, qi: (0, 0))]
    args += [g, lam_params, sub_g.reshape(1, LANES)]
    return pl.pallas_call(
        functools.partial(_diff_kernel, packed=tuple(p for _, _, p in kv_list), n_heads=n_heads,
                          lam_init=lam_init),
        grid=(b // nb, w // wb, t // tq),
        in_specs=in_specs,
        out_specs=qspec,
        out_shape=jax.ShapeDtypeStruct((b, t, w), BF16),
        scratch_shapes=[pltpu.VMEM((SCORE_SLOTS * ROW_BLK, n_keys), F32)],
        compiler_params=_params("parallel", "parallel", "parallel"),
        name="diff_attn",
    )(*args)


def _rope_tables(t_len):
    m = DIFF_HEAD_DIM // 4
    t = jnp.arange(t_len)
    rows = (t // GRID_W).astype(F32)
    cols = (t % GRID_W).astype(F32)
    inv = ROPE_BASE ** (-jnp.arange(m, dtype=F32) / m)
    ang_r = rows[:, None] * inv[None, :]
    ang_c = cols[:, None] * inv[None, :]
    cos = jnp.concatenate([jnp.cos(ang_r)] * 2 + [jnp.cos(ang_c)] * 2, axis=1)
    sin = jnp.concatenate([-jnp.sin(ang_r), jnp.sin(ang_r), -jnp.sin(ang_c), jnp.sin(ang_c)], axis=1)
    reps = LANES // DIFF_HEAD_DIM
    return jnp.tile(cos, (1, reps)), jnp.tile(sin, (1, reps))


def kernel(x_prompt, x_sample, c, cache_na_k, cache_na_v, state_lru, cache_diff_k, cache_diff_v, c_ctx,
           e_norm, e_ada_w, e_ada_b, e_w_in, e_rpb, e_conv_w, e_conv_b, e_lru_wa, e_lru_ba, e_lru_wx, e_lru_bx,
           e_lru_lam, e_w_out, o_norm, o_ada_w, o_ada_b, o_w_in, o_lq1, o_lk1, o_lq2, o_lk2, o_sub_g, o_w_out,
           final_norm):
    bp, tp, d = x_prompt.shape
    bs, ts, _ = x_sample.shape
    past = cache_na_k.shape[2]
    pad_rows = 2 * SUBLANES - bs - 1
    cc = jnp.concatenate([c, c_ctx[None, :], jnp.zeros((pad_rows, d), F32)], axis=0)
    nw, lw, dw = NA_WIDTH, LRU_WIDTH, DIFF_WIDTH
    rope_tables = _rope_tables(ts) if DEPTH > 1 else None

    def layer_params(i):
        j = i // 2
        if i % 2 == 0:
            return e_ada_w[j], e_ada_b[j], e_norm[j], e_w_in[j], e_w_out[j]
        return o_ada_w[j], o_ada_b[j], o_norm[j], o_w_in[j], o_w_out[j]

    mods = []
    for i in range(DEPTH):
        ada_w, ada_b = layer_params(i)[:2]
        m = _modulation(cc, ada_w, ada_b)
        mods.append((m[:bs].reshape(bs, 1, 3 * d), m[bs:bs + 1].reshape(1, 1, 3 * d)))

    def projection(i, latent):
        norm_g, w_in = layer_params(i)[2:4]
        mod = mods[i][0 if latent else 1]
        kv_dtype = BF16 if latent else F32
        if i % 2 == 0:
            qmul = NA_HEAD_DIM ** -0.5 * LOG2E
            outs = [(nw, BF16, False, qmul, False), (nw, kv_dtype, False, None, False),
                    (nw, kv_dtype, False, None, False), (nw, BF16, False, None, False),
                    (lw, F32, False, None, False), (lw, BF16, False, None, False)]
            return mod, norm_g, w_in, outs, None
        qmul = DIFF_HEAD_DIM ** -0.5 * LOG2E
        outs = [(dw, BF16, latent, qmul, False), (dw, kv_dtype, latent, None, not latent),
                (dw, kv_dtype, False, None, not latent), (dw, BF16, False, None, False)]
        return mod, norm_g, w_in, outs, rope_tables if latent else None

    xp, xs = x_prompt, x_sample
    _, proj_p = _stage(xp, nxt=projection(0, False), tm=512)
    _, proj_s = _stage(xs, nxt=projection(0, True), tm=512)
    na_k, na_v, lru_s, df_k, df_v = [], [], [], [], []
    for i in range(DEPTH):
        j = i // 2
        w_out = layer_params(i)[4]
        if i % 2 == 0:
            wg = _lru_gate_weights(e_lru_wa[j], e_lru_ba[j], e_lru_wx[j], e_lru_bx[j])

            q, k, v, ga, xb, gb = proj_p
            oa = _ctx_attn(q, k, v, ga)
            zeros = jnp.zeros((bp, lw), F32)
            ob, ff, fb = _lru_branch(xb, gb, zeros, zeros, e_conv_w[j], e_conv_b[j], wg, e_lru_lam[j])
            mixed_p = [oa, ob]
            na_k.append(k.reshape(bp, tp, NA_HEADS, NA_HEAD_DIM))
            na_v.append(v.reshape(bp, tp, NA_HEADS, NA_HEAD_DIM))
            lru_s.append(jnp.stack([ff, fb], axis=1))

            q, k, v, ga, xb, gb = proj_s
            ck = cache_na_k[:, j].reshape(bs, past, nw)
            cv = cache_na_v[:, j].reshape(bs, past, nw)
            oa = _na_attn(q, k, v, ck, cv, e_rpb[j], ga)
            ob, _, _ = _lru_branch(xb, gb, state_lru[:, j, 0], state_lru[:, j, 1],
                                   e_conv_w[j], e_conv_b[j], wg, e_lru_lam[j])
            mixed_s = [oa, ob]
        else:
            lam_init = 0.8 - 0.6 * math.exp(-0.3 * i)
            lam_params = jnp.stack([o_lq1[j], o_lk1[j], o_lq2[j], o_lk2[j]], axis=0)

            q, k, v, g = proj_p
            mixed_p = [_diff_attn(q, [(k, v, True)], g, lam_params, o_sub_g[j], lam_init, tq=tp,
                                  heads_per_step=DIFF_HEADS, batch_per_step=CTX_BATCH_PER_STEP)]
            df_k.append(k.reshape(bp, tp, DIFF_HEADS, 2 * DIFF_HEAD_DIM))
            df_v.append(v.reshape(bp, tp, DIFF_HEADS, 2 * DIFF_HEAD_DIM))

            q, k, v, g = proj_s
            ck = cache_diff_k[:, j].reshape(bs, past * DIFF_HEADS, 2 * DIFF_HEAD_DIM)
            cv = cache_diff_v[:, j].reshape(bs, past * DIFF_HEADS, 2 * DIFF_HEAD_DIM)
            mixed_s = [_diff_attn(q, [(ck, cv, True), (k, v, False)], g, lam_params, o_sub_g[j], lam_init, tq=ts,
                                  heads_per_step=4)]

        if i == DEPTH - 1:
            xp = _final_stage(xp, mixed_p, w_out, mods[i][1], final_norm)
            xs = _final_stage(xs, mixed_s, w_out, mods[i][0], final_norm)
        else:
            xp, proj_p = _stage(xp, prev=(mixed_p, w_out, mods[i][1]), nxt=projection(i + 1, False), tm=512)
            xs, proj_s = _stage(xs, prev=(mixed_s, w_out, mods[i][0]), nxt=projection(i + 1, True), tm=512)

    return (xp, xs, jnp.stack(na_k, axis=1), jnp.stack(na_v, axis=1), jnp.stack(lru_s, axis=1),
            jnp.stack(df_k, axis=1), jnp.stack(df_v, axis=1))
```

```python
import functools
import math

import jax
import jax.numpy as jnp
from jax import lax
from jax.experimental import pallas as pl
from jax.experimental.pallas import tpu as pltpu

DEPTH = 2
GRID_W = 64
NA_HEADS = 8
NA_HEAD_DIM = 64
NA_WIDTH = NA_HEADS * NA_HEAD_DIM
NA_ROWS_MAX = 8
NA_COLS = 16
RPB_ROWS = 2 * NA_ROWS_MAX - 1
RPB_COLS = 2 * NA_COLS - 1
LRU_WIDTH = 512
LRU_BLOCKS = 8
LRU_BLOCK = LRU_WIDTH // LRU_BLOCKS
LRU_C = 8.0
DIFF_HEADS = 8
DIFF_HEAD_DIM = 64
DIFF_WIDTH = DIFF_HEADS * 2 * DIFF_HEAD_DIM
ROPE_BASE = 10000.0
EPS = 1e-6
LOG2E = math.log2(math.e)

LANES = 128
SUBLANES = 8
MXU_DIM = 256
VMEM_LIMIT = 56 * 1024 * 1024

CAST_COLS = 512
IN_BUFS = 4
LRU_SEGMENTS = 4
LRU_UNROLL = 8
LRU_CHUNK = 256
CONV_PAD_L = 2
CONV_PAD_R = 1
ROW_BLK = MXU_DIM
KEY_TILE = MXU_DIM
SCORE_SLOTS = 4

NA_Q_ROWS = 4
NA_K_ROWS = 12
NA_PAIR_TILES = 16
NA_PAIRS_PER_STEP = 4
CTX_BATCH_PER_STEP = 4

F32 = jnp.float32
BF16 = jnp.bfloat16


def _params(*semantics):
    return pltpu.CompilerParams(dimension_semantics=semantics, vmem_limit_bytes=VMEM_LIMIT)


def _silu(x):
    h = 0.5 * x
    return h * jnp.tanh(h) + h


def _dot(a, b):
    return jnp.dot(a, b, preferred_element_type=F32)


def _dot_nt(a, b):
    return lax.dot_general(a, b, (((1,), (1,)), ((), ())), preferred_element_type=F32)


def _mod_kernel(c_ref, w_ref, b_ref, o_ref):
    s = _silu(c_ref[...])
    o_ref[...] = _dot(s.astype(BF16), w_ref[...].astype(BF16)) + b_ref[...]


def _modulation(cc, w, b):
    rows, d = cc.shape
    n = w.shape[1]
    tn = 1024
    return pl.pallas_call(
        _mod_kernel,
        grid=(n // tn,),
        in_specs=[pl.BlockSpec((rows, d), lambda j: (0, 0)),
                  pl.BlockSpec((d, tn), lambda j: (0, j)),
                  pl.BlockSpec((1, tn), lambda j: (0, j))],
        out_specs=pl.BlockSpec((rows, tn), lambda j: (0, j)),
        out_shape=jax.ShapeDtypeStruct((rows, n), F32),
        compiler_params=_params("arbitrary"),
        name="modulation",
    )(cc, w, b.reshape(1, n))


def _rope_block(x, cos, sin_signed, low16):
    partner = jnp.where(low16, pltpu.roll(x, LANES - 16, axis=1), pltpu.roll(x, 16, axis=1))
    return x * cos + partner * sin_signed


def _cast_weights_once(w_ref, wbf_ref):
    @pl.when((pl.program_id(0) == 0) & (pl.program_id(1) == 0))
    def _():
        for off in range(0, w_ref.shape[1], CAST_COLS):
            wbf_ref[:, off:off + CAST_COLS] = w_ref[:, off:off + CAST_COLS].astype(BF16)


def _weight_chunk_copy(w_hbm, stage_ref, sem, c):
    cols = slice(c * CAST_COLS, (c + 1) * CAST_COLS)
    return pltpu.make_async_copy(w_hbm.at[:, cols], stage_ref.at[:, cols], sem.at[c])


def _land_weight_chunks(w_hbm, stage_ref, sem, wbf_ref, off, width):
    for c in range(off // CAST_COLS, (off + width) // CAST_COLS):
        _weight_chunk_copy(w_hbm, stage_ref, sem, c).wait()
        cols = slice(c * CAST_COLS, (c + 1) * CAST_COLS)
        wbf_ref[:, cols] = stage_ref[:, cols].astype(BF16)


def _norm_project(x, mod_ref, g_ref, wbf_ref, rope_refs, out_refs, outs, land=None):
    tm, d = x.shape
    y = x * lax.rsqrt(jnp.mean(x * x, axis=-1, keepdims=True) + EPS) * g_ref[...]
    mod = mod_ref[0]
    h = (y * (1.0 + mod[:, d:2 * d]) + mod[:, :d]).astype(BF16)
    if rope_refs:
        cos = rope_refs[0][...]
        sin = rope_refs[1][...]
        lane = lax.broadcasted_iota(jnp.int32, (1, LANES), 1)
        low16 = (lane & 16) == 0
    off = 0
    for o_ref, (width, roped, mul, packed) in zip(out_refs, outs):
        if land is not None:
            land(off, width)
        acc = _dot(h, wbf_ref[:, off:off + width])
        if roped:
            acc = jnp.concatenate(
                [_rope_block(acc[:, j:j + LANES], cos, sin, low16) for j in range(0, width, LANES)], axis=1)
        if mul is not None:
            acc = acc * mul
        if packed:
            n_heads = width // LANES
            t = o_ref.shape[1] // n_heads
            for bi in range(o_ref.shape[0]):
                for hh in range(n_heads):
                    o_ref[bi, pl.ds(hh, t, stride=n_heads), :] = (
                        acc[bi * t:(bi + 1) * t, hh * LANES:(hh + 1) * LANES].astype(o_ref.dtype))
        else:
            o_ref[...] = acc.astype(o_ref.dtype).reshape(o_ref.shape)
        off += width


def _token_rows(ref):
    return ref[...].reshape(-1, ref.shape[-1])


def _gated_residual(o_refs, wbf_ref, x_ref, mod_ref):
    d = x_ref.shape[-1]
    acc, row = None, 0
    for o_ref in o_refs:
        k = o_ref.shape[-1]
        part = _dot(_token_rows(o_ref), wbf_ref[row:row + k, :])
        acc = part if acc is None else acc + part
        row += k
    return _token_rows(x_ref) + mod_ref[0][:, 2 * d:] * acc


def _stage_kernel(*refs, n_o, final, outs, rope):
    refs = list(refs)
    take = lambda n: [refs.pop(0) for _ in range(n)]
    if n_o:
        o_refs = take(n_o)
        wout_ref, x_ref, mod_prev_ref = take(3)
        fin_ref = take(1)[0] if final else None
    else:
        x_ref, = take(1)
    if outs:
        mod_ref, g_ref, win_ref = take(3)
        rope_refs = take(2) if rope else []
    xout_ref = take(1)[0] if n_o else None
    out_refs = take(len(outs))
    wout_bf = take(1)[0] if n_o else None
    if n_o:
        _cast_weights_once(wout_ref, wout_bf)

    def step(land):
        if n_o:
            x = _gated_residual(o_refs, wout_bf, x_ref, mod_prev_ref)
            if final:
                x_new = x * lax.rsqrt(jnp.mean(x * x, axis=-1, keepdims=True) + EPS) * fin_ref[...]
            else:
                x_new = x
            xout_ref[...] = x_new.reshape(xout_ref.shape)
        else:
            x = _token_rows(x_ref)
        if outs:
            _norm_project(x, mod_ref, g_ref, win_bf, rope_refs, out_refs, outs, land)

    if not outs:
        step(None)
        return
    stage_ref, win_bf, sem = take(3)
    first = (pl.program_id(0) == 0) & (pl.program_id(1) == 0)

    @pl.when(first)
    def _():
        for c in range(win_ref.shape[1] // CAST_COLS):
            _weight_chunk_copy(win_ref, stage_ref, sem, c).start()
        step(functools.partial(_land_weight_chunks, win_ref, stage_ref, sem, win_bf))

    @pl.when(jnp.logical_not(first))
    def _():
        step(None)


def _stage(x, prev=None, nxt=None, final_g=None, tm=256):
    b, t, d = x.shape
    shared_mod = all(m.shape[0] == 1 for m in ([prev[2]] if prev else []) + ([nxt[0]] if nxt else []))
    bb = max(1, tm // t) if shared_mod else 1
    tm = min(tm, t)
    tile = lambda width: pl.BlockSpec((bb, tm, width), lambda i, j: (i, j, 0))
    mod_spec = lambda mod: pl.BlockSpec(
        (1, 1, 3 * d), (lambda i, j: (i, 0, 0)) if mod.shape[0] > 1 else (lambda i, j: (0, 0, 0)))
    resident = lambda w: pl.BlockSpec(w.shape, lambda i, j: (0, 0), pipeline_mode=pl.Buffered(1))
    row_spec = pl.BlockSpec((1, d), lambda i, j: (0, 0))
    in_specs, args, out_specs, out_shape, scratch = [], [], [], [], []
    n_o, outs, rope = 0, (), False
    if prev is not None:
        o_list, w_out, mod_prev = prev
        n_o = len(o_list)
        in_specs += [tile(o.shape[-1]) for o in o_list] + [resident(w_out), tile(d), mod_spec(mod_prev)]
        args += list(o_list) + [w_out, x, mod_prev]
        if final_g is not None:
            in_specs.append(row_spec)
            args.append(final_g.reshape(1, d))
        out_specs.append(tile(d))
        out_shape.append(jax.ShapeDtypeStruct((b, t, d), F32))
        scratch.append(pltpu.VMEM(w_out.shape, BF16))
    else:
        in_specs.append(tile(d))
        args.append(x)
    if nxt is not None:
        mod_next, norm_g, w_in, out_list, rope_tables = nxt
        assert all(wd % CAST_COLS == 0 for wd, _, _, _, _ in out_list)
        in_specs += [mod_spec(mod_next), row_spec, pl.BlockSpec(memory_space=pl.ANY)]
        args += [mod_next, norm_g.reshape(1, d), w_in]
        rope = rope_tables is not None
        if rope:
            in_specs += [pl.BlockSpec((tm, LANES), lambda i, j: (j, 0))] * 2
            args += list(rope_tables)
        for width, dt, _, _, packed in out_list:
            if packed:
                heads = width // LANES
                out_specs.append(pl.BlockSpec((bb, tm * heads, LANES), lambda i, j: (i, j, 0)))
                out_shape.append(jax.ShapeDtypeStruct((b, t * heads, LANES), dt))
            else:
                out_specs.append(tile(width))
                out_shape.append(jax.ShapeDtypeStruct((b, t, width), dt))
        outs = tuple((w_, r, m, p) for w_, _, r, m, p in out_list)
        scratch += [pltpu.VMEM(w_in.shape, F32), pltpu.VMEM(w_in.shape, BF16),
                    pltpu.SemaphoreType.DMA((w_in.shape[1] // CAST_COLS,))]
    res = pl.pallas_call(
        functools.partial(_stage_kernel, n_o=n_o, final=final_g is not None, outs=outs, rope=rope),
        grid=(b // bb, t // tm),
        in_specs=in_specs,
        out_specs=out_specs,
        out_shape=out_shape,
        scratch_shapes=scratch,
        compiler_params=_params("arbitrary", "arbitrary"),
        name="stage",
    )(*args)
    return (res[0], list(res[1:])) if n_o else (None, list(res))


def _ring_copy(hbm_ref, buf_ref, sem, step, bb, tm, nt):
    slot = lax.rem(step, IN_BUFS)
    rows = pl.ds(lax.div(step, nt) * bb, bb)
    toks = pl.ds(lax.rem(step, nt) * tm, tm)
    return pltpu.make_async_copy(hbm_ref.at[rows, toks, :], buf_ref.at[slot], sem.at[slot])


def _final_kernel(*refs, n_o, bb, tm, nt, n_steps):
    hbm_refs = refs[:n_o + 1]
    wout_ref, mod_ref, fin_ref, out_ref, wout_bf = refs[n_o + 1:n_o + 6]
    bufs = refs[n_o + 6:2 * n_o + 7]
    sems = refs[2 * n_o + 7:]
    step = pl.program_id(0) * nt + pl.program_id(1)

    def start(s):
        for h, b, sm in zip(hbm_refs, bufs, sems):
            _ring_copy(h, b, sm, s, bb, tm, nt).start()

    @pl.when(step == 0)
    def _():
        for s in range(min(IN_BUFS - 1, n_steps)):
            start(jnp.int32(s))

    @pl.when(step + (IN_BUFS - 1) < n_steps)
    def _():
        start(step + (IN_BUFS - 1))

    for h, b, sm in zip(hbm_refs, bufs, sems):
        _ring_copy(h, b, sm, step, bb, tm, nt).wait()
    _cast_weights_once(wout_ref, wout_bf)
    slot = lax.rem(step, IN_BUFS)
    d = out_ref.shape[-1]
    acc, row = None, 0
    for b in bufs[:n_o]:
        k = b.shape[-1]
        part = _dot(b[slot].reshape(-1, k), wout_bf[row:row + k, :])
        acc = part if acc is None else acc + part
        row += k
    x = bufs[n_o][slot].reshape(-1, d) + mod_ref[0][:, 2 * d:] * acc
    y = x * lax.rsqrt(jnp.mean(x * x, axis=-1, keepdims=True) + EPS) * fin_ref[...]
    out_ref[...] = y.reshape(out_ref.shape)


def _final_stage(x, o_list, w_out, mod, final_g, tm=1024):
    b, t, d = x.shape
    bb = max(1, tm // t) if mod.shape[0] == 1 else 1
    tm = min(tm, t)
    nt = t // tm
    n_steps = (b // bb) * nt
    mod_map = (lambda i, j: (i, 0, 0)) if mod.shape[0] > 1 else (lambda i, j: (0, 0, 0))
    streamed = list(o_list) + [x]
    in_specs = [pl.BlockSpec(memory_space=pl.ANY)] * len(streamed)
    in_specs += [pl.BlockSpec(w_out.shape, lambda i, j: (0, 0), pipeline_mode=pl.Buffered(1)),
                 pl.BlockSpec((1, 1, 3 * d), mod_map),
                 pl.BlockSpec((1, d), lambda i, j: (0, 0))]
    scratch = [pltpu.VMEM(w_out.shape, BF16)]
    scratch += [pltpu.VMEM((IN_BUFS, bb, tm, a.shape[-1]), a.dtype) for a in streamed]
    scratch += [pltpu.SemaphoreType.DMA((IN_BUFS,)) for _ in streamed]
    return pl.pallas_call(
        functools.partial(_final_kernel, n_o=len(o_list), bb=bb, tm=tm, nt=nt, n_steps=n_steps),
        grid=(b // bb, nt),
        in_specs=in_specs,
        out_specs=pl.BlockSpec((bb, tm, d), lambda i, j: (i, j, 0)),
        out_shape=jax.ShapeDtypeStruct((b, t, d), F32),
        scratch_shapes=scratch,
        compiler_params=_params("arbitrary", "arbitrary"),
        name="final_stage",
    )(*streamed, w_out, mod, final_g.reshape(1, d))


def _attend(q_blocks, kv_blocks, s_ref, score_fn=None):
    n_rb = len(q_blocks)

    def tile_of(rb, c):
        slot = rb % SCORE_SLOTS
        return (slice(slot * ROW_BLK, (slot + 1) * ROW_BLK), slice(c * KEY_TILE, (c + 1) * KEY_TILE))

    def score_tile(rb, c):
        s_ref[tile_of(rb, c)] = _dot_nt(q_blocks[rb], kv_blocks[rb][0][c])

    def max_tile(rb, c, m_acc):
        s = s_ref[tile_of(rb, c)]
        adjusted = None if score_fn is None else score_fn(rb, c, s)
        if adjusted is not None:
            s = adjusted
            s_ref[tile_of(rb, c)] = s
        mt = jnp.maximum(s[:, :LANES], s[:, LANES:])
        return mt if m_acc is None else jnp.maximum(m_acc, mt)

    def n_tiles(rb):
        return len(kv_blocks[rb][0]) if 0 <= rb < n_rb else 0

    outs = []
    row_max = {}
    for step in range(n_rb + 2):
        rb_score, rb_max, rb_exp = step, step - 1, step - 2
        m_acc, acc = None, None
        for c in range(max(n_tiles(rb_score), n_tiles(rb_max), n_tiles(rb_exp))):
            if c < n_tiles(rb_score):
                score_tile(rb_score, c)
            if c < n_tiles(rb_max):
                m_acc = max_tile(rb_max, c, m_acc)
            if c < n_tiles(rb_exp):
                e = jnp.exp2(s_ref[tile_of(rb_exp, c)] - row_max[rb_exp])
                pv = _dot(e.astype(BF16), kv_blocks[rb_exp][1][c])
                acc = pv if acc is None else acc + pv
        if m_acc is not None:
            row_max[rb_max] = m_acc.max(axis=-1, keepdims=True)
        if acc is not None:
            outs.append(acc)
    return outs


def _lane_low_half():
    return lax.broadcasted_iota(jnp.int32, (1, LANES), 1) < (LANES // 2)


def _values_and_ones(v_tiles):
    ones = jnp.ones((KEY_TILE, LANES), BF16)
    return [jnp.concatenate([v, ones], axis=1) for v in v_tiles]


def _normalise_pair(acc_lo, acc_hi, low):
    return jnp.where(low, acc_lo[:, :LANES] / acc_lo[:, LANES:], acc_hi[:, :LANES] / acc_hi[:, LANES:])


def _split_lane_halves(q, low):
    zero = jnp.zeros_like(q)
    return jnp.where(low, q, zero), jnp.where(low, zero, q)


def _tiles(ref, n_tiles, j, bi=0):
    return [ref[bi, c * KEY_TILE:(c + 1) * KEY_TILE, j:j + LANES].astype(BF16) for c in range(n_tiles)]


def _ctx_attn_kernel(q_ref, k_ref, v_ref, g_ref, o_ref, s_ref):
    low = _lane_low_half()
    nb, tq, w = q_ref.shape
    n_tiles = k_ref.shape[1] // KEY_TILE
    tiles = [(bi, j, r) for bi in range(nb) for j in range(0, w, LANES) for r in range(0, tq, ROW_BLK)]
    q_blocks, kv_blocks, kv = [], [], {}
    for bi, j, r in tiles:
        if (bi, j) not in kv:
            kv[bi, j] = (_tiles(k_ref, n_tiles, j, bi), _values_and_ones(_tiles(v_ref, n_tiles, j, bi)))
        q_blocks += list(_split_lane_halves(q_ref[bi, r:r + ROW_BLK, j:j + LANES], low))
        kv_blocks += [kv[bi, j]] * 2
    res = _attend(q_blocks, kv_blocks, s_ref)
    for i, (bi, j, r) in enumerate(tiles):
        o = _normalise_pair(res[2 * i], res[2 * i + 1], low)
        gate = _silu(g_ref[bi, r:r + ROW_BLK, j:j + LANES].astype(F32))
        o_ref[bi, r:r + ROW_BLK, j:j + LANES] = (o * gate).astype(o_ref.dtype)


def _ctx_attn(q, k, v, g):
    b, t, w = q.shape
    nb = CTX_BATCH_PER_STEP
    spec = pl.BlockSpec((nb, t, w), lambda i: (i, 0, 0))
    return pl.pallas_call(
        _ctx_attn_kernel,
        grid=(b // nb,),
        in_specs=[spec] * 4,
        out_specs=spec,
        out_shape=jax.ShapeDtypeStruct((b, t, w), BF16),
        scratch_shapes=[pltpu.VMEM((SCORE_SLOTS * ROW_BLK, k.shape[1]), F32)],
        compiler_params=_params("parallel"),
        name="ctx_attn",
    )(q, k, v, g)


def _na_pair_tiles(rpb_ref, head, pt_ref):
    right = lax.broadcasted_iota(jnp.int32, (GRID_W, LANES), 1) >= GRID_W
    zero_row = jnp.zeros((1, LANES), F32)

    def toeplitz(row, lane0):
        shift = (lane0 - (NA_COLS - 1)) % LANES
        return pltpu.roll(jnp.broadcast_to(row, (GRID_W, LANES)), shift, axis=1, stride=1, stride_axis=0)

    for p in range(NA_PAIR_TILES):
        left_row = rpb_ref[head, p - 1:p, :] if p >= 1 else zero_row
        right_row = rpb_ref[head, p:p + 1, :]
        pt_ref[head, p] = jnp.where(right, toeplitz(right_row, GRID_W), toeplitz(left_row, 0)) * LOG2E


def _na_row_start(r, n_rows):
    kr = min(NA_ROWS_MAX, n_rows)
    return min(max(r - kr // 2, 0), n_rows - kr)


def _na_window(grp, n_rows):
    kr = min(NA_ROWS_MAX, n_rows)
    starts = [_na_row_start(grp * NA_Q_ROWS + i, n_rows) for i in range(NA_Q_ROWS)]
    rows_per_tile = KEY_TILE // GRID_W
    need = -(-(max(starts) + kr - min(starts)) // rows_per_tile) * rows_per_tile
    return min(min(starts), n_rows - need), need


def _na_kernel(q_ref, k_ref, v_ref, ck_ref, cv_ref, rpb_ref, g_ref, o_ref, pt_ref, badd_ref, s_ref, *, n_rows):
    kr = min(NA_ROWS_MAX, n_rows)
    n_grp = n_rows // NA_Q_ROWS
    n_pairs = q_ref.shape[2] // LANES
    low = _lane_low_half()

    @pl.when(pl.program_id(1) == 0)
    def _build_bias():
        for head in range(2 * n_pairs):
            _na_pair_tiles(rpb_ref, head, pt_ref)
        qcol = lax.broadcasted_iota(jnp.int32, (GRID_W, LANES), 0)
        lane = lax.broadcasted_iota(jnp.int32, (GRID_W, LANES), 1)
        right = lane >= GRID_W
        kcol = lane & (GRID_W - 1)
        col_start = jnp.clip(qcol - NA_COLS // 2, 0, GRID_W - NA_COLS)
        col_ok = (kcol >= col_start) & (kcol < col_start + NA_COLS)
        for grp in range(n_grp):
            krow0, key_rows = _na_window(grp, n_rows)
            for i in range(NA_Q_ROWS):
                r = grp * NA_Q_ROWS + i
                row_start = _na_row_start(r, n_rows)
                for m in range(key_rows // 2):
                    key_row = krow0 + 2 * m
                    ok_l = row_start <= key_row < row_start + kr
                    ok_r = row_start <= key_row + 1 < row_start + kr
                    rows = slice(i * GRID_W, (i + 1) * GRID_W)
                    cols = slice(m * LANES, (m + 1) * LANES)
                    if not (ok_l or ok_r):
                        for head in range(2 * n_pairs):
                            badd_ref[head * n_grp + grp, rows, cols] = jnp.full((GRID_W, LANES), -jnp.inf, F32)
                        continue
                    ok = col_ok
                    if not ok_r:
                        ok = ok & jnp.logical_not(right)
                    if not ok_l:
                        ok = ok & right
                    p = key_row - r + NA_ROWS_MAX
                    for head in range(2 * n_pairs):
                        badd_ref[head * n_grp + grp, rows, cols] = jnp.where(ok, pt_ref[head, p], -jnp.inf)

    n_ctx = ck_ref.shape[1] // KEY_TILE
    q_blocks, kv_blocks, blocks = [], [], []
    for pair in range(n_pairs):
        lanes = slice(pair * LANES, (pair + 1) * LANES)
        ctx_k = _tiles(ck_ref, n_ctx, pair * LANES)
        ctx_v = _tiles(cv_ref, n_ctx, pair * LANES)
        for grp in range(n_grp):
            krow0, key_rows = _na_window(grp, n_rows)
            k0 = krow0 * GRID_W
            win = [slice(k0 + c * KEY_TILE, k0 + (c + 1) * KEY_TILE) for c in range(key_rows * GRID_W // KEY_TILE)]
            kv = ([k_ref[0, w, lanes] for w in win] + ctx_k,
                  _values_and_ones([v_ref[0, w, lanes] for w in win] + ctx_v))
            q_blocks += list(_split_lane_halves(q_ref[0, grp * ROW_BLK:(grp + 1) * ROW_BLK, lanes], low))
            kv_blocks += [kv, kv]
            blocks += [(2 * pair, grp, len(win)), (2 * pair + 1, grp, len(win))]

    def masked_bias(rb, c, s):
        head, grp, n_win = blocks[rb]
        if c >= n_win:
            return None
        cols = slice(c * KEY_TILE, (c + 1) * KEY_TILE)
        return s + badd_ref[head * n_grp + grp, :, cols]

    res = _attend(q_blocks, kv_blocks, s_ref, masked_bias)
    for pair in range(n_pairs):
        lanes = slice(pair * LANES, (pair + 1) * LANES)
        for grp in range(n_grp):
            rb = 2 * (pair * n_grp + grp)
            o = _normalise_pair(res[rb], res[rb + 1], low)
            rows = slice(grp * ROW_BLK, (grp + 1) * ROW_BLK)
            o_ref[0, rows, lanes] = (o * _silu(g_ref[0, rows, lanes].astype(F32))).astype(o_ref.dtype)


def _na_attn(q, k, v, ck, cv, rpb, g):
    _, rpb_rows, rpb_cols = rpb.shape
    assert rpb_rows == RPB_ROWS and rpb_cols == RPB_COLS
    rpb = jnp.pad(rpb, ((0, 0), (0, NA_PAIR_TILES - rpb_rows), (0, LANES - rpb_cols)))
    b, t, w = q.shape
    c = ck.shape[1]
    n_rows = t // GRID_W
    assert NA_Q_ROWS * GRID_W == ROW_BLK and n_rows % NA_Q_ROWS == 0 and n_rows >= NA_K_ROWS
    n_grp = n_rows // NA_Q_ROWS
    n_win = NA_K_ROWS * GRID_W
    wb = NA_PAIRS_PER_STEP * LANES
    hb = 2 * NA_PAIRS_PER_STEP
    tspec = pl.BlockSpec((1, t, wb), lambda hp, bi: (bi, 0, hp))
    cspec = pl.BlockSpec((1, c, wb), lambda hp, bi: (bi, 0, hp))
    ptspec = pl.BlockSpec((hb, NA_PAIR_TILES, LANES), lambda hp, bi: (hp, 0, 0))
    return pl.pallas_call(
        functools.partial(_na_kernel, n_rows=n_rows),
        grid=(w // wb, b),
        in_specs=[tspec, tspec, tspec, cspec, cspec, ptspec, tspec],
        out_specs=tspec,
        out_shape=jax.ShapeDtypeStruct((b, t, w), BF16),
        scratch_shapes=[pltpu.VMEM((hb, NA_PAIR_TILES, GRID_W, LANES), F32),
                        pltpu.VMEM((hb * n_grp, ROW_BLK, n_win), F32),
                        pltpu.VMEM((SCORE_SLOTS * ROW_BLK, n_win + c), F32)],
        compiler_params=_params("parallel", "arbitrary"),
        name="na_attn",
    )(q, k, v, ck, cv, rpb, g)


def _softplus(x):
    return jnp.maximum(x, 0.0) + jnp.log(1.0 + jnp.exp(-jnp.abs(x)))


def _lru_kernel(x_ref, gb_ref, h0f_ref, h0b_ref, cw_ref, cb_ref, wg_ref, lam_ref,
                o_ref, ff_ref, fb_ref, xt_s, af_s, uf_s, ab_s, ub_s, hf_s, hb_s):
    nb, t, _ = x_ref.shape
    pad = CONV_PAD_L * nb
    cw = 0.5 * cw_ref[...]
    cb = 0.5 * cb_ref[...]
    wg = wg_ref[0]
    kk = (-0.5 * LRU_C * LOG2E) * _softplus(-lam_ref[...])

    xt_s[0:pad, :] = jnp.zeros((pad, LANES), F32)
    xt_s[pad + t * nb:pad + (t + CONV_PAD_R) * nb, :] = jnp.zeros((CONV_PAD_R * nb, LANES), F32)

    def to_time_major(b, carry):
        xt_s[pl.ds(pad + b, t, stride=nb), :] = x_ref[b]
        return carry

    lax.fori_loop(0, nb, to_time_major, 0)

    rows_per_chunk = LRU_CHUNK * nb
    lane = lax.broadcasted_iota(jnp.int32, (rows_per_chunk, LANES), 1)
    bias_cols = jnp.where(lane < 2, 1.0, 0.0).astype(BF16)

    def gates(c, carry):
        r0 = pl.multiple_of(c * rows_per_chunk, rows_per_chunk)

        def tap(i):
            return xt_s[pl.ds(r0 + i * nb, rows_per_chunk), :]

        xh = cw[0:1] * tap(0) + cw[1:2] * tap(1) + cw[2:3] * tap(2) + cw[3:4] * tap(3) + cb
        th = jnp.tanh(_dot(jnp.concatenate([xh.astype(BF16), bias_cols], axis=1), wg))
        rows = pl.ds(r0, rows_per_chunk)
        for d, (a_s, u_s) in enumerate(((af_s, uf_s), (ab_s, ub_s))):
            t_a = th[:, (2 * d) * LANES:(2 * d + 1) * LANES]
            t_x = th[:, (2 * d + 1) * LANES:(2 * d + 2) * LANES]
            a = jnp.exp2(kk[d:d + 1] * t_a + kk[d:d + 1])
            y = 1.0 - a * a
            sq = y * lax.rsqrt(jnp.maximum(y, 1e-30))
            a_s[rows, :] = a
            u_s[rows, :] = sq * (xh * t_x + xh)
        return carry

    lax.fori_loop(0, t // LRU_CHUNK, gates, 0)

    ts = t // LRU_SEGMENTS
    refs = ((af_s, uf_s), (ab_s, ub_s))

    slab = nb * LRU_UNROLL

    def slab_rows(d, j, i):
        base = (j * ts + i * LRU_UNROLL) * nb
        return pl.ds(pl.multiple_of(base if d == 0 else t * nb - slab - base, slab), slab)

    def step_of(d, block, k):
        kk_ = k if d == 0 else LRU_UNROLL - 1 - k
        return block[kk_ * nb:(kk_ + 1) * nb]

    def compose(i, carry):
        state = list(carry)
        for d in range(2):
            for j in range(LRU_SEGMENTS - 1):
                n = d * (LRU_SEGMENTS - 1) + j
                a_blk = refs[d][0][slab_rows(d, j, i), :]
                u_blk = refs[d][1][slab_rows(d, j, i), :]
                g, big_a = state[n]
                for k in range(LRU_UNROLL):
                    a = step_of(d, a_blk, k)
                    g, big_a = a * g + step_of(d, u_blk, k), a * big_a
                state[n] = (g, big_a)
        return tuple(state)

    ones = jnp.ones((nb, LANES), F32)
    h0 = (h0f_ref[...], h0b_ref[...])
    init = tuple((h0[d] if j == 0 else jnp.zeros((nb, LANES), F32), ones)
                 for d in range(2) for j in range(LRU_SEGMENTS - 1))
    ends = lax.fori_loop(0, ts // LRU_UNROLL, compose, init)
    starts = []
    for d in range(2):
        h = h0[d]
        for j in range(LRU_SEGMENTS):
            starts.append(h)
            if j < LRU_SEGMENTS - 1:
                g, big_a = ends[d * (LRU_SEGMENTS - 1) + j]
                h = g if j == 0 else g + big_a * h

    def rerun(i, carry):
        state = list(carry)
        for d, h_s in enumerate((hf_s, hb_s)):
            for j in range(LRU_SEGMENTS):
                n = d * LRU_SEGMENTS + j
                a_blk = refs[d][0][slab_rows(d, j, i), :]
                u_blk = refs[d][1][slab_rows(d, j, i), :]
                hs = []
                for k in range(LRU_UNROLL):
                    state[n] = step_of(d, a_blk, k) * state[n] + step_of(d, u_blk, k)
                    hs.append(state[n])
                h_s[slab_rows(d, j, i), :] = jnp.concatenate(hs if d == 0 else hs[::-1], axis=0)
        return tuple(state)

    final = lax.fori_loop(0, ts // LRU_UNROLL, rerun, tuple(starts))
    ff_ref[...] = final[LRU_SEGMENTS - 1]
    fb_ref[...] = final[2 * LRU_SEGMENTS - 1]

    def add_directions(c, carry):
        rows = pl.ds(pl.multiple_of(c * rows_per_chunk, rows_per_chunk), rows_per_chunk)
        hf_s[rows, :] = hf_s[rows, :] + hb_s[rows, :]
        return carry

    lax.fori_loop(0, t // LRU_CHUNK, add_directions, 0)

    def emit(b, carry):
        hs = hf_s[pl.ds(b, t, stride=nb), :]
        o_ref[b] = (hs * _silu(gb_ref[b].astype(F32))).astype(o_ref.dtype)
        return carry

    lax.fori_loop(0, nb, emit, 0)


def _lru_branch(xb, gb, h0f, h0b, conv_w, conv_b, wg, lam):
    b, t, w = xb.shape
    nb = SUBLANES
    assert t % (LRU_SEGMENTS * LRU_UNROLL) == 0 and t % LRU_CHUNK == 0
    assert conv_w.shape[0] == CONV_PAD_L + 1 + CONV_PAD_R
    xspec = pl.BlockSpec((nb, t, LANES), lambda i, c: (i, 0, c))
    hspec = pl.BlockSpec((nb, LANES), lambda i, c: (i, c))
    scratch = [pltpu.VMEM(((t + CONV_PAD_L + CONV_PAD_R) * nb, LANES), F32)]
    scratch += [pltpu.VMEM((t * nb, LANES), F32) for _ in range(6)]
    return pl.pallas_call(
        _lru_kernel,
        grid=(b // nb, w // LANES),
        in_specs=[xspec, xspec, hspec, hspec,
                  pl.BlockSpec((conv_w.shape[0], LANES), lambda i, c: (0, c)),
                  pl.BlockSpec((1, LANES), lambda i, c: (0, c)),
                  pl.BlockSpec((1, 2 * LANES, 4 * LANES), lambda i, c: (c, 0, 0)),
                  pl.BlockSpec((2, LANES), lambda i, c: (0, c))],
        out_specs=[xspec, hspec, hspec],
        out_shape=[jax.ShapeDtypeStruct((b, t, w), BF16),
                   jax.ShapeDtypeStruct((b, w), F32),
                   jax.ShapeDtypeStruct((b, w), F32)],
        scratch_shapes=scratch,
        compiler_params=_params("parallel", "parallel"),
        name="lru_branch",
    )(xb, gb, h0f, h0b, conv_w, conv_b.reshape(1, w), wg, lam)


def _lru_gate_weights(wa, ba, wx, bx):
    def tiles(w):
        w2 = w.reshape(LRU_BLOCKS // 2, 2, LRU_BLOCK, LRU_BLOCK)
        z = jnp.zeros_like(w2[:, 0])
        return jnp.concatenate([jnp.concatenate([w2[:, 0], z], axis=2),
                                jnp.concatenate([z, w2[:, 1]], axis=2)], axis=1)
    wg = jnp.concatenate([tiles(wa[0]), tiles(wx[0]), tiles(wa[1]), tiles(wx[1])], axis=2)
    n_tiles = LRU_WIDTH // LANES
    bg = jnp.stack([ba[0], bx[0], ba[1], bx[1]], axis=0).reshape(4, n_tiles, LANES)
    bg = 0.5 * jnp.transpose(bg, (1, 0, 2)).reshape(n_tiles, 1, 4 * LANES)
    hi = bg.astype(BF16)
    lo = (bg - hi.astype(F32)).astype(BF16)
    rest = jnp.zeros((n_tiles, LANES - 2, 4 * LANES), BF16)
    return jnp.concatenate([wg.astype(BF16), hi, lo, rest], axis=1)


def _packed_tiles(ref, n_heads, head, bi=0):
    n_tiles = ref.shape[1] // (n_heads * KEY_TILE)
    return [ref[bi, pl.ds(c * KEY_TILE * n_heads + head, KEY_TILE, stride=n_heads), :].astype(BF16)
            for c in range(n_tiles)]


def _diff_kernel(*refs, packed, n_heads, lam_init):
    n_kv = len(packed)
    q_ref = refs[0]
    kv_refs = refs[1:1 + 2 * n_kv]
    g_ref, lamp_ref, subg_ref, o_ref, s_ref = refs[1 + 2 * n_kv:]
    low = _lane_low_half()
    tq = q_ref.shape[1]
    first_head = pl.program_id(1) * (q_ref.shape[2] // LANES)
    lp = lamp_ref[...]
    lam = (jnp.exp(jnp.sum(lp[0:1] * lp[1:2], axis=-1, keepdims=True))
           - jnp.exp(jnp.sum(lp[2:3] * lp[3:4], axis=-1, keepdims=True)) + lam_init)
    sub_gain = subg_ref[...] * (1.0 - lam_init)
    nb = q_ref.shape[0]
    tiles = [(bi, j) for bi in range(nb) for j in range(0, q_ref.shape[2], LANES)]
    q_blocks, kv_blocks = [], []
    for bi, j in tiles:
        k_tiles, v_tiles = [], []
        for i in range(n_kv):
            if packed[i]:
                head = first_head + j // LANES
                k_tiles += _packed_tiles(kv_refs[2 * i], n_heads, head, bi)
                v_tiles += _packed_tiles(kv_refs[2 * i + 1], n_heads, head, bi)
            else:
                n_tiles = kv_refs[2 * i].shape[1] // KEY_TILE
                k_tiles += _tiles(kv_refs[2 * i], n_tiles, j, bi)
                v_tiles += _tiles(kv_refs[2 * i + 1], n_tiles, j, bi)
        v_tiles = _values_and_ones(v_tiles)
        for r in range(0, tq, ROW_BLK):
            q_blocks += list(_split_lane_halves(q_ref[bi, r:r + ROW_BLK, j:j + LANES], low))
            kv_blocks += [(k_tiles, v_tiles)] * 2
    res = _attend(q_blocks, kv_blocks, s_ref)
    i = 0
    for bi, j in tiles:
        for r in range(0, tq, ROW_BLK):
            acc1, acc2 = res[i], res[i + 1]
            i += 2
            o = acc1[:, :LANES] / acc1[:, LANES:] - acc2[:, :LANES] * (lam / acc2[:, LANES:])
            o = o * lax.rsqrt(jnp.mean(o * o, axis=-1, keepdims=True) + EPS) * sub_gain
            gate = _silu(g_ref[bi, r:r + ROW_BLK, j:j + LANES].astype(F32))
            o_ref[bi, r:r + ROW_BLK, j:j + LANES] = (o * gate).astype(o_ref.dtype)


def _diff_attn(q, kv_list, g, lam_params, sub_g, lam_init, tq, heads_per_step, batch_per_step=1):
    b, t, w = q.shape
    wb = heads_per_step * LANES
    nb = batch_per_step
    qspec = pl.BlockSpec((nb, tq, wb), lambda bi, h, qi: (bi, qi, h))
    n_heads = w // LANES
    in_specs = [qspec]
    args = [q]
    n_keys = 0
    for k, v, packed in kv_list:
        if packed:
            spec = pl.BlockSpec((nb, k.shape[1], LANES), lambda bi, h, qi: (bi, 0, 0))
            n_keys += k.shape[1] // n_heads
        else:
            spec = pl.BlockSpec((nb, k.shape[1], wb), lambda bi, h, qi: (bi, 0, h))
            n_keys += k.shape[1]
        in_specs += [spec, spec]
        args += [k, v]
    in_specs += [qspec,
                 pl.BlockSpec(lam_params.shape, lambda bi, h, qi: (0, 0)),
                 pl.BlockSpec((1, LANES), lambda bi, h, qi: (0, 0))]
    args += [g, lam_params, sub_g.reshape(1, LANES)]
    return pl.pallas_call(
        functools.partial(_diff_kernel, packed=tuple(p for _, _, p in kv_list), n_heads=n_heads,
                          lam_init=lam_init),
        grid=(b // nb, w // wb, t // tq),
        in_specs=in_specs,
        out_specs=qspec,
        out_shape=jax.ShapeDtypeStruct((b, t, w), BF16),
        scratch_shapes=[pltpu.VMEM((SCORE_SLOTS * ROW_BLK, n_keys), F32)],
        compiler_params=_params("parallel", "parallel", "parallel"),
        name="diff_attn",
    )(*args)


def _rope_tables(t_len):
    m = DIFF_HEAD_DIM // 4
    t = jnp.arange(t_len)
    rows = (t // GRID_W).astype(F32)
    cols = (t % GRID_W).astype(F32)
    inv = ROPE_BASE ** (-jnp.arange(m, dtype=F32) / m)
    ang_r = rows[:, None] * inv[None, :]
    ang_c = cols[:, None] * inv[None, :]
    cos = jnp.concatenate([jnp.cos(ang_r)] * 2 + [jnp.cos(ang_c)] * 2, axis=1)
    sin = jnp.concatenate([-jnp.sin(ang_r), jnp.sin(ang_r), -jnp.sin(ang_c), jnp.sin(ang_c)], axis=1)
    reps = LANES // DIFF_HEAD_DIM
    return jnp.tile(cos, (1, reps)), jnp.tile(sin, (1, reps))


def kernel(x_prompt, x_sample, c, cache_na_k, cache_na_v, state_lru, cache_diff_k, cache_diff_v, c_ctx,
           e_norm, e_ada_w, e_ada_b, e_w_in, e_rpb, e_conv_w, e_conv_b, e_lru_wa, e_lru_ba, e_lru_wx, e_lru_bx,
           e_lru_lam, e_w_out, o_norm, o_ada_w, o_ada_b, o_w_in, o_lq1, o_lk1, o_lq2, o_lk2, o_sub_g, o_w_out,
           final_norm):
    bp, tp, d = x_prompt.shape
    bs, ts, _ = x_sample.shape
    past = cache_na_k.shape[2]
    pad_rows = 2 * SUBLANES - bs - 1
    cc = jnp.concatenate([c, c_ctx[None, :], jnp.zeros((pad_rows, d), F32)], axis=0)
    nw, lw, dw = NA_WIDTH, LRU_WIDTH, DIFF_WIDTH
    rope_tables = _rope_tables(ts) if DEPTH > 1 else None

    def layer_params(i):
        j = i // 2
        if i % 2 == 0:
            return e_ada_w[j], e_ada_b[j], e_norm[j], e_w_in[j], e_w_out[j]
        return o_ada_w[j], o_ada_b[j], o_norm[j], o_w_in[j], o_w_out[j]

    mods = []
    for i in range(DEPTH):
        ada_w, ada_b = layer_params(i)[:2]
        m = _modulation(cc, ada_w, ada_b)
        mods.append((m[:bs].reshape(bs, 1, 3 * d), m[bs:bs + 1].reshape(1, 1, 3 * d)))

    def projection(i, latent):
        norm_g, w_in = layer_params(i)[2:4]
        mod = mods[i][0 if latent else 1]
        kv_dtype = BF16 if latent else F32
        if i % 2 == 0:
            qmul = NA_HEAD_DIM ** -0.5 * LOG2E
            outs = [(nw, BF16, False, qmul, False), (nw, kv_dtype, False, None, False),
                    (nw, kv_dtype, False, None, False), (nw, BF16, False, None, False),
                    (lw, F32, False, None, False), (lw, BF16, False, None, False)]
            return mod, norm_g, w_in, outs, None
        qmul = DIFF_HEAD_DIM ** -0.5 * LOG2E
        outs = [(dw, BF16, latent, qmul, False), (dw, kv_dtype, latent, None, not latent),
                (dw, kv_dtype, False, None, not latent), (dw, BF16, False, None, False)]
        return mod, norm_g, w_in, outs, rope_tables if latent else None

    xp, xs = x_prompt, x_sample
    _, proj_p = _stage(xp, nxt=projection(0, False), tm=512)
    _, proj_s = _stage(xs, nxt=projection(0, True), tm=512)
    na_k, na_v, lru_s, df_k, df_v = [], [], [], [], []
    for i in range(DEPTH):
        j = i // 2
        w_out = layer_params(i)[4]
        if i % 2 == 0:
            wg = _lru_gate_weights(e_lru_wa[j], e_lru_ba[j], e_lru_wx[j], e_lru_bx[j])

            q, k, v, ga, xb, gb = proj_p
            oa = _ctx_attn(q, k, v, ga)
            zeros = jnp.zeros((bp, lw), F32)
            ob, ff, fb = _lru_branch(xb, gb, zeros, zeros, e_conv_w[j], e_conv_b[j], wg, e_lru_lam[j])
            mixed_p = [oa, ob]
            na_k.append(k.reshape(bp, tp, NA_HEADS, NA_HEAD_DIM))
            na_v.append(v.reshape(bp, tp, NA_HEADS, NA_HEAD_DIM))
            lru_s.append(jnp.stack([ff, fb], axis=1))

            q, k, v, ga, xb, gb = proj_s
            ck = cache_na_k[:, j].reshape(bs, past, nw)
            cv = cache_na_v[:, j].reshape(bs, past, nw)
            oa = _na_attn(q, k, v, ck, cv, e_rpb[j], ga)
            ob, _, _ = _lru_branch(xb, gb, state_lru[:, j, 0], state_lru[:, j, 1],
                                   e_conv_w[j], e_conv_b[j], wg, e_lru_lam[j])
            mixed_s = [oa, ob]
        else:
            lam_init = 0.8 - 0.6 * math.exp(-0.3 * i)
            lam_params = jnp.stack([o_lq1[j], o_lk1[j], o_lq2[j], o_lk2[j]], axis=0)

            q, k, v, g = proj_p
            mixed_p = [_diff_attn(q, [(k, v, True)], g, lam_params, o_sub_g[j], lam_init, tq=tp,
                                  heads_per_step=DIFF_HEADS, batch_per_step=CTX_BATCH_PER_STEP)]
            df_k.append(k.reshape(bp, tp, DIFF_HEADS, 2 * DIFF_HEAD_DIM))
            df_v.append(v.reshape(bp, tp, DIFF_HEADS, 2 * DIFF_HEAD_DIM))

            q, k, v, g = proj_s
            ck = cache_diff_k[:, j].reshape(bs, past * DIFF_HEADS, 2 * DIFF_HEAD_DIM)
            cv = cache_diff_v[:, j].reshape(bs, past * DIFF_HEADS, 2 * DIFF_HEAD_DIM)
            mixed_s = [_diff_attn(q, [(ck, cv, True), (k, v, False)], g, lam_params, o_sub_g[j], lam_init, tq=ts,
                                  heads_per_step=4)]

        if i == DEPTH - 1:
            xp = _final_stage(xp, mixed_p, w_out, mods[i][1], final_norm)
            xs = _final_stage(xs, mixed_s, w_out, mods[i][0], final_norm)
        else:
            xp, proj_p = _stage(xp, prev=(mixed_p, w_out, mods[i][1]), nxt=projection(i + 1, False), tm=512)
            xs, proj_s = _stage(xs, prev=(mixed_s, w_out, mods[i][0]), nxt=projection(i + 1, True), tm=512)

    return (xp, xs, jnp.stack(na_k, axis=1), jnp.stack(na_v, axis=1), jnp.stack(lru_s, axis=1),
            jnp.stack(df_k, axis=1), jnp.stack(df_v, axis=1))
```

```python
import functools
import math

import jax
import jax.numpy as jnp
from jax import lax
from jax.experimental import pallas as pl
from jax.experimental.pallas import tpu as pltpu

DEPTH = 2
GRID_W = 64
NA_HEADS = 8
NA_HEAD_DIM = 64
NA_WIDTH = NA_HEADS * NA_HEAD_DIM
NA_ROWS_MAX = 8
NA_COLS = 16
RPB_ROWS = 2 * NA_ROWS_MAX - 1
RPB_COLS = 2 * NA_COLS - 1
LRU_WIDTH = 512
LRU_BLOCKS = 8
LRU_BLOCK = LRU_WIDTH // LRU_BLOCKS
LRU_C = 8.0
DIFF_HEADS = 8
DIFF_HEAD_DIM = 64
DIFF_WIDTH = DIFF_HEADS * 2 * DIFF_HEAD_DIM
ROPE_BASE = 10000.0
EPS = 1e-6
LOG2E = math.log2(math.e)

LANES = 128
SUBLANES = 8
MXU_DIM = 256
VMEM_LIMIT = 56 * 1024 * 1024

CAST_COLS = 512
IN_BUFS = 3
LRU_SEGMENTS = 4
LRU_UNROLL = 8
LRU_CHUNK = 256
CONV_PAD_L = 2
CONV_PAD_R = 1
ROW_BLK = MXU_DIM
KEY_TILE = MXU_DIM
SCORE_SLOTS = 4

NA_Q_ROWS = 4
NA_K_ROWS = 12
NA_PAIR_TILES = 16
NA_PAIRS_PER_STEP = 4
CTX_BATCH_PER_STEP = 4

F32 = jnp.float32
BF16 = jnp.bfloat16


def _params(*semantics):
    return pltpu.CompilerParams(dimension_semantics=semantics, vmem_limit_bytes=VMEM_LIMIT)


def _silu(x):
    h = 0.5 * x
    return h * jnp.tanh(h) + h


def _dot(a, b):
    return jnp.dot(a, b, preferred_element_type=F32)


def _dot_nt(a, b):
    return lax.dot_general(a, b, (((1,), (1,)), ((), ())), preferred_element_type=F32)


def _mod_kernel(c_ref, w_ref, b_ref, o_ref):
    s = _silu(c_ref[...])
    o_ref[...] = _dot(s.astype(BF16), w_ref[...].astype(BF16)) + b_ref[...]


def _modulation(cc, w, b):
    rows, d = cc.shape
    n = w.shape[1]
    tn = 1024
    return pl.pallas_call(
        _mod_kernel,
        grid=(n // tn,),
        in_specs=[pl.BlockSpec((rows, d), lambda j: (0, 0)),
                  pl.BlockSpec((d, tn), lambda j: (0, j)),
                  pl.BlockSpec((1, tn), lambda j: (0, j))],
        out_specs=pl.BlockSpec((rows, tn), lambda j: (0, j)),
        out_shape=jax.ShapeDtypeStruct((rows, n), F32),
        compiler_params=_params("arbitrary"),
        name="modulation",
    )(cc, w, b.reshape(1, n))


def _rope_block(x, cos, sin_signed, low16):
    partner = jnp.where(low16, pltpu.roll(x, LANES - 16, axis=1), pltpu.roll(x, 16, axis=1))
    return x * cos + partner * sin_signed


def _cast_weights_once(w_ref, wbf_ref):
    @pl.when((pl.program_id(0) == 0) & (pl.program_id(1) == 0))
    def _():
        for off in range(0, w_ref.shape[1], CAST_COLS):
            wbf_ref[:, off:off + CAST_COLS] = w_ref[:, off:off + CAST_COLS].astype(BF16)


def _weight_chunk_copy(w_hbm, stage_ref, sem, c):
    cols = slice(c * CAST_COLS, (c + 1) * CAST_COLS)
    return pltpu.make_async_copy(w_hbm.at[:, cols], stage_ref.at[:, cols], sem.at[c])


def _land_weight_chunks(w_hbm, stage_ref, sem, wbf_ref, off, width):
    for c in range(off // CAST_COLS, (off + width) // CAST_COLS):
        _weight_chunk_copy(w_hbm, stage_ref, sem, c).wait()
        cols = slice(c * CAST_COLS, (c + 1) * CAST_COLS)
        wbf_ref[:, cols] = stage_ref[:, cols].astype(BF16)


def _norm_project(x, mod_ref, g_ref, wbf_ref, rope_refs, out_refs, outs, land=None):
    tm, d = x.shape
    y = x * lax.rsqrt(jnp.mean(x * x, axis=-1, keepdims=True) + EPS) * g_ref[...]
    mod = mod_ref[0]
    h = (y * (1.0 + mod[:, d:2 * d]) + mod[:, :d]).astype(BF16)
    if rope_refs:
        cos = rope_refs[0][...]
        sin = rope_refs[1][...]
        lane = lax.broadcasted_iota(jnp.int32, (1, LANES), 1)
        low16 = (lane & 16) == 0
    off = 0
    for o_ref, (width, roped, mul, packed) in zip(out_refs, outs):
        if land is not None:
            land(off, width)
        acc = _dot(h, wbf_ref[:, off:off + width])
        if roped:
            acc = jnp.concatenate(
                [_rope_block(acc[:, j:j + LANES], cos, sin, low16) for j in range(0, width, LANES)], axis=1)
        if mul is not None:
            acc = acc * mul
        if packed:
            n_heads = width // LANES
            t = o_ref.shape[1] // n_heads
            for bi in range(o_ref.shape[0]):
                for hh in range(n_heads):
                    o_ref[bi, pl.ds(hh, t, stride=n_heads), :] = (
                        acc[bi * t:(bi + 1) * t, hh * LANES:(hh + 1) * LANES].astype(o_ref.dtype))
        else:
            o_ref[...] = acc.astype(o_ref.dtype).reshape(o_ref.shape)
        off += width


def _token_rows(ref):
    return ref[...].reshape(-1, ref.shape[-1])


def _gated_residual(o_refs, wbf_ref, x_ref, mod_ref):
    d = x_ref.shape[-1]
    acc, row = None, 0
    for o_ref in o_refs:
        k = o_ref.shape[-1]
        part = _dot(_token_rows(o_ref), wbf_ref[row:row + k, :])
        acc = part if acc is None else acc + part
        row += k
    return _token_rows(x_ref) + mod_ref[0][:, 2 * d:] * acc


def _stage_kernel(*refs, n_o, outs, rope):
    refs = list(refs)
    take = lambda n: [refs.pop(0) for _ in range(n)]
    if n_o:
        o_refs = take(n_o)
        wout_ref, x_ref, mod_prev_ref = take(3)
    else:
        x_ref, = take(1)
    mod_ref, g_ref, win_ref = take(3)
    rope_refs = take(2) if rope else []
    xout_ref = take(1)[0] if n_o else None
    out_refs = take(len(outs))
    wout_bf = take(1)[0] if n_o else None
    if n_o:
        _cast_weights_once(wout_ref, wout_bf)

    def step(land):
        if n_o:
            x = _gated_residual(o_refs, wout_bf, x_ref, mod_prev_ref)
            xout_ref[...] = x.reshape(xout_ref.shape)
        else:
            x = _token_rows(x_ref)
        _norm_project(x, mod_ref, g_ref, win_bf, rope_refs, out_refs, outs, land)

    stage_ref, win_bf, sem = take(3)
    first = (pl.program_id(0) == 0) & (pl.program_id(1) == 0)

    @pl.when(first)
    def _():
        for c in range(win_ref.shape[1] // CAST_COLS):
            _weight_chunk_copy(win_ref, stage_ref, sem, c).start()
        step(functools.partial(_land_weight_chunks, win_ref, stage_ref, sem, win_bf))

    @pl.when(jnp.logical_not(first))
    def _():
        step(None)


def _stage(x, nxt, prev=None, tm=512):
    b, t, d = x.shape
    shared_mod = all(m.shape[0] == 1 for m in ([prev[2]] if prev else []) + [nxt[0]])
    bb = max(1, tm // t) if shared_mod else 1
    tm = min(tm, t)
    tile = lambda width: pl.BlockSpec((bb, tm, width), lambda i, j: (i, j, 0))
    mod_spec = lambda mod: pl.BlockSpec(
        (1, 1, 3 * d), (lambda i, j: (i, 0, 0)) if mod.shape[0] > 1 else (lambda i, j: (0, 0, 0)))
    resident = lambda w: pl.BlockSpec(w.shape, lambda i, j: (0, 0), pipeline_mode=pl.Buffered(1))
    row_spec = pl.BlockSpec((1, d), lambda i, j: (0, 0))
    in_specs, args, out_specs, out_shape, scratch = [], [], [], [], []
    n_o = 0
    if prev is not None:
        o_list, w_out, mod_prev = prev
        n_o = len(o_list)
        in_specs += [tile(o.shape[-1]) for o in o_list] + [resident(w_out), tile(d), mod_spec(mod_prev)]
        args += list(o_list) + [w_out, x, mod_prev]
        out_specs.append(tile(d))
        out_shape.append(jax.ShapeDtypeStruct((b, t, d), F32))
        scratch.append(pltpu.VMEM(w_out.shape, BF16))
    else:
        in_specs.append(tile(d))
        args.append(x)
    mod_next, norm_g, w_in, out_list, rope_tables = nxt
    assert all(wd % CAST_COLS == 0 for wd, _, _, _, _ in out_list)
    in_specs += [mod_spec(mod_next), row_spec, pl.BlockSpec(memory_space=pl.ANY)]
    args += [mod_next, norm_g.reshape(1, d), w_in]
    rope = rope_tables is not None
    if rope:
        in_specs += [pl.BlockSpec((tm, LANES), lambda i, j: (j, 0))] * 2
        args += list(rope_tables)
    for width, dt, _, _, packed in out_list:
        if packed:
            heads = width // LANES
            out_specs.append(pl.BlockSpec((bb, tm * heads, LANES), lambda i, j: (i, j, 0)))
            out_shape.append(jax.ShapeDtypeStruct((b, t * heads, LANES), dt))
        else:
            out_specs.append(tile(width))
            out_shape.append(jax.ShapeDtypeStruct((b, t, width), dt))
    outs = tuple((w_, r, m, p) for w_, _, r, m, p in out_list)
    scratch += [pltpu.VMEM(w_in.shape, F32), pltpu.VMEM(w_in.shape, BF16),
                pltpu.SemaphoreType.DMA((w_in.shape[1] // CAST_COLS,))]
    res = pl.pallas_call(
        functools.partial(_stage_kernel, n_o=n_o, outs=outs, rope=rope),
        grid=(b // bb, t // tm),
        in_specs=in_specs,
        out_specs=out_specs,
        out_shape=out_shape,
        scratch_shapes=scratch,
        compiler_params=_params("arbitrary", "arbitrary"),
        name="stage",
    )(*args)
    return (res[0], list(res[1:])) if n_o else (None, list(res))


def _ring_copy(hbm_ref, buf_ref, sem, step, bb, tm, nt):
    slot = lax.rem(step, IN_BUFS)
    rows = pl.ds(lax.div(step, nt) * bb, bb)
    toks = pl.ds(lax.rem(step, nt) * tm, tm)
    return pltpu.make_async_copy(hbm_ref.at[rows, toks, :], buf_ref.at[slot], sem.at[slot])


def _final_kernel(*refs, n_o, bb, tm, nt, n_steps):
    hbm_refs = refs[:n_o + 1]
    wout_ref, mod_ref, fin_ref, out_ref, wout_bf = refs[n_o + 1:n_o + 6]
    bufs = refs[n_o + 6:2 * n_o + 7]
    sems = refs[2 * n_o + 7:]
    step = pl.program_id(0) * nt + pl.program_id(1)

    def start(s):
        for h, b, sm in zip(hbm_refs, bufs, sems):
            _ring_copy(h, b, sm, s, bb, tm, nt).start()

    @pl.when(step == 0)
    def _():
        for s in range(min(IN_BUFS - 1, n_steps)):
            start(jnp.int32(s))

    @pl.when(step + (IN_BUFS - 1) < n_steps)
    def _():
        start(step + (IN_BUFS - 1))

    for h, b, sm in zip(hbm_refs, bufs, sems):
        _ring_copy(h, b, sm, step, bb, tm, nt).wait()
    _cast_weights_once(wout_ref, wout_bf)
    slot = lax.rem(step, IN_BUFS)
    d = out_ref.shape[-1]
    acc, row = None, 0
    for b in bufs[:n_o]:
        k = b.shape[-1]
        part = _dot(b[slot].reshape(-1, k), wout_bf[row:row + k, :])
        acc = part if acc is None else acc + part
        row += k
    x = bufs[n_o][slot].reshape(-1, d) + mod_ref[0][:, 2 * d:] * acc
    y = x * lax.rsqrt(jnp.mean(x * x, axis=-1, keepdims=True) + EPS) * fin_ref[...]
    out_ref[...] = y.reshape(out_ref.shape)


def _final_stage(x, o_list, w_out, mod, final_g, tm=1024):
    b, t, d = x.shape
    bb = max(1, tm // t) if mod.shape[0] == 1 else 1
    tm = min(tm, t)
    nt = t // tm
    n_steps = (b // bb) * nt
    mod_map = (lambda i, j: (i, 0, 0)) if mod.shape[0] > 1 else (lambda i, j: (0, 0, 0))
    streamed = list(o_list) + [x]
    in_specs = [pl.BlockSpec(memory_space=pl.ANY)] * len(streamed)
    in_specs += [pl.BlockSpec(w_out.shape, lambda i, j: (0, 0), pipeline_mode=pl.Buffered(1)),
                 pl.BlockSpec((1, 1, 3 * d), mod_map),
                 pl.BlockSpec((1, d), lambda i, j: (0, 0))]
    scratch = [pltpu.VMEM(w_out.shape, BF16)]
    scratch += [pltpu.VMEM((IN_BUFS, bb, tm, a.shape[-1]), a.dtype) for a in streamed]
    scratch += [pltpu.SemaphoreType.DMA((IN_BUFS,)) for _ in streamed]
    return pl.pallas_call(
        functools.partial(_final_kernel, n_o=len(o_list), bb=bb, tm=tm, nt=nt, n_steps=n_steps),
        grid=(b // bb, nt),
        in_specs=in_specs,
        out_specs=pl.BlockSpec((bb, tm, d), lambda i, j: (i, j, 0)),
        out_shape=jax.ShapeDtypeStruct((b, t, d), F32),
        scratch_shapes=scratch,
        compiler_params=_params("arbitrary", "arbitrary"),
        name="final_stage",
    )(*streamed, w_out, mod, final_g.reshape(1, d))


def _attend(q_blocks, kv_blocks, s_ref, score_fn=None):
    n_rb = len(q_blocks)

    def tile_of(rb, c):
        slot = rb % SCORE_SLOTS
        return (slice(slot * ROW_BLK, (slot + 1) * ROW_BLK), slice(c * KEY_TILE, (c + 1) * KEY_TILE))

    def score_tile(rb, c):
        s_ref[tile_of(rb, c)] = _dot_nt(q_blocks[rb], kv_blocks[rb][0][c])

    def max_tile(rb, c, m_acc):
        s = s_ref[tile_of(rb, c)]
        adjusted = None if score_fn is None else score_fn(rb, c, s)
        if adjusted is not None:
            s = adjusted
            s_ref[tile_of(rb, c)] = s
        mt = jnp.maximum(s[:, :LANES], s[:, LANES:])
        return mt if m_acc is None else jnp.maximum(m_acc, mt)

    def n_tiles(rb):
        return len(kv_blocks[rb][0]) if 0 <= rb < n_rb else 0

    outs = []
    row_max = {}
    for step in range(n_rb + 2):
        rb_score, rb_max, rb_exp = step, step - 1, step - 2
        m_acc, acc = None, None
        for c in range(max(n_tiles(rb_score), n_tiles(rb_max), n_tiles(rb_exp))):
            if c < n_tiles(rb_score):
                score_tile(rb_score, c)
            if c < n_tiles(rb_max):
                m_acc = max_tile(rb_max, c, m_acc)
            if c < n_tiles(rb_exp):
                e = jnp.exp2(s_ref[tile_of(rb_exp, c)] - row_max[rb_exp])
                pv = _dot(e.astype(BF16), kv_blocks[rb_exp][1][c])
                acc = pv if acc is None else acc + pv
        if m_acc is not None:
            row_max[rb_max] = m_acc.max(axis=-1, keepdims=True)
        if acc is not None:
            outs.append(acc)
    return outs


def _lane_low_half():
    return lax.broadcasted_iota(jnp.int32, (1, LANES), 1) < (LANES // 2)


def _values_and_ones(v_tiles):
    ones = jnp.ones((KEY_TILE, LANES), BF16)
    return [jnp.concatenate([v, ones], axis=1) for v in v_tiles]


def _normalise_pair(acc_lo, acc_hi, low):
    return jnp.where(low, acc_lo[:, :LANES] / acc_lo[:, LANES:], acc_hi[:, :LANES] / acc_hi[:, LANES:])


def _split_lane_halves(q, low):
    zero = jnp.zeros_like(q)
    return jnp.where(low, q, zero), jnp.where(low, zero, q)


def _tiles(ref, n_tiles, j, bi=0):
    return [ref[bi, c * KEY_TILE:(c + 1) * KEY_TILE, j:j + LANES].astype(BF16) for c in range(n_tiles)]


def _ctx_attn_kernel(q_ref, k_ref, v_ref, g_ref, o_ref, s_ref):
    low = _lane_low_half()
    nb, tq, w = q_ref.shape
    n_tiles = k_ref.shape[1] // KEY_TILE
    tiles = [(bi, j, r) for bi in range(nb) for j in range(0, w, LANES) for r in range(0, tq, ROW_BLK)]
    q_blocks, kv_blocks, kv = [], [], {}
    for bi, j, r in tiles:
        if (bi, j) not in kv:
            kv[bi, j] = (_tiles(k_ref, n_tiles, j, bi), _values_and_ones(_tiles(v_ref, n_tiles, j, bi)))
        q_blocks += list(_split_lane_halves(q_ref[bi, r:r + ROW_BLK, j:j + LANES], low))
        kv_blocks += [kv[bi, j]] * 2
    res = _attend(q_blocks, kv_blocks, s_ref)
    for i, (bi, j, r) in enumerate(tiles):
        o = _normalise_pair(res[2 * i], res[2 * i + 1], low)
        gate = _silu(g_ref[bi, r:r + ROW_BLK, j:j + LANES].astype(F32))
        o_ref[bi, r:r + ROW_BLK, j:j + LANES] = (o * gate).astype(o_ref.dtype)


def _ctx_attn(q, k, v, g):
    b, t, w = q.shape
    nb = CTX_BATCH_PER_STEP
    spec = pl.BlockSpec((nb, t, w), lambda i: (i, 0, 0))
    return pl.pallas_call(
        _ctx_attn_kernel,
        grid=(b // nb,),
        in_specs=[spec] * 4,
        out_specs=spec,
        out_shape=jax.ShapeDtypeStruct((b, t, w), BF16),
        scratch_shapes=[pltpu.VMEM((SCORE_SLOTS * ROW_BLK, k.shape[1]), F32)],
        compiler_params=_params("parallel"),
        name="ctx_attn",
    )(q, k, v, g)


def _na_pair_tiles(rpb_ref, head, pt_ref):
    right = lax.broadcasted_iota(jnp.int32, (GRID_W, LANES), 1) >= GRID_W
    zero_row = jnp.zeros((1, LANES), F32)

    def toeplitz(row, lane0):
        shift = (lane0 - (NA_COLS - 1)) % LANES
        return pltpu.roll(jnp.broadcast_to(row, (GRID_W, LANES)), shift, axis=1, stride=1, stride_axis=0)

    for p in range(NA_PAIR_TILES):
        left_row = rpb_ref[head, p - 1:p, :] if p >= 1 else zero_row
        right_row = rpb_ref[head, p:p + 1, :]
        pt_ref[head, p] = jnp.where(right, toeplitz(right_row, GRID_W), toeplitz(left_row, 0)) * LOG2E


def _na_row_start(r, n_rows):
    kr = min(NA_ROWS_MAX, n_rows)
    return min(max(r - kr // 2, 0), n_rows - kr)


def _na_window(grp, n_rows):
    kr = min(NA_ROWS_MAX, n_rows)
    starts = [_na_row_start(grp * NA_Q_ROWS + i, n_rows) for i in range(NA_Q_ROWS)]
    rows_per_tile = KEY_TILE // GRID_W
    need = -(-(max(starts) + kr - min(starts)) // rows_per_tile) * rows_per_tile
    return min(min(starts), n_rows - need), need


def _na_kernel(q_ref, k_ref, v_ref, ck_ref, cv_ref, rpb_ref, g_ref, o_ref, pt_ref, badd_ref, s_ref, *, n_rows):
    kr = min(NA_ROWS_MAX, n_rows)
    n_grp = n_rows // NA_Q_ROWS
    n_pairs = q_ref.shape[2] // LANES
    low = _lane_low_half()

    @pl.when(pl.program_id(1) == 0)
    def _build_bias():
        for head in range(2 * n_pairs):
            _na_pair_tiles(rpb_ref, head, pt_ref)
        qcol = lax.broadcasted_iota(jnp.int32, (GRID_W, LANES), 0)
        lane = lax.broadcasted_iota(jnp.int32, (GRID_W, LANES), 1)
        right = lane >= GRID_W
        kcol = lane & (GRID_W - 1)
        col_start = jnp.clip(qcol - NA_COLS // 2, 0, GRID_W - NA_COLS)
        col_ok = (kcol >= col_start) & (kcol < col_start + NA_COLS)
        for grp in range(n_grp):
            krow0, key_rows = _na_window(grp, n_rows)
            for i in range(NA_Q_ROWS):
                r = grp * NA_Q_ROWS + i
                row_start = _na_row_start(r, n_rows)
                for m in range(key_rows // 2):
                    key_row = krow0 + 2 * m
                    ok_l = row_start <= key_row < row_start + kr
                    ok_r = row_start <= key_row + 1 < row_start + kr
                    rows = slice(i * GRID_W, (i + 1) * GRID_W)
                    cols = slice(m * LANES, (m + 1) * LANES)
                    if not (ok_l or ok_r):
                        for head in range(2 * n_pairs):
                            badd_ref[head * n_grp + grp, rows, cols] = jnp.full((GRID_W, LANES), -jnp.inf, F32)
                        continue
                    ok = col_ok
                    if not ok_r:
                        ok = ok & jnp.logical_not(right)
                    if not ok_l:
                        ok = ok & right
                    p = key_row - r + NA_ROWS_MAX
                    for head in range(2 * n_pairs):
                        badd_ref[head * n_grp + grp, rows, cols] = jnp.where(ok, pt_ref[head, p], -jnp.inf)

    n_ctx = ck_ref.shape[1] // KEY_TILE
    q_blocks, kv_blocks, blocks = [], [], []
    for pair in range(n_pairs):
        lanes = slice(pair * LANES, (pair + 1) * LANES)
        ctx_k = _tiles(ck_ref, n_ctx, pair * LANES)
        ctx_v = _tiles(cv_ref, n_ctx, pair * LANES)
        for grp in range(n_grp):
            krow0, key_rows = _na_window(grp, n_rows)
            k0 = krow0 * GRID_W
            win = [slice(k0 + c * KEY_TILE, k0 + (c + 1) * KEY_TILE) for c in range(key_rows * GRID_W // KEY_TILE)]
            kv = ([k_ref[0, w, lanes] for w in win] + ctx_k,
                  _values_and_ones([v_ref[0, w, lanes] for w in win] + ctx_v))
            q_blocks += list(_split_lane_halves(q_ref[0, grp * ROW_BLK:(grp + 1) * ROW_BLK, lanes], low))
            kv_blocks += [kv, kv]
            blocks += [(2 * pair, grp, len(win)), (2 * pair + 1, grp, len(win))]

    def masked_bias(rb, c, s):
        head, grp, n_win = blocks[rb]
        if c >= n_win:
            return None
        cols = slice(c * KEY_TILE, (c + 1) * KEY_TILE)
        return s + badd_ref[head * n_grp + grp, :, cols]

    res = _attend(q_blocks, kv_blocks, s_ref, masked_bias)
    for pair in range(n_pairs):
        lanes = slice(pair * LANES, (pair + 1) * LANES)
        for grp in range(n_grp):
            rb = 2 * (pair * n_grp + grp)
            o = _normalise_pair(res[rb], res[rb + 1], low)
            rows = slice(grp * ROW_BLK, (grp + 1) * ROW_BLK)
            o_ref[0, rows, lanes] = (o * _silu(g_ref[0, rows, lanes].astype(F32))).astype(o_ref.dtype)


def _na_attn(q, k, v, ck, cv, rpb, g):
    _, rpb_rows, rpb_cols = rpb.shape
    assert rpb_rows == RPB_ROWS and rpb_cols == RPB_COLS
    rpb = jnp.pad(rpb, ((0, 0), (0, NA_PAIR_TILES - rpb_rows), (0, LANES - rpb_cols)))
    b, t, w = q.shape
    c = ck.shape[1]
    n_rows = t // GRID_W
    assert NA_Q_ROWS * GRID_W == ROW_BLK and n_rows % NA_Q_ROWS == 0 and n_rows >= NA_K_ROWS
    n_grp = n_rows // NA_Q_ROWS
    n_win = NA_K_ROWS * GRID_W
    wb = NA_PAIRS_PER_STEP * LANES
    hb = 2 * NA_PAIRS_PER_STEP
    tspec = pl.BlockSpec((1, t, wb), lambda hp, bi: (bi, 0, hp))
    cspec = pl.BlockSpec((1, c, wb), lambda hp, bi: (bi, 0, hp))
    ptspec = pl.BlockSpec((hb, NA_PAIR_TILES, LANES), lambda hp, bi: (hp, 0, 0))
    return pl.pallas_call(
        functools.partial(_na_kernel, n_rows=n_rows),
        grid=(w // wb, b),
        in_specs=[tspec, tspec, tspec, cspec, cspec, ptspec, tspec],
        out_specs=tspec,
        out_shape=jax.ShapeDtypeStruct((b, t, w), BF16),
        scratch_shapes=[pltpu.VMEM((hb, NA_PAIR_TILES, GRID_W, LANES), F32),
                        pltpu.VMEM((hb * n_grp, ROW_BLK, n_win), F32),
                        pltpu.VMEM((SCORE_SLOTS * ROW_BLK, n_win + c), F32)],
        compiler_params=_params("parallel", "arbitrary"),
        name="na_attn",
    )(q, k, v, ck, cv, rpb, g)


def _softplus(x):
    return jnp.maximum(x, 0.0) + jnp.log(1.0 + jnp.exp(-jnp.abs(x)))


def _lru_kernel(x_ref, gb_ref, h0f_ref, h0b_ref, cw_ref, cb_ref, wg_ref, lam_ref,
                o_ref, ff_ref, fb_ref, xt_s, af_s, uf_s, ab_s, ub_s, hf_s, hb_s):
    nb, t, _ = x_ref.shape
    pad = CONV_PAD_L * nb
    cw = 0.5 * cw_ref[...]
    cb = 0.5 * cb_ref[...]
    wg = wg_ref[0]
    kk = (-0.5 * LRU_C * LOG2E) * _softplus(-lam_ref[...])

    xt_s[0:pad, :] = jnp.zeros((pad, LANES), F32)
    xt_s[pad + t * nb:pad + (t + CONV_PAD_R) * nb, :] = jnp.zeros((CONV_PAD_R * nb, LANES), F32)

    def to_time_major(b, carry):
        xt_s[pl.ds(pad + b, t, stride=nb), :] = x_ref[b]
        return carry

    lax.fori_loop(0, nb, to_time_major, 0)

    rows_per_chunk = LRU_CHUNK * nb
    lane = lax.broadcasted_iota(jnp.int32, (rows_per_chunk, LANES), 1)
    bias_cols = jnp.where(lane < 2, 1.0, 0.0).astype(BF16)

    def gates(c, carry):
        r0 = pl.multiple_of(c * rows_per_chunk, rows_per_chunk)

        def tap(i):
            return xt_s[pl.ds(r0 + i * nb, rows_per_chunk), :]

        xh = cw[0:1] * tap(0) + cw[1:2] * tap(1) + cw[2:3] * tap(2) + cw[3:4] * tap(3) + cb
        th = jnp.tanh(_dot(jnp.concatenate([xh.astype(BF16), bias_cols], axis=1), wg))
        rows = pl.ds(r0, rows_per_chunk)
        for d, (a_s, u_s) in enumerate(((af_s, uf_s), (ab_s, ub_s))):
            t_a = th[:, (2 * d) * LANES:(2 * d + 1) * LANES]
            t_x = th[:, (2 * d + 1) * LANES:(2 * d + 2) * LANES]
            a = jnp.exp2(kk[d:d + 1] * t_a + kk[d:d + 1])
            y = 1.0 - a * a
            sq = y * lax.rsqrt(jnp.maximum(y, 1e-30))
            a_s[rows, :] = a
            u_s[rows, :] = sq * (xh * t_x + xh)
        return carry

    lax.fori_loop(0, t // LRU_CHUNK, gates, 0)

    ts = t // LRU_SEGMENTS
    refs = ((af_s, uf_s), (ab_s, ub_s))

    slab = nb * LRU_UNROLL

    def slab_rows(d, j, i):
        base = (j * ts + i * LRU_UNROLL) * nb
        return pl.ds(pl.multiple_of(base if d == 0 else t * nb - slab - base, slab), slab)

    def step_of(d, block, k):
        kk_ = k if d == 0 else LRU_UNROLL - 1 - k
        return block[kk_ * nb:(kk_ + 1) * nb]

    def compose(i, carry):
        state = list(carry)
        for d in range(2):
            for j in range(LRU_SEGMENTS - 1):
                n = d * (LRU_SEGMENTS - 1) + j
                a_blk = refs[d][0][slab_rows(d, j, i), :]
                u_blk = refs[d][1][slab_rows(d, j, i), :]
                g, big_a = state[n]
                for k in range(LRU_UNROLL):
                    a = step_of(d, a_blk, k)
                    g, big_a = a * g + step_of(d, u_blk, k), a * big_a
                state[n] = (g, big_a)
        return tuple(state)

    ones = jnp.ones((nb, LANES), F32)
    h0 = (h0f_ref[...], h0b_ref[...])
    init = tuple((h0[d] if j == 0 else jnp.zeros((nb, LANES), F32), ones)
                 for d in range(2) for j in range(LRU_SEGMENTS - 1))
    ends = lax.fori_loop(0, ts // LRU_UNROLL, compose, init)
    starts = []
    for d in range(2):
        h = h0[d]
        for j in range(LRU_SEGMENTS):
            starts.append(h)
            if j < LRU_SEGMENTS - 1:
                g, big_a = ends[d * (LRU_SEGMENTS - 1) + j]
                h = g if j == 0 else g + big_a * h

    def rerun(i, carry):
        state = list(carry)
        for d, h_s in enumerate((hf_s, hb_s)):
            for j in range(LRU_SEGMENTS):
                n = d * LRU_SEGMENTS + j
                a_blk = refs[d][0][slab_rows(d, j, i), :]
                u_blk = refs[d][1][slab_rows(d, j, i), :]
                hs = []
                for k in range(LRU_UNROLL):
                    state[n] = step_of(d, a_blk, k) * state[n] + step_of(d, u_blk, k)
                    hs.append(state[n])
                h_s[slab_rows(d, j, i), :] = jnp.concatenate(hs if d == 0 else hs[::-1], axis=0)
        return tuple(state)

    final = lax.fori_loop(0, ts // LRU_UNROLL, rerun, tuple(starts))
    ff_ref[...] = final[LRU_SEGMENTS - 1]
    fb_ref[...] = final[2 * LRU_SEGMENTS - 1]

    def add_directions(c, carry):
        rows = pl.ds(pl.multiple_of(c * rows_per_chunk, rows_per_chunk), rows_per_chunk)
        hf_s[rows, :] = hf_s[rows, :] + hb_s[rows, :]
        return carry

    lax.fori_loop(0, t // LRU_CHUNK, add_directions, 0)

    def emit(b, carry):
        hs = hf_s[pl.ds(b, t, stride=nb), :]
        o_ref[b] = (hs * _silu(gb_ref[b].astype(F32))).astype(o_ref.dtype)
        return carry

    lax.fori_loop(0, nb, emit, 0)


def _lru_branch(xb, gb, h0f, h0b, conv_w, conv_b, wg, lam):
    b, t, w = xb.shape
    nb = SUBLANES
    assert t % (LRU_SEGMENTS * LRU_UNROLL) == 0 and t % LRU_CHUNK == 0
    assert conv_w.shape[0] == CONV_PAD_L + 1 + CONV_PAD_R
    xspec = pl.BlockSpec((nb, t, LANES), lambda i, c: (i, 0, c))
    hspec = pl.BlockSpec((nb, LANES), lambda i, c: (i, c))
    scratch = [pltpu.VMEM(((t + CONV_PAD_L + CONV_PAD_R) * nb, LANES), F32)]
    scratch += [pltpu.VMEM((t * nb, LANES), F32) for _ in range(6)]
    return pl.pallas_call(
        _lru_kernel,
        grid=(b // nb, w // LANES),
        in_specs=[xspec, xspec, hspec, hspec,
                  pl.BlockSpec((conv_w.shape[0], LANES), lambda i, c: (0, c)),
                  pl.BlockSpec((1, LANES), lambda i, c: (0, c)),
                  pl.BlockSpec((1, 2 * LANES, 4 * LANES), lambda i, c: (c, 0, 0)),
                  pl.BlockSpec((2, LANES), lambda i, c: (0, c))],
        out_specs=[xspec, hspec, hspec],
        out_shape=[jax.ShapeDtypeStruct((b, t, w), BF16),
                   jax.ShapeDtypeStruct((b, w), F32),
                   jax.ShapeDtypeStruct((b, w), F32)],
        scratch_shapes=scratch,
        compiler_params=_params("parallel", "parallel"),
        name="lru_branch",
    )(xb, gb, h0f, h0b, conv_w, conv_b.reshape(1, w), wg, lam)


def _lru_gate_weights(wa, ba, wx, bx):
    def tiles(w):
        w2 = w.reshape(LRU_BLOCKS // 2, 2, LRU_BLOCK, LRU_BLOCK)
        z = jnp.zeros_like(w2[:, 0])
        return jnp.concatenate([jnp.concatenate([w2[:, 0], z], axis=2),
                                jnp.concatenate([z, w2[:, 1]], axis=2)], axis=1)
    wg = jnp.concatenate([tiles(wa[0]), tiles(wx[0]), tiles(wa[1]), tiles(wx[1])], axis=2)
    n_tiles = LRU_WIDTH // LANES
    bg = jnp.stack([ba[0], bx[0], ba[1], bx[1]], axis=0).reshape(4, n_tiles, LANES)
    bg = 0.5 * jnp.transpose(bg, (1, 0, 2)).reshape(n_tiles, 1, 4 * LANES)
    hi = bg.astype(BF16)
    lo = (bg - hi.astype(F32)).astype(BF16)
    rest = jnp.zeros((n_tiles, LANES - 2, 4 * LANES), BF16)
    return jnp.concatenate([wg.astype(BF16), hi, lo, rest], axis=1)


def _packed_tiles(ref, n_heads, head, bi=0):
    n_tiles = ref.shape[1] // (n_heads * KEY_TILE)
    return [ref[bi, pl.ds(c * KEY_TILE * n_heads + head, KEY_TILE, stride=n_heads), :].astype(BF16)
            for c in range(n_tiles)]


def _diff_kernel(*refs, packed, n_heads, lam_init):
    n_kv = len(packed)
    q_ref = refs[0]
    kv_refs = refs[1:1 + 2 * n_kv]
    g_ref, lamp_ref, subg_ref, o_ref, s_ref = refs[1 + 2 * n_kv:]
    low = _lane_low_half()
    tq = q_ref.shape[1]
    first_head = pl.program_id(1) * (q_ref.shape[2] // LANES)
    lp = lamp_ref[...]
    lam = (jnp.exp(jnp.sum(lp[0:1] * lp[1:2], axis=-1, keepdims=True))
           - jnp.exp(jnp.sum(lp[2:3] * lp[3:4], axis=-1, keepdims=True)) + lam_init)
    sub_gain = subg_ref[...] * (1.0 - lam_init)
    nb = q_ref.shape[0]
    tiles = [(bi, j) for bi in range(nb) for j in range(0, q_ref.shape[2], LANES)]
    q_blocks, kv_blocks = [], []
    for bi, j in tiles:
        k_tiles, v_tiles = [], []
        for i in range(n_kv):
            if packed[i]:
                head = first_head + j // LANES
                k_tiles += _packed_tiles(kv_refs[2 * i], n_heads, head, bi)
                v_tiles += _packed_tiles(kv_refs[2 * i + 1], n_heads, head, bi)
            else:
                n_tiles = kv_refs[2 * i].shape[1] // KEY_TILE
                k_tiles += _tiles(kv_refs[2 * i], n_tiles, j, bi)
                v_tiles += _tiles(kv_refs[2 * i + 1], n_tiles, j, bi)
        v_tiles = _values_and_ones(v_tiles)
        for r in range(0, tq, ROW_BLK):
            q_blocks += list(_split_lane_halves(q_ref[bi, r:r + ROW_BLK, j:j + LANES], low))
            kv_blocks += [(k_tiles, v_tiles)] * 2
    res = _attend(q_blocks, kv_blocks, s_ref)
    i = 0
    for bi, j in tiles:
        for r in range(0, tq, ROW_BLK):
            acc1, acc2 = res[i], res[i + 1]
            i += 2
            o = acc1[:, :LANES] / acc1[:, LANES:] - acc2[:, :LANES] * (lam / acc2[:, LANES:])
            o = o * lax.rsqrt(jnp.mean(o * o, axis=-1, keepdims=True) + EPS) * sub_gain
            gate = _silu(g_ref[bi, r:r + ROW_BLK, j:j + LANES].astype(F32))
            o_ref[bi, r:r + ROW_BLK, j:j + LANES] = (o * gate).astype(o_ref.dtype)


def _diff_attn(q, kv_list, g, lam_params, sub_g, lam_init, tq, heads_per_step, batch_per_step=1):
    b, t, w = q.shape
    wb = heads_per_step * LANES
    nb = batch_per_step
    qspec = pl.BlockSpec((nb, tq, wb), lambda bi, h, qi: (bi, qi, h))
    n_heads = w // LANES
    in_specs = [qspec]
    args = [q]
    n_keys = 0
    for k, v, packed in kv_list:
        if packed:
            spec = pl.BlockSpec((nb, k.shape[1], LANES), lambda bi, h, qi: (bi, 0, 0))
            n_keys += k.shape[1] // n_heads
        else:
            spec = pl.BlockSpec((nb, k.shape[1], wb), lambda bi, h, qi: (bi, 0, h))
            n_keys += k.shape[1]
        in_specs += [spec, spec]
        args += [k, v]
    in_specs += [qspec,
                 pl.BlockSpec(lam_params.shape, lambda bi, h, qi: (0, 0)),
                 pl.BlockSpec((1, LANES), lambda bi, h, qi: (0, 0))]
    args += [g, lam_params, sub_g.reshape(1, LANES)]
    return pl.pallas_call(
        functools.partial(_diff_kernel, packed=tuple(p for _, _, p in kv_list), n_heads=n_heads,
                          lam_init=lam_init),
        grid=(b // nb, w // wb, t // tq),
        in_specs=in_specs,
        out_specs=qspec,
        out_shape=jax.ShapeDtypeStruct((b, t, w), BF16),
        scratch_shapes=[pltpu.VMEM((SCORE_SLOTS * ROW_BLK, n_keys), F32)],
        compiler_params=_params("parallel", "parallel", "parallel"),
        name="diff_attn",
    )(*args)


def _rope_tables(t_len):
    m = DIFF_HEAD_DIM // 4
    t = jnp.arange(t_len)
    rows = (t // GRID_W).astype(F32)
    cols = (t % GRID_W).astype(F32)
    inv = ROPE_BASE ** (-jnp.arange(m, dtype=F32) / m)
    ang_r = rows[:, None] * inv[None, :]
    ang_c = cols[:, None] * inv[None, :]
    cos = jnp.concatenate([jnp.cos(ang_r)] * 2 + [jnp.cos(ang_c)] * 2, axis=1)
    sin = jnp.concatenate([-jnp.sin(ang_r), jnp.sin(ang_r), -jnp.sin(ang_c), jnp.sin(ang_c)], axis=1)
    reps = LANES // DIFF_HEAD_DIM
    return jnp.tile(cos, (1, reps)), jnp.tile(sin, (1, reps))


def kernel(x_prompt, x_sample, c, cache_na_k, cache_na_v, state_lru, cache_diff_k, cache_diff_v, c_ctx,
           e_norm, e_ada_w, e_ada_b, e_w_in, e_rpb, e_conv_w, e_conv_b, e_lru_wa, e_lru_ba, e_lru_wx, e_lru_bx,
           e_lru_lam, e_w_out, o_norm, o_ada_w, o_ada_b, o_w_in, o_lq1, o_lk1, o_lq2, o_lk2, o_sub_g, o_w_out,
           final_norm):
    bp, tp, d = x_prompt.shape
    bs, ts, _ = x_sample.shape
    past = cache_na_k.shape[2]
    pad_rows = 2 * SUBLANES - bs - 1
    cc = jnp.concatenate([c, c_ctx[None, :], jnp.zeros((pad_rows, d), F32)], axis=0)
    nw, lw, dw = NA_WIDTH, LRU_WIDTH, DIFF_WIDTH
    rope_tables = _rope_tables(ts) if DEPTH > 1 else None

    def layer_params(i):
        j = i // 2
        if i % 2 == 0:
            return e_ada_w[j], e_ada_b[j], e_norm[j], e_w_in[j], e_w_out[j]
        return o_ada_w[j], o_ada_b[j], o_norm[j], o_w_in[j], o_w_out[j]

    mods = []
    for i in range(DEPTH):
        ada_w, ada_b = layer_params(i)[:2]
        m = _modulation(cc, ada_w, ada_b)
        mods.append((m[:bs].reshape(bs, 1, 3 * d), m[bs:bs + 1].reshape(1, 1, 3 * d)))

    def projection(i, latent):
        norm_g, w_in = layer_params(i)[2:4]
        mod = mods[i][0 if latent else 1]
        kv_dtype = BF16 if latent else F32
        if i % 2 == 0:
            qmul = NA_HEAD_DIM ** -0.5 * LOG2E
            outs = [(nw, BF16, False, qmul, False), (nw, kv_dtype, False, None, False),
                    (nw, kv_dtype, False, None, False), (nw, BF16, False, None, False),
                    (lw, F32, False, None, False), (lw, BF16, False, None, False)]
            return mod, norm_g, w_in, outs, None
        qmul = DIFF_HEAD_DIM ** -0.5 * LOG2E
        outs = [(dw, BF16, latent, qmul, False), (dw, kv_dtype, latent, None, not latent),
                (dw, kv_dtype, False, None, not latent), (dw, BF16, False, None, False)]
        return mod, norm_g, w_in, outs, rope_tables if latent else None

    xp, xs = x_prompt, x_sample
    _, proj_p = _stage(xp, projection(0, False))
    _, proj_s = _stage(xs, projection(0, True))
    na_k, na_v, lru_s, df_k, df_v = [], [], [], [], []
    for i in range(DEPTH):
        j = i // 2
        w_out = layer_params(i)[4]
        if i % 2 == 0:
            wg = _lru_gate_weights(e_lru_wa[j], e_lru_ba[j], e_lru_wx[j], e_lru_bx[j])

            q, k, v, ga, xb, gb = proj_p
            oa = _ctx_attn(q, k, v, ga)
            zeros = jnp.zeros((bp, lw), F32)
            ob, ff, fb = _lru_branch(xb, gb, zeros, zeros, e_conv_w[j], e_conv_b[j], wg, e_lru_lam[j])
            mixed_p = [oa, ob]
            na_k.append(k.reshape(bp, tp, NA_HEADS, NA_HEAD_DIM))
            na_v.append(v.reshape(bp, tp, NA_HEADS, NA_HEAD_DIM))
            lru_s.append(jnp.stack([ff, fb], axis=1))

            q, k, v, ga, xb, gb = proj_s
            ck = cache_na_k[:, j].reshape(bs, past, nw)
            cv = cache_na_v[:, j].reshape(bs, past, nw)
            oa = _na_attn(q, k, v, ck, cv, e_rpb[j], ga)
            ob, _, _ = _lru_branch(xb, gb, state_lru[:, j, 0], state_lru[:, j, 1],
                                   e_conv_w[j], e_conv_b[j], wg, e_lru_lam[j])
            mixed_s = [oa, ob]
        else:
            lam_init = 0.8 - 0.6 * math.exp(-0.3 * i)
            lam_params = jnp.stack([o_lq1[j], o_lk1[j], o_lq2[j], o_lk2[j]], axis=0)

            q, k, v, g = proj_p
            mixed_p = [_diff_attn(q, [(k, v, True)], g, lam_params, o_sub_g[j], lam_init, tq=tp,
                                  heads_per_step=DIFF_HEADS, batch_per_step=CTX_BATCH_PER_STEP)]
            df_k.append(k.reshape(bp, tp, DIFF_HEADS, 2 * DIFF_HEAD_DIM))
            df_v.append(v.reshape(bp, tp, DIFF_HEADS, 2 * DIFF_HEAD_DIM))

            q, k, v, g = proj_s
            ck = cache_diff_k[:, j].reshape(bs, past * DIFF_HEADS, 2 * DIFF_HEAD_DIM)
            cv = cache_diff_v[:, j].reshape(bs, past * DIFF_HEADS, 2 * DIFF_HEAD_DIM)
            mixed_s = [_diff_attn(q, [(ck, cv, True), (k, v, False)], g, lam_params, o_sub_g[j], lam_init, tq=ts,
                                  heads_per_step=4)]

        if i == DEPTH - 1:
            xp = _final_stage(xp, mixed_p, w_out, mods[i][1], final_norm)
            xs = _final_stage(xs, mixed_s, w_out, mods[i][0], final_norm)
        else:
            xp, proj_p = _stage(xp, projection(i + 1, False), prev=(mixed_p, w_out, mods[i][1]))
            xs, proj_s = _stage(xs, projection(i + 1, True), prev=(mixed_s, w_out, mods[i][0]))

    return (xp, xs, jnp.stack(na_k, axis=1), jnp.stack(na_v, axis=1), jnp.stack(lru_s, axis=1),
            jnp.stack(df_k, axis=1), jnp.stack(df_v, axis=1))
```

```python
import functools
import math

import jax
import jax.numpy as jnp
from jax import lax
from jax.experimental import pallas as pl
from jax.experimental.pallas import tpu as pltpu

DEPTH = 2
GRID_W = 64
NA_HEADS = 8
NA_HEAD_DIM = 64
NA_WIDTH = NA_HEADS * NA_HEAD_DIM
NA_ROWS_MAX = 8
NA_COLS = 16
RPB_ROWS = 2 * NA_ROWS_MAX - 1
RPB_COLS = 2 * NA_COLS - 1
LRU_WIDTH = 512
LRU_BLOCKS = 8
LRU_BLOCK = LRU_WIDTH // LRU_BLOCKS
LRU_C = 8.0
DIFF_HEADS = 8
DIFF_HEAD_DIM = 64
DIFF_WIDTH = DIFF_HEADS * 2 * DIFF_HEAD_DIM
ROPE_BASE = 10000.0
EPS = 1e-6
LOG2E = math.log2(math.e)

LANES = 128
SUBLANES = 8
MXU_DIM = 256
VMEM_LIMIT = 56 * 1024 * 1024

CAST_COLS = 512
IN_BUFS = 3
LRU_SEGMENTS = 4
LRU_UNROLL = 8
LRU_CHUNK = 256
CONV_PAD_L = 2
CONV_PAD_R = 1
ROW_BLK = MXU_DIM
KEY_TILE = MXU_DIM
SCORE_SLOTS = 4

NA_Q_ROWS = 4
NA_K_ROWS = 12
NA_PAIR_TILES = 16
NA_PAIRS_PER_STEP = 4
CTX_BATCH_PER_STEP = 4

F32 = jnp.float32
BF16 = jnp.bfloat16


def _params(*semantics):
    return pltpu.CompilerParams(dimension_semantics=semantics, vmem_limit_bytes=VMEM_LIMIT)


def _silu(x):
    h = 0.5 * x
    return h * jnp.tanh(h) + h


def _dot(a, b):
    return jnp.dot(a, b, preferred_element_type=F32)


def _dot_nt(a, b):
    return lax.dot_general(a, b, (((1,), (1,)), ((), ())), preferred_element_type=F32)


def _mod_kernel(c_ref, w_ref, b_ref, o_ref):
    s = _silu(c_ref[...])
    o_ref[...] = _dot(s.astype(BF16), w_ref[...].astype(BF16)) + b_ref[...]


def _modulation(cc, w, b):
    rows, d = cc.shape
    n = w.shape[1]
    tn = 1024
    return pl.pallas_call(
        _mod_kernel,
        grid=(n // tn,),
        in_specs=[pl.BlockSpec((rows, d), lambda j: (0, 0)),
                  pl.BlockSpec((d, tn), lambda j: (0, j)),
                  pl.BlockSpec((1, tn), lambda j: (0, j))],
        out_specs=pl.BlockSpec((rows, tn), lambda j: (0, j)),
        out_shape=jax.ShapeDtypeStruct((rows, n), F32),
        compiler_params=_params("arbitrary"),
        name="modulation",
    )(cc, w, b.reshape(1, n))


def _rope_block(x, cos, sin_signed, low16):
    partner = jnp.where(low16, pltpu.roll(x, LANES - 16, axis=1), pltpu.roll(x, 16, axis=1))
    return x * cos + partner * sin_signed


def _cast_weights_once(w_ref, wbf_ref):
    @pl.when((pl.program_id(0) == 0) & (pl.program_id(1) == 0))
    def _():
        for off in range(0, w_ref.shape[1], CAST_COLS):
            wbf_ref[:, off:off + CAST_COLS] = w_ref[:, off:off + CAST_COLS].astype(BF16)


def _weight_chunk_copy(w_hbm, stage_ref, sem, c):
    cols = slice(c * CAST_COLS, (c + 1) * CAST_COLS)
    return pltpu.make_async_copy(w_hbm.at[:, cols], stage_ref.at[:, cols], sem.at[c])


def _land_weight_chunks(w_hbm, stage_ref, sem, wbf_ref, off, width):
    for c in range(off // CAST_COLS, (off + width) // CAST_COLS):
        _weight_chunk_copy(w_hbm, stage_ref, sem, c).wait()
        cols = slice(c * CAST_COLS, (c + 1) * CAST_COLS)
        wbf_ref[:, cols] = stage_ref[:, cols].astype(BF16)


def _norm_project(x, mod_ref, g_ref, wbf_ref, rope_refs, out_refs, outs, land=None):
    tm, d = x.shape
    y = x * lax.rsqrt(jnp.mean(x * x, axis=-1, keepdims=True) + EPS) * g_ref[...]
    mod = mod_ref[0]
    h = (y * (1.0 + mod[:, d:2 * d]) + mod[:, :d]).astype(BF16)
    if rope_refs:
        cos = rope_refs[0][...]
        sin = rope_refs[1][...]
        lane = lax.broadcasted_iota(jnp.int32, (1, LANES), 1)
        low16 = (lane & 16) == 0
    off = 0
    for o_ref, (width, roped, mul, packed) in zip(out_refs, outs):
        if land is not None:
            land(off, width)
        acc = _dot(h, wbf_ref[:, off:off + width])
        if roped:
            acc = jnp.concatenate(
                [_rope_block(acc[:, j:j + LANES], cos, sin, low16) for j in range(0, width, LANES)], axis=1)
        if mul is not None:
            acc = acc * mul
        if packed:
            n_heads = width // LANES
            t = o_ref.shape[1] // n_heads
            for bi in range(o_ref.shape[0]):
                for hh in range(n_heads):
                    o_ref[bi, pl.ds(hh, t, stride=n_heads), :] = (
                        acc[bi * t:(bi + 1) * t, hh * LANES:(hh + 1) * LANES].astype(o_ref.dtype))
        else:
            o_ref[...] = acc.astype(o_ref.dtype).reshape(o_ref.shape)
        off += width


def _token_rows(ref):
    return ref[...].reshape(-1, ref.shape[-1])


def _gated_residual(o_refs, wbf_ref, x_ref, mod_ref):
    d = x_ref.shape[-1]
    acc, row = None, 0
    for o_ref in o_refs:
        k = o_ref.shape[-1]
        part = _dot(_token_rows(o_ref), wbf_ref[row:row + k, :])
        acc = part if acc is None else acc + part
        row += k
    return _token_rows(x_ref) + mod_ref[0][:, 2 * d:] * acc


def _stage_kernel(*refs, n_o, outs, rope):
    refs = list(refs)
    take = lambda n: [refs.pop(0) for _ in range(n)]
    if n_o:
        o_refs = take(n_o)
        wout_ref, x_ref, mod_prev_ref = take(3)
    else:
        x_ref, = take(1)
    mod_ref, g_ref, win_ref = take(3)
    rope_refs = take(2) if rope else []
    xout_ref = take(1)[0] if n_o else None
    out_refs = take(len(outs))
    wout_bf = take(1)[0] if n_o else None
    if n_o:
        _cast_weights_once(wout_ref, wout_bf)

    def step(land):
        if n_o:
            x = _gated_residual(o_refs, wout_bf, x_ref, mod_prev_ref)
            xout_ref[...] = x.reshape(xout_ref.shape)
        else:
            x = _token_rows(x_ref)
        _norm_project(x, mod_ref, g_ref, win_bf, rope_refs, out_refs, outs, land)

    stage_ref, win_bf, sem = take(3)
    first = (pl.program_id(0) == 0) & (pl.program_id(1) == 0)

    @pl.when(first)
    def _():
        for c in range(win_ref.shape[1] // CAST_COLS):
            _weight_chunk_copy(win_ref, stage_ref, sem, c).start()
        step(functools.partial(_land_weight_chunks, win_ref, stage_ref, sem, win_bf))

    @pl.when(jnp.logical_not(first))
    def _():
        step(None)


def _stage(x, nxt, prev=None, tm=512):
    b, t, d = x.shape
    shared_mod = all(m.shape[0] == 1 for m in ([prev[2]] if prev else []) + [nxt[0]])
    bb = max(1, tm // t) if shared_mod else 1
    tm = min(tm, t)
    tile = lambda width: pl.BlockSpec((bb, tm, width), lambda i, j: (i, j, 0))
    mod_spec = lambda mod: pl.BlockSpec(
        (1, 1, 3 * d), (lambda i, j: (i, 0, 0)) if mod.shape[0] > 1 else (lambda i, j: (0, 0, 0)))
    resident = lambda w: pl.BlockSpec(w.shape, lambda i, j: (0, 0), pipeline_mode=pl.Buffered(1))
    row_spec = pl.BlockSpec((1, d), lambda i, j: (0, 0))
    in_specs, args, out_specs, out_shape, scratch = [], [], [], [], []
    n_o = 0
    if prev is not None:
        o_list, w_out, mod_prev = prev
        n_o = len(o_list)
        in_specs += [tile(o.shape[-1]) for o in o_list] + [resident(w_out), tile(d), mod_spec(mod_prev)]
        args += list(o_list) + [w_out, x, mod_prev]
        out_specs.append(tile(d))
        out_shape.append(jax.ShapeDtypeStruct((b, t, d), F32))
        scratch.append(pltpu.VMEM(w_out.shape, BF16))
    else:
        in_specs.append(tile(d))
        args.append(x)
    mod_next, norm_g, w_in, out_list, rope_tables = nxt
    assert all(wd % CAST_COLS == 0 for wd, _, _, _, _ in out_list)
    in_specs += [mod_spec(mod_next), row_spec, pl.BlockSpec(memory_space=pl.ANY)]
    args += [mod_next, norm_g.reshape(1, d), w_in]
    rope = rope_tables is not None
    if rope:
        in_specs += [pl.BlockSpec((tm, LANES), lambda i, j: (j, 0))] * 2
        args += list(rope_tables)
    for width, dt, _, _, packed in out_list:
        if packed:
            heads = width // LANES
            out_specs.append(pl.BlockSpec((bb, tm * heads, LANES), lambda i, j: (i, j, 0)))
            out_shape.append(jax.ShapeDtypeStruct((b, t * heads, LANES), dt))
        else:
            out_specs.append(tile(width))
            out_shape.append(jax.ShapeDtypeStruct((b, t, width), dt))
    outs = tuple((w_, r, m, p) for w_, _, r, m, p in out_list)
    scratch += [pltpu.VMEM(w_in.shape, F32), pltpu.VMEM(w_in.shape, BF16),
                pltpu.SemaphoreType.DMA((w_in.shape[1] // CAST_COLS,))]
    res = pl.pallas_call(
        functools.partial(_stage_kernel, n_o=n_o, outs=outs, rope=rope),
        grid=(b // bb, t // tm),
        in_specs=in_specs,
        out_specs=out_specs,
        out_shape=out_shape,
        scratch_shapes=scratch,
        compiler_params=_params("arbitrary", "arbitrary"),
        name="stage",
    )(*args)
    return (res[0], list(res[1:])) if n_o else (None, list(res))


def _ring_copy(hbm_ref, buf_ref, sem, step, bb, tm, nt):
    slot = lax.rem(step, IN_BUFS)
    rows = pl.ds(lax.div(step, nt) * bb, bb)
    toks = pl.ds(lax.rem(step, nt) * tm, tm)
    return pltpu.make_async_copy(hbm_ref.at[rows, toks, :], buf_ref.at[slot], sem.at[slot])


def _final_kernel(*refs, n_o, bb, tm, nt, n_steps):
    hbm_refs = refs[:n_o + 1]
    wout_ref, mod_ref, fin_ref, out_ref, wout_bf = refs[n_o + 1:n_o + 6]
    bufs = refs[n_o + 6:2 * n_o + 7]
    sems = refs[2 * n_o + 7:]
    step = pl.program_id(0) * nt + pl.program_id(1)

    def start(s):
        for h, b, sm in zip(hbm_refs, bufs, sems):
            _ring_copy(h, b, sm, s, bb, tm, nt).start()

    @pl.when(step == 0)
    def _():
        for s in range(min(IN_BUFS - 1, n_steps)):
            start(jnp.int32(s))

    @pl.when(step + (IN_BUFS - 1) < n_steps)
    def _():
        start(step + (IN_BUFS - 1))

    for h, b, sm in zip(hbm_refs, bufs, sems):
        _ring_copy(h, b, sm, step, bb, tm, nt).wait()
    _cast_weights_once(wout_ref, wout_bf)
    slot = lax.rem(step, IN_BUFS)
    d = out_ref.shape[-1]
    acc, row = None, 0
    for b in bufs[:n_o]:
        k = b.shape[-1]
        part = _dot(b[slot].reshape(-1, k), wout_bf[row:row + k, :])
        acc = part if acc is None else acc + part
        row += k
    x = bufs[n_o][slot].reshape(-1, d) + mod_ref[0][:, 2 * d:] * acc
    y = x * lax.rsqrt(jnp.mean(x * x, axis=-1, keepdims=True) + EPS) * fin_ref[...]
    out_ref[...] = y.reshape(out_ref.shape)


def _final_stage(x, o_list, w_out, mod, final_g, tm=512):
    b, t, d = x.shape
    bb = max(1, tm // t) if mod.shape[0] == 1 else 1
    tm = min(tm, t)
    nt = t // tm
    n_steps = (b // bb) * nt
    mod_map = (lambda i, j: (i, 0, 0)) if mod.shape[0] > 1 else (lambda i, j: (0, 0, 0))
    streamed = list(o_list) + [x]
    in_specs = [pl.BlockSpec(memory_space=pl.ANY)] * len(streamed)
    in_specs += [pl.BlockSpec(w_out.shape, lambda i, j: (0, 0), pipeline_mode=pl.Buffered(1)),
                 pl.BlockSpec((1, 1, 3 * d), mod_map),
                 pl.BlockSpec((1, d), lambda i, j: (0, 0))]
    scratch = [pltpu.VMEM(w_out.shape, BF16)]
    scratch += [pltpu.VMEM((IN_BUFS, bb, tm, a.shape[-1]), a.dtype) for a in streamed]
    scratch += [pltpu.SemaphoreType.DMA((IN_BUFS,)) for _ in streamed]
    return pl.pallas_call(
        functools.partial(_final_kernel, n_o=len(o_list), bb=bb, tm=tm, nt=nt, n_steps=n_steps),
        grid=(b // bb, nt),
        in_specs=in_specs,
        out_specs=pl.BlockSpec((bb, tm, d), lambda i, j: (i, j, 0)),
        out_shape=jax.ShapeDtypeStruct((b, t, d), F32),
        scratch_shapes=scratch,
        compiler_params=_params("arbitrary", "arbitrary"),
        name="final_stage",
    )(*streamed, w_out, mod, final_g.reshape(1, d))


def _attend(q_blocks, kv_blocks, s_ref, score_fn=None):
    n_rb = len(q_blocks)

    def tile_of(rb, c):
        slot = rb % SCORE_SLOTS
        return (slice(slot * ROW_BLK, (slot + 1) * ROW_BLK), slice(c * KEY_TILE, (c + 1) * KEY_TILE))

    def score_tile(rb, c):
        s_ref[tile_of(rb, c)] = _dot_nt(q_blocks[rb], kv_blocks[rb][0][c])

    def max_tile(rb, c, m_acc):
        s = s_ref[tile_of(rb, c)]
        adjusted = None if score_fn is None else score_fn(rb, c, s)
        if adjusted is not None:
            s = adjusted
            s_ref[tile_of(rb, c)] = s
        mt = jnp.maximum(s[:, :LANES], s[:, LANES:])
        return mt if m_acc is None else jnp.maximum(m_acc, mt)

    def n_tiles(rb):
        return len(kv_blocks[rb][0]) if 0 <= rb < n_rb else 0

    outs = []
    row_max = {}
    for step in range(n_rb + 2):
        rb_score, rb_max, rb_exp = step, step - 1, step - 2
        m_acc, acc = None, None
        for c in range(max(n_tiles(rb_score), n_tiles(rb_max), n_tiles(rb_exp))):
            if c < n_tiles(rb_score):
                score_tile(rb_score, c)
            if c < n_tiles(rb_max):
                m_acc = max_tile(rb_max, c, m_acc)
            if c < n_tiles(rb_exp):
                e = jnp.exp2(s_ref[tile_of(rb_exp, c)] - row_max[rb_exp])
                pv = _dot(e.astype(BF16), kv_blocks[rb_exp][1][c])
                acc = pv if acc is None else acc + pv
        if m_acc is not None:
            row_max[rb_max] = m_acc.max(axis=-1, keepdims=True)
        if acc is not None:
            outs.append(acc)
    return outs


def _lane_low_half():
    return lax.broadcasted_iota(jnp.int32, (1, LANES), 1) < (LANES // 2)


def _values_and_ones(v_tiles):
    ones = jnp.ones((KEY_TILE, LANES), BF16)
    return [jnp.concatenate([v, ones], axis=1) for v in v_tiles]


def _normalise_pair(acc_lo, acc_hi, low):
    return jnp.where(low, acc_lo[:, :LANES] / acc_lo[:, LANES:], acc_hi[:, :LANES] / acc_hi[:, LANES:])


def _split_lane_halves(q, low):
    zero = jnp.zeros_like(q)
    return jnp.where(low, q, zero), jnp.where(low, zero, q)


def _tiles(ref, n_tiles, j, bi=0):
    return [ref[bi, c * KEY_TILE:(c + 1) * KEY_TILE, j:j + LANES].astype(BF16) for c in range(n_tiles)]


def _ctx_attn_kernel(q_ref, k_ref, v_ref, g_ref, o_ref, s_ref):
    low = _lane_low_half()
    nb, tq, w = q_ref.shape
    n_tiles = k_ref.shape[1] // KEY_TILE
    tiles = [(bi, j, r) for bi in range(nb) for j in range(0, w, LANES) for r in range(0, tq, ROW_BLK)]
    q_blocks, kv_blocks, kv = [], [], {}
    for bi, j, r in tiles:
        if (bi, j) not in kv:
            kv[bi, j] = (_tiles(k_ref, n_tiles, j, bi), _values_and_ones(_tiles(v_ref, n_tiles, j, bi)))
        q_blocks += list(_split_lane_halves(q_ref[bi, r:r + ROW_BLK, j:j + LANES], low))
        kv_blocks += [kv[bi, j]] * 2
    res = _attend(q_blocks, kv_blocks, s_ref)
    for i, (bi, j, r) in enumerate(tiles):
        o = _normalise_pair(res[2 * i], res[2 * i + 1], low)
        gate = _silu(g_ref[bi, r:r + ROW_BLK, j:j + LANES].astype(F32))
        o_ref[bi, r:r + ROW_BLK, j:j + LANES] = (o * gate).astype(o_ref.dtype)


def _ctx_attn(q, k, v, g):
    b, t, w = q.shape
    nb = CTX_BATCH_PER_STEP
    spec = pl.BlockSpec((nb, t, w), lambda i: (i, 0, 0))
    return pl.pallas_call(
        _ctx_attn_kernel,
        grid=(b // nb,),
        in_specs=[spec] * 4,
        out_specs=spec,
        out_shape=jax.ShapeDtypeStruct((b, t, w), BF16),
        scratch_shapes=[pltpu.VMEM((SCORE_SLOTS * ROW_BLK, k.shape[1]), F32)],
        compiler_params=_params("parallel"),
        name="ctx_attn",
    )(q, k, v, g)


def _na_pair_tiles(rpb_ref, head, pt_ref):
    right = lax.broadcasted_iota(jnp.int32, (GRID_W, LANES), 1) >= GRID_W
    zero_row = jnp.zeros((1, LANES), F32)

    def toeplitz(row, lane0):
        shift = (lane0 - (NA_COLS - 1)) % LANES
        return pltpu.roll(jnp.broadcast_to(row, (GRID_W, LANES)), shift, axis=1, stride=1, stride_axis=0)

    for p in range(NA_PAIR_TILES):
        left_row = rpb_ref[head, p - 1:p, :] if p >= 1 else zero_row
        right_row = rpb_ref[head, p:p + 1, :]
        pt_ref[head, p] = jnp.where(right, toeplitz(right_row, GRID_W), toeplitz(left_row, 0)) * LOG2E


def _na_row_start(r, n_rows):
    kr = min(NA_ROWS_MAX, n_rows)
    return min(max(r - kr // 2, 0), n_rows - kr)


def _na_window(grp, n_rows):
    kr = min(NA_ROWS_MAX, n_rows)
    starts = [_na_row_start(grp * NA_Q_ROWS + i, n_rows) for i in range(NA_Q_ROWS)]
    rows_per_tile = KEY_TILE // GRID_W
    need = -(-(max(starts) + kr - min(starts)) // rows_per_tile) * rows_per_tile
    return min(min(starts), n_rows - need), need


def _na_kernel(q_ref, k_ref, v_ref, ck_ref, cv_ref, rpb_ref, g_ref, o_ref, pt_ref, badd_ref, s_ref, *, n_rows):
    kr = min(NA_ROWS_MAX, n_rows)
    n_grp = n_rows // NA_Q_ROWS
    n_pairs = q_ref.shape[2] // LANES
    low = _lane_low_half()

    @pl.when(pl.program_id(1) == 0)
    def _build_bias():
        for head in range(2 * n_pairs):
            _na_pair_tiles(rpb_ref, head, pt_ref)
        qcol = lax.broadcasted_iota(jnp.int32, (GRID_W, LANES), 0)
        lane = lax.broadcasted_iota(jnp.int32, (GRID_W, LANES), 1)
        right = lane >= GRID_W
        kcol = lane & (GRID_W - 1)
        col_start = jnp.clip(qcol - NA_COLS // 2, 0, GRID_W - NA_COLS)
        col_ok = (kcol >= col_start) & (kcol < col_start + NA_COLS)
        for grp in range(n_grp):
            krow0, key_rows = _na_window(grp, n_rows)
            for i in range(NA_Q_ROWS):
                r = grp * NA_Q_ROWS + i
                row_start = _na_row_start(r, n_rows)
                for m in range(key_rows // 2):
                    key_row = krow0 + 2 * m
                    ok_l = row_start <= key_row < row_start + kr
                    ok_r = row_start <= key_row + 1 < row_start + kr
                    rows = slice(i * GRID_W, (i + 1) * GRID_W)
                    cols = slice(m * LANES, (m + 1) * LANES)
                    if not (ok_l or ok_r):
                        for head in range(2 * n_pairs):
                            badd_ref[head * n_grp + grp, rows, cols] = jnp.full((GRID_W, LANES), -jnp.inf, F32)
                        continue
                    ok = col_ok
                    if not ok_r:
                        ok = ok & jnp.logical_not(right)
                    if not ok_l:
                        ok = ok & right
                    p = key_row - r + NA_ROWS_MAX
                    for head in range(2 * n_pairs):
                        badd_ref[head * n_grp + grp, rows, cols] = jnp.where(ok, pt_ref[head, p], -jnp.inf)

    n_ctx = ck_ref.shape[1] // KEY_TILE
    q_blocks, kv_blocks, blocks = [], [], []
    for pair in range(n_pairs):
        lanes = slice(pair * LANES, (pair + 1) * LANES)
        ctx_k = _tiles(ck_ref, n_ctx, pair * LANES)
        ctx_v = _tiles(cv_ref, n_ctx, pair * LANES)
        for grp in range(n_grp):
            krow0, key_rows = _na_window(grp, n_rows)
            k0 = krow0 * GRID_W
            win = [slice(k0 + c * KEY_TILE, k0 + (c + 1) * KEY_TILE) for c in range(key_rows * GRID_W // KEY_TILE)]
            kv = ([k_ref[0, w, lanes] for w in win] + ctx_k,
                  _values_and_ones([v_ref[0, w, lanes] for w in win] + ctx_v))
            q_blocks += list(_split_lane_halves(q_ref[0, grp * ROW_BLK:(grp + 1) * ROW_BLK, lanes], low))
            kv_blocks += [kv, kv]
            blocks += [(2 * pair, grp, len(win)), (2 * pair + 1, grp, len(win))]

    def masked_bias(rb, c, s):
        head, grp, n_win = blocks[rb]
        if c >= n_win:
            return None
        cols = slice(c * KEY_TILE, (c + 1) * KEY_TILE)
        return s + badd_ref[head * n_grp + grp, :, cols]

    res = _attend(q_blocks, kv_blocks, s_ref, masked_bias)
    for pair in range(n_pairs):
        lanes = slice(pair * LANES, (pair + 1) * LANES)
        for grp in range(n_grp):
            rb = 2 * (pair * n_grp + grp)
            o = _normalise_pair(res[rb], res[rb + 1], low)
            rows = slice(grp * ROW_BLK, (grp + 1) * ROW_BLK)
            o_ref[0, rows, lanes] = (o * _silu(g_ref[0, rows, lanes].astype(F32))).astype(o_ref.dtype)


def _na_attn(q, k, v, ck, cv, rpb, g):
    _, rpb_rows, rpb_cols = rpb.shape
    assert rpb_rows == RPB_ROWS and rpb_cols == RPB_COLS
    rpb = jnp.pad(rpb, ((0, 0), (0, NA_PAIR_TILES - rpb_rows), (0, LANES - rpb_cols)))
    b, t, w = q.shape
    c = ck.shape[1]
    n_rows = t // GRID_W
    assert NA_Q_ROWS * GRID_W == ROW_BLK and n_rows % NA_Q_ROWS == 0 and n_rows >= NA_K_ROWS
    n_grp = n_rows // NA_Q_ROWS
    n_win = NA_K_ROWS * GRID_W
    wb = NA_PAIRS_PER_STEP * LANES
    hb = 2 * NA_PAIRS_PER_STEP
    tspec = pl.BlockSpec((1, t, wb), lambda hp, bi: (bi, 0, hp))
    cspec = pl.BlockSpec((1, c, wb), lambda hp, bi: (bi, 0, hp))
    ptspec = pl.BlockSpec((hb, NA_PAIR_TILES, LANES), lambda hp, bi: (hp, 0, 0))
    return pl.pallas_call(
        functools.partial(_na_kernel, n_rows=n_rows),
        grid=(w // wb, b),
        in_specs=[tspec, tspec, tspec, cspec, cspec, ptspec, tspec],
        out_specs=tspec,
        out_shape=jax.ShapeDtypeStruct((b, t, w), BF16),
        scratch_shapes=[pltpu.VMEM((hb, NA_PAIR_TILES, GRID_W, LANES), F32),
                        pltpu.VMEM((hb * n_grp, ROW_BLK, n_win), F32),
                        pltpu.VMEM((SCORE_SLOTS * ROW_BLK, n_win + c), F32)],
        compiler_params=_params("parallel", "arbitrary"),
        name="na_attn",
    )(q, k, v, ck, cv, rpb, g)


def _softplus(x):
    return jnp.maximum(x, 0.0) + jnp.log(1.0 + jnp.exp(-jnp.abs(x)))


def _lru_kernel(x_ref, gb_ref, h0f_ref, h0b_ref, cw_ref, cb_ref, wg_ref, lam_ref,
                o_ref, ff_ref, fb_ref, xt_s, af_s, uf_s, ab_s, ub_s, hf_s, hb_s):
    nb, t, _ = x_ref.shape
    pad = CONV_PAD_L * nb
    cw = 0.5 * cw_ref[...]
    cb = 0.5 * cb_ref[...]
    wg = wg_ref[0]
    kk = (-0.5 * LRU_C * LOG2E) * _softplus(-lam_ref[...])

    xt_s[0:pad, :] = jnp.zeros((pad, LANES), F32)
    xt_s[pad + t * nb:pad + (t + CONV_PAD_R) * nb, :] = jnp.zeros((CONV_PAD_R * nb, LANES), F32)

    def to_time_major(b, carry):
        xt_s[pl.ds(pad + b, t, stride=nb), :] = x_ref[b]
        return carry

    lax.fori_loop(0, nb, to_time_major, 0)

    rows_per_chunk = LRU_CHUNK * nb
    lane = lax.broadcasted_iota(jnp.int32, (rows_per_chunk, LANES), 1)
    bias_cols = jnp.where(lane < 2, 1.0, 0.0).astype(BF16)

    def gates(c, carry):
        r0 = pl.multiple_of(c * rows_per_chunk, rows_per_chunk)

        def tap(i):
            return xt_s[pl.ds(r0 + i * nb, rows_per_chunk), :]

        xh = cw[0:1] * tap(0) + cw[1:2] * tap(1) + cw[2:3] * tap(2) + cw[3:4] * tap(3) + cb
        th = jnp.tanh(_dot(jnp.concatenate([xh.astype(BF16), bias_cols], axis=1), wg))
        rows = pl.ds(r0, rows_per_chunk)
        for d, (a_s, u_s) in enumerate(((af_s, uf_s), (ab_s, ub_s))):
            t_a = th[:, (2 * d) * LANES:(2 * d + 1) * LANES]
            t_x = th[:, (2 * d + 1) * LANES:(2 * d + 2) * LANES]
            a = jnp.exp2(kk[d:d + 1] * t_a + kk[d:d + 1])
            y = 1.0 - a * a
            sq = y * lax.rsqrt(jnp.maximum(y, 1e-30))
            a_s[rows, :] = a
            u_s[rows, :] = sq * (xh * t_x + xh)
        return carry

    lax.fori_loop(0, t // LRU_CHUNK, gates, 0)

    ts = t // LRU_SEGMENTS
    refs = ((af_s, uf_s), (ab_s, ub_s))

    slab = nb * LRU_UNROLL

    def slab_rows(d, j, i):
        base = (j * ts + i * LRU_UNROLL) * nb
        return pl.ds(pl.multiple_of(base if d == 0 else t * nb - slab - base, slab), slab)

    def step_of(d, block, k):
        kk_ = k if d == 0 else LRU_UNROLL - 1 - k
        return block[kk_ * nb:(kk_ + 1) * nb]

    def compose(i, carry):
        state = list(carry)
        for d in range(2):
            for j in range(LRU_SEGMENTS - 1):
                n = d * (LRU_SEGMENTS - 1) + j
                a_blk = refs[d][0][slab_rows(d, j, i), :]
                u_blk = refs[d][1][slab_rows(d, j, i), :]
                g, big_a = state[n]
                for k in range(LRU_UNROLL):
                    a = step_of(d, a_blk, k)
                    g, big_a = a * g + step_of(d, u_blk, k), a * big_a
                state[n] = (g, big_a)
        return tuple(state)

    ones = jnp.ones((nb, LANES), F32)
    h0 = (h0f_ref[...], h0b_ref[...])
    init = tuple((h0[d] if j == 0 else jnp.zeros((nb, LANES), F32), ones)
                 for d in range(2) for j in range(LRU_SEGMENTS - 1))
    ends = lax.fori_loop(0, ts // LRU_UNROLL, compose, init)
    starts = []
    for d in range(2):
        h = h0[d]
        for j in range(LRU_SEGMENTS):
            starts.append(h)
            if j < LRU_SEGMENTS - 1:
                g, big_a = ends[d * (LRU_SEGMENTS - 1) + j]
                h = g if j == 0 else g + big_a * h

    def rerun(i, carry):
        state = list(carry)
        for d, h_s in enumerate((hf_s, hb_s)):
            for j in range(LRU_SEGMENTS):
                n = d * LRU_SEGMENTS + j
                a_blk = refs[d][0][slab_rows(d, j, i), :]
                u_blk = refs[d][1][slab_rows(d, j, i), :]
                hs = []
                for k in range(LRU_UNROLL):
                    state[n] = step_of(d, a_blk, k) * state[n] + step_of(d, u_blk, k)
                    hs.append(state[n])
                h_s[slab_rows(d, j, i), :] = jnp.concatenate(hs if d == 0 else hs[::-1], axis=0)
        return tuple(state)

    final = lax.fori_loop(0, ts // LRU_UNROLL, rerun, tuple(starts))
    ff_ref[...] = final[LRU_SEGMENTS - 1]
    fb_ref[...] = final[2 * LRU_SEGMENTS - 1]

    def add_directions(c, carry):
        rows = pl.ds(pl.multiple_of(c * rows_per_chunk, rows_per_chunk), rows_per_chunk)
        hf_s[rows, :] = hf_s[rows, :] + hb_s[rows, :]
        return carry

    lax.fori_loop(0, t // LRU_CHUNK, add_directions, 0)

    def emit(b, carry):
        hs = hf_s[pl.ds(b, t, stride=nb), :]
        o_ref[b] = (hs * _silu(gb_ref[b].astype(F32))).astype(o_ref.dtype)
        return carry

    lax.fori_loop(0, nb, emit, 0)


def _lru_branch(xb, gb, h0f, h0b, conv_w, conv_b, wg, lam):
    b, t, w = xb.shape
    nb = SUBLANES
    assert t % (LRU_SEGMENTS * LRU_UNROLL) == 0 and t % LRU_CHUNK == 0
    assert conv_w.shape[0] == CONV_PAD_L + 1 + CONV_PAD_R
    xspec = pl.BlockSpec((nb, t, LANES), lambda i, c: (i, 0, c))
    hspec = pl.BlockSpec((nb, LANES), lambda i, c: (i, c))
    scratch = [pltpu.VMEM(((t + CONV_PAD_L + CONV_PAD_R) * nb, LANES), F32)]
    scratch += [pltpu.VMEM((t * nb, LANES), F32) for _ in range(6)]
    return pl.pallas_call(
        _lru_kernel,
        grid=(b // nb, w // LANES),
        in_specs=[xspec, xspec, hspec, hspec,
                  pl.BlockSpec((conv_w.shape[0], LANES), lambda i, c: (0, c)),
                  pl.BlockSpec((1, LANES), lambda i, c: (0, c)),
                  pl.BlockSpec((1, 2 * LANES, 4 * LANES), lambda i, c: (c, 0, 0)),
                  pl.BlockSpec((2, LANES), lambda i, c: (0, c))],
        out_specs=[xspec, hspec, hspec],
        out_shape=[jax.ShapeDtypeStruct((b, t, w), BF16),
                   jax.ShapeDtypeStruct((b, w), F32),
                   jax.ShapeDtypeStruct((b, w), F32)],
        scratch_shapes=scratch,
        compiler_params=_params("parallel", "parallel"),
        name="lru_branch",
    )(xb, gb, h0f, h0b, conv_w, conv_b.reshape(1, w), wg, lam)


def _lru_gate_weights(wa, ba, wx, bx):
    def tiles(w):
        w2 = w.reshape(LRU_BLOCKS // 2, 2, LRU_BLOCK, LRU_BLOCK)
        z = jnp.zeros_like(w2[:, 0])
        return jnp.concatenate([jnp.concatenate([w2[:, 0], z], axis=2),
                                jnp.concatenate([z, w2[:, 1]], axis=2)], axis=1)
    wg = jnp.concatenate([tiles(wa[0]), tiles(wx[0]), tiles(wa[1]), tiles(wx[1])], axis=2)
    n_tiles = LRU_WIDTH // LANES
    bg = jnp.stack([ba[0], bx[0], ba[1], bx[1]], axis=0).reshape(4, n_tiles, LANES)
    bg = 0.5 * jnp.transpose(bg, (1, 0, 2)).reshape(n_tiles, 1, 4 * LANES)
    hi = bg.astype(BF16)
    lo = (bg - hi.astype(F32)).astype(BF16)
    rest = jnp.zeros((n_tiles, LANES - 2, 4 * LANES), BF16)
    return jnp.concatenate([wg.astype(BF16), hi, lo, rest], axis=1)


def _packed_tiles(ref, n_heads, head, bi=0):
    n_tiles = ref.shape[1] // (n_heads * KEY_TILE)
    return [ref[bi, pl.ds(c * KEY_TILE * n_heads + head, KEY_TILE, stride=n_heads), :].astype(BF16)
            for c in range(n_tiles)]


def _diff_kernel(*refs, packed, n_heads, lam_init):
    n_kv = len(packed)
    q_ref = refs[0]
    kv_refs = refs[1:1 + 2 * n_kv]
    g_ref, lamp_ref, subg_ref, o_ref, s_ref = refs[1 + 2 * n_kv:]
    low = _lane_low_half()
    tq = q_ref.shape[1]
    first_head = pl.program_id(1) * (q_ref.shape[2] // LANES)
    lp = lamp_ref[...]
    lam = (jnp.exp(jnp.sum(lp[0:1] * lp[1:2], axis=-1, keepdims=True))
           - jnp.exp(jnp.sum(lp[2:3] * lp[3:4], axis=-1, keepdims=True)) + lam_init)
    sub_gain = subg_ref[...] * (1.0 - lam_init)
    nb = q_ref.shape[0]
    tiles = [(bi, j) for bi in range(nb) for j in range(0, q_ref.shape[2], LANES)]
    q_blocks, kv_blocks = [], []
    for bi, j in tiles:
        k_tiles, v_tiles = [], []
        for i in range(n_kv):
            if packed[i]:
                head = first_head + j // LANES
                k_tiles += _packed_tiles(kv_refs[2 * i], n_heads, head, bi)
                v_tiles += _packed_tiles(kv_refs[2 * i + 1], n_heads, head, bi)
            else:
                n_tiles = kv_refs[2 * i].shape[1] // KEY_TILE
                k_tiles += _tiles(kv_refs[2 * i], n_tiles, j, bi)
                v_tiles += _tiles(kv_refs[2 * i + 1], n_tiles, j, bi)
        v_tiles = _values_and_ones(v_tiles)
        for r in range(0, tq, ROW_BLK):
            q_blocks += list(_split_lane_halves(q_ref[bi, r:r + ROW_BLK, j:j + LANES], low))
            kv_blocks += [(k_tiles, v_tiles)] * 2
    res = _attend(q_blocks, kv_blocks, s_ref)
    i = 0
    for bi, j in tiles:
        for r in range(0, tq, ROW_BLK):
            acc1, acc2 = res[i], res[i + 1]
            i += 2
            o = acc1[:, :LANES] / acc1[:, LANES:] - acc2[:, :LANES] * (lam / acc2[:, LANES:])
            o = o * lax.rsqrt(jnp.mean(o * o, axis=-1, keepdims=True) + EPS) * sub_gain
            gate = _silu(g_ref[bi, r:r + ROW_BLK, j:j + LANES].astype(F32))
            o_ref[bi, r:r + ROW_BLK, j:j + LANES] = (o * gate).astype(o_ref.dtype)


def _diff_attn(q, kv_list, g, lam_params, sub_g, lam_init, tq, heads_per_step, batch_per_step=1):
    b, t, w = q.shape
    wb = heads_per_step * LANES
    nb = batch_per_step
    qspec = pl.BlockSpec((nb, tq, wb), lambda bi, h, qi: (bi, qi, h))
    n_heads = w // LANES
    in_specs = [qspec]
    args = [q]
    n_keys = 0
    for k, v, packed in kv_list:
        if packed:
            spec = pl.BlockSpec((nb, k.shape[1], LANES), lambda bi, h, qi: (bi, 0, 0))
            n_keys += k.shape[1] // n_heads
        else:
            spec = pl.BlockSpec((nb, k.shape[1], wb), lambda bi, h, qi: (bi, 0, h))
            n_keys += k.shape[1]
        in_specs += [spec, spec]
        args += [k, v]
    in_specs += [qspec,
                 pl.BlockSpec(lam_params.shape, lambda bi, h, qi: (0, 0)),
                 pl.BlockSpec((1, LANES), lambda bi, h, qi: (0, 0))]
    args += [g, lam_params, sub_g.reshape(1, LANES)]
    return pl.pallas_call(
        functools.partial(_diff_kernel, packed=tuple(p for _, _, p in kv_list), n_heads=n_heads,
                          lam_init=lam_init),
        grid=(b // nb, w // wb, t // tq),
        in_specs=in_specs,
        out_specs=qspec,
        out_shape=jax.ShapeDtypeStruct((b, t, w), BF16),
        scratch_shapes=[pltpu.VMEM((SCORE_SLOTS * ROW_BLK, n_keys), F32)],
        compiler_params=_params("parallel", "parallel", "parallel"),
        name="diff_attn",
    )(*args)


def _rope_tables(t_len):
    m = DIFF_HEAD_DIM // 4
    t = jnp.arange(t_len)
    rows = (t // GRID_W).astype(F32)
    cols = (t % GRID_W).astype(F32)
    inv = ROPE_BASE ** (-jnp.arange(m, dtype=F32) / m)
    ang_r = rows[:, None] * inv[None, :]
    ang_c = cols[:, None] * inv[None, :]
    cos = jnp.concatenate([jnp.cos(ang_r)] * 2 + [jnp.cos(ang_c)] * 2, axis=1)
    sin = jnp.concatenate([-jnp.sin(ang_r), jnp.sin(ang_r), -jnp.sin(ang_c), jnp.sin(ang_c)], axis=1)
    reps = LANES // DIFF_HEAD_DIM
    return jnp.tile(cos, (1, reps)), jnp.tile(sin, (1, reps))


def kernel(x_prompt, x_sample, c, cache_na_k, cache_na_v, state_lru, cache_diff_k, cache_diff_v, c_ctx,
           e_norm, e_ada_w, e_ada_b, e_w_in, e_rpb, e_conv_w, e_conv_b, e_lru_wa, e_lru_ba, e_lru_wx, e_lru_bx,
           e_lru_lam, e_w_out, o_norm, o_ada_w, o_ada_b, o_w_in, o_lq1, o_lk1, o_lq2, o_lk2, o_sub_g, o_w_out,
           final_norm):
    bp, tp, d = x_prompt.shape
    bs, ts, _ = x_sample.shape
    past = cache_na_k.shape[2]
    pad_rows = 2 * SUBLANES - bs - 1
    cc = jnp.concatenate([c, c_ctx[None, :], jnp.zeros((pad_rows, d), F32)], axis=0)
    nw, lw, dw = NA_WIDTH, LRU_WIDTH, DIFF_WIDTH
    rope_tables = _rope_tables(ts) if DEPTH > 1 else None

    def layer_params(i):
        j = i // 2
        if i % 2 == 0:
            return e_ada_w[j], e_ada_b[j], e_norm[j], e_w_in[j], e_w_out[j]
        return o_ada_w[j], o_ada_b[j], o_norm[j], o_w_in[j], o_w_out[j]

    mods = []
    for i in range(DEPTH):
        ada_w, ada_b = layer_params(i)[:2]
        m = _modulation(cc, ada_w, ada_b)
        mods.append((m[:bs].reshape(bs, 1, 3 * d), m[bs:bs + 1].reshape(1, 1, 3 * d)))

    def projection(i, latent):
        norm_g, w_in = layer_params(i)[2:4]
        mod = mods[i][0 if latent else 1]
        kv_dtype = BF16 if latent else F32
        if i % 2 == 0:
            qmul = NA_HEAD_DIM ** -0.5 * LOG2E
            outs = [(nw, BF16, False, qmul, False), (nw, kv_dtype, False, None, False),
                    (nw, kv_dtype, False, None, False), (nw, BF16, False, None, False),
                    (lw, F32, False, None, False), (lw, BF16, False, None, False)]
            return mod, norm_g, w_in, outs, None
        qmul = DIFF_HEAD_DIM ** -0.5 * LOG2E
        outs = [(dw, BF16, latent, qmul, False), (dw, kv_dtype, latent, None, not latent),
                (dw, kv_dtype, False, None, not latent), (dw, BF16, False, None, False)]
        return mod, norm_g, w_in, outs, rope_tables if latent else None

    xp, xs = x_prompt, x_sample
    _, proj_p = _stage(xp, projection(0, False))
    _, proj_s = _stage(xs, projection(0, True))
    na_k, na_v, lru_s, df_k, df_v = [], [], [], [], []
    for i in range(DEPTH):
        j = i // 2
        w_out = layer_params(i)[4]
        if i % 2 == 0:
            wg = _lru_gate_weights(e_lru_wa[j], e_lru_ba[j], e_lru_wx[j], e_lru_bx[j])

            q, k, v, ga, xb, gb = proj_p
            oa = _ctx_attn(q, k, v, ga)
            zeros = jnp.zeros((bp, lw), F32)
            ob, ff, fb = _lru_branch(xb, gb, zeros, zeros, e_conv_w[j], e_conv_b[j], wg, e_lru_lam[j])
            mixed_p = [oa, ob]
            na_k.append(k.reshape(bp, tp, NA_HEADS, NA_HEAD_DIM))
            na_v.append(v.reshape(bp, tp, NA_HEADS, NA_HEAD_DIM))
            lru_s.append(jnp.stack([ff, fb], axis=1))

            q, k, v, ga, xb, gb = proj_s
            ck = cache_na_k[:, j].reshape(bs, past, nw)
            cv = cache_na_v[:, j].reshape(bs, past, nw)
            oa = _na_attn(q, k, v, ck, cv, e_rpb[j], ga)
            ob, _, _ = _lru_branch(xb, gb, state_lru[:, j, 0], state_lru[:, j, 1],
                                   e_conv_w[j], e_conv_b[j], wg, e_lru_lam[j])
            mixed_s = [oa, ob]
        else:
            lam_init = 0.8 - 0.6 * math.exp(-0.3 * i)
            lam_params = jnp.stack([o_lq1[j], o_lk1[j], o_lq2[j], o_lk2[j]], axis=0)

            q, k, v, g = proj_p
            mixed_p = [_diff_attn(q, [(k, v, True)], g, lam_params, o_sub_g[j], lam_init, tq=tp,
                                  heads_per_step=DIFF_HEADS, batch_per_step=CTX_BATCH_PER_STEP)]
            df_k.append(k.reshape(bp, tp, DIFF_HEADS, 2 * DIFF_HEAD_DIM))
            df_v.append(v.reshape(bp, tp, DIFF_HEADS, 2 * DIFF_HEAD_DIM))

            q, k, v, g = proj_s
            ck = cache_diff_k[:, j].reshape(bs, past * DIFF_HEADS, 2 * DIFF_HEAD_DIM)
            cv = cache_diff_v[:, j].reshape(bs, past * DIFF_HEADS, 2 * DIFF_HEAD_DIM)
            mixed_s = [_diff_attn(q, [(ck, cv, True), (k, v, False)], g, lam_params, o_sub_g[j], lam_init, tq=ts,
                                  heads_per_step=4)]

        if i == DEPTH - 1:
            xp = _final_stage(xp, mixed_p, w_out, mods[i][1], final_norm)
            xs = _final_stage(xs, mixed_s, w_out, mods[i][0], final_norm)
        else:
            xp, proj_p = _stage(xp, projection(i + 1, False), prev=(mixed_p, w_out, mods[i][1]))
            xs, proj_s = _stage(xs, projection(i + 1, True), prev=(mixed_s, w_out, mods[i][0]))

    return (xp, xs, jnp.stack(na_k, axis=1), jnp.stack(na_v, axis=1), jnp.stack(lru_s, axis=1),
            jnp.stack(df_k, axis=1), jnp.stack(df_v, axis=1))
```
